```python
import jax, jax.numpy as jnp
from jax import lax
import numpy as np

D_MODEL = 2048
BATCH = 8
SEQ = 4096
DEPTH = 2

D_POOL = D_MODEL // 2
POOL_WINDOWS = (2, 4, 8, 16)
N_POOL_GROUPS = len(POOL_WINDOWS)
POOL_GROUP = D_POOL // N_POOL_GROUPS
D_SGU = D_MODEL // 2
SGU_CHUNK = 128
N_SGU_GROUPS = 8
SGU_GROUP = D_SGU // N_SGU_GROUPS
D_CONV = D_MODEL // 2
CONV_WIDTH = 3
N_BRANCH = 3
D_FF = 5632
EPS = 1e-6

OFF_POOL = 0
OFF_SGU = OFF_POOL + D_POOL
OFF_CONV = OFF_SGU + 2 * D_SGU
OFF_GATE = OFF_CONV + 3 * D_CONV
D_IN = OFF_GATE + N_BRANCH * D_MODEL

kernel_name = "hybrid_pool_sgu_shortconv_encoder"


def rmsnorm(x, g):
    xf = x.astype(jnp.float32)
    y = xf * lax.rsqrt(jnp.mean(xf * xf, axis=-1, keepdims=True) + EPS)
    return (y * g.astype(jnp.float32)).astype(x.dtype)


def dwconv3_centred(x, w):
    xp = jnp.pad(x, ((0, 0), (1, 1), (0, 0)))
    return xp[:, :-2] * w[0] + xp[:, 1:-1] * w[1] + xp[:, 2:] * w[2]


def multiscale_pool_mixer(a, w_pool, pool_scale):
    bsz, L, _ = a.shape
    af = a.astype(jnp.float32)
    csum = jnp.concatenate([jnp.zeros((bsz, 1, D_POOL), jnp.float32), jnp.cumsum(af, axis=1)], axis=1)
    t = jnp.arange(L, dtype=jnp.int32)
    outs = []
    for gi, w in enumerate(POOL_WINDOWS):
        lo = jnp.clip(t - w // 2, 0, L - 1)
        hi = jnp.clip(t + (w - w // 2 - 1), 0, L - 1)
        cg = csum[..., gi * POOL_GROUP:(gi + 1) * POOL_GROUP]
        s = jnp.take(cg, hi + 1, axis=1) - jnp.take(cg, lo, axis=1)
        cnt = (hi - lo + 1).astype(jnp.float32)[None, :, None]
        outs.append(s / cnt)
    pooled = jnp.concatenate(outs, axis=-1) - af
    pooled = pooled.astype(a.dtype).reshape(bsz, L, N_POOL_GROUPS, POOL_GROUP)
    y = jnp.einsum('blgc,gcd->blgd', pooled, w_pool).reshape(bsz, L, D_POOL)
    return y * pool_scale


def spatial_gating_mixer(uv, sgu_norm_g, sgu_w, sgu_b):
    bsz, L, _ = uv.shape
    uv = jax.nn.gelu(uv)
    u, v = uv[..., :D_SGU], uv[..., D_SGU:]
    v = rmsnorm(v, sgu_norm_g)
    v = v.reshape(bsz, L // SGU_CHUNK, SGU_CHUNK, N_SGU_GROUPS, SGU_GROUP)
    z = jnp.einsum('gpq,bnqgc->bnpgc', sgu_w, v) + sgu_b[None, None, :, :, None]
    return u * z.reshape(bsz, L, D_SGU)


def short_conv_mixer(xbc, conv_w):
    xc = xbc[..., :D_CONV]
    bg = xbc[..., D_CONV:2 * D_CONV]
    cg = xbc[..., 2 * D_CONV:]
    return bg * dwconv3_centred(cg * xc, conv_w)


def conv_glu_ffn(h, w_up, ffn_conv_w, w_down):
    up = dwconv3_centred(h @ w_up, ffn_conv_w)
    gate, val = up[..., :D_FF], up[..., D_FF:]
    return (jax.nn.silu(gate) * val) @ w_down


def _fwd_setup_inputs(seed: int = 0) -> dict:
    key = jax.random.key(seed)
    ks = jax.random.split(key, 20)
    f32 = jnp.float32
    nrm = lambda k, shape, s: jax.random.normal(k, shape, f32) * s
    return {
        "x": jax.random.normal(ks[0], (BATCH, SEQ, D_MODEL), f32),
        "norm1_g": 1.0 + nrm(ks[1], (DEPTH, D_MODEL), 0.02),
        "w_in": nrm(ks[2], (DEPTH, D_MODEL, D_IN), D_MODEL ** -0.5),
        "w_pool": nrm(ks[3], (DEPTH, N_POOL_GROUPS, POOL_GROUP, POOL_GROUP), POOL_GROUP ** -0.5),
        "pool_scale": 1.0 + nrm(ks[4], (DEPTH, D_POOL), 0.1),
        "sgu_norm_g": 1.0 + nrm(ks[5], (DEPTH, D_SGU), 0.02),
        "sgu_w": nrm(ks[6], (DEPTH, N_SGU_GROUPS, SGU_CHUNK, SGU_CHUNK), SGU_CHUNK ** -0.5),
        "sgu_b": 1.0 + nrm(ks[7], (DEPTH, SGU_CHUNK, N_SGU_GROUPS), 0.02),
        "conv_w": nrm(ks[8], (DEPTH, CONV_WIDTH, D_CONV), CONV_WIDTH ** -0.5),
        "w_pool_out": nrm(ks[9], (DEPTH, D_POOL, D_MODEL), D_POOL ** -0.5),
        "w_sgu_out": nrm(ks[10], (DEPTH, D_SGU, D_MODEL), D_SGU ** -0.5),
        "w_conv_out": nrm(ks[11], (DEPTH, D_CONV, D_MODEL), D_CONV ** -0.5),
        "w_o": nrm(ks[12], (DEPTH, D_MODEL, D_MODEL), D_MODEL ** -0.5),
        "norm2_g": 1.0 + nrm(ks[13], (DEPTH, D_MODEL), 0.02),
        "w_up": nrm(ks[14], (DEPTH, D_MODEL, 2 * D_FF), D_MODEL ** -0.5),
        "ffn_conv_w": nrm(ks[15], (DEPTH, CONV_WIDTH, 2 * D_FF), CONV_WIDTH ** -0.5),
        "w_down": nrm(ks[16], (DEPTH, D_FF, D_MODEL), D_FF ** -0.5),
        "final_g": 1.0 + nrm(ks[17], (D_MODEL,), 0.02),
    }


def _fwd_reference(x, norm1_g, w_in, w_pool, pool_scale, sgu_norm_g, sgu_w, sgu_b, conv_w,
              w_pool_out, w_sgu_out, w_conv_out, w_o, norm2_g, w_up, ffn_conv_w, w_down, final_g):
    bsz, L, _ = x.shape
    for l in range(DEPTH):
        h = rmsnorm(x, norm1_g[l])
        proj = h @ w_in[l]
        y_a = multiscale_pool_mixer(proj[..., OFF_POOL:OFF_SGU], w_pool[l], pool_scale[l])
        y_b = spatial_gating_mixer(proj[..., OFF_SGU:OFF_CONV], sgu_norm_g[l], sgu_w[l], sgu_b[l])
        y_c = short_conv_mixer(proj[..., OFF_CONV:OFF_GATE], conv_w[l])
        gates = jax.nn.sigmoid(proj[..., OFF_GATE:].astype(jnp.float32)).astype(x.dtype)
        gates = gates.reshape(bsz, L, N_BRANCH, D_MODEL)
        merged = (gates[..., 0, :] * (y_a @ w_pool_out[l])
                  + gates[..., 1, :] * (y_b @ w_sgu_out[l])
                  + gates[..., 2, :] * (y_c @ w_conv_out[l]))
        x = x + merged @ w_o[l]
        x = x + conv_glu_ffn(rmsnorm(x, norm2_g[l]), w_up[l], ffn_conv_w[l], w_down[l])
    return rmsnorm(x, final_g)


import jax as _jax
import jax.numpy as _jnp

TWIN_FORMAT = 'train_step'
FWD_PARAMS = ['x', 'norm1_g', 'w_in', 'w_pool', 'pool_scale', 'sgu_norm_g', 'sgu_w', 'sgu_b', 'conv_w', 'w_pool_out', 'w_sgu_out', 'w_conv_out', 'w_o', 'norm2_g', 'w_up', 'ffn_conv_w', 'w_down', 'final_g']
TWIN_WEIGHTS = ['norm1_g', 'w_in', 'w_pool', 'pool_scale', 'sgu_norm_g', 'sgu_w', 'sgu_b', 'conv_w', 'w_pool_out', 'w_sgu_out', 'w_conv_out', 'w_o', 'norm2_g', 'w_up', 'ffn_conv_w', 'w_down', 'final_g']
TWIN_DIFF_INPUT = 'x'
TWIN_INPUTS = ['x', 'norm1_g', 'w_in', 'w_pool', 'pool_scale', 'sgu_norm_g', 'sgu_w', 'sgu_b', 'conv_w', 'w_pool_out', 'w_sgu_out', 'w_conv_out', 'w_o', 'norm2_g', 'w_up', 'ffn_conv_w', 'w_down', 'final_g', 'loss_target', 'm_norm1_g', 'm_w_in', 'm_w_pool', 'm_pool_scale', 'm_sgu_norm_g', 'm_sgu_w', 'm_sgu_b', 'm_conv_w', 'm_w_pool_out', 'm_w_sgu_out', 'm_w_conv_out', 'm_w_o', 'm_norm2_g', 'm_w_up', 'm_ffn_conv_w', 'm_w_down', 'm_final_g', 'v_norm1_g', 'v_w_in', 'v_w_pool', 'v_pool_scale', 'v_sgu_norm_g', 'v_sgu_w', 'v_sgu_b', 'v_conv_w', 'v_w_pool_out', 'v_w_sgu_out', 'v_w_conv_out', 'v_w_o', 'v_norm2_g', 'v_w_up', 'v_ffn_conv_w', 'v_w_down', 'v_final_g']
TWIN_OUTPUTS = ['loss', 'grad_x', 'grad_norm1_g', 'grad_w_in', 'grad_w_pool', 'grad_pool_scale', 'grad_sgu_norm_g', 'grad_sgu_w', 'grad_sgu_b', 'grad_conv_w', 'grad_w_pool_out', 'grad_w_sgu_out', 'grad_w_conv_out', 'grad_w_o', 'grad_norm2_g', 'grad_w_up', 'grad_ffn_conv_w', 'grad_w_down', 'grad_final_g', 'delta_norm1_g', 'delta_w_in', 'delta_w_pool', 'delta_pool_scale', 'delta_sgu_norm_g', 'delta_sgu_w', 'delta_sgu_b', 'delta_conv_w', 'delta_w_pool_out', 'delta_w_sgu_out', 'delta_w_conv_out', 'delta_w_o', 'delta_norm2_g', 'delta_w_up', 'delta_ffn_conv_w', 'delta_w_down', 'delta_final_g', 'new_m_norm1_g', 'new_m_w_in', 'new_m_w_pool', 'new_m_pool_scale', 'new_m_sgu_norm_g', 'new_m_sgu_w', 'new_m_sgu_b', 'new_m_conv_w', 'new_m_w_pool_out', 'new_m_w_sgu_out', 'new_m_w_conv_out', 'new_m_w_o', 'new_m_norm2_g', 'new_m_w_up', 'new_m_ffn_conv_w', 'new_m_w_down', 'new_m_final_g', 'new_v_norm1_g', 'new_v_w_in', 'new_v_w_pool', 'new_v_pool_scale', 'new_v_sgu_norm_g', 'new_v_sgu_w', 'new_v_sgu_b', 'new_v_conv_w', 'new_v_w_pool_out', 'new_v_w_sgu_out', 'new_v_w_conv_out', 'new_v_w_o', 'new_v_norm2_g', 'new_v_w_up', 'new_v_ffn_conv_w', 'new_v_w_down', 'new_v_final_g']
TWIN_LEAF_KINDS = {'loss': 'loss', 'grad_x': 'grad_x', 'grad_norm1_g': 'grad_w', 'grad_w_in': 'grad_w', 'grad_w_pool': 'grad_w', 'grad_pool_scale': 'grad_w', 'grad_sgu_norm_g': 'grad_w', 'grad_sgu_w': 'grad_w', 'grad_sgu_b': 'grad_w', 'grad_conv_w': 'grad_w', 'grad_w_pool_out': 'grad_w', 'grad_w_sgu_out': 'grad_w', 'grad_w_conv_out': 'grad_w', 'grad_w_o': 'grad_w', 'grad_norm2_g': 'grad_w', 'grad_w_up': 'grad_w', 'grad_ffn_conv_w': 'grad_w', 'grad_w_down': 'grad_w', 'grad_final_g': 'grad_w', 'delta_norm1_g': 'delta_w', 'delta_w_in': 'delta_w', 'delta_w_pool': 'delta_w', 'delta_pool_scale': 'delta_w', 'delta_sgu_norm_g': 'delta_w', 'delta_sgu_w': 'delta_w', 'delta_sgu_b': 'delta_w', 'delta_conv_w': 'delta_w', 'delta_w_pool_out': 'delta_w', 'delta_w_sgu_out': 'delta_w', 'delta_w_conv_out': 'delta_w', 'delta_w_o': 'delta_w', 'delta_norm2_g': 'delta_w', 'delta_w_up': 'delta_w', 'delta_ffn_conv_w': 'delta_w', 'delta_w_down': 'delta_w', 'delta_final_g': 'delta_w', 'new_m_norm1_g': 'new_m', 'new_m_w_in': 'new_m', 'new_m_w_pool': 'new_m', 'new_m_pool_scale': 'new_m', 'new_m_sgu_norm_g': 'new_m', 'new_m_sgu_w': 'new_m', 'new_m_sgu_b': 'new_m', 'new_m_conv_w': 'new_m', 'new_m_w_pool_out': 'new_m', 'new_m_w_sgu_out': 'new_m', 'new_m_w_conv_out': 'new_m', 'new_m_w_o': 'new_m', 'new_m_norm2_g': 'new_m', 'new_m_w_up': 'new_m', 'new_m_ffn_conv_w': 'new_m', 'new_m_w_down': 'new_m', 'new_m_final_g': 'new_m', 'new_v_norm1_g': 'new_v', 'new_v_w_in': 'new_v', 'new_v_w_pool': 'new_v', 'new_v_pool_scale': 'new_v', 'new_v_sgu_norm_g': 'new_v', 'new_v_sgu_w': 'new_v', 'new_v_sgu_b': 'new_v', 'new_v_conv_w': 'new_v', 'new_v_w_pool_out': 'new_v', 'new_v_w_sgu_out': 'new_v', 'new_v_w_conv_out': 'new_v', 'new_v_w_o': 'new_v', 'new_v_norm2_g': 'new_v', 'new_v_w_up': 'new_v', 'new_v_ffn_conv_w': 'new_v', 'new_v_w_down': 'new_v', 'new_v_final_g': 'new_v'}


def _forward(args):
    return _fwd_reference(*[args[k] for k in FWD_PARAMS])


def _output_shape():
    def fwd():
        inp = _fwd_setup_inputs(0)
        return _fwd_reference(*[inp[k] for k in FWD_PARAMS])
    out = _jax.eval_shape(fwd)
    return out.shape, out.dtype

N_MICROBATCH = 1
ADAM_LR = 0.001
ADAM_B1 = 0.9
ADAM_B2 = 0.999
ADAM_EPS = 1e-08
ADAM_WD = 0.01
ADAM_STEP = 10
PER_EXAMPLE_BATCH_AXIS = {'x': 0, 'loss_target': 0}
SHARED_INPUTS = []
_WEIGHT_DTYPES = {'norm1_g': _jnp.float32, 'w_in': _jnp.float32, 'w_pool': _jnp.float32, 'pool_scale': _jnp.float32, 'sgu_norm_g': _jnp.float32, 'sgu_w': _jnp.float32, 'sgu_b': _jnp.float32, 'conv_w': _jnp.float32, 'w_pool_out': _jnp.float32, 'w_sgu_out': _jnp.float32, 'w_conv_out': _jnp.float32, 'w_o': _jnp.float32, 'norm2_g': _jnp.float32, 'w_up': _jnp.float32, 'ffn_conv_w': _jnp.float32, 'w_down': _jnp.float32, 'final_g': _jnp.float32}
MOMENT_SCALE = {'norm1_g': 9.150736e-02, 'w_in': 3.771507e-02, 'w_pool': 4.946236e-02, 'pool_scale': 5.309610e-02, 'sgu_norm_g': 3.587391e-02, 'sgu_w': 3.682579e-02, 'sgu_b': 3.908978e-02, 'conv_w': 5.833479e-02, 'w_pool_out': 3.486544e-02, 'w_sgu_out': 3.690127e-02, 'w_conv_out': 3.910575e-02, 'w_o': 6.393278e-02, 'norm2_g': 5.466682e-02, 'w_up': 2.330594e-02, 'ffn_conv_w': 2.324863e-02, 'w_down': 3.801130e-02, 'final_g': 1.599872e+01}


def _to_microbatches(a, axis):
    t = _jnp.moveaxis(a, axis, 0)
    t = t.reshape((N_MICROBATCH, t.shape[0] // N_MICROBATCH) + t.shape[1:])
    return _jnp.moveaxis(t, 1, axis + 1)


def setup_inputs(seed: int = 0) -> dict:
    inp = _fwd_setup_inputs(seed)
    key = _jax.random.fold_in(_jax.random.key(seed), 7919)
    shape, _ = _output_shape()
    out = dict(inp)
    out["loss_target"] = _jax.random.normal(_jax.random.fold_in(key, 0), shape, _jnp.float32)
    for i, name in enumerate(TWIN_WEIGHTS):
        w = inp[name].astype(_jnp.float32)
        if MOMENT_SCALE is None:
            s = _jnp.sqrt(_jnp.mean(_jnp.square(w)) + 1e-30)
        else:
            s = MOMENT_SCALE[name]
        km, kv = _jax.random.split(_jax.random.fold_in(key, i + 1))
        out[name] = w
        out["m_" + name] = s * _jax.random.normal(km, w.shape, _jnp.float32)
        out["v_" + name] = (s * s) * _jax.random.uniform(kv, w.shape, _jnp.float32, 0.5, 1.5)
    if N_MICROBATCH > 1:
        for name, axis in PER_EXAMPLE_BATCH_AXIS.items():
            out[name] = _to_microbatches(out[name], axis)
    return {'x': out['x'], 'norm1_g': out['norm1_g'], 'w_in': out['w_in'], 'w_pool': out['w_pool'], 'pool_scale': out['pool_scale'], 'sgu_norm_g': out['sgu_norm_g'], 'sgu_w': out['sgu_w'], 'sgu_b': out['sgu_b'], 'conv_w': out['conv_w'], 'w_pool_out': out['w_pool_out'], 'w_sgu_out': out['w_sgu_out'], 'w_conv_out': out['w_conv_out'], 'w_o': out['w_o'], 'norm2_g': out['norm2_g'], 'w_up': out['w_up'], 'ffn_conv_w': out['ffn_conv_w'], 'w_down': out['w_down'], 'final_g': out['final_g'], 'loss_target': out['loss_target'], 'm_norm1_g': out['m_norm1_g'], 'm_w_in': out['m_w_in'], 'm_w_pool': out['m_w_pool'], 'm_pool_scale': out['m_pool_scale'], 'm_sgu_norm_g': out['m_sgu_norm_g'], 'm_sgu_w': out['m_sgu_w'], 'm_sgu_b': out['m_sgu_b'], 'm_conv_w': out['m_conv_w'], 'm_w_pool_out': out['m_w_pool_out'], 'm_w_sgu_out': out['m_w_sgu_out'], 'm_w_conv_out': out['m_w_conv_out'], 'm_w_o': out['m_w_o'], 'm_norm2_g': out['m_norm2_g'], 'm_w_up': out['m_w_up'], 'm_ffn_conv_w': out['m_ffn_conv_w'], 'm_w_down': out['m_w_down'], 'm_final_g': out['m_final_g'], 'v_norm1_g': out['v_norm1_g'], 'v_w_in': out['v_w_in'], 'v_w_pool': out['v_w_pool'], 'v_pool_scale': out['v_pool_scale'], 'v_sgu_norm_g': out['v_sgu_norm_g'], 'v_sgu_w': out['v_sgu_w'], 'v_sgu_b': out['v_sgu_b'], 'v_conv_w': out['v_conv_w'], 'v_w_pool_out': out['v_w_pool_out'], 'v_w_sgu_out': out['v_w_sgu_out'], 'v_w_conv_out': out['v_w_conv_out'], 'v_w_o': out['v_w_o'], 'v_norm2_g': out['v_norm2_g'], 'v_w_up': out['v_w_up'], 'v_ffn_conv_w': out['v_ffn_conv_w'], 'v_w_down': out['v_w_down'], 'v_final_g': out['v_final_g']}


def _loss(weights, diff, rest, loss_target):
    with _jax.named_scope("forward"):
        args = {**rest, TWIN_DIFF_INPUT: diff, **{k: w.astype(_WEIGHT_DTYPES[k]) for k, w in weights.items()}}
        y = _forward(args)
    with _jax.named_scope("loss_head"):
        err = _jnp.square(y.astype(_jnp.float32) - loss_target)
        return 0.5 * _jnp.sum(_jnp.mean(err, axis=-1)) if err.ndim else 0.5 * err


def _adamw(w, g, m, v):
    m = ADAM_B1 * m + (1.0 - ADAM_B1) * g
    v = ADAM_B2 * v + (1.0 - ADAM_B2) * _jnp.square(g)
    m_hat = m / (1.0 - ADAM_B1 ** ADAM_STEP)
    v_hat = v / (1.0 - ADAM_B2 ** ADAM_STEP)
    delta = -ADAM_LR * (m_hat / (_jnp.sqrt(v_hat) + ADAM_EPS) + ADAM_WD * w)
    return delta, m, v


def reference(x, norm1_g, w_in, w_pool, pool_scale, sgu_norm_g, sgu_w, sgu_b, conv_w, w_pool_out, w_sgu_out, w_conv_out, w_o, norm2_g, w_up, ffn_conv_w, w_down, final_g, loss_target, m_norm1_g, m_w_in, m_w_pool, m_pool_scale, m_sgu_norm_g, m_sgu_w, m_sgu_b, m_conv_w, m_w_pool_out, m_w_sgu_out, m_w_conv_out, m_w_o, m_norm2_g, m_w_up, m_ffn_conv_w, m_w_down, m_final_g, v_norm1_g, v_w_in, v_w_pool, v_pool_scale, v_sgu_norm_g, v_sgu_w, v_sgu_b, v_conv_w, v_w_pool_out, v_w_sgu_out, v_w_conv_out, v_w_o, v_norm2_g, v_w_up, v_ffn_conv_w, v_w_down, v_final_g):
    given = dict(x=x, norm1_g=norm1_g, w_in=w_in, w_pool=w_pool, pool_scale=pool_scale, sgu_norm_g=sgu_norm_g, sgu_w=sgu_w, sgu_b=sgu_b, conv_w=conv_w, w_pool_out=w_pool_out, w_sgu_out=w_sgu_out, w_conv_out=w_conv_out, w_o=w_o, norm2_g=norm2_g, w_up=w_up, ffn_conv_w=ffn_conv_w, w_down=w_down, final_g=final_g, loss_target=loss_target, m_norm1_g=m_norm1_g, m_w_in=m_w_in, m_w_pool=m_w_pool, m_pool_scale=m_pool_scale, m_sgu_norm_g=m_sgu_norm_g, m_sgu_w=m_sgu_w, m_sgu_b=m_sgu_b, m_conv_w=m_conv_w, m_w_pool_out=m_w_pool_out, m_w_sgu_out=m_w_sgu_out, m_w_conv_out=m_w_conv_out, m_w_o=m_w_o, m_norm2_g=m_norm2_g, m_w_up=m_w_up, m_ffn_conv_w=m_ffn_conv_w, m_w_down=m_w_down, m_final_g=m_final_g, v_norm1_g=v_norm1_g, v_w_in=v_w_in, v_w_pool=v_w_pool, v_pool_scale=v_pool_scale, v_sgu_norm_g=v_sgu_norm_g, v_sgu_w=v_sgu_w, v_sgu_b=v_sgu_b, v_conv_w=v_conv_w, v_w_pool_out=v_w_pool_out, v_w_sgu_out=v_w_sgu_out, v_w_conv_out=v_w_conv_out, v_w_o=v_w_o, v_norm2_g=v_norm2_g, v_w_up=v_w_up, v_ffn_conv_w=v_ffn_conv_w, v_w_down=v_w_down, v_final_g=v_final_g)
    weights = {n: given[n] for n in TWIN_WEIGHTS}
    shared = {n: given[n] for n in SHARED_INPUTS}
    per_example = {n: given[n] for n in ['x']}
    grad_fn = _jax.value_and_grad(_loss, argnums=(0, 1))

    def one_microbatch(ex, loss_target):
        ex = dict(ex)
        diff = ex.pop(TWIN_DIFF_INPUT)
        return grad_fn(weights, diff, {**shared, **ex}, loss_target)

    if N_MICROBATCH == 1:
        loss, (grad_w, grad_x) = one_microbatch(per_example, given["loss_target"])
    else:
        def body(carry, xs):
            loss_sum, grad_sum = carry
            l_k, (gw_k, gx_k) = one_microbatch(xs[0], xs[1])
            with _jax.named_scope("update"):
                return (loss_sum + l_k, _jax.tree.map(_jnp.add, grad_sum, gw_k)), gx_k

        init = (_jnp.zeros((), _jnp.float32), _jax.tree.map(_jnp.zeros_like, weights))
        (loss, grad_w), grad_x = _jax.lax.scan(body, init, (per_example, given["loss_target"]))
    with _jax.named_scope("update"):
        delta_w, new_m, new_v = {}, {}, {}
        for n in TWIN_WEIGHTS:
            delta_w[n], new_m[n], new_v[n] = _adamw(weights[n], grad_w[n], given["m_" + n], given["v_" + n])
    return (loss, grad_x, *[grad_w[n] for n in TWIN_WEIGHTS], *[delta_w[n] for n in TWIN_WEIGHTS],
            *[new_m[n] for n in TWIN_WEIGHTS], *[new_v[n] for n in TWIN_WEIGHTS])
```

```python
import functools

import jax
import jax.numpy as jnp
from jax import lax
from jax.experimental import pallas as pl
from jax.experimental.pallas import tpu as pltpu

BF16 = jnp.bfloat16
F32 = jnp.float32
EPS = 1e-6
MIB = 1024 * 1024
V7X_VMEM_BYTES = 64 * MIB
VMEM_LIMIT = 56 * MIB
LANE = 128
N_DEV = 8


def _params(sem, **kw):
    return pltpu.CompilerParams(dimension_semantics=sem, vmem_limit_bytes=VMEM_LIMIT, **kw)


def _pick(n, cands):
    for c in cands:
        if n % c == 0:
            return c
    return n


_DIMS = {"nn": (((1,), (0,)), ((), ())), "nt": (((1,), (1,)), ((), ())), "tn": (((0,), (0,)), ((), ()))}


def _matmul(a, b, *, mode, out_dtype, name, res=None, b_slices=None, out_slices=None, prev=None,
            tm=None, tn=None, tk=None):
    soff, scnt = b_slices if b_slices is not None else (0, 1)
    ooff, ocnt = out_slices if out_slices is not None else (0, 1)
    if mode == "nn":
        M = a.shape[0]
        Kc = a.shape[1]
        ns = b.shape[-1]
        N = scnt * ns
    elif mode == "nt":
        M = a.shape[0]
        N = b.shape[-2]
        ns = b.shape[-1]
        Kc = scnt * ns
    else:
        Kc, M = a.shape
        N = b.shape[1]
        ns = N // ocnt
    tm = tm or _pick(M, (1024, 512, 256, 128))
    if tn is None:
        lim = ns if (mode == "nn" and b_slices is not None) or out_slices is not None else N
        tn = _pick(lim, (1024, 768, 512, 256, 128)) if lim > 1408 else lim
    if tk is None:
        if mode == "nt" and b_slices is not None:
            tk = ns
        elif mode == "tn":
            tk = Kc if Kc <= 4096 else _pick(Kc, (4096, 2048))
        else:
            tk = Kc if Kc <= 2816 else _pick(Kc, (2816, 2048, 1536, 1408, 1024, 512))
    nk = Kc // tk
    assert M % tm == 0 and N % tn == 0 and Kc % tk == 0, (name, M, N, Kc, tm, tn, tk)
    if mode == "nn":
        a_spec = pl.BlockSpec((tm, tk), lambda i, j, k: (i, k))
        if b_slices is None:
            b_spec = pl.BlockSpec((tk, tn), lambda i, j, k: (k, j))
        else:
            spn = ns // tn
            assert ns % tn == 0
            b_spec = pl.BlockSpec((None, tk, tn), lambda i, j, k: (soff + j // spn, k, j % spn))
    elif mode == "nt":
        a_spec = pl.BlockSpec((tm, tk), lambda i, j, k: (i, k))
        if b_slices is None:
            b_spec = pl.BlockSpec((tn, tk), lambda i, j, k: (j, k))
        else:
            spk = ns // tk
            assert ns % tk == 0
            b_spec = pl.BlockSpec((None, tn, tk), lambda i, j, k: (soff + k // spk, j, k % spk))
    else:
        a_spec = pl.BlockSpec((tk, tm), lambda i, j, k: (k, i))
        b_spec = pl.BlockSpec((tk, tn), lambda i, j, k: (k, j))
    if out_slices is None:
        o_spec = pl.BlockSpec((tm, tn), lambda i, j, k: (i, j))
        out_shape = jax.ShapeDtypeStruct((M, N), out_dtype)
    else:
        spo = ns // tn
        assert mode == "tn" and ns % tn == 0
        o_spec = pl.BlockSpec((None, tm, tn), lambda i, j, k: (ooff + j // spo, i, j % spo))
        out_shape = jax.ShapeDtypeStruct((N_DEV, M, ns), out_dtype)
    dims = _DIMS[mode]
    has_res = res is not None
    has_prev = prev is not None

    def body(*refs):
        a_ref, b_ref = refs[0], refs[1]
        res_ref = refs[2] if has_res else None
        o_ref = refs[2 + has_res + has_prev]
        part = lax.dot_general(a_ref[...], b_ref[...], dims, preferred_element_type=F32)

        def finish(acc):
            if has_res:
                acc = acc + res_ref[...]
            o_ref[...] = acc.astype(o_ref.dtype)

        if nk == 1:
            finish(part)
        else:
            acc_ref = refs[3 + has_res + has_prev]
            k = pl.program_id(2)

            @pl.when(k == 0)
            def _():
                acc_ref[...] = part

            @pl.when(jnp.logical_and(k > 0, k < nk - 1))
            def _():
                acc_ref[...] += part

            @pl.when(k == nk - 1)
            def _():
                finish(acc_ref[...] + part)

    in_specs = [a_spec, b_spec]
    args = [a, b]
    if has_res:
        in_specs.append(pl.BlockSpec((tm, tn), lambda i, j, k: (i, j)))
        args.append(res)
    aliases = {}
    if has_prev:
        aliases = {len(args): 0}
        in_specs.append(pl.BlockSpec(memory_space=pl.ANY))
        args.append(prev)
    return pl.pallas_call(
        body,
        name=name,
        grid=(M // tm, N // tn, nk),
        in_specs=in_specs,
        out_specs=o_spec,
        out_shape=out_shape,
        scratch_shapes=[pltpu.VMEM((tm, tn), F32)] if nk > 1 else [],
        input_output_aliases=aliases,
        compiler_params=_params(("parallel", "parallel", "arbitrary")),
    )(*args)


HALO = 16
CH = 256


def _fill_pad(pad_ref, chunk_fn, T):
    z = jnp.zeros((HALO, pad_ref.shape[1]), F32)
    pad_ref[pl.ds(0, HALO), :] = z
    pad_ref[pl.ds(HALO + T, HALO), :] = z

    def body(c, carry):
        r0 = pl.multiple_of(c * CH, CH)
        pad_ref[pl.ds(HALO + r0, CH), :] = chunk_fn(r0)
        return carry

    lax.fori_loop(0, T // CH, body, 0)


def _ext(pad_ref, r0):
    return pad_ref[pl.ds(r0, CH + 2 * HALO), :]


def _tap(ext, o):
    if o == 0:
        return ext[HALO:HALO + CH]
    return pltpu.roll(ext, (-o) % ext.shape[0], axis=0)[HALO:HALO + CH]


def _chunks(T, fn, init=0):
    def body(c, carry):
        return fn(pl.multiple_of(c * CH, CH), carry)

    return lax.fori_loop(0, T // CH, body, init)


def _conv3(ext, w):
    return _tap(ext, -1) * w[0:1] + _tap(ext, 0) * w[1:2] + _tap(ext, 1) * w[2:3]


def _conv3_t(ext, w):
    return _tap(ext, 1) * w[0:1] + _tap(ext, 0) * w[1:2] + _tap(ext, -1) * w[2:3]


def _conv3_dw(xext, d):
    return [jnp.sum(_tap(xext, k - 1) * d, axis=0, keepdims=True) for k in range(3)]


def _sigmoid(x):
    return 1.0 / (1.0 + jnp.exp(-x))


_GELU_C = 0.7978845608028654


def _gelu(x):
    return 0.5 * x * (1.0 + jnp.tanh(_GELU_C * (x + 0.044715 * (x * x * x))))


def _gelu_grad(x):
    t = jnp.tanh(_GELU_C * (x + 0.044715 * (x * x * x)))
    return 0.5 * (1.0 + t) + 0.5 * x * (1.0 - t * t) * (_GELU_C * (1.0 + 3.0 * 0.044715 * (x * x)))


def _row_tile(T):
    return _pick(T, (256, 128))


def _rms_fwd(x, g, name):
    T, D = x.shape
    tm = _row_tile(T)

    def body(x_ref, g_ref, h_ref):
        xv = x_ref[...]
        r = lax.rsqrt(jnp.mean(xv * xv, axis=-1, keepdims=True) + EPS)
        h_ref[...] = (xv * r * g_ref[...]).astype(BF16)

    return pl.pallas_call(
        body, name=name, grid=(T // tm,),
        in_specs=[pl.BlockSpec((tm, D), lambda i: (i, 0)), pl.BlockSpec((1, D), lambda i: (0, 0))],
        out_specs=pl.BlockSpec((tm, D), lambda i: (i, 0)),
        out_shape=jax.ShapeDtypeStruct((T, D), BF16),
        compiler_params=_params(("parallel",)),
    )(x, g)


def _rms_bwd(dh, x, g, dres, name):
    T, D = x.shape
    tm = _row_tile(T)

    def body(dh_ref, x_ref, g_ref, dres_ref, dx_ref, dxb_ref, dg_ref):
        i = pl.program_id(0)
        xv = x_ref[...]
        r = lax.rsqrt(jnp.mean(xv * xv, axis=-1, keepdims=True) + EPS)
        n = xv * r
        dh_v = dh_ref[...].astype(F32)
        dn = dh_v * g_ref[...]
        dx = dres_ref[...] + r * (dn - n * jnp.mean(dn * n, axis=-1, keepdims=True))
        dx_ref[...] = dx
        dxb_ref[...] = dx.astype(BF16)
        dg = jnp.sum(dh_v * n, axis=0, keepdims=True)

        @pl.when(i == 0)
        def _():
            dg_ref[...] = dg

        @pl.when(i > 0)
        def _():
            dg_ref[...] += dg

    row = pl.BlockSpec((tm, D), lambda i: (i, 0))
    vec = pl.BlockSpec((1, D), lambda i: (0, 0))
    return pl.pallas_call(
        body, name=name, grid=(T // tm,),
        in_specs=[row, row, vec, row],
        out_specs=[row, row, vec],
        out_shape=[jax.ShapeDtypeStruct((T, D), F32), jax.ShapeDtypeStruct((T, D), BF16),
                   jax.ShapeDtypeStruct((1, D), F32)],
        compiler_params=_params(("arbitrary",)),
    )(dh, x, g, dres)


def _loss_head(x, tgt, g, name):
    T, D = x.shape
    tm = _row_tile(T)

    def body(x_ref, t_ref, g_ref, dx_ref, dxb_ref, dg_ref, l_ref):
        i = pl.program_id(0)
        xv = x_ref[...]
        gv = g_ref[...]
        r = lax.rsqrt(jnp.mean(xv * xv, axis=-1, keepdims=True) + EPS)
        n = xv * r
        e = n * gv - t_ref[...]
        dy = e * (1.0 / D)
        dn = dy * gv
        dx = r * (dn - n * jnp.mean(dn * n, axis=-1, keepdims=True))
        dx_ref[...] = dx
        dxb_ref[...] = dx.astype(BF16)
        dg = jnp.sum(dy * n, axis=0, keepdims=True)
        per_tok = jnp.mean(e * e, axis=-1, keepdims=True)
        lv = jnp.broadcast_to(0.5 * jnp.sum(per_tok, axis=0, keepdims=True), (1, LANE))

        @pl.when(i == 0)
        def _():
            dg_ref[...] = dg
            l_ref[...] = lv

        @pl.when(i > 0)
        def _():
            dg_ref[...] += dg
            l_ref[...] += lv

    row = pl.BlockSpec((tm, D), lambda i: (i, 0))
    vec = pl.BlockSpec((1, D), lambda i: (0, 0))
    return pl.pallas_call(
        body, name=name, grid=(T // tm,),
        in_specs=[row, row, vec],
        out_specs=[row, row, vec, pl.BlockSpec((1, LANE), lambda i: (0, 0))],
        out_shape=[jax.ShapeDtypeStruct((T, D), F32), jax.ShapeDtypeStruct((T, D), BF16),
                   jax.ShapeDtypeStruct((1, D), F32), jax.ShapeDtypeStruct((1, LANE), F32)],
        compiler_params=_params(("arbitrary",)),
    )(x, tgt, g)


POOL_WINDOWS = (2, 4, 8, 16)


def _pool_offsets(w):
    return range(-(w // 2), w - w // 2)


def _pool_cnt(r0, w, T, shape):
    t = r0 + lax.broadcasted_iota(jnp.int32, shape, 0)
    lo = jnp.maximum(t - w // 2, 0)
    hi = jnp.minimum(t + (w - w // 2 - 1), T - 1)
    return (hi - lo + 1).astype(F32)


def _pooled(pad_ref, r0, w, T):
    ext = _ext(pad_ref, r0)
    s = None
    for o in _pool_offsets(w):
        tap = _tap(ext, o)
        s = tap if s is None else s + tap
    cur = ext[HALO:HALO + CH]
    return s / _pool_cnt(r0, w, T, cur.shape) - cur


def _pool_fwd(P, w_pool, scale, name):
    T = P.shape[0]
    G, PG, _ = w_pool.shape

    def body(a_ref, w_ref, s_ref, o_ref, pad_ref):
        g = pl.program_id(0)
        _fill_pad(pad_ref, lambda r0: a_ref[pl.ds(r0, CH), :].astype(F32), T)
        for gi, w in enumerate(POOL_WINDOWS):
            @pl.when(g == gi)
            def _(w=w):
                def chunk(r0, carry):
                    pooled = _pooled(pad_ref, r0, w, T).astype(BF16)
                    y = jnp.dot(pooled, w_ref[...], preferred_element_type=F32) * s_ref[...]
                    o_ref[pl.ds(r0, CH), :] = y.astype(BF16)
                    return carry

                _chunks(T, chunk)

    return pl.pallas_call(
        body, name=name, grid=(G,),
        in_specs=[pl.BlockSpec((T, PG), lambda g: (0, g)),
                  pl.BlockSpec((None, PG, PG), lambda g: (g, 0, 0)),
                  pl.BlockSpec((1, PG), lambda g: (0, g))],
        out_specs=pl.BlockSpec((T, PG), lambda g: (0, g)),
        out_shape=jax.ShapeDtypeStruct((T, G * PG), BF16),
        scratch_shapes=[pltpu.VMEM((T + 2 * HALO, PG), F32)],
        compiler_params=_params(("parallel",)),
    )(P, w_pool, scale)


def _pool_bwd(P, dy, w_pool, scale, name):
    T = P.shape[0]
    G, PG, _ = w_pool.shape

    def body(a_ref, dy_ref, w_ref, s_ref, da_ref, dw_ref, ds_ref, pad_ref, gpad_ref):
        g = pl.program_id(0)
        gpad_ref[pl.ds(0, HALO), :] = jnp.zeros((HALO, PG), F32)
        gpad_ref[pl.ds(HALO + T, HALO), :] = jnp.zeros((HALO, PG), F32)
        _fill_pad(pad_ref, lambda r0: a_ref[pl.ds(r0, CH), :].astype(F32), T)
        for gi, w in enumerate(POOL_WINDOWS):
            @pl.when(g == gi)
            def _(w=w):
                def chunk1(r0, carry):
                    dw, ds = carry
                    pooled = _pooled(pad_ref, r0, w, T).astype(BF16)
                    ypre = jnp.dot(pooled, w_ref[...], preferred_element_type=F32)
                    dyv = dy_ref[pl.ds(r0, CH), :].astype(F32)
                    ds = ds + jnp.sum(dyv * ypre, axis=0, keepdims=True)
                    dyp = (dyv * s_ref[...]).astype(BF16)
                    dw = dw + lax.dot_general(pooled, dyp, _DIMS["tn"], preferred_element_type=F32)
                    dpool = lax.dot_general(dyp, w_ref[...], _DIMS["nt"], preferred_element_type=F32)
                    gpad_ref[pl.ds(HALO + r0, CH), :] = dpool / _pool_cnt(r0, w, T, dpool.shape)
                    return dw, ds

                dw, ds = _chunks(T, chunk1, (jnp.zeros((PG, PG), F32), jnp.zeros((1, PG), F32)))
                dw_ref[...] = dw
                ds_ref[...] = ds

                def chunk2(r0, carry):
                    ext = _ext(gpad_ref, r0)
                    cur = ext[HALO:HALO + CH]
                    acc = None
                    for o in _pool_offsets(w):
                        tap = _tap(ext, -o)
                        acc = tap if acc is None else acc + tap
                    da_ref[pl.ds(r0, CH), :] = (acc - cur * _pool_cnt(r0, w, T, cur.shape)).astype(BF16)
                    return carry

                _chunks(T, chunk2)

    col = pl.BlockSpec((T, PG), lambda g: (0, g))
    return pl.pallas_call(
        body, name=name, grid=(G,),
        in_specs=[col, col, pl.BlockSpec((None, PG, PG), lambda g: (g, 0, 0)), pl.BlockSpec((1, PG), lambda g: (0, g))],
        out_specs=[col, pl.BlockSpec((None, PG, PG), lambda g: (g, 0, 0)), pl.BlockSpec((1, PG), lambda g: (0, g))],
        out_shape=[jax.ShapeDtypeStruct((T, G * PG), BF16), jax.ShapeDtypeStruct((G, PG, PG), F32),
                   jax.ShapeDtypeStruct((1, G * PG), F32)],
        scratch_shapes=[pltpu.VMEM((T + 2 * HALO, PG), F32), pltpu.VMEM((T + 2 * HALO, PG), F32)],
        compiler_params=_params(("parallel",)),
    )(P, dy, w_pool, scale)


SGU_CHUNK = 128


def _sgu_common(u_ref, v_ref, gs_ref):
    up = u_ref[...].astype(F32)
    vp = v_ref[...].astype(F32)
    gv = _gelu(vp)
    rv = lax.rsqrt(jnp.mean(gv * gv, axis=-1, keepdims=True) + EPS)
    nrm = gv * rv
    return up, vp, nrm, rv, (nrm * gs_ref[...]).astype(BF16)


def _sgu_fwd(P, sgu_w, b_exp, gs, name):
    T = P.shape[0]
    G, _, SG = b_exp.shape
    DP = G * SG
    tm = _row_tile(T)

    def body(u_ref, v_ref, w_ref, b_ref, gs_ref, o_ref):
        up, _, _, _, vn = _sgu_common(u_ref, v_ref, gs_ref)
        gu = _gelu(up)
        for n in range(tm // SGU_CHUNK):
            rows = slice(n * SGU_CHUNK, (n + 1) * SGU_CHUNK)
            for g in range(G):
                cols = slice(g * SG, (g + 1) * SG)
                z = jnp.dot(w_ref[g], vn[rows, cols], preferred_element_type=F32) + b_ref[g]
                o_ref[rows, cols] = (gu[rows, cols] * z).astype(BF16)

    return pl.pallas_call(
        body, name=name, grid=(T // tm,),
        in_specs=[pl.BlockSpec((tm, DP), lambda i: (i, 1)), pl.BlockSpec((tm, DP), lambda i: (i, 2)),
                  pl.BlockSpec(sgu_w.shape, lambda i: (0, 0, 0)), pl.BlockSpec(b_exp.shape, lambda i: (0, 0, 0)),
                  pl.BlockSpec((1, DP), lambda i: (0, 0))],
        out_specs=pl.BlockSpec((tm, DP), lambda i: (i, 0)),
        out_shape=jax.ShapeDtypeStruct((T, DP), BF16),
        compiler_params=_params(("parallel",)),
    )(P, P, sgu_w, b_exp, gs)


def _sgu_bwd(P, dy, da, sgu_w, b_exp, gs, name):
    T = P.shape[0]
    G, _, SG = b_exp.shape
    DP = G * SG
    tm = _row_tile(T)

    def body(u_ref, v_ref, dy_ref, da_ref, w_ref, b_ref, gs_ref, o_ref, dw_ref, db_ref, dgs_ref, dzs_ref):
        i = pl.program_id(0)
        up, vp, nrm, rv, vn = _sgu_common(u_ref, v_ref, gs_ref)
        gu = _gelu(up)
        dyv = dy_ref[...].astype(F32)
        o_ref[:, 0:DP] = da_ref[...]

        @pl.when(i == 0)
        def _():
            dw_ref[...] = jnp.zeros(dw_ref.shape, F32)
            dzs_ref[...] = jnp.zeros(dzs_ref.shape, F32)
            dgs_ref[...] = jnp.zeros(dgs_ref.shape, F32)

        dgs = jnp.zeros((1, DP), F32)
        for n in range(tm // SGU_CHUNK):
            rows = slice(n * SGU_CHUNK, (n + 1) * SGU_CHUNK)
            dvn_parts = []
            for g in range(G):
                cols = slice(g * SG, (g + 1) * SG)
                vng = vn[rows, cols]
                z = jnp.dot(w_ref[g], vng, preferred_element_type=F32) + b_ref[g]
                dyg = dyv[rows, cols]
                du = dyg * z
                o_ref[rows, DP + g * SG:DP + (g + 1) * SG] = (du * _gelu_grad(up[rows, cols])).astype(BF16)
                dz = dyg * gu[rows, cols]
                dzb = dz.astype(BF16)
                dzs_ref[g] += dz
                dw_ref[g] += lax.dot_general(dzb, vng, _DIMS["nt"], preferred_element_type=F32)
                dvn_parts.append(lax.dot_general(w_ref[g], dzb, _DIMS["tn"], preferred_element_type=F32))
            dvn = jnp.concatenate(dvn_parts, axis=1)
            nr = nrm[rows]
            dgs = dgs + jnp.sum(dvn * nr, axis=0, keepdims=True)
            dn = dvn * gs_ref[...]
            dgv = rv[rows] * (dn - nr * jnp.mean(dn * nr, axis=-1, keepdims=True))
            o_ref[rows, 2 * DP:3 * DP] = (dgv * _gelu_grad(vp[rows])).astype(BF16)
        dgs_ref[...] += dgs

        @pl.when(i == T // tm - 1)
        def _():
            for g in range(G):
                db_ref[g] = jnp.broadcast_to(jnp.sum(dzs_ref[g], axis=1, keepdims=True), (SGU_CHUNK, SG))

    full3 = lambda a: pl.BlockSpec(a.shape, lambda i: (0, 0, 0))
    return pl.pallas_call(
        body, name=name, grid=(T // tm,),
        in_specs=[pl.BlockSpec((tm, DP), lambda i: (i, 1)), pl.BlockSpec((tm, DP), lambda i: (i, 2)),
                  pl.BlockSpec((tm, DP), lambda i: (i, 0)), pl.BlockSpec((tm, DP), lambda i: (i, 0)),
                  full3(sgu_w), full3(b_exp), pl.BlockSpec((1, DP), lambda i: (0, 0))],
        out_specs=[pl.BlockSpec((tm, 3 * DP), lambda i: (i, 0)), full3(sgu_w), full3(b_exp),
                   pl.BlockSpec((1, DP), lambda i: (0, 0))],
        out_shape=[jax.ShapeDtypeStruct((T, 6 * DP), BF16), jax.ShapeDtypeStruct(sgu_w.shape, F32),
                   jax.ShapeDtypeStruct(b_exp.shape, F32), jax.ShapeDtypeStruct((1, DP), F32)],
        scratch_shapes=[pltpu.VMEM(b_exp.shape, F32)],
        compiler_params=_params(("arbitrary",)),
    )(P, P, dy, da, sgu_w, b_exp, gs)


def _conv_fwd(P, cw, name):
    T = P.shape[0]
    DP = cw.shape[1]
    tc = 256
    nb = DP // tc

    def body(xc_ref, bg_ref, cg_ref, w_ref, o_ref, pad_ref):
        _fill_pad(pad_ref, lambda r0: cg_ref[pl.ds(r0, CH), :].astype(F32) * xc_ref[pl.ds(r0, CH), :].astype(F32), T)
        w = w_ref[...]

        def chunk(r0, carry):
            cq = _conv3(_ext(pad_ref, r0), w)
            o_ref[pl.ds(r0, CH), :] = (bg_ref[pl.ds(r0, CH), :].astype(F32) * cq).astype(BF16)
            return carry

        _chunks(T, chunk)

    col = lambda off: pl.BlockSpec((T, tc), lambda j: (0, off * nb + j))
    return pl.pallas_call(
        body, name=name, grid=(nb,),
        in_specs=[col(3), col(4), col(5), pl.BlockSpec((3, tc), lambda j: (0, j))],
        out_specs=pl.BlockSpec((T, tc), lambda j: (0, j)),
        out_shape=jax.ShapeDtypeStruct((T, DP), BF16),
        scratch_shapes=[pltpu.VMEM((T + 2 * HALO, tc), F32)],
        compiler_params=_params(("parallel",)),
    )(P, P, P, cw)


def _conv_bwd(P, dy, cw, dPl, name):
    T = P.shape[0]
    DP = cw.shape[1]
    tc = 256
    nb = DP // tc

    def body(xc_ref, bg_ref, cg_ref, dy_ref, w_ref, prev_ref, o_ref, dw_ref, qpad_ref, dpad_ref):
        del prev_ref
        seg = pl.program_id(1)
        w = w_ref[...]
        rows = lambda ref, r0: ref[pl.ds(r0, CH), :].astype(F32)
        _fill_pad(qpad_ref, lambda r0: rows(cg_ref, r0) * rows(xc_ref, r0), T)
        _fill_pad(dpad_ref, lambda r0: rows(dy_ref, r0) * rows(bg_ref, r0), T)

        @pl.when(seg == 0)
        def _():
            def chunk(r0, carry):
                dq = _conv3_t(_ext(dpad_ref, r0), w)
                o_ref[pl.ds(r0, CH), :] = (dq * rows(cg_ref, r0)).astype(BF16)
                dcq = rows(dy_ref, r0) * rows(bg_ref, r0)
                return [a + b for a, b in zip(carry, _conv3_dw(_ext(qpad_ref, r0), dcq))]

            dws = _chunks(T, chunk, [jnp.zeros((1, tc), F32)] * 3)
            dw_ref[...] = jnp.concatenate(dws + [jnp.zeros((5, tc), F32)], axis=0)

        @pl.when(seg == 1)
        def _():
            def chunk(r0, carry):
                cq = _conv3(_ext(qpad_ref, r0), w)
                o_ref[pl.ds(r0, CH), :] = (rows(dy_ref, r0) * cq).astype(BF16)
                return carry

            _chunks(T, chunk)

        @pl.when(seg == 2)
        def _():
            def chunk(r0, carry):
                dq = _conv3_t(_ext(dpad_ref, r0), w)
                o_ref[pl.ds(r0, CH), :] = (dq * rows(xc_ref, r0)).astype(BF16)
                return carry

            _chunks(T, chunk)

    col = lambda off: pl.BlockSpec((T, tc), lambda j, s: (0, off * nb + j))
    return pl.pallas_call(
        body, name=name, grid=(nb, 3),
        in_specs=[col(3), col(4), col(5), pl.BlockSpec((T, tc), lambda j, s: (0, j)),
                  pl.BlockSpec((3, tc), lambda j, s: (0, j)), pl.BlockSpec(memory_space=pl.ANY)],
        out_specs=[pl.BlockSpec((T, tc), lambda j, s: (0, (3 + s) * nb + j)), pl.BlockSpec((8, tc), lambda j, s: (0, j))],
        out_shape=[jax.ShapeDtypeStruct(dPl.shape, BF16), jax.ShapeDtypeStruct((8, DP), F32)],
        scratch_shapes=[pltpu.VMEM((T + 2 * HALO, tc), F32), pltpu.VMEM((T + 2 * HALO, tc), F32)],
        input_output_aliases={5: 0},
        compiler_params=_params(("parallel", "arbitrary")),
    )(P, P, P, dy, cw, dPl)


def _merge_fwd(P, ys, ws, name):
    T = P.shape[0]
    DP, D = ws[0].shape
    tm = _pick(T, (512, 256, 128))
    tn = _pick(D, (1024, 512, 256))
    goff = (6 * DP) // tn

    def body(ya, yb, yc, wa, wb, wc, ga, gb, gc, oa_ref, ob_ref, oc_ref, m_ref):
        m = None
        for y, w, g, o_ref in ((ya, wa, ga, oa_ref), (yb, wb, gb, ob_ref), (yc, wc, gc, oc_ref)):
            o = jnp.dot(y[...], w[...], preferred_element_type=F32)
            o_ref[...] = o.astype(BF16)
            t = _sigmoid(g[...].astype(F32)) * o
            m = t if m is None else m + t
        m_ref[...] = m.astype(BF16)

    yspec = pl.BlockSpec((tm, DP), lambda i, j: (i, 0))
    wspec = pl.BlockSpec((DP, tn), lambda i, j: (0, j))
    gspec = lambda b: pl.BlockSpec((tm, tn), lambda i, j: (i, goff + b * (D // tn) + j))
    ospec = pl.BlockSpec((tm, tn), lambda i, j: (i, j))
    return pl.pallas_call(
        body, name=name, grid=(T // tm, D // tn),
        in_specs=[yspec] * 3 + [wspec] * 3 + [gspec(0), gspec(1), gspec(2)],
        out_specs=[ospec] * 4,
        out_shape=[jax.ShapeDtypeStruct((T, D), BF16)] * 4,
        compiler_params=_params(("parallel", "parallel")),
    )(*ys, *ws, P, P, P)


def _merge_bwd(P, dM, os_, name):
    T, D = dM.shape
    tm = _row_tile(T)

    def body(dm_ref, oa, ob, oc, g_ref, da_ref, db_ref, dc_ref, dg_ref):
        dm = dm_ref[...].astype(F32)
        for b, (o_ref, d_ref) in enumerate(((oa, da_ref), (ob, db_ref), (oc, dc_ref))):
            s = _sigmoid(g_ref[:, b * D:(b + 1) * D].astype(F32))
            d_ref[...] = (dm * s).astype(BF16)
            dg_ref[:, b * D:(b + 1) * D] = (dm * o_ref[...].astype(F32) * s * (1.0 - s)).astype(BF16)

    row = pl.BlockSpec((tm, D), lambda i: (i, 0))
    return pl.pallas_call(
        body, name=name, grid=(T // tm,),
        in_specs=[row] * 4 + [pl.BlockSpec((tm, 3 * D), lambda i: (i, 1))],
        out_specs=[row] * 3 + [pl.BlockSpec((tm, 3 * D), lambda i: (i, 0))],
        out_shape=[jax.ShapeDtypeStruct((T, D), BF16)] * 3 + [jax.ShapeDtypeStruct((T, 3 * D), BF16)],
        compiler_params=_params(("parallel",)),
    )(dM, *os_, P)


def _glu_fwd(U, fw, name):
    T, F2 = U.shape
    F = F2 // 2
    tc = 256
    nb = F // tc

    def body(ug_ref, uv_ref, wg_ref, wv_ref, o_ref, gpad_ref, vpad_ref):
        _fill_pad(gpad_ref, lambda r0: ug_ref[pl.ds(r0, CH), :].astype(F32), T)
        _fill_pad(vpad_ref, lambda r0: uv_ref[pl.ds(r0, CH), :].astype(F32), T)
        wg, wv = wg_ref[...], wv_ref[...]

        def chunk(r0, carry):
            gate = _conv3(_ext(gpad_ref, r0), wg)
            val = _conv3(_ext(vpad_ref, r0), wv)
            o_ref[pl.ds(r0, CH), :] = (gate * _sigmoid(gate) * val).astype(BF16)
            return carry

        _chunks(T, chunk)

    return pl.pallas_call(
        body, name=name, grid=(nb,),
        in_specs=[pl.BlockSpec((T, tc), lambda j: (0, j)), pl.BlockSpec((T, tc), lambda j: (0, nb + j)),
                  pl.BlockSpec((3, tc), lambda j: (0, j)), pl.BlockSpec((3, tc), lambda j: (0, nb + j))],
        out_specs=pl.BlockSpec((T, tc), lambda j: (0, j)),
        out_shape=jax.ShapeDtypeStruct((T, F), BF16),
        scratch_shapes=[pltpu.VMEM((T + 2 * HALO, tc), F32)] * 2,
        compiler_params=_params(("parallel",)),
    )(U, U, fw, fw)


def _glu_bwd(U, dact, fw, name):
    T, F2 = U.shape
    F = F2 // 2
    tc = 128
    nb = F // tc

    def body(ug_ref, uv_ref, da_ref, wg_ref, wv_ref, dug_ref, duv_ref, dwg_ref, dwv_ref, gpad, vpad, dgpad, dvpad):
        _fill_pad(gpad, lambda r0: ug_ref[pl.ds(r0, CH), :].astype(F32), T)
        _fill_pad(vpad, lambda r0: uv_ref[pl.ds(r0, CH), :].astype(F32), T)
        wg, wv = wg_ref[...], wv_ref[...]

        def chunk1(r0, carry):
            gext, vext = _ext(gpad, r0), _ext(vpad, r0)
            gate = _conv3(gext, wg)
            val = _conv3(vext, wv)
            s = _sigmoid(gate)
            da = da_ref[pl.ds(r0, CH), :].astype(F32)
            dval = da * (gate * s)
            dgate = da * val * (s * (1.0 + gate * (1.0 - s)))
            dgpad[pl.ds(HALO + r0, CH), :] = dgate
            dvpad[pl.ds(HALO + r0, CH), :] = dval
            new = _conv3_dw(gext, dgate) + _conv3_dw(vext, dval)
            return [a + b for a, b in zip(carry, new)]

        z = jnp.zeros((HALO, tc), F32)
        for p in (dgpad, dvpad):
            p[pl.ds(0, HALO), :] = z
            p[pl.ds(HALO + T, HALO), :] = z
        dws = _chunks(T, chunk1, [jnp.zeros((1, tc), F32)] * 6)
        dwg_ref[...] = jnp.concatenate(dws[:3] + [jnp.zeros((5, tc), F32)], axis=0)
        dwv_ref[...] = jnp.concatenate(dws[3:] + [jnp.zeros((5, tc), F32)], axis=0)

        def chunk2(r0, carry):
            dug_ref[pl.ds(r0, CH), :] = _conv3_t(_ext(dgpad, r0), wg).astype(BF16)
            duv_ref[pl.ds(r0, CH), :] = _conv3_t(_ext(dvpad, r0), wv).astype(BF16)
            return carry

        _chunks(T, chunk2)

    lo = pl.BlockSpec((T, tc), lambda j: (0, j))
    hi = pl.BlockSpec((T, tc), lambda j: (0, nb + j))
    wlo = pl.BlockSpec((3, tc), lambda j: (0, j))
    whi = pl.BlockSpec((3, tc), lambda j: (0, nb + j))
    dwspec = pl.BlockSpec((8, tc), lambda j: (0, j))
    return pl.pallas_call(
        body, name=name, grid=(nb,),
        in_specs=[lo, hi, lo, wlo, whi],
        out_specs=[lo, lo, dwspec, dwspec],
        out_shape=[jax.ShapeDtypeStruct((T, F), BF16)] * 2 + [jax.ShapeDtypeStruct((8, F), F32)] * 2,
        scratch_shapes=[pltpu.VMEM((T + 2 * HALO, tc), F32)] * 4,
        compiler_params=_params(("parallel",)),
    )(U, U, dact, fw, fw)


_ANY = pl.BlockSpec(memory_space=pl.ANY)
_MESH = pl.DeviceIdType.MESH


def _mesh_pos():
    return lax.axis_index("x"), lax.axis_index("y"), lax.axis_index("c")


def _other_chips(x, y):
    return [(1 - x, y), (x, 1 - y), (1 - x, 1 - y)]


def _all_gather(shards, name):
    n = len(shards)

    def body(*refs):
        ins, outs = refs[:n], refs[n:2 * n]
        send_sems, recv_sems, local_sems = refs[2 * n:]
        x, y, c = _mesh_pos()
        me, sibling = (x, y, c), (x, y, 1 - c)
        chips = _other_chips(x, y)

        def copy(a, k, block, to, src=None):
            px, py, pc = block
            dst = outs[a].at[4 * px + 2 * py + pc]
            return pltpu.make_async_remote_copy(
                src_ref=dst if src is None else src, dst_ref=dst,
                send_sem=send_sems.at[a * 7 + k], recv_sem=recv_sems.at[a * 7 + k],
                device_id=to, device_id_type=_MESH)

        mine = [pltpu.make_async_copy(ins[a], outs[a].at[4 * x + 2 * y + c], local_sems.at[a]) for a in range(n)]
        for cp in mine:
            cp.start()
        first = []
        for a in range(n):
            first.append(copy(a, 0, me, sibling, src=ins[a]))
            first += [copy(a, 1 + j, me, (*chip, c), src=ins[a]) for j, chip in enumerate(chips)]
        for cp in first:
            cp.start()
        passed = []
        for a in range(n):
            for j, chip in enumerate(chips):
                copy(a, 1 + j, (*chip, c), me).wait_recv()
                fwd = copy(a, 4 + j, (*chip, c), sibling)
                fwd.start()
                passed.append(fwd)
        for a in range(n):
            copy(a, 0, sibling, me).wait_recv()
            for j, chip in enumerate(chips):
                copy(a, 4 + j, (*chip, 1 - c), me).wait_recv()
        for cp in first + passed:
            cp.wait_send()
        for cp in mine:
            cp.wait()

    return pl.pallas_call(
        body, name=name,
        in_specs=[_ANY] * n, out_specs=[_ANY] * n,
        out_shape=[jax.ShapeDtypeStruct((N_DEV,) + s.shape, s.dtype) for s in shards],
        scratch_shapes=[pltpu.SemaphoreType.DMA((7 * n,)), pltpu.SemaphoreType.DMA((7 * n,)),
                        pltpu.SemaphoreType.DMA((n,))],
    )(*shards)


def _rs_sibling(grads, name):
    n = len(grads)

    def body(*refs):
        ins, outs = refs[:n], refs[n:2 * n]
        send_sems, recv_sems = refs[2 * n:]
        x, y, c = _mesh_pos()
        cps = []
        for a in range(n):
            for q in range(4):
                cps.append(pltpu.make_async_remote_copy(
                    src_ref=ins[a].at[2 * q + (1 - c)], dst_ref=outs[a].at[q],
                    send_sem=send_sems.at[4 * a + q], recv_sem=recv_sems.at[4 * a + q],
                    device_id=(x, y, 1 - c), device_id_type=_MESH))
        for cp in cps:
            cp.start()
        for cp in cps:
            cp.wait()

    return pl.pallas_call(
        body, name=name,
        in_specs=[_ANY] * n, out_specs=[_ANY] * n,
        out_shape=[jax.ShapeDtypeStruct((4,) + g.shape[1:], g.dtype) for g in grads],
        scratch_shapes=[pltpu.SemaphoreType.DMA((4 * n,)), pltpu.SemaphoreType.DMA((4 * n,))],
    )(*grads)


def _rs_chips(pairs, name):
    n = len(pairs)

    def body(*refs):
        ins, outs = refs[:n], refs[n:2 * n]
        send_sems, recv_sems = refs[2 * n:]
        x, y, c = _mesh_pos()
        cps = []
        for a in range(n):
            for j, (px, py) in enumerate(_other_chips(x, y)):
                cps.append(pltpu.make_async_remote_copy(
                    src_ref=ins[a].at[2 * px + py], dst_ref=outs[a].at[j],
                    send_sem=send_sems.at[3 * a + j], recv_sem=recv_sems.at[3 * a + j],
                    device_id=(px, py, c), device_id_type=_MESH))
        for cp in cps:
            cp.start()
        for cp in cps:
            cp.wait()

    return pl.pallas_call(
        body, name=name,
        in_specs=[_ANY] * n, out_specs=[_ANY] * n,
        out_shape=[jax.ShapeDtypeStruct((3,) + p.shape[1:], p.dtype) for p in pairs],
        scratch_shapes=[pltpu.SemaphoreType.DMA((3 * n,)), pltpu.SemaphoreType.DMA((3 * n,))],
    )(*pairs)


def _rows_tile(R, C, budget=2 * MIB):
    for t in (2048, 1024, 704, 512, 352, 256, 128, 64, 32, 16, 8):
        if R % t == 0 and t * C * 4 <= budget:
            return t
    return R


def _pair_sum(grad, recv, core, name):
    _, R, C = grad.shape
    tr = _rows_tile(R, C)

    def body(c_ref, g_ref, r_ref, o_ref):
        del c_ref
        o_ref[...] = (g_ref[...].astype(F32) + r_ref[...].astype(F32)).astype(BF16)

    return pl.pallas_call(
        body, name=name,
        grid_spec=pltpu.PrefetchScalarGridSpec(
            num_scalar_prefetch=1, grid=(4, R // tr),
            in_specs=[pl.BlockSpec((None, tr, C), lambda q, i, c: (2 * q + c[0], i, 0)),
                      pl.BlockSpec((None, tr, C), lambda q, i, c: (q, i, 0))],
            out_specs=pl.BlockSpec((None, tr, C), lambda q, i, c: (q, i, 0))),
        out_shape=jax.ShapeDtypeStruct((4, R, C), BF16),
        compiler_params=_params(("parallel", "parallel")),
    )(core, grad, recv)


ADAM_LR = 0.001
ADAM_B1 = 0.9
ADAM_B2 = 0.999
ADAM_EPS = 1e-08
ADAM_WD = 0.01
ADAM_STEP = 10


def _adamw(w, g, m, v):
    m = ADAM_B1 * m + (1.0 - ADAM_B1) * g
    v = ADAM_B2 * v + (1.0 - ADAM_B2) * (g * g)
    m_hat = m / (1.0 - ADAM_B1 ** ADAM_STEP)
    v_hat = v / (1.0 - ADAM_B2 ** ADAM_STEP)
    delta = -ADAM_LR * (m_hat / (jnp.sqrt(v_hat) + ADAM_EPS) + ADAM_WD * w)
    return delta, m, v


def _adamw_big(pair, recv, w, m, v, chip, layer, prev, name):
    L, R, C = w.shape
    tr = _rows_tile(R, C, MIB)
    has_prev = prev is not None

    def body(chip_ref, p_ref, r0_ref, r1_ref, r2_ref, w_ref, m_ref, v_ref, *rest):
        del chip_ref
        g_ref, d_ref, mo_ref, vo_ref = rest[-4:]
        g = ((p_ref[...].astype(F32) + r0_ref[...].astype(F32)) + r1_ref[...].astype(F32)) + r2_ref[...].astype(F32)
        delta, m2, v2 = _adamw(w_ref[...], g, m_ref[...], v_ref[...])
        g_ref[...] = g
        d_ref[...] = delta
        mo_ref[...] = m2
        vo_ref[...] = v2

    lay = pl.BlockSpec((None, tr, C), lambda i, c: (layer, i, 0))
    rspec = lambda j: pl.BlockSpec((None, tr, C), lambda i, c: (j, i, 0))
    in_specs = [pl.BlockSpec((None, tr, C), lambda i, c: (c[0], i, 0)), rspec(0), rspec(1), rspec(2), lay, lay, lay]
    args = [chip, pair, recv, recv, recv, w, m, v]
    aliases = {}
    if has_prev:
        in_specs += [_ANY] * 4
        args += list(prev)
        aliases = {8 + k: k for k in range(4)}
    return pl.pallas_call(
        body, name=name,
        grid_spec=pltpu.PrefetchScalarGridSpec(
            num_scalar_prefetch=1, grid=(R // tr,), in_specs=in_specs, out_specs=[lay] * 4),
        out_shape=[jax.ShapeDtypeStruct((L, R, C), F32)] * 4,
        input_output_aliases=aliases,
        compiler_params=_params(("parallel",)),
    )(*args)


def _sum_slots(gathered, name):
    _, R, C = gathered.shape
    tr = _rows_tile(R, C, MIB // 2)

    def body(x_ref, o_ref):
        s = x_ref[0]
        for k in range(1, N_DEV):
            s = s + x_ref[k]
        o_ref[...] = s

    return pl.pallas_call(
        body, name=name, grid=(R // tr,),
        in_specs=[pl.BlockSpec((N_DEV, tr, C), lambda i: (0, i, 0))],
        out_specs=pl.BlockSpec((tr, C), lambda i: (i, 0)),
        out_shape=jax.ShapeDtypeStruct((R, C), F32),
        compiler_params=_params(("parallel",)),
    )(gathered)


def _adamw_small(w, g, m, v, name):
    R, C = w.shape
    tr = _rows_tile(R, C, MIB // 2)

    def body(w_ref, g_ref, m_ref, v_ref, d_ref, mo_ref, vo_ref):
        d_ref[...], mo_ref[...], vo_ref[...] = _adamw(w_ref[...], g_ref[...], m_ref[...], v_ref[...])

    spec = pl.BlockSpec((tr, C), lambda i: (i, 0))
    return pl.pallas_call(
        body, name=name, grid=(R // tr,),
        in_specs=[spec] * 4, out_specs=[spec] * 3,
        out_shape=[jax.ShapeDtypeStruct((R, C), F32)] * 3,
        compiler_params=_params(("parallel",)),
    )(w, g, m, v)


def _pack(arrs):
    parts = []
    for a in arrs:
        flat = a.reshape(-1)
        pad = (-flat.shape[0]) % (8 * LANE)
        if pad:
            flat = jnp.pad(flat, (0, pad))
        parts.append(flat.reshape(-1, LANE))
    return jnp.concatenate(parts, axis=0)


def _unpack(packed, shapes):
    out, r = [], 0
    for s in shapes:
        n = 1
        for d in s:
            n *= d
        rows = -(-n // (8 * LANE)) * 8
        out.append(packed[r:r + rows].reshape(-1)[:n].reshape(s))
        r += rows
    return out


def _to_slices(a, ns):
    K = a.shape[0]
    return a.reshape(K, N_DEV, ns).transpose(1, 0, 2)


def _from_slices(a):
    _, K, ns = a.shape
    return a.transpose(1, 0, 2).reshape(K, N_DEV * ns)


def kernel(x, norm1_g, w_in, w_pool, pool_scale, sgu_norm_g, sgu_w, sgu_b, conv_w, w_pool_out, w_sgu_out, w_conv_out, w_o, norm2_g, w_up, ffn_conv_w, w_down, final_g, loss_target, m_norm1_g, m_w_in, m_w_pool, m_pool_scale, m_sgu_norm_g, m_sgu_w, m_sgu_b, m_conv_w, m_w_pool_out, m_w_sgu_out, m_w_conv_out, m_w_o, m_norm2_g, m_w_up, m_ffn_conv_w, m_w_down, m_final_g, v_norm1_g, v_w_in, v_w_pool, v_pool_scale, v_sgu_norm_g, v_sgu_w, v_sgu_b, v_conv_w, v_w_pool_out, v_w_sgu_out, v_w_conv_out, v_w_o, v_norm2_g, v_w_up, v_ffn_conv_w, v_w_down, v_final_g):
    L = norm1_g.shape[0]
    x0 = x[0]
    tgt = loss_target[0]
    G, PG = w_pool.shape[1], w_pool.shape[3]
    F = w_down.shape[1] * N_DEV
    px, py, pc = _mesh_pos()
    me = 4 * px + 2 * py + pc
    core = jnp.reshape(pc, (1,)).astype(jnp.int32)
    chip = jnp.reshape(2 * px + py, (1,)).astype(jnp.int32)

    big = (w_in, w_up, w_o, w_down, w_pool_out, w_sgu_out, w_conv_out)
    shards = []
    for l in range(L):
        shards += [w[l].astype(BF16) for w in big]
        shards += [w_pool[l].reshape(G * w_pool.shape[2], PG).astype(BF16), conv_w[l], ffn_conv_w[l]]
    gathered = _all_gather(shards, "ag_weights")
    per = len(big) + 3

    def layer_weights(l):
        win3, wup3, wo, wdown, wa, wb, wc, wp, cw, fw = gathered[l * per:(l + 1) * per]
        return dict(
            win3=win3, wup3=wup3, wo=wo.reshape(-1, wo.shape[2]), wdown=wdown.reshape(-1, wdown.shape[2]),
            wa=_from_slices(wa), wb=_from_slices(wb), wc=_from_slices(wc),
            wpool=wp.reshape(N_DEV, G, -1, PG).transpose(1, 0, 2, 3).reshape(G, PG, PG),
            cw=_from_slices(cw), fw=_from_slices(fw),
            sguw=sgu_w[l].astype(BF16),
            bexp=jnp.broadcast_to(sgu_b[l].T[:, :, None],
                                  (sgu_b.shape[2], sgu_b.shape[1], sgu_norm_g.shape[1] // sgu_b.shape[2])),
            g1=norm1_g[l][None], g2=norm2_g[l][None], gs=sgu_norm_g[l][None], scale=pool_scale[l][None])

    W = [layer_weights(l) for l in range(L)]

    saved = []
    xc = x0
    for l in range(L):
        w = W[l]
        h1 = _rms_fwd(xc, w["g1"], f"rms1_fwd_{l}")
        P = _matmul(h1, w["win3"], mode="nn", out_dtype=BF16, name=f"proj_in_{l}", b_slices=(0, N_DEV))
        ya = _pool_fwd(P, w["wpool"], w["scale"], f"pool_fwd_{l}")
        yb = _sgu_fwd(P, w["sguw"], w["bexp"], w["gs"], f"sgu_fwd_{l}")
        yc = _conv_fwd(P, w["cw"], f"conv_fwd_{l}")
        oa, ob, oc, M = _merge_fwd(P, (ya, yb, yc), (w["wa"], w["wb"], w["wc"]), f"merge_fwd_{l}")
        x1 = _matmul(M, w["wo"], mode="nn", out_dtype=F32, name=f"proj_o_{l}", res=xc)
        h2 = _rms_fwd(x1, w["g2"], f"rms2_fwd_{l}")
        U = _matmul(h2, w["wup3"], mode="nn", out_dtype=BF16, name=f"proj_up_{l}", b_slices=(0, N_DEV))
        act = _glu_fwd(U, w["fw"], f"glu_fwd_{l}")
        x2 = _matmul(act, w["wdown"], mode="nn", out_dtype=F32, name=f"proj_down_{l}", res=x1, tm=_pick(x1.shape[0], (512, 256)))
        saved.append(dict(x0=xc, h1=h1, P=P, ya=ya, yb=yb, yc=yc, oa=oa, ob=ob, oc=oc, M=M, x1=x1, h2=h2, U=U, act=act))
        xc = x2

    dx, dxb, d_final_g, loss_part = _loss_head(xc, tgt, final_g[None], "loss_head")
    loss = lax.psum(loss_part[0, 0], ("x", "y", "c"))

    big_grads = [None] * L
    small = [None] * L
    for l in reversed(range(L)):
        w, s = W[l], saved[l]
        dact = _matmul(dxb, w["wdown"], mode="nt", out_dtype=BF16, name=f"d_act_{l}", tn=_pick(F, (512, 256)))
        g_down = _matmul(s["act"], dxb, mode="tn", out_dtype=BF16, name=f"g_down_{l}", tm=_pick(F, (512, 256)))
        dug, duv, dfwg, dfwv = _glu_bwd(s["U"], dact, w["fw"], f"glu_bwd_{l}")
        dh2 = _matmul(dug, w["wup3"], mode="nt", out_dtype=F32, name=f"d_h2a_{l}", b_slices=(0, 4))
        dh2 = _matmul(duv, w["wup3"], mode="nt", out_dtype=F32, name=f"d_h2b_{l}", b_slices=(4, 4), res=dh2)
        g_up = _matmul(s["h2"], dug, mode="tn", out_dtype=BF16, name=f"g_upa_{l}", out_slices=(0, 4), tm=512)
        g_up = _matmul(s["h2"], duv, mode="tn", out_dtype=BF16, name=f"g_upb_{l}", out_slices=(4, 4), prev=g_up, tm=512)
        dx1, dx1b, dg2 = _rms_bwd(dh2, s["x1"], w["g2"], dx, f"rms2_bwd_{l}")
        dM = _matmul(dx1b, w["wo"], mode="nt", out_dtype=BF16, name=f"d_m_{l}")
        g_o = _matmul(s["M"], dx1b, mode="tn", out_dtype=BF16, name=f"g_o_{l}")
        doa, dob, doc, dPg = _merge_bwd(s["P"], dM, (s["oa"], s["ob"], s["oc"]), f"merge_bwd_{l}")
        dya = _matmul(doa, w["wa"], mode="nt", out_dtype=BF16, name=f"d_ya_{l}")
        dyb = _matmul(dob, w["wb"], mode="nt", out_dtype=BF16, name=f"d_yb_{l}")
        dyc = _matmul(doc, w["wc"], mode="nt", out_dtype=BF16, name=f"d_yc_{l}")
        g_a = _matmul(s["ya"], doa, mode="tn", out_dtype=BF16, name=f"g_a_{l}")
        g_b = _matmul(s["yb"], dob, mode="tn", out_dtype=BF16, name=f"g_b_{l}")
        g_c = _matmul(s["yc"], doc, mode="tn", out_dtype=BF16, name=f"g_c_{l}")
        da, g_pool, dscale = _pool_bwd(s["P"], dya, w["wpool"], w["scale"], f"pool_bwd_{l}")
        dPl, g_sguw, db_exp, dgs = _sgu_bwd(s["P"], dyb, da, w["sguw"], w["bexp"], w["gs"], f"sgu_bwd_{l}")
        dPl, dcw = _conv_bwd(s["P"], dyc, w["cw"], dPl, f"conv_bwd_{l}")
        dh1 = _matmul(dPl, w["win3"], mode="nt", out_dtype=F32, name=f"d_h1a_{l}", b_slices=(0, 4))
        dh1 = _matmul(dPg, w["win3"], mode="nt", out_dtype=F32, name=f"d_h1b_{l}", b_slices=(4, 4), res=dh1)
        g_in = _matmul(s["h1"], dPl, mode="tn", out_dtype=BF16, name=f"g_ina_{l}", out_slices=(0, 4))
        g_in = _matmul(s["h1"], dPg, mode="tn", out_dtype=BF16, name=f"g_inb_{l}", out_slices=(4, 4), prev=g_in)
        dx, dxb, dg1 = _rms_bwd(dh1, s["x0"], w["g1"], dx1, f"rms1_bwd_{l}")
        ns_out = w_pool_out.shape[2]
        g_pool_s = g_pool.reshape(G, N_DEV, -1, PG).transpose(1, 0, 2, 3).reshape(N_DEV, -1, PG).astype(BF16)
        big_grads[l] = [g_in, g_up, g_o.reshape(N_DEV, -1, g_o.shape[1]), g_down.reshape(N_DEV, -1, g_down.shape[1]),
                        _to_slices(g_a, ns_out), _to_slices(g_b, ns_out), _to_slices(g_c, ns_out), g_pool_s]
        small[l] = dict(norm1_g=dg1[0], pool_scale=dscale[0], sgu_norm_g=dgs[0], sgu_w=g_sguw,
                        sgu_b=db_exp[:, :, 0].T, conv_w=dcw[:3], norm2_g=dg2[0],
                        ffn_conv_w=jnp.concatenate([dfwg[:3], dfwv[:3]], axis=1))
    grad_x = dx[None]

    flat = [g for l in range(L) for g in big_grads[l]]
    from_sibling = _rs_sibling(flat, "rs_sibling")
    pairs = [_pair_sum(g, r, core, f"pair_sum_{i}") for i, (g, r) in enumerate(zip(flat, from_sibling))]
    from_chips = _rs_chips(pairs, "rs_chips")
    nb = len(big_grads[0])
    wmv = [(w_in, m_w_in, v_w_in), (w_up, m_w_up, v_w_up), (w_o, m_w_o, v_w_o), (w_down, m_w_down, v_w_down),
           (w_pool_out, m_w_pool_out, v_w_pool_out), (w_sgu_out, m_w_sgu_out, v_w_sgu_out),
           (w_conv_out, m_w_conv_out, v_w_conv_out), (w_pool, m_w_pool, v_w_pool)]
    big_out = []
    for k, (wk, mk, vk) in enumerate(wmv):
        shp = wk.shape
        as3 = lambda a: a.reshape(L, -1, shp[-1])
        outs = None
        for l in reversed(range(L)):
            i = l * nb + k
            outs = _adamw_big(pairs[i], from_chips[i], as3(wk), as3(mk), as3(vk), chip, l, outs, f"adamw_{k}_{l}")
        big_out.append([o.reshape(shp) for o in outs])
    big_names = ["w_in", "w_up", "w_o", "w_down", "w_pool_out", "w_sgu_out", "w_conv_out", "w_pool"]
    res = {n: o for n, o in zip(big_names, big_out)}

    snames = ["norm1_g", "pool_scale", "sgu_norm_g", "sgu_w", "sgu_b", "conv_w", "norm2_g", "ffn_conv_w"]
    sparts = [jnp.stack([small[l][n] for l in range(L)]) for n in snames] + [d_final_g[0]]
    snames = snames + ["final_g"]
    packed = _pack(sparts)
    total = _sum_slots(_all_gather([packed], "ag_small")[0], "sum_small")
    sgrads = dict(zip(snames, _unpack(total, [p.shape for p in sparts])))
    for n, width in (("conv_w", conv_w.shape[2]), ("ffn_conv_w", ffn_conv_w.shape[2])):
        sgrads[n] = lax.dynamic_slice_in_dim(sgrads[n], me * width, width, axis=2)
    sw = dict(norm1_g=(norm1_g, m_norm1_g, v_norm1_g), pool_scale=(pool_scale, m_pool_scale, v_pool_scale),
              sgu_norm_g=(sgu_norm_g, m_sgu_norm_g, v_sgu_norm_g), sgu_w=(sgu_w, m_sgu_w, v_sgu_w),
              sgu_b=(sgu_b, m_sgu_b, v_sgu_b), conv_w=(conv_w, m_conv_w, v_conv_w),
              norm2_g=(norm2_g, m_norm2_g, v_norm2_g), ffn_conv_w=(ffn_conv_w, m_ffn_conv_w, v_ffn_conv_w),
              final_g=(final_g, m_final_g, v_final_g))
    shapes = [sw[n][0].shape for n in snames]
    upd = _adamw_small(_pack([sw[n][0] for n in snames]), _pack([sgrads[n] for n in snames]),
                       _pack([sw[n][1] for n in snames]), _pack([sw[n][2] for n in snames]), "adamw_small")
    sdelta, sm, sv = (dict(zip(snames, _unpack(u, shapes))) for u in upd)
    for n in snames:
        res[n] = [sgrads[n], sdelta[n], sm[n], sv[n]]

    order = ["norm1_g", "w_in", "w_pool", "pool_scale", "sgu_norm_g", "sgu_w", "sgu_b", "conv_w", "w_pool_out",
             "w_sgu_out", "w_conv_out", "w_o", "norm2_g", "w_up", "ffn_conv_w", "w_down", "final_g"]
    return (loss, grad_x) + tuple(res[n][k] for k in range(4) for n in order)
```

```python
import functools

import jax
import jax.numpy as jnp
from jax import lax
from jax.experimental import pallas as pl
from jax.experimental.pallas import tpu as pltpu

BF16 = jnp.bfloat16
F32 = jnp.float32
EPS = 1e-6
MIB = 1024 * 1024
V7X_VMEM_BYTES = 64 * MIB
VMEM_LIMIT = 56 * MIB
LANE = 128
N_DEV = 8


def _params(sem, **kw):
    return pltpu.CompilerParams(dimension_semantics=sem, vmem_limit_bytes=VMEM_LIMIT, **kw)


def _pick(n, cands):
    for c in cands:
        if n % c == 0:
            return c
    return n


_DIMS = {"nn": (((1,), (0,)), ((), ())), "nt": (((1,), (1,)), ((), ())), "tn": (((0,), (0,)), ((), ()))}


def _matmul(a, b, *, mode, out_dtype, name, res=None, b_slices=None, out_slices=None, prev=None,
            deps=(), tm=None, tn=None, tk=None):
    soff, scnt = b_slices if b_slices is not None else (0, 1)
    ooff, ocnt = out_slices if out_slices is not None else (0, 1)
    if mode == "nn":
        M = a.shape[0]
        Kc = a.shape[1]
        ns = b.shape[-1]
        N = scnt * ns
    elif mode == "nt":
        M = a.shape[0]
        N = b.shape[-2]
        ns = b.shape[-1]
        Kc = scnt * ns
    else:
        Kc, M = a.shape
        N = b.shape[1]
        ns = N // ocnt
    tm = tm or _pick(M, (1024, 512, 256, 128))
    if tn is None:
        lim = ns if (mode == "nn" and b_slices is not None) or out_slices is not None else N
        tn = _pick(lim, (1024, 768, 512, 256, 128)) if lim > 1408 else lim
    if tk is None:
        if mode == "nt" and b_slices is not None:
            tk = ns
        elif mode == "tn":
            tk = Kc if Kc <= 4096 else _pick(Kc, (4096, 2048))
        else:
            tk = Kc if Kc <= 2816 else _pick(Kc, (2816, 2048, 1536, 1408, 1024, 512))
    nk = Kc // tk
    assert M % tm == 0 and N % tn == 0 and Kc % tk == 0, (name, M, N, Kc, tm, tn, tk)
    if mode == "nn":
        a_spec = pl.BlockSpec((tm, tk), lambda i, j, k: (i, k))
        if b_slices is None:
            b_spec = pl.BlockSpec((tk, tn), lambda i, j, k: (k, j))
        else:
            spn = ns // tn
            assert ns % tn == 0
            b_spec = pl.BlockSpec((None, tk, tn), lambda i, j, k: (soff + j // spn, k, j % spn))
    elif mode == "nt":
        a_spec = pl.BlockSpec((tm, tk), lambda i, j, k: (i, k))
        if b_slices is None:
            b_spec = pl.BlockSpec((tn, tk), lambda i, j, k: (j, k))
        else:
            spk = ns // tk
            assert ns % tk == 0
            b_spec = pl.BlockSpec((None, tn, tk), lambda i, j, k: (soff + k // spk, j, k % spk))
    else:
        a_spec = pl.BlockSpec((tk, tm), lambda i, j, k: (k, i))
        b_spec = pl.BlockSpec((tk, tn), lambda i, j, k: (k, j))
    if out_slices is None:
        o_spec = pl.BlockSpec((tm, tn), lambda i, j, k: (i, j))
        out_shape = jax.ShapeDtypeStruct((M, N), out_dtype)
    else:
        spo = ns // tn
        assert mode == "tn" and ns % tn == 0
        o_spec = pl.BlockSpec((None, tm, tn), lambda i, j, k: (ooff + j // spo, i, j % spo))
        out_shape = jax.ShapeDtypeStruct((N_DEV, M, ns), out_dtype)
    dims = _DIMS[mode]
    has_res = res is not None
    has_prev = prev is not None

    def body(*refs):
        a_ref, b_ref = refs[0], refs[1]
        res_ref = refs[2] if has_res else None
        o_ref = refs[2 + has_res + has_prev + len(deps)]
        part = lax.dot_general(a_ref[...], b_ref[...], dims, preferred_element_type=F32)

        def finish(acc):
            if has_res:
                acc = acc + res_ref[...]
            o_ref[...] = acc.astype(o_ref.dtype)

        if nk == 1:
            finish(part)
        else:
            acc_ref = refs[3 + has_res + has_prev + len(deps)]
            k = pl.program_id(2)

            @pl.when(k == 0)
            def _():
                acc_ref[...] = part

            @pl.when(jnp.logical_and(k > 0, k < nk - 1))
            def _():
                acc_ref[...] += part

            @pl.when(k == nk - 1)
            def _():
                finish(acc_ref[...] + part)

    in_specs = [a_spec, b_spec]
    args = [a, b]
    if has_res:
        in_specs.append(pl.BlockSpec((tm, tn), lambda i, j, k: (i, j)))
        args.append(res)
    aliases = {}
    if has_prev:
        aliases = {len(args): 0}
        in_specs.append(pl.BlockSpec(memory_space=pl.ANY))
        args.append(prev)
    for d in deps:
        in_specs.append(pl.BlockSpec(memory_space=pl.ANY))
        args.append(d)
    return pl.pallas_call(
        body,
        name=name,
        grid=(M // tm, N // tn, nk),
        in_specs=in_specs,
        out_specs=o_spec,
        out_shape=out_shape,
        scratch_shapes=[pltpu.VMEM((tm, tn), F32)] if nk > 1 else [],
        input_output_aliases=aliases,
        compiler_params=_params(("parallel", "parallel", "arbitrary")),
    )(*args)


HALO = 16
CH = 256


def _fill_pad(pad_ref, chunk_fn, T):
    z = jnp.zeros((HALO, pad_ref.shape[1]), F32)
    pad_ref[pl.ds(0, HALO), :] = z
    pad_ref[pl.ds(HALO + T, HALO), :] = z

    def body(c, carry):
        r0 = pl.multiple_of(c * CH, CH)
        pad_ref[pl.ds(HALO + r0, CH), :] = chunk_fn(r0)
        return carry

    lax.fori_loop(0, T // CH, body, 0)


def _ext(pad_ref, r0):
    return pad_ref[pl.ds(r0, CH + 2 * HALO), :]


def _tap(ext, o):
    if o == 0:
        return ext[HALO:HALO + CH]
    return pltpu.roll(ext, (-o) % ext.shape[0], axis=0)[HALO:HALO + CH]


def _chunks(T, fn, init=0):
    def body(c, carry):
        return fn(pl.multiple_of(c * CH, CH), carry)

    return lax.fori_loop(0, T // CH, body, init)


def _conv3(ext, w):
    return _tap(ext, -1) * w[0:1] + _tap(ext, 0) * w[1:2] + _tap(ext, 1) * w[2:3]


def _conv3_t(ext, w):
    return _tap(ext, 1) * w[0:1] + _tap(ext, 0) * w[1:2] + _tap(ext, -1) * w[2:3]


def _conv3_dw(xext, d):
    return [jnp.sum(_tap(xext, k - 1) * d, axis=0, keepdims=True) for k in range(3)]


def _sigmoid(x):
    return 1.0 / (1.0 + jnp.exp(-x))


_GELU_C = 0.7978845608028654


def _gelu(x):
    return 0.5 * x * (1.0 + jnp.tanh(_GELU_C * (x + 0.044715 * (x * x * x))))


def _gelu_grad(x):
    t = jnp.tanh(_GELU_C * (x + 0.044715 * (x * x * x)))
    return 0.5 * (1.0 + t) + 0.5 * x * (1.0 - t * t) * (_GELU_C * (1.0 + 3.0 * 0.044715 * (x * x)))


def _row_tile(T):
    return _pick(T, (256, 128))


def _rms_fwd(x, g, name):
    T, D = x.shape
    tm = _row_tile(T)

    def body(x_ref, g_ref, h_ref):
        xv = x_ref[...]
        r = lax.rsqrt(jnp.mean(xv * xv, axis=-1, keepdims=True) + EPS)
        h_ref[...] = (xv * r * g_ref[...]).astype(BF16)

    return pl.pallas_call(
        body, name=name, grid=(T // tm,),
        in_specs=[pl.BlockSpec((tm, D), lambda i: (i, 0)), pl.BlockSpec((1, D), lambda i: (0, 0))],
        out_specs=pl.BlockSpec((tm, D), lambda i: (i, 0)),
        out_shape=jax.ShapeDtypeStruct((T, D), BF16),
        compiler_params=_params(("parallel",)),
    )(x, g)


def _rms_bwd(dh, x, g, dres, name):
    T, D = x.shape
    tm = _row_tile(T)

    def body(dh_ref, x_ref, g_ref, dres_ref, dx_ref, dxb_ref, dg_ref):
        i = pl.program_id(0)
        xv = x_ref[...]
        r = lax.rsqrt(jnp.mean(xv * xv, axis=-1, keepdims=True) + EPS)
        n = xv * r
        dh_v = dh_ref[...].astype(F32)
        dn = dh_v * g_ref[...]
        dx = dres_ref[...] + r * (dn - n * jnp.mean(dn * n, axis=-1, keepdims=True))
        dx_ref[...] = dx
        dxb_ref[...] = dx.astype(BF16)
        dg = jnp.sum(dh_v * n, axis=0, keepdims=True)

        @pl.when(i == 0)
        def _():
            dg_ref[...] = dg

        @pl.when(i > 0)
        def _():
            dg_ref[...] += dg

    row = pl.BlockSpec((tm, D), lambda i: (i, 0))
    vec = pl.BlockSpec((1, D), lambda i: (0, 0))
    return pl.pallas_call(
        body, name=name, grid=(T // tm,),
        in_specs=[row, row, vec, row],
        out_specs=[row, row, vec],
        out_shape=[jax.ShapeDtypeStruct((T, D), F32), jax.ShapeDtypeStruct((T, D), BF16),
                   jax.ShapeDtypeStruct((1, D), F32)],
        compiler_params=_params(("arbitrary",)),
    )(dh, x, g, dres)


def _loss_head(x, tgt, g, name):
    T, D = x.shape
    tm = _row_tile(T)

    def body(x_ref, t_ref, g_ref, dx_ref, dxb_ref, dg_ref, l_ref):
        i = pl.program_id(0)
        xv = x_ref[...]
        gv = g_ref[...]
        r = lax.rsqrt(jnp.mean(xv * xv, axis=-1, keepdims=True) + EPS)
        n = xv * r
        e = n * gv - t_ref[...]
        dy = e * (1.0 / D)
        dn = dy * gv
        dx = r * (dn - n * jnp.mean(dn * n, axis=-1, keepdims=True))
        dx_ref[...] = dx
        dxb_ref[...] = dx.astype(BF16)
        dg = jnp.sum(dy * n, axis=0, keepdims=True)
        per_tok = jnp.mean(e * e, axis=-1, keepdims=True)
        lv = jnp.broadcast_to(0.5 * jnp.sum(per_tok, axis=0, keepdims=True), (1, LANE))

        @pl.when(i == 0)
        def _():
            dg_ref[...] = dg
            l_ref[...] = lv

        @pl.when(i > 0)
        def _():
            dg_ref[...] += dg
            l_ref[...] += lv

    row = pl.BlockSpec((tm, D), lambda i: (i, 0))
    vec = pl.BlockSpec((1, D), lambda i: (0, 0))
    return pl.pallas_call(
        body, name=name, grid=(T // tm,),
        in_specs=[row, row, vec],
        out_specs=[row, row, vec, pl.BlockSpec((1, LANE), lambda i: (0, 0))],
        out_shape=[jax.ShapeDtypeStruct((T, D), F32), jax.ShapeDtypeStruct((T, D), BF16),
                   jax.ShapeDtypeStruct((1, D), F32), jax.ShapeDtypeStruct((1, LANE), F32)],
        compiler_params=_params(("arbitrary",)),
    )(x, tgt, g)


POOL_WINDOWS = (2, 4, 8, 16)


def _pool_offsets(w):
    return range(-(w // 2), w - w // 2)


def _pool_cnt(r0, w, T, shape):
    t = r0 + lax.broadcasted_iota(jnp.int32, shape, 0)
    lo = jnp.maximum(t - w // 2, 0)
    hi = jnp.minimum(t + (w - w // 2 - 1), T - 1)
    return (hi - lo + 1).astype(F32)


def _pooled(pad_ref, r0, w, T):
    ext = _ext(pad_ref, r0)
    s = None
    for o in _pool_offsets(w):
        tap = _tap(ext, o)
        s = tap if s is None else s + tap
    cur = ext[HALO:HALO + CH]
    return s / _pool_cnt(r0, w, T, cur.shape) - cur


def _pool_fwd(P, w_pool, scale, name):
    T = P.shape[0]
    G, PG, _ = w_pool.shape

    def body(a_ref, w_ref, s_ref, o_ref, pad_ref):
        g = pl.program_id(0)
        _fill_pad(pad_ref, lambda r0: a_ref[pl.ds(r0, CH), :].astype(F32), T)
        for gi, w in enumerate(POOL_WINDOWS):
            @pl.when(g == gi)
            def _(w=w):
                def chunk(r0, carry):
                    pooled = _pooled(pad_ref, r0, w, T).astype(BF16)
                    y = jnp.dot(pooled, w_ref[...], preferred_element_type=F32) * s_ref[...]
                    o_ref[pl.ds(r0, CH), :] = y.astype(BF16)
                    return carry

                _chunks(T, chunk)

    return pl.pallas_call(
        body, name=name, grid=(G,),
        in_specs=[pl.BlockSpec((T, PG), lambda g: (0, g)),
                  pl.BlockSpec((None, PG, PG), lambda g: (g, 0, 0)),
                  pl.BlockSpec((1, PG), lambda g: (0, g))],
        out_specs=pl.BlockSpec((T, PG), lambda g: (0, g)),
        out_shape=jax.ShapeDtypeStruct((T, G * PG), BF16),
        scratch_shapes=[pltpu.VMEM((T + 2 * HALO, PG), F32)],
        compiler_params=_params(("parallel",)),
    )(P, w_pool, scale)


def _pool_bwd(P, dy, w_pool, scale, name):
    T = P.shape[0]
    G, PG, _ = w_pool.shape

    def body(a_ref, dy_ref, w_ref, s_ref, da_ref, dw_ref, ds_ref, pad_ref, gpad_ref):
        g = pl.program_id(0)
        gpad_ref[pl.ds(0, HALO), :] = jnp.zeros((HALO, PG), F32)
        gpad_ref[pl.ds(HALO + T, HALO), :] = jnp.zeros((HALO, PG), F32)
        _fill_pad(pad_ref, lambda r0: a_ref[pl.ds(r0, CH), :].astype(F32), T)
        for gi, w in enumerate(POOL_WINDOWS):
            @pl.when(g == gi)
            def _(w=w):
                def chunk1(r0, carry):
                    dw, ds = carry
                    pooled = _pooled(pad_ref, r0, w, T).astype(BF16)
                    ypre = jnp.dot(pooled, w_ref[...], preferred_element_type=F32)
                    dyv = dy_ref[pl.ds(r0, CH), :].astype(F32)
                    ds = ds + jnp.sum(dyv * ypre, axis=0, keepdims=True)
                    dyp = (dyv * s_ref[...]).astype(BF16)
                    dw = dw + lax.dot_general(pooled, dyp, _DIMS["tn"], preferred_element_type=F32)
                    dpool = lax.dot_general(dyp, w_ref[...], _DIMS["nt"], preferred_element_type=F32)
                    gpad_ref[pl.ds(HALO + r0, CH), :] = dpool / _pool_cnt(r0, w, T, dpool.shape)
                    return dw, ds

                dw, ds = _chunks(T, chunk1, (jnp.zeros((PG, PG), F32), jnp.zeros((1, PG), F32)))
                dw_ref[...] = dw
                ds_ref[...] = ds

                def chunk2(r0, carry):
                    ext = _ext(gpad_ref, r0)
                    cur = ext[HALO:HALO + CH]
                    acc = None
                    for o in _pool_offsets(w):
                        tap = _tap(ext, -o)
                        acc = tap if acc is None else acc + tap
                    da_ref[pl.ds(r0, CH), :] = (acc - cur * _pool_cnt(r0, w, T, cur.shape)).astype(BF16)
                    return carry

                _chunks(T, chunk2)

    col = pl.BlockSpec((T, PG), lambda g: (0, g))
    return pl.pallas_call(
        body, name=name, grid=(G,),
        in_specs=[col, col, pl.BlockSpec((None, PG, PG), lambda g: (g, 0, 0)), pl.BlockSpec((1, PG), lambda g: (0, g))],
        out_specs=[col, pl.BlockSpec((None, PG, PG), lambda g: (g, 0, 0)), pl.BlockSpec((1, PG), lambda g: (0, g))],
        out_shape=[jax.ShapeDtypeStruct((T, G * PG), BF16), jax.ShapeDtypeStruct((G, PG, PG), F32),
                   jax.ShapeDtypeStruct((1, G * PG), F32)],
        scratch_shapes=[pltpu.VMEM((T + 2 * HALO, PG), F32), pltpu.VMEM((T + 2 * HALO, PG), F32)],
        compiler_params=_params(("parallel",)),
    )(P, dy, w_pool, scale)


SGU_CHUNK = 128


def _sgu_common(u_ref, v_ref, gs_ref):
    up = u_ref[...].astype(F32)
    vp = v_ref[...].astype(F32)
    gv = _gelu(vp)
    rv = lax.rsqrt(jnp.mean(gv * gv, axis=-1, keepdims=True) + EPS)
    nrm = gv * rv
    return up, vp, nrm, rv, (nrm * gs_ref[...]).astype(BF16)


def _sgu_fwd(P, sgu_w, b_exp, gs, name):
    T = P.shape[0]
    G, _, SG = b_exp.shape
    DP = G * SG
    tm = _row_tile(T)

    def body(u_ref, v_ref, w_ref, b_ref, gs_ref, o_ref):
        up, _, _, _, vn = _sgu_common(u_ref, v_ref, gs_ref)
        gu = _gelu(up)
        for n in range(tm // SGU_CHUNK):
            rows = slice(n * SGU_CHUNK, (n + 1) * SGU_CHUNK)
            for g in range(G):
                cols = slice(g * SG, (g + 1) * SG)
                z = jnp.dot(w_ref[g], vn[rows, cols], preferred_element_type=F32) + b_ref[g]
                o_ref[rows, cols] = (gu[rows, cols] * z).astype(BF16)

    return pl.pallas_call(
        body, name=name, grid=(T // tm,),
        in_specs=[pl.BlockSpec((tm, DP), lambda i: (i, 1)), pl.BlockSpec((tm, DP), lambda i: (i, 2)),
                  pl.BlockSpec(sgu_w.shape, lambda i: (0, 0, 0)), pl.BlockSpec(b_exp.shape, lambda i: (0, 0, 0)),
                  pl.BlockSpec((1, DP), lambda i: (0, 0))],
        out_specs=pl.BlockSpec((tm, DP), lambda i: (i, 0)),
        out_shape=jax.ShapeDtypeStruct((T, DP), BF16),
        compiler_params=_params(("parallel",)),
    )(P, P, sgu_w, b_exp, gs)


def _sgu_bwd(P, dy, da, sgu_w, b_exp, gs, name):
    T = P.shape[0]
    G, _, SG = b_exp.shape
    DP = G * SG
    tm = _row_tile(T)

    def body(u_ref, v_ref, dy_ref, da_ref, w_ref, b_ref, gs_ref, o_ref, dw_ref, db_ref, dgs_ref, dzs_ref):
        i = pl.program_id(0)
        up, vp, nrm, rv, vn = _sgu_common(u_ref, v_ref, gs_ref)
        gu = _gelu(up)
        dyv = dy_ref[...].astype(F32)
        o_ref[:, 0:DP] = da_ref[...]

        @pl.when(i == 0)
        def _():
            dw_ref[...] = jnp.zeros(dw_ref.shape, F32)
            dzs_ref[...] = jnp.zeros(dzs_ref.shape, F32)
            dgs_ref[...] = jnp.zeros(dgs_ref.shape, F32)

        dgs = jnp.zeros((1, DP), F32)
        for n in range(tm // SGU_CHUNK):
            rows = slice(n * SGU_CHUNK, (n + 1) * SGU_CHUNK)
            dvn_parts = []
            for g in range(G):
                cols = slice(g * SG, (g + 1) * SG)
                vng = vn[rows, cols]
                z = jnp.dot(w_ref[g], vng, preferred_element_type=F32) + b_ref[g]
                dyg = dyv[rows, cols]
                du = dyg * z
                o_ref[rows, DP + g * SG:DP + (g + 1) * SG] = (du * _gelu_grad(up[rows, cols])).astype(BF16)
                dz = dyg * gu[rows, cols]
                dzb = dz.astype(BF16)
                dzs_ref[g] += dz
                dw_ref[g] += lax.dot_general(dzb, vng, _DIMS["nt"], preferred_element_type=F32)
                dvn_parts.append(lax.dot_general(w_ref[g], dzb, _DIMS["tn"], preferred_element_type=F32))
            dvn = jnp.concatenate(dvn_parts, axis=1)
            nr = nrm[rows]
            dgs = dgs + jnp.sum(dvn * nr, axis=0, keepdims=True)
            dn = dvn * gs_ref[...]
            dgv = rv[rows] * (dn - nr * jnp.mean(dn * nr, axis=-1, keepdims=True))
            o_ref[rows, 2 * DP:3 * DP] = (dgv * _gelu_grad(vp[rows])).astype(BF16)
        dgs_ref[...] += dgs

        @pl.when(i == T // tm - 1)
        def _():
            for g in range(G):
                db_ref[g] = jnp.broadcast_to(jnp.sum(dzs_ref[g], axis=1, keepdims=True), (SGU_CHUNK, SG))

    full3 = lambda a: pl.BlockSpec(a.shape, lambda i: (0, 0, 0))
    return pl.pallas_call(
        body, name=name, grid=(T // tm,),
        in_specs=[pl.BlockSpec((tm, DP), lambda i: (i, 1)), pl.BlockSpec((tm, DP), lambda i: (i, 2)),
                  pl.BlockSpec((tm, DP), lambda i: (i, 0)), pl.BlockSpec((tm, DP), lambda i: (i, 0)),
                  full3(sgu_w), full3(b_exp), pl.BlockSpec((1, DP), lambda i: (0, 0))],
        out_specs=[pl.BlockSpec((tm, 3 * DP), lambda i: (i, 0)), full3(sgu_w), full3(b_exp),
                   pl.BlockSpec((1, DP), lambda i: (0, 0))],
        out_shape=[jax.ShapeDtypeStruct((T, 6 * DP), BF16), jax.ShapeDtypeStruct(sgu_w.shape, F32),
                   jax.ShapeDtypeStruct(b_exp.shape, F32), jax.ShapeDtypeStruct((1, DP), F32)],
        scratch_shapes=[pltpu.VMEM(b_exp.shape, F32)],
        compiler_params=_params(("arbitrary",)),
    )(P, P, dy, da, sgu_w, b_exp, gs)


def _conv_fwd(P, cw, name):
    T = P.shape[0]
    DP = cw.shape[1]
    tc = 256
    nb = DP // tc

    def body(xc_ref, bg_ref, cg_ref, w_ref, o_ref, pad_ref):
        _fill_pad(pad_ref, lambda r0: cg_ref[pl.ds(r0, CH), :].astype(F32) * xc_ref[pl.ds(r0, CH), :].astype(F32), T)
        w = w_ref[...]

        def chunk(r0, carry):
            cq = _conv3(_ext(pad_ref, r0), w)
            o_ref[pl.ds(r0, CH), :] = (bg_ref[pl.ds(r0, CH), :].astype(F32) * cq).astype(BF16)
            return carry

        _chunks(T, chunk)

    col = lambda off: pl.BlockSpec((T, tc), lambda j: (0, off * nb + j))
    return pl.pallas_call(
        body, name=name, grid=(nb,),
        in_specs=[col(3), col(4), col(5), pl.BlockSpec((3, tc), lambda j: (0, j))],
        out_specs=pl.BlockSpec((T, tc), lambda j: (0, j)),
        out_shape=jax.ShapeDtypeStruct((T, DP), BF16),
        scratch_shapes=[pltpu.VMEM((T + 2 * HALO, tc), F32)],
        compiler_params=_params(("parallel",)),
    )(P, P, P, cw)


def _conv_bwd(P, dy, cw, dPl, name):
    T = P.shape[0]
    DP = cw.shape[1]
    tc = 256
    nb = DP // tc

    def body(xc_ref, bg_ref, cg_ref, dy_ref, w_ref, prev_ref, o_ref, dw_ref, qpad_ref, dpad_ref):
        del prev_ref
        seg = pl.program_id(1)
        w = w_ref[...]
        rows = lambda ref, r0: ref[pl.ds(r0, CH), :].astype(F32)
        _fill_pad(qpad_ref, lambda r0: rows(cg_ref, r0) * rows(xc_ref, r0), T)
        _fill_pad(dpad_ref, lambda r0: rows(dy_ref, r0) * rows(bg_ref, r0), T)

        @pl.when(seg == 0)
        def _():
            def chunk(r0, carry):
                dq = _conv3_t(_ext(dpad_ref, r0), w)
                o_ref[pl.ds(r0, CH), :] = (dq * rows(cg_ref, r0)).astype(BF16)
                dcq = rows(dy_ref, r0) * rows(bg_ref, r0)
                return [a + b for a, b in zip(carry, _conv3_dw(_ext(qpad_ref, r0), dcq))]

            dws = _chunks(T, chunk, [jnp.zeros((1, tc), F32)] * 3)
            dw_ref[...] = jnp.concatenate(dws + [jnp.zeros((5, tc), F32)], axis=0)

        @pl.when(seg == 1)
        def _():
            def chunk(r0, carry):
                cq = _conv3(_ext(qpad_ref, r0), w)
                o_ref[pl.ds(r0, CH), :] = (rows(dy_ref, r0) * cq).astype(BF16)
                return carry

            _chunks(T, chunk)

        @pl.when(seg == 2)
        def _():
            def chunk(r0, carry):
                dq = _conv3_t(_ext(dpad_ref, r0), w)
                o_ref[pl.ds(r0, CH), :] = (dq * rows(xc_ref, r0)).astype(BF16)
                return carry

            _chunks(T, chunk)

    col = lambda off: pl.BlockSpec((T, tc), lambda j, s: (0, off * nb + j))
    return pl.pallas_call(
        body, name=name, grid=(nb, 3),
        in_specs=[col(3), col(4), col(5), pl.BlockSpec((T, tc), lambda j, s: (0, j)),
                  pl.BlockSpec((3, tc), lambda j, s: (0, j)), pl.BlockSpec(memory_space=pl.ANY)],
        out_specs=[pl.BlockSpec((T, tc), lambda j, s: (0, (3 + s) * nb + j)), pl.BlockSpec((8, tc), lambda j, s: (0, j))],
        out_shape=[jax.ShapeDtypeStruct(dPl.shape, BF16), jax.ShapeDtypeStruct((8, DP), F32)],
        scratch_shapes=[pltpu.VMEM((T + 2 * HALO, tc), F32), pltpu.VMEM((T + 2 * HALO, tc), F32)],
        input_output_aliases={5: 0},
        compiler_params=_params(("parallel", "arbitrary")),
    )(P, P, P, dy, cw, dPl)


def _merge_fwd(P, ys, ws, name):
    T = P.shape[0]
    DP, D = ws[0].shape
    tm = _pick(T, (512, 256, 128))
    tn = _pick(D, (1024, 512, 256))
    goff = (6 * DP) // tn

    def body(ya, yb, yc, wa, wb, wc, ga, gb, gc, oa_ref, ob_ref, oc_ref, m_ref):
        m = None
        for y, w, g, o_ref in ((ya, wa, ga, oa_ref), (yb, wb, gb, ob_ref), (yc, wc, gc, oc_ref)):
            o = jnp.dot(y[...], w[...], preferred_element_type=F32)
            o_ref[...] = o.astype(BF16)
            t = _sigmoid(g[...].astype(F32)) * o
            m = t if m is None else m + t
        m_ref[...] = m.astype(BF16)

    yspec = pl.BlockSpec((tm, DP), lambda i, j: (i, 0))
    wspec = pl.BlockSpec((DP, tn), lambda i, j: (0, j))
    gspec = lambda b: pl.BlockSpec((tm, tn), lambda i, j: (i, goff + b * (D // tn) + j))
    ospec = pl.BlockSpec((tm, tn), lambda i, j: (i, j))
    return pl.pallas_call(
        body, name=name, grid=(T // tm, D // tn),
        in_specs=[yspec] * 3 + [wspec] * 3 + [gspec(0), gspec(1), gspec(2)],
        out_specs=[ospec] * 4,
        out_shape=[jax.ShapeDtypeStruct((T, D), BF16)] * 4,
        compiler_params=_params(("parallel", "parallel")),
    )(*ys, *ws, P, P, P)


def _merge_bwd(P, dM, os_, name):
    T, D = dM.shape
    tm = _row_tile(T)

    def body(dm_ref, oa, ob, oc, g_ref, da_ref, db_ref, dc_ref, dg_ref):
        dm = dm_ref[...].astype(F32)
        for b, (o_ref, d_ref) in enumerate(((oa, da_ref), (ob, db_ref), (oc, dc_ref))):
            s = _sigmoid(g_ref[:, b * D:(b + 1) * D].astype(F32))
            d_ref[...] = (dm * s).astype(BF16)
            dg_ref[:, b * D:(b + 1) * D] = (dm * o_ref[...].astype(F32) * s * (1.0 - s)).astype(BF16)

    row = pl.BlockSpec((tm, D), lambda i: (i, 0))
    return pl.pallas_call(
        body, name=name, grid=(T // tm,),
        in_specs=[row] * 4 + [pl.BlockSpec((tm, 3 * D), lambda i: (i, 1))],
        out_specs=[row] * 3 + [pl.BlockSpec((tm, 3 * D), lambda i: (i, 0))],
        out_shape=[jax.ShapeDtypeStruct((T, D), BF16)] * 3 + [jax.ShapeDtypeStruct((T, 3 * D), BF16)],
        compiler_params=_params(("parallel",)),
    )(dM, *os_, P)


def _glu_fwd(U, fw, name):
    T, F2 = U.shape
    F = F2 // 2
    tc = 256
    nb = F // tc

    def body(ug_ref, uv_ref, wg_ref, wv_ref, o_ref, gpad_ref, vpad_ref):
        _fill_pad(gpad_ref, lambda r0: ug_ref[pl.ds(r0, CH), :].astype(F32), T)
        _fill_pad(vpad_ref, lambda r0: uv_ref[pl.ds(r0, CH), :].astype(F32), T)
        wg, wv = wg_ref[...], wv_ref[...]

        def chunk(r0, carry):
            gate = _conv3(_ext(gpad_ref, r0), wg)
            val = _conv3(_ext(vpad_ref, r0), wv)
            o_ref[pl.ds(r0, CH), :] = (gate * _sigmoid(gate) * val).astype(BF16)
            return carry

        _chunks(T, chunk)

    return pl.pallas_call(
        body, name=name, grid=(nb,),
        in_specs=[pl.BlockSpec((T, tc), lambda j: (0, j)), pl.BlockSpec((T, tc), lambda j: (0, nb + j)),
                  pl.BlockSpec((3, tc), lambda j: (0, j)), pl.BlockSpec((3, tc), lambda j: (0, nb + j))],
        out_specs=pl.BlockSpec((T, tc), lambda j: (0, j)),
        out_shape=jax.ShapeDtypeStruct((T, F), BF16),
        scratch_shapes=[pltpu.VMEM((T + 2 * HALO, tc), F32)] * 2,
        compiler_params=_params(("parallel",)),
    )(U, U, fw, fw)


def _glu_bwd(U, dact, fw, name):
    T, F2 = U.shape
    F = F2 // 2
    tc = 128
    nb = F // tc

    def body(ug_ref, uv_ref, da_ref, wg_ref, wv_ref, dug_ref, duv_ref, dwg_ref, dwv_ref, gpad, vpad, dgpad, dvpad):
        _fill_pad(gpad, lambda r0: ug_ref[pl.ds(r0, CH), :].astype(F32), T)
        _fill_pad(vpad, lambda r0: uv_ref[pl.ds(r0, CH), :].astype(F32), T)
        wg, wv = wg_ref[...], wv_ref[...]

        def chunk1(r0, carry):
            gext, vext = _ext(gpad, r0), _ext(vpad, r0)
            gate = _conv3(gext, wg)
            val = _conv3(vext, wv)
            s = _sigmoid(gate)
            da = da_ref[pl.ds(r0, CH), :].astype(F32)
            dval = da * (gate * s)
            dgate = da * val * (s * (1.0 + gate * (1.0 - s)))
            dgpad[pl.ds(HALO + r0, CH), :] = dgate
            dvpad[pl.ds(HALO + r0, CH), :] = dval
            new = _conv3_dw(gext, dgate) + _conv3_dw(vext, dval)
            return [a + b for a, b in zip(carry, new)]

        z = jnp.zeros((HALO, tc), F32)
        for p in (dgpad, dvpad):
            p[pl.ds(0, HALO), :] = z
            p[pl.ds(HALO + T, HALO), :] = z
        dws = _chunks(T, chunk1, [jnp.zeros((1, tc), F32)] * 6)
        dwg_ref[...] = jnp.concatenate(dws[:3] + [jnp.zeros((5, tc), F32)], axis=0)
        dwv_ref[...] = jnp.concatenate(dws[3:] + [jnp.zeros((5, tc), F32)], axis=0)

        def chunk2(r0, carry):
            dug_ref[pl.ds(r0, CH), :] = _conv3_t(_ext(dgpad, r0), wg).astype(BF16)
            duv_ref[pl.ds(r0, CH), :] = _conv3_t(_ext(dvpad, r0), wv).astype(BF16)
            return carry

        _chunks(T, chunk2)

    lo = pl.BlockSpec((T, tc), lambda j: (0, j))
    hi = pl.BlockSpec((T, tc), lambda j: (0, nb + j))
    wlo = pl.BlockSpec((3, tc), lambda j: (0, j))
    whi = pl.BlockSpec((3, tc), lambda j: (0, nb + j))
    dwspec = pl.BlockSpec((8, tc), lambda j: (0, j))
    return pl.pallas_call(
        body, name=name, grid=(nb,),
        in_specs=[lo, hi, lo, wlo, whi],
        out_specs=[lo, lo, dwspec, dwspec],
        out_shape=[jax.ShapeDtypeStruct((T, F), BF16)] * 2 + [jax.ShapeDtypeStruct((8, F), F32)] * 2,
        scratch_shapes=[pltpu.VMEM((T + 2 * HALO, tc), F32)] * 4,
        compiler_params=_params(("parallel",)),
    )(U, U, dact, fw, fw)


_ANY = pl.BlockSpec(memory_space=pl.ANY)
_MESH = pl.DeviceIdType.MESH


def _mesh_pos():
    return lax.axis_index("x"), lax.axis_index("y"), lax.axis_index("c")


def _other_chips(x, y):
    return [(1 - x, y), (x, 1 - y), (1 - x, 1 - y)]


def _all_gather(shards, name):
    n = len(shards)

    def body(*refs):
        ins, outs = refs[:n], refs[n:2 * n]
        send_sems, recv_sems, local_sems = refs[2 * n:]
        x, y, c = _mesh_pos()
        me, sibling = (x, y, c), (x, y, 1 - c)
        chips = _other_chips(x, y)

        def copy(a, k, block, to, src=None):
            px, py, pc = block
            dst = outs[a].at[4 * px + 2 * py + pc]
            return pltpu.make_async_remote_copy(
                src_ref=dst if src is None else src, dst_ref=dst,
                send_sem=send_sems.at[a * 7 + k], recv_sem=recv_sems.at[a * 7 + k],
                device_id=to, device_id_type=_MESH)

        mine = [pltpu.make_async_copy(ins[a], outs[a].at[4 * x + 2 * y + c], local_sems.at[a]) for a in range(n)]
        for cp in mine:
            cp.start()
        first = []
        for a in range(n):
            first.append(copy(a, 0, me, sibling, src=ins[a]))
            first += [copy(a, 1 + j, me, (*chip, c), src=ins[a]) for j, chip in enumerate(chips)]
        for cp in first:
            cp.start()
        passed = []
        for a in range(n):
            for j, chip in enumerate(chips):
                copy(a, 1 + j, (*chip, c), me).wait_recv()
                fwd = copy(a, 4 + j, (*chip, c), sibling)
                fwd.start()
                passed.append(fwd)
        for a in range(n):
            copy(a, 0, sibling, me).wait_recv()
            for j, chip in enumerate(chips):
                copy(a, 4 + j, (*chip, 1 - c), me).wait_recv()
        for cp in first + passed:
            cp.wait_send()
        for cp in mine:
            cp.wait()

    return pl.pallas_call(
        body, name=name,
        in_specs=[_ANY] * n, out_specs=[_ANY] * n,
        out_shape=[jax.ShapeDtypeStruct((N_DEV,) + s.shape, s.dtype) for s in shards],
        scratch_shapes=[pltpu.SemaphoreType.DMA((7 * n,)), pltpu.SemaphoreType.DMA((7 * n,)),
                        pltpu.SemaphoreType.DMA((n,))],
    )(*shards)


def _rs_sibling(grads, name):
    n = len(grads)

    def body(*refs):
        ins, outs = refs[:n], refs[n:2 * n]
        send_sems, recv_sems = refs[2 * n:]
        x, y, c = _mesh_pos()
        cps = []
        for a in range(n):
            for q in range(4):
                cps.append(pltpu.make_async_remote_copy(
                    src_ref=ins[a].at[2 * q + (1 - c)], dst_ref=outs[a].at[q],
                    send_sem=send_sems.at[4 * a + q], recv_sem=recv_sems.at[4 * a + q],
                    device_id=(x, y, 1 - c), device_id_type=_MESH))
        for cp in cps:
            cp.start()
        for cp in cps:
            cp.wait()

    return pl.pallas_call(
        body, name=name,
        in_specs=[_ANY] * n, out_specs=[_ANY] * n,
        out_shape=[jax.ShapeDtypeStruct((4,) + g.shape[1:], g.dtype) for g in grads],
        scratch_shapes=[pltpu.SemaphoreType.DMA((4 * n,)), pltpu.SemaphoreType.DMA((4 * n,))],
    )(*grads)


def _rs_chips(pairs, name):
    n = len(pairs)

    def body(*refs):
        ins, outs = refs[:n], refs[n:2 * n]
        send_sems, recv_sems = refs[2 * n:]
        x, y, c = _mesh_pos()
        cps = []
        for a in range(n):
            for j, (px, py) in enumerate(_other_chips(x, y)):
                cps.append(pltpu.make_async_remote_copy(
                    src_ref=ins[a].at[2 * px + py], dst_ref=outs[a].at[j],
                    send_sem=send_sems.at[3 * a + j], recv_sem=recv_sems.at[3 * a + j],
                    device_id=(px, py, c), device_id_type=_MESH))
        for cp in cps:
            cp.start()
        for cp in cps:
            cp.wait()

    return pl.pallas_call(
        body, name=name,
        in_specs=[_ANY] * n, out_specs=[_ANY] * n,
        out_shape=[jax.ShapeDtypeStruct((3,) + p.shape[1:], p.dtype) for p in pairs],
        scratch_shapes=[pltpu.SemaphoreType.DMA((3 * n,)), pltpu.SemaphoreType.DMA((3 * n,))],
    )(*pairs)


_HBM = pl.BlockSpec(memory_space=pltpu.HBM)
_SEM = pl.BlockSpec(memory_space=pltpu.SEMAPHORE)
_EFFECT = pltpu.SideEffectType.DATAFLOW_SIDE_EFFECTING


def _in_hbm(a):
    return pltpu.with_memory_space_constraint(a, pltpu.HBM)


def _split_start(groups, name):
    sizes = [len(bufs) for bufs, _, _ in groups]
    nb = sum(sizes)
    ng = len(groups)

    def body(*refs):
        ins = refs[:nb]
        sems = refs[nb:nb + 2 * ng]
        token = refs[-1]
        off = 0
        for gi, (bufs, make_copies, _) in enumerate(groups):
            for cp in make_copies(ins[off:off + len(bufs)], sems[2 * gi], sems[2 * gi + 1]):
                cp.start()
            off += len(bufs)
        token[...] = jnp.zeros(token.shape, token.dtype)

    flat = [b for bufs, _, _ in groups for b in bufs]
    sem_shapes = []
    for _, _, n in groups:
        sem_shapes += [pltpu.SemaphoreType.DMA((n,)), pltpu.SemaphoreType.DMA((n,))]
    outs = pl.pallas_call(
        body, name=name,
        in_specs=[_HBM] * nb,
        out_specs=[_SEM] * (2 * ng) + [_HBM] * nb + [pl.BlockSpec(memory_space=pltpu.VMEM)],
        out_shape=sem_shapes + [pltpu.HBM(b.shape, b.dtype) for b in flat] + [jax.ShapeDtypeStruct((8, LANE), F32)],
        input_output_aliases={i: 2 * ng + i for i in range(nb)},
        compiler_params=pltpu.CompilerParams(has_side_effects=_EFFECT),
    )(*[_in_hbm(b) for b in flat])
    res, off = [], 2 * ng
    for gi, n in enumerate(sizes):
        res.append((outs[2 * gi], outs[2 * gi + 1], list(outs[off:off + n])))
        off += n
    return res, outs[-1]


def _split_wait(started, make_copies, after, name):
    send, recv, bufs = started
    nb = len(bufs)

    def body(*refs):
        for cp in make_copies(refs[:nb], refs[nb], refs[nb + 1]):
            cp.wait_send()
            cp.wait_recv()

    outs = pl.pallas_call(
        body, name=name,
        in_specs=[_HBM] * nb + [_SEM, _SEM, _ANY],
        out_specs=[_HBM] * nb,
        out_shape=[pltpu.HBM(b.shape, b.dtype) for b in bufs],
        input_output_aliases={i: i for i in range(nb)},
        compiler_params=pltpu.CompilerParams(has_side_effects=_EFFECT),
    )(*bufs, send, recv, after)
    return list(outs)


def _copies_gather_chips(nbuf):
    def make(bufs, send, recv):
        x, y, c = _mesh_pos()
        targets = [(x, y, 1 - c)] + [(px, py, c) for px, py in _other_chips(x, y)]
        cps = []
        for a in range(nbuf):
            mine = bufs[a].at[4 * x + 2 * y + c]
            for k, to in enumerate(targets):
                cps.append(pltpu.make_async_remote_copy(src_ref=mine, dst_ref=mine, send_sem=send.at[4 * a + k],
                                                        recv_sem=recv.at[4 * a + k], device_id=to, device_id_type=_MESH))
        return cps

    return make, 4 * nbuf


def _copies_gather_forward(nbuf):
    def make(bufs, send, recv):
        x, y, c = _mesh_pos()
        cps = []
        for a in range(nbuf):
            for j, (px, py) in enumerate(_other_chips(x, y)):
                slot = bufs[a].at[4 * px + 2 * py + c]
                cps.append(pltpu.make_async_remote_copy(src_ref=slot, dst_ref=slot, send_sem=send.at[3 * a + j],
                                                        recv_sem=recv.at[3 * a + j], device_id=(x, y, 1 - c),
                                                        device_id_type=_MESH))
        return cps

    return make, 3 * nbuf


def _copies_rs_sibling(n):
    def make(bufs, send, recv):
        x, y, c = _mesh_pos()
        cps = []
        for a in range(n):
            for q in range(4):
                cps.append(pltpu.make_async_remote_copy(
                    src_ref=bufs[a].at[2 * q + (1 - c)], dst_ref=bufs[n + a].at[q], send_sem=send.at[4 * a + q],
                    recv_sem=recv.at[4 * a + q], device_id=(x, y, 1 - c), device_id_type=_MESH))
        return cps

    return make, 4 * n


def _copies_rs_chips(n):
    def make(bufs, send, recv):
        x, y, c = _mesh_pos()
        cps = []
        for a in range(n):
            for j, (px, py) in enumerate(_other_chips(x, y)):
                cps.append(pltpu.make_async_remote_copy(
                    src_ref=bufs[a].at[2 * px + py], dst_ref=bufs[n + a].at[j], send_sem=send.at[3 * a + j],
                    recv_sem=recv.at[3 * a + j], device_id=(px, py, c), device_id_type=_MESH))
        return cps

    return make, 3 * n


def _place_own(shards, name):
    n = len(shards)

    def body(*refs):
        ins, outs, sems = refs[:n], refs[n:2 * n], refs[2 * n]
        x, y, c = _mesh_pos()
        cps = [pltpu.make_async_copy(ins[a], outs[a].at[4 * x + 2 * y + c], sems.at[a]) for a in range(n)]
        for cp in cps:
            cp.start()
        for cp in cps:
            cp.wait()

    return pl.pallas_call(
        body, name=name,
        in_specs=[_ANY] * n, out_specs=[_ANY] * n,
        out_shape=[jax.ShapeDtypeStruct((N_DEV,) + s.shape, s.dtype) for s in shards],
        scratch_shapes=[pltpu.SemaphoreType.DMA((n,))],
    )(*shards)


def _rows_tile(R, C, budget=2 * MIB):
    if R * C * 4 <= budget:
        return R
    for t in (2048, 1024, 704, 512, 352, 256, 128, 64, 32, 16, 8):
        if R % t == 0 and t * C * 4 <= budget:
            return t
    return R


def _pair_sum(grad, recv, core, name):
    _, R, C = grad.shape
    tr = _rows_tile(R, C)

    def body(c_ref, g_ref, r_ref, o_ref):
        del c_ref
        o_ref[...] = (g_ref[...].astype(F32) + r_ref[...].astype(F32)).astype(BF16)

    return pl.pallas_call(
        body, name=name,
        grid_spec=pltpu.PrefetchScalarGridSpec(
            num_scalar_prefetch=1, grid=(4, R // tr),
            in_specs=[pl.BlockSpec((None, tr, C), lambda q, i, c: (2 * q + c[0], i, 0)),
                      pl.BlockSpec((None, tr, C), lambda q, i, c: (q, i, 0))],
            out_specs=pl.BlockSpec((None, tr, C), lambda q, i, c: (q, i, 0))),
        out_shape=jax.ShapeDtypeStruct((4, R, C), BF16),
        compiler_params=_params(("parallel", "parallel")),
    )(core, grad, recv)


ADAM_LR = 0.001
ADAM_B1 = 0.9
ADAM_B2 = 0.999
ADAM_EPS = 1e-08
ADAM_WD = 0.01
ADAM_STEP = 10


def _adamw(w, g, m, v):
    m = ADAM_B1 * m + (1.0 - ADAM_B1) * g
    v = ADAM_B2 * v + (1.0 - ADAM_B2) * (g * g)
    m_hat = m / (1.0 - ADAM_B1 ** ADAM_STEP)
    v_hat = v / (1.0 - ADAM_B2 ** ADAM_STEP)
    delta = -ADAM_LR * (m_hat / (jnp.sqrt(v_hat) + ADAM_EPS) + ADAM_WD * w)
    return delta, m, v


def _adamw_big(pair, recv, w, m, v, chip, layer, prev, name):
    L, R, C = w.shape
    tr = _rows_tile(R, C, MIB)
    has_prev = prev is not None

    def body(chip_ref, p_ref, r0_ref, r1_ref, r2_ref, w_ref, m_ref, v_ref, *rest):
        del chip_ref
        g_ref, d_ref, mo_ref, vo_ref = rest[-4:]
        g = ((p_ref[...].astype(F32) + r0_ref[...].astype(F32)) + r1_ref[...].astype(F32)) + r2_ref[...].astype(F32)
        delta, m2, v2 = _adamw(w_ref[...], g, m_ref[...], v_ref[...])
        g_ref[...] = g
        d_ref[...] = delta
        mo_ref[...] = m2
        vo_ref[...] = v2

    lay = pl.BlockSpec((None, tr, C), lambda i, c: (layer, i, 0))
    rspec = lambda j: pl.BlockSpec((None, tr, C), lambda i, c: (j, i, 0))
    in_specs = [pl.BlockSpec((None, tr, C), lambda i, c: (c[0], i, 0)), rspec(0), rspec(1), rspec(2), lay, lay, lay]
    args = [chip, pair, recv, recv, recv, w, m, v]
    aliases = {}
    if has_prev:
        in_specs += [_ANY] * 4
        args += list(prev)
        aliases = {8 + k: k for k in range(4)}
    return pl.pallas_call(
        body, name=name,
        grid_spec=pltpu.PrefetchScalarGridSpec(
            num_scalar_prefetch=1, grid=(R // tr,), in_specs=in_specs, out_specs=[lay] * 4),
        out_shape=[jax.ShapeDtypeStruct((L, R, C), F32)] * 4,
        input_output_aliases=aliases,
        compiler_params=_params(("parallel",)),
    )(*args)


def _sum_slots(gathered, name):
    _, R, C = gathered.shape
    tr = _rows_tile(R, C, 2 * MIB)

    def body(x_ref, o_ref):
        s = x_ref[0]
        for k in range(1, N_DEV):
            s = s + x_ref[k]
        o_ref[...] = s

    return pl.pallas_call(
        body, name=name, grid=(R // tr,),
        in_specs=[pl.BlockSpec((N_DEV, tr, C), lambda i: (0, i, 0))],
        out_specs=pl.BlockSpec((tr, C), lambda i: (i, 0)),
        out_shape=jax.ShapeDtypeStruct((R, C), F32),
        compiler_params=_params(("parallel",)),
    )(gathered)


def _adamw_small(w, g, m, v, name):
    R, C = w.shape
    tr = _rows_tile(R, C, 2 * MIB)

    def body(w_ref, g_ref, m_ref, v_ref, d_ref, mo_ref, vo_ref):
        d_ref[...], mo_ref[...], vo_ref[...] = _adamw(w_ref[...], g_ref[...], m_ref[...], v_ref[...])

    spec = pl.BlockSpec((tr, C), lambda i: (i, 0))
    return pl.pallas_call(
        body, name=name, grid=(R // tr,),
        in_specs=[spec] * 4, out_specs=[spec] * 3,
        out_shape=[jax.ShapeDtypeStruct((R, C), F32)] * 3,
        compiler_params=_params(("parallel",)),
    )(w, g, m, v)


def _pack(arrs):
    parts = []
    for a in arrs:
        flat = a.reshape(-1)
        pad = (-flat.shape[0]) % (8 * LANE)
        if pad:
            flat = jnp.pad(flat, (0, pad))
        parts.append(flat.reshape(-1, LANE))
    return jnp.concatenate(parts, axis=0)


def _unpack(packed, shapes):
    out, r = [], 0
    for s in shapes:
        n = 1
        for d in s:
            n *= d
        rows = -(-n // (8 * LANE)) * 8
        out.append(packed[r:r + rows].reshape(-1)[:n].reshape(s))
        r += rows
    return out


def _to_slices(a, ns):
    K = a.shape[0]
    return a.reshape(K, N_DEV, ns).transpose(1, 0, 2)


def _from_slices(a):
    _, K, ns = a.shape
    return a.transpose(1, 0, 2).reshape(K, N_DEV * ns)


def kernel(x, norm1_g, w_in, w_pool, pool_scale, sgu_norm_g, sgu_w, sgu_b, conv_w, w_pool_out, w_sgu_out, w_conv_out, w_o, norm2_g, w_up, ffn_conv_w, w_down, final_g, loss_target, m_norm1_g, m_w_in, m_w_pool, m_pool_scale, m_sgu_norm_g, m_sgu_w, m_sgu_b, m_conv_w, m_w_pool_out, m_w_sgu_out, m_w_conv_out, m_w_o, m_norm2_g, m_w_up, m_ffn_conv_w, m_w_down, m_final_g, v_norm1_g, v_w_in, v_w_pool, v_pool_scale, v_sgu_norm_g, v_sgu_w, v_sgu_b, v_conv_w, v_w_pool_out, v_w_sgu_out, v_w_conv_out, v_w_o, v_norm2_g, v_w_up, v_ffn_conv_w, v_w_down, v_final_g):
    L = norm1_g.shape[0]
    x0 = x[0]
    tgt = loss_target[0]
    G, PG = w_pool.shape[1], w_pool.shape[3]
    F = w_down.shape[1] * N_DEV
    px, py, pc = _mesh_pos()
    me = 4 * px + 2 * py + pc
    core = jnp.reshape(pc, (1,)).astype(jnp.int32)
    chip = jnp.reshape(2 * px + py, (1,)).astype(jnp.int32)

    shards, group_sizes = [], []
    for l in range(L):
        bf = lambda w: w[l].astype(BF16)
        grp = [[bf(w_in)],
               [bf(w_pool_out), bf(w_sgu_out), bf(w_conv_out),
                w_pool[l].reshape(G * w_pool.shape[2], PG).astype(BF16), conv_w[l], bf(w_o)],
               [bf(w_up), ffn_conv_w[l]],
               [bf(w_down)]]
        for gshards in grp:
            shards += gshards
            group_sizes.append(len(gshards))
    placed = _place_own(shards, "ag_place")
    stage1, off = [], 0
    for n in group_sizes:
        stage1.append((placed[off:off + n],) + _copies_gather_chips(n))
        off += n
    started1, tok0 = _split_start(stage1, "ag_start")
    started2 = {}

    def gather_forward(k, after):
        n = group_sizes[k]
        bufs = _split_wait(started1[k], stage1[k][1], after, f"ag_wait1_{k}")
        st, tok = _split_start([(bufs,) + _copies_gather_forward(n)], f"ag_fwd_{k}")
        started2[k] = st[0]
        return tok

    def gather_done(k, after):
        return _split_wait(started2.pop(k), _copies_gather_forward(group_sizes[k])[0], after, f"ag_wait2_{k}")

    W, saved = [], []
    xc = x0
    tok = gather_forward(0, tok0)
    for l in range(L):
        w = dict(
            sguw=sgu_w[l].astype(BF16),
            bexp=jnp.broadcast_to(sgu_b[l].T[:, :, None],
                                  (sgu_b.shape[2], sgu_b.shape[1], sgu_norm_g.shape[1] // sgu_b.shape[2])),
            g1=norm1_g[l][None], g2=norm2_g[l][None], gs=sgu_norm_g[l][None], scale=pool_scale[l][None])
        ka, kb, kc, kd = 4 * l, 4 * l + 1, 4 * l + 2, 4 * l + 3
        h1 = _rms_fwd(xc, w["g1"], f"rms1_fwd_{l}")
        (w["win3"],) = gather_done(ka, h1)
        if l > 0:
            tok = gather_forward(kb, h1)
        P = _matmul(h1, w["win3"], mode="nn", out_dtype=BF16, name=f"proj_in_{l}", b_slices=(0, N_DEV), deps=[tok])
        if l == 0:
            gather_forward(kb, P)
        wa, wb, wc, wp, cw, wo = gather_done(kb, P)
        w.update(wa=_from_slices(wa), wb=_from_slices(wb), wc=_from_slices(wc), cw=_from_slices(cw),
                 wpool=wp.reshape(N_DEV, G, -1, PG).transpose(1, 0, 2, 3).reshape(G, PG, PG),
                 wo=wo.reshape(-1, wo.shape[2]))
        ya = _pool_fwd(P, w["wpool"], w["scale"], f"pool_fwd_{l}")
        yb = _sgu_fwd(P, w["sguw"], w["bexp"], w["gs"], f"sgu_fwd_{l}")
        yc = _conv_fwd(P, w["cw"], f"conv_fwd_{l}")
        oa, ob, oc, M = _merge_fwd(P, (ya, yb, yc), (w["wa"], w["wb"], w["wc"]), f"merge_fwd_{l}")
        tok = gather_forward(kc, M)
        x1 = _matmul(M, w["wo"], mode="nn", out_dtype=F32, name=f"proj_o_{l}", res=xc, deps=[tok])
        h2 = _rms_fwd(x1, w["g2"], f"rms2_fwd_{l}")
        w["wup3"], fw = gather_done(kc, h2)
        w["fw"] = _from_slices(fw)
        tok = gather_forward(kd, h2)
        U = _matmul(h2, w["wup3"], mode="nn", out_dtype=BF16, name=f"proj_up_{l}", b_slices=(0, N_DEV), deps=[tok])
        act = _glu_fwd(U, w["fw"], f"glu_fwd_{l}")
        (wdown,) = gather_done(kd, act)
        w["wdown"] = wdown.reshape(-1, wdown.shape[2])
        deps = [gather_forward(4 * (l + 1), act)] if l + 1 < L else []
        x2 = _matmul(act, w["wdown"], mode="nn", out_dtype=F32, name=f"proj_down_{l}", res=x1, deps=deps,
                     tm=_pick(x1.shape[0], (512, 256)))
        if deps:
            tok = deps[0]
        W.append(w)
        saved.append(dict(x0=xc, h1=h1, P=P, ya=ya, yb=yb, yc=yc, oa=oa, ob=ob, oc=oc, M=M, x1=x1, h2=h2, U=U, act=act))
        xc = x2

    dx, dxb, d_final_g, loss_part = _loss_head(xc, tgt, final_g[None], "loss_head")
    loss = lax.psum(loss_part[0, 0], ("x", "y", "c"))

    wmv = dict(w_in=(w_in, m_w_in, v_w_in), w_up=(w_up, m_w_up, v_w_up), w_o=(w_o, m_w_o, v_w_o),
               w_down=(w_down, m_w_down, v_w_down), w_pool_out=(w_pool_out, m_w_pool_out, v_w_pool_out),
               w_sgu_out=(w_sgu_out, m_w_sgu_out, v_w_sgu_out), w_conv_out=(w_conv_out, m_w_conv_out, v_w_conv_out),
               w_pool=(w_pool, m_w_pool, v_w_pool))
    adam_out = {}

    def rs_begin(tag, layer, names, grads):
        n = len(grads)
        lands = [lax.empty((4,) + g.shape[1:], BF16) for g in grads]
        st, tok = _split_start([(list(grads) + lands,) + _copies_rs_sibling(n)], f"rs_sib_{tag}")
        return dict(tag=tag, layer=layer, names=names, n=n, st=st[0]), tok

    def rs_to_chips(state, after):
        n, tag = state["n"], state["tag"]
        bufs = _split_wait(state["st"], _copies_rs_sibling(n)[0], after, f"rs_sibw_{tag}")
        pairs = [_pair_sum(g, r, core, f"pair_{tag}_{i}") for i, (g, r) in enumerate(zip(bufs[:n], bufs[n:]))]
        lands = [lax.empty((3,) + p.shape[1:], BF16) for p in pairs]
        st, tok = _split_start([(pairs + lands,) + _copies_rs_chips(n)], f"rs_chips_{tag}")
        state["st"] = st[0]
        return tok

    def rs_finish(state, after):
        n, tag, layer = state["n"], state["tag"], state["layer"]
        bufs = _split_wait(state["st"], _copies_rs_chips(n)[0], after, f"rs_chipsw_{tag}")
        for name, pair, recv in zip(state["names"], bufs[:n], bufs[n:]):
            wk, mk, vk = (a.reshape(L, -1, a.shape[-1]) for a in wmv[name])
            adam_out[name] = _adamw_big(pair, recv, wk, mk, vk, chip, layer, adam_out.get(name), f"adamw_{name}_{layer}")

    small = [None] * L
    pending = []
    deps = []
    ns_out = w_pool_out.shape[2]
    for l in reversed(range(L)):
        w, s = W[l], saved[l]
        dact = _matmul(dxb, w["wdown"], mode="nt", out_dtype=BF16, name=f"d_act_{l}", tn=_pick(F, (512, 256)), deps=deps)
        g_down = _matmul(s["act"], dxb, mode="tn", out_dtype=BF16, name=f"g_down_{l}", tm=_pick(F, (512, 256)))
        dug, duv, dfwg, dfwv = _glu_bwd(s["U"], dact, w["fw"], f"glu_bwd_{l}")
        dh2 = _matmul(dug, w["wup3"], mode="nt", out_dtype=F32, name=f"d_h2a_{l}", b_slices=(0, 4))
        dh2 = _matmul(duv, w["wup3"], mode="nt", out_dtype=F32, name=f"d_h2b_{l}", b_slices=(4, 4), res=dh2)
        g_up = _matmul(s["h2"], dug, mode="tn", out_dtype=BF16, name=f"g_upa_{l}", out_slices=(0, 4), tm=512)
        g_up = _matmul(s["h2"], duv, mode="tn", out_dtype=BF16, name=f"g_upb_{l}", out_slices=(4, 4), prev=g_up, tm=512)
        ra, tok = rs_begin(f"a{l}", l, ["w_down", "w_up"], [g_down.reshape(N_DEV, -1, g_down.shape[1]), g_up])
        if pending:
            rs_finish(pending.pop(0), g_up)
        dx1, dx1b, dg2 = _rms_bwd(dh2, s["x1"], w["g2"], dx, f"rms2_bwd_{l}")
        dM = _matmul(dx1b, w["wo"], mode="nt", out_dtype=BF16, name=f"d_m_{l}", deps=[tok])
        g_o = _matmul(s["M"], dx1b, mode="tn", out_dtype=BF16, name=f"g_o_{l}")
        doa, dob, doc, dPg = _merge_bwd(s["P"], dM, (s["oa"], s["ob"], s["oc"]), f"merge_bwd_{l}")
        tok = rs_to_chips(ra, dPg)
        if pending:
            rs_finish(pending.pop(0), dPg)
        dya = _matmul(doa, w["wa"], mode="nt", out_dtype=BF16, name=f"d_ya_{l}", deps=[tok])
        dyb = _matmul(dob, w["wb"], mode="nt", out_dtype=BF16, name=f"d_yb_{l}")
        dyc = _matmul(doc, w["wc"], mode="nt", out_dtype=BF16, name=f"d_yc_{l}")
        g_a = _matmul(s["ya"], doa, mode="tn", out_dtype=BF16, name=f"g_a_{l}")
        g_b = _matmul(s["yb"], dob, mode="tn", out_dtype=BF16, name=f"g_b_{l}")
        g_c = _matmul(s["yc"], doc, mode="tn", out_dtype=BF16, name=f"g_c_{l}")
        da, g_pool, dscale = _pool_bwd(s["P"], dya, w["wpool"], w["scale"], f"pool_bwd_{l}")
        g_pool_s = g_pool.reshape(G, N_DEV, -1, PG).transpose(1, 0, 2, 3).reshape(N_DEV, -1, PG).astype(BF16)
        rb, tok = rs_begin(f"b{l}", l, ["w_o", "w_pool_out", "w_sgu_out", "w_conv_out", "w_pool"],
                           [g_o.reshape(N_DEV, -1, g_o.shape[1]), _to_slices(g_a, ns_out), _to_slices(g_b, ns_out),
                            _to_slices(g_c, ns_out), g_pool_s])
        if pending:
            rs_finish(pending.pop(0), da)
        dPl, g_sguw, db_exp, dgs = _sgu_bwd(s["P"], dyb, da, w["sguw"], w["bexp"], w["gs"], f"sgu_bwd_{l}")
        dPl, dcw = _conv_bwd(s["P"], dyc, w["cw"], dPl, f"conv_bwd_{l}")
        dh1 = _matmul(dPl, w["win3"], mode="nt", out_dtype=F32, name=f"d_h1a_{l}", b_slices=(0, 4), deps=[tok])
        dh1 = _matmul(dPg, w["win3"], mode="nt", out_dtype=F32, name=f"d_h1b_{l}", b_slices=(4, 4), res=dh1)
        tok = rs_to_chips(rb, dh1)
        g_in = _matmul(s["h1"], dPl, mode="tn", out_dtype=BF16, name=f"g_ina_{l}", out_slices=(0, 4), deps=[tok])
        g_in = _matmul(s["h1"], dPg, mode="tn", out_dtype=BF16, name=f"g_inb_{l}", out_slices=(4, 4), prev=g_in)
        rc, _ = rs_begin(f"c{l}", l, ["w_in"], [g_in])
        dx, dxb, dg1 = _rms_bwd(dh1, s["x0"], w["g1"], dx1, f"rms1_bwd_{l}")
        deps = [rs_to_chips(rc, dx)]
        pending = [ra, rb, rc]
        small[l] = dict(norm1_g=dg1[0], pool_scale=dscale[0], sgu_norm_g=dgs[0], sgu_w=g_sguw,
                        sgu_b=db_exp[:, :, 0].T, conv_w=dcw[:3], norm2_g=dg2[0],
                        ffn_conv_w=jnp.concatenate([dfwg[:3], dfwv[:3]], axis=1))
    grad_x = dx[None]

    snames = ["norm1_g", "pool_scale", "sgu_norm_g", "sgu_w", "sgu_b", "conv_w", "norm2_g", "ffn_conv_w"]
    sparts = [jnp.stack([small[l][n] for l in range(L)]) for n in snames] + [d_final_g[0]]
    snames = snames + ["final_g"]
    packed = _pack(sparts)
    total = _sum_slots(_all_gather([packed], "ag_small")[0], "sum_small")
    sgrads = dict(zip(snames, _unpack(total, [p.shape for p in sparts])))
    for n, width in (("conv_w", conv_w.shape[2]), ("ffn_conv_w", ffn_conv_w.shape[2])):
        sgrads[n] = lax.dynamic_slice_in_dim(sgrads[n], me * width, width, axis=2)
    sw = dict(norm1_g=(norm1_g, m_norm1_g, v_norm1_g), pool_scale=(pool_scale, m_pool_scale, v_pool_scale),
              sgu_norm_g=(sgu_norm_g, m_sgu_norm_g, v_sgu_norm_g), sgu_w=(sgu_w, m_sgu_w, v_sgu_w),
              sgu_b=(sgu_b, m_sgu_b, v_sgu_b), conv_w=(conv_w, m_conv_w, v_conv_w),
              norm2_g=(norm2_g, m_norm2_g, v_norm2_g), ffn_conv_w=(ffn_conv_w, m_ffn_conv_w, v_ffn_conv_w),
              final_g=(final_g, m_final_g, v_final_g))
    shapes = [sw[n][0].shape for n in snames]
    upd = _adamw_small(_pack([sw[n][0] for n in snames]), _pack([sgrads[n] for n in snames]),
                       _pack([sw[n][1] for n in snames]), _pack([sw[n][2] for n in snames]), "adamw_small")
    sdelta, sm, sv = (dict(zip(snames, _unpack(u, shapes))) for u in upd)
    res = {n: [sgrads[n], sdelta[n], sm[n], sv[n]] for n in snames}

    rs_finish(pending[0], total)
    rs_finish(pending[1], upd[0])
    rs_finish(pending[2], upd[0])
    for n, outs in adam_out.items():
        res[n] = [o.reshape(wmv[n][0].shape) for o in outs]

    order = ["norm1_g", "w_in", "w_pool", "pool_scale", "sgu_norm_g", "sgu_w", "sgu_b", "conv_w", "w_pool_out",
             "w_sgu_out", "w_conv_out", "w_o", "norm2_g", "w_up", "ffn_conv_w", "w_down", "final_g"]
    return (loss, grad_x) + tuple(res[n][k] for k in range(4) for n in order)
```

```python
import functools

import jax
import jax.numpy as jnp
from jax import lax
from jax.experimental import pallas as pl
from jax.experimental.pallas import tpu as pltpu

BF16 = jnp.bfloat16
F32 = jnp.float32
EPS = 1e-6
MIB = 1024 * 1024
V7X_VMEM_BYTES = 64 * MIB
VMEM_LIMIT = 56 * MIB
LANE = 128
N_DEV = 8


def _params(sem, **kw):
    return pltpu.CompilerParams(dimension_semantics=sem, vmem_limit_bytes=VMEM_LIMIT, **kw)


def _pick(n, cands):
    for c in cands:
        if n % c == 0:
            return c
    return n


_DIMS = {"nn": (((1,), (0,)), ((), ())), "nt": (((1,), (1,)), ((), ())), "tn": (((0,), (0,)), ((), ()))}


def _matmul(a, b, *, mode, out_dtype, name, res=None, b_slices=None, out_slices=None, prev=None,
            deps=(), tm=None, tn=None, tk=None):
    soff, scnt = b_slices if b_slices is not None else (0, 1)
    ooff, ocnt = out_slices if out_slices is not None else (0, 1)
    if mode == "nn":
        M = a.shape[0]
        Kc = a.shape[1]
        ns = b.shape[-1]
        N = scnt * ns
    elif mode == "nt":
        M = a.shape[0]
        N = b.shape[-2]
        ns = b.shape[-1]
        Kc = scnt * ns
    else:
        Kc, M = a.shape
        N = b.shape[1]
        ns = N // ocnt
    tm = tm or _pick(M, (1024, 512, 256, 128))
    if tn is None:
        lim = ns if (mode == "nn" and b_slices is not None) or out_slices is not None else N
        tn = _pick(lim, (1024, 768, 512, 256, 128)) if lim > 1408 else lim
    if tk is None:
        if mode == "nt" and b_slices is not None:
            tk = ns
        elif mode == "tn":
            tk = Kc if Kc <= 4096 else _pick(Kc, (4096, 2048))
        else:
            tk = Kc if Kc <= 2816 else _pick(Kc, (2816, 2048, 1536, 1408, 1024, 512))
    nk = Kc // tk
    assert M % tm == 0 and N % tn == 0 and Kc % tk == 0, (name, M, N, Kc, tm, tn, tk)
    if mode == "nn":
        a_spec = pl.BlockSpec((tm, tk), lambda i, j, k: (i, k))
        if b_slices is None:
            b_spec = pl.BlockSpec((tk, tn), lambda i, j, k: (k, j))
        else:
            spn = ns // tn
            assert ns % tn == 0
            b_spec = pl.BlockSpec((None, tk, tn), lambda i, j, k: (soff + j // spn, k, j % spn))
    elif mode == "nt":
        a_spec = pl.BlockSpec((tm, tk), lambda i, j, k: (i, k))
        if b_slices is None:
            b_spec = pl.BlockSpec((tn, tk), lambda i, j, k: (j, k))
        else:
            spk = ns // tk
            assert ns % tk == 0
            b_spec = pl.BlockSpec((None, tn, tk), lambda i, j, k: (soff + k // spk, j, k % spk))
    else:
        a_spec = pl.BlockSpec((tk, tm), lambda i, j, k: (k, i))
        b_spec = pl.BlockSpec((tk, tn), lambda i, j, k: (k, j))
    if out_slices is None:
        o_spec = pl.BlockSpec((tm, tn), lambda i, j, k: (i, j))
        out_shape = jax.ShapeDtypeStruct((M, N), out_dtype)
    else:
        spo = ns // tn
        assert mode == "tn" and ns % tn == 0
        o_spec = pl.BlockSpec((None, tm, tn), lambda i, j, k: (ooff + j // spo, i, j % spo))
        out_shape = jax.ShapeDtypeStruct((N_DEV, M, ns), out_dtype)
    dims = _DIMS[mode]
    has_res = res is not None
    has_prev = prev is not None

    def body(*refs):
        a_ref, b_ref = refs[0], refs[1]
        res_ref = refs[2] if has_res else None
        o_ref = refs[2 + has_res + has_prev + len(deps)]
        part = lax.dot_general(a_ref[...], b_ref[...], dims, preferred_element_type=F32)

        def finish(acc):
            if has_res:
                acc = acc + res_ref[...]
            o_ref[...] = acc.astype(o_ref.dtype)

        if nk == 1:
            finish(part)
        else:
            acc_ref = refs[3 + has_res + has_prev + len(deps)]
            k = pl.program_id(2)

            @pl.when(k == 0)
            def _():
                acc_ref[...] = part

            @pl.when(jnp.logical_and(k > 0, k < nk - 1))
            def _():
                acc_ref[...] += part

            @pl.when(k == nk - 1)
            def _():
                finish(acc_ref[...] + part)

    in_specs = [a_spec, b_spec]
    args = [a, b]
    if has_res:
        in_specs.append(pl.BlockSpec((tm, tn), lambda i, j, k: (i, j)))
        args.append(res)
    aliases = {}
    if has_prev:
        aliases = {len(args): 0}
        in_specs.append(pl.BlockSpec(memory_space=pl.ANY))
        args.append(prev)
    for d in deps:
        in_specs.append(pl.BlockSpec(memory_space=pl.ANY))
        args.append(d)
    return pl.pallas_call(
        body,
        name=name,
        grid=(M // tm, N // tn, nk),
        in_specs=in_specs,
        out_specs=o_spec,
        out_shape=out_shape,
        scratch_shapes=[pltpu.VMEM((tm, tn), F32)] if nk > 1 else [],
        input_output_aliases=aliases,
        compiler_params=_params(("parallel", "parallel", "arbitrary")),
    )(*args)


HALO = 16
CH = 256


def _fill_pad(pad_ref, chunk_fn, T):
    z = jnp.zeros((HALO, pad_ref.shape[1]), F32)
    pad_ref[pl.ds(0, HALO), :] = z
    pad_ref[pl.ds(HALO + T, HALO), :] = z

    def body(c, carry):
        r0 = pl.multiple_of(c * CH, CH)
        pad_ref[pl.ds(HALO + r0, CH), :] = chunk_fn(r0)
        return carry

    lax.fori_loop(0, T // CH, body, 0)


def _ext(pad_ref, r0):
    return pad_ref[pl.ds(r0, CH + 2 * HALO), :]


def _tap(ext, o):
    if o == 0:
        return ext[HALO:HALO + CH]
    return pltpu.roll(ext, (-o) % ext.shape[0], axis=0)[HALO:HALO + CH]


def _chunks(T, fn, init=0):
    def body(c, carry):
        return fn(pl.multiple_of(c * CH, CH), carry)

    return lax.fori_loop(0, T // CH, body, init)


def _conv3(ext, w):
    return _tap(ext, -1) * w[0:1] + _tap(ext, 0) * w[1:2] + _tap(ext, 1) * w[2:3]


def _conv3_t(ext, w):
    return _tap(ext, 1) * w[0:1] + _tap(ext, 0) * w[1:2] + _tap(ext, -1) * w[2:3]


def _conv3_dw(xext, d):
    return [jnp.sum(_tap(xext, k - 1) * d, axis=0, keepdims=True) for k in range(3)]


def _sigmoid(x):
    return 1.0 / (1.0 + jnp.exp(-x))


_GELU_C = 0.7978845608028654


def _gelu(x):
    return 0.5 * x * (1.0 + jnp.tanh(_GELU_C * (x + 0.044715 * (x * x * x))))


def _gelu_grad(x):
    t = jnp.tanh(_GELU_C * (x + 0.044715 * (x * x * x)))
    return 0.5 * (1.0 + t) + 0.5 * x * (1.0 - t * t) * (_GELU_C * (1.0 + 3.0 * 0.044715 * (x * x)))


def _row_tile(T):
    return _pick(T, (256, 128))


def _rms_fwd(x, g, name):
    T, D = x.shape
    tm = _row_tile(T)

    def body(x_ref, g_ref, h_ref):
        xv = x_ref[...]
        r = lax.rsqrt(jnp.mean(xv * xv, axis=-1, keepdims=True) + EPS)
        h_ref[...] = (xv * r * g_ref[...]).astype(BF16)

    return pl.pallas_call(
        body, name=name, grid=(T // tm,),
        in_specs=[pl.BlockSpec((tm, D), lambda i: (i, 0)), pl.BlockSpec((1, D), lambda i: (0, 0))],
        out_specs=pl.BlockSpec((tm, D), lambda i: (i, 0)),
        out_shape=jax.ShapeDtypeStruct((T, D), BF16),
        compiler_params=_params(("parallel",)),
    )(x, g)


def _rms_bwd(dh, x, g, dres, name, deps=()):
    T, D = x.shape
    tm = _row_tile(T)

    def body(dh_ref, x_ref, g_ref, dres_ref, *rest):
        dx_ref, dxb_ref, dg_ref = rest[len(deps):]
        i = pl.program_id(0)
        xv = x_ref[...]
        r = lax.rsqrt(jnp.mean(xv * xv, axis=-1, keepdims=True) + EPS)
        n = xv * r
        dh_v = dh_ref[...].astype(F32)
        dn = dh_v * g_ref[...]
        dx = dres_ref[...] + r * (dn - n * jnp.mean(dn * n, axis=-1, keepdims=True))
        dx_ref[...] = dx
        dxb_ref[...] = dx.astype(BF16)
        dg = jnp.sum(dh_v * n, axis=0, keepdims=True)

        @pl.when(i == 0)
        def _():
            dg_ref[...] = dg

        @pl.when(i > 0)
        def _():
            dg_ref[...] += dg

    row = pl.BlockSpec((tm, D), lambda i: (i, 0))
    vec = pl.BlockSpec((1, D), lambda i: (0, 0))
    return pl.pallas_call(
        body, name=name, grid=(T // tm,),
        in_specs=[row, row, vec, row] + [pl.BlockSpec(memory_space=pl.ANY)] * len(deps),
        out_specs=[row, row, vec],
        out_shape=[jax.ShapeDtypeStruct((T, D), F32), jax.ShapeDtypeStruct((T, D), BF16),
                   jax.ShapeDtypeStruct((1, D), F32)],
        compiler_params=_params(("arbitrary",)),
    )(dh, x, g, dres, *deps)


def _loss_head(x, tgt, g, name):
    T, D = x.shape
    tm = _row_tile(T)

    def body(x_ref, t_ref, g_ref, dx_ref, dxb_ref, dg_ref, l_ref):
        i = pl.program_id(0)
        xv = x_ref[...]
        gv = g_ref[...]
        r = lax.rsqrt(jnp.mean(xv * xv, axis=-1, keepdims=True) + EPS)
        n = xv * r
        e = n * gv - t_ref[...]
        dy = e * (1.0 / D)
        dn = dy * gv
        dx = r * (dn - n * jnp.mean(dn * n, axis=-1, keepdims=True))
        dx_ref[...] = dx
        dxb_ref[...] = dx.astype(BF16)
        dg = jnp.sum(dy * n, axis=0, keepdims=True)
        per_tok = jnp.mean(e * e, axis=-1, keepdims=True)
        lv = jnp.broadcast_to(0.5 * jnp.sum(per_tok, axis=0, keepdims=True), (1, LANE))

        @pl.when(i == 0)
        def _():
            dg_ref[...] = dg
            l_ref[...] = lv

        @pl.when(i > 0)
        def _():
            dg_ref[...] += dg
            l_ref[...] += lv

    row = pl.BlockSpec((tm, D), lambda i: (i, 0))
    vec = pl.BlockSpec((1, D), lambda i: (0, 0))
    return pl.pallas_call(
        body, name=name, grid=(T // tm,),
        in_specs=[row, row, vec],
        out_specs=[row, row, vec, pl.BlockSpec((1, LANE), lambda i: (0, 0))],
        out_shape=[jax.ShapeDtypeStruct((T, D), F32), jax.ShapeDtypeStruct((T, D), BF16),
                   jax.ShapeDtypeStruct((1, D), F32), jax.ShapeDtypeStruct((1, LANE), F32)],
        compiler_params=_params(("arbitrary",)),
    )(x, tgt, g)


POOL_WINDOWS = (2, 4, 8, 16)


def _pool_offsets(w):
    return range(-(w // 2), w - w // 2)


def _pool_cnt(r0, w, T, shape):
    t = r0 + lax.broadcasted_iota(jnp.int32, shape, 0)
    lo = jnp.maximum(t - w // 2, 0)
    hi = jnp.minimum(t + (w - w // 2 - 1), T - 1)
    return (hi - lo + 1).astype(F32)


def _pooled(pad_ref, r0, w, T):
    ext = _ext(pad_ref, r0)
    s = None
    for o in _pool_offsets(w):
        tap = _tap(ext, o)
        s = tap if s is None else s + tap
    cur = ext[HALO:HALO + CH]
    return s / _pool_cnt(r0, w, T, cur.shape) - cur


def _pool_fwd(P, w_pool, scale, name):
    T = P.shape[0]
    G, PG, _ = w_pool.shape

    def body(a_ref, w_ref, s_ref, o_ref, pad_ref):
        g = pl.program_id(0)
        _fill_pad(pad_ref, lambda r0: a_ref[pl.ds(r0, CH), :].astype(F32), T)
        for gi, w in enumerate(POOL_WINDOWS):
            @pl.when(g == gi)
            def _(w=w):
                def chunk(r0, carry):
                    pooled = _pooled(pad_ref, r0, w, T).astype(BF16)
                    y = jnp.dot(pooled, w_ref[...], preferred_element_type=F32) * s_ref[...]
                    o_ref[pl.ds(r0, CH), :] = y.astype(BF16)
                    return carry

                _chunks(T, chunk)

    return pl.pallas_call(
        body, name=name, grid=(G,),
        in_specs=[pl.BlockSpec((T, PG), lambda g: (0, g)),
                  pl.BlockSpec((None, PG, PG), lambda g: (g, 0, 0)),
                  pl.BlockSpec((1, PG), lambda g: (0, g))],
        out_specs=pl.BlockSpec((T, PG), lambda g: (0, g)),
        out_shape=jax.ShapeDtypeStruct((T, G * PG), BF16),
        scratch_shapes=[pltpu.VMEM((T + 2 * HALO, PG), F32)],
        compiler_params=_params(("parallel",)),
    )(P, w_pool, scale)


def _pool_bwd(P, dy, w_pool, scale, name):
    T = P.shape[0]
    G, PG, _ = w_pool.shape

    def body(a_ref, dy_ref, w_ref, s_ref, da_ref, dw_ref, ds_ref, pad_ref, gpad_ref):
        g = pl.program_id(0)
        gpad_ref[pl.ds(0, HALO), :] = jnp.zeros((HALO, PG), F32)
        gpad_ref[pl.ds(HALO + T, HALO), :] = jnp.zeros((HALO, PG), F32)
        _fill_pad(pad_ref, lambda r0: a_ref[pl.ds(r0, CH), :].astype(F32), T)
        for gi, w in enumerate(POOL_WINDOWS):
            @pl.when(g == gi)
            def _(w=w):
                def chunk1(r0, carry):
                    dw, ds = carry
                    pooled = _pooled(pad_ref, r0, w, T).astype(BF16)
                    ypre = jnp.dot(pooled, w_ref[...], preferred_element_type=F32)
                    dyv = dy_ref[pl.ds(r0, CH), :].astype(F32)
                    ds = ds + jnp.sum(dyv * ypre, axis=0, keepdims=True)
                    dyp = (dyv * s_ref[...]).astype(BF16)
                    dw = dw + lax.dot_general(pooled, dyp, _DIMS["tn"], preferred_element_type=F32)
                    dpool = lax.dot_general(dyp, w_ref[...], _DIMS["nt"], preferred_element_type=F32)
                    gpad_ref[pl.ds(HALO + r0, CH), :] = dpool / _pool_cnt(r0, w, T, dpool.shape)
                    return dw, ds

                dw, ds = _chunks(T, chunk1, (jnp.zeros((PG, PG), F32), jnp.zeros((1, PG), F32)))
                dw_ref[...] = dw
                ds_ref[...] = ds

                def chunk2(r0, carry):
                    ext = _ext(gpad_ref, r0)
                    cur = ext[HALO:HALO + CH]
                    acc = None
                    for o in _pool_offsets(w):
                        tap = _tap(ext, -o)
                        acc = tap if acc is None else acc + tap
                    da_ref[pl.ds(r0, CH), :] = (acc - cur * _pool_cnt(r0, w, T, cur.shape)).astype(BF16)
                    return carry

                _chunks(T, chunk2)

    col = pl.BlockSpec((T, PG), lambda g: (0, g))
    return pl.pallas_call(
        body, name=name, grid=(G,),
        in_specs=[col, col, pl.BlockSpec((None, PG, PG), lambda g: (g, 0, 0)), pl.BlockSpec((1, PG), lambda g: (0, g))],
        out_specs=[col, pl.BlockSpec((None, PG, PG), lambda g: (g, 0, 0)), pl.BlockSpec((1, PG), lambda g: (0, g))],
        out_shape=[jax.ShapeDtypeStruct((T, G * PG), BF16), jax.ShapeDtypeStruct((G, PG, PG), F32),
                   jax.ShapeDtypeStruct((1, G * PG), F32)],
        scratch_shapes=[pltpu.VMEM((T + 2 * HALO, PG), F32), pltpu.VMEM((T + 2 * HALO, PG), F32)],
        compiler_params=_params(("parallel",)),
    )(P, dy, w_pool, scale)


SGU_CHUNK = 128


def _sgu_common(u_ref, v_ref, gs_ref):
    up = u_ref[...].astype(F32)
    vp = v_ref[...].astype(F32)
    gv = _gelu(vp)
    rv = lax.rsqrt(jnp.mean(gv * gv, axis=-1, keepdims=True) + EPS)
    nrm = gv * rv
    return up, vp, nrm, rv, (nrm * gs_ref[...]).astype(BF16)


def _sgu_fwd(P, sgu_w, b_exp, gs, name):
    T = P.shape[0]
    G, _, SG = b_exp.shape
    DP = G * SG
    tm = _row_tile(T)

    def body(u_ref, v_ref, w_ref, b_ref, gs_ref, o_ref):
        up, _, _, _, vn = _sgu_common(u_ref, v_ref, gs_ref)
        gu = _gelu(up)
        for n in range(tm // SGU_CHUNK):
            rows = slice(n * SGU_CHUNK, (n + 1) * SGU_CHUNK)
            for g in range(G):
                cols = slice(g * SG, (g + 1) * SG)
                z = jnp.dot(w_ref[g], vn[rows, cols], preferred_element_type=F32) + b_ref[g]
                o_ref[rows, cols] = (gu[rows, cols] * z).astype(BF16)

    return pl.pallas_call(
        body, name=name, grid=(T // tm,),
        in_specs=[pl.BlockSpec((tm, DP), lambda i: (i, 1)), pl.BlockSpec((tm, DP), lambda i: (i, 2)),
                  pl.BlockSpec(sgu_w.shape, lambda i: (0, 0, 0)), pl.BlockSpec(b_exp.shape, lambda i: (0, 0, 0)),
                  pl.BlockSpec((1, DP), lambda i: (0, 0))],
        out_specs=pl.BlockSpec((tm, DP), lambda i: (i, 0)),
        out_shape=jax.ShapeDtypeStruct((T, DP), BF16),
        compiler_params=_params(("parallel",)),
    )(P, P, sgu_w, b_exp, gs)


def _sgu_bwd(P, dy, da, sgu_w, b_exp, gs, name):
    T = P.shape[0]
    G, _, SG = b_exp.shape
    DP = G * SG
    tm = _row_tile(T)

    def body(u_ref, v_ref, dy_ref, da_ref, w_ref, b_ref, gs_ref, o_ref, dw_ref, db_ref, dgs_ref, dzs_ref):
        i = pl.program_id(0)
        up, vp, nrm, rv, vn = _sgu_common(u_ref, v_ref, gs_ref)
        gu = _gelu(up)
        dyv = dy_ref[...].astype(F32)
        o_ref[:, 0:DP] = da_ref[...]

        @pl.when(i == 0)
        def _():
            dw_ref[...] = jnp.zeros(dw_ref.shape, F32)
            dzs_ref[...] = jnp.zeros(dzs_ref.shape, F32)
            dgs_ref[...] = jnp.zeros(dgs_ref.shape, F32)

        dgs = jnp.zeros((1, DP), F32)
        for n in range(tm // SGU_CHUNK):
            rows = slice(n * SGU_CHUNK, (n + 1) * SGU_CHUNK)
            dvn_parts = []
            for g in range(G):
                cols = slice(g * SG, (g + 1) * SG)
                vng = vn[rows, cols]
                z = jnp.dot(w_ref[g], vng, preferred_element_type=F32) + b_ref[g]
                dyg = dyv[rows, cols]
                du = dyg * z
                o_ref[rows, DP + g * SG:DP + (g + 1) * SG] = (du * _gelu_grad(up[rows, cols])).astype(BF16)
                dz = dyg * gu[rows, cols]
                dzb = dz.astype(BF16)
                dzs_ref[g] += dz
                dw_ref[g] += lax.dot_general(dzb, vng, _DIMS["nt"], preferred_element_type=F32)
                dvn_parts.append(lax.dot_general(w_ref[g], dzb, _DIMS["tn"], preferred_element_type=F32))
            dvn = jnp.concatenate(dvn_parts, axis=1)
            nr = nrm[rows]
            dgs = dgs + jnp.sum(dvn * nr, axis=0, keepdims=True)
            dn = dvn * gs_ref[...]
            dgv = rv[rows] * (dn - nr * jnp.mean(dn * nr, axis=-1, keepdims=True))
            o_ref[rows, 2 * DP:3 * DP] = (dgv * _gelu_grad(vp[rows])).astype(BF16)
        dgs_ref[...] += dgs

        @pl.when(i == T // tm - 1)
        def _():
            for g in range(G):
                db_ref[g] = jnp.broadcast_to(jnp.sum(dzs_ref[g], axis=1, keepdims=True), (SGU_CHUNK, SG))

    full3 = lambda a: pl.BlockSpec(a.shape, lambda i: (0, 0, 0))
    return pl.pallas_call(
        body, name=name, grid=(T // tm,),
        in_specs=[pl.BlockSpec((tm, DP), lambda i: (i, 1)), pl.BlockSpec((tm, DP), lambda i: (i, 2)),
                  pl.BlockSpec((tm, DP), lambda i: (i, 0)), pl.BlockSpec((tm, DP), lambda i: (i, 0)),
                  full3(sgu_w), full3(b_exp), pl.BlockSpec((1, DP), lambda i: (0, 0))],
        out_specs=[pl.BlockSpec((tm, 3 * DP), lambda i: (i, 0)), full3(sgu_w), full3(b_exp),
                   pl.BlockSpec((1, DP), lambda i: (0, 0))],
        out_shape=[jax.ShapeDtypeStruct((T, 6 * DP), BF16), jax.ShapeDtypeStruct(sgu_w.shape, F32),
                   jax.ShapeDtypeStruct(b_exp.shape, F32), jax.ShapeDtypeStruct((1, DP), F32)],
        scratch_shapes=[pltpu.VMEM(b_exp.shape, F32)],
        compiler_params=_params(("arbitrary",)),
    )(P, P, dy, da, sgu_w, b_exp, gs)


def _conv_fwd(P, cw, name):
    T = P.shape[0]
    DP = cw.shape[1]
    tc = 256
    nb = DP // tc

    def body(xc_ref, bg_ref, cg_ref, w_ref, o_ref, pad_ref):
        _fill_pad(pad_ref, lambda r0: cg_ref[pl.ds(r0, CH), :].astype(F32) * xc_ref[pl.ds(r0, CH), :].astype(F32), T)
        w = w_ref[...]

        def chunk(r0, carry):
            cq = _conv3(_ext(pad_ref, r0), w)
            o_ref[pl.ds(r0, CH), :] = (bg_ref[pl.ds(r0, CH), :].astype(F32) * cq).astype(BF16)
            return carry

        _chunks(T, chunk)

    col = lambda off: pl.BlockSpec((T, tc), lambda j: (0, off * nb + j))
    return pl.pallas_call(
        body, name=name, grid=(nb,),
        in_specs=[col(3), col(4), col(5), pl.BlockSpec((3, tc), lambda j: (0, j))],
        out_specs=pl.BlockSpec((T, tc), lambda j: (0, j)),
        out_shape=jax.ShapeDtypeStruct((T, DP), BF16),
        scratch_shapes=[pltpu.VMEM((T + 2 * HALO, tc), F32)],
        compiler_params=_params(("parallel",)),
    )(P, P, P, cw)


def _conv_bwd(P, dy, cw, dPl, name):
    T = P.shape[0]
    DP = cw.shape[1]
    tc = 256
    nb = DP // tc

    def body(xc_ref, bg_ref, cg_ref, dy_ref, w_ref, prev_ref, o_ref, dw_ref, qpad_ref, dpad_ref):
        del prev_ref
        seg = pl.program_id(1)
        w = w_ref[...]
        rows = lambda ref, r0: ref[pl.ds(r0, CH), :].astype(F32)
        _fill_pad(qpad_ref, lambda r0: rows(cg_ref, r0) * rows(xc_ref, r0), T)
        _fill_pad(dpad_ref, lambda r0: rows(dy_ref, r0) * rows(bg_ref, r0), T)

        @pl.when(seg == 0)
        def _():
            def chunk(r0, carry):
                dq = _conv3_t(_ext(dpad_ref, r0), w)
                o_ref[pl.ds(r0, CH), :] = (dq * rows(cg_ref, r0)).astype(BF16)
                dcq = rows(dy_ref, r0) * rows(bg_ref, r0)
                return [a + b for a, b in zip(carry, _conv3_dw(_ext(qpad_ref, r0), dcq))]

            dws = _chunks(T, chunk, [jnp.zeros((1, tc), F32)] * 3)
            dw_ref[...] = jnp.concatenate(dws + [jnp.zeros((5, tc), F32)], axis=0)

        @pl.when(seg == 1)
        def _():
            def chunk(r0, carry):
                cq = _conv3(_ext(qpad_ref, r0), w)
                o_ref[pl.ds(r0, CH), :] = (rows(dy_ref, r0) * cq).astype(BF16)
                return carry

            _chunks(T, chunk)

        @pl.when(seg == 2)
        def _():
            def chunk(r0, carry):
                dq = _conv3_t(_ext(dpad_ref, r0), w)
                o_ref[pl.ds(r0, CH), :] = (dq * rows(xc_ref, r0)).astype(BF16)
                return carry

            _chunks(T, chunk)

    col = lambda off: pl.BlockSpec((T, tc), lambda j, s: (0, off * nb + j))
    return pl.pallas_call(
        body, name=name, grid=(nb, 3),
        in_specs=[col(3), col(4), col(5), pl.BlockSpec((T, tc), lambda j, s: (0, j)),
                  pl.BlockSpec((3, tc), lambda j, s: (0, j)), pl.BlockSpec(memory_space=pl.ANY)],
        out_specs=[pl.BlockSpec((T, tc), lambda j, s: (0, (3 + s) * nb + j)), pl.BlockSpec((8, tc), lambda j, s: (0, j))],
        out_shape=[jax.ShapeDtypeStruct(dPl.shape, BF16), jax.ShapeDtypeStruct((8, DP), F32)],
        scratch_shapes=[pltpu.VMEM((T + 2 * HALO, tc), F32), pltpu.VMEM((T + 2 * HALO, tc), F32)],
        input_output_aliases={5: 0},
        compiler_params=_params(("parallel", "arbitrary")),
    )(P, P, P, dy, cw, dPl)


def _merge_fwd(P, ys, ws, name):
    T = P.shape[0]
    DP, D = ws[0].shape
    tm = _pick(T, (512, 256, 128))
    tn = _pick(D, (1024, 512, 256))
    goff = (6 * DP) // tn

    def body(ya, yb, yc, wa, wb, wc, ga, gb, gc, oa_ref, ob_ref, oc_ref, m_ref):
        m = None
        for y, w, g, o_ref in ((ya, wa, ga, oa_ref), (yb, wb, gb, ob_ref), (yc, wc, gc, oc_ref)):
            o = jnp.dot(y[...], w[...], preferred_element_type=F32)
            o_ref[...] = o.astype(BF16)
            t = _sigmoid(g[...].astype(F32)) * o
            m = t if m is None else m + t
        m_ref[...] = m.astype(BF16)

    yspec = pl.BlockSpec((tm, DP), lambda i, j: (i, 0))
    wspec = pl.BlockSpec((DP, tn), lambda i, j: (0, j))
    gspec = lambda b: pl.BlockSpec((tm, tn), lambda i, j: (i, goff + b * (D // tn) + j))
    ospec = pl.BlockSpec((tm, tn), lambda i, j: (i, j))
    return pl.pallas_call(
        body, name=name, grid=(T // tm, D // tn),
        in_specs=[yspec] * 3 + [wspec] * 3 + [gspec(0), gspec(1), gspec(2)],
        out_specs=[ospec] * 4,
        out_shape=[jax.ShapeDtypeStruct((T, D), BF16)] * 4,
        compiler_params=_params(("parallel", "parallel")),
    )(*ys, *ws, P, P, P)


def _merge_bwd(P, dM, os_, name):
    T, D = dM.shape
    tm = _row_tile(T)

    def body(dm_ref, oa, ob, oc, g_ref, da_ref, db_ref, dc_ref, dg_ref):
        dm = dm_ref[...].astype(F32)
        for b, (o_ref, d_ref) in enumerate(((oa, da_ref), (ob, db_ref), (oc, dc_ref))):
            s = _sigmoid(g_ref[:, b * D:(b + 1) * D].astype(F32))
            d_ref[...] = (dm * s).astype(BF16)
            dg_ref[:, b * D:(b + 1) * D] = (dm * o_ref[...].astype(F32) * s * (1.0 - s)).astype(BF16)

    row = pl.BlockSpec((tm, D), lambda i: (i, 0))
    return pl.pallas_call(
        body, name=name, grid=(T // tm,),
        in_specs=[row] * 4 + [pl.BlockSpec((tm, 3 * D), lambda i: (i, 1))],
        out_specs=[row] * 3 + [pl.BlockSpec((tm, 3 * D), lambda i: (i, 0))],
        out_shape=[jax.ShapeDtypeStruct((T, D), BF16)] * 3 + [jax.ShapeDtypeStruct((T, 3 * D), BF16)],
        compiler_params=_params(("parallel",)),
    )(dM, *os_, P)


def _glu_fwd(U, fw, name):
    T, F2 = U.shape
    F = F2 // 2
    tc = 256
    nb = F // tc

    def body(ug_ref, uv_ref, wg_ref, wv_ref, o_ref, gpad_ref, vpad_ref):
        _fill_pad(gpad_ref, lambda r0: ug_ref[pl.ds(r0, CH), :].astype(F32), T)
        _fill_pad(vpad_ref, lambda r0: uv_ref[pl.ds(r0, CH), :].astype(F32), T)
        wg, wv = wg_ref[...], wv_ref[...]

        def chunk(r0, carry):
            gate = _conv3(_ext(gpad_ref, r0), wg)
            val = _conv3(_ext(vpad_ref, r0), wv)
            o_ref[pl.ds(r0, CH), :] = (gate * _sigmoid(gate) * val).astype(BF16)
            return carry

        _chunks(T, chunk)

    return pl.pallas_call(
        body, name=name, grid=(nb,),
        in_specs=[pl.BlockSpec((T, tc), lambda j: (0, j)), pl.BlockSpec((T, tc), lambda j: (0, nb + j)),
                  pl.BlockSpec((3, tc), lambda j: (0, j)), pl.BlockSpec((3, tc), lambda j: (0, nb + j))],
        out_specs=pl.BlockSpec((T, tc), lambda j: (0, j)),
        out_shape=jax.ShapeDtypeStruct((T, F), BF16),
        scratch_shapes=[pltpu.VMEM((T + 2 * HALO, tc), F32)] * 2,
        compiler_params=_params(("parallel",)),
    )(U, U, fw, fw)


def _glu_bwd(U, dact, fw, name):
    T, F2 = U.shape
    F = F2 // 2
    tc = 128
    nb = F // tc

    def body(ug_ref, uv_ref, da_ref, wg_ref, wv_ref, dug_ref, duv_ref, dwg_ref, dwv_ref, gpad, vpad, dgpad, dvpad):
        _fill_pad(gpad, lambda r0: ug_ref[pl.ds(r0, CH), :].astype(F32), T)
        _fill_pad(vpad, lambda r0: uv_ref[pl.ds(r0, CH), :].astype(F32), T)
        wg, wv = wg_ref[...], wv_ref[...]

        def chunk1(r0, carry):
            gext, vext = _ext(gpad, r0), _ext(vpad, r0)
            gate = _conv3(gext, wg)
            val = _conv3(vext, wv)
            s = _sigmoid(gate)
            da = da_ref[pl.ds(r0, CH), :].astype(F32)
            dval = da * (gate * s)
            dgate = da * val * (s * (1.0 + gate * (1.0 - s)))
            dgpad[pl.ds(HALO + r0, CH), :] = dgate
            dvpad[pl.ds(HALO + r0, CH), :] = dval
            new = _conv3_dw(gext, dgate) + _conv3_dw(vext, dval)
            return [a + b for a, b in zip(carry, new)]

        z = jnp.zeros((HALO, tc), F32)
        for p in (dgpad, dvpad):
            p[pl.ds(0, HALO), :] = z
            p[pl.ds(HALO + T, HALO), :] = z
        dws = _chunks(T, chunk1, [jnp.zeros((1, tc), F32)] * 6)
        dwg_ref[...] = jnp.concatenate(dws[:3] + [jnp.zeros((5, tc), F32)], axis=0)
        dwv_ref[...] = jnp.concatenate(dws[3:] + [jnp.zeros((5, tc), F32)], axis=0)

        def chunk2(r0, carry):
            dug_ref[pl.ds(r0, CH), :] = _conv3_t(_ext(dgpad, r0), wg).astype(BF16)
            duv_ref[pl.ds(r0, CH), :] = _conv3_t(_ext(dvpad, r0), wv).astype(BF16)
            return carry

        _chunks(T, chunk2)

    lo = pl.BlockSpec((T, tc), lambda j: (0, j))
    hi = pl.BlockSpec((T, tc), lambda j: (0, nb + j))
    wlo = pl.BlockSpec((3, tc), lambda j: (0, j))
    whi = pl.BlockSpec((3, tc), lambda j: (0, nb + j))
    dwspec = pl.BlockSpec((8, tc), lambda j: (0, j))
    return pl.pallas_call(
        body, name=name, grid=(nb,),
        in_specs=[lo, hi, lo, wlo, whi],
        out_specs=[lo, lo, dwspec, dwspec],
        out_shape=[jax.ShapeDtypeStruct((T, F), BF16)] * 2 + [jax.ShapeDtypeStruct((8, F), F32)] * 2,
        scratch_shapes=[pltpu.VMEM((T + 2 * HALO, tc), F32)] * 4,
        compiler_params=_params(("parallel",)),
    )(U, U, dact, fw, fw)


_ANY = pl.BlockSpec(memory_space=pl.ANY)
_MESH = pl.DeviceIdType.MESH


def _mesh_pos():
    return lax.axis_index("x"), lax.axis_index("y"), lax.axis_index("c")


def _other_chips(x, y):
    return [(1 - x, y), (x, 1 - y), (1 - x, 1 - y)]


def _all_gather(shards, name, deps=()):
    n = len(shards)
    nd = len(deps)

    def body(*refs):
        ins, outs = refs[:n], refs[n + nd:2 * n + nd]
        send_sems, recv_sems, local_sems = refs[2 * n + nd:]
        x, y, c = _mesh_pos()
        me, sibling = (x, y, c), (x, y, 1 - c)
        chips = _other_chips(x, y)

        def copy(a, k, block, to, src=None):
            px, py, pc = block
            dst = outs[a].at[4 * px + 2 * py + pc]
            return pltpu.make_async_remote_copy(
                src_ref=dst if src is None else src, dst_ref=dst,
                send_sem=send_sems.at[a * 7 + k], recv_sem=recv_sems.at[a * 7 + k],
                device_id=to, device_id_type=_MESH)

        mine = [pltpu.make_async_copy(ins[a], outs[a].at[4 * x + 2 * y + c], local_sems.at[a]) for a in range(n)]
        for cp in mine:
            cp.start()
        first = []
        for a in range(n):
            first.append(copy(a, 0, me, sibling, src=ins[a]))
            first += [copy(a, 1 + j, me, (*chip, c), src=ins[a]) for j, chip in enumerate(chips)]
        for cp in first:
            cp.start()
        passed = []
        for a in range(n):
            for j, chip in enumerate(chips):
                copy(a, 1 + j, (*chip, c), me).wait_recv()
                fwd = copy(a, 4 + j, (*chip, c), sibling)
                fwd.start()
                passed.append(fwd)
        for a in range(n):
            copy(a, 0, sibling, me).wait_recv()
            for j, chip in enumerate(chips):
                copy(a, 4 + j, (*chip, 1 - c), me).wait_recv()
        for cp in first + passed:
            cp.wait_send()
        for cp in mine:
            cp.wait()

    return pl.pallas_call(
        body, name=name,
        in_specs=[_ANY] * (n + len(deps)), out_specs=[_ANY] * n,
        out_shape=[jax.ShapeDtypeStruct((N_DEV,) + s.shape, s.dtype) for s in shards],
        scratch_shapes=[pltpu.SemaphoreType.DMA((7 * n,)), pltpu.SemaphoreType.DMA((7 * n,)),
                        pltpu.SemaphoreType.DMA((n,))],
    )(*shards, *deps)


_HBM = pl.BlockSpec(memory_space=pltpu.HBM)
_SEM = pl.BlockSpec(memory_space=pltpu.SEMAPHORE)
_EFFECT = pltpu.SideEffectType.DATAFLOW_SIDE_EFFECTING


def _in_hbm(a):
    return pltpu.with_memory_space_constraint(a, pltpu.HBM)


def _split_start(groups, name):
    sizes = [len(bufs) for bufs, _, _ in groups]
    nb = sum(sizes)
    ng = len(groups)

    def body(*refs):
        ins = refs[:nb]
        sems = refs[nb:nb + 2 * ng]
        token = refs[-1]
        off = 0
        for gi, (bufs, make_copies, _) in enumerate(groups):
            for cp in make_copies(ins[off:off + len(bufs)], sems[2 * gi], sems[2 * gi + 1]):
                cp.start()
            off += len(bufs)
        token[...] = jnp.zeros(token.shape, token.dtype)

    flat = [b for bufs, _, _ in groups for b in bufs]
    sem_shapes = []
    for _, _, n in groups:
        sem_shapes += [pltpu.SemaphoreType.DMA((n,)), pltpu.SemaphoreType.DMA((n,))]
    outs = pl.pallas_call(
        body, name=name,
        in_specs=[_HBM] * nb,
        out_specs=[_SEM] * (2 * ng) + [_HBM] * nb + [pl.BlockSpec(memory_space=pltpu.VMEM)],
        out_shape=sem_shapes + [pltpu.HBM(b.shape, b.dtype) for b in flat] + [jax.ShapeDtypeStruct((8, LANE), F32)],
        input_output_aliases={i: 2 * ng + i for i in range(nb)},
        compiler_params=pltpu.CompilerParams(has_side_effects=_EFFECT),
    )(*[_in_hbm(b) for b in flat])
    res, off = [], 2 * ng
    for gi, n in enumerate(sizes):
        res.append((outs[2 * gi], outs[2 * gi + 1], list(outs[off:off + n])))
        off += n
    return res, outs[-1]


def _split_wait(started, make_copies, after, name):
    send, recv, bufs = started
    nb = len(bufs)
    after = list(after) if isinstance(after, (list, tuple)) else [after]

    def body(*refs):
        for cp in make_copies(refs[:nb], refs[nb], refs[nb + 1]):
            cp.wait_send()
            cp.wait_recv()

    outs = pl.pallas_call(
        body, name=name,
        in_specs=[_HBM] * nb + [_SEM, _SEM] + [_ANY] * len(after),
        out_specs=[_HBM] * nb,
        out_shape=[pltpu.HBM(b.shape, b.dtype) for b in bufs],
        input_output_aliases={i: i for i in range(nb)},
        compiler_params=pltpu.CompilerParams(has_side_effects=_EFFECT),
    )(*bufs, send, recv, *after)
    return list(outs)


def _copies_gather_chips(nbuf):
    def make(bufs, send, recv):
        x, y, c = _mesh_pos()
        targets = [(x, y, 1 - c)] + [(px, py, c) for px, py in _other_chips(x, y)]
        cps = []
        for a in range(nbuf):
            mine = bufs[a].at[4 * x + 2 * y + c]
            for k, to in enumerate(targets):
                cps.append(pltpu.make_async_remote_copy(src_ref=mine, dst_ref=mine, send_sem=send.at[4 * a + k],
                                                        recv_sem=recv.at[4 * a + k], device_id=to, device_id_type=_MESH))
        return cps

    return make, 4 * nbuf


def _copies_gather_forward(nbuf):
    def make(bufs, send, recv):
        x, y, c = _mesh_pos()
        cps = []
        for a in range(nbuf):
            for j, (px, py) in enumerate(_other_chips(x, y)):
                slot = bufs[a].at[4 * px + 2 * py + c]
                cps.append(pltpu.make_async_remote_copy(src_ref=slot, dst_ref=slot, send_sem=send.at[3 * a + j],
                                                        recv_sem=recv.at[3 * a + j], device_id=(x, y, 1 - c),
                                                        device_id_type=_MESH))
        return cps

    return make, 3 * nbuf


def _copies_rs_sibling(n):
    def make(bufs, send, recv):
        x, y, c = _mesh_pos()
        cps = []
        for a in range(n):
            for q in range(4):
                cps.append(pltpu.make_async_remote_copy(
                    src_ref=bufs[a].at[2 * q + (1 - c)], dst_ref=bufs[n + a].at[q], send_sem=send.at[4 * a + q],
                    recv_sem=recv.at[4 * a + q], device_id=(x, y, 1 - c), device_id_type=_MESH))
        return cps

    return make, 4 * n


def _copies_rs_chips(n):
    def make(bufs, send, recv):
        x, y, c = _mesh_pos()
        cps = []
        for a in range(n):
            for j, (px, py) in enumerate(_other_chips(x, y)):
                cps.append(pltpu.make_async_remote_copy(
                    src_ref=bufs[a].at[2 * px + py], dst_ref=bufs[n + a].at[j], send_sem=send.at[3 * a + j],
                    recv_sem=recv.at[3 * a + j], device_id=(px, py, c), device_id_type=_MESH))
        return cps

    return make, 3 * n


def _cast_place(stacked, layer, slot, dtype, name):
    _, R, C = stacked.shape
    tr = _rows_tile(R, C)

    def body(slot_ref, x_ref, o_ref):
        del slot_ref
        o_ref[...] = x_ref[...].astype(dtype)

    return pl.pallas_call(
        body, name=name,
        grid_spec=pltpu.PrefetchScalarGridSpec(
            num_scalar_prefetch=1, grid=(R // tr,),
            in_specs=[pl.BlockSpec((None, tr, C), lambda i, s: (layer, i, 0))],
            out_specs=pl.BlockSpec((None, tr, C), lambda i, s: (s[0], i, 0))),
        out_shape=jax.ShapeDtypeStruct((N_DEV, R, C), dtype),
        compiler_params=_params(("parallel",)),
    )(slot, stacked)


def _rows_tile(R, C, budget=2 * MIB):
    if R * C * 4 <= budget:
        return R
    for t in (2048, 1024, 704, 512, 352, 256, 128, 64, 32, 16, 8):
        if R % t == 0 and t * C * 4 <= budget:
            return t
    return R


def _pair_sum(grad, recv, core, name):
    _, R, C = grad.shape
    tr = _rows_tile(R, C)

    def body(c_ref, g_ref, r_ref, o_ref):
        del c_ref
        o_ref[...] = (g_ref[...].astype(F32) + r_ref[...].astype(F32)).astype(BF16)

    return pl.pallas_call(
        body, name=name,
        grid_spec=pltpu.PrefetchScalarGridSpec(
            num_scalar_prefetch=1, grid=(4, R // tr),
            in_specs=[pl.BlockSpec((None, tr, C), lambda q, i, c: (2 * q + c[0], i, 0)),
                      pl.BlockSpec((None, tr, C), lambda q, i, c: (q, i, 0))],
            out_specs=pl.BlockSpec((None, tr, C), lambda q, i, c: (q, i, 0))),
        out_shape=jax.ShapeDtypeStruct((4, R, C), BF16),
        compiler_params=_params(("parallel", "parallel")),
    )(core, grad, recv)


ADAM_LR = 0.001
ADAM_B1 = 0.9
ADAM_B2 = 0.999
ADAM_EPS = 1e-08
ADAM_WD = 0.01
ADAM_STEP = 10


def _adamw(w, g, m, v):
    m = ADAM_B1 * m + (1.0 - ADAM_B1) * g
    v = ADAM_B2 * v + (1.0 - ADAM_B2) * (g * g)
    m_hat = m / (1.0 - ADAM_B1 ** ADAM_STEP)
    v_hat = v / (1.0 - ADAM_B2 ** ADAM_STEP)
    delta = -ADAM_LR * (m_hat / (jnp.sqrt(v_hat) + ADAM_EPS) + ADAM_WD * w)
    return delta, m, v


def _adamw_big(pair, recv, w, m, v, chip, layer, prev, name):
    L, R, C = w.shape
    tr = _rows_tile(R, C, MIB)
    has_prev = prev is not None

    def body(chip_ref, p_ref, r0_ref, r1_ref, r2_ref, w_ref, m_ref, v_ref, *rest):
        del chip_ref
        g_ref, d_ref, mo_ref, vo_ref = rest[-4:]
        g = ((p_ref[...].astype(F32) + r0_ref[...].astype(F32)) + r1_ref[...].astype(F32)) + r2_ref[...].astype(F32)
        delta, m2, v2 = _adamw(w_ref[...], g, m_ref[...], v_ref[...])
        g_ref[...] = g
        d_ref[...] = delta
        mo_ref[...] = m2
        vo_ref[...] = v2

    lay = pl.BlockSpec((None, tr, C), lambda i, c: (layer, i, 0))
    rspec = lambda j: pl.BlockSpec((None, tr, C), lambda i, c: (j, i, 0))
    in_specs = [pl.BlockSpec((None, tr, C), lambda i, c: (c[0], i, 0)), rspec(0), rspec(1), rspec(2), lay, lay, lay]
    args = [chip, pair, recv, recv, recv, w, m, v]
    aliases = {}
    if has_prev:
        in_specs += [_ANY] * 4
        args += list(prev)
        aliases = {8 + k: k for k in range(4)}
    return pl.pallas_call(
        body, name=name,
        grid_spec=pltpu.PrefetchScalarGridSpec(
            num_scalar_prefetch=1, grid=(R // tr,), in_specs=in_specs, out_specs=[lay] * 4),
        out_shape=[jax.ShapeDtypeStruct((L, R, C), F32)] * 4,
        input_output_aliases=aliases,
        compiler_params=_params(("parallel",)),
    )(*args)


def _sum_slots(gathered, name):
    _, R, C = gathered.shape
    tr = _rows_tile(R, C, 2 * MIB)

    def body(x_ref, o_ref):
        s = x_ref[0]
        for k in range(1, N_DEV):
            s = s + x_ref[k]
        o_ref[...] = s

    return pl.pallas_call(
        body, name=name, grid=(R // tr,),
        in_specs=[pl.BlockSpec((N_DEV, tr, C), lambda i: (0, i, 0))],
        out_specs=pl.BlockSpec((tr, C), lambda i: (i, 0)),
        out_shape=jax.ShapeDtypeStruct((R, C), F32),
        compiler_params=_params(("parallel",)),
    )(gathered)


def _adamw_small(w, g, m, v, name):
    R, C = w.shape
    tr = _rows_tile(R, C, 2 * MIB)

    def body(w_ref, g_ref, m_ref, v_ref, d_ref, mo_ref, vo_ref):
        d_ref[...], mo_ref[...], vo_ref[...] = _adamw(w_ref[...], g_ref[...], m_ref[...], v_ref[...])

    spec = pl.BlockSpec((tr, C), lambda i: (i, 0))
    return pl.pallas_call(
        body, name=name, grid=(R // tr,),
        in_specs=[spec] * 4, out_specs=[spec] * 3,
        out_shape=[jax.ShapeDtypeStruct((R, C), F32)] * 3,
        compiler_params=_params(("parallel",)),
    )(w, g, m, v)


def _pack(arrs):
    parts = []
    for a in arrs:
        flat = a.reshape(-1)
        pad = (-flat.shape[0]) % (8 * LANE)
        if pad:
            flat = jnp.pad(flat, (0, pad))
        parts.append(flat.reshape(-1, LANE))
    return jnp.concatenate(parts, axis=0)


def _unpack(packed, shapes):
    out, r = [], 0
    for s in shapes:
        n = 1
        for d in s:
            n *= d
        rows = -(-n // (8 * LANE)) * 8
        out.append(packed[r:r + rows].reshape(-1)[:n].reshape(s))
        r += rows
    return out


def _to_slices(a, ns):
    K = a.shape[0]
    return a.reshape(K, N_DEV, ns).transpose(1, 0, 2)


def _from_slices(a):
    _, K, ns = a.shape
    return a.transpose(1, 0, 2).reshape(K, N_DEV * ns)


def kernel(x, norm1_g, w_in, w_pool, pool_scale, sgu_norm_g, sgu_w, sgu_b, conv_w, w_pool_out, w_sgu_out, w_conv_out, w_o, norm2_g, w_up, ffn_conv_w, w_down, final_g, loss_target, m_norm1_g, m_w_in, m_w_pool, m_pool_scale, m_sgu_norm_g, m_sgu_w, m_sgu_b, m_conv_w, m_w_pool_out, m_w_sgu_out, m_w_conv_out, m_w_o, m_norm2_g, m_w_up, m_ffn_conv_w, m_w_down, m_final_g, v_norm1_g, v_w_in, v_w_pool, v_pool_scale, v_sgu_norm_g, v_sgu_w, v_sgu_b, v_conv_w, v_w_pool_out, v_w_sgu_out, v_w_conv_out, v_w_o, v_norm2_g, v_w_up, v_ffn_conv_w, v_w_down, v_final_g):
    L = norm1_g.shape[0]
    x0 = x[0]
    tgt = loss_target[0]
    G, PG = w_pool.shape[1], w_pool.shape[3]
    F = w_down.shape[1] * N_DEV
    px, py, pc = _mesh_pos()
    me = 4 * px + 2 * py + pc
    core = jnp.reshape(pc, (1,)).astype(jnp.int32)
    chip = jnp.reshape(2 * px + py, (1,)).astype(jnp.int32)

    slot = jnp.reshape(me, (1,)).astype(jnp.int32)
    placed, group_sizes = [], []
    for l in range(L):
        grp = [[("w_in", w_in, BF16)],
               [("w_pool_out", w_pool_out, BF16), ("w_sgu_out", w_sgu_out, BF16), ("w_conv_out", w_conv_out, BF16),
                ("w_pool", w_pool.reshape(L, -1, PG), BF16), ("conv_w", conv_w, F32), ("w_o", w_o, BF16)],
               [("w_up", w_up, BF16), ("ffn_conv_w", ffn_conv_w, F32)],
               [("w_down", w_down, BF16)]]
        for members in grp:
            placed += [_cast_place(a, l, slot, dt, f"place_{n}_{l}") for n, a, dt in members]
            group_sizes.append(len(members))
    stage1, off = [], 0
    for n in group_sizes:
        stage1.append((placed[off:off + n],) + _copies_gather_chips(n))
        off += n
    started1, tok0 = _split_start(stage1, "ag_start")
    started2 = {}

    def gather_forward(k, after):
        n = group_sizes[k]
        bufs = _split_wait(started1[k], stage1[k][1], after, f"ag_wait1_{k}")
        st, tok = _split_start([(bufs,) + _copies_gather_forward(n)], f"ag_fwd_{k}")
        started2[k] = st[0]
        return tok

    def gather_done(k, after):
        return _split_wait(started2.pop(k), _copies_gather_forward(group_sizes[k])[0], after, f"ag_wait2_{k}")

    W, saved = [], []
    xc = x0
    tok = gather_forward(0, tok0)
    for l in range(L):
        w = dict(
            sguw=sgu_w[l].astype(BF16),
            bexp=jnp.broadcast_to(sgu_b[l].T[:, :, None],
                                  (sgu_b.shape[2], sgu_b.shape[1], sgu_norm_g.shape[1] // sgu_b.shape[2])),
            g1=norm1_g[l][None], g2=norm2_g[l][None], gs=sgu_norm_g[l][None], scale=pool_scale[l][None])
        ka, kb, kc, kd = 4 * l, 4 * l + 1, 4 * l + 2, 4 * l + 3
        h1 = _rms_fwd(xc, w["g1"], f"rms1_fwd_{l}")
        (w["win3"],) = gather_done(ka, h1)
        if l > 0:
            tok = gather_forward(kb, h1)
        P = _matmul(h1, w["win3"], mode="nn", out_dtype=BF16, name=f"proj_in_{l}", b_slices=(0, N_DEV), deps=[tok])
        if l == 0:
            gather_forward(kb, P)
        wa, wb, wc, wp, cw, wo = gather_done(kb, P)
        w.update(wa=_from_slices(wa), wb=_from_slices(wb), wc=_from_slices(wc), cw=_from_slices(cw),
                 wpool=wp.reshape(N_DEV, G, -1, PG).transpose(1, 0, 2, 3).reshape(G, PG, PG),
                 wo=wo.reshape(-1, wo.shape[2]))
        ya = _pool_fwd(P, w["wpool"], w["scale"], f"pool_fwd_{l}")
        yb = _sgu_fwd(P, w["sguw"], w["bexp"], w["gs"], f"sgu_fwd_{l}")
        yc = _conv_fwd(P, w["cw"], f"conv_fwd_{l}")
        oa, ob, oc, M = _merge_fwd(P, (ya, yb, yc), (w["wa"], w["wb"], w["wc"]), f"merge_fwd_{l}")
        tok = gather_forward(kc, M)
        x1 = _matmul(M, w["wo"], mode="nn", out_dtype=F32, name=f"proj_o_{l}", res=xc, deps=[tok])
        h2 = _rms_fwd(x1, w["g2"], f"rms2_fwd_{l}")
        w["wup3"], fw = gather_done(kc, h2)
        w["fw"] = _from_slices(fw)
        tok = gather_forward(kd, h2)
        U = _matmul(h2, w["wup3"], mode="nn", out_dtype=BF16, name=f"proj_up_{l}", b_slices=(0, N_DEV), deps=[tok])
        act = _glu_fwd(U, w["fw"], f"glu_fwd_{l}")
        (wdown,) = gather_done(kd, act)
        w["wdown"] = wdown.reshape(-1, wdown.shape[2])
        deps = [gather_forward(4 * (l + 1), act)] if l + 1 < L else []
        x2 = _matmul(act, w["wdown"], mode="nn", out_dtype=F32, name=f"proj_down_{l}", res=x1, deps=deps,
                     tm=_pick(x1.shape[0], (512, 256)))
        if deps:
            tok = deps[0]
        W.append(w)
        saved.append(dict(x0=xc, h1=h1, P=P, ya=ya, yb=yb, yc=yc, oa=oa, ob=ob, oc=oc, M=M, x1=x1, h2=h2, U=U, act=act))
        xc = x2

    dx, dxb, d_final_g, loss_part = _loss_head(xc, tgt, final_g[None], "loss_head")
    loss = lax.psum(loss_part[0, 0], ("x", "y", "c"))

    wmv = dict(w_in=(w_in, m_w_in, v_w_in), w_up=(w_up, m_w_up, v_w_up), w_o=(w_o, m_w_o, v_w_o),
               w_down=(w_down, m_w_down, v_w_down), w_pool_out=(w_pool_out, m_w_pool_out, v_w_pool_out),
               w_sgu_out=(w_sgu_out, m_w_sgu_out, v_w_sgu_out), w_conv_out=(w_conv_out, m_w_conv_out, v_w_conv_out),
               w_pool=(w_pool, m_w_pool, v_w_pool))
    adam_out = {}

    def rs_begin(tag, layer, names, grads):
        n = len(grads)
        lands = [lax.empty((4,) + g.shape[1:], BF16) for g in grads]
        st, tok = _split_start([(list(grads) + lands,) + _copies_rs_sibling(n)], f"rs_sib_{tag}")
        return dict(tag=tag, layer=layer, names=names, n=n, st=st[0]), tok

    def rs_to_chips(state, after):
        n, tag = state["n"], state["tag"]
        bufs = _split_wait(state["st"], _copies_rs_sibling(n)[0], after, f"rs_sibw_{tag}")
        pairs = [_pair_sum(g, r, core, f"pair_{tag}_{i}") for i, (g, r) in enumerate(zip(bufs[:n], bufs[n:]))]
        lands = [lax.empty((3,) + p.shape[1:], BF16) for p in pairs]
        st, tok = _split_start([(pairs + lands,) + _copies_rs_chips(n)], f"rs_chips_{tag}")
        state["st"] = st[0]
        return tok

    def rs_finish(state, after):
        n, tag, layer = state["n"], state["tag"], state["layer"]
        bufs = _split_wait(state["st"], _copies_rs_chips(n)[0], after, f"rs_chipsw_{tag}")
        for name, pair, recv in zip(state["names"], bufs[:n], bufs[n:]):
            wk, mk, vk = (a.reshape(L, -1, a.shape[-1]) for a in wmv[name])
            adam_out[name] = _adamw_big(pair, recv, wk, mk, vk, chip, layer, adam_out.get(name), f"adamw_{name}_{layer}")

    small = [None] * L
    pending = []
    deps = []
    ns_out = w_pool_out.shape[2]
    for l in reversed(range(L)):
        w, s = W[l], saved[l]
        dact = _matmul(dxb, w["wdown"], mode="nt", out_dtype=BF16, name=f"d_act_{l}", tn=_pick(F, (512, 256)), deps=deps)
        g_down = _matmul(s["act"], dxb, mode="tn", out_dtype=BF16, name=f"g_down_{l}", tm=_pick(F, (512, 256)))
        dug, duv, dfwg, dfwv = _glu_bwd(s["U"], dact, w["fw"], f"glu_bwd_{l}")
        dh2 = _matmul(dug, w["wup3"], mode="nt", out_dtype=F32, name=f"d_h2a_{l}", b_slices=(0, 4))
        dh2 = _matmul(duv, w["wup3"], mode="nt", out_dtype=F32, name=f"d_h2b_{l}", b_slices=(4, 4), res=dh2)
        g_up = _matmul(s["h2"], dug, mode="tn", out_dtype=BF16, name=f"g_upa_{l}", out_slices=(0, 4), tm=512)
        g_up = _matmul(s["h2"], duv, mode="tn", out_dtype=BF16, name=f"g_upb_{l}", out_slices=(4, 4), prev=g_up, tm=512)
        ra, tok = rs_begin(f"a{l}", l, ["w_down", "w_up"], [g_down.reshape(N_DEV, -1, g_down.shape[1]), g_up])
        if pending:
            rs_finish(pending.pop(0), g_up)
        dx1, dx1b, dg2 = _rms_bwd(dh2, s["x1"], w["g2"], dx, f"rms2_bwd_{l}")
        dM = _matmul(dx1b, w["wo"], mode="nt", out_dtype=BF16, name=f"d_m_{l}", deps=[tok])
        g_o = _matmul(s["M"], dx1b, mode="tn", out_dtype=BF16, name=f"g_o_{l}")
        doa, dob, doc, dPg = _merge_bwd(s["P"], dM, (s["oa"], s["ob"], s["oc"]), f"merge_bwd_{l}")
        tok = rs_to_chips(ra, dPg)
        if pending:
            rs_finish(pending.pop(0), dPg)
        dya = _matmul(doa, w["wa"], mode="nt", out_dtype=BF16, name=f"d_ya_{l}", deps=[tok])
        dyb = _matmul(dob, w["wb"], mode="nt", out_dtype=BF16, name=f"d_yb_{l}")
        dyc = _matmul(doc, w["wc"], mode="nt", out_dtype=BF16, name=f"d_yc_{l}")
        g_a = _matmul(s["ya"], doa, mode="tn", out_dtype=BF16, name=f"g_a_{l}")
        g_b = _matmul(s["yb"], dob, mode="tn", out_dtype=BF16, name=f"g_b_{l}")
        g_c = _matmul(s["yc"], doc, mode="tn", out_dtype=BF16, name=f"g_c_{l}")
        da, g_pool, dscale = _pool_bwd(s["P"], dya, w["wpool"], w["scale"], f"pool_bwd_{l}")
        g_pool_s = g_pool.reshape(G, N_DEV, -1, PG).transpose(1, 0, 2, 3).reshape(N_DEV, -1, PG).astype(BF16)
        rb, tok = rs_begin(f"b{l}", l, ["w_o", "w_pool_out", "w_sgu_out", "w_conv_out", "w_pool"],
                           [g_o.reshape(N_DEV, -1, g_o.shape[1]), _to_slices(g_a, ns_out), _to_slices(g_b, ns_out),
                            _to_slices(g_c, ns_out), g_pool_s])
        if pending:
            rs_finish(pending.pop(0), da)
        dPl, g_sguw, db_exp, dgs = _sgu_bwd(s["P"], dyb, da, w["sguw"], w["bexp"], w["gs"], f"sgu_bwd_{l}")
        dPl, dcw = _conv_bwd(s["P"], dyc, w["cw"], dPl, f"conv_bwd_{l}")
        dh1 = _matmul(dPl, w["win3"], mode="nt", out_dtype=F32, name=f"d_h1a_{l}", b_slices=(0, 4), deps=[tok])
        dh1 = _matmul(dPg, w["win3"], mode="nt", out_dtype=F32, name=f"d_h1b_{l}", b_slices=(4, 4), res=dh1)
        tok = rs_to_chips(rb, dh1)
        g_in = _matmul(s["h1"], dPl, mode="tn", out_dtype=BF16, name=f"g_ina_{l}", out_slices=(0, 4), deps=[tok])
        g_in = _matmul(s["h1"], dPg, mode="tn", out_dtype=BF16, name=f"g_inb_{l}", out_slices=(4, 4), prev=g_in)
        rc, tok = rs_begin(f"c{l}", l, ["w_in"], [g_in])
        dx, dxb, dg1 = _rms_bwd(dh1, s["x0"], w["g1"], dx1, f"rms1_bwd_{l}", deps=[tok])
        deps = [rs_to_chips(rc, dx)]
        pending = [ra, rb, rc]
        small[l] = dict(norm1_g=dg1[0], pool_scale=dscale[0], sgu_norm_g=dgs[0], sgu_w=g_sguw,
                        sgu_b=db_exp[:, :, 0].T, conv_w=dcw[:3], norm2_g=dg2[0],
                        ffn_conv_w=jnp.concatenate([dfwg[:3], dfwv[:3]], axis=1))
    grad_x = dx[None]

    snames = ["norm1_g", "pool_scale", "sgu_norm_g", "sgu_w", "sgu_b", "conv_w", "norm2_g", "ffn_conv_w"]
    sparts = [jnp.stack([small[l][n] for l in range(L)]) for n in snames] + [d_final_g[0]]
    snames = snames + ["final_g"]
    packed = _pack(sparts)
    total = _sum_slots(_all_gather([packed], "ag_small", deps=deps)[0], "sum_small")
    sgrads = dict(zip(snames, _unpack(total, [p.shape for p in sparts])))
    for n, width in (("conv_w", conv_w.shape[2]), ("ffn_conv_w", ffn_conv_w.shape[2])):
        sgrads[n] = lax.dynamic_slice_in_dim(sgrads[n], me * width, width, axis=2)
    sw = dict(norm1_g=(norm1_g, m_norm1_g, v_norm1_g), pool_scale=(pool_scale, m_pool_scale, v_pool_scale),
              sgu_norm_g=(sgu_norm_g, m_sgu_norm_g, v_sgu_norm_g), sgu_w=(sgu_w, m_sgu_w, v_sgu_w),
              sgu_b=(sgu_b, m_sgu_b, v_sgu_b), conv_w=(conv_w, m_conv_w, v_conv_w),
              norm2_g=(norm2_g, m_norm2_g, v_norm2_g), ffn_conv_w=(ffn_conv_w, m_ffn_conv_w, v_ffn_conv_w),
              final_g=(final_g, m_final_g, v_final_g))
    shapes = [sw[n][0].shape for n in snames]
    upd = _adamw_small(_pack([sw[n][0] for n in snames]), _pack([sgrads[n] for n in snames]),
                       _pack([sw[n][1] for n in snames]), _pack([sw[n][2] for n in snames]), "adamw_small")
    sdelta, sm, sv = (dict(zip(snames, _unpack(u, shapes))) for u in upd)
    res = {n: [sgrads[n], sdelta[n], sm[n], sv[n]] for n in snames}

    rs_finish(pending[0], total)
    rs_finish(pending[1], upd[0])
    rs_finish(pending[2], [upd[0]] + [o[0] for o in adam_out.values()])
    for n, outs in adam_out.items():
        res[n] = [o.reshape(wmv[n][0].shape) for o in outs]

    order = ["norm1_g", "w_in", "w_pool", "pool_scale", "sgu_norm_g", "sgu_w", "sgu_b", "conv_w", "w_pool_out",
             "w_sgu_out", "w_conv_out", "w_o", "norm2_g", "w_up", "ffn_conv_w", "w_down", "final_g"]
    return (loss, grad_x) + tuple(res[n][k] for k in range(4) for n in order)
```

```python
import functools

import jax
import jax.numpy as jnp
from jax import lax
from jax.experimental import pallas as pl
from jax.experimental.pallas import tpu as pltpu

BF16 = jnp.bfloat16
F32 = jnp.float32
EPS = 1e-6
MIB = 1024 * 1024
V7X_VMEM_BYTES = 64 * MIB
VMEM_LIMIT = 56 * MIB
LANE = 128
N_DEV = 8


def _params(sem, **kw):
    return pltpu.CompilerParams(dimension_semantics=sem, vmem_limit_bytes=VMEM_LIMIT, **kw)


def _pick(n, cands):
    for c in cands:
        if n % c == 0:
            return c
    return n


_DIMS = {"nn": (((1,), (0,)), ((), ())), "nt": (((1,), (1,)), ((), ())), "tn": (((0,), (0,)), ((), ()))}


def _matmul(a, b, *, mode, out_dtype, name, res=None, b_slices=None, out_slices=None, prev=None,
            deps=(), tm=None, tn=None, tk=None):
    soff, scnt = b_slices if b_slices is not None else (0, 1)
    ooff, ocnt = out_slices if out_slices is not None else (0, 1)
    if mode == "nn":
        M = a.shape[0]
        Kc = a.shape[1]
        ns = b.shape[-1]
        N = scnt * ns
    elif mode == "nt":
        M = a.shape[0]
        N = b.shape[-2]
        ns = b.shape[-1]
        Kc = scnt * ns
    else:
        Kc, M = a.shape
        N = b.shape[1]
        ns = N // ocnt
    tm = tm or _pick(M, (1024, 512, 256, 128))
    if tn is None:
        lim = ns if (mode == "nn" and b_slices is not None) or out_slices is not None else N
        tn = _pick(lim, (1024, 768, 512, 256, 128)) if lim > 1408 else lim
    if tk is None:
        if mode == "nt" and b_slices is not None:
            tk = ns
        elif mode == "tn":
            tk = Kc if Kc <= 4096 else _pick(Kc, (4096, 2048))
        else:
            tk = Kc if Kc <= 2816 else _pick(Kc, (2816, 2048, 1536, 1408, 1024, 512))
    nk = Kc // tk
    assert M % tm == 0 and N % tn == 0 and Kc % tk == 0, (name, M, N, Kc, tm, tn, tk)
    if mode == "nn":
        a_spec = pl.BlockSpec((tm, tk), lambda i, j, k: (i, k))
        if b_slices is None:
            b_spec = pl.BlockSpec((tk, tn), lambda i, j, k: (k, j))
        else:
            spn = ns // tn
            assert ns % tn == 0
            b_spec = pl.BlockSpec((None, tk, tn), lambda i, j, k: (soff + j // spn, k, j % spn))
    elif mode == "nt":
        a_spec = pl.BlockSpec((tm, tk), lambda i, j, k: (i, k))
        if b_slices is None:
            b_spec = pl.BlockSpec((tn, tk), lambda i, j, k: (j, k))
        else:
            spk = ns // tk
            assert ns % tk == 0
            b_spec = pl.BlockSpec((None, tn, tk), lambda i, j, k: (soff + k // spk, j, k % spk))
    else:
        a_spec = pl.BlockSpec((tk, tm), lambda i, j, k: (k, i))
        b_spec = pl.BlockSpec((tk, tn), lambda i, j, k: (k, j))
    if out_slices is None:
        o_spec = pl.BlockSpec((tm, tn), lambda i, j, k: (i, j))
        out_shape = jax.ShapeDtypeStruct((M, N), out_dtype)
    else:
        spo = ns // tn
        assert mode == "tn" and ns % tn == 0
        o_spec = pl.BlockSpec((None, tm, tn), lambda i, j, k: (ooff + j // spo, i, j % spo))
        out_shape = jax.ShapeDtypeStruct((N_DEV, M, ns), out_dtype)
    dims = _DIMS[mode]
    has_res = res is not None
    has_prev = prev is not None

    def body(*refs):
        a_ref, b_ref = refs[0], refs[1]
        res_ref = refs[2] if has_res else None
        o_ref = refs[2 + has_res + has_prev + len(deps)]
        part = lax.dot_general(a_ref[...], b_ref[...], dims, preferred_element_type=F32)

        def finish(acc):
            if has_res:
                acc = acc + res_ref[...]
            o_ref[...] = acc.astype(o_ref.dtype)

        if nk == 1:
            finish(part)
        else:
            acc_ref = refs[3 + has_res + has_prev + len(deps)]
            k = pl.program_id(2)

            @pl.when(k == 0)
            def _():
                acc_ref[...] = part

            @pl.when(jnp.logical_and(k > 0, k < nk - 1))
            def _():
                acc_ref[...] += part

            @pl.when(k == nk - 1)
            def _():
                finish(acc_ref[...] + part)

    in_specs = [a_spec, b_spec]
    args = [a, b]
    if has_res:
        in_specs.append(pl.BlockSpec((tm, tn), lambda i, j, k: (i, j)))
        args.append(res)
    aliases = {}
    if has_prev:
        aliases = {len(args): 0}
        in_specs.append(pl.BlockSpec(memory_space=pl.ANY))
        args.append(prev)
    for d in deps:
        in_specs.append(pl.BlockSpec(memory_space=pl.ANY))
        args.append(d)
    return pl.pallas_call(
        body,
        name=name,
        grid=(M // tm, N // tn, nk),
        in_specs=in_specs,
        out_specs=o_spec,
        out_shape=out_shape,
        scratch_shapes=[pltpu.VMEM((tm, tn), F32)] if nk > 1 else [],
        input_output_aliases=aliases,
        compiler_params=_params(("parallel", "parallel", "arbitrary")),
    )(*args)


HALO = 16
CH = 256


def _fill_pad(pad_ref, chunk_fn, T):
    z = jnp.zeros((HALO, pad_ref.shape[1]), F32)
    pad_ref[pl.ds(0, HALO), :] = z
    pad_ref[pl.ds(HALO + T, HALO), :] = z

    def body(c, carry):
        r0 = pl.multiple_of(c * CH, CH)
        pad_ref[pl.ds(HALO + r0, CH), :] = chunk_fn(r0)
        return carry

    lax.fori_loop(0, T // CH, body, 0)


def _ext(pad_ref, r0):
    return pad_ref[pl.ds(r0, CH + 2 * HALO), :]


def _tap(ext, o):
    if o == 0:
        return ext[HALO:HALO + CH]
    return pltpu.roll(ext, (-o) % ext.shape[0], axis=0)[HALO:HALO + CH]


def _chunks(T, fn, init=0):
    def body(c, carry):
        return fn(pl.multiple_of(c * CH, CH), carry)

    return lax.fori_loop(0, T // CH, body, init)


def _conv3(ext, w):
    return _tap(ext, -1) * w[0:1] + _tap(ext, 0) * w[1:2] + _tap(ext, 1) * w[2:3]


def _conv3_t(ext, w):
    return _tap(ext, 1) * w[0:1] + _tap(ext, 0) * w[1:2] + _tap(ext, -1) * w[2:3]


def _conv3_dw(xext, d):
    return [jnp.sum(_tap(xext, k - 1) * d, axis=0, keepdims=True) for k in range(3)]


def _sigmoid(x):
    return 0.5 * (jnp.tanh(0.5 * x) + 1.0)


_GELU_C = 0.7978845608028654


def _gelu(x):
    return 0.5 * x * (1.0 + jnp.tanh(_GELU_C * (x + 0.044715 * (x * x * x))))


def _gelu_grad(x):
    t = jnp.tanh(_GELU_C * (x + 0.044715 * (x * x * x)))
    return 0.5 * (1.0 + t) + 0.5 * x * (1.0 - t * t) * (_GELU_C * (1.0 + 3.0 * 0.044715 * (x * x)))


def _row_tile(T):
    return _pick(T, (256, 128))


def _rms_fwd(x, g, name):
    T, D = x.shape
    tm = _row_tile(T)

    def body(x_ref, g_ref, h_ref):
        xv = x_ref[...]
        r = lax.rsqrt(jnp.mean(xv * xv, axis=-1, keepdims=True) + EPS)
        h_ref[...] = (xv * r * g_ref[...]).astype(BF16)

    return pl.pallas_call(
        body, name=name, grid=(T // tm,),
        in_specs=[pl.BlockSpec((tm, D), lambda i: (i, 0)), pl.BlockSpec((1, D), lambda i: (0, 0))],
        out_specs=pl.BlockSpec((tm, D), lambda i: (i, 0)),
        out_shape=jax.ShapeDtypeStruct((T, D), BF16),
        compiler_params=_params(("parallel",)),
    )(x, g)


def _rms_bwd(dh, x, g, dres, name, deps=()):
    T, D = x.shape
    tm = _row_tile(T)

    def body(dh_ref, x_ref, g_ref, dres_ref, *rest):
        dx_ref, dxb_ref, dg_ref = rest[len(deps):]
        i = pl.program_id(0)
        xv = x_ref[...]
        r = lax.rsqrt(jnp.mean(xv * xv, axis=-1, keepdims=True) + EPS)
        n = xv * r
        dh_v = dh_ref[...].astype(F32)
        dn = dh_v * g_ref[...]
        dx = dres_ref[...] + r * (dn - n * jnp.mean(dn * n, axis=-1, keepdims=True))
        dx_ref[...] = dx
        dxb_ref[...] = dx.astype(BF16)
        dg = jnp.sum(dh_v * n, axis=0, keepdims=True)

        @pl.when(i == 0)
        def _():
            dg_ref[...] = dg

        @pl.when(i > 0)
        def _():
            dg_ref[...] += dg

    row = pl.BlockSpec((tm, D), lambda i: (i, 0))
    vec = pl.BlockSpec((1, D), lambda i: (0, 0))
    return pl.pallas_call(
        body, name=name, grid=(T // tm,),
        in_specs=[row, row, vec, row] + [pl.BlockSpec(memory_space=pl.ANY)] * len(deps),
        out_specs=[row, row, vec],
        out_shape=[jax.ShapeDtypeStruct((T, D), F32), jax.ShapeDtypeStruct((T, D), BF16),
                   jax.ShapeDtypeStruct((1, D), F32)],
        compiler_params=_params(("arbitrary",)),
    )(dh, x, g, dres, *deps)


def _loss_head(x, tgt, g, name):
    T, D = x.shape
    tm = _row_tile(T)

    def body(x_ref, t_ref, g_ref, dx_ref, dxb_ref, dg_ref, l_ref):
        i = pl.program_id(0)
        xv = x_ref[...]
        gv = g_ref[...]
        r = lax.rsqrt(jnp.mean(xv * xv, axis=-1, keepdims=True) + EPS)
        n = xv * r
        e = n * gv - t_ref[...]
        dy = e * (1.0 / D)
        dn = dy * gv
        dx = r * (dn - n * jnp.mean(dn * n, axis=-1, keepdims=True))
        dx_ref[...] = dx
        dxb_ref[...] = dx.astype(BF16)
        dg = jnp.sum(dy * n, axis=0, keepdims=True)
        per_tok = jnp.mean(e * e, axis=-1, keepdims=True)
        lv = jnp.broadcast_to(0.5 * jnp.sum(per_tok, axis=0, keepdims=True), (1, LANE))

        @pl.when(i == 0)
        def _():
            dg_ref[...] = dg
            l_ref[...] = lv

        @pl.when(i > 0)
        def _():
            dg_ref[...] += dg
            l_ref[...] += lv

    row = pl.BlockSpec((tm, D), lambda i: (i, 0))
    vec = pl.BlockSpec((1, D), lambda i: (0, 0))
    return pl.pallas_call(
        body, name=name, grid=(T // tm,),
        in_specs=[row, row, vec],
        out_specs=[row, row, vec, pl.BlockSpec((1, LANE), lambda i: (0, 0))],
        out_shape=[jax.ShapeDtypeStruct((T, D), F32), jax.ShapeDtypeStruct((T, D), BF16),
                   jax.ShapeDtypeStruct((1, D), F32), jax.ShapeDtypeStruct((1, LANE), F32)],
        compiler_params=_params(("arbitrary",)),
    )(x, tgt, g)


POOL_WINDOWS = (2, 4, 8, 16)


def _pool_offsets(w):
    return range(-(w // 2), w - w // 2)


def _pool_cnt(r0, w, T, shape):
    t = r0 + lax.broadcasted_iota(jnp.int32, shape, 0)
    lo = jnp.maximum(t - w // 2, 0)
    hi = jnp.minimum(t + (w - w // 2 - 1), T - 1)
    return (hi - lo + 1).astype(F32)


def _pooled(pad_ref, r0, w, T):
    ext = _ext(pad_ref, r0)
    s = None
    for o in _pool_offsets(w):
        tap = _tap(ext, o)
        s = tap if s is None else s + tap
    cur = ext[HALO:HALO + CH]
    return s / _pool_cnt(r0, w, T, cur.shape) - cur


def _pool_fwd(P, w_pool, scale, name):
    T = P.shape[0]
    G, PG, _ = w_pool.shape

    def body(a_ref, w_ref, s_ref, o_ref, pad_ref):
        g = pl.program_id(0)
        _fill_pad(pad_ref, lambda r0: a_ref[pl.ds(r0, CH), :].astype(F32), T)
        for gi, w in enumerate(POOL_WINDOWS):
            @pl.when(g == gi)
            def _(w=w):
                def chunk(r0, carry):
                    pooled = _pooled(pad_ref, r0, w, T).astype(BF16)
                    y = jnp.dot(pooled, w_ref[...], preferred_element_type=F32) * s_ref[...]
                    o_ref[pl.ds(r0, CH), :] = y.astype(BF16)
                    return carry

                _chunks(T, chunk)

    return pl.pallas_call(
        body, name=name, grid=(G,),
        in_specs=[pl.BlockSpec((T, PG), lambda g: (0, g)),
                  pl.BlockSpec((None, PG, PG), lambda g: (g, 0, 0)),
                  pl.BlockSpec((1, PG), lambda g: (0, g))],
        out_specs=pl.BlockSpec((T, PG), lambda g: (0, g)),
        out_shape=jax.ShapeDtypeStruct((T, G * PG), BF16),
        scratch_shapes=[pltpu.VMEM((T + 2 * HALO, PG), F32)],
        compiler_params=_params(("parallel",)),
    )(P, w_pool, scale)


def _pool_bwd(P, dy, w_pool, scale, name):
    T = P.shape[0]
    G, PG, _ = w_pool.shape

    def body(a_ref, dy_ref, w_ref, s_ref, da_ref, dw_ref, ds_ref, pad_ref, gpad_ref):
        g = pl.program_id(0)
        gpad_ref[pl.ds(0, HALO), :] = jnp.zeros((HALO, PG), F32)
        gpad_ref[pl.ds(HALO + T, HALO), :] = jnp.zeros((HALO, PG), F32)
        _fill_pad(pad_ref, lambda r0: a_ref[pl.ds(r0, CH), :].astype(F32), T)
        for gi, w in enumerate(POOL_WINDOWS):
            @pl.when(g == gi)
            def _(w=w):
                def chunk1(r0, carry):
                    dw, ds = carry
                    pooled = _pooled(pad_ref, r0, w, T).astype(BF16)
                    ypre = jnp.dot(pooled, w_ref[...], preferred_element_type=F32)
                    dyv = dy_ref[pl.ds(r0, CH), :].astype(F32)
                    ds = ds + jnp.sum(dyv * ypre, axis=0, keepdims=True)
                    dyp = (dyv * s_ref[...]).astype(BF16)
                    dw = dw + lax.dot_general(pooled, dyp, _DIMS["tn"], preferred_element_type=F32)
                    dpool = lax.dot_general(dyp, w_ref[...], _DIMS["nt"], preferred_element_type=F32)
                    gpad_ref[pl.ds(HALO + r0, CH), :] = dpool / _pool_cnt(r0, w, T, dpool.shape)
                    return dw, ds

                dw, ds = _chunks(T, chunk1, (jnp.zeros((PG, PG), F32), jnp.zeros((1, PG), F32)))
                dw_ref[...] = dw
                ds_ref[...] = ds

                def chunk2(r0, carry):
                    ext = _ext(gpad_ref, r0)
                    cur = ext[HALO:HALO + CH]
                    acc = None
                    for o in _pool_offsets(w):
                        tap = _tap(ext, -o)
                        acc = tap if acc is None else acc + tap
                    da_ref[pl.ds(r0, CH), :] = (acc - cur * _pool_cnt(r0, w, T, cur.shape)).astype(BF16)
                    return carry

                _chunks(T, chunk2)

    col = pl.BlockSpec((T, PG), lambda g: (0, g))
    return pl.pallas_call(
        body, name=name, grid=(G,),
        in_specs=[col, col, pl.BlockSpec((None, PG, PG), lambda g: (g, 0, 0)), pl.BlockSpec((1, PG), lambda g: (0, g))],
        out_specs=[col, pl.BlockSpec((None, PG, PG), lambda g: (g, 0, 0)), pl.BlockSpec((1, PG), lambda g: (0, g))],
        out_shape=[jax.ShapeDtypeStruct((T, G * PG), BF16), jax.ShapeDtypeStruct((G, PG, PG), F32),
                   jax.ShapeDtypeStruct((1, G * PG), F32)],
        scratch_shapes=[pltpu.VMEM((T + 2 * HALO, PG), F32), pltpu.VMEM((T + 2 * HALO, PG), F32)],
        compiler_params=_params(("parallel",)),
    )(P, dy, w_pool, scale)


SGU_CHUNK = 128


def _sgu_common(u_ref, v_ref, gs_ref):
    up = u_ref[...].astype(F32)
    vp = v_ref[...].astype(F32)
    gv = _gelu(vp)
    rv = lax.rsqrt(jnp.mean(gv * gv, axis=-1, keepdims=True) + EPS)
    nrm = gv * rv
    return up, vp, nrm, rv, (nrm * gs_ref[...]).astype(BF16)


def _sgu_fwd(P, sgu_w, b_exp, gs, name):
    T = P.shape[0]
    G, _, SG = b_exp.shape
    DP = G * SG
    tm = _row_tile(T)

    def body(u_ref, v_ref, w_ref, b_ref, gs_ref, o_ref):
        up, _, _, _, vn = _sgu_common(u_ref, v_ref, gs_ref)
        gu = _gelu(up)
        for n in range(tm // SGU_CHUNK):
            rows = slice(n * SGU_CHUNK, (n + 1) * SGU_CHUNK)
            for g in range(G):
                cols = slice(g * SG, (g + 1) * SG)
                z = jnp.dot(w_ref[g], vn[rows, cols], preferred_element_type=F32) + b_ref[g]
                o_ref[rows, cols] = (gu[rows, cols] * z).astype(BF16)

    return pl.pallas_call(
        body, name=name, grid=(T // tm,),
        in_specs=[pl.BlockSpec((tm, DP), lambda i: (i, 1)), pl.BlockSpec((tm, DP), lambda i: (i, 2)),
                  pl.BlockSpec(sgu_w.shape, lambda i: (0, 0, 0)), pl.BlockSpec(b_exp.shape, lambda i: (0, 0, 0)),
                  pl.BlockSpec((1, DP), lambda i: (0, 0))],
        out_specs=pl.BlockSpec((tm, DP), lambda i: (i, 0)),
        out_shape=jax.ShapeDtypeStruct((T, DP), BF16),
        compiler_params=_params(("parallel",)),
    )(P, P, sgu_w, b_exp, gs)


def _sgu_bwd(P, dy, da, sgu_w, b_exp, gs, name):
    T = P.shape[0]
    G, _, SG = b_exp.shape
    DP = G * SG
    tm = _row_tile(T)

    def body(u_ref, v_ref, dy_ref, da_ref, w_ref, b_ref, gs_ref, o_ref, dw_ref, db_ref, dgs_ref, dzs_ref):
        i = pl.program_id(0)
        up, vp, nrm, rv, vn = _sgu_common(u_ref, v_ref, gs_ref)
        gu = _gelu(up)
        dyv = dy_ref[...].astype(F32)
        o_ref[:, 0:DP] = da_ref[...]

        @pl.when(i == 0)
        def _():
            dw_ref[...] = jnp.zeros(dw_ref.shape, F32)
            dzs_ref[...] = jnp.zeros(dzs_ref.shape, F32)
            dgs_ref[...] = jnp.zeros(dgs_ref.shape, F32)

        dgs = jnp.zeros((1, DP), F32)
        for n in range(tm // SGU_CHUNK):
            rows = slice(n * SGU_CHUNK, (n + 1) * SGU_CHUNK)
            dvn_parts = []
            for g in range(G):
                cols = slice(g * SG, (g + 1) * SG)
                vng = vn[rows, cols]
                z = jnp.dot(w_ref[g], vng, preferred_element_type=F32) + b_ref[g]
                dyg = dyv[rows, cols]
                du = dyg * z
                o_ref[rows, DP + g * SG:DP + (g + 1) * SG] = (du * _gelu_grad(up[rows, cols])).astype(BF16)
                dz = dyg * gu[rows, cols]
                dzb = dz.astype(BF16)
                dzs_ref[g] += dz
                dw_ref[g] += lax.dot_general(dzb, vng, _DIMS["nt"], preferred_element_type=F32)
                dvn_parts.append(lax.dot_general(w_ref[g], dzb, _DIMS["tn"], preferred_element_type=F32))
            dvn = jnp.concatenate(dvn_parts, axis=1)
            nr = nrm[rows]
            dgs = dgs + jnp.sum(dvn * nr, axis=0, keepdims=True)
            dn = dvn * gs_ref[...]
            dgv = rv[rows] * (dn - nr * jnp.mean(dn * nr, axis=-1, keepdims=True))
            o_ref[rows, 2 * DP:3 * DP] = (dgv * _gelu_grad(vp[rows])).astype(BF16)
        dgs_ref[...] += dgs

        @pl.when(i == T // tm - 1)
        def _():
            for g in range(G):
                db_ref[g] = jnp.broadcast_to(jnp.sum(dzs_ref[g], axis=1, keepdims=True), (SGU_CHUNK, SG))

    full3 = lambda a: pl.BlockSpec(a.shape, lambda i: (0, 0, 0))
    return pl.pallas_call(
        body, name=name, grid=(T // tm,),
        in_specs=[pl.BlockSpec((tm, DP), lambda i: (i, 1)), pl.BlockSpec((tm, DP), lambda i: (i, 2)),
                  pl.BlockSpec((tm, DP), lambda i: (i, 0)), pl.BlockSpec((tm, DP), lambda i: (i, 0)),
                  full3(sgu_w), full3(b_exp), pl.BlockSpec((1, DP), lambda i: (0, 0))],
        out_specs=[pl.BlockSpec((tm, 3 * DP), lambda i: (i, 0)), full3(sgu_w), full3(b_exp),
                   pl.BlockSpec((1, DP), lambda i: (0, 0))],
        out_shape=[jax.ShapeDtypeStruct((T, 6 * DP), BF16), jax.ShapeDtypeStruct(sgu_w.shape, F32),
                   jax.ShapeDtypeStruct(b_exp.shape, F32), jax.ShapeDtypeStruct((1, DP), F32)],
        scratch_shapes=[pltpu.VMEM(b_exp.shape, F32)],
        compiler_params=_params(("arbitrary",)),
    )(P, P, dy, da, sgu_w, b_exp, gs)


def _conv_fwd(P, cw, name):
    T = P.shape[0]
    DP = cw.shape[1]
    tc = 256
    nb = DP // tc

    def body(xc_ref, bg_ref, cg_ref, w_ref, o_ref, pad_ref):
        _fill_pad(pad_ref, lambda r0: cg_ref[pl.ds(r0, CH), :].astype(F32) * xc_ref[pl.ds(r0, CH), :].astype(F32), T)
        w = w_ref[...]

        def chunk(r0, carry):
            cq = _conv3(_ext(pad_ref, r0), w)
            o_ref[pl.ds(r0, CH), :] = (bg_ref[pl.ds(r0, CH), :].astype(F32) * cq).astype(BF16)
            return carry

        _chunks(T, chunk)

    col = lambda off: pl.BlockSpec((T, tc), lambda j: (0, off * nb + j))
    return pl.pallas_call(
        body, name=name, grid=(nb,),
        in_specs=[col(3), col(4), col(5), pl.BlockSpec((3, tc), lambda j: (0, j))],
        out_specs=pl.BlockSpec((T, tc), lambda j: (0, j)),
        out_shape=jax.ShapeDtypeStruct((T, DP), BF16),
        scratch_shapes=[pltpu.VMEM((T + 2 * HALO, tc), F32)],
        compiler_params=_params(("parallel",)),
    )(P, P, P, cw)


def _conv_bwd(P, dy, cw, dPl, name):
    T = P.shape[0]
    DP = cw.shape[1]
    tc = 256
    nb = DP // tc

    def body(xc_ref, bg_ref, cg_ref, dy_ref, w_ref, prev_ref, o_ref, dw_ref, qpad_ref, dpad_ref):
        del prev_ref
        seg = pl.program_id(1)
        w = w_ref[...]
        rows = lambda ref, r0: ref[pl.ds(r0, CH), :].astype(F32)
        _fill_pad(qpad_ref, lambda r0: rows(cg_ref, r0) * rows(xc_ref, r0), T)
        _fill_pad(dpad_ref, lambda r0: rows(dy_ref, r0) * rows(bg_ref, r0), T)

        @pl.when(seg == 0)
        def _():
            def chunk(r0, carry):
                dq = _conv3_t(_ext(dpad_ref, r0), w)
                o_ref[pl.ds(r0, CH), :] = (dq * rows(cg_ref, r0)).astype(BF16)
                dcq = rows(dy_ref, r0) * rows(bg_ref, r0)
                return [a + b for a, b in zip(carry, _conv3_dw(_ext(qpad_ref, r0), dcq))]

            dws = _chunks(T, chunk, [jnp.zeros((1, tc), F32)] * 3)
            dw_ref[...] = jnp.concatenate(dws + [jnp.zeros((5, tc), F32)], axis=0)

        @pl.when(seg == 1)
        def _():
            def chunk(r0, carry):
                cq = _conv3(_ext(qpad_ref, r0), w)
                o_ref[pl.ds(r0, CH), :] = (rows(dy_ref, r0) * cq).astype(BF16)
                return carry

            _chunks(T, chunk)

        @pl.when(seg == 2)
        def _():
            def chunk(r0, carry):
                dq = _conv3_t(_ext(dpad_ref, r0), w)
                o_ref[pl.ds(r0, CH), :] = (dq * rows(xc_ref, r0)).astype(BF16)
                return carry

            _chunks(T, chunk)

    col = lambda off: pl.BlockSpec((T, tc), lambda j, s: (0, off * nb + j))
    return pl.pallas_call(
        body, name=name, grid=(nb, 3),
        in_specs=[col(3), col(4), col(5), pl.BlockSpec((T, tc), lambda j, s: (0, j)),
                  pl.BlockSpec((3, tc), lambda j, s: (0, j)), pl.BlockSpec(memory_space=pl.ANY)],
        out_specs=[pl.BlockSpec((T, tc), lambda j, s: (0, (3 + s) * nb + j)), pl.BlockSpec((8, tc), lambda j, s: (0, j))],
        out_shape=[jax.ShapeDtypeStruct(dPl.shape, BF16), jax.ShapeDtypeStruct((8, DP), F32)],
        scratch_shapes=[pltpu.VMEM((T + 2 * HALO, tc), F32), pltpu.VMEM((T + 2 * HALO, tc), F32)],
        input_output_aliases={5: 0},
        compiler_params=_params(("parallel", "arbitrary")),
    )(P, P, P, dy, cw, dPl)


def _merge_fwd(P, ys, ws, name):
    T = P.shape[0]
    DP, D = ws[0].shape
    tm = _pick(T, (512, 256, 128))
    tn = _pick(D, (1024, 512, 256))
    goff = (6 * DP) // tn

    def body(ya, yb, yc, wa, wb, wc, ga, gb, gc, oa_ref, ob_ref, oc_ref, m_ref):
        m = None
        for y, w, g, o_ref in ((ya, wa, ga, oa_ref), (yb, wb, gb, ob_ref), (yc, wc, gc, oc_ref)):
            o = jnp.dot(y[...], w[...], preferred_element_type=F32)
            o_ref[...] = o.astype(BF16)
            t = _sigmoid(g[...].astype(F32)) * o
            m = t if m is None else m + t
        m_ref[...] = m.astype(BF16)

    yspec = pl.BlockSpec((tm, DP), lambda i, j: (i, 0))
    wspec = pl.BlockSpec((DP, tn), lambda i, j: (0, j))
    gspec = lambda b: pl.BlockSpec((tm, tn), lambda i, j: (i, goff + b * (D // tn) + j))
    ospec = pl.BlockSpec((tm, tn), lambda i, j: (i, j))
    return pl.pallas_call(
        body, name=name, grid=(T // tm, D // tn),
        in_specs=[yspec] * 3 + [wspec] * 3 + [gspec(0), gspec(1), gspec(2)],
        out_specs=[ospec] * 4,
        out_shape=[jax.ShapeDtypeStruct((T, D), BF16)] * 4,
        compiler_params=_params(("parallel", "parallel")),
    )(*ys, *ws, P, P, P)


def _merge_bwd(P, dM, os_, name):
    T, D = dM.shape
    tm = _row_tile(T)

    def body(dm_ref, oa, ob, oc, g_ref, da_ref, db_ref, dc_ref, dg_ref):
        dm = dm_ref[...].astype(F32)
        for b, (o_ref, d_ref) in enumerate(((oa, da_ref), (ob, db_ref), (oc, dc_ref))):
            s = _sigmoid(g_ref[:, b * D:(b + 1) * D].astype(F32))
            d_ref[...] = (dm * s).astype(BF16)
            dg_ref[:, b * D:(b + 1) * D] = (dm * o_ref[...].astype(F32) * s * (1.0 - s)).astype(BF16)

    row = pl.BlockSpec((tm, D), lambda i: (i, 0))
    return pl.pallas_call(
        body, name=name, grid=(T // tm,),
        in_specs=[row] * 4 + [pl.BlockSpec((tm, 3 * D), lambda i: (i, 1))],
        out_specs=[row] * 3 + [pl.BlockSpec((tm, 3 * D), lambda i: (i, 0))],
        out_shape=[jax.ShapeDtypeStruct((T, D), BF16)] * 3 + [jax.ShapeDtypeStruct((T, 3 * D), BF16)],
        compiler_params=_params(("parallel",)),
    )(dM, *os_, P)


def _glu_fwd(U, fw, name):
    T, F2 = U.shape
    F = F2 // 2
    tc = 256
    nb = F // tc

    def body(ug_ref, uv_ref, wg_ref, wv_ref, o_ref, gpad_ref, vpad_ref):
        _fill_pad(gpad_ref, lambda r0: ug_ref[pl.ds(r0, CH), :].astype(F32), T)
        _fill_pad(vpad_ref, lambda r0: uv_ref[pl.ds(r0, CH), :].astype(F32), T)
        wg, wv = wg_ref[...], wv_ref[...]

        def chunk(r0, carry):
            gate = _conv3(_ext(gpad_ref, r0), wg)
            val = _conv3(_ext(vpad_ref, r0), wv)
            o_ref[pl.ds(r0, CH), :] = (gate * _sigmoid(gate) * val).astype(BF16)
            return carry

        _chunks(T, chunk)

    return pl.pallas_call(
        body, name=name, grid=(nb,),
        in_specs=[pl.BlockSpec((T, tc), lambda j: (0, j)), pl.BlockSpec((T, tc), lambda j: (0, nb + j)),
                  pl.BlockSpec((3, tc), lambda j: (0, j)), pl.BlockSpec((3, tc), lambda j: (0, nb + j))],
        out_specs=pl.BlockSpec((T, tc), lambda j: (0, j)),
        out_shape=jax.ShapeDtypeStruct((T, F), BF16),
        scratch_shapes=[pltpu.VMEM((T + 2 * HALO, tc), F32)] * 2,
        compiler_params=_params(("parallel",)),
    )(U, U, fw, fw)


def _glu_bwd(U, dact, fw, name):
    T, F2 = U.shape
    F = F2 // 2
    tc = 128
    nb = F // tc

    def body(ug_ref, uv_ref, da_ref, wg_ref, wv_ref, dug_ref, duv_ref, dwg_ref, dwv_ref, gpad, vpad, dgpad, dvpad):
        _fill_pad(gpad, lambda r0: ug_ref[pl.ds(r0, CH), :].astype(F32), T)
        _fill_pad(vpad, lambda r0: uv_ref[pl.ds(r0, CH), :].astype(F32), T)
        wg, wv = wg_ref[...], wv_ref[...]

        def chunk1(r0, carry):
            gext, vext = _ext(gpad, r0), _ext(vpad, r0)
            gate = _conv3(gext, wg)
            val = _conv3(vext, wv)
            s = _sigmoid(gate)
            da = da_ref[pl.ds(r0, CH), :].astype(F32)
            dval = da * (gate * s)
            dgate = da * val * (s * (1.0 + gate * (1.0 - s)))
            dgpad[pl.ds(HALO + r0, CH), :] = dgate
            dvpad[pl.ds(HALO + r0, CH), :] = dval
            new = _conv3_dw(gext, dgate) + _conv3_dw(vext, dval)
            return [a + b for a, b in zip(carry, new)]

        z = jnp.zeros((HALO, tc), F32)
        for p in (dgpad, dvpad):
            p[pl.ds(0, HALO), :] = z
            p[pl.ds(HALO + T, HALO), :] = z
        dws = _chunks(T, chunk1, [jnp.zeros((1, tc), F32)] * 6)
        dwg_ref[...] = jnp.concatenate(dws[:3] + [jnp.zeros((5, tc), F32)], axis=0)
        dwv_ref[...] = jnp.concatenate(dws[3:] + [jnp.zeros((5, tc), F32)], axis=0)

        def chunk2(r0, carry):
            dug_ref[pl.ds(r0, CH), :] = _conv3_t(_ext(dgpad, r0), wg).astype(BF16)
            duv_ref[pl.ds(r0, CH), :] = _conv3_t(_ext(dvpad, r0), wv).astype(BF16)
            return carry

        _chunks(T, chunk2)

    lo = pl.BlockSpec((T, tc), lambda j: (0, j))
    hi = pl.BlockSpec((T, tc), lambda j: (0, nb + j))
    wlo = pl.BlockSpec((3, tc), lambda j: (0, j))
    whi = pl.BlockSpec((3, tc), lambda j: (0, nb + j))
    dwspec = pl.BlockSpec((8, tc), lambda j: (0, j))
    return pl.pallas_call(
        body, name=name, grid=(nb,),
        in_specs=[lo, hi, lo, wlo, whi],
        out_specs=[lo, lo, dwspec, dwspec],
        out_shape=[jax.ShapeDtypeStruct((T, F), BF16)] * 2 + [jax.ShapeDtypeStruct((8, F), F32)] * 2,
        scratch_shapes=[pltpu.VMEM((T + 2 * HALO, tc), F32)] * 4,
        compiler_params=_params(("parallel",)),
    )(U, U, dact, fw, fw)


_ANY = pl.BlockSpec(memory_space=pl.ANY)
_MESH = pl.DeviceIdType.MESH


def _mesh_pos():
    return lax.axis_index("x"), lax.axis_index("y"), lax.axis_index("c")


def _other_chips(x, y):
    return [(1 - x, y), (x, 1 - y), (1 - x, 1 - y)]


_HBM = pl.BlockSpec(memory_space=pltpu.HBM)
_SEM = pl.BlockSpec(memory_space=pltpu.SEMAPHORE)
_EFFECT = pltpu.SideEffectType.DATAFLOW_SIDE_EFFECTING


def _in_hbm(a):
    return pltpu.with_memory_space_constraint(a, pltpu.HBM)


def _split_start(groups, name):
    sizes = [len(bufs) for bufs, _, _ in groups]
    nb = sum(sizes)
    ng = len(groups)

    def body(*refs):
        ins = refs[:nb]
        sems = refs[nb:nb + 2 * ng]
        token = refs[-1]
        off = 0
        for gi, (bufs, make_copies, _) in enumerate(groups):
            for cp in make_copies(ins[off:off + len(bufs)], sems[2 * gi], sems[2 * gi + 1]):
                cp.start()
            off += len(bufs)
        token[...] = jnp.zeros(token.shape, token.dtype)

    flat = [b for bufs, _, _ in groups for b in bufs]
    sem_shapes = []
    for _, _, n in groups:
        sem_shapes += [pltpu.SemaphoreType.DMA((n,)), pltpu.SemaphoreType.DMA((n,))]
    outs = pl.pallas_call(
        body, name=name,
        in_specs=[_HBM] * nb,
        out_specs=[_SEM] * (2 * ng) + [_HBM] * nb + [pl.BlockSpec(memory_space=pltpu.VMEM)],
        out_shape=sem_shapes + [pltpu.HBM(b.shape, b.dtype) for b in flat] + [jax.ShapeDtypeStruct((8, LANE), F32)],
        input_output_aliases={i: 2 * ng + i for i in range(nb)},
        compiler_params=pltpu.CompilerParams(has_side_effects=_EFFECT),
    )(*[_in_hbm(b) for b in flat])
    res, off = [], 2 * ng
    for gi, n in enumerate(sizes):
        res.append((outs[2 * gi], outs[2 * gi + 1], list(outs[off:off + n])))
        off += n
    return res, outs[-1]


def _split_wait(started, make_copies, after, name):
    send, recv, bufs = started
    nb = len(bufs)
    after = list(after) if isinstance(after, (list, tuple)) else [after]

    def body(*refs):
        for cp in make_copies(refs[:nb], refs[nb], refs[nb + 1]):
            cp.wait_send()
            cp.wait_recv()

    outs = pl.pallas_call(
        body, name=name,
        in_specs=[_HBM] * nb + [_SEM, _SEM] + [_ANY] * len(after),
        out_specs=[_HBM] * nb,
        out_shape=[pltpu.HBM(b.shape, b.dtype) for b in bufs],
        input_output_aliases={i: i for i in range(nb)},
        compiler_params=pltpu.CompilerParams(has_side_effects=_EFFECT),
    )(*bufs, send, recv, *after)
    return list(outs)


def _copies_gather_chips(nbuf):
    def make(bufs, send, recv):
        x, y, c = _mesh_pos()
        targets = [(x, y, 1 - c)] + [(px, py, c) for px, py in _other_chips(x, y)]
        cps = []
        for a in range(nbuf):
            mine = bufs[a].at[4 * x + 2 * y + c]
            for k, to in enumerate(targets):
                cps.append(pltpu.make_async_remote_copy(src_ref=mine, dst_ref=mine, send_sem=send.at[4 * a + k],
                                                        recv_sem=recv.at[4 * a + k], device_id=to, device_id_type=_MESH))
        return cps

    return make, 4 * nbuf


def _copies_gather_forward(nbuf):
    def make(bufs, send, recv):
        x, y, c = _mesh_pos()
        cps = []
        for a in range(nbuf):
            for j, (px, py) in enumerate(_other_chips(x, y)):
                slot = bufs[a].at[4 * px + 2 * py + c]
                cps.append(pltpu.make_async_remote_copy(src_ref=slot, dst_ref=slot, send_sem=send.at[3 * a + j],
                                                        recv_sem=recv.at[3 * a + j], device_id=(x, y, 1 - c),
                                                        device_id_type=_MESH))
        return cps

    return make, 3 * nbuf


def _copies_rs_sibling(n):
    def make(bufs, send, recv):
        x, y, c = _mesh_pos()
        cps = []
        for a in range(n):
            for q in range(4):
                cps.append(pltpu.make_async_remote_copy(
                    src_ref=bufs[a].at[2 * q + (1 - c)], dst_ref=bufs[n + a].at[q], send_sem=send.at[4 * a + q],
                    recv_sem=recv.at[4 * a + q], device_id=(x, y, 1 - c), device_id_type=_MESH))
        return cps

    return make, 4 * n


def _copies_rs_chips(n):
    def make(bufs, send, recv):
        x, y, c = _mesh_pos()
        cps = []
        for a in range(n):
            for j, (px, py) in enumerate(_other_chips(x, y)):
                cps.append(pltpu.make_async_remote_copy(
                    src_ref=bufs[a].at[2 * px + py], dst_ref=bufs[n + a].at[j], send_sem=send.at[3 * a + j],
                    recv_sem=recv.at[3 * a + j], device_id=(px, py, c), device_id_type=_MESH))
        return cps

    return make, 3 * n


def _cast_place(stacked, layer, slot, dtype, name, deps=()):
    _, R, C = stacked.shape
    tr = _rows_tile(R, C)

    def body(slot_ref, x_ref, *rest):
        del slot_ref
        rest[-1][...] = x_ref[...].astype(dtype)

    return pl.pallas_call(
        body, name=name,
        grid_spec=pltpu.PrefetchScalarGridSpec(
            num_scalar_prefetch=1, grid=(R // tr,),
            in_specs=[pl.BlockSpec((None, tr, C), lambda i, s: (layer, i, 0))] + [_ANY] * len(deps),
            out_specs=pl.BlockSpec((None, tr, C), lambda i, s: (s[0], i, 0))),
        out_shape=jax.ShapeDtypeStruct((N_DEV, R, C), dtype),
        compiler_params=_params(("parallel",)),
    )(slot, stacked, *deps)


def _rows_tile(R, C, budget=2 * MIB):
    if R * C * 4 <= budget:
        return R
    for t in (2048, 1024, 704, 512, 352, 256, 128, 64, 32, 16, 8):
        if R % t == 0 and t * C * 4 <= budget:
            return t
    return R


def _pair_sum(grad, recv, core, name):
    _, R, C = grad.shape
    tr = _rows_tile(R, C)

    def body(c_ref, g_ref, r_ref, o_ref):
        del c_ref
        o_ref[...] = (g_ref[...].astype(F32) + r_ref[...].astype(F32)).astype(BF16)

    return pl.pallas_call(
        body, name=name,
        grid_spec=pltpu.PrefetchScalarGridSpec(
            num_scalar_prefetch=1, grid=(4, R // tr),
            in_specs=[pl.BlockSpec((None, tr, C), lambda q, i, c: (2 * q + c[0], i, 0)),
                      pl.BlockSpec((None, tr, C), lambda q, i, c: (q, i, 0))],
            out_specs=pl.BlockSpec((None, tr, C), lambda q, i, c: (q, i, 0))),
        out_shape=jax.ShapeDtypeStruct((4, R, C), BF16),
        compiler_params=_params(("parallel", "parallel")),
    )(core, grad, recv)


ADAM_LR = 0.001
ADAM_B1 = 0.9
ADAM_B2 = 0.999
ADAM_EPS = 1e-08
ADAM_WD = 0.01
ADAM_STEP = 10


def _adamw(w, g, m, v):
    m = ADAM_B1 * m + (1.0 - ADAM_B1) * g
    v = ADAM_B2 * v + (1.0 - ADAM_B2) * (g * g)
    m_hat = m / (1.0 - ADAM_B1 ** ADAM_STEP)
    v_hat = v / (1.0 - ADAM_B2 ** ADAM_STEP)
    delta = -ADAM_LR * (m_hat / (jnp.sqrt(v_hat) + ADAM_EPS) + ADAM_WD * w)
    return delta, m, v


def _adamw_big(pair, recv, w, m, v, chip, layer, prev, name):
    L, R, C = w.shape
    tr = _rows_tile(R, C)
    has_prev = prev is not None

    def body(chip_ref, p_ref, r0_ref, r1_ref, r2_ref, w_ref, m_ref, v_ref, *rest):
        del chip_ref
        g_ref, d_ref, mo_ref, vo_ref = rest[-4:]
        g = ((p_ref[...].astype(F32) + r0_ref[...].astype(F32)) + r1_ref[...].astype(F32)) + r2_ref[...].astype(F32)
        delta, m2, v2 = _adamw(w_ref[...], g, m_ref[...], v_ref[...])
        g_ref[...] = g
        d_ref[...] = delta
        mo_ref[...] = m2
        vo_ref[...] = v2

    lay = pl.BlockSpec((None, tr, C), lambda i, c: (layer, i, 0))
    rspec = lambda j: pl.BlockSpec((None, tr, C), lambda i, c: (j, i, 0))
    in_specs = [pl.BlockSpec((None, tr, C), lambda i, c: (c[0], i, 0)), rspec(0), rspec(1), rspec(2), lay, lay, lay]
    args = [chip, pair, recv, recv, recv, w, m, v]
    aliases = {}
    if has_prev:
        in_specs += [_ANY] * 4
        args += list(prev)
        aliases = {8 + k: k for k in range(4)}
    return pl.pallas_call(
        body, name=name,
        grid_spec=pltpu.PrefetchScalarGridSpec(
            num_scalar_prefetch=1, grid=(R // tr,), in_specs=in_specs, out_specs=[lay] * 4),
        out_shape=[jax.ShapeDtypeStruct((L, R, C), F32)] * 4,
        input_output_aliases=aliases,
        compiler_params=_params(("parallel",)),
    )(*args)


def _sum_slots(gathered, name):
    _, R, C = gathered.shape
    tr = _rows_tile(R, C, 2 * MIB)

    def body(x_ref, o_ref):
        s = x_ref[0]
        for k in range(1, N_DEV):
            s = s + x_ref[k]
        o_ref[...] = s

    return pl.pallas_call(
        body, name=name, grid=(R // tr,),
        in_specs=[pl.BlockSpec((N_DEV, tr, C), lambda i: (0, i, 0))],
        out_specs=pl.BlockSpec((tr, C), lambda i: (i, 0)),
        out_shape=jax.ShapeDtypeStruct((R, C), F32),
        compiler_params=_params(("parallel",)),
    )(gathered)


def _adamw_small(w, g, m, v, name):
    R, C = w.shape
    tr = _rows_tile(R, C, 2 * MIB)

    def body(w_ref, g_ref, m_ref, v_ref, d_ref, mo_ref, vo_ref):
        d_ref[...], mo_ref[...], vo_ref[...] = _adamw(w_ref[...], g_ref[...], m_ref[...], v_ref[...])

    spec = pl.BlockSpec((tr, C), lambda i: (i, 0))
    return pl.pallas_call(
        body, name=name, grid=(R // tr,),
        in_specs=[spec] * 4, out_specs=[spec] * 3,
        out_shape=[jax.ShapeDtypeStruct((R, C), F32)] * 3,
        compiler_params=_params(("parallel",)),
    )(w, g, m, v)


def _pack(arrs):
    parts = []
    for a in arrs:
        flat = a.reshape(-1)
        pad = (-flat.shape[0]) % (8 * LANE)
        if pad:
            flat = jnp.pad(flat, (0, pad))
        parts.append(flat.reshape(-1, LANE))
    return jnp.concatenate(parts, axis=0)


def _unpack(packed, shapes):
    out, r = [], 0
    for s in shapes:
        n = 1
        for d in s:
            n *= d
        rows = -(-n // (8 * LANE)) * 8
        out.append(packed[r:r + rows].reshape(-1)[:n].reshape(s))
        r += rows
    return out


def _to_slices(a, ns):
    K = a.shape[0]
    return a.reshape(K, N_DEV, ns).transpose(1, 0, 2)


def _from_slices(a):
    _, K, ns = a.shape
    return a.transpose(1, 0, 2).reshape(K, N_DEV * ns)


def kernel(x, norm1_g, w_in, w_pool, pool_scale, sgu_norm_g, sgu_w, sgu_b, conv_w, w_pool_out, w_sgu_out, w_conv_out, w_o, norm2_g, w_up, ffn_conv_w, w_down, final_g, loss_target, m_norm1_g, m_w_in, m_w_pool, m_pool_scale, m_sgu_norm_g, m_sgu_w, m_sgu_b, m_conv_w, m_w_pool_out, m_w_sgu_out, m_w_conv_out, m_w_o, m_norm2_g, m_w_up, m_ffn_conv_w, m_w_down, m_final_g, v_norm1_g, v_w_in, v_w_pool, v_pool_scale, v_sgu_norm_g, v_sgu_w, v_sgu_b, v_conv_w, v_w_pool_out, v_w_sgu_out, v_w_conv_out, v_w_o, v_norm2_g, v_w_up, v_ffn_conv_w, v_w_down, v_final_g):
    L = norm1_g.shape[0]
    x0 = x[0]
    tgt = loss_target[0]
    G, PG = w_pool.shape[1], w_pool.shape[3]
    F = w_down.shape[1] * N_DEV
    px, py, pc = _mesh_pos()
    me = 4 * px + 2 * py + pc
    core = jnp.reshape(pc, (1,)).astype(jnp.int32)
    chip = jnp.reshape(2 * px + py, (1,)).astype(jnp.int32)

    slot = jnp.reshape(me, (1,)).astype(jnp.int32)
    stage1, group_sizes, started1, tok0 = [], [], [], None
    for l in range(L):
        grp = [[("w_in", w_in, BF16)],
               [("w_pool_out", w_pool_out, BF16), ("w_sgu_out", w_sgu_out, BF16), ("w_conv_out", w_conv_out, BF16),
                ("w_pool", w_pool.reshape(L, -1, PG), BF16), ("conv_w", conv_w, F32), ("w_o", w_o, BF16)],
               [("w_up", w_up, BF16), ("ffn_conv_w", ffn_conv_w, F32)],
               [("w_down", w_down, BF16)]]
        for members in grp:
            deps = [] if tok0 is None else [tok0]
            bufs = [_cast_place(a, l, slot, dt, f"place_{n}_{l}", deps) for n, a, dt in members]
            stage1.append((bufs,) + _copies_gather_chips(len(members)))
            group_sizes.append(len(members))
            if tok0 is None:
                started1, tok0 = _split_start(stage1, "ag_start_first")
    rest, _ = _split_start(stage1[1:], "ag_start")
    started1 = started1 + rest
    started2 = {}

    def gather_forward(k, after):
        n = group_sizes[k]
        bufs = _split_wait(started1[k], stage1[k][1], after, f"ag_wait1_{k}")
        st, tok = _split_start([(bufs,) + _copies_gather_forward(n)], f"ag_fwd_{k}")
        started2[k] = st[0]
        return tok

    def gather_done(k, after):
        return _split_wait(started2.pop(k), _copies_gather_forward(group_sizes[k])[0], after, f"ag_wait2_{k}")

    W, saved = [], []
    xc = x0
    for l in range(L):
        w = dict(
            sguw=sgu_w[l].astype(BF16),
            bexp=jnp.broadcast_to(sgu_b[l].T[:, :, None],
                                  (sgu_b.shape[2], sgu_b.shape[1], sgu_norm_g.shape[1] // sgu_b.shape[2])),
            g1=norm1_g[l][None], g2=norm2_g[l][None], gs=sgu_norm_g[l][None], scale=pool_scale[l][None])
        ka, kb, kc, kd = 4 * l, 4 * l + 1, 4 * l + 2, 4 * l + 3
        h1 = _rms_fwd(xc, w["g1"], f"rms1_fwd_{l}")
        if l == 0:
            tok = gather_forward(ka, h1)
        (w["win3"],) = gather_done(ka, h1)
        if l > 0:
            tok = gather_forward(kb, h1)
        P = _matmul(h1, w["win3"], mode="nn", out_dtype=BF16, name=f"proj_in_{l}", b_slices=(0, N_DEV), deps=[tok])
        if l == 0:
            gather_forward(kb, P)
        wa, wb, wc, wp, cw, wo = gather_done(kb, P)
        w.update(wa=_from_slices(wa), wb=_from_slices(wb), wc=_from_slices(wc), cw=_from_slices(cw),
                 wpool=wp.reshape(N_DEV, G, -1, PG).transpose(1, 0, 2, 3).reshape(G, PG, PG),
                 wo=wo.reshape(-1, wo.shape[2]))
        ya = _pool_fwd(P, w["wpool"], w["scale"], f"pool_fwd_{l}")
        yb = _sgu_fwd(P, w["sguw"], w["bexp"], w["gs"], f"sgu_fwd_{l}")
        yc = _conv_fwd(P, w["cw"], f"conv_fwd_{l}")
        oa, ob, oc, M = _merge_fwd(P, (ya, yb, yc), (w["wa"], w["wb"], w["wc"]), f"merge_fwd_{l}")
        tok = gather_forward(kc, M)
        x1 = _matmul(M, w["wo"], mode="nn", out_dtype=F32, name=f"proj_o_{l}", res=xc, deps=[tok])
        h2 = _rms_fwd(x1, w["g2"], f"rms2_fwd_{l}")
        w["wup3"], fw = gather_done(kc, h2)
        w["fw"] = _from_slices(fw)
        tok = gather_forward(kd, h2)
        U = _matmul(h2, w["wup3"], mode="nn", out_dtype=BF16, name=f"proj_up_{l}", b_slices=(0, N_DEV), deps=[tok])
        act = _glu_fwd(U, w["fw"], f"glu_fwd_{l}")
        (wdown,) = gather_done(kd, act)
        w["wdown"] = wdown.reshape(-1, wdown.shape[2])
        deps = [gather_forward(4 * (l + 1), act)] if l + 1 < L else []
        x2 = _matmul(act, w["wdown"], mode="nn", out_dtype=F32, name=f"proj_down_{l}", res=x1, deps=deps,
                     tm=_pick(x1.shape[0], (512, 256)))
        if deps:
            tok = deps[0]
        W.append(w)
        saved.append(dict(x0=xc, h1=h1, P=P, ya=ya, yb=yb, yc=yc, oa=oa, ob=ob, oc=oc, M=M, x1=x1, h2=h2, U=U, act=act))
        xc = x2

    dx, dxb, d_final_g, loss_part = _loss_head(xc, tgt, final_g[None], "loss_head")
    loss = lax.psum(loss_part[0, 0], ("x", "y", "c"))

    wmv = dict(w_in=(w_in, m_w_in, v_w_in), w_up=(w_up, m_w_up, v_w_up), w_o=(w_o, m_w_o, v_w_o),
               w_down=(w_down, m_w_down, v_w_down), w_pool_out=(w_pool_out, m_w_pool_out, v_w_pool_out),
               w_sgu_out=(w_sgu_out, m_w_sgu_out, v_w_sgu_out), w_conv_out=(w_conv_out, m_w_conv_out, v_w_conv_out),
               w_pool=(w_pool, m_w_pool, v_w_pool))
    adam_out = {}

    def rs_begin(tag, layer, names, grads):
        n = len(grads)
        lands = [lax.empty((4,) + g.shape[1:], BF16) for g in grads]
        st, tok = _split_start([(list(grads) + lands,) + _copies_rs_sibling(n)], f"rs_sib_{tag}")
        return dict(tag=tag, layer=layer, names=names, n=n, st=st[0]), tok

    def rs_to_chips(state, after):
        n, tag = state["n"], state["tag"]
        bufs = _split_wait(state["st"], _copies_rs_sibling(n)[0], after, f"rs_sibw_{tag}")
        pairs = [_pair_sum(g, r, core, f"pair_{tag}_{i}") for i, (g, r) in enumerate(zip(bufs[:n], bufs[n:]))]
        lands = [lax.empty((3,) + p.shape[1:], BF16) for p in pairs]
        st, tok = _split_start([(pairs + lands,) + _copies_rs_chips(n)], f"rs_chips_{tag}")
        state["st"] = st[0]
        return tok

    def rs_finish(state, after):
        n, tag, layer = state["n"], state["tag"], state["layer"]
        bufs = _split_wait(state["st"], _copies_rs_chips(n)[0], after, f"rs_chipsw_{tag}")
        for name, pair, recv in zip(state["names"], bufs[:n], bufs[n:]):
            wk, mk, vk = (a.reshape(L, -1, a.shape[-1]) for a in wmv[name])
            adam_out[name] = _adamw_big(pair, recv, wk, mk, vk, chip, layer, adam_out.get(name), f"adamw_{name}_{layer}")

    small = [None] * L
    pending = []
    deps = []
    ns_out = w_pool_out.shape[2]
    for l in reversed(range(L)):
        w, s = W[l], saved[l]
        dact = _matmul(dxb, w["wdown"], mode="nt", out_dtype=BF16, name=f"d_act_{l}", tn=_pick(F, (512, 256)), deps=deps)
        g_down = _matmul(s["act"], dxb, mode="tn", out_dtype=BF16, name=f"g_down_{l}", tm=_pick(F, (512, 256)))
        dug, duv, dfwg, dfwv = _glu_bwd(s["U"], dact, w["fw"], f"glu_bwd_{l}")
        dh2 = _matmul(dug, w["wup3"], mode="nt", out_dtype=F32, name=f"d_h2a_{l}", b_slices=(0, 4))
        dh2 = _matmul(duv, w["wup3"], mode="nt", out_dtype=F32, name=f"d_h2b_{l}", b_slices=(4, 4), res=dh2)
        g_up = _matmul(s["h2"], dug, mode="tn", out_dtype=BF16, name=f"g_upa_{l}", out_slices=(0, 4), tm=512)
        g_up = _matmul(s["h2"], duv, mode="tn", out_dtype=BF16, name=f"g_upb_{l}", out_slices=(4, 4), prev=g_up, tm=512)
        ra, tok = rs_begin(f"a{l}", l, ["w_down", "w_up"], [g_down.reshape(N_DEV, -1, g_down.shape[1]), g_up])
        if pending:
            rs_finish(pending.pop(0), g_up)
        dx1, dx1b, dg2 = _rms_bwd(dh2, s["x1"], w["g2"], dx, f"rms2_bwd_{l}")
        dM = _matmul(dx1b, w["wo"], mode="nt", out_dtype=BF16, name=f"d_m_{l}", deps=[tok])
        g_o = _matmul(s["M"], dx1b, mode="tn", out_dtype=BF16, name=f"g_o_{l}")
        doa, dob, doc, dPg = _merge_bwd(s["P"], dM, (s["oa"], s["ob"], s["oc"]), f"merge_bwd_{l}")
        tok = rs_to_chips(ra, dPg)
        if pending:
            rs_finish(pending.pop(0), dPg)
        dya = _matmul(doa, w["wa"], mode="nt", out_dtype=BF16, name=f"d_ya_{l}", deps=[tok])
        dyb = _matmul(dob, w["wb"], mode="nt", out_dtype=BF16, name=f"d_yb_{l}")
        dyc = _matmul(doc, w["wc"], mode="nt", out_dtype=BF16, name=f"d_yc_{l}")
        g_a = _matmul(s["ya"], doa, mode="tn", out_dtype=BF16, name=f"g_a_{l}")
        g_b = _matmul(s["yb"], dob, mode="tn", out_dtype=BF16, name=f"g_b_{l}")
        g_c = _matmul(s["yc"], doc, mode="tn", out_dtype=BF16, name=f"g_c_{l}")
        da, g_pool, dscale = _pool_bwd(s["P"], dya, w["wpool"], w["scale"], f"pool_bwd_{l}")
        g_pool_s = g_pool.reshape(G, N_DEV, -1, PG).transpose(1, 0, 2, 3).reshape(N_DEV, -1, PG).astype(BF16)
        rb, tok = rs_begin(f"b{l}", l, ["w_o", "w_pool_out", "w_sgu_out", "w_conv_out", "w_pool"],
                           [g_o.reshape(N_DEV, -1, g_o.shape[1]), _to_slices(g_a, ns_out), _to_slices(g_b, ns_out),
                            _to_slices(g_c, ns_out), g_pool_s])
        if pending:
            rs_finish(pending.pop(0), da)
        dPl, g_sguw, db_exp, dgs = _sgu_bwd(s["P"], dyb, da, w["sguw"], w["bexp"], w["gs"], f"sgu_bwd_{l}")
        dPl, dcw = _conv_bwd(s["P"], dyc, w["cw"], dPl, f"conv_bwd_{l}")
        tok2 = rs_to_chips(rb, dPl)
        g_in = _matmul(s["h1"], dPl, mode="tn", out_dtype=BF16, name=f"g_ina_{l}", out_slices=(0, 4), deps=[tok, tok2])
        g_in = _matmul(s["h1"], dPg, mode="tn", out_dtype=BF16, name=f"g_inb_{l}", out_slices=(4, 4), prev=g_in)
        rc, tok = rs_begin(f"c{l}", l, ["w_in"], [g_in])
        dh1 = _matmul(dPl, w["win3"], mode="nt", out_dtype=F32, name=f"d_h1a_{l}", b_slices=(0, 4), deps=[tok])
        tok = rs_to_chips(rc, dh1)
        dh1 = _matmul(dPg, w["win3"], mode="nt", out_dtype=F32, name=f"d_h1b_{l}", b_slices=(4, 4), res=dh1, deps=[tok])
        dx, dxb, dg1 = _rms_bwd(dh1, s["x0"], w["g1"], dx1, f"rms1_bwd_{l}")
        deps = []
        pending = [ra, rb, rc]
        small[l] = dict(norm1_g=dg1[0], pool_scale=dscale[0], sgu_norm_g=dgs[0], sgu_w=g_sguw,
                        sgu_b=db_exp[:, :, 0].T, conv_w=dcw[:3], norm2_g=dg2[0],
                        ffn_conv_w=jnp.concatenate([dfwg[:3], dfwv[:3]], axis=1))
    grad_x = dx[None]

    snames = ["norm1_g", "pool_scale", "sgu_norm_g", "sgu_w", "sgu_b", "conv_w", "norm2_g", "ffn_conv_w"]
    sparts = [jnp.stack([small[l][n] for l in range(L)]) for n in snames] + [d_final_g[0]]
    snames = snames + ["final_g"]
    packed = _pack(sparts)
    sbuf = _cast_place(packed[None], 0, slot, F32, "place_small")
    st, tok = _split_start([([sbuf],) + _copies_gather_chips(1)], "ag_small_start")
    rs_finish(pending[0], tok)
    bufs = _split_wait(st[0], _copies_gather_chips(1)[0], [adam_out[n][0] for n in pending[0]["names"]],
                       "ag_small_wait1")
    st, tok = _split_start([(bufs,) + _copies_gather_forward(1)], "ag_small_fwd")
    rs_finish(pending[1], tok)
    (gathered_small,) = _split_wait(st[0], _copies_gather_forward(1)[0],
                                    [adam_out[n][0] for n in pending[1]["names"]], "ag_small_wait2")
    total = _sum_slots(gathered_small, "sum_small")
    sgrads = dict(zip(snames, _unpack(total, [p.shape for p in sparts])))
    for n, width in (("conv_w", conv_w.shape[2]), ("ffn_conv_w", ffn_conv_w.shape[2])):
        sgrads[n] = lax.dynamic_slice_in_dim(sgrads[n], me * width, width, axis=2)
    sw = dict(norm1_g=(norm1_g, m_norm1_g, v_norm1_g), pool_scale=(pool_scale, m_pool_scale, v_pool_scale),
              sgu_norm_g=(sgu_norm_g, m_sgu_norm_g, v_sgu_norm_g), sgu_w=(sgu_w, m_sgu_w, v_sgu_w),
              sgu_b=(sgu_b, m_sgu_b, v_sgu_b), conv_w=(conv_w, m_conv_w, v_conv_w),
              norm2_g=(norm2_g, m_norm2_g, v_norm2_g), ffn_conv_w=(ffn_conv_w, m_ffn_conv_w, v_ffn_conv_w),
              final_g=(final_g, m_final_g, v_final_g))
    shapes = [sw[n][0].shape for n in snames]
    upd = _adamw_small(_pack([sw[n][0] for n in snames]), _pack([sgrads[n] for n in snames]),
                       _pack([sw[n][1] for n in snames]), _pack([sw[n][2] for n in snames]), "adamw_small")
    sdelta, sm, sv = (dict(zip(snames, _unpack(u, shapes))) for u in upd)
    res = {n: [sgrads[n], sdelta[n], sm[n], sv[n]] for n in snames}

    rs_finish(pending[2], [upd[0]] + [o[0] for o in adam_out.values()])
    for n, outs in adam_out.items():
        res[n] = [o.reshape(wmv[n][0].shape) for o in outs]

    order = ["norm1_g", "w_in", "w_pool", "pool_scale", "sgu_norm_g", "sgu_w", "sgu_b", "conv_w", "w_pool_out",
             "w_sgu_out", "w_conv_out", "w_o", "norm2_g", "w_up", "ffn_conv_w", "w_down", "final_g"]
    return (loss, grad_x) + tuple(res[n][k] for k in range(4) for n in order)
```

```python
import functools

import jax
import jax.numpy as jnp
from jax import lax
from jax.experimental import pallas as pl
from jax.experimental.pallas import tpu as pltpu

BF16 = jnp.bfloat16
F32 = jnp.float32
EPS = 1e-6
MIB = 1024 * 1024
V7X_VMEM_BYTES = 64 * MIB
VMEM_LIMIT = 56 * MIB
LANE = 128
N_DEV = 8


def _params(sem, **kw):
    return pltpu.CompilerParams(dimension_semantics=sem, vmem_limit_bytes=VMEM_LIMIT, **kw)


def _pick(n, cands):
    for c in cands:
        if n % c == 0:
            return c
    return n


_DIMS = {"nn": (((1,), (0,)), ((), ())), "nt": (((1,), (1,)), ((), ())), "tn": (((0,), (0,)), ((), ()))}


def _matmul(a, b, *, mode, out_dtype, name, res=None, b_slices=None, out_slices=None, prev=None,
            deps=(), tm=None, tn=None, tk=None):
    soff, scnt = b_slices if b_slices is not None else (0, 1)
    ooff, ocnt = out_slices if out_slices is not None else (0, 1)
    if mode == "nn":
        M = a.shape[0]
        Kc = a.shape[1]
        ns = b.shape[-1]
        N = scnt * ns
    elif mode == "nt":
        M = a.shape[0]
        N = b.shape[-2]
        ns = b.shape[-1]
        Kc = scnt * ns
    else:
        Kc, M = a.shape
        N = b.shape[1]
        ns = N // ocnt
    tm = tm or _pick(M, (1024, 512, 256, 128))
    if tn is None:
        lim = ns if (mode == "nn" and b_slices is not None) or out_slices is not None else N
        tn = _pick(lim, (1024, 768, 512, 256, 128)) if lim > 1408 else lim
    if tk is None:
        if mode == "nt" and b_slices is not None:
            tk = ns
        elif mode == "tn":
            tk = Kc if Kc <= 4096 else _pick(Kc, (4096, 2048))
        else:
            tk = Kc if Kc <= 2816 else _pick(Kc, (2816, 2048, 1536, 1408, 1024, 512))
    nk = Kc // tk
    assert M % tm == 0 and N % tn == 0 and Kc % tk == 0, (name, M, N, Kc, tm, tn, tk)
    if mode == "nn":
        a_spec = pl.BlockSpec((tm, tk), lambda i, j, k: (i, k))
        if b_slices is None:
            b_spec = pl.BlockSpec((tk, tn), lambda i, j, k: (k, j))
        else:
            spn = ns // tn
            assert ns % tn == 0
            b_spec = pl.BlockSpec((None, tk, tn), lambda i, j, k: (soff + j // spn, k, j % spn))
    elif mode == "nt":
        a_spec = pl.BlockSpec((tm, tk), lambda i, j, k: (i, k))
        if b_slices is None:
            b_spec = pl.BlockSpec((tn, tk), lambda i, j, k: (j, k))
        else:
            spk = ns // tk
            assert ns % tk == 0
            b_spec = pl.BlockSpec((None, tn, tk), lambda i, j, k: (soff + k // spk, j, k % spk))
    else:
        a_spec = pl.BlockSpec((tk, tm), lambda i, j, k: (k, i))
        b_spec = pl.BlockSpec((tk, tn), lambda i, j, k: (k, j))
    if out_slices is None:
        o_spec = pl.BlockSpec((tm, tn), lambda i, j, k: (i, j))
        out_shape = jax.ShapeDtypeStruct((M, N), out_dtype)
    else:
        spo = ns // tn
        assert mode == "tn" and ns % tn == 0
        o_spec = pl.BlockSpec((None, tm, tn), lambda i, j, k: (ooff + j // spo, i, j % spo))
        out_shape = jax.ShapeDtypeStruct((N_DEV, M, ns), out_dtype)
    dims = _DIMS[mode]
    has_res = res is not None
    has_prev = prev is not None

    def body(*refs):
        a_ref, b_ref = refs[0], refs[1]
        res_ref = refs[2] if has_res else None
        o_ref = refs[2 + has_res + has_prev + len(deps)]

        def finish(acc):
            if has_res:
                acc = acc + res_ref[...]
            o_ref[...] = acc.astype(o_ref.dtype)

        if nk == 1:
            finish(lax.dot_general(a_ref[...], b_ref[...], dims, preferred_element_type=F32))
        else:
            acc_ref = refs[3 + has_res + has_prev + len(deps)]
            k = pl.program_id(2)

            @pl.when(k == 0)
            def _():
                acc_ref[...] = jnp.zeros(acc_ref.shape, F32)

            acc_ref[...] += lax.dot_general(a_ref[...], b_ref[...], dims, preferred_element_type=F32)

            @pl.when(k == nk - 1)
            def _():
                finish(acc_ref[...])

    in_specs = [a_spec, b_spec]
    args = [a, b]
    if has_res:
        in_specs.append(pl.BlockSpec((tm, tn), lambda i, j, k: (i, j)))
        args.append(res)
    aliases = {}
    if has_prev:
        aliases = {len(args): 0}
        in_specs.append(pl.BlockSpec(memory_space=pl.ANY))
        args.append(prev)
    for d in deps:
        in_specs.append(pl.BlockSpec(memory_space=pl.ANY))
        args.append(d)
    return pl.pallas_call(
        body,
        name=name,
        grid=(M // tm, N // tn, nk),
        in_specs=in_specs,
        out_specs=o_spec,
        out_shape=out_shape,
        scratch_shapes=[pltpu.VMEM((tm, tn), F32)] if nk > 1 else [],
        input_output_aliases=aliases,
        compiler_params=_params(("parallel", "parallel", "arbitrary")),
    )(*args)


HALO = 16
CH = 256


def _fill_pad(pad_ref, chunk_fn, T):
    z = jnp.zeros((HALO, pad_ref.shape[1]), F32)
    pad_ref[pl.ds(0, HALO), :] = z
    pad_ref[pl.ds(HALO + T, HALO), :] = z

    def body(c, carry):
        r0 = pl.multiple_of(c * CH, CH)
        pad_ref[pl.ds(HALO + r0, CH), :] = chunk_fn(r0)
        return carry

    lax.fori_loop(0, T // CH, body, 0)


def _ext(pad_ref, r0):
    return pad_ref[pl.ds(r0, CH + 2 * HALO), :]


def _tap(ext, o):
    if o == 0:
        return ext[HALO:HALO + CH]
    return pltpu.roll(ext, (-o) % ext.shape[0], axis=0)[HALO:HALO + CH]


def _chunks(T, fn, init=0):
    def body(c, carry):
        return fn(pl.multiple_of(c * CH, CH), carry)

    return lax.fori_loop(0, T // CH, body, init)


def _conv3(ext, w):
    return _tap(ext, -1) * w[0:1] + _tap(ext, 0) * w[1:2] + _tap(ext, 1) * w[2:3]


def _conv3_t(ext, w):
    return _tap(ext, 1) * w[0:1] + _tap(ext, 0) * w[1:2] + _tap(ext, -1) * w[2:3]


def _conv3_dw(xext, d):
    return [jnp.sum(_tap(xext, k - 1) * d, axis=0, keepdims=True) for k in range(3)]


def _sigmoid(x):
    return 0.5 * (jnp.tanh(0.5 * x) + 1.0)


_GELU_C = 0.7978845608028654


def _gelu(x):
    return 0.5 * x * (1.0 + jnp.tanh(_GELU_C * (x + 0.044715 * (x * x * x))))


def _gelu_grad(x):
    t = jnp.tanh(_GELU_C * (x + 0.044715 * (x * x * x)))
    return 0.5 * (1.0 + t) + 0.5 * x * (1.0 - t * t) * (_GELU_C * (1.0 + 3.0 * 0.044715 * (x * x)))


def _row_tile(T):
    return _pick(T, (256, 128))


def _rms_fwd(x, g, name):
    T, D = x.shape
    tm = _row_tile(T)

    def body(x_ref, g_ref, h_ref):
        xv = x_ref[...]
        r = lax.rsqrt(jnp.mean(xv * xv, axis=-1, keepdims=True) + EPS)
        h_ref[...] = (xv * r * g_ref[...]).astype(BF16)

    return pl.pallas_call(
        body, name=name, grid=(T // tm,),
        in_specs=[pl.BlockSpec((tm, D), lambda i: (i, 0)), pl.BlockSpec((1, D), lambda i: (0, 0))],
        out_specs=pl.BlockSpec((tm, D), lambda i: (i, 0)),
        out_shape=jax.ShapeDtypeStruct((T, D), BF16),
        compiler_params=_params(("parallel",)),
    )(x, g)


def _rms_bwd(dh, x, g, dres, name, deps=()):
    T, D = x.shape
    tm = _row_tile(T)

    def body(dh_ref, x_ref, g_ref, dres_ref, *rest):
        dx_ref, dxb_ref, dg_ref = rest[len(deps):]
        i = pl.program_id(0)
        xv = x_ref[...]
        r = lax.rsqrt(jnp.mean(xv * xv, axis=-1, keepdims=True) + EPS)
        n = xv * r
        dh_v = dh_ref[...].astype(F32)
        dn = dh_v * g_ref[...]
        dx = dres_ref[...] + r * (dn - n * jnp.mean(dn * n, axis=-1, keepdims=True))
        dx_ref[...] = dx
        dxb_ref[...] = dx.astype(BF16)
        dg = jnp.sum(dh_v * n, axis=0, keepdims=True)

        @pl.when(i == 0)
        def _():
            dg_ref[...] = dg

        @pl.when(i > 0)
        def _():
            dg_ref[...] += dg

    row = pl.BlockSpec((tm, D), lambda i: (i, 0))
    vec = pl.BlockSpec((1, D), lambda i: (0, 0))
    return pl.pallas_call(
        body, name=name, grid=(T // tm,),
        in_specs=[row, row, vec, row] + [pl.BlockSpec(memory_space=pl.ANY)] * len(deps),
        out_specs=[row, row, vec],
        out_shape=[jax.ShapeDtypeStruct((T, D), F32), jax.ShapeDtypeStruct((T, D), BF16),
                   jax.ShapeDtypeStruct((1, D), F32)],
        compiler_params=_params(("arbitrary",)),
    )(dh, x, g, dres, *deps)


def _loss_head(x, tgt, g, name):
    T, D = x.shape
    tm = _row_tile(T)

    def body(x_ref, t_ref, g_ref, dx_ref, dxb_ref, dg_ref, l_ref):
        i = pl.program_id(0)
        xv = x_ref[...]
        gv = g_ref[...]
        r = lax.rsqrt(jnp.mean(xv * xv, axis=-1, keepdims=True) + EPS)
        n = xv * r
        e = n * gv - t_ref[...]
        dy = e * (1.0 / D)
        dn = dy * gv
        dx = r * (dn - n * jnp.mean(dn * n, axis=-1, keepdims=True))
        dx_ref[...] = dx
        dxb_ref[...] = dx.astype(BF16)
        dg = jnp.sum(dy * n, axis=0, keepdims=True)
        per_tok = jnp.mean(e * e, axis=-1, keepdims=True)
        lv = jnp.broadcast_to(0.5 * jnp.sum(per_tok, axis=0, keepdims=True), (1, LANE))

        @pl.when(i == 0)
        def _():
            dg_ref[...] = dg
            l_ref[...] = lv

        @pl.when(i > 0)
        def _():
            dg_ref[...] += dg
            l_ref[...] += lv

    row = pl.BlockSpec((tm, D), lambda i: (i, 0))
    vec = pl.BlockSpec((1, D), lambda i: (0, 0))
    return pl.pallas_call(
        body, name=name, grid=(T // tm,),
        in_specs=[row, row, vec],
        out_specs=[row, row, vec, pl.BlockSpec((1, LANE), lambda i: (0, 0))],
        out_shape=[jax.ShapeDtypeStruct((T, D), F32), jax.ShapeDtypeStruct((T, D), BF16),
                   jax.ShapeDtypeStruct((1, D), F32), jax.ShapeDtypeStruct((1, LANE), F32)],
        compiler_params=_params(("arbitrary",)),
    )(x, tgt, g)


POOL_WINDOWS = (2, 4, 8, 16)


def _pool_offsets(w):
    return range(-(w // 2), w - w // 2)


def _pool_cnt(r0, w, T, shape):
    t = r0 + lax.broadcasted_iota(jnp.int32, shape, 0)
    lo = jnp.maximum(t - w // 2, 0)
    hi = jnp.minimum(t + (w - w // 2 - 1), T - 1)
    return (hi - lo + 1).astype(F32)


def _pooled(pad_ref, r0, w, T):
    ext = _ext(pad_ref, r0)
    s = None
    for o in _pool_offsets(w):
        tap = _tap(ext, o)
        s = tap if s is None else s + tap
    cur = ext[HALO:HALO + CH]
    return s / _pool_cnt(r0, w, T, cur.shape) - cur


def _pool_fwd(P, w_pool, scale, name):
    T = P.shape[0]
    G, PG, _ = w_pool.shape

    def body(a_ref, w_ref, s_ref, o_ref, pad_ref):
        g = pl.program_id(0)
        _fill_pad(pad_ref, lambda r0: a_ref[pl.ds(r0, CH), :].astype(F32), T)
        for gi, w in enumerate(POOL_WINDOWS):
            @pl.when(g == gi)
            def _(w=w):
                def chunk(r0, carry):
                    pooled = _pooled(pad_ref, r0, w, T).astype(BF16)
                    y = jnp.dot(pooled, w_ref[...], preferred_element_type=F32) * s_ref[...]
                    o_ref[pl.ds(r0, CH), :] = y.astype(BF16)
                    return carry

                _chunks(T, chunk)

    return pl.pallas_call(
        body, name=name, grid=(G,),
        in_specs=[pl.BlockSpec((T, PG), lambda g: (0, g)),
                  pl.BlockSpec((None, PG, PG), lambda g: (g, 0, 0)),
                  pl.BlockSpec((1, PG), lambda g: (0, g))],
        out_specs=pl.BlockSpec((T, PG), lambda g: (0, g)),
        out_shape=jax.ShapeDtypeStruct((T, G * PG), BF16),
        scratch_shapes=[pltpu.VMEM((T + 2 * HALO, PG), F32)],
        compiler_params=_params(("parallel",)),
    )(P, w_pool, scale)


def _pool_bwd(P, dy, w_pool, scale, name):
    T = P.shape[0]
    G, PG, _ = w_pool.shape

    def body(a_ref, dy_ref, w_ref, s_ref, da_ref, dw_ref, ds_ref, pad_ref, gpad_ref):
        g = pl.program_id(0)
        gpad_ref[pl.ds(0, HALO), :] = jnp.zeros((HALO, PG), F32)
        gpad_ref[pl.ds(HALO + T, HALO), :] = jnp.zeros((HALO, PG), F32)
        _fill_pad(pad_ref, lambda r0: a_ref[pl.ds(r0, CH), :].astype(F32), T)
        for gi, w in enumerate(POOL_WINDOWS):
            @pl.when(g == gi)
            def _(w=w):
                def chunk1(r0, carry):
                    dw, ds = carry
                    pooled = _pooled(pad_ref, r0, w, T).astype(BF16)
                    ypre = jnp.dot(pooled, w_ref[...], preferred_element_type=F32)
                    dyv = dy_ref[pl.ds(r0, CH), :].astype(F32)
                    ds = ds + jnp.sum(dyv * ypre, axis=0, keepdims=True)
                    dyp = (dyv * s_ref[...]).astype(BF16)
                    dw = dw + lax.dot_general(pooled, dyp, _DIMS["tn"], preferred_element_type=F32)
                    dpool = lax.dot_general(dyp, w_ref[...], _DIMS["nt"], preferred_element_type=F32)
                    gpad_ref[pl.ds(HALO + r0, CH), :] = dpool / _pool_cnt(r0, w, T, dpool.shape)
                    return dw, ds

                dw, ds = _chunks(T, chunk1, (jnp.zeros((PG, PG), F32), jnp.zeros((1, PG), F32)))
                dw_ref[...] = dw
                ds_ref[...] = ds

                def chunk2(r0, carry):
                    ext = _ext(gpad_ref, r0)
                    cur = ext[HALO:HALO + CH]
                    acc = None
                    for o in _pool_offsets(w):
                        tap = _tap(ext, -o)
                        acc = tap if acc is None else acc + tap
                    da_ref[pl.ds(r0, CH), :] = (acc - cur * _pool_cnt(r0, w, T, cur.shape)).astype(BF16)
                    return carry

                _chunks(T, chunk2)

    col = pl.BlockSpec((T, PG), lambda g: (0, g))
    return pl.pallas_call(
        body, name=name, grid=(G,),
        in_specs=[col, col, pl.BlockSpec((None, PG, PG), lambda g: (g, 0, 0)), pl.BlockSpec((1, PG), lambda g: (0, g))],
        out_specs=[col, pl.BlockSpec((None, PG, PG), lambda g: (g, 0, 0)), pl.BlockSpec((1, PG), lambda g: (0, g))],
        out_shape=[jax.ShapeDtypeStruct((T, G * PG), BF16), jax.ShapeDtypeStruct((G, PG, PG), F32),
                   jax.ShapeDtypeStruct((1, G * PG), F32)],
        scratch_shapes=[pltpu.VMEM((T + 2 * HALO, PG), F32), pltpu.VMEM((T + 2 * HALO, PG), F32)],
        compiler_params=_params(("parallel",)),
    )(P, dy, w_pool, scale)


SGU_CHUNK = 128


def _sgu_common(u_ref, v_ref, gs_ref):
    up = u_ref[...].astype(F32)
    vp = v_ref[...].astype(F32)
    gv = _gelu(vp)
    rv = lax.rsqrt(jnp.mean(gv * gv, axis=-1, keepdims=True) + EPS)
    nrm = gv * rv
    return up, vp, nrm, rv, (nrm * gs_ref[...]).astype(BF16)


def _sgu_fwd(P, sgu_w, b_exp, gs, name):
    T = P.shape[0]
    G, _, SG = b_exp.shape
    DP = G * SG
    tm = _row_tile(T)

    def body(u_ref, v_ref, w_ref, b_ref, gs_ref, o_ref):
        up, _, _, _, vn = _sgu_common(u_ref, v_ref, gs_ref)
        gu = _gelu(up)
        for n in range(tm // SGU_CHUNK):
            rows = slice(n * SGU_CHUNK, (n + 1) * SGU_CHUNK)
            for g in range(G):
                cols = slice(g * SG, (g + 1) * SG)
                z = jnp.dot(w_ref[g], vn[rows, cols], preferred_element_type=F32) + b_ref[g]
                o_ref[rows, cols] = (gu[rows, cols] * z).astype(BF16)

    return pl.pallas_call(
        body, name=name, grid=(T // tm,),
        in_specs=[pl.BlockSpec((tm, DP), lambda i: (i, 1)), pl.BlockSpec((tm, DP), lambda i: (i, 2)),
                  pl.BlockSpec(sgu_w.shape, lambda i: (0, 0, 0)), pl.BlockSpec(b_exp.shape, lambda i: (0, 0, 0)),
                  pl.BlockSpec((1, DP), lambda i: (0, 0))],
        out_specs=pl.BlockSpec((tm, DP), lambda i: (i, 0)),
        out_shape=jax.ShapeDtypeStruct((T, DP), BF16),
        compiler_params=_params(("parallel",)),
    )(P, P, sgu_w, b_exp, gs)


def _sgu_bwd(P, dy, da, sgu_w, b_exp, gs, name):
    T = P.shape[0]
    G, _, SG = b_exp.shape
    DP = G * SG
    tm = _row_tile(T)

    def body(u_ref, v_ref, dy_ref, da_ref, w_ref, b_ref, gs_ref, o_ref, dw_ref, db_ref, dgs_ref, dzs_ref):
        i = pl.program_id(0)
        up, vp, nrm, rv, vn = _sgu_common(u_ref, v_ref, gs_ref)
        gu = _gelu(up)
        dyv = dy_ref[...].astype(F32)
        o_ref[:, 0:DP] = da_ref[...]

        @pl.when(i == 0)
        def _():
            dw_ref[...] = jnp.zeros(dw_ref.shape, F32)
            dzs_ref[...] = jnp.zeros(dzs_ref.shape, F32)
            dgs_ref[...] = jnp.zeros(dgs_ref.shape, F32)

        dgs = jnp.zeros((1, DP), F32)
        for n in range(tm // SGU_CHUNK):
            rows = slice(n * SGU_CHUNK, (n + 1) * SGU_CHUNK)
            dvn_parts = []
            for g in range(G):
                cols = slice(g * SG, (g + 1) * SG)
                vng = vn[rows, cols]
                z = jnp.dot(w_ref[g], vng, preferred_element_type=F32) + b_ref[g]
                dyg = dyv[rows, cols]
                du = dyg * z
                o_ref[rows, DP + g * SG:DP + (g + 1) * SG] = (du * _gelu_grad(up[rows, cols])).astype(BF16)
                dz = dyg * gu[rows, cols]
                dzb = dz.astype(BF16)
                dzs_ref[g] += dz
                dw_ref[g] += lax.dot_general(dzb, vng, _DIMS["nt"], preferred_element_type=F32)
                dvn_parts.append(lax.dot_general(w_ref[g], dzb, _DIMS["tn"], preferred_element_type=F32))
            dvn = jnp.concatenate(dvn_parts, axis=1)
            nr = nrm[rows]
            dgs = dgs + jnp.sum(dvn * nr, axis=0, keepdims=True)
            dn = dvn * gs_ref[...]
            dgv = rv[rows] * (dn - nr * jnp.mean(dn * nr, axis=-1, keepdims=True))
            o_ref[rows, 2 * DP:3 * DP] = (dgv * _gelu_grad(vp[rows])).astype(BF16)
        dgs_ref[...] += dgs

        @pl.when(i == T // tm - 1)
        def _():
            for g in range(G):
                db_ref[g] = jnp.broadcast_to(jnp.sum(dzs_ref[g], axis=1, keepdims=True), (SGU_CHUNK, SG))

    full3 = lambda a: pl.BlockSpec(a.shape, lambda i: (0, 0, 0))
    return pl.pallas_call(
        body, name=name, grid=(T // tm,),
        in_specs=[pl.BlockSpec((tm, DP), lambda i: (i, 1)), pl.BlockSpec((tm, DP), lambda i: (i, 2)),
                  pl.BlockSpec((tm, DP), lambda i: (i, 0)), pl.BlockSpec((tm, DP), lambda i: (i, 0)),
                  full3(sgu_w), full3(b_exp), pl.BlockSpec((1, DP), lambda i: (0, 0))],
        out_specs=[pl.BlockSpec((tm, 3 * DP), lambda i: (i, 0)), full3(sgu_w), full3(b_exp),
                   pl.BlockSpec((1, DP), lambda i: (0, 0))],
        out_shape=[jax.ShapeDtypeStruct((T, 6 * DP), BF16), jax.ShapeDtypeStruct(sgu_w.shape, F32),
                   jax.ShapeDtypeStruct(b_exp.shape, F32), jax.ShapeDtypeStruct((1, DP), F32)],
        scratch_shapes=[pltpu.VMEM(b_exp.shape, F32)],
        compiler_params=_params(("arbitrary",)),
    )(P, P, dy, da, sgu_w, b_exp, gs)


def _conv_fwd(P, cw, name):
    T = P.shape[0]
    DP = cw.shape[1]
    tc = 256
    nb = DP // tc

    def body(xc_ref, bg_ref, cg_ref, w_ref, o_ref, pad_ref):
        _fill_pad(pad_ref, lambda r0: cg_ref[pl.ds(r0, CH), :].astype(F32) * xc_ref[pl.ds(r0, CH), :].astype(F32), T)
        w = w_ref[...]

        def chunk(r0, carry):
            cq = _conv3(_ext(pad_ref, r0), w)
            o_ref[pl.ds(r0, CH), :] = (bg_ref[pl.ds(r0, CH), :].astype(F32) * cq).astype(BF16)
            return carry

        _chunks(T, chunk)

    col = lambda off: pl.BlockSpec((T, tc), lambda j: (0, off * nb + j))
    return pl.pallas_call(
        body, name=name, grid=(nb,),
        in_specs=[col(3), col(4), col(5), pl.BlockSpec((3, tc), lambda j: (0, j))],
        out_specs=pl.BlockSpec((T, tc), lambda j: (0, j)),
        out_shape=jax.ShapeDtypeStruct((T, DP), BF16),
        scratch_shapes=[pltpu.VMEM((T + 2 * HALO, tc), F32)],
        compiler_params=_params(("parallel",)),
    )(P, P, P, cw)


def _conv_bwd(P, dy, cw, dPl, name):
    T = P.shape[0]
    DP = cw.shape[1]
    tc = 256
    nb = DP // tc

    def body(xc_ref, bg_ref, cg_ref, dy_ref, w_ref, prev_ref, o_ref, dw_ref, qpad_ref, dpad_ref):
        del prev_ref
        seg = pl.program_id(1)
        w = w_ref[...]
        rows = lambda ref, r0: ref[pl.ds(r0, CH), :].astype(F32)
        _fill_pad(qpad_ref, lambda r0: rows(cg_ref, r0) * rows(xc_ref, r0), T)
        _fill_pad(dpad_ref, lambda r0: rows(dy_ref, r0) * rows(bg_ref, r0), T)

        @pl.when(seg == 0)
        def _():
            def chunk(r0, carry):
                dq = _conv3_t(_ext(dpad_ref, r0), w)
                o_ref[pl.ds(r0, CH), :] = (dq * rows(cg_ref, r0)).astype(BF16)
                dcq = rows(dy_ref, r0) * rows(bg_ref, r0)
                return [a + b for a, b in zip(carry, _conv3_dw(_ext(qpad_ref, r0), dcq))]

            dws = _chunks(T, chunk, [jnp.zeros((1, tc), F32)] * 3)
            dw_ref[...] = jnp.concatenate(dws + [jnp.zeros((5, tc), F32)], axis=0)

        @pl.when(seg == 1)
        def _():
            def chunk(r0, carry):
                cq = _conv3(_ext(qpad_ref, r0), w)
                o_ref[pl.ds(r0, CH), :] = (rows(dy_ref, r0) * cq).astype(BF16)
                return carry

            _chunks(T, chunk)

        @pl.when(seg == 2)
        def _():
            def chunk(r0, carry):
                dq = _conv3_t(_ext(dpad_ref, r0), w)
                o_ref[pl.ds(r0, CH), :] = (dq * rows(xc_ref, r0)).astype(BF16)
                return carry

            _chunks(T, chunk)

    col = lambda off: pl.BlockSpec((T, tc), lambda j, s: (0, off * nb + j))
    return pl.pallas_call(
        body, name=name, grid=(nb, 3),
        in_specs=[col(3), col(4), col(5), pl.BlockSpec((T, tc), lambda j, s: (0, j)),
                  pl.BlockSpec((3, tc), lambda j, s: (0, j)), pl.BlockSpec(memory_space=pl.ANY)],
        out_specs=[pl.BlockSpec((T, tc), lambda j, s: (0, (3 + s) * nb + j)), pl.BlockSpec((8, tc), lambda j, s: (0, j))],
        out_shape=[jax.ShapeDtypeStruct(dPl.shape, BF16), jax.ShapeDtypeStruct((8, DP), F32)],
        scratch_shapes=[pltpu.VMEM((T + 2 * HALO, tc), F32), pltpu.VMEM((T + 2 * HALO, tc), F32)],
        input_output_aliases={5: 0},
        compiler_params=_params(("parallel", "arbitrary")),
    )(P, P, P, dy, cw, dPl)


def _merge_fwd(P, ys, ws, name):
    T = P.shape[0]
    DP, D = ws[0].shape
    tm = _pick(T, (512, 256, 128))
    tn = _pick(D, (1024, 512, 256))
    goff = (6 * DP) // tn

    def body(ya, yb, yc, wa, wb, wc, ga, gb, gc, oa_ref, ob_ref, oc_ref, m_ref):
        m = None
        for y, w, g, o_ref in ((ya, wa, ga, oa_ref), (yb, wb, gb, ob_ref), (yc, wc, gc, oc_ref)):
            o = jnp.dot(y[...], w[...], preferred_element_type=F32)
            o_ref[...] = o.astype(BF16)
            t = _sigmoid(g[...].astype(F32)) * o
            m = t if m is None else m + t
        m_ref[...] = m.astype(BF16)

    yspec = pl.BlockSpec((tm, DP), lambda i, j: (i, 0))
    wspec = pl.BlockSpec((DP, tn), lambda i, j: (0, j))
    gspec = lambda b: pl.BlockSpec((tm, tn), lambda i, j: (i, goff + b * (D // tn) + j))
    ospec = pl.BlockSpec((tm, tn), lambda i, j: (i, j))
    return pl.pallas_call(
        body, name=name, grid=(T // tm, D // tn),
        in_specs=[yspec] * 3 + [wspec] * 3 + [gspec(0), gspec(1), gspec(2)],
        out_specs=[ospec] * 4,
        out_shape=[jax.ShapeDtypeStruct((T, D), BF16)] * 4,
        compiler_params=_params(("parallel", "parallel")),
    )(*ys, *ws, P, P, P)


def _merge_bwd(P, dM, os_, name):
    T, D = dM.shape
    tm = _row_tile(T)

    def body(dm_ref, oa, ob, oc, g_ref, da_ref, db_ref, dc_ref, dg_ref):
        dm = dm_ref[...].astype(F32)
        for b, (o_ref, d_ref) in enumerate(((oa, da_ref), (ob, db_ref), (oc, dc_ref))):
            s = _sigmoid(g_ref[:, b * D:(b + 1) * D].astype(F32))
            d_ref[...] = (dm * s).astype(BF16)
            dg_ref[:, b * D:(b + 1) * D] = (dm * o_ref[...].astype(F32) * s * (1.0 - s)).astype(BF16)

    row = pl.BlockSpec((tm, D), lambda i: (i, 0))
    return pl.pallas_call(
        body, name=name, grid=(T // tm,),
        in_specs=[row] * 4 + [pl.BlockSpec((tm, 3 * D), lambda i: (i, 1))],
        out_specs=[row] * 3 + [pl.BlockSpec((tm, 3 * D), lambda i: (i, 0))],
        out_shape=[jax.ShapeDtypeStruct((T, D), BF16)] * 3 + [jax.ShapeDtypeStruct((T, 3 * D), BF16)],
        compiler_params=_params(("parallel",)),
    )(dM, *os_, P)


def _glu_fwd(U, fw, name):
    T, F2 = U.shape
    F = F2 // 2
    tc = 256
    nb = F // tc

    def body(ug_ref, uv_ref, wg_ref, wv_ref, o_ref, gpad_ref, vpad_ref):
        _fill_pad(gpad_ref, lambda r0: ug_ref[pl.ds(r0, CH), :].astype(F32), T)
        _fill_pad(vpad_ref, lambda r0: uv_ref[pl.ds(r0, CH), :].astype(F32), T)
        wg, wv = wg_ref[...], wv_ref[...]

        def chunk(r0, carry):
            gate = _conv3(_ext(gpad_ref, r0), wg)
            val = _conv3(_ext(vpad_ref, r0), wv)
            o_ref[pl.ds(r0, CH), :] = (gate * _sigmoid(gate) * val).astype(BF16)
            return carry

        _chunks(T, chunk)

    return pl.pallas_call(
        body, name=name, grid=(nb,),
        in_specs=[pl.BlockSpec((T, tc), lambda j: (0, j)), pl.BlockSpec((T, tc), lambda j: (0, nb + j)),
                  pl.BlockSpec((3, tc), lambda j: (0, j)), pl.BlockSpec((3, tc), lambda j: (0, nb + j))],
        out_specs=pl.BlockSpec((T, tc), lambda j: (0, j)),
        out_shape=jax.ShapeDtypeStruct((T, F), BF16),
        scratch_shapes=[pltpu.VMEM((T + 2 * HALO, tc), F32)] * 2,
        compiler_params=_params(("parallel",)),
    )(U, U, fw, fw)


def _glu_bwd(U, dact, fw, name):
    T, F2 = U.shape
    F = F2 // 2
    tc = 128
    nb = F // tc

    def body(ug_ref, uv_ref, da_ref, wg_ref, wv_ref, dug_ref, duv_ref, dwg_ref, dwv_ref, gpad, vpad, dgpad, dvpad):
        _fill_pad(gpad, lambda r0: ug_ref[pl.ds(r0, CH), :].astype(F32), T)
        _fill_pad(vpad, lambda r0: uv_ref[pl.ds(r0, CH), :].astype(F32), T)
        wg, wv = wg_ref[...], wv_ref[...]

        def chunk1(r0, carry):
            gate = _conv3(_ext(gpad, r0), wg)
            val = _conv3(_ext(vpad, r0), wv)
            s = _sigmoid(gate)
            da = da_ref[pl.ds(r0, CH), :].astype(F32)
            dgpad[pl.ds(HALO + r0, CH), :] = da * val * (s * (1.0 + gate * (1.0 - s)))
            dvpad[pl.ds(HALO + r0, CH), :] = da * (gate * s)
            return carry

        z = jnp.zeros((HALO, tc), F32)
        for p in (dgpad, dvpad):
            p[pl.ds(0, HALO), :] = z
            p[pl.ds(HALO + T, HALO), :] = z
        _chunks(T, chunk1)

        def chunk2(r0, carry):
            new = []
            for pad, dpad, wk, o_ref in ((gpad, dgpad, wg, dug_ref), (vpad, dvpad, wv, duv_ref)):
                dext = _ext(dpad, r0)
                taps = [_tap(dext, 1), _tap(dext, 0), _tap(dext, -1)]
                o_ref[pl.ds(r0, CH), :] = (taps[0] * wk[0:1] + taps[1] * wk[1:2] + taps[2] * wk[2:3]).astype(BF16)
                xs = pad[pl.ds(HALO + r0, CH), :]
                new += [jnp.sum(xs * t, axis=0, keepdims=True) for t in taps]
            return [a + b for a, b in zip(carry, new)]

        dws = _chunks(T, chunk2, [jnp.zeros((1, tc), F32)] * 6)
        dwg_ref[...] = jnp.concatenate(dws[:3] + [jnp.zeros((5, tc), F32)], axis=0)
        dwv_ref[...] = jnp.concatenate(dws[3:] + [jnp.zeros((5, tc), F32)], axis=0)

    lo = pl.BlockSpec((T, tc), lambda j: (0, j))
    hi = pl.BlockSpec((T, tc), lambda j: (0, nb + j))
    wlo = pl.BlockSpec((3, tc), lambda j: (0, j))
    whi = pl.BlockSpec((3, tc), lambda j: (0, nb + j))
    dwspec = pl.BlockSpec((8, tc), lambda j: (0, j))
    return pl.pallas_call(
        body, name=name, grid=(nb,),
        in_specs=[lo, hi, lo, wlo, whi],
        out_specs=[lo, lo, dwspec, dwspec],
        out_shape=[jax.ShapeDtypeStruct((T, F), BF16)] * 2 + [jax.ShapeDtypeStruct((8, F), F32)] * 2,
        scratch_shapes=[pltpu.VMEM((T + 2 * HALO, tc), F32)] * 4,
        compiler_params=_params(("parallel",)),
    )(U, U, dact, fw, fw)


_ANY = pl.BlockSpec(memory_space=pl.ANY)
_MESH = pl.DeviceIdType.MESH


def _mesh_pos():
    return lax.axis_index("x"), lax.axis_index("y"), lax.axis_index("c")


def _other_chips(x, y):
    return [(1 - x, y), (x, 1 - y), (1 - x, 1 - y)]


_HBM = pl.BlockSpec(memory_space=pltpu.HBM)
_SEM = pl.BlockSpec(memory_space=pltpu.SEMAPHORE)
_EFFECT = pltpu.SideEffectType.DATAFLOW_SIDE_EFFECTING


def _in_hbm(a):
    return pltpu.with_memory_space_constraint(a, pltpu.HBM)


def _split_start(groups, name):
    sizes = [len(bufs) for bufs, _, _ in groups]
    nb = sum(sizes)
    ng = len(groups)

    def body(*refs):
        ins = refs[:nb]
        sems = refs[nb:nb + 2 * ng]
        token = refs[-1]
        off = 0
        for gi, (bufs, make_copies, _) in enumerate(groups):
            for cp in make_copies(ins[off:off + len(bufs)], sems[2 * gi], sems[2 * gi + 1]):
                cp.start()
            off += len(bufs)
        token[...] = jnp.zeros(token.shape, token.dtype)

    flat = [b for bufs, _, _ in groups for b in bufs]
    sem_shapes = []
    for _, _, n in groups:
        sem_shapes += [pltpu.SemaphoreType.DMA((n,)), pltpu.SemaphoreType.DMA((n,))]
    outs = pl.pallas_call(
        body, name=name,
        in_specs=[_HBM] * nb,
        out_specs=[_SEM] * (2 * ng) + [_HBM] * nb + [pl.BlockSpec(memory_space=pltpu.VMEM)],
        out_shape=sem_shapes + [pltpu.HBM(b.shape, b.dtype) for b in flat] + [jax.ShapeDtypeStruct((8, LANE), F32)],
        input_output_aliases={i: 2 * ng + i for i in range(nb)},
        compiler_params=pltpu.CompilerParams(has_side_effects=_EFFECT),
    )(*[_in_hbm(b) for b in flat])
    res, off = [], 2 * ng
    for gi, n in enumerate(sizes):
        res.append((outs[2 * gi], outs[2 * gi + 1], list(outs[off:off + n])))
        off += n
    return res, outs[-1]


def _split_wait(started, make_copies, after, name):
    send, recv, bufs = started
    nb = len(bufs)
    after = list(after) if isinstance(after, (list, tuple)) else [after]

    def body(*refs):
        for cp in make_copies(refs[:nb], refs[nb], refs[nb + 1]):
            cp.wait_send()
            cp.wait_recv()

    outs = pl.pallas_call(
        body, name=name,
        in_specs=[_HBM] * nb + [_SEM, _SEM] + [_ANY] * len(after),
        out_specs=[_HBM] * nb,
        out_shape=[pltpu.HBM(b.shape, b.dtype) for b in bufs],
        input_output_aliases={i: i for i in range(nb)},
        compiler_params=pltpu.CompilerParams(has_side_effects=_EFFECT),
    )(*bufs, send, recv, *after)
    return list(outs)


def _copies_gather_chips(nbuf):
    def make(bufs, send, recv):
        x, y, c = _mesh_pos()
        targets = [(x, y, 1 - c)] + [(px, py, c) for px, py in _other_chips(x, y)]
        cps = []
        for a in range(nbuf):
            mine = bufs[a].at[4 * x + 2 * y + c]
            for k, to in enumerate(targets):
                cps.append(pltpu.make_async_remote_copy(src_ref=mine, dst_ref=mine, send_sem=send.at[4 * a + k],
                                                        recv_sem=recv.at[4 * a + k], device_id=to, device_id_type=_MESH))
        return cps

    return make, 4 * nbuf


def _copies_gather_forward(nbuf):
    def make(bufs, send, recv):
        x, y, c = _mesh_pos()
        cps = []
        for a in range(nbuf):
            for j, (px, py) in enumerate(_other_chips(x, y)):
                slot = bufs[a].at[4 * px + 2 * py + c]
                cps.append(pltpu.make_async_remote_copy(src_ref=slot, dst_ref=slot, send_sem=send.at[3 * a + j],
                                                        recv_sem=recv.at[3 * a + j], device_id=(x, y, 1 - c),
                                                        device_id_type=_MESH))
        return cps

    return make, 3 * nbuf


def _copies_rs_sibling(n):
    def make(bufs, send, recv):
        x, y, c = _mesh_pos()
        cps = []
        for a in range(n):
            for q in range(4):
                cps.append(pltpu.make_async_remote_copy(
                    src_ref=bufs[a].at[2 * q + (1 - c)], dst_ref=bufs[n + a].at[q], send_sem=send.at[4 * a + q],
                    recv_sem=recv.at[4 * a + q], device_id=(x, y, 1 - c), device_id_type=_MESH))
        return cps

    return make, 4 * n


def _copies_rs_chips(n):
    def make(bufs, send, recv):
        x, y, c = _mesh_pos()
        cps = []
        for a in range(n):
            for j, (px, py) in enumerate(_other_chips(x, y)):
                cps.append(pltpu.make_async_remote_copy(
                    src_ref=bufs[a].at[2 * px + py], dst_ref=bufs[n + a].at[j], send_sem=send.at[3 * a + j],
                    recv_sem=recv.at[3 * a + j], device_id=(px, py, c), device_id_type=_MESH))
        return cps

    return make, 3 * n


def _cast_place(stacked, layer, slot, dtype, name, deps=()):
    _, R, C = stacked.shape
    tr = _rows_tile(R, C)

    def body(slot_ref, x_ref, *rest):
        del slot_ref
        rest[-1][...] = x_ref[...].astype(dtype)

    return pl.pallas_call(
        body, name=name,
        grid_spec=pltpu.PrefetchScalarGridSpec(
            num_scalar_prefetch=1, grid=(R // tr,),
            in_specs=[pl.BlockSpec((None, tr, C), lambda i, s: (layer, i, 0))] + [_ANY] * len(deps),
            out_specs=pl.BlockSpec((None, tr, C), lambda i, s: (s[0], i, 0))),
        out_shape=jax.ShapeDtypeStruct((N_DEV, R, C), dtype),
        compiler_params=_params(("parallel",)),
    )(slot, stacked, *deps)


def _rows_tile(R, C, budget=2 * MIB):
    if R * C * 4 <= budget:
        return R
    for t in (2048, 1024, 704, 512, 352, 256, 128, 64, 32, 16, 8):
        if R % t == 0 and t * C * 4 <= budget:
            return t
    return R


def _pair_sum(grad, recv, core, name):
    _, R, C = grad.shape
    tr = _rows_tile(R, C, 8 * MIB)

    def body(c_ref, g_ref, r_ref, o_ref):
        del c_ref
        o_ref[...] = (g_ref[...].astype(F32) + r_ref[...].astype(F32)).astype(BF16)

    return pl.pallas_call(
        body, name=name,
        grid_spec=pltpu.PrefetchScalarGridSpec(
            num_scalar_prefetch=1, grid=(4, R // tr),
            in_specs=[pl.BlockSpec((None, tr, C), lambda q, i, c: (2 * q + c[0], i, 0)),
                      pl.BlockSpec((None, tr, C), lambda q, i, c: (q, i, 0))],
            out_specs=pl.BlockSpec((None, tr, C), lambda q, i, c: (q, i, 0))),
        out_shape=jax.ShapeDtypeStruct((4, R, C), BF16),
        compiler_params=_params(("parallel", "parallel")),
    )(core, grad, recv)


ADAM_LR = 0.001
ADAM_B1 = 0.9
ADAM_B2 = 0.999
ADAM_EPS = 1e-08
ADAM_WD = 0.01
ADAM_STEP = 10


def _adamw(w, g, m, v):
    m = ADAM_B1 * m + (1.0 - ADAM_B1) * g
    v = ADAM_B2 * v + (1.0 - ADAM_B2) * (g * g)
    m_hat = m / (1.0 - ADAM_B1 ** ADAM_STEP)
    v_hat = v / (1.0 - ADAM_B2 ** ADAM_STEP)
    delta = -ADAM_LR * (m_hat / (jnp.sqrt(v_hat) + ADAM_EPS) + ADAM_WD * w)
    return delta, m, v


def _adamw_big(pair, recv, w, m, v, chip, layer, prev, name):
    L, R, C = w.shape
    tr = _rows_tile(R, C)
    has_prev = prev is not None

    def body(chip_ref, p_ref, r0_ref, r1_ref, r2_ref, w_ref, m_ref, v_ref, *rest):
        del chip_ref
        g_ref, d_ref, mo_ref, vo_ref = rest[-4:]
        g = ((p_ref[...].astype(F32) + r0_ref[...].astype(F32)) + r1_ref[...].astype(F32)) + r2_ref[...].astype(F32)
        delta, m2, v2 = _adamw(w_ref[...], g, m_ref[...], v_ref[...])
        g_ref[...] = g
        d_ref[...] = delta
        mo_ref[...] = m2
        vo_ref[...] = v2

    lay = pl.BlockSpec((None, tr, C), lambda i, c: (layer, i, 0))
    rspec = lambda j: pl.BlockSpec((None, tr, C), lambda i, c: (j, i, 0))
    in_specs = [pl.BlockSpec((None, tr, C), lambda i, c: (c[0], i, 0)), rspec(0), rspec(1), rspec(2), lay, lay, lay]
    args = [chip, pair, recv, recv, recv, w, m, v]
    aliases = {}
    if has_prev:
        in_specs += [_ANY] * 4
        args += list(prev)
        aliases = {8 + k: k for k in range(4)}
    return pl.pallas_call(
        body, name=name,
        grid_spec=pltpu.PrefetchScalarGridSpec(
            num_scalar_prefetch=1, grid=(R // tr,), in_specs=in_specs, out_specs=[lay] * 4),
        out_shape=[jax.ShapeDtypeStruct((L, R, C), F32)] * 4,
        input_output_aliases=aliases,
        compiler_params=_params(("parallel",)),
    )(*args)


def _sum_slots(gathered, name):
    _, R, C = gathered.shape
    tr = _rows_tile(R, C, 2 * MIB)

    def body(x_ref, o_ref):
        s = x_ref[0]
        for k in range(1, N_DEV):
            s = s + x_ref[k]
        o_ref[...] = s

    return pl.pallas_call(
        body, name=name, grid=(R // tr,),
        in_specs=[pl.BlockSpec((N_DEV, tr, C), lambda i: (0, i, 0))],
        out_specs=pl.BlockSpec((tr, C), lambda i: (i, 0)),
        out_shape=jax.ShapeDtypeStruct((R, C), F32),
        compiler_params=_params(("parallel",)),
    )(gathered)


def _adamw_small(w, g, m, v, name):
    R, C = w.shape
    tr = _rows_tile(R, C, 2 * MIB)

    def body(w_ref, g_ref, m_ref, v_ref, d_ref, mo_ref, vo_ref):
        d_ref[...], mo_ref[...], vo_ref[...] = _adamw(w_ref[...], g_ref[...], m_ref[...], v_ref[...])

    spec = pl.BlockSpec((tr, C), lambda i: (i, 0))
    return pl.pallas_call(
        body, name=name, grid=(R // tr,),
        in_specs=[spec] * 4, out_specs=[spec] * 3,
        out_shape=[jax.ShapeDtypeStruct((R, C), F32)] * 3,
        compiler_params=_params(("parallel",)),
    )(w, g, m, v)


def _pack(arrs):
    parts = []
    for a in arrs:
        flat = a.reshape(-1)
        pad = (-flat.shape[0]) % (8 * LANE)
        if pad:
            flat = jnp.pad(flat, (0, pad))
        parts.append(flat.reshape(-1, LANE))
    return jnp.concatenate(parts, axis=0)


def _unpack(packed, shapes):
    out, r = [], 0
    for s in shapes:
        n = 1
        for d in s:
            n *= d
        rows = -(-n // (8 * LANE)) * 8
        out.append(packed[r:r + rows].reshape(-1)[:n].reshape(s))
        r += rows
    return out


def _to_slices(a, ns):
    K = a.shape[0]
    return a.reshape(K, N_DEV, ns).transpose(1, 0, 2)


def _from_slices(a):
    _, K, ns = a.shape
    return a.transpose(1, 0, 2).reshape(K, N_DEV * ns)


def kernel(x, norm1_g, w_in, w_pool, pool_scale, sgu_norm_g, sgu_w, sgu_b, conv_w, w_pool_out, w_sgu_out, w_conv_out, w_o, norm2_g, w_up, ffn_conv_w, w_down, final_g, loss_target, m_norm1_g, m_w_in, m_w_pool, m_pool_scale, m_sgu_norm_g, m_sgu_w, m_sgu_b, m_conv_w, m_w_pool_out, m_w_sgu_out, m_w_conv_out, m_w_o, m_norm2_g, m_w_up, m_ffn_conv_w, m_w_down, m_final_g, v_norm1_g, v_w_in, v_w_pool, v_pool_scale, v_sgu_norm_g, v_sgu_w, v_sgu_b, v_conv_w, v_w_pool_out, v_w_sgu_out, v_w_conv_out, v_w_o, v_norm2_g, v_w_up, v_ffn_conv_w, v_w_down, v_final_g):
    L = norm1_g.shape[0]
    x0 = x[0]
    tgt = loss_target[0]
    G, PG = w_pool.shape[1], w_pool.shape[3]
    F = w_down.shape[1] * N_DEV
    px, py, pc = _mesh_pos()
    me = 4 * px + 2 * py + pc
    core = jnp.reshape(pc, (1,)).astype(jnp.int32)
    chip = jnp.reshape(2 * px + py, (1,)).astype(jnp.int32)

    slot = jnp.reshape(me, (1,)).astype(jnp.int32)
    stage1, group_sizes, started1, tok0 = [], [], [], None
    for l in range(L):
        grp = [[("w_in", w_in, BF16)],
               [("w_pool_out", w_pool_out, BF16), ("w_sgu_out", w_sgu_out, BF16), ("w_conv_out", w_conv_out, BF16),
                ("w_pool", w_pool.reshape(L, -1, PG), BF16), ("conv_w", conv_w, F32), ("w_o", w_o, BF16)],
               [("w_up", w_up, BF16), ("ffn_conv_w", ffn_conv_w, F32)],
               [("w_down", w_down, BF16)]]
        for members in grp:
            deps = [] if tok0 is None else [tok0]
            bufs = [_cast_place(a, l, slot, dt, f"place_{n}_{l}", deps) for n, a, dt in members]
            stage1.append((bufs,) + _copies_gather_chips(len(members)))
            group_sizes.append(len(members))
            if tok0 is None:
                started1, tok0 = _split_start(stage1, "ag_start_first")
    rest, _ = _split_start(stage1[1:], "ag_start")
    started1 = started1 + rest
    started2 = {}

    def gather_forward(k, after):
        n = group_sizes[k]
        bufs = _split_wait(started1[k], stage1[k][1], after, f"ag_wait1_{k}")
        st, tok = _split_start([(bufs,) + _copies_gather_forward(n)], f"ag_fwd_{k}")
        started2[k] = st[0]
        return tok

    def gather_done(k, after):
        return _split_wait(started2.pop(k), _copies_gather_forward(group_sizes[k])[0], after, f"ag_wait2_{k}")

    W, saved = [], []
    xc = x0
    for l in range(L):
        w = dict(
            sguw=sgu_w[l].astype(BF16),
            bexp=jnp.broadcast_to(sgu_b[l].T[:, :, None],
                                  (sgu_b.shape[2], sgu_b.shape[1], sgu_norm_g.shape[1] // sgu_b.shape[2])),
            g1=norm1_g[l][None], g2=norm2_g[l][None], gs=sgu_norm_g[l][None], scale=pool_scale[l][None])
        ka, kb, kc, kd = 4 * l, 4 * l + 1, 4 * l + 2, 4 * l + 3
        h1 = _rms_fwd(xc, w["g1"], f"rms1_fwd_{l}")
        if l == 0:
            tok = gather_forward(ka, h1)
        (w["win3"],) = gather_done(ka, h1)
        if l > 0:
            tok = gather_forward(kb, h1)
        P = _matmul(h1, w["win3"], mode="nn", out_dtype=BF16, name=f"proj_in_{l}", b_slices=(0, N_DEV), deps=[tok])
        if l == 0:
            gather_forward(kb, P)
        wa, wb, wc, wp, cw, wo = gather_done(kb, P)
        w.update(wa=_from_slices(wa), wb=_from_slices(wb), wc=_from_slices(wc), cw=_from_slices(cw),
                 wpool=wp.reshape(N_DEV, G, -1, PG).transpose(1, 0, 2, 3).reshape(G, PG, PG),
                 wo=wo.reshape(-1, wo.shape[2]))
        ya = _pool_fwd(P, w["wpool"], w["scale"], f"pool_fwd_{l}")
        yb = _sgu_fwd(P, w["sguw"], w["bexp"], w["gs"], f"sgu_fwd_{l}")
        yc = _conv_fwd(P, w["cw"], f"conv_fwd_{l}")
        oa, ob, oc, M = _merge_fwd(P, (ya, yb, yc), (w["wa"], w["wb"], w["wc"]), f"merge_fwd_{l}")
        tok = gather_forward(kc, M)
        x1 = _matmul(M, w["wo"], mode="nn", out_dtype=F32, name=f"proj_o_{l}", res=xc, deps=[tok])
        h2 = _rms_fwd(x1, w["g2"], f"rms2_fwd_{l}")
        w["wup3"], fw = gather_done(kc, h2)
        w["fw"] = _from_slices(fw)
        tok = gather_forward(kd, h2)
        U = _matmul(h2, w["wup3"], mode="nn", out_dtype=BF16, name=f"proj_up_{l}", b_slices=(0, N_DEV), deps=[tok])
        act = _glu_fwd(U, w["fw"], f"glu_fwd_{l}")
        (wdown,) = gather_done(kd, act)
        w["wdown"] = wdown.reshape(-1, wdown.shape[2])
        deps = [gather_forward(4 * (l + 1), act)] if l + 1 < L else []
        x2 = _matmul(act, w["wdown"], mode="nn", out_dtype=F32, name=f"proj_down_{l}", res=x1, deps=deps,
                     tm=_pick(x1.shape[0], (512, 256)))
        if deps:
            tok = deps[0]
        W.append(w)
        saved.append(dict(x0=xc, h1=h1, P=P, ya=ya, yb=yb, yc=yc, oa=oa, ob=ob, oc=oc, M=M, x1=x1, h2=h2, U=U, act=act))
        xc = x2

    dx, dxb, d_final_g, loss_part = _loss_head(xc, tgt, final_g[None], "loss_head")
    loss = lax.psum(loss_part[0, 0], ("x", "y", "c"))

    wmv = dict(w_in=(w_in, m_w_in, v_w_in), w_up=(w_up, m_w_up, v_w_up), w_o=(w_o, m_w_o, v_w_o),
               w_down=(w_down, m_w_down, v_w_down), w_pool_out=(w_pool_out, m_w_pool_out, v_w_pool_out),
               w_sgu_out=(w_sgu_out, m_w_sgu_out, v_w_sgu_out), w_conv_out=(w_conv_out, m_w_conv_out, v_w_conv_out),
               w_pool=(w_pool, m_w_pool, v_w_pool))
    adam_out = {}

    def rs_begin(tag, layer, names, grads):
        n = len(grads)
        lands = [lax.empty((4,) + g.shape[1:], BF16) for g in grads]
        st, tok = _split_start([(list(grads) + lands,) + _copies_rs_sibling(n)], f"rs_sib_{tag}")
        return dict(tag=tag, layer=layer, names=names, n=n, st=st[0]), tok

    def rs_to_chips(state, after):
        n, tag = state["n"], state["tag"]
        bufs = _split_wait(state["st"], _copies_rs_sibling(n)[0], after, f"rs_sibw_{tag}")
        pairs = [_pair_sum(g, r, core, f"pair_{tag}_{i}") for i, (g, r) in enumerate(zip(bufs[:n], bufs[n:]))]
        lands = [lax.empty((3,) + p.shape[1:], BF16) for p in pairs]
        st, tok = _split_start([(pairs + lands,) + _copies_rs_chips(n)], f"rs_chips_{tag}")
        state["st"] = st[0]
        return tok

    def rs_finish(state, after):
        n, tag, layer = state["n"], state["tag"], state["layer"]
        bufs = _split_wait(state["st"], _copies_rs_chips(n)[0], after, f"rs_chipsw_{tag}")
        for name, pair, recv in zip(state["names"], bufs[:n], bufs[n:]):
            wk, mk, vk = (a.reshape(L, -1, a.shape[-1]) for a in wmv[name])
            adam_out[name] = _adamw_big(pair, recv, wk, mk, vk, chip, layer, adam_out.get(name), f"adamw_{name}_{layer}")

    small = [None] * L
    pending = []
    deps = []
    ns_out = w_pool_out.shape[2]
    for l in reversed(range(L)):
        w, s = W[l], saved[l]
        dact = _matmul(dxb, w["wdown"], mode="nt", out_dtype=BF16, name=f"d_act_{l}", tn=_pick(F, (512, 256)), deps=deps)
        g_down = _matmul(s["act"], dxb, mode="tn", out_dtype=BF16, name=f"g_down_{l}", tm=_pick(F, (512, 256)))
        dug, duv, dfwg, dfwv = _glu_bwd(s["U"], dact, w["fw"], f"glu_bwd_{l}")
        dh2 = _matmul(dug, w["wup3"], mode="nt", out_dtype=F32, name=f"d_h2a_{l}", b_slices=(0, 4))
        dh2 = _matmul(duv, w["wup3"], mode="nt", out_dtype=F32, name=f"d_h2b_{l}", b_slices=(4, 4), res=dh2)
        g_up = _matmul(s["h2"], dug, mode="tn", out_dtype=BF16, name=f"g_upa_{l}", out_slices=(0, 4), tm=512)
        g_up = _matmul(s["h2"], duv, mode="tn", out_dtype=BF16, name=f"g_upb_{l}", out_slices=(4, 4), prev=g_up, tm=512)
        ra, tok = rs_begin(f"a{l}", l, ["w_down", "w_up"], [g_down.reshape(N_DEV, -1, g_down.shape[1]), g_up])
        if pending:
            rs_finish(pending.pop(0), g_up)
        dx1, dx1b, dg2 = _rms_bwd(dh2, s["x1"], w["g2"], dx, f"rms2_bwd_{l}")
        dM = _matmul(dx1b, w["wo"], mode="nt", out_dtype=BF16, name=f"d_m_{l}", deps=[tok])
        g_o = _matmul(s["M"], dx1b, mode="tn", out_dtype=BF16, name=f"g_o_{l}")
        doa, dob, doc, dPg = _merge_bwd(s["P"], dM, (s["oa"], s["ob"], s["oc"]), f"merge_bwd_{l}")
        tok = rs_to_chips(ra, dPg)
        if pending:
            rs_finish(pending.pop(0), dPg)
        dya = _matmul(doa, w["wa"], mode="nt", out_dtype=BF16, name=f"d_ya_{l}", deps=[tok])
        dyb = _matmul(dob, w["wb"], mode="nt", out_dtype=BF16, name=f"d_yb_{l}")
        dyc = _matmul(doc, w["wc"], mode="nt", out_dtype=BF16, name=f"d_yc_{l}")
        g_a = _matmul(s["ya"], doa, mode="tn", out_dtype=BF16, name=f"g_a_{l}")
        g_b = _matmul(s["yb"], dob, mode="tn", out_dtype=BF16, name=f"g_b_{l}")
        g_c = _matmul(s["yc"], doc, mode="tn", out_dtype=BF16, name=f"g_c_{l}")
        da, g_pool, dscale = _pool_bwd(s["P"], dya, w["wpool"], w["scale"], f"pool_bwd_{l}")
        g_pool_s = g_pool.reshape(G, N_DEV, -1, PG).transpose(1, 0, 2, 3).reshape(N_DEV, -1, PG).astype(BF16)
        rb, tok = rs_begin(f"b{l}", l, ["w_o", "w_pool_out", "w_sgu_out", "w_conv_out", "w_pool"],
                           [g_o.reshape(N_DEV, -1, g_o.shape[1]), _to_slices(g_a, ns_out), _to_slices(g_b, ns_out),
                            _to_slices(g_c, ns_out), g_pool_s])
        if pending:
            rs_finish(pending.pop(0), da)
        dPl, g_sguw, db_exp, dgs = _sgu_bwd(s["P"], dyb, da, w["sguw"], w["bexp"], w["gs"], f"sgu_bwd_{l}")
        dPl, dcw = _conv_bwd(s["P"], dyc, w["cw"], dPl, f"conv_bwd_{l}")
        tok2 = rs_to_chips(rb, dPl)
        g_in = _matmul(s["h1"], dPl, mode="tn", out_dtype=BF16, name=f"g_ina_{l}", out_slices=(0, 4), deps=[tok, tok2])
        g_in = _matmul(s["h1"], dPg, mode="tn", out_dtype=BF16, name=f"g_inb_{l}", out_slices=(4, 4), prev=g_in)
        rc, tok = rs_begin(f"c{l}", l, ["w_in"], [g_in])
        dh1 = _matmul(dPl, w["win3"], mode="nt", out_dtype=F32, name=f"d_h1a_{l}", b_slices=(0, 4), deps=[tok])
        tok = rs_to_chips(rc, dh1)
        dh1 = _matmul(dPg, w["win3"], mode="nt", out_dtype=F32, name=f"d_h1b_{l}", b_slices=(4, 4), res=dh1, deps=[tok])
        dx, dxb, dg1 = _rms_bwd(dh1, s["x0"], w["g1"], dx1, f"rms1_bwd_{l}")
        deps = []
        pending = [ra, rb, rc]
        small[l] = dict(norm1_g=dg1[0], pool_scale=dscale[0], sgu_norm_g=dgs[0], sgu_w=g_sguw,
                        sgu_b=db_exp[:, :, 0].T, conv_w=dcw[:3], norm2_g=dg2[0],
                        ffn_conv_w=jnp.concatenate([dfwg[:3], dfwv[:3]], axis=1))
    grad_x = dx[None]

    snames = ["norm1_g", "pool_scale", "sgu_norm_g", "sgu_w", "sgu_b", "conv_w", "norm2_g", "ffn_conv_w"]
    sparts = [jnp.stack([small[l][n] for l in range(L)]) for n in snames] + [d_final_g[0]]
    snames = snames + ["final_g"]
    packed = _pack(sparts)
    sbuf = _cast_place(packed[None], 0, slot, F32, "place_small")
    st, tok = _split_start([([sbuf],) + _copies_gather_chips(1)], "ag_small_start")
    rs_finish(pending[0], tok)
    bufs = _split_wait(st[0], _copies_gather_chips(1)[0], [adam_out[n][0] for n in pending[0]["names"]],
                       "ag_small_wait1")
    st, tok = _split_start([(bufs,) + _copies_gather_forward(1)], "ag_small_fwd")
    rs_finish(pending[1], tok)
    (gathered_small,) = _split_wait(st[0], _copies_gather_forward(1)[0],
                                    [adam_out[n][0] for n in pending[1]["names"]], "ag_small_wait2")
    total = _sum_slots(gathered_small, "sum_small")
    sgrads = dict(zip(snames, _unpack(total, [p.shape for p in sparts])))
    for n, width in (("conv_w", conv_w.shape[2]), ("ffn_conv_w", ffn_conv_w.shape[2])):
        sgrads[n] = lax.dynamic_slice_in_dim(sgrads[n], me * width, width, axis=2)
    sw = dict(norm1_g=(norm1_g, m_norm1_g, v_norm1_g), pool_scale=(pool_scale, m_pool_scale, v_pool_scale),
              sgu_norm_g=(sgu_norm_g, m_sgu_norm_g, v_sgu_norm_g), sgu_w=(sgu_w, m_sgu_w, v_sgu_w),
              sgu_b=(sgu_b, m_sgu_b, v_sgu_b), conv_w=(conv_w, m_conv_w, v_conv_w),
              norm2_g=(norm2_g, m_norm2_g, v_norm2_g), ffn_conv_w=(ffn_conv_w, m_ffn_conv_w, v_ffn_conv_w),
              final_g=(final_g, m_final_g, v_final_g))
    shapes = [sw[n][0].shape for n in snames]
    upd = _adamw_small(_pack([sw[n][0] for n in snames]), _pack([sgrads[n] for n in snames]),
                       _pack([sw[n][1] for n in snames]), _pack([sw[n][2] for n in snames]), "adamw_small")
    sdelta, sm, sv = (dict(zip(snames, _unpack(u, shapes))) for u in upd)
    res = {n: [sgrads[n], sdelta[n], sm[n], sv[n]] for n in snames}

    rs_finish(pending[2], [upd[0]] + [o[0] for o in adam_out.values()])
    for n, outs in adam_out.items():
        res[n] = [o.reshape(wmv[n][0].shape) for o in outs]

    order = ["norm1_g", "w_in", "w_pool", "pool_scale", "sgu_norm_g", "sgu_w", "sgu_b", "conv_w", "w_pool_out",
             "w_sgu_out", "w_conv_out", "w_o", "norm2_g", "w_up", "ffn_conv_w", "w_down", "final_g"]
    return (loss, grad_x) + tuple(res[n][k] for k in range(4) for n in order)
```

```python
import functools

import jax
import jax.numpy as jnp
from jax import lax
from jax.experimental import pallas as pl
from jax.experimental.pallas import tpu as pltpu

BF16 = jnp.bfloat16
F32 = jnp.float32
EPS = 1e-6
MIB = 1024 * 1024
V7X_VMEM_BYTES = 64 * MIB
VMEM_LIMIT = 56 * MIB
LANE = 128
N_DEV = 8


def _params(sem, **kw):
    return pltpu.CompilerParams(dimension_semantics=sem, vmem_limit_bytes=VMEM_LIMIT, **kw)


def _pick(n, cands):
    for c in cands:
        if n % c == 0:
            return c
    return n


_DIMS = {"nn": (((1,), (0,)), ((), ())), "nt": (((1,), (1,)), ((), ())), "tn": (((0,), (0,)), ((), ()))}


def _matmul(a, b, *, mode, out_dtype, name, res=None, b_slices=None, out_slices=None, prev=None,
            deps=(), col_chunk=None, tm=None, tn=None, tk=None):
    soff, scnt = b_slices if b_slices is not None else (0, 1)
    ooff, ocnt = out_slices if out_slices is not None else (0, 1)
    if mode == "nn":
        M = a.shape[0]
        Kc = a.shape[1]
        ns = b.shape[-1]
        N = scnt * ns
        if col_chunk is not None:
            nq, cq = col_chunk
            tn = ns // nq
            N = scnt * tn
    elif mode == "nt":
        M = a.shape[0]
        N = b.shape[-2]
        ns = b.shape[-1]
        Kc = scnt * ns
    else:
        Kc, M = a.shape
        N = b.shape[1]
        ns = N // ocnt
    tm = tm or _pick(M, (1024, 512, 256, 128))
    if tn is None:
        lim = ns if (mode == "nn" and b_slices is not None) or out_slices is not None else N
        tn = _pick(lim, (1024, 768, 512, 256, 128)) if lim > 1408 else lim
    if tk is None:
        if mode == "nt" and b_slices is not None:
            tk = ns
        elif mode == "tn":
            tk = Kc if Kc <= 4096 else _pick(Kc, (4096, 2048))
        else:
            tk = Kc if Kc <= 2816 else _pick(Kc, (2816, 2048, 1536, 1408, 1024, 512))
    nk = Kc // tk
    assert M % tm == 0 and N % tn == 0 and Kc % tk == 0, (name, M, N, Kc, tm, tn, tk)
    if mode == "nn":
        a_spec = pl.BlockSpec((tm, tk), lambda i, j, k: (i, k))
        if b_slices is None:
            b_spec = pl.BlockSpec((tk, tn), lambda i, j, k: (k, j))
        else:
            spn = ns // tn
            assert ns % tn == 0
            b_spec = pl.BlockSpec((None, tk, tn), lambda i, j, k: (soff + j // spn, k, j % spn))
            if col_chunk is not None:
                b_spec = pl.BlockSpec((None, tk, tn), lambda i, j, k: (soff + j, k, cq))
    elif mode == "nt":
        a_spec = pl.BlockSpec((tm, tk), lambda i, j, k: (i, k))
        if b_slices is None:
            b_spec = pl.BlockSpec((tn, tk), lambda i, j, k: (j, k))
        else:
            spk = ns // tk
            assert ns % tk == 0
            b_spec = pl.BlockSpec((None, tn, tk), lambda i, j, k: (soff + k // spk, j, k % spk))
    else:
        a_spec = pl.BlockSpec((tk, tm), lambda i, j, k: (k, i))
        b_spec = pl.BlockSpec((tk, tn), lambda i, j, k: (k, j))
    if col_chunk is not None:
        o_spec = pl.BlockSpec((tm, tn), lambda i, j, k: (i, j * nq + cq))
        out_shape = jax.ShapeDtypeStruct((M, scnt * ns), out_dtype)
    elif out_slices is None:
        o_spec = pl.BlockSpec((tm, tn), lambda i, j, k: (i, j))
        out_shape = jax.ShapeDtypeStruct((M, N), out_dtype)
    else:
        spo = ns // tn
        assert mode == "tn" and ns % tn == 0
        o_spec = pl.BlockSpec((None, tm, tn), lambda i, j, k: (ooff + j // spo, i, j % spo))
        out_shape = jax.ShapeDtypeStruct((N_DEV, M, ns), out_dtype)
    dims = _DIMS[mode]
    has_res = res is not None
    has_prev = prev is not None

    def body(*refs):
        a_ref, b_ref = refs[0], refs[1]
        res_ref = refs[2] if has_res else None
        o_ref = refs[2 + has_res + has_prev + len(deps)]

        def finish(acc):
            if has_res:
                acc = acc + res_ref[...]
            o_ref[...] = acc.astype(o_ref.dtype)

        if nk == 1:
            finish(lax.dot_general(a_ref[...], b_ref[...], dims, preferred_element_type=F32))
        else:
            acc_ref = refs[3 + has_res + has_prev + len(deps)]
            k = pl.program_id(2)

            @pl.when(k == 0)
            def _():
                acc_ref[...] = jnp.zeros(acc_ref.shape, F32)

            acc_ref[...] += lax.dot_general(a_ref[...], b_ref[...], dims, preferred_element_type=F32)

            @pl.when(k == nk - 1)
            def _():
                finish(acc_ref[...])

    in_specs = [a_spec, b_spec]
    args = [a, b]
    if has_res:
        in_specs.append(pl.BlockSpec((tm, tn), lambda i, j, k: (i, j)))
        args.append(res)
    aliases = {}
    if has_prev:
        aliases = {len(args): 0}
        in_specs.append(pl.BlockSpec(memory_space=pl.ANY))
        args.append(prev)
    for d in deps:
        in_specs.append(pl.BlockSpec(memory_space=pl.ANY))
        args.append(d)
    return pl.pallas_call(
        body,
        name=name,
        grid=(M // tm, N // tn, nk),
        in_specs=in_specs,
        out_specs=o_spec,
        out_shape=out_shape,
        scratch_shapes=[pltpu.VMEM((tm, tn), F32)] if nk > 1 else [],
        input_output_aliases=aliases,
        compiler_params=_params(("parallel", "parallel", "arbitrary")),
    )(*args)


HALO = 16
CH = 256


def _fill_pad(pad_ref, chunk_fn, T):
    z = jnp.zeros((HALO, pad_ref.shape[1]), F32)
    pad_ref[pl.ds(0, HALO), :] = z
    pad_ref[pl.ds(HALO + T, HALO), :] = z

    def body(c, carry):
        r0 = pl.multiple_of(c * CH, CH)
        pad_ref[pl.ds(HALO + r0, CH), :] = chunk_fn(r0)
        return carry

    lax.fori_loop(0, T // CH, body, 0)


def _ext(pad_ref, r0):
    return pad_ref[pl.ds(r0, CH + 2 * HALO), :]


def _tap(ext, o):
    if o == 0:
        return ext[HALO:HALO + CH]
    return pltpu.roll(ext, (-o) % ext.shape[0], axis=0)[HALO:HALO + CH]


def _chunks(T, fn, init=0):
    def body(c, carry):
        return fn(pl.multiple_of(c * CH, CH), carry)

    return lax.fori_loop(0, T // CH, body, init)


def _conv3(ext, w):
    return _tap(ext, -1) * w[0:1] + _tap(ext, 0) * w[1:2] + _tap(ext, 1) * w[2:3]


def _conv3_t(ext, w):
    return _tap(ext, 1) * w[0:1] + _tap(ext, 0) * w[1:2] + _tap(ext, -1) * w[2:3]


def _conv3_dw(xext, d):
    return [jnp.sum(_tap(xext, k - 1) * d, axis=0, keepdims=True) for k in range(3)]


def _sigmoid(x):
    return 0.5 * (jnp.tanh(0.5 * x) + 1.0)


_GELU_C = 0.7978845608028654


def _gelu(x):
    return 0.5 * x * (1.0 + jnp.tanh(_GELU_C * (x + 0.044715 * (x * x * x))))


def _gelu_grad(x):
    t = jnp.tanh(_GELU_C * (x + 0.044715 * (x * x * x)))
    return 0.5 * (1.0 + t) + 0.5 * x * (1.0 - t * t) * (_GELU_C * (1.0 + 3.0 * 0.044715 * (x * x)))


def _row_tile(T):
    return _pick(T, (256, 128))


def _rms_fwd(x, g, name):
    T, D = x.shape
    tm = _row_tile(T)

    def body(x_ref, g_ref, h_ref):
        xv = x_ref[...]
        r = lax.rsqrt(jnp.mean(xv * xv, axis=-1, keepdims=True) + EPS)
        h_ref[...] = (xv * r * g_ref[...]).astype(BF16)

    return pl.pallas_call(
        body, name=name, grid=(T // tm,),
        in_specs=[pl.BlockSpec((tm, D), lambda i: (i, 0)), pl.BlockSpec((1, D), lambda i: (0, 0))],
        out_specs=pl.BlockSpec((tm, D), lambda i: (i, 0)),
        out_shape=jax.ShapeDtypeStruct((T, D), BF16),
        compiler_params=_params(("parallel",)),
    )(x, g)


def _rms_bwd(dh, x, g, dres, name, deps=()):
    T, D = x.shape
    tm = _row_tile(T)

    def body(dh_ref, x_ref, g_ref, dres_ref, *rest):
        dx_ref, dxb_ref, dg_ref = rest[len(deps):]
        i = pl.program_id(0)
        xv = x_ref[...]
        r = lax.rsqrt(jnp.mean(xv * xv, axis=-1, keepdims=True) + EPS)
        n = xv * r
        dh_v = dh_ref[...].astype(F32)
        dn = dh_v * g_ref[...]
        dx = dres_ref[...] + r * (dn - n * jnp.mean(dn * n, axis=-1, keepdims=True))
        dx_ref[...] = dx
        dxb_ref[...] = dx.astype(BF16)
        dg = jnp.sum(dh_v * n, axis=0, keepdims=True)

        @pl.when(i == 0)
        def _():
            dg_ref[...] = dg

        @pl.when(i > 0)
        def _():
            dg_ref[...] += dg

    row = pl.BlockSpec((tm, D), lambda i: (i, 0))
    vec = pl.BlockSpec((1, D), lambda i: (0, 0))
    return pl.pallas_call(
        body, name=name, grid=(T // tm,),
        in_specs=[row, row, vec, row] + [pl.BlockSpec(memory_space=pl.ANY)] * len(deps),
        out_specs=[row, row, vec],
        out_shape=[jax.ShapeDtypeStruct((T, D), F32), jax.ShapeDtypeStruct((T, D), BF16),
                   jax.ShapeDtypeStruct((1, D), F32)],
        compiler_params=_params(("arbitrary",)),
    )(dh, x, g, dres, *deps)


def _loss_head(x, tgt, g, name):
    T, D = x.shape
    tm = _row_tile(T)

    def body(x_ref, t_ref, g_ref, dx_ref, dxb_ref, dg_ref, l_ref):
        i = pl.program_id(0)
        xv = x_ref[...]
        gv = g_ref[...]
        r = lax.rsqrt(jnp.mean(xv * xv, axis=-1, keepdims=True) + EPS)
        n = xv * r
        e = n * gv - t_ref[...]
        dy = e * (1.0 / D)
        dn = dy * gv
        dx = r * (dn - n * jnp.mean(dn * n, axis=-1, keepdims=True))
        dx_ref[...] = dx
        dxb_ref[...] = dx.astype(BF16)
        dg = jnp.sum(dy * n, axis=0, keepdims=True)
        per_tok = jnp.mean(e * e, axis=-1, keepdims=True)
        lv = jnp.broadcast_to(0.5 * jnp.sum(per_tok, axis=0, keepdims=True), (1, LANE))

        @pl.when(i == 0)
        def _():
            dg_ref[...] = dg
            l_ref[...] = lv

        @pl.when(i > 0)
        def _():
            dg_ref[...] += dg
            l_ref[...] += lv

    row = pl.BlockSpec((tm, D), lambda i: (i, 0))
    vec = pl.BlockSpec((1, D), lambda i: (0, 0))
    return pl.pallas_call(
        body, name=name, grid=(T // tm,),
        in_specs=[row, row, vec],
        out_specs=[row, row, vec, pl.BlockSpec((1, LANE), lambda i: (0, 0))],
        out_shape=[jax.ShapeDtypeStruct((T, D), F32), jax.ShapeDtypeStruct((T, D), BF16),
                   jax.ShapeDtypeStruct((1, D), F32), jax.ShapeDtypeStruct((1, LANE), F32)],
        compiler_params=_params(("arbitrary",)),
    )(x, tgt, g)


POOL_WINDOWS = (2, 4, 8, 16)


def _pool_offsets(w):
    return range(-(w // 2), w - w // 2)


def _pool_cnt(r0, w, T, shape):
    t = r0 + lax.broadcasted_iota(jnp.int32, shape, 0)
    lo = jnp.maximum(t - w // 2, 0)
    hi = jnp.minimum(t + (w - w // 2 - 1), T - 1)
    return (hi - lo + 1).astype(F32)


def _pooled(pad_ref, r0, w, T):
    ext = _ext(pad_ref, r0)
    s = None
    for o in _pool_offsets(w):
        tap = _tap(ext, o)
        s = tap if s is None else s + tap
    cur = ext[HALO:HALO + CH]
    return s / _pool_cnt(r0, w, T, cur.shape) - cur


def _pool_fwd(P, w_pool, scale, name):
    T = P.shape[0]
    G, PG, _ = w_pool.shape

    def body(a_ref, w_ref, s_ref, o_ref, pad_ref):
        g = pl.program_id(0)
        _fill_pad(pad_ref, lambda r0: a_ref[pl.ds(r0, CH), :].astype(F32), T)
        for gi, w in enumerate(POOL_WINDOWS):
            @pl.when(g == gi)
            def _(w=w):
                def chunk(r0, carry):
                    pooled = _pooled(pad_ref, r0, w, T).astype(BF16)
                    y = jnp.dot(pooled, w_ref[...], preferred_element_type=F32) * s_ref[...]
                    o_ref[pl.ds(r0, CH), :] = y.astype(BF16)
                    return carry

                _chunks(T, chunk)

    return pl.pallas_call(
        body, name=name, grid=(G,),
        in_specs=[pl.BlockSpec((T, PG), lambda g: (0, g)),
                  pl.BlockSpec((None, PG, PG), lambda g: (g, 0, 0)),
                  pl.BlockSpec((1, PG), lambda g: (0, g))],
        out_specs=pl.BlockSpec((T, PG), lambda g: (0, g)),
        out_shape=jax.ShapeDtypeStruct((T, G * PG), BF16),
        scratch_shapes=[pltpu.VMEM((T + 2 * HALO, PG), F32)],
        compiler_params=_params(("parallel",)),
    )(P, w_pool, scale)


def _pool_bwd(P, dy, w_pool, scale, name):
    T = P.shape[0]
    G, PG, _ = w_pool.shape

    def body(a_ref, dy_ref, w_ref, s_ref, da_ref, dw_ref, ds_ref, pad_ref, gpad_ref):
        g = pl.program_id(0)
        gpad_ref[pl.ds(0, HALO), :] = jnp.zeros((HALO, PG), F32)
        gpad_ref[pl.ds(HALO + T, HALO), :] = jnp.zeros((HALO, PG), F32)
        _fill_pad(pad_ref, lambda r0: a_ref[pl.ds(r0, CH), :].astype(F32), T)
        for gi, w in enumerate(POOL_WINDOWS):
            @pl.when(g == gi)
            def _(w=w):
                def chunk1(r0, carry):
                    dw, ds = carry
                    pooled = _pooled(pad_ref, r0, w, T).astype(BF16)
                    ypre = jnp.dot(pooled, w_ref[...], preferred_element_type=F32)
                    dyv = dy_ref[pl.ds(r0, CH), :].astype(F32)
                    ds = ds + jnp.sum(dyv * ypre, axis=0, keepdims=True)
                    dyp = (dyv * s_ref[...]).astype(BF16)
                    dw = dw + lax.dot_general(pooled, dyp, _DIMS["tn"], preferred_element_type=F32)
                    dpool = lax.dot_general(dyp, w_ref[...], _DIMS["nt"], preferred_element_type=F32)
                    gpad_ref[pl.ds(HALO + r0, CH), :] = dpool / _pool_cnt(r0, w, T, dpool.shape)
                    return dw, ds

                dw, ds = _chunks(T, chunk1, (jnp.zeros((PG, PG), F32), jnp.zeros((1, PG), F32)))
                dw_ref[...] = dw
                ds_ref[...] = ds

                def chunk2(r0, carry):
                    ext = _ext(gpad_ref, r0)
                    cur = ext[HALO:HALO + CH]
                    acc = None
                    for o in _pool_offsets(w):
                        tap = _tap(ext, -o)
                        acc = tap if acc is None else acc + tap
                    da_ref[pl.ds(r0, CH), :] = (acc - cur * _pool_cnt(r0, w, T, cur.shape)).astype(BF16)
                    return carry

                _chunks(T, chunk2)

    col = pl.BlockSpec((T, PG), lambda g: (0, g))
    return pl.pallas_call(
        body, name=name, grid=(G,),
        in_specs=[col, col, pl.BlockSpec((None, PG, PG), lambda g: (g, 0, 0)), pl.BlockSpec((1, PG), lambda g: (0, g))],
        out_specs=[col, pl.BlockSpec((None, PG, PG), lambda g: (g, 0, 0)), pl.BlockSpec((1, PG), lambda g: (0, g))],
        out_shape=[jax.ShapeDtypeStruct((T, G * PG), BF16), jax.ShapeDtypeStruct((G, PG, PG), F32),
                   jax.ShapeDtypeStruct((1, G * PG), F32)],
        scratch_shapes=[pltpu.VMEM((T + 2 * HALO, PG), F32), pltpu.VMEM((T + 2 * HALO, PG), F32)],
        compiler_params=_params(("parallel",)),
    )(P, dy, w_pool, scale)


SGU_CHUNK = 128


def _sgu_common(u_ref, v_ref, gs_ref):
    up = u_ref[...].astype(F32)
    vp = v_ref[...].astype(F32)
    gv = _gelu(vp)
    rv = lax.rsqrt(jnp.mean(gv * gv, axis=-1, keepdims=True) + EPS)
    nrm = gv * rv
    return up, vp, nrm, rv, (nrm * gs_ref[...]).astype(BF16)


def _sgu_fwd(P, sgu_w, b_exp, gs, name):
    T = P.shape[0]
    G, _, SG = b_exp.shape
    DP = G * SG
    tm = _row_tile(T)

    def body(u_ref, v_ref, w_ref, b_ref, gs_ref, o_ref):
        up, _, _, _, vn = _sgu_common(u_ref, v_ref, gs_ref)
        gu = _gelu(up)
        for n in range(tm // SGU_CHUNK):
            rows = slice(n * SGU_CHUNK, (n + 1) * SGU_CHUNK)
            for g in range(G):
                cols = slice(g * SG, (g + 1) * SG)
                z = jnp.dot(w_ref[g], vn[rows, cols], preferred_element_type=F32) + b_ref[g]
                o_ref[rows, cols] = (gu[rows, cols] * z).astype(BF16)

    return pl.pallas_call(
        body, name=name, grid=(T // tm,),
        in_specs=[pl.BlockSpec((tm, DP), lambda i: (i, 1)), pl.BlockSpec((tm, DP), lambda i: (i, 2)),
                  pl.BlockSpec(sgu_w.shape, lambda i: (0, 0, 0)), pl.BlockSpec(b_exp.shape, lambda i: (0, 0, 0)),
                  pl.BlockSpec((1, DP), lambda i: (0, 0))],
        out_specs=pl.BlockSpec((tm, DP), lambda i: (i, 0)),
        out_shape=jax.ShapeDtypeStruct((T, DP), BF16),
        compiler_params=_params(("parallel",)),
    )(P, P, sgu_w, b_exp, gs)


def _sgu_bwd(P, dy, da, sgu_w, b_exp, gs, name):
    T = P.shape[0]
    G, _, SG = b_exp.shape
    DP = G * SG
    tm = _row_tile(T)

    def body(u_ref, v_ref, dy_ref, da_ref, w_ref, b_ref, gs_ref, o_ref, dw_ref, db_ref, dgs_ref, dzs_ref):
        i = pl.program_id(0)
        up, vp, nrm, rv, vn = _sgu_common(u_ref, v_ref, gs_ref)
        gu = _gelu(up)
        dyv = dy_ref[...].astype(F32)
        o_ref[:, 0:DP] = da_ref[...]

        @pl.when(i == 0)
        def _():
            dw_ref[...] = jnp.zeros(dw_ref.shape, F32)
            dzs_ref[...] = jnp.zeros(dzs_ref.shape, F32)
            dgs_ref[...] = jnp.zeros(dgs_ref.shape, F32)

        dgs = jnp.zeros((1, DP), F32)
        for n in range(tm // SGU_CHUNK):
            rows = slice(n * SGU_CHUNK, (n + 1) * SGU_CHUNK)
            dvn_parts = []
            for g in range(G):
                cols = slice(g * SG, (g + 1) * SG)
                vng = vn[rows, cols]
                z = jnp.dot(w_ref[g], vng, preferred_element_type=F32) + b_ref[g]
                dyg = dyv[rows, cols]
                du = dyg * z
                o_ref[rows, DP + g * SG:DP + (g + 1) * SG] = (du * _gelu_grad(up[rows, cols])).astype(BF16)
                dz = dyg * gu[rows, cols]
                dzb = dz.astype(BF16)
                dzs_ref[g] += dz
                dw_ref[g] += lax.dot_general(dzb, vng, _DIMS["nt"], preferred_element_type=F32)
                dvn_parts.append(lax.dot_general(w_ref[g], dzb, _DIMS["tn"], preferred_element_type=F32))
            dvn = jnp.concatenate(dvn_parts, axis=1)
            nr = nrm[rows]
            dgs = dgs + jnp.sum(dvn * nr, axis=0, keepdims=True)
            dn = dvn * gs_ref[...]
            dgv = rv[rows] * (dn - nr * jnp.mean(dn * nr, axis=-1, keepdims=True))
            o_ref[rows, 2 * DP:3 * DP] = (dgv * _gelu_grad(vp[rows])).astype(BF16)
        dgs_ref[...] += dgs

        @pl.when(i == T // tm - 1)
        def _():
            for g in range(G):
                db_ref[g] = jnp.broadcast_to(jnp.sum(dzs_ref[g], axis=1, keepdims=True), (SGU_CHUNK, SG))

    full3 = lambda a: pl.BlockSpec(a.shape, lambda i: (0, 0, 0))
    return pl.pallas_call(
        body, name=name, grid=(T // tm,),
        in_specs=[pl.BlockSpec((tm, DP), lambda i: (i, 1)), pl.BlockSpec((tm, DP), lambda i: (i, 2)),
                  pl.BlockSpec((tm, DP), lambda i: (i, 0)), pl.BlockSpec((tm, DP), lambda i: (i, 0)),
                  full3(sgu_w), full3(b_exp), pl.BlockSpec((1, DP), lambda i: (0, 0))],
        out_specs=[pl.BlockSpec((tm, 3 * DP), lambda i: (i, 0)), full3(sgu_w), full3(b_exp),
                   pl.BlockSpec((1, DP), lambda i: (0, 0))],
        out_shape=[jax.ShapeDtypeStruct((T, 6 * DP), BF16), jax.ShapeDtypeStruct(sgu_w.shape, F32),
                   jax.ShapeDtypeStruct(b_exp.shape, F32), jax.ShapeDtypeStruct((1, DP), F32)],
        scratch_shapes=[pltpu.VMEM(b_exp.shape, F32)],
        compiler_params=_params(("arbitrary",)),
    )(P, P, dy, da, sgu_w, b_exp, gs)


def _conv_fwd(P, cw, name):
    T = P.shape[0]
    DP = cw.shape[1]
    tc = 256
    nb = DP // tc

    def body(xc_ref, bg_ref, cg_ref, w_ref, o_ref, pad_ref):
        _fill_pad(pad_ref, lambda r0: cg_ref[pl.ds(r0, CH), :].astype(F32) * xc_ref[pl.ds(r0, CH), :].astype(F32), T)
        w = w_ref[...]

        def chunk(r0, carry):
            cq = _conv3(_ext(pad_ref, r0), w)
            o_ref[pl.ds(r0, CH), :] = (bg_ref[pl.ds(r0, CH), :].astype(F32) * cq).astype(BF16)
            return carry

        _chunks(T, chunk)

    col = lambda off: pl.BlockSpec((T, tc), lambda j: (0, off * nb + j))
    return pl.pallas_call(
        body, name=name, grid=(nb,),
        in_specs=[col(3), col(4), col(5), pl.BlockSpec((3, tc), lambda j: (0, j))],
        out_specs=pl.BlockSpec((T, tc), lambda j: (0, j)),
        out_shape=jax.ShapeDtypeStruct((T, DP), BF16),
        scratch_shapes=[pltpu.VMEM((T + 2 * HALO, tc), F32)],
        compiler_params=_params(("parallel",)),
    )(P, P, P, cw)


def _conv_bwd(P, dy, cw, dPl, name):
    T = P.shape[0]
    DP = cw.shape[1]
    tc = 256
    nb = DP // tc

    def body(xc_ref, bg_ref, cg_ref, dy_ref, w_ref, prev_ref, o_ref, dw_ref, qpad_ref, dpad_ref):
        del prev_ref
        seg = pl.program_id(1)
        w = w_ref[...]
        rows = lambda ref, r0: ref[pl.ds(r0, CH), :].astype(F32)
        _fill_pad(qpad_ref, lambda r0: rows(cg_ref, r0) * rows(xc_ref, r0), T)
        _fill_pad(dpad_ref, lambda r0: rows(dy_ref, r0) * rows(bg_ref, r0), T)

        @pl.when(seg == 0)
        def _():
            def chunk(r0, carry):
                dq = _conv3_t(_ext(dpad_ref, r0), w)
                o_ref[pl.ds(r0, CH), :] = (dq * rows(cg_ref, r0)).astype(BF16)
                dcq = rows(dy_ref, r0) * rows(bg_ref, r0)
                return [a + b for a, b in zip(carry, _conv3_dw(_ext(qpad_ref, r0), dcq))]

            dws = _chunks(T, chunk, [jnp.zeros((1, tc), F32)] * 3)
            dw_ref[...] = jnp.concatenate(dws + [jnp.zeros((5, tc), F32)], axis=0)

        @pl.when(seg == 1)
        def _():
            def chunk(r0, carry):
                cq = _conv3(_ext(qpad_ref, r0), w)
                o_ref[pl.ds(r0, CH), :] = (rows(dy_ref, r0) * cq).astype(BF16)
                return carry

            _chunks(T, chunk)

        @pl.when(seg == 2)
        def _():
            def chunk(r0, carry):
                dq = _conv3_t(_ext(dpad_ref, r0), w)
                o_ref[pl.ds(r0, CH), :] = (dq * rows(xc_ref, r0)).astype(BF16)
                return carry

            _chunks(T, chunk)

    col = lambda off: pl.BlockSpec((T, tc), lambda j, s: (0, off * nb + j))
    return pl.pallas_call(
        body, name=name, grid=(nb, 3),
        in_specs=[col(3), col(4), col(5), pl.BlockSpec((T, tc), lambda j, s: (0, j)),
                  pl.BlockSpec((3, tc), lambda j, s: (0, j)), pl.BlockSpec(memory_space=pl.ANY)],
        out_specs=[pl.BlockSpec((T, tc), lambda j, s: (0, (3 + s) * nb + j)), pl.BlockSpec((8, tc), lambda j, s: (0, j))],
        out_shape=[jax.ShapeDtypeStruct(dPl.shape, BF16), jax.ShapeDtypeStruct((8, DP), F32)],
        scratch_shapes=[pltpu.VMEM((T + 2 * HALO, tc), F32), pltpu.VMEM((T + 2 * HALO, tc), F32)],
        input_output_aliases={5: 0},
        compiler_params=_params(("parallel", "arbitrary")),
    )(P, P, P, dy, cw, dPl)


def _merge_fwd(P, ys, ws, name):
    T = P.shape[0]
    DP, D = ws[0].shape
    tm = _pick(T, (512, 256, 128))
    tn = _pick(D, (1024, 512, 256))
    goff = (6 * DP) // tn

    def body(ya, yb, yc, wa, wb, wc, ga, gb, gc, oa_ref, ob_ref, oc_ref, m_ref):
        m = None
        for y, w, g, o_ref in ((ya, wa, ga, oa_ref), (yb, wb, gb, ob_ref), (yc, wc, gc, oc_ref)):
            o = jnp.dot(y[...], w[...], preferred_element_type=F32)
            o_ref[...] = o.astype(BF16)
            t = _sigmoid(g[...].astype(F32)) * o
            m = t if m is None else m + t
        m_ref[...] = m.astype(BF16)

    yspec = pl.BlockSpec((tm, DP), lambda i, j: (i, 0))
    wspec = pl.BlockSpec((DP, tn), lambda i, j: (0, j))
    gspec = lambda b: pl.BlockSpec((tm, tn), lambda i, j: (i, goff + b * (D // tn) + j))
    ospec = pl.BlockSpec((tm, tn), lambda i, j: (i, j))
    return pl.pallas_call(
        body, name=name, grid=(T // tm, D // tn),
        in_specs=[yspec] * 3 + [wspec] * 3 + [gspec(0), gspec(1), gspec(2)],
        out_specs=[ospec] * 4,
        out_shape=[jax.ShapeDtypeStruct((T, D), BF16)] * 4,
        compiler_params=_params(("parallel", "parallel")),
    )(*ys, *ws, P, P, P)


def _merge_bwd(P, dM, os_, name):
    T, D = dM.shape
    tm = _row_tile(T)

    def body(dm_ref, oa, ob, oc, g_ref, da_ref, db_ref, dc_ref, dg_ref):
        dm = dm_ref[...].astype(F32)
        for b, (o_ref, d_ref) in enumerate(((oa, da_ref), (ob, db_ref), (oc, dc_ref))):
            s = _sigmoid(g_ref[:, b * D:(b + 1) * D].astype(F32))
            d_ref[...] = (dm * s).astype(BF16)
            dg_ref[:, b * D:(b + 1) * D] = (dm * o_ref[...].astype(F32) * s * (1.0 - s)).astype(BF16)

    row = pl.BlockSpec((tm, D), lambda i: (i, 0))
    return pl.pallas_call(
        body, name=name, grid=(T // tm,),
        in_specs=[row] * 4 + [pl.BlockSpec((tm, 3 * D), lambda i: (i, 1))],
        out_specs=[row] * 3 + [pl.BlockSpec((tm, 3 * D), lambda i: (i, 0))],
        out_shape=[jax.ShapeDtypeStruct((T, D), BF16)] * 3 + [jax.ShapeDtypeStruct((T, 3 * D), BF16)],
        compiler_params=_params(("parallel",)),
    )(dM, *os_, P)


def _glu_fwd(U, fw, name):
    T, F2 = U.shape
    F = F2 // 2
    tc = 256
    nb = F // tc

    def body(ug_ref, uv_ref, wg_ref, wv_ref, o_ref, gpad_ref, vpad_ref):
        _fill_pad(gpad_ref, lambda r0: ug_ref[pl.ds(r0, CH), :].astype(F32), T)
        _fill_pad(vpad_ref, lambda r0: uv_ref[pl.ds(r0, CH), :].astype(F32), T)
        wg, wv = wg_ref[...], wv_ref[...]

        def chunk(r0, carry):
            gate = _conv3(_ext(gpad_ref, r0), wg)
            val = _conv3(_ext(vpad_ref, r0), wv)
            o_ref[pl.ds(r0, CH), :] = (gate * _sigmoid(gate) * val).astype(BF16)
            return carry

        _chunks(T, chunk)

    return pl.pallas_call(
        body, name=name, grid=(nb,),
        in_specs=[pl.BlockSpec((T, tc), lambda j: (0, j)), pl.BlockSpec((T, tc), lambda j: (0, nb + j)),
                  pl.BlockSpec((3, tc), lambda j: (0, j)), pl.BlockSpec((3, tc), lambda j: (0, nb + j))],
        out_specs=pl.BlockSpec((T, tc), lambda j: (0, j)),
        out_shape=jax.ShapeDtypeStruct((T, F), BF16),
        scratch_shapes=[pltpu.VMEM((T + 2 * HALO, tc), F32)] * 2,
        compiler_params=_params(("parallel",)),
    )(U, U, fw, fw)


def _glu_bwd(U, dact, fw, name):
    T, F2 = U.shape
    F = F2 // 2
    tc = 128
    nb = F // tc

    def body(ug_ref, uv_ref, da_ref, wg_ref, wv_ref, dug_ref, duv_ref, dwg_ref, dwv_ref, gpad, vpad, dgpad, dvpad):
        _fill_pad(gpad, lambda r0: ug_ref[pl.ds(r0, CH), :].astype(F32), T)
        _fill_pad(vpad, lambda r0: uv_ref[pl.ds(r0, CH), :].astype(F32), T)
        wg, wv = wg_ref[...], wv_ref[...]

        def chunk1(r0, carry):
            gate = _conv3(_ext(gpad, r0), wg)
            val = _conv3(_ext(vpad, r0), wv)
            s = _sigmoid(gate)
            da = da_ref[pl.ds(r0, CH), :].astype(F32)
            dgpad[pl.ds(HALO + r0, CH), :] = da * val * (s * (1.0 + gate * (1.0 - s)))
            dvpad[pl.ds(HALO + r0, CH), :] = da * (gate * s)
            return carry

        z = jnp.zeros((HALO, tc), F32)
        for p in (dgpad, dvpad):
            p[pl.ds(0, HALO), :] = z
            p[pl.ds(HALO + T, HALO), :] = z
        _chunks(T, chunk1)

        def chunk2(r0, carry):
            new = []
            for pad, dpad, wk, o_ref in ((gpad, dgpad, wg, dug_ref), (vpad, dvpad, wv, duv_ref)):
                dext = _ext(dpad, r0)
                taps = [_tap(dext, 1), _tap(dext, 0), _tap(dext, -1)]
                o_ref[pl.ds(r0, CH), :] = (taps[0] * wk[0:1] + taps[1] * wk[1:2] + taps[2] * wk[2:3]).astype(BF16)
                xs = pad[pl.ds(HALO + r0, CH), :]
                new += [jnp.sum(xs * t, axis=0, keepdims=True) for t in taps]
            return [a + b for a, b in zip(carry, new)]

        dws = _chunks(T, chunk2, [jnp.zeros((1, tc), F32)] * 6)
        dwg_ref[...] = jnp.concatenate(dws[:3] + [jnp.zeros((5, tc), F32)], axis=0)
        dwv_ref[...] = jnp.concatenate(dws[3:] + [jnp.zeros((5, tc), F32)], axis=0)

    lo = pl.BlockSpec((T, tc), lambda j: (0, j))
    hi = pl.BlockSpec((T, tc), lambda j: (0, nb + j))
    wlo = pl.BlockSpec((3, tc), lambda j: (0, j))
    whi = pl.BlockSpec((3, tc), lambda j: (0, nb + j))
    dwspec = pl.BlockSpec((8, tc), lambda j: (0, j))
    return pl.pallas_call(
        body, name=name, grid=(nb,),
        in_specs=[lo, hi, lo, wlo, whi],
        out_specs=[lo, lo, dwspec, dwspec],
        out_shape=[jax.ShapeDtypeStruct((T, F), BF16)] * 2 + [jax.ShapeDtypeStruct((8, F), F32)] * 2,
        scratch_shapes=[pltpu.VMEM((T + 2 * HALO, tc), F32)] * 4,
        compiler_params=_params(("parallel",)),
    )(U, U, dact, fw, fw)


_ANY = pl.BlockSpec(memory_space=pl.ANY)
_MESH = pl.DeviceIdType.MESH


def _mesh_pos():
    return lax.axis_index("x"), lax.axis_index("y"), lax.axis_index("c")


def _other_chips(x, y):
    return [(1 - x, y), (x, 1 - y), (1 - x, 1 - y)]


_HBM = pl.BlockSpec(memory_space=pltpu.HBM)
_SEM = pl.BlockSpec(memory_space=pltpu.SEMAPHORE)
_EFFECT = pltpu.SideEffectType.DATAFLOW_SIDE_EFFECTING


def _in_hbm(a):
    return pltpu.with_memory_space_constraint(a, pltpu.HBM)


def _split_start(groups, name):
    flat, where = [], []
    for bufs, _, _ in groups:
        idx = []
        for b in bufs:
            hit = [i for i, f in enumerate(flat) if f is b]
            if not hit:
                flat.append(b)
            idx.append(hit[0] if hit else len(flat) - 1)
        where.append(idx)
    nb = len(flat)
    ng = len(groups)

    def body(*refs):
        ins = refs[:nb]
        sems = refs[nb:nb + 2 * ng]
        token = refs[-1]
        for gi, (_, make_copies, _) in enumerate(groups):
            for cp in make_copies([ins[i] for i in where[gi]], sems[2 * gi], sems[2 * gi + 1]):
                cp.start()
        token[...] = jnp.zeros(token.shape, token.dtype)

    sem_shapes = []
    for _, _, n in groups:
        sem_shapes += [pltpu.SemaphoreType.DMA((n,)), pltpu.SemaphoreType.DMA((n,))]
    outs = pl.pallas_call(
        body, name=name,
        in_specs=[_HBM] * nb,
        out_specs=[_SEM] * (2 * ng) + [_HBM] * nb + [pl.BlockSpec(memory_space=pltpu.VMEM)],
        out_shape=sem_shapes + [pltpu.HBM(b.shape, b.dtype) for b in flat] + [jax.ShapeDtypeStruct((8, LANE), F32)],
        input_output_aliases={i: 2 * ng + i for i in range(nb)},
        compiler_params=pltpu.CompilerParams(has_side_effects=_EFFECT),
    )(*[_in_hbm(b) for b in flat])
    res = [(outs[2 * gi], outs[2 * gi + 1], [outs[2 * ng + i] for i in where[gi]]) for gi in range(ng)]
    return res, outs[-1]


def _split_wait(started, make_copies, after, name):
    send, recv, bufs = started
    nb = len(bufs)
    after = list(after) if isinstance(after, (list, tuple)) else [after]

    def body(*refs):
        for cp in make_copies(refs[:nb], refs[nb], refs[nb + 1]):
            cp.wait_send()
            cp.wait_recv()

    outs = pl.pallas_call(
        body, name=name,
        in_specs=[_HBM] * nb + [_SEM, _SEM] + [_ANY] * len(after),
        out_specs=[_HBM] * nb,
        out_shape=[pltpu.HBM(b.shape, b.dtype) for b in bufs],
        input_output_aliases={i: i for i in range(nb)},
        compiler_params=pltpu.CompilerParams(has_side_effects=_EFFECT),
    )(*bufs, send, recv, *after)
    return list(outs)


def _slot(buf, s, cols):
    return buf.at[s] if cols is None else buf.at[s, :, pl.ds(cols[0], cols[1])]


def _copies_gather_chips(nbuf, cols=None):
    def make(bufs, send, recv):
        x, y, c = _mesh_pos()
        targets = [(x, y, 1 - c)] + [(px, py, c) for px, py in _other_chips(x, y)]
        cps = []
        for a in range(nbuf):
            mine = _slot(bufs[a], 4 * x + 2 * y + c, cols)
            for k, to in enumerate(targets):
                cps.append(pltpu.make_async_remote_copy(src_ref=mine, dst_ref=mine, send_sem=send.at[4 * a + k],
                                                        recv_sem=recv.at[4 * a + k], device_id=to, device_id_type=_MESH))
        return cps

    return make, 4 * nbuf


def _copies_gather_forward(nbuf, cols=None):
    def make(bufs, send, recv):
        x, y, c = _mesh_pos()
        cps = []
        for a in range(nbuf):
            for j, (px, py) in enumerate(_other_chips(x, y)):
                slot = _slot(bufs[a], 4 * px + 2 * py + c, cols)
                cps.append(pltpu.make_async_remote_copy(src_ref=slot, dst_ref=slot, send_sem=send.at[3 * a + j],
                                                        recv_sem=recv.at[3 * a + j], device_id=(x, y, 1 - c),
                                                        device_id_type=_MESH))
        return cps

    return make, 3 * nbuf


def _copies_rs_sibling(n):
    def make(bufs, send, recv):
        x, y, c = _mesh_pos()
        cps = []
        for a in range(n):
            for q in range(4):
                cps.append(pltpu.make_async_remote_copy(
                    src_ref=bufs[a].at[2 * q + (1 - c)], dst_ref=bufs[n + a].at[q], send_sem=send.at[4 * a + q],
                    recv_sem=recv.at[4 * a + q], device_id=(x, y, 1 - c), device_id_type=_MESH))
        return cps

    return make, 4 * n


def _copies_rs_chips(n):
    def make(bufs, send, recv):
        x, y, c = _mesh_pos()
        cps = []
        for a in range(n):
            for j, (px, py) in enumerate(_other_chips(x, y)):
                cps.append(pltpu.make_async_remote_copy(
                    src_ref=bufs[a].at[2 * px + py], dst_ref=bufs[n + a].at[j], send_sem=send.at[3 * a + j],
                    recv_sem=recv.at[3 * a + j], device_id=(px, py, c), device_id_type=_MESH))
        return cps

    return make, 3 * n


def _cast_place(stacked, layer, slot, dtype, name, deps=()):
    _, R, C = stacked.shape
    tr = _rows_tile(R, C)

    def body(slot_ref, x_ref, *rest):
        del slot_ref
        rest[-1][...] = x_ref[...].astype(dtype)

    return pl.pallas_call(
        body, name=name,
        grid_spec=pltpu.PrefetchScalarGridSpec(
            num_scalar_prefetch=1, grid=(R // tr,),
            in_specs=[pl.BlockSpec((None, tr, C), lambda i, s: (layer, i, 0))] + [_ANY] * len(deps),
            out_specs=pl.BlockSpec((None, tr, C), lambda i, s: (s[0], i, 0))),
        out_shape=jax.ShapeDtypeStruct((N_DEV, R, C), dtype),
        compiler_params=_params(("parallel",)),
    )(slot, stacked, *deps)


def _rows_tile(R, C, budget=2 * MIB):
    if R * C * 4 <= budget:
        return R
    for t in (2048, 1024, 704, 512, 352, 256, 128, 64, 32, 16, 8):
        if R % t == 0 and t * C * 4 <= budget:
            return t
    return R


def _pair_sum(grad, recv, core, name):
    _, R, C = grad.shape
    tr = _rows_tile(R, C, 8 * MIB)

    def body(c_ref, g_ref, r_ref, o_ref):
        del c_ref
        o_ref[...] = (g_ref[...].astype(F32) + r_ref[...].astype(F32)).astype(BF16)

    return pl.pallas_call(
        body, name=name,
        grid_spec=pltpu.PrefetchScalarGridSpec(
            num_scalar_prefetch=1, grid=(4, R // tr),
            in_specs=[pl.BlockSpec((None, tr, C), lambda q, i, c: (2 * q + c[0], i, 0)),
                      pl.BlockSpec((None, tr, C), lambda q, i, c: (q, i, 0))],
            out_specs=pl.BlockSpec((None, tr, C), lambda q, i, c: (q, i, 0))),
        out_shape=jax.ShapeDtypeStruct((4, R, C), BF16),
        compiler_params=_params(("parallel", "parallel")),
    )(core, grad, recv)


ADAM_LR = 0.001
ADAM_B1 = 0.9
ADAM_B2 = 0.999
ADAM_EPS = 1e-08
ADAM_WD = 0.01
ADAM_STEP = 10


def _adamw(w, g, m, v):
    m = ADAM_B1 * m + (1.0 - ADAM_B1) * g
    v = ADAM_B2 * v + (1.0 - ADAM_B2) * (g * g)
    m_hat = m / (1.0 - ADAM_B1 ** ADAM_STEP)
    v_hat = v / (1.0 - ADAM_B2 ** ADAM_STEP)
    delta = -ADAM_LR * (m_hat / (jnp.sqrt(v_hat) + ADAM_EPS) + ADAM_WD * w)
    return delta, m, v


def _adamw_big(pair, recv, w, m, v, chip, layer, prev, name):
    L, R, C = w.shape
    tr = _rows_tile(R, C)
    has_prev = prev is not None

    def body(chip_ref, p_ref, r0_ref, r1_ref, r2_ref, w_ref, m_ref, v_ref, *rest):
        del chip_ref
        g_ref, d_ref, mo_ref, vo_ref = rest[-4:]
        g = ((p_ref[...].astype(F32) + r0_ref[...].astype(F32)) + r1_ref[...].astype(F32)) + r2_ref[...].astype(F32)
        delta, m2, v2 = _adamw(w_ref[...], g, m_ref[...], v_ref[...])
        g_ref[...] = g
        d_ref[...] = delta
        mo_ref[...] = m2
        vo_ref[...] = v2

    lay = pl.BlockSpec((None, tr, C), lambda i, c: (layer, i, 0))
    rspec = lambda j: pl.BlockSpec((None, tr, C), lambda i, c: (j, i, 0))
    in_specs = [pl.BlockSpec((None, tr, C), lambda i, c: (c[0], i, 0)), rspec(0), rspec(1), rspec(2), lay, lay, lay]
    args = [chip, pair, recv, recv, recv, w, m, v]
    aliases = {}
    if has_prev:
        in_specs += [_ANY] * 4
        args += list(prev)
        aliases = {8 + k: k for k in range(4)}
    return pl.pallas_call(
        body, name=name,
        grid_spec=pltpu.PrefetchScalarGridSpec(
            num_scalar_prefetch=1, grid=(R // tr,), in_specs=in_specs, out_specs=[lay] * 4),
        out_shape=[jax.ShapeDtypeStruct((L, R, C), F32)] * 4,
        input_output_aliases=aliases,
        compiler_params=_params(("parallel",)),
    )(*args)


def _sum_slots(gathered, name):
    _, R, C = gathered.shape
    tr = _rows_tile(R, C, 2 * MIB)

    def body(x_ref, o_ref):
        s = x_ref[0]
        for k in range(1, N_DEV):
            s = s + x_ref[k]
        o_ref[...] = s

    return pl.pallas_call(
        body, name=name, grid=(R // tr,),
        in_specs=[pl.BlockSpec((N_DEV, tr, C), lambda i: (0, i, 0))],
        out_specs=pl.BlockSpec((tr, C), lambda i: (i, 0)),
        out_shape=jax.ShapeDtypeStruct((R, C), F32),
        compiler_params=_params(("parallel",)),
    )(gathered)


def _adamw_small(w, g, m, v, name):
    R, C = w.shape
    tr = _rows_tile(R, C, 2 * MIB)

    def body(w_ref, g_ref, m_ref, v_ref, d_ref, mo_ref, vo_ref):
        d_ref[...], mo_ref[...], vo_ref[...] = _adamw(w_ref[...], g_ref[...], m_ref[...], v_ref[...])

    spec = pl.BlockSpec((tr, C), lambda i: (i, 0))
    return pl.pallas_call(
        body, name=name, grid=(R // tr,),
        in_specs=[spec] * 4, out_specs=[spec] * 3,
        out_shape=[jax.ShapeDtypeStruct((R, C), F32)] * 3,
        compiler_params=_params(("parallel",)),
    )(w, g, m, v)


def _pack(arrs):
    parts = []
    for a in arrs:
        flat = a.reshape(-1)
        pad = (-flat.shape[0]) % (8 * LANE)
        if pad:
            flat = jnp.pad(flat, (0, pad))
        parts.append(flat.reshape(-1, LANE))
    return jnp.concatenate(parts, axis=0)


def _unpack(packed, shapes):
    out, r = [], 0
    for s in shapes:
        n = 1
        for d in s:
            n *= d
        rows = -(-n // (8 * LANE)) * 8
        out.append(packed[r:r + rows].reshape(-1)[:n].reshape(s))
        r += rows
    return out


def _to_slices(a, ns):
    K = a.shape[0]
    return a.reshape(K, N_DEV, ns).transpose(1, 0, 2)


def _from_slices(a):
    _, K, ns = a.shape
    return a.transpose(1, 0, 2).reshape(K, N_DEV * ns)


def kernel(x, norm1_g, w_in, w_pool, pool_scale, sgu_norm_g, sgu_w, sgu_b, conv_w, w_pool_out, w_sgu_out, w_conv_out, w_o, norm2_g, w_up, ffn_conv_w, w_down, final_g, loss_target, m_norm1_g, m_w_in, m_w_pool, m_pool_scale, m_sgu_norm_g, m_sgu_w, m_sgu_b, m_conv_w, m_w_pool_out, m_w_sgu_out, m_w_conv_out, m_w_o, m_norm2_g, m_w_up, m_ffn_conv_w, m_w_down, m_final_g, v_norm1_g, v_w_in, v_w_pool, v_pool_scale, v_sgu_norm_g, v_sgu_w, v_sgu_b, v_conv_w, v_w_pool_out, v_w_sgu_out, v_w_conv_out, v_w_o, v_norm2_g, v_w_up, v_ffn_conv_w, v_w_down, v_final_g):
    L = norm1_g.shape[0]
    x0 = x[0]
    tgt = loss_target[0]
    G, PG = w_pool.shape[1], w_pool.shape[3]
    F = w_down.shape[1] * N_DEV
    px, py, pc = _mesh_pos()
    me = 4 * px + 2 * py + pc
    core = jnp.reshape(pc, (1,)).astype(jnp.int32)
    chip = jnp.reshape(2 * px + py, (1,)).astype(jnp.int32)

    slot = jnp.reshape(me, (1,)).astype(jnp.int32)
    ns_in = w_in.shape[2]
    nq = 3 if ns_in % (3 * 256) == 0 else 1
    chunk_cols = [(q * (ns_in // nq), ns_in // nq) for q in range(nq)]
    win0 = _cast_place(w_in, 0, slot, BF16, "place_w_in_0")
    first, tok0 = _split_start([([win0],) + _copies_gather_chips(1, chunk_cols[0])], "ag_start_first")
    win0 = first[0][2][0]
    stage1, group_sizes = [None], [1]
    for l in range(L):
        grp = [[("w_in", w_in, BF16)],
               [("w_pool_out", w_pool_out, BF16), ("w_sgu_out", w_sgu_out, BF16), ("w_conv_out", w_conv_out, BF16),
                ("w_pool", w_pool.reshape(L, -1, PG), BF16), ("conv_w", conv_w, F32), ("w_o", w_o, BF16)],
               [("w_up", w_up, BF16), ("ffn_conv_w", ffn_conv_w, F32)],
               [("w_down", w_down, BF16)]]
        for members in grp[(1 if l == 0 else 0):]:
            bufs = [_cast_place(a, l, slot, dt, f"place_{n}_{l}", [tok0]) for n, a, dt in members]
            stage1.append((bufs,) + _copies_gather_chips(len(members)))
            group_sizes.append(len(members))
    rest, _ = _split_start([([win0],) + _copies_gather_chips(1, cols) for cols in chunk_cols[1:]] + stage1[1:], "ag_start")
    if nq > 1:
        win0 = rest[0][2][0]
    chunk_sems = [first[0][:2]] + [r[:2] for r in rest[:nq - 1]]
    started1 = [None] + rest[nq - 1:]
    started2 = {}

    def gather_forward(k, after):
        n = group_sizes[k]
        bufs = _split_wait(started1[k], stage1[k][1], after, f"ag_wait1_{k}")
        st, tok = _split_start([(bufs,) + _copies_gather_forward(n)], f"ag_fwd_{k}")
        started2[k] = st[0]
        return tok

    def gather_done(k, after):
        return _split_wait(started2.pop(k), _copies_gather_forward(group_sizes[k])[0], after, f"ag_wait2_{k}")

    W, saved = [], []
    xc = x0
    for l in range(L):
        w = dict(
            sguw=sgu_w[l].astype(BF16),
            bexp=jnp.broadcast_to(sgu_b[l].T[:, :, None],
                                  (sgu_b.shape[2], sgu_b.shape[1], sgu_norm_g.shape[1] // sgu_b.shape[2])),
            g1=norm1_g[l][None], g2=norm2_g[l][None], gs=sgu_norm_g[l][None], scale=pool_scale[l][None])
        ka, kb, kc, kd = 4 * l, 4 * l + 1, 4 * l + 2, 4 * l + 3
        h1 = _rms_fwd(xc, w["g1"], f"rms1_fwd_{l}")
        if l == 0:
            P, after = None, h1
            for q, cols in enumerate(chunk_cols):
                (win0,) = _split_wait(chunk_sems[q] + ([win0],), _copies_gather_chips(1, cols)[0], after,
                                      f"ag_wait1_0_{q}")
                st, tok = _split_start([([win0],) + _copies_gather_forward(1, cols)], f"ag_fwd_0_{q}")
                (win0,) = _split_wait(st[0], _copies_gather_forward(1, cols)[0], after, f"ag_wait2_0_{q}")
                P = _matmul(h1, win0, mode="nn", out_dtype=BF16, name=f"proj_in_0_{q}", b_slices=(0, N_DEV),
                            col_chunk=(nq, q) if nq > 1 else None, prev=P, deps=[tok])
                after = P
            w["win3"] = win0
        else:
            (w["win3"],) = gather_done(ka, h1)
            tok = gather_forward(kb, h1)
            P = _matmul(h1, w["win3"], mode="nn", out_dtype=BF16, name=f"proj_in_{l}", b_slices=(0, N_DEV), deps=[tok])
        if l == 0:
            gather_forward(kb, P)
        wa, wb, wc, wp, cw, wo = gather_done(kb, P)
        w.update(wa=_from_slices(wa), wb=_from_slices(wb), wc=_from_slices(wc), cw=_from_slices(cw),
                 wpool=wp.reshape(N_DEV, G, -1, PG).transpose(1, 0, 2, 3).reshape(G, PG, PG),
                 wo=wo.reshape(-1, wo.shape[2]))
        ya = _pool_fwd(P, w["wpool"], w["scale"], f"pool_fwd_{l}")
        yb = _sgu_fwd(P, w["sguw"], w["bexp"], w["gs"], f"sgu_fwd_{l}")
        yc = _conv_fwd(P, w["cw"], f"conv_fwd_{l}")
        oa, ob, oc, M = _merge_fwd(P, (ya, yb, yc), (w["wa"], w["wb"], w["wc"]), f"merge_fwd_{l}")
        tok = gather_forward(kc, M)
        x1 = _matmul(M, w["wo"], mode="nn", out_dtype=F32, name=f"proj_o_{l}", res=xc, deps=[tok])
        h2 = _rms_fwd(x1, w["g2"], f"rms2_fwd_{l}")
        w["wup3"], fw = gather_done(kc, h2)
        w["fw"] = _from_slices(fw)
        tok = gather_forward(kd, h2)
        U = _matmul(h2, w["wup3"], mode="nn", out_dtype=BF16, name=f"proj_up_{l}", b_slices=(0, N_DEV), deps=[tok])
        act = _glu_fwd(U, w["fw"], f"glu_fwd_{l}")
        (wdown,) = gather_done(kd, act)
        w["wdown"] = wdown.reshape(-1, wdown.shape[2])
        deps = [gather_forward(4 * (l + 1), act)] if l + 1 < L else []
        x2 = _matmul(act, w["wdown"], mode="nn", out_dtype=F32, name=f"proj_down_{l}", res=x1, deps=deps,
                     tm=_pick(x1.shape[0], (512, 256)))
        if deps:
            tok = deps[0]
        W.append(w)
        saved.append(dict(x0=xc, h1=h1, P=P, ya=ya, yb=yb, yc=yc, oa=oa, ob=ob, oc=oc, M=M, x1=x1, h2=h2, U=U, act=act))
        xc = x2

    dx, dxb, d_final_g, loss_part = _loss_head(xc, tgt, final_g[None], "loss_head")
    loss = lax.psum(loss_part[0, 0], ("x", "y", "c"))

    wmv = dict(w_in=(w_in, m_w_in, v_w_in), w_up=(w_up, m_w_up, v_w_up), w_o=(w_o, m_w_o, v_w_o),
               w_down=(w_down, m_w_down, v_w_down), w_pool_out=(w_pool_out, m_w_pool_out, v_w_pool_out),
               w_sgu_out=(w_sgu_out, m_w_sgu_out, v_w_sgu_out), w_conv_out=(w_conv_out, m_w_conv_out, v_w_conv_out),
               w_pool=(w_pool, m_w_pool, v_w_pool))
    adam_out = {}

    def rs_begin(tag, layer, names, grads):
        n = len(grads)
        lands = [lax.empty((4,) + g.shape[1:], BF16) for g in grads]
        st, tok = _split_start([(list(grads) + lands,) + _copies_rs_sibling(n)], f"rs_sib_{tag}")
        return dict(tag=tag, layer=layer, names=names, n=n, st=st[0]), tok

    def rs_to_chips(state, after):
        n, tag = state["n"], state["tag"]
        bufs = _split_wait(state["st"], _copies_rs_sibling(n)[0], after, f"rs_sibw_{tag}")
        pairs = [_pair_sum(g, r, core, f"pair_{tag}_{i}") for i, (g, r) in enumerate(zip(bufs[:n], bufs[n:]))]
        lands = [lax.empty((3,) + p.shape[1:], BF16) for p in pairs]
        st, tok = _split_start([(pairs + lands,) + _copies_rs_chips(n)], f"rs_chips_{tag}")
        state["st"] = st[0]
        return tok

    def rs_finish(state, after):
        n, tag, layer = state["n"], state["tag"], state["layer"]
        bufs = _split_wait(state["st"], _copies_rs_chips(n)[0], after, f"rs_chipsw_{tag}")
        for name, pair, recv in zip(state["names"], bufs[:n], bufs[n:]):
            wk, mk, vk = (a.reshape(L, -1, a.shape[-1]) for a in wmv[name])
            adam_out[name] = _adamw_big(pair, recv, wk, mk, vk, chip, layer, adam_out.get(name), f"adamw_{name}_{layer}")

    small = [None] * L
    pending = []
    deps = []
    ns_out = w_pool_out.shape[2]
    for l in reversed(range(L)):
        w, s = W[l], saved[l]
        dact = _matmul(dxb, w["wdown"], mode="nt", out_dtype=BF16, name=f"d_act_{l}", tn=_pick(F, (512, 256)), deps=deps)
        g_down = _matmul(s["act"], dxb, mode="tn", out_dtype=BF16, name=f"g_down_{l}", tm=_pick(F, (512, 256)))
        dug, duv, dfwg, dfwv = _glu_bwd(s["U"], dact, w["fw"], f"glu_bwd_{l}")
        dh2 = _matmul(dug, w["wup3"], mode="nt", out_dtype=F32, name=f"d_h2a_{l}", b_slices=(0, 4))
        dh2 = _matmul(duv, w["wup3"], mode="nt", out_dtype=F32, name=f"d_h2b_{l}", b_slices=(4, 4), res=dh2)
        g_up = _matmul(s["h2"], dug, mode="tn", out_dtype=BF16, name=f"g_upa_{l}", out_slices=(0, 4), tm=512)
        g_up = _matmul(s["h2"], duv, mode="tn", out_dtype=BF16, name=f"g_upb_{l}", out_slices=(4, 4), prev=g_up, tm=512)
        ra, tok = rs_begin(f"a{l}", l, ["w_down", "w_up"], [g_down.reshape(N_DEV, -1, g_down.shape[1]), g_up])
        if pending:
            rs_finish(pending.pop(0), g_up)
        dx1, dx1b, dg2 = _rms_bwd(dh2, s["x1"], w["g2"], dx, f"rms2_bwd_{l}")
        dM = _matmul(dx1b, w["wo"], mode="nt", out_dtype=BF16, name=f"d_m_{l}", deps=[tok])
        g_o = _matmul(s["M"], dx1b, mode="tn", out_dtype=BF16, name=f"g_o_{l}")
        doa, dob, doc, dPg = _merge_bwd(s["P"], dM, (s["oa"], s["ob"], s["oc"]), f"merge_bwd_{l}")
        tok = rs_to_chips(ra, dPg)
        if pending:
            rs_finish(pending.pop(0), dPg)
        dya = _matmul(doa, w["wa"], mode="nt", out_dtype=BF16, name=f"d_ya_{l}", deps=[tok])
        dyb = _matmul(dob, w["wb"], mode="nt", out_dtype=BF16, name=f"d_yb_{l}")
        dyc = _matmul(doc, w["wc"], mode="nt", out_dtype=BF16, name=f"d_yc_{l}")
        g_a = _matmul(s["ya"], doa, mode="tn", out_dtype=BF16, name=f"g_a_{l}")
        g_b = _matmul(s["yb"], dob, mode="tn", out_dtype=BF16, name=f"g_b_{l}")
        g_c = _matmul(s["yc"], doc, mode="tn", out_dtype=BF16, name=f"g_c_{l}")
        da, g_pool, dscale = _pool_bwd(s["P"], dya, w["wpool"], w["scale"], f"pool_bwd_{l}")
        g_pool_s = g_pool.reshape(G, N_DEV, -1, PG).transpose(1, 0, 2, 3).reshape(N_DEV, -1, PG).astype(BF16)
        rb, tok = rs_begin(f"b{l}", l, ["w_o", "w_pool_out", "w_sgu_out", "w_conv_out", "w_pool"],
                           [g_o.reshape(N_DEV, -1, g_o.shape[1]), _to_slices(g_a, ns_out), _to_slices(g_b, ns_out),
                            _to_slices(g_c, ns_out), g_pool_s])
        if pending:
            rs_finish(pending.pop(0), da)
        dPl, g_sguw, db_exp, dgs = _sgu_bwd(s["P"], dyb, da, w["sguw"], w["bexp"], w["gs"], f"sgu_bwd_{l}")
        dPl, dcw = _conv_bwd(s["P"], dyc, w["cw"], dPl, f"conv_bwd_{l}")
        tok2 = rs_to_chips(rb, dPl)
        g_in = _matmul(s["h1"], dPl, mode="tn", out_dtype=BF16, name=f"g_ina_{l}", out_slices=(0, 4), deps=[tok, tok2])
        g_in = _matmul(s["h1"], dPg, mode="tn", out_dtype=BF16, name=f"g_inb_{l}", out_slices=(4, 4), prev=g_in)
        rc, tok = rs_begin(f"c{l}", l, ["w_in"], [g_in])
        dh1 = _matmul(dPl, w["win3"], mode="nt", out_dtype=F32, name=f"d_h1a_{l}", b_slices=(0, 4), deps=[tok])
        tok = rs_to_chips(rc, dh1)
        dh1 = _matmul(dPg, w["win3"], mode="nt", out_dtype=F32, name=f"d_h1b_{l}", b_slices=(4, 4), res=dh1, deps=[tok])
        dx, dxb, dg1 = _rms_bwd(dh1, s["x0"], w["g1"], dx1, f"rms1_bwd_{l}")
        deps = []
        pending = [ra, rb, rc]
        small[l] = dict(norm1_g=dg1[0], pool_scale=dscale[0], sgu_norm_g=dgs[0], sgu_w=g_sguw,
                        sgu_b=db_exp[:, :, 0].T, conv_w=dcw[:3], norm2_g=dg2[0],
                        ffn_conv_w=jnp.concatenate([dfwg[:3], dfwv[:3]], axis=1))
    grad_x = dx[None]

    snames = ["norm1_g", "pool_scale", "sgu_norm_g", "sgu_w", "sgu_b", "conv_w", "norm2_g", "ffn_conv_w"]
    sparts = [jnp.stack([small[l][n] for l in range(L)]) for n in snames] + [d_final_g[0]]
    snames = snames + ["final_g"]
    packed = _pack(sparts)
    sbuf = _cast_place(packed[None], 0, slot, F32, "place_small")
    st, tok = _split_start([([sbuf],) + _copies_gather_chips(1)], "ag_small_start")
    rs_finish(pending[0], tok)
    bufs = _split_wait(st[0], _copies_gather_chips(1)[0], [adam_out[n][0] for n in pending[0]["names"]],
                       "ag_small_wait1")
    st, tok = _split_start([(bufs,) + _copies_gather_forward(1)], "ag_small_fwd")
    rs_finish(pending[1], tok)
    (gathered_small,) = _split_wait(st[0], _copies_gather_forward(1)[0],
                                    [adam_out[n][0] for n in pending[1]["names"]], "ag_small_wait2")
    total = _sum_slots(gathered_small, "sum_small")
    sgrads = dict(zip(snames, _unpack(total, [p.shape for p in sparts])))
    for n, width in (("conv_w", conv_w.shape[2]), ("ffn_conv_w", ffn_conv_w.shape[2])):
        sgrads[n] = lax.dynamic_slice_in_dim(sgrads[n], me * width, width, axis=2)
    sw = dict(norm1_g=(norm1_g, m_norm1_g, v_norm1_g), pool_scale=(pool_scale, m_pool_scale, v_pool_scale),
              sgu_norm_g=(sgu_norm_g, m_sgu_norm_g, v_sgu_norm_g), sgu_w=(sgu_w, m_sgu_w, v_sgu_w),
              sgu_b=(sgu_b, m_sgu_b, v_sgu_b), conv_w=(conv_w, m_conv_w, v_conv_w),
              norm2_g=(norm2_g, m_norm2_g, v_norm2_g), ffn_conv_w=(ffn_conv_w, m_ffn_conv_w, v_ffn_conv_w),
              final_g=(final_g, m_final_g, v_final_g))
    shapes = [sw[n][0].shape for n in snames]
    upd = _adamw_small(_pack([sw[n][0] for n in snames]), _pack([sgrads[n] for n in snames]),
                       _pack([sw[n][1] for n in snames]), _pack([sw[n][2] for n in snames]), "adamw_small")
    sdelta, sm, sv = (dict(zip(snames, _unpack(u, shapes))) for u in upd)
    res = {n: [sgrads[n], sdelta[n], sm[n], sv[n]] for n in snames}

    rs_finish(pending[2], [upd[0]] + [o[0] for o in adam_out.values()])
    for n, outs in adam_out.items():
        res[n] = [o.reshape(wmv[n][0].shape) for o in outs]

    order = ["norm1_g", "w_in", "w_pool", "pool_scale", "sgu_norm_g", "sgu_w", "sgu_b", "conv_w", "w_pool_out",
             "w_sgu_out", "w_conv_out", "w_o", "norm2_g", "w_up", "ffn_conv_w", "w_down", "final_g"]
    return (loss, grad_x) + tuple(res[n][k] for k in range(4) for n in order)
```

```python
import functools

import jax
import jax.numpy as jnp
from jax import lax
from jax.experimental import pallas as pl
from jax.experimental.pallas import tpu as pltpu

BF16 = jnp.bfloat16
F32 = jnp.float32
EPS = 1e-6
MIB = 1024 * 1024
V7X_VMEM_BYTES = 64 * MIB
VMEM_LIMIT = 56 * MIB
LANE = 128
N_DEV = 8


def _params(sem, **kw):
    return pltpu.CompilerParams(dimension_semantics=sem, vmem_limit_bytes=VMEM_LIMIT, **kw)


def _pick(n, cands):
    for c in cands:
        if n % c == 0:
            return c
    return n


_DIMS = {"nn": (((1,), (0,)), ((), ())), "nt": (((1,), (1,)), ((), ())), "tn": (((0,), (0,)), ((), ()))}


def _matmul(a, b, *, mode, out_dtype, name, res=None, b_slices=None, out_slices=None, prev=None,
            deps=(), col_chunk=None, tm=None, tn=None, tk=None):
    soff, scnt = b_slices if b_slices is not None else (0, 1)
    ooff, ocnt = out_slices if out_slices is not None else (0, 1)
    if mode == "nn":
        M = a.shape[0]
        Kc = a.shape[1]
        ns = b.shape[-1]
        N = scnt * ns
        if col_chunk is not None:
            nq, cq = col_chunk
            tn = ns // nq
            N = scnt * tn
    elif mode == "nt":
        M = a.shape[0]
        N = b.shape[-2]
        ns = b.shape[-1]
        Kc = scnt * ns
    else:
        Kc, M = a.shape
        N = b.shape[1]
        ns = N // ocnt
    tm = tm or _pick(M, (1024, 512, 256, 128))
    if tn is None:
        lim = ns if (mode == "nn" and b_slices is not None) or out_slices is not None else N
        tn = _pick(lim, (1024, 768, 512, 256, 128)) if lim > 1408 else lim
    if tk is None:
        if mode == "nt" and b_slices is not None:
            tk = ns
        elif mode == "tn":
            tk = Kc if Kc <= 4096 else _pick(Kc, (4096, 2048))
        else:
            tk = Kc if Kc <= 2816 else _pick(Kc, (2816, 2048, 1536, 1408, 1024, 512))
    nk = Kc // tk
    assert M % tm == 0 and N % tn == 0 and Kc % tk == 0, (name, M, N, Kc, tm, tn, tk)
    if mode == "nn":
        a_spec = pl.BlockSpec((tm, tk), lambda i, j, k: (i, k))
        if b_slices is None:
            b_spec = pl.BlockSpec((tk, tn), lambda i, j, k: (k, j))
        else:
            spn = ns // tn
            assert ns % tn == 0
            b_spec = pl.BlockSpec((None, tk, tn), lambda i, j, k: (soff + j // spn, k, j % spn))
            if col_chunk is not None:
                b_spec = pl.BlockSpec((None, tk, tn), lambda i, j, k: (soff + j, k, cq))
    elif mode == "nt":
        a_spec = pl.BlockSpec((tm, tk), lambda i, j, k: (i, k))
        if b_slices is None:
            b_spec = pl.BlockSpec((tn, tk), lambda i, j, k: (j, k))
        else:
            spk = ns // tk
            assert ns % tk == 0
            b_spec = pl.BlockSpec((None, tn, tk), lambda i, j, k: (soff + k // spk, j, k % spk))
    else:
        a_spec = pl.BlockSpec((tk, tm), lambda i, j, k: (k, i))
        b_spec = pl.BlockSpec((tk, tn), lambda i, j, k: (k, j))
    if col_chunk is not None:
        o_spec = pl.BlockSpec((tm, tn), lambda i, j, k: (i, j * nq + cq))
        out_shape = jax.ShapeDtypeStruct((M, scnt * ns), out_dtype)
    elif out_slices is None:
        o_spec = pl.BlockSpec((tm, tn), lambda i, j, k: (i, j))
        out_shape = jax.ShapeDtypeStruct((M, N), out_dtype)
    else:
        spo = ns // tn
        assert mode == "tn" and ns % tn == 0
        o_spec = pl.BlockSpec((None, tm, tn), lambda i, j, k: (ooff + j // spo, i, j % spo))
        out_shape = jax.ShapeDtypeStruct((N_DEV, M, ns), out_dtype)
    dims = _DIMS[mode]
    has_res = res is not None
    has_prev = prev is not None

    def body(*refs):
        a_ref, b_ref = refs[0], refs[1]
        res_ref = refs[2] if has_res else None
        o_ref = refs[2 + has_res + has_prev + len(deps)]

        def finish(acc):
            if has_res:
                acc = acc + res_ref[...]
            o_ref[...] = acc.astype(o_ref.dtype)

        if nk == 1:
            finish(lax.dot_general(a_ref[...], b_ref[...], dims, preferred_element_type=F32))
        else:
            acc_ref = refs[3 + has_res + has_prev + len(deps)]
            k = pl.program_id(2)

            @pl.when(k == 0)
            def _():
                acc_ref[...] = jnp.zeros(acc_ref.shape, F32)

            acc_ref[...] += lax.dot_general(a_ref[...], b_ref[...], dims, preferred_element_type=F32)

            @pl.when(k == nk - 1)
            def _():
                finish(acc_ref[...])

    in_specs = [a_spec, b_spec]
    args = [a, b]
    if has_res:
        in_specs.append(pl.BlockSpec((tm, tn), lambda i, j, k: (i, j)))
        args.append(res)
    aliases = {}
    if has_prev:
        aliases = {len(args): 0}
        in_specs.append(pl.BlockSpec(memory_space=pl.ANY))
        args.append(prev)
    for d in deps:
        in_specs.append(pl.BlockSpec(memory_space=pl.ANY))
        args.append(d)
    return pl.pallas_call(
        body,
        name=name,
        grid=(M // tm, N // tn, nk),
        in_specs=in_specs,
        out_specs=o_spec,
        out_shape=out_shape,
        scratch_shapes=[pltpu.VMEM((tm, tn), F32)] if nk > 1 else [],
        input_output_aliases=aliases,
        compiler_params=_params(("parallel", "parallel", "arbitrary")),
    )(*args)


HALO = 16
CH = 256


def _fill_pad(pad_ref, chunk_fn, T):
    z = jnp.zeros((HALO, pad_ref.shape[1]), F32)
    pad_ref[pl.ds(0, HALO), :] = z
    pad_ref[pl.ds(HALO + T, HALO), :] = z

    def body(c, carry):
        r0 = pl.multiple_of(c * CH, CH)
        pad_ref[pl.ds(HALO + r0, CH), :] = chunk_fn(r0)
        return carry

    lax.fori_loop(0, T // CH, body, 0)


def _ext(pad_ref, r0):
    return pad_ref[pl.ds(r0, CH + 2 * HALO), :]


def _tap(ext, o):
    if o == 0:
        return ext[HALO:HALO + CH]
    return pltpu.roll(ext, (-o) % ext.shape[0], axis=0)[HALO:HALO + CH]


def _chunks(T, fn, init=0):
    def body(c, carry):
        return fn(pl.multiple_of(c * CH, CH), carry)

    return lax.fori_loop(0, T // CH, body, init)


def _conv3(ext, w):
    return _tap(ext, -1) * w[0:1] + _tap(ext, 0) * w[1:2] + _tap(ext, 1) * w[2:3]


def _conv3_t(ext, w):
    return _tap(ext, 1) * w[0:1] + _tap(ext, 0) * w[1:2] + _tap(ext, -1) * w[2:3]


def _conv3_dw(xext, d):
    return [jnp.sum(_tap(xext, k - 1) * d, axis=0, keepdims=True) for k in range(3)]


def _sigmoid(x):
    return 0.5 * (jnp.tanh(0.5 * x) + 1.0)


_GELU_C = 0.7978845608028654


def _gelu(x):
    return 0.5 * x * (1.0 + jnp.tanh(_GELU_C * (x + 0.044715 * (x * x * x))))


def _gelu_grad(x):
    t = jnp.tanh(_GELU_C * (x + 0.044715 * (x * x * x)))
    return 0.5 * (1.0 + t) + 0.5 * x * (1.0 - t * t) * (_GELU_C * (1.0 + 3.0 * 0.044715 * (x * x)))


def _row_tile(T):
    return _pick(T, (256, 128))


def _rms_fwd(x, g, name):
    T, D = x.shape
    tm = _row_tile(T)

    def body(x_ref, g_ref, h_ref):
        xv = x_ref[...]
        r = lax.rsqrt(jnp.mean(xv * xv, axis=-1, keepdims=True) + EPS)
        h_ref[...] = (xv * r * g_ref[...]).astype(BF16)

    return pl.pallas_call(
        body, name=name, grid=(T // tm,),
        in_specs=[pl.BlockSpec((tm, D), lambda i: (i, 0)), pl.BlockSpec((1, D), lambda i: (0, 0))],
        out_specs=pl.BlockSpec((tm, D), lambda i: (i, 0)),
        out_shape=jax.ShapeDtypeStruct((T, D), BF16),
        compiler_params=_params(("parallel",)),
    )(x, g)


def _rms_bwd(dh, x, g, dres, name, deps=()):
    T, D = x.shape
    tm = _row_tile(T)

    def body(dh_ref, x_ref, g_ref, dres_ref, *rest):
        dx_ref, dxb_ref, dg_ref = rest[len(deps):]
        i = pl.program_id(0)
        xv = x_ref[...]
        r = lax.rsqrt(jnp.mean(xv * xv, axis=-1, keepdims=True) + EPS)
        n = xv * r
        dh_v = dh_ref[...].astype(F32)
        dn = dh_v * g_ref[...]
        dx = dres_ref[...] + r * (dn - n * jnp.mean(dn * n, axis=-1, keepdims=True))
        dx_ref[...] = dx
        dxb_ref[...] = dx.astype(BF16)
        dg = jnp.sum(dh_v * n, axis=0, keepdims=True)

        @pl.when(i == 0)
        def _():
            dg_ref[...] = dg

        @pl.when(i > 0)
        def _():
            dg_ref[...] += dg

    row = pl.BlockSpec((tm, D), lambda i: (i, 0))
    vec = pl.BlockSpec((1, D), lambda i: (0, 0))
    return pl.pallas_call(
        body, name=name, grid=(T // tm,),
        in_specs=[row, row, vec, row] + [pl.BlockSpec(memory_space=pl.ANY)] * len(deps),
        out_specs=[row, row, vec],
        out_shape=[jax.ShapeDtypeStruct((T, D), F32), jax.ShapeDtypeStruct((T, D), BF16),
                   jax.ShapeDtypeStruct((1, D), F32)],
        compiler_params=_params(("arbitrary",)),
    )(dh, x, g, dres, *deps)


def _loss_head(x, tgt, g, name):
    T, D = x.shape
    tm = _row_tile(T)

    def body(x_ref, t_ref, g_ref, dx_ref, dxb_ref, dg_ref, l_ref):
        i = pl.program_id(0)
        xv = x_ref[...]
        gv = g_ref[...]
        r = lax.rsqrt(jnp.mean(xv * xv, axis=-1, keepdims=True) + EPS)
        n = xv * r
        e = n * gv - t_ref[...]
        dy = e * (1.0 / D)
        dn = dy * gv
        dx = r * (dn - n * jnp.mean(dn * n, axis=-1, keepdims=True))
        dx_ref[...] = dx
        dxb_ref[...] = dx.astype(BF16)
        dg = jnp.sum(dy * n, axis=0, keepdims=True)
        per_tok = jnp.mean(e * e, axis=-1, keepdims=True)
        lv = jnp.broadcast_to(0.5 * jnp.sum(per_tok, axis=0, keepdims=True), (1, LANE))

        @pl.when(i == 0)
        def _():
            dg_ref[...] = dg
            l_ref[...] = lv

        @pl.when(i > 0)
        def _():
            dg_ref[...] += dg
            l_ref[...] += lv

    row = pl.BlockSpec((tm, D), lambda i: (i, 0))
    vec = pl.BlockSpec((1, D), lambda i: (0, 0))
    return pl.pallas_call(
        body, name=name, grid=(T // tm,),
        in_specs=[row, row, vec],
        out_specs=[row, row, vec, pl.BlockSpec((1, LANE), lambda i: (0, 0))],
        out_shape=[jax.ShapeDtypeStruct((T, D), F32), jax.ShapeDtypeStruct((T, D), BF16),
                   jax.ShapeDtypeStruct((1, D), F32), jax.ShapeDtypeStruct((1, LANE), F32)],
        compiler_params=_params(("arbitrary",)),
    )(x, tgt, g)


POOL_WINDOWS = (2, 4, 8, 16)


def _pool_offsets(w):
    return range(-(w // 2), w - w // 2)


def _pool_cnt(r0, w, T, shape):
    t = r0 + lax.broadcasted_iota(jnp.int32, shape, 0)
    lo = jnp.maximum(t - w // 2, 0)
    hi = jnp.minimum(t + (w - w // 2 - 1), T - 1)
    return (hi - lo + 1).astype(F32)


def _pooled(pad_ref, r0, w, T):
    ext = _ext(pad_ref, r0)
    s = None
    for o in _pool_offsets(w):
        tap = _tap(ext, o)
        s = tap if s is None else s + tap
    cur = ext[HALO:HALO + CH]
    return s / _pool_cnt(r0, w, T, cur.shape) - cur


def _pool_fwd(P, w_pool, scale, name):
    T = P.shape[0]
    G, PG, _ = w_pool.shape

    def body(a_ref, w_ref, s_ref, o_ref, pad_ref):
        g = pl.program_id(0)
        _fill_pad(pad_ref, lambda r0: a_ref[pl.ds(r0, CH), :].astype(F32), T)
        for gi, w in enumerate(POOL_WINDOWS):
            @pl.when(g == gi)
            def _(w=w):
                def chunk(r0, carry):
                    pooled = _pooled(pad_ref, r0, w, T).astype(BF16)
                    y = jnp.dot(pooled, w_ref[...], preferred_element_type=F32) * s_ref[...]
                    o_ref[pl.ds(r0, CH), :] = y.astype(BF16)
                    return carry

                _chunks(T, chunk)

    return pl.pallas_call(
        body, name=name, grid=(G,),
        in_specs=[pl.BlockSpec((T, PG), lambda g: (0, g)),
                  pl.BlockSpec((None, PG, PG), lambda g: (g, 0, 0)),
                  pl.BlockSpec((1, PG), lambda g: (0, g))],
        out_specs=pl.BlockSpec((T, PG), lambda g: (0, g)),
        out_shape=jax.ShapeDtypeStruct((T, G * PG), BF16),
        scratch_shapes=[pltpu.VMEM((T + 2 * HALO, PG), F32)],
        compiler_params=_params(("parallel",)),
    )(P, w_pool, scale)


def _pool_bwd(P, dy, w_pool, scale, name):
    T = P.shape[0]
    G, PG, _ = w_pool.shape

    def body(a_ref, dy_ref, w_ref, s_ref, da_ref, dw_ref, ds_ref, pad_ref, gpad_ref):
        g = pl.program_id(0)
        gpad_ref[pl.ds(0, HALO), :] = jnp.zeros((HALO, PG), F32)
        gpad_ref[pl.ds(HALO + T, HALO), :] = jnp.zeros((HALO, PG), F32)
        _fill_pad(pad_ref, lambda r0: a_ref[pl.ds(r0, CH), :].astype(F32), T)
        for gi, w in enumerate(POOL_WINDOWS):
            @pl.when(g == gi)
            def _(w=w):
                def chunk1(r0, carry):
                    dw, ds = carry
                    pooled = _pooled(pad_ref, r0, w, T).astype(BF16)
                    ypre = jnp.dot(pooled, w_ref[...], preferred_element_type=F32)
                    dyv = dy_ref[pl.ds(r0, CH), :].astype(F32)
                    ds = ds + jnp.sum(dyv * ypre, axis=0, keepdims=True)
                    dyp = (dyv * s_ref[...]).astype(BF16)
                    dw = dw + lax.dot_general(pooled, dyp, _DIMS["tn"], preferred_element_type=F32)
                    dpool = lax.dot_general(dyp, w_ref[...], _DIMS["nt"], preferred_element_type=F32)
                    gpad_ref[pl.ds(HALO + r0, CH), :] = dpool / _pool_cnt(r0, w, T, dpool.shape)
                    return dw, ds

                dw, ds = _chunks(T, chunk1, (jnp.zeros((PG, PG), F32), jnp.zeros((1, PG), F32)))
                dw_ref[...] = dw
                ds_ref[...] = ds

                def chunk2(r0, carry):
                    ext = _ext(gpad_ref, r0)
                    cur = ext[HALO:HALO + CH]
                    acc = None
                    for o in _pool_offsets(w):
                        tap = _tap(ext, -o)
                        acc = tap if acc is None else acc + tap
                    da_ref[pl.ds(r0, CH), :] = (acc - cur * _pool_cnt(r0, w, T, cur.shape)).astype(BF16)
                    return carry

                _chunks(T, chunk2)

    col = pl.BlockSpec((T, PG), lambda g: (0, g))
    return pl.pallas_call(
        body, name=name, grid=(G,),
        in_specs=[col, col, pl.BlockSpec((None, PG, PG), lambda g: (g, 0, 0)), pl.BlockSpec((1, PG), lambda g: (0, g))],
        out_specs=[col, pl.BlockSpec((None, PG, PG), lambda g: (g, 0, 0)), pl.BlockSpec((1, PG), lambda g: (0, g))],
        out_shape=[jax.ShapeDtypeStruct((T, G * PG), BF16), jax.ShapeDtypeStruct((G, PG, PG), F32),
                   jax.ShapeDtypeStruct((1, G * PG), F32)],
        scratch_shapes=[pltpu.VMEM((T + 2 * HALO, PG), F32), pltpu.VMEM((T + 2 * HALO, PG), F32)],
        compiler_params=_params(("parallel",)),
    )(P, dy, w_pool, scale)


SGU_CHUNK = 128


def _sgu_common(u_ref, v_ref, gs_ref):
    up = u_ref[...].astype(F32)
    vp = v_ref[...].astype(F32)
    gv = _gelu(vp)
    rv = lax.rsqrt(jnp.mean(gv * gv, axis=-1, keepdims=True) + EPS)
    nrm = gv * rv
    return up, vp, nrm, rv, (nrm * gs_ref[...]).astype(BF16)


def _sgu_fwd(P, sgu_w, b_exp, gs, name):
    T = P.shape[0]
    G, _, SG = b_exp.shape
    DP = G * SG
    tm = _row_tile(T)

    def body(u_ref, v_ref, w_ref, b_ref, gs_ref, o_ref):
        up, _, _, _, vn = _sgu_common(u_ref, v_ref, gs_ref)
        gu = _gelu(up)
        for n in range(tm // SGU_CHUNK):
            rows = slice(n * SGU_CHUNK, (n + 1) * SGU_CHUNK)
            for g in range(G):
                cols = slice(g * SG, (g + 1) * SG)
                z = jnp.dot(w_ref[g], vn[rows, cols], preferred_element_type=F32) + b_ref[g]
                o_ref[rows, cols] = (gu[rows, cols] * z).astype(BF16)

    return pl.pallas_call(
        body, name=name, grid=(T // tm,),
        in_specs=[pl.BlockSpec((tm, DP), lambda i: (i, 1)), pl.BlockSpec((tm, DP), lambda i: (i, 2)),
                  pl.BlockSpec(sgu_w.shape, lambda i: (0, 0, 0)), pl.BlockSpec(b_exp.shape, lambda i: (0, 0, 0)),
                  pl.BlockSpec((1, DP), lambda i: (0, 0))],
        out_specs=pl.BlockSpec((tm, DP), lambda i: (i, 0)),
        out_shape=jax.ShapeDtypeStruct((T, DP), BF16),
        compiler_params=_params(("parallel",)),
    )(P, P, sgu_w, b_exp, gs)


def _sgu_bwd(P, dy, da, sgu_w, b_exp, gs, name):
    T = P.shape[0]
    G, _, SG = b_exp.shape
    DP = G * SG
    tm = _row_tile(T)

    def body(u_ref, v_ref, dy_ref, da_ref, w_ref, b_ref, gs_ref, o_ref, dw_ref, db_ref, dgs_ref, dzs_ref):
        i = pl.program_id(0)
        up, vp, nrm, rv, vn = _sgu_common(u_ref, v_ref, gs_ref)
        gu = _gelu(up)
        dyv = dy_ref[...].astype(F32)
        o_ref[:, 0:DP] = da_ref[...]

        @pl.when(i == 0)
        def _():
            dw_ref[...] = jnp.zeros(dw_ref.shape, F32)
            dzs_ref[...] = jnp.zeros(dzs_ref.shape, F32)
            dgs_ref[...] = jnp.zeros(dgs_ref.shape, F32)

        dgs = jnp.zeros((1, DP), F32)
        for n in range(tm // SGU_CHUNK):
            rows = slice(n * SGU_CHUNK, (n + 1) * SGU_CHUNK)
            dvn_parts = []
            for g in range(G):
                cols = slice(g * SG, (g + 1) * SG)
                vng = vn[rows, cols]
                z = jnp.dot(w_ref[g], vng, preferred_element_type=F32) + b_ref[g]
                dyg = dyv[rows, cols]
                du = dyg * z
                o_ref[rows, DP + g * SG:DP + (g + 1) * SG] = (du * _gelu_grad(up[rows, cols])).astype(BF16)
                dz = dyg * gu[rows, cols]
                dzb = dz.astype(BF16)
                dzs_ref[g] += dz
                dw_ref[g] += lax.dot_general(dzb, vng, _DIMS["nt"], preferred_element_type=F32)
                dvn_parts.append(lax.dot_general(w_ref[g], dzb, _DIMS["tn"], preferred_element_type=F32))
            dvn = jnp.concatenate(dvn_parts, axis=1)
            nr = nrm[rows]
            dgs = dgs + jnp.sum(dvn * nr, axis=0, keepdims=True)
            dn = dvn * gs_ref[...]
            dgv = rv[rows] * (dn - nr * jnp.mean(dn * nr, axis=-1, keepdims=True))
            o_ref[rows, 2 * DP:3 * DP] = (dgv * _gelu_grad(vp[rows])).astype(BF16)
        dgs_ref[...] += dgs

        @pl.when(i == T // tm - 1)
        def _():
            for g in range(G):
                db_ref[g] = jnp.broadcast_to(jnp.sum(dzs_ref[g], axis=1, keepdims=True), (SGU_CHUNK, SG))

    full3 = lambda a: pl.BlockSpec(a.shape, lambda i: (0, 0, 0))
    return pl.pallas_call(
        body, name=name, grid=(T // tm,),
        in_specs=[pl.BlockSpec((tm, DP), lambda i: (i, 1)), pl.BlockSpec((tm, DP), lambda i: (i, 2)),
                  pl.BlockSpec((tm, DP), lambda i: (i, 0)), pl.BlockSpec((tm, DP), lambda i: (i, 0)),
                  full3(sgu_w), full3(b_exp), pl.BlockSpec((1, DP), lambda i: (0, 0))],
        out_specs=[pl.BlockSpec((tm, 3 * DP), lambda i: (i, 0)), full3(sgu_w), full3(b_exp),
                   pl.BlockSpec((1, DP), lambda i: (0, 0))],
        out_shape=[jax.ShapeDtypeStruct((T, 6 * DP), BF16), jax.ShapeDtypeStruct(sgu_w.shape, F32),
                   jax.ShapeDtypeStruct(b_exp.shape, F32), jax.ShapeDtypeStruct((1, DP), F32)],
        scratch_shapes=[pltpu.VMEM(b_exp.shape, F32)],
        compiler_params=_params(("arbitrary",)),
    )(P, P, dy, da, sgu_w, b_exp, gs)


def _conv_fwd(P, cw, name):
    T = P.shape[0]
    DP = cw.shape[1]
    tc = 256
    nb = DP // tc

    def body(xc_ref, bg_ref, cg_ref, w_ref, o_ref, pad_ref):
        _fill_pad(pad_ref, lambda r0: cg_ref[pl.ds(r0, CH), :].astype(F32) * xc_ref[pl.ds(r0, CH), :].astype(F32), T)
        w = w_ref[...]

        def chunk(r0, carry):
            cq = _conv3(_ext(pad_ref, r0), w)
            o_ref[pl.ds(r0, CH), :] = (bg_ref[pl.ds(r0, CH), :].astype(F32) * cq).astype(BF16)
            return carry

        _chunks(T, chunk)

    col = lambda off: pl.BlockSpec((T, tc), lambda j: (0, off * nb + j))
    return pl.pallas_call(
        body, name=name, grid=(nb,),
        in_specs=[col(3), col(4), col(5), pl.BlockSpec((3, tc), lambda j: (0, j))],
        out_specs=pl.BlockSpec((T, tc), lambda j: (0, j)),
        out_shape=jax.ShapeDtypeStruct((T, DP), BF16),
        scratch_shapes=[pltpu.VMEM((T + 2 * HALO, tc), F32)],
        compiler_params=_params(("parallel",)),
    )(P, P, P, cw)


def _conv_bwd(P, dy, cw, dPl, name):
    T = P.shape[0]
    DP = cw.shape[1]
    tc = 256
    nb = DP // tc

    def body(xc_ref, bg_ref, cg_ref, dy_ref, w_ref, prev_ref, o_ref, dw_ref, qpad_ref, dpad_ref):
        del prev_ref
        seg = pl.program_id(1)
        w = w_ref[...]
        rows = lambda ref, r0: ref[pl.ds(r0, CH), :].astype(F32)
        _fill_pad(qpad_ref, lambda r0: rows(cg_ref, r0) * rows(xc_ref, r0), T)
        _fill_pad(dpad_ref, lambda r0: rows(dy_ref, r0) * rows(bg_ref, r0), T)

        @pl.when(seg == 0)
        def _():
            def chunk(r0, carry):
                dq = _conv3_t(_ext(dpad_ref, r0), w)
                o_ref[pl.ds(r0, CH), :] = (dq * rows(cg_ref, r0)).astype(BF16)
                dcq = rows(dy_ref, r0) * rows(bg_ref, r0)
                return [a + b for a, b in zip(carry, _conv3_dw(_ext(qpad_ref, r0), dcq))]

            dws = _chunks(T, chunk, [jnp.zeros((1, tc), F32)] * 3)
            dw_ref[...] = jnp.concatenate(dws + [jnp.zeros((5, tc), F32)], axis=0)

        @pl.when(seg == 1)
        def _():
            def chunk(r0, carry):
                cq = _conv3(_ext(qpad_ref, r0), w)
                o_ref[pl.ds(r0, CH), :] = (rows(dy_ref, r0) * cq).astype(BF16)
                return carry

            _chunks(T, chunk)

        @pl.when(seg == 2)
        def _():
            def chunk(r0, carry):
                dq = _conv3_t(_ext(dpad_ref, r0), w)
                o_ref[pl.ds(r0, CH), :] = (dq * rows(xc_ref, r0)).astype(BF16)
                return carry

            _chunks(T, chunk)

    col = lambda off: pl.BlockSpec((T, tc), lambda j, s: (0, off * nb + j))
    return pl.pallas_call(
        body, name=name, grid=(nb, 3),
        in_specs=[col(3), col(4), col(5), pl.BlockSpec((T, tc), lambda j, s: (0, j)),
                  pl.BlockSpec((3, tc), lambda j, s: (0, j)), pl.BlockSpec(memory_space=pl.ANY)],
        out_specs=[pl.BlockSpec((T, tc), lambda j, s: (0, (3 + s) * nb + j)), pl.BlockSpec((8, tc), lambda j, s: (0, j))],
        out_shape=[jax.ShapeDtypeStruct(dPl.shape, BF16), jax.ShapeDtypeStruct((8, DP), F32)],
        scratch_shapes=[pltpu.VMEM((T + 2 * HALO, tc), F32), pltpu.VMEM((T + 2 * HALO, tc), F32)],
        input_output_aliases={5: 0},
        compiler_params=_params(("parallel", "arbitrary")),
    )(P, P, P, dy, cw, dPl)


def _merge_fwd(P, ys, ws, name):
    T = P.shape[0]
    DP, D = ws[0].shape
    tm = _pick(T, (512, 256, 128))
    tn = _pick(D, (1024, 512, 256))
    goff = (6 * DP) // tn

    def body(ya, yb, yc, wa, wb, wc, ga, gb, gc, oa_ref, ob_ref, oc_ref, m_ref):
        m = None
        for y, w, g, o_ref in ((ya, wa, ga, oa_ref), (yb, wb, gb, ob_ref), (yc, wc, gc, oc_ref)):
            o = jnp.dot(y[...], w[...], preferred_element_type=F32)
            o_ref[...] = o.astype(BF16)
            t = _sigmoid(g[...].astype(F32)) * o
            m = t if m is None else m + t
        m_ref[...] = m.astype(BF16)

    yspec = pl.BlockSpec((tm, DP), lambda i, j: (i, 0))
    wspec = pl.BlockSpec((DP, tn), lambda i, j: (0, j))
    gspec = lambda b: pl.BlockSpec((tm, tn), lambda i, j: (i, goff + b * (D // tn) + j))
    ospec = pl.BlockSpec((tm, tn), lambda i, j: (i, j))
    return pl.pallas_call(
        body, name=name, grid=(T // tm, D // tn),
        in_specs=[yspec] * 3 + [wspec] * 3 + [gspec(0), gspec(1), gspec(2)],
        out_specs=[ospec] * 4,
        out_shape=[jax.ShapeDtypeStruct((T, D), BF16)] * 4,
        compiler_params=_params(("parallel", "parallel")),
    )(*ys, *ws, P, P, P)


def _merge_bwd(P, dM, os_, name):
    T, D = dM.shape
    tm = _row_tile(T)

    def body(dm_ref, oa, ob, oc, g_ref, da_ref, db_ref, dc_ref, dg_ref):
        dm = dm_ref[...].astype(F32)
        for b, (o_ref, d_ref) in enumerate(((oa, da_ref), (ob, db_ref), (oc, dc_ref))):
            s = _sigmoid(g_ref[:, b * D:(b + 1) * D].astype(F32))
            d_ref[...] = (dm * s).astype(BF16)
            dg_ref[:, b * D:(b + 1) * D] = (dm * o_ref[...].astype(F32) * s * (1.0 - s)).astype(BF16)

    row = pl.BlockSpec((tm, D), lambda i: (i, 0))
    return pl.pallas_call(
        body, name=name, grid=(T // tm,),
        in_specs=[row] * 4 + [pl.BlockSpec((tm, 3 * D), lambda i: (i, 1))],
        out_specs=[row] * 3 + [pl.BlockSpec((tm, 3 * D), lambda i: (i, 0))],
        out_shape=[jax.ShapeDtypeStruct((T, D), BF16)] * 3 + [jax.ShapeDtypeStruct((T, 3 * D), BF16)],
        compiler_params=_params(("parallel",)),
    )(dM, *os_, P)


def _glu_fwd(U, fw, name):
    T, F2 = U.shape
    F = F2 // 2
    tc = 256
    nb = F // tc

    def body(ug_ref, uv_ref, wg_ref, wv_ref, o_ref, gpad_ref, vpad_ref):
        _fill_pad(gpad_ref, lambda r0: ug_ref[pl.ds(r0, CH), :].astype(F32), T)
        _fill_pad(vpad_ref, lambda r0: uv_ref[pl.ds(r0, CH), :].astype(F32), T)
        wg, wv = wg_ref[...], wv_ref[...]

        def chunk(r0, carry):
            gate = _conv3(_ext(gpad_ref, r0), wg)
            val = _conv3(_ext(vpad_ref, r0), wv)
            o_ref[pl.ds(r0, CH), :] = (gate * _sigmoid(gate) * val).astype(BF16)
            return carry

        _chunks(T, chunk)

    return pl.pallas_call(
        body, name=name, grid=(nb,),
        in_specs=[pl.BlockSpec((T, tc), lambda j: (0, j)), pl.BlockSpec((T, tc), lambda j: (0, nb + j)),
                  pl.BlockSpec((3, tc), lambda j: (0, j)), pl.BlockSpec((3, tc), lambda j: (0, nb + j))],
        out_specs=pl.BlockSpec((T, tc), lambda j: (0, j)),
        out_shape=jax.ShapeDtypeStruct((T, F), BF16),
        scratch_shapes=[pltpu.VMEM((T + 2 * HALO, tc), F32)] * 2,
        compiler_params=_params(("parallel",)),
    )(U, U, fw, fw)


def _glu_bwd(U, dact, fw, name):
    T, F2 = U.shape
    F = F2 // 2
    tc = 128
    nb = F // tc

    def body(ug_ref, uv_ref, da_ref, wg_ref, wv_ref, dug_ref, duv_ref, dwg_ref, dwv_ref, gpad, vpad, dgpad, dvpad):
        _fill_pad(gpad, lambda r0: ug_ref[pl.ds(r0, CH), :].astype(F32), T)
        _fill_pad(vpad, lambda r0: uv_ref[pl.ds(r0, CH), :].astype(F32), T)
        wg, wv = wg_ref[...], wv_ref[...]

        def chunk1(r0, carry):
            gate = _conv3(_ext(gpad, r0), wg)
            val = _conv3(_ext(vpad, r0), wv)
            s = _sigmoid(gate)
            da = da_ref[pl.ds(r0, CH), :].astype(F32)
            dgpad[pl.ds(HALO + r0, CH), :] = da * val * (s * (1.0 + gate * (1.0 - s)))
            dvpad[pl.ds(HALO + r0, CH), :] = da * (gate * s)
            return carry

        z = jnp.zeros((HALO, tc), F32)
        for p in (dgpad, dvpad):
            p[pl.ds(0, HALO), :] = z
            p[pl.ds(HALO + T, HALO), :] = z
        _chunks(T, chunk1)

        def chunk2(r0, carry):
            new = []
            for pad, dpad, wk, o_ref in ((gpad, dgpad, wg, dug_ref), (vpad, dvpad, wv, duv_ref)):
                dext = _ext(dpad, r0)
                taps = [_tap(dext, 1), _tap(dext, 0), _tap(dext, -1)]
                o_ref[pl.ds(r0, CH), :] = (taps[0] * wk[0:1] + taps[1] * wk[1:2] + taps[2] * wk[2:3]).astype(BF16)
                xs = pad[pl.ds(HALO + r0, CH), :]
                new += [jnp.sum(xs * t, axis=0, keepdims=True) for t in taps]
            return [a + b for a, b in zip(carry, new)]

        dws = _chunks(T, chunk2, [jnp.zeros((1, tc), F32)] * 6)
        dwg_ref[...] = jnp.concatenate(dws[:3] + [jnp.zeros((5, tc), F32)], axis=0)
        dwv_ref[...] = jnp.concatenate(dws[3:] + [jnp.zeros((5, tc), F32)], axis=0)

    lo = pl.BlockSpec((T, tc), lambda j: (0, j))
    hi = pl.BlockSpec((T, tc), lambda j: (0, nb + j))
    wlo = pl.BlockSpec((3, tc), lambda j: (0, j))
    whi = pl.BlockSpec((3, tc), lambda j: (0, nb + j))
    dwspec = pl.BlockSpec((8, tc), lambda j: (0, j))
    return pl.pallas_call(
        body, name=name, grid=(nb,),
        in_specs=[lo, hi, lo, wlo, whi],
        out_specs=[lo, lo, dwspec, dwspec],
        out_shape=[jax.ShapeDtypeStruct((T, F), BF16)] * 2 + [jax.ShapeDtypeStruct((8, F), F32)] * 2,
        scratch_shapes=[pltpu.VMEM((T + 2 * HALO, tc), F32)] * 4,
        compiler_params=_params(("parallel",)),
    )(U, U, dact, fw, fw)


_ANY = pl.BlockSpec(memory_space=pl.ANY)
_MESH = pl.DeviceIdType.MESH


def _mesh_pos():
    return lax.axis_index("x"), lax.axis_index("y"), lax.axis_index("c")


def _other_chips(x, y):
    return [(1 - x, y), (x, 1 - y), (1 - x, 1 - y)]


_HBM = pl.BlockSpec(memory_space=pltpu.HBM)
_SEM = pl.BlockSpec(memory_space=pltpu.SEMAPHORE)
_EFFECT = pltpu.SideEffectType.DATAFLOW_SIDE_EFFECTING


def _in_hbm(a):
    return pltpu.with_memory_space_constraint(a, pltpu.HBM)


def _split_start(groups, name):
    flat, where = [], []
    for bufs, _, _ in groups:
        idx = []
        for b in bufs:
            hit = [i for i, f in enumerate(flat) if f is b]
            if not hit:
                flat.append(b)
            idx.append(hit[0] if hit else len(flat) - 1)
        where.append(idx)
    nb = len(flat)
    ng = len(groups)

    def body(*refs):
        ins = refs[:nb]
        sems = refs[nb:nb + 2 * ng]
        token = refs[-1]
        for gi, (_, make_copies, _) in enumerate(groups):
            for cp in make_copies([ins[i] for i in where[gi]], sems[2 * gi], sems[2 * gi + 1]):
                cp.start()
        token[...] = jnp.zeros(token.shape, token.dtype)

    sem_shapes = []
    for _, _, n in groups:
        sem_shapes += [pltpu.SemaphoreType.DMA((n,)), pltpu.SemaphoreType.DMA((n,))]
    outs = pl.pallas_call(
        body, name=name,
        in_specs=[_HBM] * nb,
        out_specs=[_SEM] * (2 * ng) + [_HBM] * nb + [pl.BlockSpec(memory_space=pltpu.VMEM)],
        out_shape=sem_shapes + [pltpu.HBM(b.shape, b.dtype) for b in flat] + [jax.ShapeDtypeStruct((8, LANE), F32)],
        input_output_aliases={i: 2 * ng + i for i in range(nb)},
        compiler_params=pltpu.CompilerParams(has_side_effects=_EFFECT),
    )(*[_in_hbm(b) for b in flat])
    res = [(outs[2 * gi], outs[2 * gi + 1], [outs[2 * ng + i] for i in where[gi]]) for gi in range(ng)]
    return res, outs[-1]


def _split_wait(started, make_copies, after, name):
    send, recv, bufs = started
    nb = len(bufs)
    after = list(after) if isinstance(after, (list, tuple)) else [after]

    def body(*refs):
        for cp in make_copies(refs[:nb], refs[nb], refs[nb + 1]):
            cp.wait_send()
            cp.wait_recv()

    outs = pl.pallas_call(
        body, name=name,
        in_specs=[_HBM] * nb + [_SEM, _SEM] + [_ANY] * len(after),
        out_specs=[_HBM] * nb,
        out_shape=[pltpu.HBM(b.shape, b.dtype) for b in bufs],
        input_output_aliases={i: i for i in range(nb)},
        compiler_params=pltpu.CompilerParams(has_side_effects=_EFFECT),
    )(*bufs, send, recv, *after)
    return list(outs)


def _slot(buf, s, cols):
    return buf.at[s] if cols is None else buf.at[s, :, pl.ds(cols[0], cols[1])]


def _copies_gather_chips(nbuf, cols=None):
    def make(bufs, send, recv):
        x, y, c = _mesh_pos()
        targets = [(x, y, 1 - c)] + [(px, py, c) for px, py in _other_chips(x, y)]
        cps = []
        for a in range(nbuf):
            mine = _slot(bufs[a], 4 * x + 2 * y + c, cols)
            for k, to in enumerate(targets):
                cps.append(pltpu.make_async_remote_copy(src_ref=mine, dst_ref=mine, send_sem=send.at[4 * a + k],
                                                        recv_sem=recv.at[4 * a + k], device_id=to, device_id_type=_MESH))
        return cps

    return make, 4 * nbuf


def _copies_gather_forward(nbuf, cols=None):
    def make(bufs, send, recv):
        x, y, c = _mesh_pos()
        cps = []
        for a in range(nbuf):
            for j, (px, py) in enumerate(_other_chips(x, y)):
                slot = _slot(bufs[a], 4 * px + 2 * py + c, cols)
                cps.append(pltpu.make_async_remote_copy(src_ref=slot, dst_ref=slot, send_sem=send.at[3 * a + j],
                                                        recv_sem=recv.at[3 * a + j], device_id=(x, y, 1 - c),
                                                        device_id_type=_MESH))
        return cps

    return make, 3 * nbuf


def _copies_rs_sibling(n):
    def make(bufs, send, recv):
        x, y, c = _mesh_pos()
        cps = []
        for a in range(n):
            for q in range(4):
                cps.append(pltpu.make_async_remote_copy(
                    src_ref=bufs[a].at[2 * q + (1 - c)], dst_ref=bufs[n + a].at[q], send_sem=send.at[4 * a + q],
                    recv_sem=recv.at[4 * a + q], device_id=(x, y, 1 - c), device_id_type=_MESH))
        return cps

    return make, 4 * n


def _copies_rs_chips(n):
    def make(bufs, send, recv):
        x, y, c = _mesh_pos()
        cps = []
        for a in range(n):
            for j, (px, py) in enumerate(_other_chips(x, y)):
                cps.append(pltpu.make_async_remote_copy(
                    src_ref=bufs[a].at[2 * px + py], dst_ref=bufs[n + a].at[j], send_sem=send.at[3 * a + j],
                    recv_sem=recv.at[3 * a + j], device_id=(px, py, c), device_id_type=_MESH))
        return cps

    return make, 3 * n


def _cast_place(stacked, layer, slot, dtype, name, deps=()):
    _, R, C = stacked.shape
    tr = _rows_tile(R, C)

    def body(slot_ref, x_ref, *rest):
        del slot_ref
        rest[-1][...] = x_ref[...].astype(dtype)

    return pl.pallas_call(
        body, name=name,
        grid_spec=pltpu.PrefetchScalarGridSpec(
            num_scalar_prefetch=1, grid=(R // tr,),
            in_specs=[pl.BlockSpec((None, tr, C), lambda i, s: (layer, i, 0))] + [_ANY] * len(deps),
            out_specs=pl.BlockSpec((None, tr, C), lambda i, s: (s[0], i, 0))),
        out_shape=jax.ShapeDtypeStruct((N_DEV, R, C), dtype),
        compiler_params=_params(("parallel",)),
    )(slot, stacked, *deps)


def _rows_tile(R, C, budget=2 * MIB):
    if R * C * 4 <= budget:
        return R
    for t in (2048, 1024, 704, 512, 352, 256, 128, 64, 32, 16, 8):
        if R % t == 0 and t * C * 4 <= budget:
            return t
    return R


def _pair_sum(grad, recv, core, name):
    _, R, C = grad.shape
    tr = _rows_tile(R, C, 8 * MIB)

    def body(c_ref, g_ref, r_ref, o_ref):
        del c_ref
        o_ref[...] = (g_ref[...].astype(F32) + r_ref[...].astype(F32)).astype(BF16)

    return pl.pallas_call(
        body, name=name,
        grid_spec=pltpu.PrefetchScalarGridSpec(
            num_scalar_prefetch=1, grid=(4, R // tr),
            in_specs=[pl.BlockSpec((None, tr, C), lambda q, i, c: (2 * q + c[0], i, 0)),
                      pl.BlockSpec((None, tr, C), lambda q, i, c: (q, i, 0))],
            out_specs=pl.BlockSpec((None, tr, C), lambda q, i, c: (q, i, 0))),
        out_shape=jax.ShapeDtypeStruct((4, R, C), BF16),
        compiler_params=_params(("parallel", "parallel")),
    )(core, grad, recv)


ADAM_LR = 0.001
ADAM_B1 = 0.9
ADAM_B2 = 0.999
ADAM_EPS = 1e-08
ADAM_WD = 0.01
ADAM_STEP = 10


def _adamw(w, g, m, v):
    m = ADAM_B1 * m + (1.0 - ADAM_B1) * g
    v = ADAM_B2 * v + (1.0 - ADAM_B2) * (g * g)
    m_hat = m / (1.0 - ADAM_B1 ** ADAM_STEP)
    v_hat = v / (1.0 - ADAM_B2 ** ADAM_STEP)
    delta = -ADAM_LR * (m_hat / (jnp.sqrt(v_hat) + ADAM_EPS) + ADAM_WD * w)
    return delta, m, v


def _adamw_big(pair, recv, w, m, v, chip, layer, prev, name):
    L, R, C = w.shape
    tr = _rows_tile(R, C)
    has_prev = prev is not None

    def body(chip_ref, p_ref, r0_ref, r1_ref, r2_ref, w_ref, m_ref, v_ref, *rest):
        del chip_ref
        g_ref, d_ref, mo_ref, vo_ref = rest[-4:]
        g = ((p_ref[...].astype(F32) + r0_ref[...].astype(F32)) + r1_ref[...].astype(F32)) + r2_ref[...].astype(F32)
        delta, m2, v2 = _adamw(w_ref[...], g, m_ref[...], v_ref[...])
        g_ref[...] = g
        d_ref[...] = delta
        mo_ref[...] = m2
        vo_ref[...] = v2

    lay = pl.BlockSpec((None, tr, C), lambda i, c: (layer, i, 0))
    rspec = lambda j: pl.BlockSpec((None, tr, C), lambda i, c: (j, i, 0))
    in_specs = [pl.BlockSpec((None, tr, C), lambda i, c: (c[0], i, 0)), rspec(0), rspec(1), rspec(2), lay, lay, lay]
    args = [chip, pair, recv, recv, recv, w, m, v]
    aliases = {}
    if has_prev:
        in_specs += [_ANY] * 4
        args += list(prev)
        aliases = {8 + k: k for k in range(4)}
    return pl.pallas_call(
        body, name=name,
        grid_spec=pltpu.PrefetchScalarGridSpec(
            num_scalar_prefetch=1, grid=(R // tr,), in_specs=in_specs, out_specs=[lay] * 4),
        out_shape=[jax.ShapeDtypeStruct((L, R, C), F32)] * 4,
        input_output_aliases=aliases,
        compiler_params=_params(("parallel",)),
    )(*args)


def _sum_slots(gathered, name):
    _, R, C = gathered.shape
    tr = _rows_tile(R, C, 2 * MIB)

    def body(x_ref, o_ref):
        s = x_ref[0]
        for k in range(1, N_DEV):
            s = s + x_ref[k]
        o_ref[...] = s

    return pl.pallas_call(
        body, name=name, grid=(R // tr,),
        in_specs=[pl.BlockSpec((N_DEV, tr, C), lambda i: (0, i, 0))],
        out_specs=pl.BlockSpec((tr, C), lambda i: (i, 0)),
        out_shape=jax.ShapeDtypeStruct((R, C), F32),
        compiler_params=_params(("parallel",)),
    )(gathered)


def _adamw_small(w, g, m, v, name):
    R, C = w.shape
    tr = _rows_tile(R, C, 2 * MIB)

    def body(w_ref, g_ref, m_ref, v_ref, d_ref, mo_ref, vo_ref):
        d_ref[...], mo_ref[...], vo_ref[...] = _adamw(w_ref[...], g_ref[...], m_ref[...], v_ref[...])

    spec = pl.BlockSpec((tr, C), lambda i: (i, 0))
    return pl.pallas_call(
        body, name=name, grid=(R // tr,),
        in_specs=[spec] * 4, out_specs=[spec] * 3,
        out_shape=[jax.ShapeDtypeStruct((R, C), F32)] * 3,
        compiler_params=_params(("parallel",)),
    )(w, g, m, v)


def _pack(arrs):
    parts = []
    for a in arrs:
        flat = a.reshape(-1)
        pad = (-flat.shape[0]) % (8 * LANE)
        if pad:
            flat = jnp.pad(flat, (0, pad))
        parts.append(flat.reshape(-1, LANE))
    return jnp.concatenate(parts, axis=0)


def _unpack(packed, shapes):
    out, r = [], 0
    for s in shapes:
        n = 1
        for d in s:
            n *= d
        rows = -(-n // (8 * LANE)) * 8
        out.append(packed[r:r + rows].reshape(-1)[:n].reshape(s))
        r += rows
    return out


def _to_slices(a, ns):
    K = a.shape[0]
    return a.reshape(K, N_DEV, ns).transpose(1, 0, 2)


def _from_slices(a):
    _, K, ns = a.shape
    return a.transpose(1, 0, 2).reshape(K, N_DEV * ns)


def kernel(x, norm1_g, w_in, w_pool, pool_scale, sgu_norm_g, sgu_w, sgu_b, conv_w, w_pool_out, w_sgu_out, w_conv_out, w_o, norm2_g, w_up, ffn_conv_w, w_down, final_g, loss_target, m_norm1_g, m_w_in, m_w_pool, m_pool_scale, m_sgu_norm_g, m_sgu_w, m_sgu_b, m_conv_w, m_w_pool_out, m_w_sgu_out, m_w_conv_out, m_w_o, m_norm2_g, m_w_up, m_ffn_conv_w, m_w_down, m_final_g, v_norm1_g, v_w_in, v_w_pool, v_pool_scale, v_sgu_norm_g, v_sgu_w, v_sgu_b, v_conv_w, v_w_pool_out, v_w_sgu_out, v_w_conv_out, v_w_o, v_norm2_g, v_w_up, v_ffn_conv_w, v_w_down, v_final_g):
    L = norm1_g.shape[0]
    x0 = x[0]
    tgt = loss_target[0]
    G, PG = w_pool.shape[1], w_pool.shape[3]
    F = w_down.shape[1] * N_DEV
    px, py, pc = _mesh_pos()
    me = 4 * px + 2 * py + pc
    core = jnp.reshape(pc, (1,)).astype(jnp.int32)
    chip = jnp.reshape(2 * px + py, (1,)).astype(jnp.int32)

    slot = jnp.reshape(me, (1,)).astype(jnp.int32)
    ns_in = w_in.shape[2]
    nq = 3 if ns_in % (3 * 256) == 0 else 1
    chunk_cols = [(q * (ns_in // nq), ns_in // nq) for q in range(nq)]
    win0 = _cast_place(w_in, 0, slot, BF16, "place_w_in_0")
    first, tok0 = _split_start([([win0],) + _copies_gather_chips(1, chunk_cols[0])], "ag_start_first")
    win0 = first[0][2][0]
    h1_first = _rms_fwd(x0, norm1_g[0][None], "rms1_fwd_0")
    stage1, group_sizes, started1 = {}, {0: 1}, {}

    def start_groups(l, extra, deps, name):
        grp = [[("w_in", w_in, BF16)],
               [("w_pool_out", w_pool_out, BF16), ("w_sgu_out", w_sgu_out, BF16), ("w_conv_out", w_conv_out, BF16),
                ("w_pool", w_pool.reshape(L, -1, PG), BF16), ("conv_w", conv_w, F32), ("w_o", w_o, BF16)],
               [("w_up", w_up, BF16), ("ffn_conv_w", ffn_conv_w, F32)],
               [("w_down", w_down, BF16)]]
        ks = []
        for gi, members in enumerate(grp):
            if l == 0 and gi == 0:
                continue
            k = 4 * l + gi
            bufs = [_cast_place(a, l, slot, dt, f"place_{n}_{l}", deps) for n, a, dt in members]
            stage1[k] = (bufs,) + _copies_gather_chips(len(members))
            group_sizes[k] = len(members)
            ks.append(k)
        st, _ = _split_start(extra + [stage1[k] for k in ks], name)
        for k, s in zip(ks, st[len(extra):]):
            started1[k] = s
        return st[:len(extra)]

    rest = start_groups(0, [([win0],) + _copies_gather_chips(1, cols) for cols in chunk_cols[1:]],
                        [tok0, h1_first], "ag_start_0")
    if nq > 1:
        win0 = rest[0][2][0]
    chunk_sems = [first[0][:2]] + [r[:2] for r in rest]
    started2 = {}

    def gather_forward(k, after):
        n = group_sizes[k]
        bufs = _split_wait(started1[k], stage1[k][1], after, f"ag_wait1_{k}")
        st, tok = _split_start([(bufs,) + _copies_gather_forward(n)], f"ag_fwd_{k}")
        started2[k] = st[0]
        return tok

    def gather_done(k, after):
        return _split_wait(started2.pop(k), _copies_gather_forward(group_sizes[k])[0], after, f"ag_wait2_{k}")

    W, saved = [], []
    xc = x0
    for l in range(L):
        w = dict(
            sguw=sgu_w[l].astype(BF16),
            bexp=jnp.broadcast_to(sgu_b[l].T[:, :, None],
                                  (sgu_b.shape[2], sgu_b.shape[1], sgu_norm_g.shape[1] // sgu_b.shape[2])),
            g1=norm1_g[l][None], g2=norm2_g[l][None], gs=sgu_norm_g[l][None], scale=pool_scale[l][None])
        ka, kb, kc, kd = 4 * l, 4 * l + 1, 4 * l + 2, 4 * l + 3
        h1 = h1_first if l == 0 else _rms_fwd(xc, w["g1"], f"rms1_fwd_{l}")
        if l == 0:
            P, after = None, h1
            for q, cols in enumerate(chunk_cols):
                (win0,) = _split_wait(chunk_sems[q] + ([win0],), _copies_gather_chips(1, cols)[0], after,
                                      f"ag_wait1_0_{q}")
                st, tok = _split_start([([win0],) + _copies_gather_forward(1, cols)], f"ag_fwd_0_{q}")
                (win0,) = _split_wait(st[0], _copies_gather_forward(1, cols)[0], after, f"ag_wait2_0_{q}")
                P = _matmul(h1, win0, mode="nn", out_dtype=BF16, name=f"proj_in_0_{q}", b_slices=(0, N_DEV),
                            col_chunk=(nq, q) if nq > 1 else None, prev=P, deps=[tok])
                after = P
            w["win3"] = win0
            for l2 in range(1, L):
                start_groups(l2, [], [P], f"ag_start_{l2}")
        else:
            (w["win3"],) = gather_done(ka, h1)
            tok = gather_forward(kb, h1)
            P = _matmul(h1, w["win3"], mode="nn", out_dtype=BF16, name=f"proj_in_{l}", b_slices=(0, N_DEV), deps=[tok])
        if l == 0:
            gather_forward(kb, P)
        wa, wb, wc, wp, cw, wo = gather_done(kb, P)
        w.update(wa=_from_slices(wa), wb=_from_slices(wb), wc=_from_slices(wc), cw=_from_slices(cw),
                 wpool=wp.reshape(N_DEV, G, -1, PG).transpose(1, 0, 2, 3).reshape(G, PG, PG),
                 wo=wo.reshape(-1, wo.shape[2]))
        ya = _pool_fwd(P, w["wpool"], w["scale"], f"pool_fwd_{l}")
        yb = _sgu_fwd(P, w["sguw"], w["bexp"], w["gs"], f"sgu_fwd_{l}")
        yc = _conv_fwd(P, w["cw"], f"conv_fwd_{l}")
        oa, ob, oc, M = _merge_fwd(P, (ya, yb, yc), (w["wa"], w["wb"], w["wc"]), f"merge_fwd_{l}")
        tok = gather_forward(kc, M)
        x1 = _matmul(M, w["wo"], mode="nn", out_dtype=F32, name=f"proj_o_{l}", res=xc, deps=[tok])
        h2 = _rms_fwd(x1, w["g2"], f"rms2_fwd_{l}")
        w["wup3"], fw = gather_done(kc, h2)
        w["fw"] = _from_slices(fw)
        tok = gather_forward(kd, h2)
        U = _matmul(h2, w["wup3"], mode="nn", out_dtype=BF16, name=f"proj_up_{l}", b_slices=(0, N_DEV), deps=[tok])
        act = _glu_fwd(U, w["fw"], f"glu_fwd_{l}")
        (wdown,) = gather_done(kd, act)
        w["wdown"] = wdown.reshape(-1, wdown.shape[2])
        deps = [gather_forward(4 * (l + 1), act)] if l + 1 < L else []
        x2 = _matmul(act, w["wdown"], mode="nn", out_dtype=F32, name=f"proj_down_{l}", res=x1, deps=deps,
                     tn=512, tk=F)
        if deps:
            tok = deps[0]
        W.append(w)
        saved.append(dict(x0=xc, h1=h1, P=P, ya=ya, yb=yb, yc=yc, oa=oa, ob=ob, oc=oc, M=M, x1=x1, h2=h2, U=U, act=act))
        xc = x2

    dx, dxb, d_final_g, loss_part = _loss_head(xc, tgt, final_g[None], "loss_head")
    loss = lax.psum(loss_part[0, 0], ("x", "y", "c"))

    wmv = dict(w_in=(w_in, m_w_in, v_w_in), w_up=(w_up, m_w_up, v_w_up), w_o=(w_o, m_w_o, v_w_o),
               w_down=(w_down, m_w_down, v_w_down), w_pool_out=(w_pool_out, m_w_pool_out, v_w_pool_out),
               w_sgu_out=(w_sgu_out, m_w_sgu_out, v_w_sgu_out), w_conv_out=(w_conv_out, m_w_conv_out, v_w_conv_out),
               w_pool=(w_pool, m_w_pool, v_w_pool))
    adam_out = {}

    def rs_begin(tag, layer, names, grads):
        n = len(grads)
        lands = [lax.empty((4,) + g.shape[1:], BF16) for g in grads]
        st, tok = _split_start([(list(grads) + lands,) + _copies_rs_sibling(n)], f"rs_sib_{tag}")
        return dict(tag=tag, layer=layer, names=names, n=n, st=st[0]), tok

    def rs_to_chips(state, after):
        n, tag = state["n"], state["tag"]
        bufs = _split_wait(state["st"], _copies_rs_sibling(n)[0], after, f"rs_sibw_{tag}")
        pairs = [_pair_sum(g, r, core, f"pair_{tag}_{i}") for i, (g, r) in enumerate(zip(bufs[:n], bufs[n:]))]
        lands = [lax.empty((3,) + p.shape[1:], BF16) for p in pairs]
        st, tok = _split_start([(pairs + lands,) + _copies_rs_chips(n)], f"rs_chips_{tag}")
        state["st"] = st[0]
        return tok

    def rs_finish(state, after):
        n, tag, layer = state["n"], state["tag"], state["layer"]
        bufs = _split_wait(state["st"], _copies_rs_chips(n)[0], after, f"rs_chipsw_{tag}")
        for name, pair, recv in zip(state["names"], bufs[:n], bufs[n:]):
            wk, mk, vk = (a.reshape(L, -1, a.shape[-1]) for a in wmv[name])
            adam_out[name] = _adamw_big(pair, recv, wk, mk, vk, chip, layer, adam_out.get(name), f"adamw_{name}_{layer}")

    small = [None] * L
    pending = []
    deps = []
    ns_out = w_pool_out.shape[2]
    for l in reversed(range(L)):
        w, s = W[l], saved[l]
        dact = _matmul(dxb, w["wdown"], mode="nt", out_dtype=BF16, name=f"d_act_{l}", tn=_pick(F, (512, 256)), deps=deps)
        g_down = _matmul(s["act"], dxb, mode="tn", out_dtype=BF16, name=f"g_down_{l}", tm=_pick(F, (512, 256)))
        dug, duv, dfwg, dfwv = _glu_bwd(s["U"], dact, w["fw"], f"glu_bwd_{l}")
        dh2 = _matmul(dug, w["wup3"], mode="nt", out_dtype=F32, name=f"d_h2a_{l}", b_slices=(0, 4))
        dh2 = _matmul(duv, w["wup3"], mode="nt", out_dtype=F32, name=f"d_h2b_{l}", b_slices=(4, 4), res=dh2)
        g_up = _matmul(s["h2"], dug, mode="tn", out_dtype=BF16, name=f"g_upa_{l}", out_slices=(0, 4), tm=512)
        g_up = _matmul(s["h2"], duv, mode="tn", out_dtype=BF16, name=f"g_upb_{l}", out_slices=(4, 4), prev=g_up, tm=512)
        ra, tok = rs_begin(f"a{l}", l, ["w_down", "w_up"], [g_down.reshape(N_DEV, -1, g_down.shape[1]), g_up])
        if pending:
            rs_finish(pending.pop(0), g_up)
        dx1, dx1b, dg2 = _rms_bwd(dh2, s["x1"], w["g2"], dx, f"rms2_bwd_{l}")
        dM = _matmul(dx1b, w["wo"], mode="nt", out_dtype=BF16, name=f"d_m_{l}", deps=[tok])
        g_o = _matmul(s["M"], dx1b, mode="tn", out_dtype=BF16, name=f"g_o_{l}")
        doa, dob, doc, dPg = _merge_bwd(s["P"], dM, (s["oa"], s["ob"], s["oc"]), f"merge_bwd_{l}")
        tok = rs_to_chips(ra, dPg)
        if pending:
            rs_finish(pending.pop(0), dPg)
        dya = _matmul(doa, w["wa"], mode="nt", out_dtype=BF16, name=f"d_ya_{l}", deps=[tok])
        dyb = _matmul(dob, w["wb"], mode="nt", out_dtype=BF16, name=f"d_yb_{l}")
        dyc = _matmul(doc, w["wc"], mode="nt", out_dtype=BF16, name=f"d_yc_{l}")
        g_a = _matmul(s["ya"], doa, mode="tn", out_dtype=BF16, name=f"g_a_{l}")
        g_b = _matmul(s["yb"], dob, mode="tn", out_dtype=BF16, name=f"g_b_{l}")
        g_c = _matmul(s["yc"], doc, mode="tn", out_dtype=BF16, name=f"g_c_{l}")
        da, g_pool, dscale = _pool_bwd(s["P"], dya, w["wpool"], w["scale"], f"pool_bwd_{l}")
        g_pool_s = g_pool.reshape(G, N_DEV, -1, PG).transpose(1, 0, 2, 3).reshape(N_DEV, -1, PG).astype(BF16)
        rb, tok = rs_begin(f"b{l}", l, ["w_o", "w_pool_out", "w_sgu_out", "w_conv_out", "w_pool"],
                           [g_o.reshape(N_DEV, -1, g_o.shape[1]), _to_slices(g_a, ns_out), _to_slices(g_b, ns_out),
                            _to_slices(g_c, ns_out), g_pool_s])
        if pending:
            rs_finish(pending.pop(0), da)
        dPl, g_sguw, db_exp, dgs = _sgu_bwd(s["P"], dyb, da, w["sguw"], w["bexp"], w["gs"], f"sgu_bwd_{l}")
        dPl, dcw = _conv_bwd(s["P"], dyc, w["cw"], dPl, f"conv_bwd_{l}")
        tok2 = rs_to_chips(rb, dPl)
        g_in = _matmul(s["h1"], dPl, mode="tn", out_dtype=BF16, name=f"g_ina_{l}", out_slices=(0, 4), deps=[tok, tok2])
        g_in = _matmul(s["h1"], dPg, mode="tn", out_dtype=BF16, name=f"g_inb_{l}", out_slices=(4, 4), prev=g_in)
        rc, tok = rs_begin(f"c{l}", l, ["w_in"], [g_in])
        dh1 = _matmul(dPl, w["win3"], mode="nt", out_dtype=F32, name=f"d_h1a_{l}", b_slices=(0, 4), deps=[tok])
        tok = rs_to_chips(rc, dh1)
        dh1 = _matmul(dPg, w["win3"], mode="nt", out_dtype=F32, name=f"d_h1b_{l}", b_slices=(4, 4), res=dh1, deps=[tok])
        dx, dxb, dg1 = _rms_bwd(dh1, s["x0"], w["g1"], dx1, f"rms1_bwd_{l}")
        deps = []
        pending = [ra, rb, rc]
        small[l] = dict(norm1_g=dg1[0], pool_scale=dscale[0], sgu_norm_g=dgs[0], sgu_w=g_sguw,
                        sgu_b=db_exp[:, :, 0].T, conv_w=dcw[:3], norm2_g=dg2[0],
                        ffn_conv_w=jnp.concatenate([dfwg[:3], dfwv[:3]], axis=1))
    grad_x = dx[None]

    snames = ["norm1_g", "pool_scale", "sgu_norm_g", "sgu_w", "sgu_b", "conv_w", "norm2_g", "ffn_conv_w"]
    sparts = [jnp.stack([small[l][n] for l in range(L)]) for n in snames] + [d_final_g[0]]
    snames = snames + ["final_g"]
    packed = _pack(sparts)
    sbuf = _cast_place(packed[None], 0, slot, F32, "place_small")
    st, tok = _split_start([([sbuf],) + _copies_gather_chips(1)], "ag_small_start")
    rs_finish(pending[0], tok)
    rs_finish(pending[1], tok)
    bufs = _split_wait(st[0], _copies_gather_chips(1)[0], [o[0] for o in adam_out.values()], "ag_small_wait1")
    st, tok = _split_start([(bufs,) + _copies_gather_forward(1)], "ag_small_fwd")
    (gathered_small,) = _split_wait(st[0], _copies_gather_forward(1)[0], tok, "ag_small_wait2")
    total = _sum_slots(gathered_small, "sum_small")
    sgrads = dict(zip(snames, _unpack(total, [p.shape for p in sparts])))
    for n, width in (("conv_w", conv_w.shape[2]), ("ffn_conv_w", ffn_conv_w.shape[2])):
        sgrads[n] = lax.dynamic_slice_in_dim(sgrads[n], me * width, width, axis=2)
    sw = dict(norm1_g=(norm1_g, m_norm1_g, v_norm1_g), pool_scale=(pool_scale, m_pool_scale, v_pool_scale),
              sgu_norm_g=(sgu_norm_g, m_sgu_norm_g, v_sgu_norm_g), sgu_w=(sgu_w, m_sgu_w, v_sgu_w),
              sgu_b=(sgu_b, m_sgu_b, v_sgu_b), conv_w=(conv_w, m_conv_w, v_conv_w),
              norm2_g=(norm2_g, m_norm2_g, v_norm2_g), ffn_conv_w=(ffn_conv_w, m_ffn_conv_w, v_ffn_conv_w),
              final_g=(final_g, m_final_g, v_final_g))
    shapes = [sw[n][0].shape for n in snames]
    upd = _adamw_small(_pack([sw[n][0] for n in snames]), _pack([sgrads[n] for n in snames]),
                       _pack([sw[n][1] for n in snames]), _pack([sw[n][2] for n in snames]), "adamw_small")
    sdelta, sm, sv = (dict(zip(snames, _unpack(u, shapes))) for u in upd)
    res = {n: [sgrads[n], sdelta[n], sm[n], sv[n]] for n in snames}

    rs_finish(pending[2], [upd[0]] + [o[0] for o in adam_out.values()])
    for n, outs in adam_out.items():
        res[n] = [o.reshape(wmv[n][0].shape) for o in outs]

    order = ["norm1_g", "w_in", "w_pool", "pool_scale", "sgu_norm_g", "sgu_w", "sgu_b", "conv_w", "w_pool_out",
             "w_sgu_out", "w_conv_out", "w_o", "norm2_g", "w_up", "ffn_conv_w", "w_down", "final_g"]
    return (loss, grad_x) + tuple(res[n][k] for k in range(4) for n in order)
```

```python
import functools

import jax
import jax.numpy as jnp
from jax import lax
from jax.experimental import pallas as pl
from jax.experimental.pallas import tpu as pltpu

BF16 = jnp.bfloat16
F32 = jnp.float32
EPS = 1e-6
MIB = 1024 * 1024
V7X_VMEM_BYTES = 64 * MIB
VMEM_LIMIT = 56 * MIB
LANE = 128
N_DEV = 8


def _params(sem, **kw):
    return pltpu.CompilerParams(dimension_semantics=sem, vmem_limit_bytes=VMEM_LIMIT, **kw)


def _pick(n, cands):
    for c in cands:
        if n % c == 0:
            return c
    return n


_DIMS = {"nn": (((1,), (0,)), ((), ())), "nt": (((1,), (1,)), ((), ())), "tn": (((0,), (0,)), ((), ()))}


def _matmul(a, b, *, mode, out_dtype, name, res=None, b_slices=None, out_slices=None, prev=None,
            deps=(), col_chunk=None, tm=None, tn=None, tk=None):
    soff, scnt = b_slices if b_slices is not None else (0, 1)
    ooff, ocnt = out_slices if out_slices is not None else (0, 1)
    if mode == "nn":
        M = a.shape[0]
        Kc = a.shape[1]
        ns = b.shape[-1]
        N = scnt * ns
        if col_chunk is not None:
            nq, cq = col_chunk
            tn = ns // nq
            N = scnt * tn
    elif mode == "nt":
        M = a.shape[0]
        N = b.shape[-2]
        ns = b.shape[-1]
        Kc = scnt * ns
    else:
        Kc, M = a.shape
        N = b.shape[1]
        ns = N // ocnt
    tm = tm or _pick(M, (1024, 512, 256, 128))
    if tn is None:
        lim = ns if (mode == "nn" and b_slices is not None) or out_slices is not None else N
        tn = _pick(lim, (1024, 768, 512, 256, 128)) if lim > 1408 else lim
    if tk is None:
        if mode == "nt" and b_slices is not None:
            tk = ns
        elif mode == "tn":
            tk = Kc if Kc <= 4096 else _pick(Kc, (4096, 2048))
        else:
            tk = Kc if Kc <= 2816 else _pick(Kc, (2816, 2048, 1536, 1408, 1024, 512))
    nk = Kc // tk
    assert M % tm == 0 and N % tn == 0 and Kc % tk == 0, (name, M, N, Kc, tm, tn, tk)
    if mode == "nn":
        a_spec = pl.BlockSpec((tm, tk), lambda i, j, k: (i, k))
        if b_slices is None:
            b_spec = pl.BlockSpec((tk, tn), lambda i, j, k: (k, j))
        else:
            spn = ns // tn
            assert ns % tn == 0
            b_spec = pl.BlockSpec((None, tk, tn), lambda i, j, k: (soff + j // spn, k, j % spn))
            if col_chunk is not None:
                b_spec = pl.BlockSpec((None, tk, tn), lambda i, j, k: (soff + j, k, cq))
    elif mode == "nt":
        a_spec = pl.BlockSpec((tm, tk), lambda i, j, k: (i, k))
        if b_slices is None:
            b_spec = pl.BlockSpec((tn, tk), lambda i, j, k: (j, k))
        else:
            spk = ns // tk
            assert ns % tk == 0
            b_spec = pl.BlockSpec((None, tn, tk), lambda i, j, k: (soff + k // spk, j, k % spk))
    else:
        a_spec = pl.BlockSpec((tk, tm), lambda i, j, k: (k, i))
        b_spec = pl.BlockSpec((tk, tn), lambda i, j, k: (k, j))
    if col_chunk is not None:
        o_spec = pl.BlockSpec((tm, tn), lambda i, j, k: (i, j * nq + cq))
        out_shape = jax.ShapeDtypeStruct((M, scnt * ns), out_dtype)
    elif out_slices is None:
        o_spec = pl.BlockSpec((tm, tn), lambda i, j, k: (i, j))
        out_shape = jax.ShapeDtypeStruct((M, N), out_dtype)
    else:
        spo = ns // tn
        assert mode == "tn" and ns % tn == 0
        o_spec = pl.BlockSpec((None, tm, tn), lambda i, j, k: (ooff + j // spo, i, j % spo))
        out_shape = jax.ShapeDtypeStruct((N_DEV, M, ns), out_dtype)
    dims = _DIMS[mode]
    has_res = res is not None
    has_prev = prev is not None

    def body(*refs):
        a_ref, b_ref = refs[0], refs[1]
        res_ref = refs[2] if has_res else None
        o_ref = refs[2 + has_res + has_prev + len(deps)]

        def finish(acc):
            if has_res:
                acc = acc + res_ref[...]
            o_ref[...] = acc.astype(o_ref.dtype)

        if nk == 1:
            finish(lax.dot_general(a_ref[...], b_ref[...], dims, preferred_element_type=F32))
        else:
            acc_ref = refs[3 + has_res + has_prev + len(deps)]
            k = pl.program_id(2)

            @pl.when(k == 0)
            def _():
                acc_ref[...] = jnp.zeros(acc_ref.shape, F32)

            acc_ref[...] += lax.dot_general(a_ref[...], b_ref[...], dims, preferred_element_type=F32)

            @pl.when(k == nk - 1)
            def _():
                finish(acc_ref[...])

    in_specs = [a_spec, b_spec]
    args = [a, b]
    if has_res:
        in_specs.append(pl.BlockSpec((tm, tn), lambda i, j, k: (i, j)))
        args.append(res)
    aliases = {}
    if has_prev:
        aliases = {len(args): 0}
        in_specs.append(pl.BlockSpec(memory_space=pl.ANY))
        args.append(prev)
    for d in deps:
        in_specs.append(pl.BlockSpec(memory_space=pl.ANY))
        args.append(d)
    return pl.pallas_call(
        body,
        name=name,
        grid=(M // tm, N // tn, nk),
        in_specs=in_specs,
        out_specs=o_spec,
        out_shape=out_shape,
        scratch_shapes=[pltpu.VMEM((tm, tn), F32)] if nk > 1 else [],
        input_output_aliases=aliases,
        compiler_params=_params(("parallel", "parallel", "arbitrary")),
    )(*args)


def _mm(a, b, *, mode, out_dtype, name, tm, tn, tk, res=None, prev=None, deps=(), cols=(0, 1), b_koff=0,
        out_off=0, out_n=None):
    if mode == "nn":
        M, Kc = a.shape
        nj = (b.shape[1] // tn - cols[0] + cols[1] - 1) // cols[1]
        out_w = b.shape[1]
    elif mode == "nt":
        M, Kc = a.shape
        nj = b.shape[0] // tn
        out_w = b.shape[0]
    else:
        Kc, M = a.shape
        nj = b.shape[1] // tn
        out_w = out_n or b.shape[1]
    nk = Kc // tk
    assert M % tm == 0 and Kc % tk == 0, (name, M, Kc, tm, tk)
    off, stride = cols
    if mode == "nn":
        a_spec = pl.BlockSpec((tm, tk), lambda i, j, k: (i, k))
        b_spec = pl.BlockSpec((tk, tn), lambda i, j, k: (k, off + j * stride))
        o_spec = pl.BlockSpec((tm, tn), lambda i, j, k: (i, off + j * stride))
    elif mode == "nt":
        a_spec = pl.BlockSpec((tm, tk), lambda i, j, k: (i, k))
        b_spec = pl.BlockSpec((tn, tk), lambda i, j, k: (j, b_koff + k))
        o_spec = pl.BlockSpec((tm, tn), lambda i, j, k: (i, j))
    else:
        a_spec = pl.BlockSpec((tk, tm), lambda i, j, k: (k, i))
        b_spec = pl.BlockSpec((tk, tn), lambda i, j, k: (k, j))
        o_spec = pl.BlockSpec((tm, tn), lambda i, j, k: (i, out_off + j))
    dims = _DIMS[mode]
    has_res = res is not None
    has_prev = prev is not None
    n_in = 2 + has_res + has_prev + len(deps)

    def body(*refs):
        a_ref, b_ref = refs[0], refs[1]
        o_ref = refs[n_in]

        def finish(acc):
            if has_res:
                acc = acc + refs[2][...]
            o_ref[...] = acc.astype(o_ref.dtype)

        if nk == 1:
            finish(lax.dot_general(a_ref[...], b_ref[...], dims, preferred_element_type=F32))
        else:
            acc_ref = refs[n_in + 1]
            k = pl.program_id(2)

            @pl.when(k == 0)
            def _():
                acc_ref[...] = jnp.zeros(acc_ref.shape, F32)

            acc_ref[...] += lax.dot_general(a_ref[...], b_ref[...], dims, preferred_element_type=F32)

            @pl.when(k == nk - 1)
            def _():
                finish(acc_ref[...])

    in_specs = [a_spec, b_spec]
    args = [a, b]
    if has_res:
        in_specs.append(pl.BlockSpec((tm, tn), lambda i, j, k: (i, j)))
        args.append(res)
    aliases = {}
    if has_prev:
        aliases = {len(args): 0}
        in_specs.append(pl.BlockSpec(memory_space=pl.ANY))
        args.append(prev)
    for d in deps:
        in_specs.append(pl.BlockSpec(memory_space=pl.ANY))
        args.append(d)
    return pl.pallas_call(
        body,
        name=name,
        grid=(M // tm, nj, nk),
        in_specs=in_specs,
        out_specs=o_spec,
        out_shape=jax.ShapeDtypeStruct((M, out_w), out_dtype),
        scratch_shapes=[pltpu.VMEM((tm, tn), F32)] if nk > 1 else [],
        input_output_aliases=aliases,
        compiler_params=_params(("parallel", "parallel", "arbitrary")),
    )(*args)


HALO = 16
CH = 256


def _fill_pad(pad_ref, chunk_fn, T):
    z = jnp.zeros((HALO, pad_ref.shape[1]), F32)
    pad_ref[pl.ds(0, HALO), :] = z
    pad_ref[pl.ds(HALO + T, HALO), :] = z

    def body(c, carry):
        r0 = pl.multiple_of(c * CH, CH)
        pad_ref[pl.ds(HALO + r0, CH), :] = chunk_fn(r0)
        return carry

    lax.fori_loop(0, T // CH, body, 0)


def _ext(pad_ref, r0):
    return pad_ref[pl.ds(r0, CH + 2 * HALO), :]


def _tap(ext, o):
    if o == 0:
        return ext[HALO:HALO + CH]
    return pltpu.roll(ext, (-o) % ext.shape[0], axis=0)[HALO:HALO + CH]


def _chunks(T, fn, init=0):
    def body(c, carry):
        return fn(pl.multiple_of(c * CH, CH), carry)

    return lax.fori_loop(0, T // CH, body, init)


def _conv3(ext, w):
    return _tap(ext, -1) * w[0:1] + _tap(ext, 0) * w[1:2] + _tap(ext, 1) * w[2:3]


def _conv3_t(ext, w):
    return _tap(ext, 1) * w[0:1] + _tap(ext, 0) * w[1:2] + _tap(ext, -1) * w[2:3]


def _conv3_dw(xext, d):
    return [jnp.sum(_tap(xext, k - 1) * d, axis=0, keepdims=True) for k in range(3)]


def _sigmoid(x):
    return 0.5 * (jnp.tanh(0.5 * x) + 1.0)


_GELU_C = 0.7978845608028654


def _gelu(x):
    return 0.5 * x * (1.0 + jnp.tanh(_GELU_C * (x + 0.044715 * (x * x * x))))


def _gelu_grad(x):
    t = jnp.tanh(_GELU_C * (x + 0.044715 * (x * x * x)))
    return 0.5 * (1.0 + t) + 0.5 * x * (1.0 - t * t) * (_GELU_C * (1.0 + 3.0 * 0.044715 * (x * x)))


def _row_tile(T):
    return _pick(T, (256, 128))


def _rms_fwd(x, g, name):
    T, D = x.shape
    tm = _row_tile(T)

    def body(x_ref, g_ref, h_ref):
        xv = x_ref[...]
        r = lax.rsqrt(jnp.mean(xv * xv, axis=-1, keepdims=True) + EPS)
        h_ref[...] = (xv * r * g_ref[...]).astype(BF16)

    return pl.pallas_call(
        body, name=name, grid=(T // tm,),
        in_specs=[pl.BlockSpec((tm, D), lambda i: (i, 0)), pl.BlockSpec((1, D), lambda i: (0, 0))],
        out_specs=pl.BlockSpec((tm, D), lambda i: (i, 0)),
        out_shape=jax.ShapeDtypeStruct((T, D), BF16),
        compiler_params=_params(("parallel",)),
    )(x, g)


def _rms_bwd(dh, x, g, dres, name, deps=()):
    T, D = x.shape
    tm = _row_tile(T)

    def body(dh_ref, x_ref, g_ref, dres_ref, *rest):
        dx_ref, dxb_ref, dg_ref = rest[len(deps):]
        i = pl.program_id(0)
        xv = x_ref[...]
        r = lax.rsqrt(jnp.mean(xv * xv, axis=-1, keepdims=True) + EPS)
        n = xv * r
        dh_v = dh_ref[...].astype(F32)
        dn = dh_v * g_ref[...]
        dx = dres_ref[...] + r * (dn - n * jnp.mean(dn * n, axis=-1, keepdims=True))
        dx_ref[...] = dx
        dxb_ref[...] = dx.astype(BF16)
        dg = jnp.sum(dh_v * n, axis=0, keepdims=True)

        @pl.when(i == 0)
        def _():
            dg_ref[...] = dg

        @pl.when(i > 0)
        def _():
            dg_ref[...] += dg

    row = pl.BlockSpec((tm, D), lambda i: (i, 0))
    vec = pl.BlockSpec((1, D), lambda i: (0, 0))
    return pl.pallas_call(
        body, name=name, grid=(T // tm,),
        in_specs=[row, row, vec, row] + [pl.BlockSpec(memory_space=pl.ANY)] * len(deps),
        out_specs=[row, row, vec],
        out_shape=[jax.ShapeDtypeStruct((T, D), F32), jax.ShapeDtypeStruct((T, D), BF16),
                   jax.ShapeDtypeStruct((1, D), F32)],
        compiler_params=_params(("arbitrary",)),
    )(dh, x, g, dres, *deps)


def _loss_head(x, tgt, g, name):
    T, D = x.shape
    tm = _row_tile(T)

    def body(x_ref, t_ref, g_ref, dx_ref, dxb_ref, dg_ref, l_ref):
        i = pl.program_id(0)
        xv = x_ref[...]
        gv = g_ref[...]
        r = lax.rsqrt(jnp.mean(xv * xv, axis=-1, keepdims=True) + EPS)
        n = xv * r
        e = n * gv - t_ref[...]
        dy = e * (1.0 / D)
        dn = dy * gv
        dx = r * (dn - n * jnp.mean(dn * n, axis=-1, keepdims=True))
        dx_ref[...] = dx
        dxb_ref[...] = dx.astype(BF16)
        dg = jnp.sum(dy * n, axis=0, keepdims=True)
        per_tok = jnp.mean(e * e, axis=-1, keepdims=True)
        lv = jnp.broadcast_to(0.5 * jnp.sum(per_tok, axis=0, keepdims=True), (1, LANE))

        @pl.when(i == 0)
        def _():
            dg_ref[...] = dg
            l_ref[...] = lv

        @pl.when(i > 0)
        def _():
            dg_ref[...] += dg
            l_ref[...] += lv

    row = pl.BlockSpec((tm, D), lambda i: (i, 0))
    vec = pl.BlockSpec((1, D), lambda i: (0, 0))
    return pl.pallas_call(
        body, name=name, grid=(T // tm,),
        in_specs=[row, row, vec],
        out_specs=[row, row, vec, pl.BlockSpec((1, LANE), lambda i: (0, 0))],
        out_shape=[jax.ShapeDtypeStruct((T, D), F32), jax.ShapeDtypeStruct((T, D), BF16),
                   jax.ShapeDtypeStruct((1, D), F32), jax.ShapeDtypeStruct((1, LANE), F32)],
        compiler_params=_params(("arbitrary",)),
    )(x, tgt, g)


POOL_WINDOWS = (2, 4, 8, 16)


def _pool_offsets(w):
    return range(-(w // 2), w - w // 2)


def _pool_cnt(r0, w, T, shape):
    t = r0 + lax.broadcasted_iota(jnp.int32, shape, 0)
    lo = jnp.maximum(t - w // 2, 0)
    hi = jnp.minimum(t + (w - w // 2 - 1), T - 1)
    return (hi - lo + 1).astype(F32)


def _pooled(pad_ref, r0, w, T):
    ext = _ext(pad_ref, r0)
    s = None
    for o in _pool_offsets(w):
        tap = _tap(ext, o)
        s = tap if s is None else s + tap
    cur = ext[HALO:HALO + CH]
    return s / _pool_cnt(r0, w, T, cur.shape) - cur


def _pool_fwd(P, w_pool, scale, name):
    T = P.shape[0]
    G, PG, _ = w_pool.shape

    def body(a_ref, w_ref, s_ref, o_ref, pad_ref):
        g = pl.program_id(0)
        _fill_pad(pad_ref, lambda r0: a_ref[pl.ds(r0, CH), :].astype(F32), T)
        for gi, w in enumerate(POOL_WINDOWS):
            @pl.when(g == gi)
            def _(w=w):
                def chunk(r0, carry):
                    pooled = _pooled(pad_ref, r0, w, T).astype(BF16)
                    y = jnp.dot(pooled, w_ref[...], preferred_element_type=F32) * s_ref[...]
                    o_ref[pl.ds(r0, CH), :] = y.astype(BF16)
                    return carry

                _chunks(T, chunk)

    return pl.pallas_call(
        body, name=name, grid=(G,),
        in_specs=[pl.BlockSpec((T, PG), lambda g: (0, g)),
                  pl.BlockSpec((None, PG, PG), lambda g: (g, 0, 0)),
                  pl.BlockSpec((1, PG), lambda g: (0, g))],
        out_specs=pl.BlockSpec((T, PG), lambda g: (0, g)),
        out_shape=jax.ShapeDtypeStruct((T, G * PG), BF16),
        scratch_shapes=[pltpu.VMEM((T + 2 * HALO, PG), F32)],
        compiler_params=_params(("parallel",)),
    )(P, w_pool, scale)


def _pool_bwd(P, dy, w_pool, scale, name):
    T = P.shape[0]
    G, PG, _ = w_pool.shape

    def body(a_ref, dy_ref, w_ref, s_ref, da_ref, dw_ref, ds_ref, pad_ref, gpad_ref):
        g = pl.program_id(0)
        gpad_ref[pl.ds(0, HALO), :] = jnp.zeros((HALO, PG), F32)
        gpad_ref[pl.ds(HALO + T, HALO), :] = jnp.zeros((HALO, PG), F32)
        _fill_pad(pad_ref, lambda r0: a_ref[pl.ds(r0, CH), :].astype(F32), T)
        for gi, w in enumerate(POOL_WINDOWS):
            @pl.when(g == gi)
            def _(w=w):
                def chunk1(r0, carry):
                    dw, ds = carry
                    pooled = _pooled(pad_ref, r0, w, T).astype(BF16)
                    ypre = jnp.dot(pooled, w_ref[...], preferred_element_type=F32)
                    dyv = dy_ref[pl.ds(r0, CH), :].astype(F32)
                    ds = ds + jnp.sum(dyv * ypre, axis=0, keepdims=True)
                    dyp = (dyv * s_ref[...]).astype(BF16)
                    dw = dw + lax.dot_general(pooled, dyp, _DIMS["tn"], preferred_element_type=F32)
                    dpool = lax.dot_general(dyp, w_ref[...], _DIMS["nt"], preferred_element_type=F32)
                    gpad_ref[pl.ds(HALO + r0, CH), :] = dpool / _pool_cnt(r0, w, T, dpool.shape)
                    return dw, ds

                dw, ds = _chunks(T, chunk1, (jnp.zeros((PG, PG), F32), jnp.zeros((1, PG), F32)))
                dw_ref[...] = dw
                ds_ref[...] = ds

                def chunk2(r0, carry):
                    ext = _ext(gpad_ref, r0)
                    cur = ext[HALO:HALO + CH]
                    acc = None
                    for o in _pool_offsets(w):
                        tap = _tap(ext, -o)
                        acc = tap if acc is None else acc + tap
                    da_ref[pl.ds(r0, CH), :] = (acc - cur * _pool_cnt(r0, w, T, cur.shape)).astype(BF16)
                    return carry

                _chunks(T, chunk2)

    col = pl.BlockSpec((T, PG), lambda g: (0, g))
    return pl.pallas_call(
        body, name=name, grid=(G,),
        in_specs=[col, col, pl.BlockSpec((None, PG, PG), lambda g: (g, 0, 0)), pl.BlockSpec((1, PG), lambda g: (0, g))],
        out_specs=[col, pl.BlockSpec((None, PG, PG), lambda g: (g, 0, 0)), pl.BlockSpec((1, PG), lambda g: (0, g))],
        out_shape=[jax.ShapeDtypeStruct((T, G * PG), BF16), jax.ShapeDtypeStruct((G, PG, PG), F32),
                   jax.ShapeDtypeStruct((1, G * PG), F32)],
        scratch_shapes=[pltpu.VMEM((T + 2 * HALO, PG), F32), pltpu.VMEM((T + 2 * HALO, PG), F32)],
        compiler_params=_params(("parallel",)),
    )(P, dy, w_pool, scale)


SGU_CHUNK = 128


def _sgu_common(u_ref, v_ref, gs_ref):
    up = u_ref[...].astype(F32)
    vp = v_ref[...].astype(F32)
    gv = _gelu(vp)
    rv = lax.rsqrt(jnp.mean(gv * gv, axis=-1, keepdims=True) + EPS)
    nrm = gv * rv
    return up, vp, nrm, rv, (nrm * gs_ref[...]).astype(BF16)


def _sgu_fwd(P, sgu_w, b_exp, gs, name):
    T = P.shape[0]
    G, _, SG = b_exp.shape
    DP = G * SG
    tm = _row_tile(T)

    def body(u_ref, v_ref, w_ref, b_ref, gs_ref, o_ref):
        up, _, _, _, vn = _sgu_common(u_ref, v_ref, gs_ref)
        gu = _gelu(up)
        for n in range(tm // SGU_CHUNK):
            rows = slice(n * SGU_CHUNK, (n + 1) * SGU_CHUNK)
            for g in range(G):
                cols = slice(g * SG, (g + 1) * SG)
                z = jnp.dot(w_ref[g], vn[rows, cols], preferred_element_type=F32) + b_ref[g]
                o_ref[rows, cols] = (gu[rows, cols] * z).astype(BF16)

    return pl.pallas_call(
        body, name=name, grid=(T // tm,),
        in_specs=[pl.BlockSpec((tm, DP), lambda i: (i, 1)), pl.BlockSpec((tm, DP), lambda i: (i, 2)),
                  pl.BlockSpec(sgu_w.shape, lambda i: (0, 0, 0)), pl.BlockSpec(b_exp.shape, lambda i: (0, 0, 0)),
                  pl.BlockSpec((1, DP), lambda i: (0, 0))],
        out_specs=pl.BlockSpec((tm, DP), lambda i: (i, 0)),
        out_shape=jax.ShapeDtypeStruct((T, DP), BF16),
        compiler_params=_params(("parallel",)),
    )(P, P, sgu_w, b_exp, gs)


def _sgu_bwd(P, dy, da, sgu_w, b_exp, gs, name):
    T = P.shape[0]
    G, _, SG = b_exp.shape
    DP = G * SG
    tm = _row_tile(T)

    def body(u_ref, v_ref, dy_ref, da_ref, w_ref, b_ref, gs_ref, o_ref, dw_ref, db_ref, dgs_ref, dzs_ref):
        i = pl.program_id(0)
        up, vp, nrm, rv, vn = _sgu_common(u_ref, v_ref, gs_ref)
        gu = _gelu(up)
        dyv = dy_ref[...].astype(F32)
        o_ref[:, 0:DP] = da_ref[...]

        @pl.when(i == 0)
        def _():
            dw_ref[...] = jnp.zeros(dw_ref.shape, F32)
            dzs_ref[...] = jnp.zeros(dzs_ref.shape, F32)
            dgs_ref[...] = jnp.zeros(dgs_ref.shape, F32)

        dgs = jnp.zeros((1, DP), F32)
        for n in range(tm // SGU_CHUNK):
            rows = slice(n * SGU_CHUNK, (n + 1) * SGU_CHUNK)
            dvn_parts = []
            for g in range(G):
                cols = slice(g * SG, (g + 1) * SG)
                vng = vn[rows, cols]
                z = jnp.dot(w_ref[g], vng, preferred_element_type=F32) + b_ref[g]
                dyg = dyv[rows, cols]
                du = dyg * z
                o_ref[rows, DP + g * SG:DP + (g + 1) * SG] = (du * _gelu_grad(up[rows, cols])).astype(BF16)
                dz = dyg * gu[rows, cols]
                dzb = dz.astype(BF16)
                dzs_ref[g] += dz
                dw_ref[g] += lax.dot_general(dzb, vng, _DIMS["nt"], preferred_element_type=F32)
                dvn_parts.append(lax.dot_general(w_ref[g], dzb, _DIMS["tn"], preferred_element_type=F32))
            dvn = jnp.concatenate(dvn_parts, axis=1)
            nr = nrm[rows]
            dgs = dgs + jnp.sum(dvn * nr, axis=0, keepdims=True)
            dn = dvn * gs_ref[...]
            dgv = rv[rows] * (dn - nr * jnp.mean(dn * nr, axis=-1, keepdims=True))
            o_ref[rows, 2 * DP:3 * DP] = (dgv * _gelu_grad(vp[rows])).astype(BF16)
        dgs_ref[...] += dgs

        @pl.when(i == T // tm - 1)
        def _():
            for g in range(G):
                db_ref[g] = jnp.broadcast_to(jnp.sum(dzs_ref[g], axis=1, keepdims=True), (SGU_CHUNK, SG))

    full3 = lambda a: pl.BlockSpec(a.shape, lambda i: (0, 0, 0))
    return pl.pallas_call(
        body, name=name, grid=(T // tm,),
        in_specs=[pl.BlockSpec((tm, DP), lambda i: (i, 1)), pl.BlockSpec((tm, DP), lambda i: (i, 2)),
                  pl.BlockSpec((tm, DP), lambda i: (i, 0)), pl.BlockSpec((tm, DP), lambda i: (i, 0)),
                  full3(sgu_w), full3(b_exp), pl.BlockSpec((1, DP), lambda i: (0, 0))],
        out_specs=[pl.BlockSpec((tm, 3 * DP), lambda i: (i, 0)), full3(sgu_w), full3(b_exp),
                   pl.BlockSpec((1, DP), lambda i: (0, 0))],
        out_shape=[jax.ShapeDtypeStruct((T, 6 * DP), BF16), jax.ShapeDtypeStruct(sgu_w.shape, F32),
                   jax.ShapeDtypeStruct(b_exp.shape, F32), jax.ShapeDtypeStruct((1, DP), F32)],
        scratch_shapes=[pltpu.VMEM(b_exp.shape, F32)],
        compiler_params=_params(("arbitrary",)),
    )(P, P, dy, da, sgu_w, b_exp, gs)


def _conv_fwd(P, cw, name):
    T = P.shape[0]
    DP = cw.shape[1]
    tc = 256
    nb = DP // tc

    def body(xc_ref, bg_ref, cg_ref, w_ref, o_ref, pad_ref):
        _fill_pad(pad_ref, lambda r0: cg_ref[pl.ds(r0, CH), :].astype(F32) * xc_ref[pl.ds(r0, CH), :].astype(F32), T)
        w = w_ref[...]

        def chunk(r0, carry):
            cq = _conv3(_ext(pad_ref, r0), w)
            o_ref[pl.ds(r0, CH), :] = (bg_ref[pl.ds(r0, CH), :].astype(F32) * cq).astype(BF16)
            return carry

        _chunks(T, chunk)

    col = lambda off: pl.BlockSpec((T, tc), lambda j: (0, off * nb + j))
    return pl.pallas_call(
        body, name=name, grid=(nb,),
        in_specs=[col(3), col(4), col(5), pl.BlockSpec((3, tc), lambda j: (0, j))],
        out_specs=pl.BlockSpec((T, tc), lambda j: (0, j)),
        out_shape=jax.ShapeDtypeStruct((T, DP), BF16),
        scratch_shapes=[pltpu.VMEM((T + 2 * HALO, tc), F32)],
        compiler_params=_params(("parallel",)),
    )(P, P, P, cw)


def _conv_bwd(P, dy, cw, dPl, name):
    T = P.shape[0]
    DP = cw.shape[1]
    tc = 256
    nb = DP // tc

    def body(xc_ref, bg_ref, cg_ref, dy_ref, w_ref, prev_ref, o_ref, dw_ref, qpad_ref, dpad_ref):
        del prev_ref
        seg = pl.program_id(1)
        w = w_ref[...]
        rows = lambda ref, r0: ref[pl.ds(r0, CH), :].astype(F32)
        _fill_pad(qpad_ref, lambda r0: rows(cg_ref, r0) * rows(xc_ref, r0), T)
        _fill_pad(dpad_ref, lambda r0: rows(dy_ref, r0) * rows(bg_ref, r0), T)

        @pl.when(seg == 0)
        def _():
            def chunk(r0, carry):
                dq = _conv3_t(_ext(dpad_ref, r0), w)
                o_ref[pl.ds(r0, CH), :] = (dq * rows(cg_ref, r0)).astype(BF16)
                dcq = rows(dy_ref, r0) * rows(bg_ref, r0)
                return [a + b for a, b in zip(carry, _conv3_dw(_ext(qpad_ref, r0), dcq))]

            dws = _chunks(T, chunk, [jnp.zeros((1, tc), F32)] * 3)
            dw_ref[...] = jnp.concatenate(dws + [jnp.zeros((5, tc), F32)], axis=0)

        @pl.when(seg == 1)
        def _():
            def chunk(r0, carry):
                cq = _conv3(_ext(qpad_ref, r0), w)
                o_ref[pl.ds(r0, CH), :] = (rows(dy_ref, r0) * cq).astype(BF16)
                return carry

            _chunks(T, chunk)

        @pl.when(seg == 2)
        def _():
            def chunk(r0, carry):
                dq = _conv3_t(_ext(dpad_ref, r0), w)
                o_ref[pl.ds(r0, CH), :] = (dq * rows(xc_ref, r0)).astype(BF16)
                return carry

            _chunks(T, chunk)

    col = lambda off: pl.BlockSpec((T, tc), lambda j, s: (0, off * nb + j))
    return pl.pallas_call(
        body, name=name, grid=(nb, 3),
        in_specs=[col(3), col(4), col(5), pl.BlockSpec((T, tc), lambda j, s: (0, j)),
                  pl.BlockSpec((3, tc), lambda j, s: (0, j)), pl.BlockSpec(memory_space=pl.ANY)],
        out_specs=[pl.BlockSpec((T, tc), lambda j, s: (0, (3 + s) * nb + j)), pl.BlockSpec((8, tc), lambda j, s: (0, j))],
        out_shape=[jax.ShapeDtypeStruct(dPl.shape, BF16), jax.ShapeDtypeStruct((8, DP), F32)],
        scratch_shapes=[pltpu.VMEM((T + 2 * HALO, tc), F32), pltpu.VMEM((T + 2 * HALO, tc), F32)],
        input_output_aliases={5: 0},
        compiler_params=_params(("parallel", "arbitrary")),
    )(P, P, P, dy, cw, dPl)


def _merge_fwd(P, ys, ws, name):
    T = P.shape[0]
    DP, D = ws[0].shape
    tm = _pick(T, (512, 256, 128))
    tn = _pick(D, (1024, 512, 256))
    goff = (6 * DP) // tn

    def body(ya, yb, yc, wa, wb, wc, ga, gb, gc, oa_ref, ob_ref, oc_ref, m_ref):
        m = None
        for y, w, g, o_ref in ((ya, wa, ga, oa_ref), (yb, wb, gb, ob_ref), (yc, wc, gc, oc_ref)):
            o = jnp.dot(y[...], w[...], preferred_element_type=F32)
            o_ref[...] = o.astype(BF16)
            t = _sigmoid(g[...].astype(F32)) * o
            m = t if m is None else m + t
        m_ref[...] = m.astype(BF16)

    yspec = pl.BlockSpec((tm, DP), lambda i, j: (i, 0))
    wspec = pl.BlockSpec((DP, tn), lambda i, j: (0, j))
    gspec = lambda b: pl.BlockSpec((tm, tn), lambda i, j: (i, goff + b * (D // tn) + j))
    ospec = pl.BlockSpec((tm, tn), lambda i, j: (i, j))
    return pl.pallas_call(
        body, name=name, grid=(T // tm, D // tn),
        in_specs=[yspec] * 3 + [wspec] * 3 + [gspec(0), gspec(1), gspec(2)],
        out_specs=[ospec] * 4,
        out_shape=[jax.ShapeDtypeStruct((T, D), BF16)] * 4,
        compiler_params=_params(("parallel", "parallel")),
    )(*ys, *ws, P, P, P)


def _merge_bwd(P, dM, os_, name):
    T, D = dM.shape
    tm = _row_tile(T)

    def body(dm_ref, oa, ob, oc, g_ref, da_ref, db_ref, dc_ref, dg_ref):
        dm = dm_ref[...].astype(F32)
        for b, (o_ref, d_ref) in enumerate(((oa, da_ref), (ob, db_ref), (oc, dc_ref))):
            s = _sigmoid(g_ref[:, b * D:(b + 1) * D].astype(F32))
            d_ref[...] = (dm * s).astype(BF16)
            dg_ref[:, b * D:(b + 1) * D] = (dm * o_ref[...].astype(F32) * s * (1.0 - s)).astype(BF16)

    row = pl.BlockSpec((tm, D), lambda i: (i, 0))
    return pl.pallas_call(
        body, name=name, grid=(T // tm,),
        in_specs=[row] * 4 + [pl.BlockSpec((tm, 3 * D), lambda i: (i, 1))],
        out_specs=[row] * 3 + [pl.BlockSpec((tm, 3 * D), lambda i: (i, 0))],
        out_shape=[jax.ShapeDtypeStruct((T, D), BF16)] * 3 + [jax.ShapeDtypeStruct((T, 3 * D), BF16)],
        compiler_params=_params(("parallel",)),
    )(dM, *os_, P)


def _glu_fwd(U, fw, name):
    T, F2 = U.shape
    F = F2 // 2
    tc = 256
    nb = F // tc

    def body(ug_ref, uv_ref, wg_ref, wv_ref, o_ref, gpad_ref, vpad_ref):
        _fill_pad(gpad_ref, lambda r0: ug_ref[pl.ds(r0, CH), :].astype(F32), T)
        _fill_pad(vpad_ref, lambda r0: uv_ref[pl.ds(r0, CH), :].astype(F32), T)
        wg, wv = wg_ref[...], wv_ref[...]

        def chunk(r0, carry):
            gate = _conv3(_ext(gpad_ref, r0), wg)
            val = _conv3(_ext(vpad_ref, r0), wv)
            o_ref[pl.ds(r0, CH), :] = (gate * _sigmoid(gate) * val).astype(BF16)
            return carry

        _chunks(T, chunk)

    return pl.pallas_call(
        body, name=name, grid=(nb,),
        in_specs=[pl.BlockSpec((T, tc), lambda j: (0, j)), pl.BlockSpec((T, tc), lambda j: (0, nb + j)),
                  pl.BlockSpec((3, tc), lambda j: (0, j)), pl.BlockSpec((3, tc), lambda j: (0, nb + j))],
        out_specs=pl.BlockSpec((T, tc), lambda j: (0, j)),
        out_shape=jax.ShapeDtypeStruct((T, F), BF16),
        scratch_shapes=[pltpu.VMEM((T + 2 * HALO, tc), F32)] * 2,
        compiler_params=_params(("parallel",)),
    )(U, U, fw, fw)


def _glu_bwd(U, dact, fw, name):
    T, F2 = U.shape
    F = F2 // 2
    tc = 128
    nb = F // tc

    def body(ug_ref, uv_ref, da_ref, wg_ref, wv_ref, dug_ref, duv_ref, dwg_ref, dwv_ref, gpad, vpad, dgpad, dvpad):
        _fill_pad(gpad, lambda r0: ug_ref[pl.ds(r0, CH), :].astype(F32), T)
        _fill_pad(vpad, lambda r0: uv_ref[pl.ds(r0, CH), :].astype(F32), T)
        wg, wv = wg_ref[...], wv_ref[...]

        def chunk1(r0, carry):
            gate = _conv3(_ext(gpad, r0), wg)
            val = _conv3(_ext(vpad, r0), wv)
            s = _sigmoid(gate)
            da = da_ref[pl.ds(r0, CH), :].astype(F32)
            dgpad[pl.ds(HALO + r0, CH), :] = da * val * (s * (1.0 + gate * (1.0 - s)))
            dvpad[pl.ds(HALO + r0, CH), :] = da * (gate * s)
            return carry

        z = jnp.zeros((HALO, tc), F32)
        for p in (dgpad, dvpad):
            p[pl.ds(0, HALO), :] = z
            p[pl.ds(HALO + T, HALO), :] = z
        _chunks(T, chunk1)

        def chunk2(r0, carry):
            new = []
            for pad, dpad, wk, o_ref in ((gpad, dgpad, wg, dug_ref), (vpad, dvpad, wv, duv_ref)):
                dext = _ext(dpad, r0)
                taps = [_tap(dext, 1), _tap(dext, 0), _tap(dext, -1)]
                o_ref[pl.ds(r0, CH), :] = (taps[0] * wk[0:1] + taps[1] * wk[1:2] + taps[2] * wk[2:3]).astype(BF16)
                xs = pad[pl.ds(HALO + r0, CH), :]
                new += [jnp.sum(xs * t, axis=0, keepdims=True) for t in taps]
            return [a + b for a, b in zip(carry, new)]

        dws = _chunks(T, chunk2, [jnp.zeros((1, tc), F32)] * 6)
        dwg_ref[...] = jnp.concatenate(dws[:3] + [jnp.zeros((5, tc), F32)], axis=0)
        dwv_ref[...] = jnp.concatenate(dws[3:] + [jnp.zeros((5, tc), F32)], axis=0)

    lo = pl.BlockSpec((T, tc), lambda j: (0, j))
    hi = pl.BlockSpec((T, tc), lambda j: (0, nb + j))
    wlo = pl.BlockSpec((3, tc), lambda j: (0, j))
    whi = pl.BlockSpec((3, tc), lambda j: (0, nb + j))
    dwspec = pl.BlockSpec((8, tc), lambda j: (0, j))
    return pl.pallas_call(
        body, name=name, grid=(nb,),
        in_specs=[lo, hi, lo, wlo, whi],
        out_specs=[lo, lo, dwspec, dwspec],
        out_shape=[jax.ShapeDtypeStruct((T, F), BF16)] * 2 + [jax.ShapeDtypeStruct((8, F), F32)] * 2,
        scratch_shapes=[pltpu.VMEM((T + 2 * HALO, tc), F32)] * 4,
        compiler_params=_params(("parallel",)),
    )(U, U, dact, fw, fw)


_ANY = pl.BlockSpec(memory_space=pl.ANY)
_MESH = pl.DeviceIdType.MESH


def _mesh_pos():
    return lax.axis_index("x"), lax.axis_index("y"), lax.axis_index("c")


def _other_chips(x, y):
    return [(1 - x, y), (x, 1 - y), (1 - x, 1 - y)]


_HBM = pl.BlockSpec(memory_space=pltpu.HBM)
_SEM = pl.BlockSpec(memory_space=pltpu.SEMAPHORE)
_EFFECT = pltpu.SideEffectType.DATAFLOW_SIDE_EFFECTING


def _in_hbm(a):
    return pltpu.with_memory_space_constraint(a, pltpu.HBM)


def _split_start(groups, name):
    flat, where = [], []
    for bufs, _, _ in groups:
        idx = []
        for b in bufs:
            hit = [i for i, f in enumerate(flat) if f is b]
            if not hit:
                flat.append(b)
            idx.append(hit[0] if hit else len(flat) - 1)
        where.append(idx)
    nb = len(flat)
    ng = len(groups)

    def body(*refs):
        ins = refs[:nb]
        sems = refs[nb:nb + 2 * ng]
        token = refs[-1]
        for gi, (_, make_copies, _) in enumerate(groups):
            for cp in make_copies([ins[i] for i in where[gi]], sems[2 * gi], sems[2 * gi + 1]):
                cp.start()
        token[...] = jnp.zeros(token.shape, token.dtype)

    sem_shapes = []
    for _, _, n in groups:
        sem_shapes += [pltpu.SemaphoreType.DMA((n,)), pltpu.SemaphoreType.DMA((n,))]
    outs = pl.pallas_call(
        body, name=name,
        in_specs=[_HBM] * nb,
        out_specs=[_SEM] * (2 * ng) + [_HBM] * nb + [pl.BlockSpec(memory_space=pltpu.VMEM)],
        out_shape=sem_shapes + [pltpu.HBM(b.shape, b.dtype) for b in flat] + [jax.ShapeDtypeStruct((8, LANE), F32)],
        input_output_aliases={i: 2 * ng + i for i in range(nb)},
        compiler_params=pltpu.CompilerParams(has_side_effects=_EFFECT),
    )(*[_in_hbm(b) for b in flat])
    res = [(outs[2 * gi], outs[2 * gi + 1], [outs[2 * ng + i] for i in where[gi]]) for gi in range(ng)]
    return res, outs[-1]


def _split_wait(started, make_copies, after, name):
    send, recv, bufs = started
    nb = len(bufs)
    after = list(after) if isinstance(after, (list, tuple)) else [after]

    def body(*refs):
        for cp in make_copies(refs[:nb], refs[nb], refs[nb + 1]):
            cp.wait_send()
            cp.wait_recv()

    outs = pl.pallas_call(
        body, name=name,
        in_specs=[_HBM] * nb + [_SEM, _SEM] + [_ANY] * len(after),
        out_specs=[_HBM] * nb,
        out_shape=[pltpu.HBM(b.shape, b.dtype) for b in bufs],
        input_output_aliases={i: i for i in range(nb)},
        compiler_params=pltpu.CompilerParams(has_side_effects=_EFFECT),
    )(*bufs, send, recv, *after)
    return list(outs)


def _slot(buf, s, cols=None):
    if len(buf.shape) == 3:
        return buf.at[s] if cols is None else buf.at[s, :, pl.ds(cols[0], cols[1])]
    ns = buf.shape[1] // N_DEV
    first, width = (0, ns) if cols is None else cols
    return buf.at[:, pl.ds(pl.multiple_of(s * ns, LANE) + first, width)]


def _copies_gather_chips(nbuf, cols=None):
    def make(bufs, send, recv):
        x, y, c = _mesh_pos()
        targets = [(x, y, 1 - c)] + [(px, py, c) for px, py in _other_chips(x, y)]
        cps = []
        for a in range(nbuf):
            mine = _slot(bufs[a], 4 * x + 2 * y + c, cols)
            for k, to in enumerate(targets):
                cps.append(pltpu.make_async_remote_copy(src_ref=mine, dst_ref=mine, send_sem=send.at[4 * a + k],
                                                        recv_sem=recv.at[4 * a + k], device_id=to, device_id_type=_MESH))
        return cps

    return make, 4 * nbuf


def _copies_gather_forward(nbuf, cols=None):
    def make(bufs, send, recv):
        x, y, c = _mesh_pos()
        cps = []
        for a in range(nbuf):
            for j, (px, py) in enumerate(_other_chips(x, y)):
                slot = _slot(bufs[a], 4 * px + 2 * py + c, cols)
                cps.append(pltpu.make_async_remote_copy(src_ref=slot, dst_ref=slot, send_sem=send.at[3 * a + j],
                                                        recv_sem=recv.at[3 * a + j], device_id=(x, y, 1 - c),
                                                        device_id_type=_MESH))
        return cps

    return make, 3 * nbuf


def _copies_rs_sibling(n):
    def make(bufs, send, recv):
        x, y, c = _mesh_pos()
        cps = []
        for a in range(n):
            for q in range(4):
                cps.append(pltpu.make_async_remote_copy(
                    src_ref=_slot(bufs[a], 2 * q + (1 - c)), dst_ref=bufs[n + a].at[q], send_sem=send.at[4 * a + q],
                    recv_sem=recv.at[4 * a + q], device_id=(x, y, 1 - c), device_id_type=_MESH))
        return cps

    return make, 4 * n


def _copies_rs_chips(n):
    def make(bufs, send, recv):
        x, y, c = _mesh_pos()
        cps = []
        for a in range(n):
            for j, (px, py) in enumerate(_other_chips(x, y)):
                cps.append(pltpu.make_async_remote_copy(
                    src_ref=bufs[a].at[2 * px + py], dst_ref=bufs[n + a].at[j], send_sem=send.at[3 * a + j],
                    recv_sem=recv.at[3 * a + j], device_id=(px, py, c), device_id_type=_MESH))
        return cps

    return make, 3 * n


def _cast_place(stacked, layer, slot, dtype, name, deps=(), by_cols=False):
    _, R, C = stacked.shape
    tr = _rows_tile(R, C)

    def body(slot_ref, x_ref, *rest):
        del slot_ref
        rest[-1][...] = x_ref[...].astype(dtype)

    if by_cols:
        out_spec = pl.BlockSpec((tr, C), lambda i, s: (i, s[0]))
        out_shape = jax.ShapeDtypeStruct((R, N_DEV * C), dtype)
    else:
        out_spec = pl.BlockSpec((None, tr, C), lambda i, s: (s[0], i, 0))
        out_shape = jax.ShapeDtypeStruct((N_DEV, R, C), dtype)
    return pl.pallas_call(
        body, name=name,
        grid_spec=pltpu.PrefetchScalarGridSpec(
            num_scalar_prefetch=1, grid=(R // tr,),
            in_specs=[pl.BlockSpec((None, tr, C), lambda i, s: (layer, i, 0))] + [_ANY] * len(deps),
            out_specs=out_spec),
        out_shape=out_shape,
        compiler_params=_params(("parallel",)),
    )(slot, stacked, *deps)


def _rows_tile(R, C, budget=2 * MIB):
    if R * C * 4 <= budget:
        return R
    for t in (2048, 1024, 704, 512, 352, 256, 128, 64, 32, 16, 8):
        if R % t == 0 and t * C * 4 <= budget:
            return t
    return R


def _pair_sum(grad, recv, core, name):
    _, R, C = recv.shape
    tr = _rows_tile(R, C, 8 * MIB)

    def body(c_ref, g_ref, r_ref, o_ref):
        del c_ref
        o_ref[...] = (g_ref[...].astype(F32) + r_ref[...].astype(F32)).astype(BF16)

    if len(grad.shape) == 3:
        g_spec = pl.BlockSpec((None, tr, C), lambda q, i, c: (2 * q + c[0], i, 0))
    else:
        g_spec = pl.BlockSpec((tr, C), lambda q, i, c: (i, 2 * q + c[0]))
    return pl.pallas_call(
        body, name=name,
        grid_spec=pltpu.PrefetchScalarGridSpec(
            num_scalar_prefetch=1, grid=(4, R // tr),
            in_specs=[g_spec,
                      pl.BlockSpec((None, tr, C), lambda q, i, c: (q, i, 0))],
            out_specs=pl.BlockSpec((None, tr, C), lambda q, i, c: (q, i, 0))),
        out_shape=jax.ShapeDtypeStruct((4, R, C), BF16),
        compiler_params=_params(("parallel", "parallel")),
    )(core, grad, recv)


ADAM_LR = 0.001
ADAM_B1 = 0.9
ADAM_B2 = 0.999
ADAM_EPS = 1e-08
ADAM_WD = 0.01
ADAM_STEP = 10


def _adamw(w, g, m, v):
    m = ADAM_B1 * m + (1.0 - ADAM_B1) * g
    v = ADAM_B2 * v + (1.0 - ADAM_B2) * (g * g)
    m_hat = m / (1.0 - ADAM_B1 ** ADAM_STEP)
    v_hat = v / (1.0 - ADAM_B2 ** ADAM_STEP)
    delta = -ADAM_LR * (m_hat / (jnp.sqrt(v_hat) + ADAM_EPS) + ADAM_WD * w)
    return delta, m, v


def _adamw_big(pair, recv, w, m, v, chip, layer, prev, name):
    L, R, C = w.shape
    tr = _rows_tile(R, C)
    has_prev = prev is not None

    def body(chip_ref, p_ref, r0_ref, r1_ref, r2_ref, w_ref, m_ref, v_ref, *rest):
        del chip_ref
        g_ref, d_ref, mo_ref, vo_ref = rest[-4:]
        g = ((p_ref[...].astype(F32) + r0_ref[...].astype(F32)) + r1_ref[...].astype(F32)) + r2_ref[...].astype(F32)
        delta, m2, v2 = _adamw(w_ref[...], g, m_ref[...], v_ref[...])
        g_ref[...] = g
        d_ref[...] = delta
        mo_ref[...] = m2
        vo_ref[...] = v2

    lay = pl.BlockSpec((None, tr, C), lambda i, c: (layer, i, 0))
    rspec = lambda j: pl.BlockSpec((None, tr, C), lambda i, c: (j, i, 0))
    in_specs = [pl.BlockSpec((None, tr, C), lambda i, c: (c[0], i, 0)), rspec(0), rspec(1), rspec(2), lay, lay, lay]
    args = [chip, pair, recv, recv, recv, w, m, v]
    aliases = {}
    if has_prev:
        in_specs += [_ANY] * 4
        args += list(prev)
        aliases = {8 + k: k for k in range(4)}
    return pl.pallas_call(
        body, name=name,
        grid_spec=pltpu.PrefetchScalarGridSpec(
            num_scalar_prefetch=1, grid=(R // tr,), in_specs=in_specs, out_specs=[lay] * 4),
        out_shape=[jax.ShapeDtypeStruct((L, R, C), F32)] * 4,
        input_output_aliases=aliases,
        compiler_params=_params(("parallel",)),
    )(*args)


def _sum_slots(gathered, name):
    _, R, C = gathered.shape
    tr = _rows_tile(R, C, 2 * MIB)

    def body(x_ref, o_ref):
        s = x_ref[0]
        for k in range(1, N_DEV):
            s = s + x_ref[k]
        o_ref[...] = s

    return pl.pallas_call(
        body, name=name, grid=(R // tr,),
        in_specs=[pl.BlockSpec((N_DEV, tr, C), lambda i: (0, i, 0))],
        out_specs=pl.BlockSpec((tr, C), lambda i: (i, 0)),
        out_shape=jax.ShapeDtypeStruct((R, C), F32),
        compiler_params=_params(("parallel",)),
    )(gathered)


def _adamw_small(w, g, m, v, name):
    R, C = w.shape
    tr = _rows_tile(R, C, 2 * MIB)

    def body(w_ref, g_ref, m_ref, v_ref, d_ref, mo_ref, vo_ref):
        d_ref[...], mo_ref[...], vo_ref[...] = _adamw(w_ref[...], g_ref[...], m_ref[...], v_ref[...])

    spec = pl.BlockSpec((tr, C), lambda i: (i, 0))
    return pl.pallas_call(
        body, name=name, grid=(R // tr,),
        in_specs=[spec] * 4, out_specs=[spec] * 3,
        out_shape=[jax.ShapeDtypeStruct((R, C), F32)] * 3,
        compiler_params=_params(("parallel",)),
    )(w, g, m, v)


def _pack(arrs):
    parts = []
    for a in arrs:
        flat = a.reshape(-1)
        pad = (-flat.shape[0]) % (8 * LANE)
        if pad:
            flat = jnp.pad(flat, (0, pad))
        parts.append(flat.reshape(-1, LANE))
    return jnp.concatenate(parts, axis=0)


def _unpack(packed, shapes):
    out, r = [], 0
    for s in shapes:
        n = 1
        for d in s:
            n *= d
        rows = -(-n // (8 * LANE)) * 8
        out.append(packed[r:r + rows].reshape(-1)[:n].reshape(s))
        r += rows
    return out


def _to_slices(a, ns):
    K = a.shape[0]
    return a.reshape(K, N_DEV, ns).transpose(1, 0, 2)


def _from_slices(a):
    _, K, ns = a.shape
    return a.transpose(1, 0, 2).reshape(K, N_DEV * ns)


def kernel(x, norm1_g, w_in, w_pool, pool_scale, sgu_norm_g, sgu_w, sgu_b, conv_w, w_pool_out, w_sgu_out, w_conv_out, w_o, norm2_g, w_up, ffn_conv_w, w_down, final_g, loss_target, m_norm1_g, m_w_in, m_w_pool, m_pool_scale, m_sgu_norm_g, m_sgu_w, m_sgu_b, m_conv_w, m_w_pool_out, m_w_sgu_out, m_w_conv_out, m_w_o, m_norm2_g, m_w_up, m_ffn_conv_w, m_w_down, m_final_g, v_norm1_g, v_w_in, v_w_pool, v_pool_scale, v_sgu_norm_g, v_sgu_w, v_sgu_b, v_conv_w, v_w_pool_out, v_w_sgu_out, v_w_conv_out, v_w_o, v_norm2_g, v_w_up, v_ffn_conv_w, v_w_down, v_final_g):
    L = norm1_g.shape[0]
    x0 = x[0]
    tgt = loss_target[0]
    G, PG = w_pool.shape[1], w_pool.shape[3]
    F = w_down.shape[1] * N_DEV
    T, D = x0.shape
    tT = _pick(T, (1024, 512, 256, 128))
    tF = _pick(F, (512, 256))
    tF2 = _pick(2 * F, (1024, 512, 256))
    kF = _pick(F, (2816, 2048, 1024, 512))
    px, py, pc = _mesh_pos()
    me = 4 * px + 2 * py + pc
    core = jnp.reshape(pc, (1,)).astype(jnp.int32)
    chip = jnp.reshape(2 * px + py, (1,)).astype(jnp.int32)

    slot = jnp.reshape(me, (1,)).astype(jnp.int32)
    ns_in = w_in.shape[2]
    nq = 3 if ns_in % (3 * 256) == 0 else 1
    chunk_cols = [(q * (ns_in // nq), ns_in // nq) for q in range(nq)]
    win0 = _cast_place(w_in, 0, slot, BF16, "place_w_in_0", by_cols=True)
    first, tok0 = _split_start([([win0],) + _copies_gather_chips(1, chunk_cols[0])], "ag_start_first")
    win0 = first[0][2][0]
    h1_first = _rms_fwd(x0, norm1_g[0][None], "rms1_fwd_0")
    stage1, group_sizes, started1 = {}, {0: 1}, {}

    def start_groups(l, extra, deps, name):
        grp = [[("w_in", w_in, BF16, True)],
               [("w_pool_out", w_pool_out, BF16, True), ("w_sgu_out", w_sgu_out, BF16, True),
                ("w_conv_out", w_conv_out, BF16, True), ("w_pool", w_pool.reshape(L, -1, PG), BF16, False),
                ("conv_w", conv_w, F32, True), ("w_o", w_o, BF16, False)],
               [("w_up", w_up, BF16, True), ("ffn_conv_w", ffn_conv_w, F32, True)],
               [("w_down", w_down, BF16, False)]]
        ks = []
        for gi, members in enumerate(grp):
            if l == 0 and gi == 0:
                continue
            k = 4 * l + gi
            bufs = [_cast_place(a, l, slot, dt, f"place_{n}_{l}", deps, bc) for n, a, dt, bc in members]
            stage1[k] = (bufs,) + _copies_gather_chips(len(members))
            group_sizes[k] = len(members)
            ks.append(k)
        st, _ = _split_start(extra + [stage1[k] for k in ks], name)
        for k, s in zip(ks, st[len(extra):]):
            started1[k] = s
        return st[:len(extra)]

    rest = start_groups(0, [([win0],) + _copies_gather_chips(1, cols) for cols in chunk_cols[1:]],
                        [tok0, h1_first], "ag_start_0")
    if nq > 1:
        win0 = rest[0][2][0]
    chunk_sems = [first[0][:2]] + [r[:2] for r in rest]
    started2 = {}

    def gather_forward(k, after):
        n = group_sizes[k]
        bufs = _split_wait(started1[k], stage1[k][1], after, f"ag_wait1_{k}")
        st, tok = _split_start([(bufs,) + _copies_gather_forward(n)], f"ag_fwd_{k}")
        started2[k] = st[0]
        return tok

    def gather_done(k, after):
        return _split_wait(started2.pop(k), _copies_gather_forward(group_sizes[k])[0], after, f"ag_wait2_{k}")

    W, saved = [], []
    xc = x0
    for l in range(L):
        w = dict(
            sguw=sgu_w[l].astype(BF16),
            bexp=jnp.broadcast_to(sgu_b[l].T[:, :, None],
                                  (sgu_b.shape[2], sgu_b.shape[1], sgu_norm_g.shape[1] // sgu_b.shape[2])),
            g1=norm1_g[l][None], g2=norm2_g[l][None], gs=sgu_norm_g[l][None], scale=pool_scale[l][None])
        ka, kb, kc, kd = 4 * l, 4 * l + 1, 4 * l + 2, 4 * l + 3
        h1 = h1_first if l == 0 else _rms_fwd(xc, w["g1"], f"rms1_fwd_{l}")
        if l == 0:
            P, after = None, h1
            for q, cols in enumerate(chunk_cols):
                (win0,) = _split_wait(chunk_sems[q] + ([win0],), _copies_gather_chips(1, cols)[0], after,
                                      f"ag_wait1_0_{q}")
                st, tok = _split_start([([win0],) + _copies_gather_forward(1, cols)], f"ag_fwd_0_{q}")
                (win0,) = _split_wait(st[0], _copies_gather_forward(1, cols)[0], after, f"ag_wait2_0_{q}")
                P = _mm(h1, win0, mode="nn", out_dtype=BF16, name=f"proj_in_0_{q}", tm=tT, tn=cols[1], tk=D,
                        cols=(q, nq), prev=P, deps=[tok])
                after = P
            w["win"] = win0
        else:
            (w["win"],) = gather_done(ka, h1)
            tok = gather_forward(kb, h1)
            P = _mm(h1, w["win"], mode="nn", out_dtype=BF16, name=f"proj_in_{l}", tm=tT, tn=1024, tk=D, deps=[tok])
        if l == 0:
            gather_forward(kb, P)
        w["wa"], w["wb"], w["wc"], wp, w["cw"], wo = gather_done(kb, P)
        w.update(wpool=wp.reshape(N_DEV, G, -1, PG).transpose(1, 0, 2, 3).reshape(G, PG, PG),
                 wo=wo.reshape(-1, wo.shape[2]))
        ya = _pool_fwd(P, w["wpool"], w["scale"], f"pool_fwd_{l}")
        yb = _sgu_fwd(P, w["sguw"], w["bexp"], w["gs"], f"sgu_fwd_{l}")
        yc = _conv_fwd(P, w["cw"], f"conv_fwd_{l}")
        oa, ob, oc, M = _merge_fwd(P, (ya, yb, yc), (w["wa"], w["wb"], w["wc"]), f"merge_fwd_{l}")
        tok = gather_forward(kc, M)
        if l == 0:
            for l2 in range(1, L):
                start_groups(l2, [], [tok], f"ag_start_{l2}")
        x1 = _mm(M, w["wo"], mode="nn", out_dtype=F32, name=f"proj_o_{l}", tm=tT, tn=1024, tk=D, res=xc, deps=[tok])
        h2 = _rms_fwd(x1, w["g2"], f"rms2_fwd_{l}")
        w["wup"], w["fw"] = gather_done(kc, h2)
        tok = gather_forward(kd, h2)
        U = _mm(h2, w["wup"], mode="nn", out_dtype=BF16, name=f"proj_up_{l}", tm=tT, tn=tF2, tk=D, deps=[tok])
        act = _glu_fwd(U, w["fw"], f"glu_fwd_{l}")
        (wdown,) = gather_done(kd, act)
        w["wdown"] = wdown.reshape(-1, wdown.shape[2])
        deps = [gather_forward(4 * (l + 1), act)] if l + 1 < L else []
        x2 = _mm(act, w["wdown"], mode="nn", out_dtype=F32, name=f"proj_down_{l}", tm=tT, tn=512, tk=F, res=x1,
                 deps=deps)
        if deps:
            tok = deps[0]
        W.append(w)
        saved.append(dict(x0=xc, h1=h1, P=P, ya=ya, yb=yb, yc=yc, oa=oa, ob=ob, oc=oc, M=M, x1=x1, h2=h2, U=U, act=act))
        xc = x2

    dx, dxb, d_final_g, loss_part = _loss_head(xc, tgt, final_g[None], "loss_head")
    loss = lax.psum(loss_part[0, 0], ("x", "y", "c"))

    wmv = dict(w_in=(w_in, m_w_in, v_w_in), w_up=(w_up, m_w_up, v_w_up), w_o=(w_o, m_w_o, v_w_o),
               w_down=(w_down, m_w_down, v_w_down), w_pool_out=(w_pool_out, m_w_pool_out, v_w_pool_out),
               w_sgu_out=(w_sgu_out, m_w_sgu_out, v_w_sgu_out), w_conv_out=(w_conv_out, m_w_conv_out, v_w_conv_out),
               w_pool=(w_pool, m_w_pool, v_w_pool))
    adam_out = {}

    def rs_begin(tag, layer, names, grads):
        n = len(grads)
        slice_shape = lambda g: g.shape[1:] if len(g.shape) == 3 else (g.shape[0], g.shape[1] // N_DEV)
        lands = [lax.empty((4,) + slice_shape(g), BF16) for g in grads]
        st, tok = _split_start([(list(grads) + lands,) + _copies_rs_sibling(n)], f"rs_sib_{tag}")
        return dict(tag=tag, layer=layer, names=names, n=n, st=st[0]), tok

    def rs_to_chips(state, after):
        n, tag = state["n"], state["tag"]
        bufs = _split_wait(state["st"], _copies_rs_sibling(n)[0], after, f"rs_sibw_{tag}")
        pairs = [_pair_sum(g, r, core, f"pair_{tag}_{i}") for i, (g, r) in enumerate(zip(bufs[:n], bufs[n:]))]
        lands = [lax.empty((3,) + p.shape[1:], BF16) for p in pairs]
        st, tok = _split_start([(pairs + lands,) + _copies_rs_chips(n)], f"rs_chips_{tag}")
        state["st"] = st[0]
        return tok

    def rs_finish(state, after):
        n, tag, layer = state["n"], state["tag"], state["layer"]
        bufs = _split_wait(state["st"], _copies_rs_chips(n)[0], after, f"rs_chipsw_{tag}")
        for name, pair, recv in zip(state["names"], bufs[:n], bufs[n:]):
            wk, mk, vk = (a.reshape(L, -1, a.shape[-1]) for a in wmv[name])
            adam_out[name] = _adamw_big(pair, recv, wk, mk, vk, chip, layer, adam_out.get(name), f"adamw_{name}_{layer}")

    small = [None] * L
    pending = []
    deps = []
    ns_out = w_pool_out.shape[2]
    for l in reversed(range(L)):
        w, s = W[l], saved[l]
        dact = _mm(dxb, w["wdown"], mode="nt", out_dtype=BF16, name=f"d_act_{l}", tm=tT, tn=tF, tk=D, deps=deps)
        g_down = _mm(s["act"], dxb, mode="tn", out_dtype=BF16, name=f"g_down_{l}", tm=tF, tn=1024, tk=T)
        dug, duv, dfwg, dfwv = _glu_bwd(s["U"], dact, w["fw"], f"glu_bwd_{l}")
        dh2 = _mm(dug, w["wup"], mode="nt", out_dtype=F32, name=f"d_h2a_{l}", tm=tT, tn=1024, tk=kF)
        dh2 = _mm(duv, w["wup"], mode="nt", out_dtype=F32, name=f"d_h2b_{l}", tm=tT, tn=1024, tk=kF, b_koff=F // kF,
                  res=dh2)
        g_up = _mm(s["h2"], dug, mode="tn", out_dtype=BF16, name=f"g_upa_{l}", tm=1024, tn=tF, tk=T, out_n=2 * F)
        g_up = _mm(s["h2"], duv, mode="tn", out_dtype=BF16, name=f"g_upb_{l}", tm=1024, tn=tF, tk=T, out_n=2 * F,
                   out_off=F // tF, prev=g_up)
        ra, tok = rs_begin(f"a{l}", l, ["w_down", "w_up"], [g_down.reshape(N_DEV, -1, g_down.shape[1]), g_up])
        if pending:
            rs_finish(pending.pop(0), g_up)
        dx1, dx1b, dg2 = _rms_bwd(dh2, s["x1"], w["g2"], dx, f"rms2_bwd_{l}")
        dM = _mm(dx1b, w["wo"], mode="nt", out_dtype=BF16, name=f"d_m_{l}", tm=tT, tn=1024, tk=D, deps=[tok])
        g_o = _mm(s["M"], dx1b, mode="tn", out_dtype=BF16, name=f"g_o_{l}", tm=1024, tn=1024, tk=T)
        doa, dob, doc, dPg = _merge_bwd(s["P"], dM, (s["oa"], s["ob"], s["oc"]), f"merge_bwd_{l}")
        tok = rs_to_chips(ra, dPg)
        if pending:
            rs_finish(pending.pop(0), dPg)
        dya = _mm(doa, w["wa"], mode="nt", out_dtype=BF16, name=f"d_ya_{l}", tm=tT, tn=1024, tk=D, deps=[tok])
        dyb = _mm(dob, w["wb"], mode="nt", out_dtype=BF16, name=f"d_yb_{l}", tm=tT, tn=1024, tk=D)
        dyc = _mm(doc, w["wc"], mode="nt", out_dtype=BF16, name=f"d_yc_{l}", tm=tT, tn=1024, tk=D)
        g_a = _mm(s["ya"], doa, mode="tn", out_dtype=BF16, name=f"g_a_{l}", tm=1024, tn=1024, tk=T)
        g_b = _mm(s["yb"], dob, mode="tn", out_dtype=BF16, name=f"g_b_{l}", tm=1024, tn=1024, tk=T)
        g_c = _mm(s["yc"], doc, mode="tn", out_dtype=BF16, name=f"g_c_{l}", tm=1024, tn=1024, tk=T)
        da, g_pool, dscale = _pool_bwd(s["P"], dya, w["wpool"], w["scale"], f"pool_bwd_{l}")
        g_pool_s = g_pool.reshape(G, N_DEV, -1, PG).transpose(1, 0, 2, 3).reshape(N_DEV, -1, PG).astype(BF16)
        rb, tok = rs_begin(f"b{l}", l, ["w_o", "w_pool_out", "w_sgu_out", "w_conv_out", "w_pool"],
                           [g_o.reshape(N_DEV, -1, g_o.shape[1]), g_a, g_b, g_c, g_pool_s])
        if pending:
            rs_finish(pending.pop(0), da)
        dPl, g_sguw, db_exp, dgs = _sgu_bwd(s["P"], dyb, da, w["sguw"], w["bexp"], w["gs"], f"sgu_bwd_{l}")
        dPl, dcw = _conv_bwd(s["P"], dyc, w["cw"], dPl, f"conv_bwd_{l}")
        tok2 = rs_to_chips(rb, dPl)
        half = dPl.shape[1]
        g_in = _mm(s["h1"], dPl, mode="tn", out_dtype=BF16, name=f"g_ina_{l}", tm=1024, tn=1024, tk=T, out_n=2 * half,
                   deps=[tok, tok2])
        g_in = _mm(s["h1"], dPg, mode="tn", out_dtype=BF16, name=f"g_inb_{l}", tm=1024, tn=1024, tk=T, out_n=2 * half,
                   out_off=half // 1024, prev=g_in)
        rc, tok = rs_begin(f"c{l}", l, ["w_in"], [g_in])
        dh1 = _mm(dPl, w["win"], mode="nt", out_dtype=F32, name=f"d_h1a_{l}", tm=tT, tn=1024, tk=D, deps=[tok])
        tok = rs_to_chips(rc, dh1)
        dh1 = _mm(dPg, w["win"], mode="nt", out_dtype=F32, name=f"d_h1b_{l}", tm=tT, tn=1024, tk=D, b_koff=half // D,
                  res=dh1, deps=[tok])
        dx, dxb, dg1 = _rms_bwd(dh1, s["x0"], w["g1"], dx1, f"rms1_bwd_{l}")
        deps = []
        pending = [ra, rb, rc]
        small[l] = dict(norm1_g=dg1[0], pool_scale=dscale[0], sgu_norm_g=dgs[0], sgu_w=g_sguw,
                        sgu_b=db_exp[:, :, 0].T, conv_w=dcw[:3], norm2_g=dg2[0],
                        ffn_conv_w=jnp.concatenate([dfwg[:3], dfwv[:3]], axis=1))
    grad_x = dx[None]

    snames = ["norm1_g", "pool_scale", "sgu_norm_g", "sgu_w", "sgu_b", "conv_w", "norm2_g", "ffn_conv_w"]
    sparts = [jnp.stack([small[l][n] for l in range(L)]) for n in snames] + [d_final_g[0]]
    snames = snames + ["final_g"]
    packed = _pack(sparts)
    sbuf = _cast_place(packed[None], 0, slot, F32, "place_small")
    st, tok = _split_start([([sbuf],) + _copies_gather_chips(1)], "ag_small_start")
    rs_finish(pending[0], tok)
    rs_finish(pending[1], tok)
    bufs = _split_wait(st[0], _copies_gather_chips(1)[0], [o[0] for o in adam_out.values()], "ag_small_wait1")
    st, tok = _split_start([(bufs,) + _copies_gather_forward(1)], "ag_small_fwd")
    (gathered_small,) = _split_wait(st[0], _copies_gather_forward(1)[0], tok, "ag_small_wait2")
    total = _sum_slots(gathered_small, "sum_small")
    sgrads = dict(zip(snames, _unpack(total, [p.shape for p in sparts])))
    for n, width in (("conv_w", conv_w.shape[2]), ("ffn_conv_w", ffn_conv_w.shape[2])):
        sgrads[n] = lax.dynamic_slice_in_dim(sgrads[n], me * width, width, axis=2)
    sw = dict(norm1_g=(norm1_g, m_norm1_g, v_norm1_g), pool_scale=(pool_scale, m_pool_scale, v_pool_scale),
              sgu_norm_g=(sgu_norm_g, m_sgu_norm_g, v_sgu_norm_g), sgu_w=(sgu_w, m_sgu_w, v_sgu_w),
              sgu_b=(sgu_b, m_sgu_b, v_sgu_b), conv_w=(conv_w, m_conv_w, v_conv_w),
              norm2_g=(norm2_g, m_norm2_g, v_norm2_g), ffn_conv_w=(ffn_conv_w, m_ffn_conv_w, v_ffn_conv_w),
              final_g=(final_g, m_final_g, v_final_g))
    shapes = [sw[n][0].shape for n in snames]
    upd = _adamw_small(_pack([sw[n][0] for n in snames]), _pack([sgrads[n] for n in snames]),
                       _pack([sw[n][1] for n in snames]), _pack([sw[n][2] for n in snames]), "adamw_small")
    sdelta, sm, sv = (dict(zip(snames, _unpack(u, shapes))) for u in upd)
    res = {n: [sgrads[n], sdelta[n], sm[n], sv[n]] for n in snames}

    rs_finish(pending[2], [upd[0]] + [o[0] for o in adam_out.values()])
    for n, outs in adam_out.items():
        res[n] = [o.reshape(wmv[n][0].shape) for o in outs]

    order = ["norm1_g", "w_in", "w_pool", "pool_scale", "sgu_norm_g", "sgu_w", "sgu_b", "conv_w", "w_pool_out",
             "w_sgu_out", "w_conv_out", "w_o", "norm2_g", "w_up", "ffn_conv_w", "w_down", "final_g"]
    return (loss, grad_x) + tuple(res[n][k] for k in range(4) for n in order)
```

```python
import functools

import jax
import jax.numpy as jnp
from jax import lax
from jax.experimental import pallas as pl
from jax.experimental.pallas import tpu as pltpu

BF16 = jnp.bfloat16
F32 = jnp.float32
EPS = 1e-6
MIB = 1024 * 1024
V7X_VMEM_BYTES = 64 * MIB
VMEM_LIMIT = 56 * MIB
LANE = 128
N_DEV = 8


def _params(sem, **kw):
    return pltpu.CompilerParams(dimension_semantics=sem, vmem_limit_bytes=VMEM_LIMIT, **kw)


def _pick(n, cands):
    for c in cands:
        if n % c == 0:
            return c
    return n


_DIMS = {"nn": (((1,), (0,)), ((), ())), "nt": (((1,), (1,)), ((), ())), "tn": (((0,), (0,)), ((), ()))}


def _mm(a, b, *, mode, out_dtype, name, tm, tn, tk, res=None, prev=None, deps=(), cols=(0, 1), b_koff=0,
        out_off=0, out_n=None):
    if mode == "nn":
        M, Kc = a.shape
        nj = (b.shape[1] // tn - cols[0] + cols[1] - 1) // cols[1]
        out_w = b.shape[1]
    elif mode == "nt":
        M, Kc = a.shape
        nj = b.shape[0] // tn
        out_w = b.shape[0]
    else:
        Kc, M = a.shape
        nj = b.shape[1] // tn
        out_w = out_n or b.shape[1]
    nk = Kc // tk
    assert M % tm == 0 and Kc % tk == 0, (name, M, Kc, tm, tk)
    off, stride = cols
    if mode == "nn":
        a_spec = pl.BlockSpec((tm, tk), lambda i, j, k: (i, k))
        b_spec = pl.BlockSpec((tk, tn), lambda i, j, k: (k, off + j * stride))
        o_spec = pl.BlockSpec((tm, tn), lambda i, j, k: (i, off + j * stride))
    elif mode == "nt":
        a_spec = pl.BlockSpec((tm, tk), lambda i, j, k: (i, k))
        b_spec = pl.BlockSpec((tn, tk), lambda i, j, k: (j, b_koff + k))
        o_spec = pl.BlockSpec((tm, tn), lambda i, j, k: (i, j))
    else:
        a_spec = pl.BlockSpec((tk, tm), lambda i, j, k: (k, i))
        b_spec = pl.BlockSpec((tk, tn), lambda i, j, k: (k, j))
        o_spec = pl.BlockSpec((tm, tn), lambda i, j, k: (i, out_off + j))
    dims = _DIMS[mode]
    has_res = res is not None
    has_prev = prev is not None
    n_in = 2 + has_res + has_prev + len(deps)

    def body(*refs):
        a_ref, b_ref = refs[0], refs[1]
        o_ref = refs[n_in]

        def finish(acc):
            if has_res:
                acc = acc + refs[2][...]
            o_ref[...] = acc.astype(o_ref.dtype)

        if nk == 1:
            finish(lax.dot_general(a_ref[...], b_ref[...], dims, preferred_element_type=F32))
        else:
            acc_ref = refs[n_in + 1]
            k = pl.program_id(2)

            @pl.when(k == 0)
            def _():
                acc_ref[...] = jnp.zeros(acc_ref.shape, F32)

            acc_ref[...] += lax.dot_general(a_ref[...], b_ref[...], dims, preferred_element_type=F32)

            @pl.when(k == nk - 1)
            def _():
                finish(acc_ref[...])

    in_specs = [a_spec, b_spec]
    args = [a, b]
    if has_res:
        in_specs.append(pl.BlockSpec((tm, tn), lambda i, j, k: (i, j)))
        args.append(res)
    aliases = {}
    if has_prev:
        aliases = {len(args): 0}
        in_specs.append(pl.BlockSpec(memory_space=pl.ANY))
        args.append(prev)
    for d in deps:
        in_specs.append(pl.BlockSpec(memory_space=pl.ANY))
        args.append(d)
    return pl.pallas_call(
        body,
        name=name,
        grid=(M // tm, nj, nk),
        in_specs=in_specs,
        out_specs=o_spec,
        out_shape=jax.ShapeDtypeStruct((M, out_w), out_dtype),
        scratch_shapes=[pltpu.VMEM((tm, tn), F32)] if nk > 1 else [],
        input_output_aliases=aliases,
        compiler_params=_params(("parallel", "parallel", "arbitrary")),
    )(*args)


HALO = 16
CH = 256


def _fill_pad(pad_ref, chunk_fn, T):
    z = jnp.zeros((HALO, pad_ref.shape[1]), F32)
    pad_ref[pl.ds(0, HALO), :] = z
    pad_ref[pl.ds(HALO + T, HALO), :] = z

    def body(c, carry):
        r0 = pl.multiple_of(c * CH, CH)
        pad_ref[pl.ds(HALO + r0, CH), :] = chunk_fn(r0)
        return carry

    lax.fori_loop(0, T // CH, body, 0)


def _ext(pad_ref, r0):
    return pad_ref[pl.ds(r0, CH + 2 * HALO), :]


def _tap(ext, o):
    if o == 0:
        return ext[HALO:HALO + CH]
    return pltpu.roll(ext, (-o) % ext.shape[0], axis=0)[HALO:HALO + CH]


def _chunks(T, fn, init=0):
    def body(c, carry):
        return fn(pl.multiple_of(c * CH, CH), carry)

    return lax.fori_loop(0, T // CH, body, init)


def _conv3(ext, w):
    return _tap(ext, -1) * w[0:1] + _tap(ext, 0) * w[1:2] + _tap(ext, 1) * w[2:3]


def _conv3_t(ext, w):
    return _tap(ext, 1) * w[0:1] + _tap(ext, 0) * w[1:2] + _tap(ext, -1) * w[2:3]


def _conv3_dw(xext, d):
    return [jnp.sum(_tap(xext, k - 1) * d, axis=0, keepdims=True) for k in range(3)]


def _sigmoid(x):
    return 0.5 * (jnp.tanh(0.5 * x) + 1.0)


def _silu(x):
    h = 0.5 * x
    return h * (jnp.tanh(h) + 1.0)


_GELU_C = 0.7978845608028654


def _gelu(x):
    return 0.5 * x * (1.0 + jnp.tanh(_GELU_C * (x + 0.044715 * (x * x * x))))


def _gelu_grad(x):
    t = jnp.tanh(_GELU_C * (x + 0.044715 * (x * x * x)))
    return 0.5 * (1.0 + t) + 0.5 * x * (1.0 - t * t) * (_GELU_C * (1.0 + 3.0 * 0.044715 * (x * x)))


def _row_tile(T):
    return _pick(T, (256, 128))


def _rms_fwd(x, g, name):
    T, D = x.shape
    tm = _row_tile(T)

    def body(x_ref, g_ref, h_ref):
        xv = x_ref[...]
        r = lax.rsqrt(jnp.mean(xv * xv, axis=-1, keepdims=True) + EPS)
        h_ref[...] = (xv * r * g_ref[...]).astype(BF16)

    return pl.pallas_call(
        body, name=name, grid=(T // tm,),
        in_specs=[pl.BlockSpec((tm, D), lambda i: (i, 0)), pl.BlockSpec((1, D), lambda i: (0, 0))],
        out_specs=pl.BlockSpec((tm, D), lambda i: (i, 0)),
        out_shape=jax.ShapeDtypeStruct((T, D), BF16),
        compiler_params=_params(("parallel",)),
    )(x, g)


def _rms_bwd(dh, x, g, dres, name, deps=()):
    T, D = x.shape
    tm = _row_tile(T)

    def body(dh_ref, x_ref, g_ref, dres_ref, *rest):
        dx_ref, dxb_ref, dg_ref = rest[len(deps):]
        i = pl.program_id(0)
        xv = x_ref[...]
        r = lax.rsqrt(jnp.mean(xv * xv, axis=-1, keepdims=True) + EPS)
        n = xv * r
        dh_v = dh_ref[...].astype(F32)
        dn = dh_v * g_ref[...]
        dx = dres_ref[...] + r * (dn - n * jnp.mean(dn * n, axis=-1, keepdims=True))
        dx_ref[...] = dx
        dxb_ref[...] = dx.astype(BF16)
        dg = jnp.sum(dh_v * n, axis=0, keepdims=True)

        @pl.when(i == 0)
        def _():
            dg_ref[...] = dg

        @pl.when(i > 0)
        def _():
            dg_ref[...] += dg

    row = pl.BlockSpec((tm, D), lambda i: (i, 0))
    vec = pl.BlockSpec((1, D), lambda i: (0, 0))
    return pl.pallas_call(
        body, name=name, grid=(T // tm,),
        in_specs=[row, row, vec, row] + [pl.BlockSpec(memory_space=pl.ANY)] * len(deps),
        out_specs=[row, row, vec],
        out_shape=[jax.ShapeDtypeStruct((T, D), F32), jax.ShapeDtypeStruct((T, D), BF16),
                   jax.ShapeDtypeStruct((1, D), F32)],
        compiler_params=_params(("arbitrary",)),
    )(dh, x, g, dres, *deps)


def _loss_head(x, tgt, g, name):
    T, D = x.shape
    tm = _row_tile(T)

    def body(x_ref, t_ref, g_ref, dx_ref, dxb_ref, dg_ref, l_ref):
        i = pl.program_id(0)
        xv = x_ref[...]
        gv = g_ref[...]
        r = lax.rsqrt(jnp.mean(xv * xv, axis=-1, keepdims=True) + EPS)
        n = xv * r
        e = n * gv - t_ref[...]
        dy = e * (1.0 / D)
        dn = dy * gv
        dx = r * (dn - n * jnp.mean(dn * n, axis=-1, keepdims=True))
        dx_ref[...] = dx
        dxb_ref[...] = dx.astype(BF16)
        dg = jnp.sum(dy * n, axis=0, keepdims=True)
        per_tok = jnp.mean(e * e, axis=-1, keepdims=True)
        lv = jnp.broadcast_to(0.5 * jnp.sum(per_tok, axis=0, keepdims=True), (1, LANE))

        @pl.when(i == 0)
        def _():
            dg_ref[...] = dg
            l_ref[...] = lv

        @pl.when(i > 0)
        def _():
            dg_ref[...] += dg
            l_ref[...] += lv

    row = pl.BlockSpec((tm, D), lambda i: (i, 0))
    vec = pl.BlockSpec((1, D), lambda i: (0, 0))
    return pl.pallas_call(
        body, name=name, grid=(T // tm,),
        in_specs=[row, row, vec],
        out_specs=[row, row, vec, pl.BlockSpec((1, LANE), lambda i: (0, 0))],
        out_shape=[jax.ShapeDtypeStruct((T, D), F32), jax.ShapeDtypeStruct((T, D), BF16),
                   jax.ShapeDtypeStruct((1, D), F32), jax.ShapeDtypeStruct((1, LANE), F32)],
        compiler_params=_params(("arbitrary",)),
    )(x, tgt, g)


POOL_WINDOWS = (2, 4, 8, 16)


def _pool_offsets(w):
    return range(-(w // 2), w - w // 2)


def _pool_cnt(r0, w, T, shape):
    t = r0 + lax.broadcasted_iota(jnp.int32, shape, 0)
    lo = jnp.maximum(t - w // 2, 0)
    hi = jnp.minimum(t + (w - w // 2 - 1), T - 1)
    return (hi - lo + 1).astype(F32)


def _pooled(pad_ref, r0, w, T):
    ext = _ext(pad_ref, r0)
    s = None
    for o in _pool_offsets(w):
        tap = _tap(ext, o)
        s = tap if s is None else s + tap
    cur = ext[HALO:HALO + CH]
    return s / _pool_cnt(r0, w, T, cur.shape) - cur


def _pool_fwd(P, w_pool, scale, name):
    T = P.shape[0]
    G, PG, _ = w_pool.shape

    def body(a_ref, w_ref, s_ref, o_ref, pad_ref):
        g = pl.program_id(0)
        _fill_pad(pad_ref, lambda r0: a_ref[pl.ds(r0, CH), :].astype(F32), T)
        for gi, w in enumerate(POOL_WINDOWS):
            @pl.when(g == gi)
            def _(w=w):
                def chunk(r0, carry):
                    pooled = _pooled(pad_ref, r0, w, T).astype(BF16)
                    y = jnp.dot(pooled, w_ref[...], preferred_element_type=F32) * s_ref[...]
                    o_ref[pl.ds(r0, CH), :] = y.astype(BF16)
                    return carry

                _chunks(T, chunk)

    return pl.pallas_call(
        body, name=name, grid=(G,),
        in_specs=[pl.BlockSpec((T, PG), lambda g: (0, g)),
                  pl.BlockSpec((None, PG, PG), lambda g: (g, 0, 0)),
                  pl.BlockSpec((1, PG), lambda g: (0, g))],
        out_specs=pl.BlockSpec((T, PG), lambda g: (0, g)),
        out_shape=jax.ShapeDtypeStruct((T, G * PG), BF16),
        scratch_shapes=[pltpu.VMEM((T + 2 * HALO, PG), F32)],
        compiler_params=_params(("parallel",)),
    )(P, w_pool, scale)


def _pool_bwd(P, dy, w_pool, scale, name):
    T = P.shape[0]
    G, PG, _ = w_pool.shape

    def body(a_ref, dy_ref, w_ref, s_ref, da_ref, dw_ref, ds_ref, pad_ref, gpad_ref):
        g = pl.program_id(0)
        gpad_ref[pl.ds(0, HALO), :] = jnp.zeros((HALO, PG), F32)
        gpad_ref[pl.ds(HALO + T, HALO), :] = jnp.zeros((HALO, PG), F32)
        _fill_pad(pad_ref, lambda r0: a_ref[pl.ds(r0, CH), :].astype(F32), T)
        for gi, w in enumerate(POOL_WINDOWS):
            @pl.when(g == gi)
            def _(w=w):
                def chunk1(r0, carry):
                    dw, ds = carry
                    pooled = _pooled(pad_ref, r0, w, T).astype(BF16)
                    ypre = jnp.dot(pooled, w_ref[...], preferred_element_type=F32)
                    dyv = dy_ref[pl.ds(r0, CH), :].astype(F32)
                    ds = ds + jnp.sum(dyv * ypre, axis=0, keepdims=True)
                    dyp = (dyv * s_ref[...]).astype(BF16)
                    dw = dw + lax.dot_general(pooled, dyp, _DIMS["tn"], preferred_element_type=F32)
                    dpool = lax.dot_general(dyp, w_ref[...], _DIMS["nt"], preferred_element_type=F32)
                    gpad_ref[pl.ds(HALO + r0, CH), :] = dpool / _pool_cnt(r0, w, T, dpool.shape)
                    return dw, ds

                dw, ds = _chunks(T, chunk1, (jnp.zeros((PG, PG), F32), jnp.zeros((1, PG), F32)))
                dw_ref[...] = dw
                ds_ref[...] = ds

                def chunk2(r0, carry):
                    ext = _ext(gpad_ref, r0)
                    cur = ext[HALO:HALO + CH]
                    acc = None
                    for o in _pool_offsets(w):
                        tap = _tap(ext, -o)
                        acc = tap if acc is None else acc + tap
                    da_ref[pl.ds(r0, CH), :] = (acc - cur * _pool_cnt(r0, w, T, cur.shape)).astype(BF16)
                    return carry

                _chunks(T, chunk2)

    col = pl.BlockSpec((T, PG), lambda g: (0, g))
    return pl.pallas_call(
        body, name=name, grid=(G,),
        in_specs=[col, col, pl.BlockSpec((None, PG, PG), lambda g: (g, 0, 0)), pl.BlockSpec((1, PG), lambda g: (0, g))],
        out_specs=[col, pl.BlockSpec((None, PG, PG), lambda g: (g, 0, 0)), pl.BlockSpec((1, PG), lambda g: (0, g))],
        out_shape=[jax.ShapeDtypeStruct((T, G * PG), BF16), jax.ShapeDtypeStruct((G, PG, PG), F32),
                   jax.ShapeDtypeStruct((1, G * PG), F32)],
        scratch_shapes=[pltpu.VMEM((T + 2 * HALO, PG), F32), pltpu.VMEM((T + 2 * HALO, PG), F32)],
        compiler_params=_params(("parallel",)),
    )(P, dy, w_pool, scale)


SGU_CHUNK = 128


def _sgu_common(u_ref, v_ref, gs_ref):
    up = u_ref[...].astype(F32)
    vp = v_ref[...].astype(F32)
    gv = _gelu(vp)
    rv = lax.rsqrt(jnp.mean(gv * gv, axis=-1, keepdims=True) + EPS)
    nrm = gv * rv
    return up, vp, nrm, rv, (nrm * gs_ref[...]).astype(BF16)


def _sgu_fwd(P, sgu_w, b_exp, gs, name):
    T = P.shape[0]
    G, _, SG = b_exp.shape
    DP = G * SG
    tm = _row_tile(T)

    def body(u_ref, v_ref, w_ref, b_ref, gs_ref, o_ref):
        up, _, _, _, vn = _sgu_common(u_ref, v_ref, gs_ref)
        gu = _gelu(up)
        for n in range(tm // SGU_CHUNK):
            rows = slice(n * SGU_CHUNK, (n + 1) * SGU_CHUNK)
            for g in range(G):
                cols = slice(g * SG, (g + 1) * SG)
                z = jnp.dot(w_ref[g], vn[rows, cols], preferred_element_type=F32) + b_ref[g]
                o_ref[rows, cols] = (gu[rows, cols] * z).astype(BF16)

    return pl.pallas_call(
        body, name=name, grid=(T // tm,),
        in_specs=[pl.BlockSpec((tm, DP), lambda i: (i, 1)), pl.BlockSpec((tm, DP), lambda i: (i, 2)),
                  pl.BlockSpec(sgu_w.shape, lambda i: (0, 0, 0)), pl.BlockSpec(b_exp.shape, lambda i: (0, 0, 0)),
                  pl.BlockSpec((1, DP), lambda i: (0, 0))],
        out_specs=pl.BlockSpec((tm, DP), lambda i: (i, 0)),
        out_shape=jax.ShapeDtypeStruct((T, DP), BF16),
        compiler_params=_params(("parallel",)),
    )(P, P, sgu_w, b_exp, gs)


def _sgu_bwd(P, dy, da, sgu_w, b_exp, gs, name):
    T = P.shape[0]
    G, _, SG = b_exp.shape
    DP = G * SG
    tm = _row_tile(T)

    def body(u_ref, v_ref, dy_ref, da_ref, w_ref, b_ref, gs_ref, o_ref, dw_ref, db_ref, dgs_ref, dzs_ref):
        i = pl.program_id(0)
        up, vp, nrm, rv, vn = _sgu_common(u_ref, v_ref, gs_ref)
        gu = _gelu(up)
        dyv = dy_ref[...].astype(F32)
        o_ref[:, 0:DP] = da_ref[...]

        @pl.when(i == 0)
        def _():
            dw_ref[...] = jnp.zeros(dw_ref.shape, F32)
            dzs_ref[...] = jnp.zeros(dzs_ref.shape, F32)
            dgs_ref[...] = jnp.zeros(dgs_ref.shape, F32)

        dgs = jnp.zeros((1, DP), F32)
        for n in range(tm // SGU_CHUNK):
            rows = slice(n * SGU_CHUNK, (n + 1) * SGU_CHUNK)
            dvn_parts = []
            for g in range(G):
                cols = slice(g * SG, (g + 1) * SG)
                vng = vn[rows, cols]
                z = jnp.dot(w_ref[g], vng, preferred_element_type=F32) + b_ref[g]
                dyg = dyv[rows, cols]
                du = dyg * z
                o_ref[rows, DP + g * SG:DP + (g + 1) * SG] = (du * _gelu_grad(up[rows, cols])).astype(BF16)
                dz = dyg * gu[rows, cols]
                dzb = dz.astype(BF16)
                dzs_ref[g] += dz
                dw_ref[g] += lax.dot_general(dzb, vng, _DIMS["nt"], preferred_element_type=F32)
                dvn_parts.append(lax.dot_general(w_ref[g], dzb, _DIMS["tn"], preferred_element_type=F32))
            dvn = jnp.concatenate(dvn_parts, axis=1)
            nr = nrm[rows]
            dgs = dgs + jnp.sum(dvn * nr, axis=0, keepdims=True)
            dn = dvn * gs_ref[...]
            dgv = rv[rows] * (dn - nr * jnp.mean(dn * nr, axis=-1, keepdims=True))
            o_ref[rows, 2 * DP:3 * DP] = (dgv * _gelu_grad(vp[rows])).astype(BF16)
        dgs_ref[...] += dgs

        @pl.when(i == T // tm - 1)
        def _():
            for g in range(G):
                db_ref[g] = jnp.broadcast_to(jnp.sum(dzs_ref[g], axis=1, keepdims=True), (SGU_CHUNK, SG))

    full3 = lambda a: pl.BlockSpec(a.shape, lambda i: (0, 0, 0))
    return pl.pallas_call(
        body, name=name, grid=(T // tm,),
        in_specs=[pl.BlockSpec((tm, DP), lambda i: (i, 1)), pl.BlockSpec((tm, DP), lambda i: (i, 2)),
                  pl.BlockSpec((tm, DP), lambda i: (i, 0)), pl.BlockSpec((tm, DP), lambda i: (i, 0)),
                  full3(sgu_w), full3(b_exp), pl.BlockSpec((1, DP), lambda i: (0, 0))],
        out_specs=[pl.BlockSpec((tm, 3 * DP), lambda i: (i, 0)), full3(sgu_w), full3(b_exp),
                   pl.BlockSpec((1, DP), lambda i: (0, 0))],
        out_shape=[jax.ShapeDtypeStruct((T, 6 * DP), BF16), jax.ShapeDtypeStruct(sgu_w.shape, F32),
                   jax.ShapeDtypeStruct(b_exp.shape, F32), jax.ShapeDtypeStruct((1, DP), F32)],
        scratch_shapes=[pltpu.VMEM(b_exp.shape, F32)],
        compiler_params=_params(("arbitrary",)),
    )(P, P, dy, da, sgu_w, b_exp, gs)


def _conv_fwd(P, cw, name):
    T = P.shape[0]
    DP = cw.shape[1]
    tc = 256
    nb = DP // tc

    def body(xc_ref, bg_ref, cg_ref, w_ref, o_ref, pad_ref):
        _fill_pad(pad_ref, lambda r0: cg_ref[pl.ds(r0, CH), :].astype(F32) * xc_ref[pl.ds(r0, CH), :].astype(F32), T)
        w = w_ref[...]

        def chunk(r0, carry):
            cq = _conv3(_ext(pad_ref, r0), w)
            o_ref[pl.ds(r0, CH), :] = (bg_ref[pl.ds(r0, CH), :].astype(F32) * cq).astype(BF16)
            return carry

        _chunks(T, chunk)

    col = lambda off: pl.BlockSpec((T, tc), lambda j: (0, off * nb + j))
    return pl.pallas_call(
        body, name=name, grid=(nb,),
        in_specs=[col(3), col(4), col(5), pl.BlockSpec((3, tc), lambda j: (0, j))],
        out_specs=pl.BlockSpec((T, tc), lambda j: (0, j)),
        out_shape=jax.ShapeDtypeStruct((T, DP), BF16),
        scratch_shapes=[pltpu.VMEM((T + 2 * HALO, tc), F32)],
        compiler_params=_params(("parallel",)),
    )(P, P, P, cw)


def _conv_bwd(P, dy, cw, dPl, name):
    T = P.shape[0]
    DP = cw.shape[1]
    tc = 256
    nb = DP // tc

    def body(xc_ref, bg_ref, cg_ref, dy_ref, w_ref, prev_ref, o_ref, dw_ref, qpad_ref, dpad_ref):
        del prev_ref
        seg = pl.program_id(1)
        w = w_ref[...]
        rows = lambda ref, r0: ref[pl.ds(r0, CH), :].astype(F32)
        _fill_pad(qpad_ref, lambda r0: rows(cg_ref, r0) * rows(xc_ref, r0), T)
        _fill_pad(dpad_ref, lambda r0: rows(dy_ref, r0) * rows(bg_ref, r0), T)

        @pl.when(seg == 0)
        def _():
            def chunk(r0, carry):
                dq = _conv3_t(_ext(dpad_ref, r0), w)
                o_ref[pl.ds(r0, CH), :] = (dq * rows(cg_ref, r0)).astype(BF16)
                dcq = rows(dy_ref, r0) * rows(bg_ref, r0)
                return [a + b for a, b in zip(carry, _conv3_dw(_ext(qpad_ref, r0), dcq))]

            dws = _chunks(T, chunk, [jnp.zeros((1, tc), F32)] * 3)
            dw_ref[...] = jnp.concatenate(dws + [jnp.zeros((5, tc), F32)], axis=0)

        @pl.when(seg == 1)
        def _():
            def chunk(r0, carry):
                cq = _conv3(_ext(qpad_ref, r0), w)
                o_ref[pl.ds(r0, CH), :] = (rows(dy_ref, r0) * cq).astype(BF16)
                return carry

            _chunks(T, chunk)

        @pl.when(seg == 2)
        def _():
            def chunk(r0, carry):
                dq = _conv3_t(_ext(dpad_ref, r0), w)
                o_ref[pl.ds(r0, CH), :] = (dq * rows(xc_ref, r0)).astype(BF16)
                return carry

            _chunks(T, chunk)

    col = lambda off: pl.BlockSpec((T, tc), lambda j, s: (0, off * nb + j))
    return pl.pallas_call(
        body, name=name, grid=(nb, 3),
        in_specs=[col(3), col(4), col(5), pl.BlockSpec((T, tc), lambda j, s: (0, j)),
                  pl.BlockSpec((3, tc), lambda j, s: (0, j)), pl.BlockSpec(memory_space=pl.ANY)],
        out_specs=[pl.BlockSpec((T, tc), lambda j, s: (0, (3 + s) * nb + j)), pl.BlockSpec((8, tc), lambda j, s: (0, j))],
        out_shape=[jax.ShapeDtypeStruct(dPl.shape, BF16), jax.ShapeDtypeStruct((8, DP), F32)],
        scratch_shapes=[pltpu.VMEM((T + 2 * HALO, tc), F32), pltpu.VMEM((T + 2 * HALO, tc), F32)],
        input_output_aliases={5: 0},
        compiler_params=_params(("parallel", "arbitrary")),
    )(P, P, P, dy, cw, dPl)


def _merge_fwd(P, ys, ws, name):
    T = P.shape[0]
    DP, D = ws[0].shape
    tm = _pick(T, (512, 256, 128))
    tn = _pick(D, (1024, 512, 256))
    goff = (6 * DP) // tn

    def body(ya, yb, yc, wa, wb, wc, ga, gb, gc, oa_ref, ob_ref, oc_ref, m_ref):
        m = None
        for y, w, g, o_ref in ((ya, wa, ga, oa_ref), (yb, wb, gb, ob_ref), (yc, wc, gc, oc_ref)):
            o = jnp.dot(y[...], w[...], preferred_element_type=F32)
            o_ref[...] = o.astype(BF16)
            t = _sigmoid(g[...].astype(F32)) * o
            m = t if m is None else m + t
        m_ref[...] = m.astype(BF16)

    yspec = pl.BlockSpec((tm, DP), lambda i, j: (i, 0))
    wspec = pl.BlockSpec((DP, tn), lambda i, j: (0, j))
    gspec = lambda b: pl.BlockSpec((tm, tn), lambda i, j: (i, goff + b * (D // tn) + j))
    ospec = pl.BlockSpec((tm, tn), lambda i, j: (i, j))
    return pl.pallas_call(
        body, name=name, grid=(T // tm, D // tn),
        in_specs=[yspec] * 3 + [wspec] * 3 + [gspec(0), gspec(1), gspec(2)],
        out_specs=[ospec] * 4,
        out_shape=[jax.ShapeDtypeStruct((T, D), BF16)] * 4,
        compiler_params=_params(("parallel", "parallel")),
    )(*ys, *ws, P, P, P)


def _merge_bwd(P, dM, os_, name):
    T, D = dM.shape
    tm = _row_tile(T)

    def body(dm_ref, oa, ob, oc, g_ref, da_ref, db_ref, dc_ref, dg_ref):
        dm = dm_ref[...].astype(F32)
        for b, (o_ref, d_ref) in enumerate(((oa, da_ref), (ob, db_ref), (oc, dc_ref))):
            s = _sigmoid(g_ref[:, b * D:(b + 1) * D].astype(F32))
            d_ref[...] = (dm * s).astype(BF16)
            dg_ref[:, b * D:(b + 1) * D] = (dm * o_ref[...].astype(F32) * s * (1.0 - s)).astype(BF16)

    row = pl.BlockSpec((tm, D), lambda i: (i, 0))
    return pl.pallas_call(
        body, name=name, grid=(T // tm,),
        in_specs=[row] * 4 + [pl.BlockSpec((tm, 3 * D), lambda i: (i, 1))],
        out_specs=[row] * 3 + [pl.BlockSpec((tm, 3 * D), lambda i: (i, 0))],
        out_shape=[jax.ShapeDtypeStruct((T, D), BF16)] * 3 + [jax.ShapeDtypeStruct((T, 3 * D), BF16)],
        compiler_params=_params(("parallel",)),
    )(dM, *os_, P)


def _glu_fwd(U, fw, name):
    T, F2 = U.shape
    F = F2 // 2
    tc = 256
    nb = F // tc

    def body(ug_ref, uv_ref, wg_ref, wv_ref, o_ref, gpad_ref, vpad_ref):
        _fill_pad(gpad_ref, lambda r0: ug_ref[pl.ds(r0, CH), :].astype(F32), T)
        _fill_pad(vpad_ref, lambda r0: uv_ref[pl.ds(r0, CH), :].astype(F32), T)
        wg, wv = wg_ref[...], wv_ref[...]

        def chunk(r0, carry):
            gate = _conv3(_ext(gpad_ref, r0), wg)
            val = _conv3(_ext(vpad_ref, r0), wv)
            o_ref[pl.ds(r0, CH), :] = (_silu(gate) * val).astype(BF16)
            return carry

        _chunks(T, chunk)

    return pl.pallas_call(
        body, name=name, grid=(nb,),
        in_specs=[pl.BlockSpec((T, tc), lambda j: (0, j)), pl.BlockSpec((T, tc), lambda j: (0, nb + j)),
                  pl.BlockSpec((3, tc), lambda j: (0, j)), pl.BlockSpec((3, tc), lambda j: (0, nb + j))],
        out_specs=pl.BlockSpec((T, tc), lambda j: (0, j)),
        out_shape=jax.ShapeDtypeStruct((T, F), BF16),
        scratch_shapes=[pltpu.VMEM((T + 2 * HALO, tc), F32)] * 2,
        compiler_params=_params(("parallel",)),
    )(U, U, fw, fw)


def _glu_bwd(U, dact, fw, name):
    T, F2 = U.shape
    F = F2 // 2
    tc = 128
    nb = F // tc

    def body(ug_ref, uv_ref, da_ref, wg_ref, wv_ref, dug_ref, duv_ref, dwg_ref, dwv_ref, gpad, vpad, dgpad, dvpad):
        _fill_pad(gpad, lambda r0: ug_ref[pl.ds(r0, CH), :].astype(F32), T)
        _fill_pad(vpad, lambda r0: uv_ref[pl.ds(r0, CH), :].astype(F32), T)
        wg, wv = wg_ref[...], wv_ref[...]

        def chunk1(r0, carry):
            gate = _conv3(_ext(gpad, r0), wg)
            val = _conv3(_ext(vpad, r0), wv)
            s = _sigmoid(gate)
            da = da_ref[pl.ds(r0, CH), :].astype(F32)
            dgpad[pl.ds(HALO + r0, CH), :] = da * val * (s * (1.0 + gate * (1.0 - s)))
            dvpad[pl.ds(HALO + r0, CH), :] = da * (gate * s)
            return carry

        z = jnp.zeros((HALO, tc), F32)
        for p in (dgpad, dvpad):
            p[pl.ds(0, HALO), :] = z
            p[pl.ds(HALO + T, HALO), :] = z
        _chunks(T, chunk1)

        def chunk2(r0, carry):
            new = []
            for pad, dpad, wk, o_ref in ((gpad, dgpad, wg, dug_ref), (vpad, dvpad, wv, duv_ref)):
                dext = _ext(dpad, r0)
                taps = [_tap(dext, 1), _tap(dext, 0), _tap(dext, -1)]
                o_ref[pl.ds(r0, CH), :] = (taps[0] * wk[0:1] + taps[1] * wk[1:2] + taps[2] * wk[2:3]).astype(BF16)
                xs = pad[pl.ds(HALO + r0, CH), :]
                new += [jnp.sum(xs * t, axis=0, keepdims=True) for t in taps]
            return [a + b for a, b in zip(carry, new)]

        dws = _chunks(T, chunk2, [jnp.zeros((1, tc), F32)] * 6)
        dwg_ref[...] = jnp.concatenate(dws[:3] + [jnp.zeros((5, tc), F32)], axis=0)
        dwv_ref[...] = jnp.concatenate(dws[3:] + [jnp.zeros((5, tc), F32)], axis=0)

    lo = pl.BlockSpec((T, tc), lambda j: (0, j))
    hi = pl.BlockSpec((T, tc), lambda j: (0, nb + j))
    wlo = pl.BlockSpec((3, tc), lambda j: (0, j))
    whi = pl.BlockSpec((3, tc), lambda j: (0, nb + j))
    dwspec = pl.BlockSpec((8, tc), lambda j: (0, j))
    return pl.pallas_call(
        body, name=name, grid=(nb,),
        in_specs=[lo, hi, lo, wlo, whi],
        out_specs=[lo, lo, dwspec, dwspec],
        out_shape=[jax.ShapeDtypeStruct((T, F), BF16)] * 2 + [jax.ShapeDtypeStruct((8, F), F32)] * 2,
        scratch_shapes=[pltpu.VMEM((T + 2 * HALO, tc), F32)] * 4,
        compiler_params=_params(("parallel",)),
    )(U, U, dact, fw, fw)


_ANY = pl.BlockSpec(memory_space=pl.ANY)
_MESH = pl.DeviceIdType.MESH


def _mesh_pos():
    return lax.axis_index("x"), lax.axis_index("y"), lax.axis_index("c")


def _other_chips(x, y):
    return [(1 - x, y), (x, 1 - y), (1 - x, 1 - y)]


_HBM = pl.BlockSpec(memory_space=pltpu.HBM)
_SEM = pl.BlockSpec(memory_space=pltpu.SEMAPHORE)
_EFFECT = pltpu.SideEffectType.DATAFLOW_SIDE_EFFECTING


def _in_hbm(a):
    return pltpu.with_memory_space_constraint(a, pltpu.HBM)


def _split_start(groups, name):
    flat, where = [], []
    for bufs, _, _ in groups:
        idx = []
        for b in bufs:
            hit = [i for i, f in enumerate(flat) if f is b]
            if not hit:
                flat.append(b)
            idx.append(hit[0] if hit else len(flat) - 1)
        where.append(idx)
    nb = len(flat)
    ng = len(groups)

    def body(*refs):
        ins = refs[:nb]
        sems = refs[nb:nb + 2 * ng]
        token = refs[-1]
        for gi, (_, make_copies, _) in enumerate(groups):
            for cp in make_copies([ins[i] for i in where[gi]], sems[2 * gi], sems[2 * gi + 1]):
                cp.start()
        token[...] = jnp.zeros(token.shape, token.dtype)

    sem_shapes = []
    for _, _, n in groups:
        sem_shapes += [pltpu.SemaphoreType.DMA((n,)), pltpu.SemaphoreType.DMA((n,))]
    outs = pl.pallas_call(
        body, name=name,
        in_specs=[_HBM] * nb,
        out_specs=[_SEM] * (2 * ng) + [_HBM] * nb + [pl.BlockSpec(memory_space=pltpu.VMEM)],
        out_shape=sem_shapes + [pltpu.HBM(b.shape, b.dtype) for b in flat] + [jax.ShapeDtypeStruct((8, LANE), F32)],
        input_output_aliases={i: 2 * ng + i for i in range(nb)},
        compiler_params=pltpu.CompilerParams(has_side_effects=_EFFECT),
    )(*[_in_hbm(b) for b in flat])
    res = [(outs[2 * gi], outs[2 * gi + 1], [outs[2 * ng + i] for i in where[gi]]) for gi in range(ng)]
    return res, outs[-1]


def _split_wait(started, make_copies, after, name):
    send, recv, bufs = started
    nb = len(bufs)
    after = list(after) if isinstance(after, (list, tuple)) else [after]

    def body(*refs):
        for cp in make_copies(refs[:nb], refs[nb], refs[nb + 1]):
            cp.wait_send()
            cp.wait_recv()

    outs = pl.pallas_call(
        body, name=name,
        in_specs=[_HBM] * nb + [_SEM, _SEM] + [_ANY] * len(after),
        out_specs=[_HBM] * nb,
        out_shape=[pltpu.HBM(b.shape, b.dtype) for b in bufs],
        input_output_aliases={i: i for i in range(nb)},
        compiler_params=pltpu.CompilerParams(has_side_effects=_EFFECT),
    )(*bufs, send, recv, *after)
    return list(outs)


def _slot(buf, s, cols=None):
    if len(buf.shape) == 3:
        return buf.at[s] if cols is None else buf.at[s, :, pl.ds(cols[0], cols[1])]
    ns = buf.shape[1] // N_DEV
    first, width = (0, ns) if cols is None else cols
    return buf.at[:, pl.ds(pl.multiple_of(s * ns, LANE) + first, width)]


def _copies_gather_chips(nbuf, cols=None):
    def make(bufs, send, recv):
        x, y, c = _mesh_pos()
        targets = [(x, y, 1 - c)] + [(px, py, c) for px, py in _other_chips(x, y)]
        cps = []
        for a in range(nbuf):
            mine = _slot(bufs[a], 4 * x + 2 * y + c, cols)
            for k, to in enumerate(targets):
                cps.append(pltpu.make_async_remote_copy(src_ref=mine, dst_ref=mine, send_sem=send.at[4 * a + k],
                                                        recv_sem=recv.at[4 * a + k], device_id=to, device_id_type=_MESH))
        return cps

    return make, 4 * nbuf


def _copies_gather_forward(nbuf, cols=None):
    def make(bufs, send, recv):
        x, y, c = _mesh_pos()
        cps = []
        for a in range(nbuf):
            for j, (px, py) in enumerate(_other_chips(x, y)):
                slot = _slot(bufs[a], 4 * px + 2 * py + c, cols)
                cps.append(pltpu.make_async_remote_copy(src_ref=slot, dst_ref=slot, send_sem=send.at[3 * a + j],
                                                        recv_sem=recv.at[3 * a + j], device_id=(x, y, 1 - c),
                                                        device_id_type=_MESH))
        return cps

    return make, 3 * nbuf


def _copies_rs_sibling(n):
    def make(bufs, send, recv):
        x, y, c = _mesh_pos()
        cps = []
        for a in range(n):
            for q in range(4):
                cps.append(pltpu.make_async_remote_copy(
                    src_ref=_slot(bufs[a], 2 * q + (1 - c)), dst_ref=bufs[n + a].at[q], send_sem=send.at[4 * a + q],
                    recv_sem=recv.at[4 * a + q], device_id=(x, y, 1 - c), device_id_type=_MESH))
        return cps

    return make, 4 * n


def _copies_rs_chips(n):
    def make(bufs, send, recv):
        x, y, c = _mesh_pos()
        cps = []
        for a in range(n):
            for j, (px, py) in enumerate(_other_chips(x, y)):
                cps.append(pltpu.make_async_remote_copy(
                    src_ref=bufs[a].at[j], dst_ref=bufs[n + a].at[j], send_sem=send.at[3 * a + j],
                    recv_sem=recv.at[3 * a + j], device_id=(px, py, c), device_id_type=_MESH))
        return cps

    return make, 3 * n


def _cast_place(stacked, layer, slot, dtype, name, deps=(), by_cols=False):
    _, R, C = stacked.shape
    tr = _rows_tile(R, C)

    def body(slot_ref, x_ref, *rest):
        del slot_ref
        rest[-1][...] = x_ref[...].astype(dtype)

    if by_cols:
        out_spec = pl.BlockSpec((tr, C), lambda i, s: (i, s[0]))
        out_shape = jax.ShapeDtypeStruct((R, N_DEV * C), dtype)
    else:
        out_spec = pl.BlockSpec((None, tr, C), lambda i, s: (s[0], i, 0))
        out_shape = jax.ShapeDtypeStruct((N_DEV, R, C), dtype)
    return pl.pallas_call(
        body, name=name,
        grid_spec=pltpu.PrefetchScalarGridSpec(
            num_scalar_prefetch=1, grid=(R // tr,),
            in_specs=[pl.BlockSpec((None, tr, C), lambda i, s: (layer, i, 0))] + [_ANY] * len(deps),
            out_specs=out_spec),
        out_shape=out_shape,
        compiler_params=_params(("parallel",)),
    )(slot, stacked, *deps)


def _rows_tile(R, C, budget=2 * MIB):
    if R * C * 4 <= budget:
        return R
    for t in (2048, 1024, 704, 512, 352, 256, 128, 64, 32, 16, 8):
        if R % t == 0 and t * C * 4 <= budget:
            return t
    return R


def _chip_of_relation(j, pos):
    x, y = pos[0], pos[1]
    return 2 * jnp.where(j == 1, x, 1 - x) + jnp.where(j == 0, y, 1 - y)


def _pair_sum(grad, recv, pos, name):
    _, R, C = recv.shape
    tr = _rows_tile(R, C, 8 * MIB)

    def body(pos_ref, g_ref, r_ref, o_ref):
        del pos_ref
        o_ref[...] = (g_ref[...].astype(F32) + r_ref[...].astype(F32)).astype(BF16)

    if len(grad.shape) == 3:
        g_spec = pl.BlockSpec((None, tr, C), lambda j, i, p: (2 * _chip_of_relation(j, p) + p[2], i, 0))
    else:
        g_spec = pl.BlockSpec((tr, C), lambda j, i, p: (i, 2 * _chip_of_relation(j, p) + p[2]))
    return pl.pallas_call(
        body, name=name,
        grid_spec=pltpu.PrefetchScalarGridSpec(
            num_scalar_prefetch=1, grid=(3, R // tr),
            in_specs=[g_spec,
                      pl.BlockSpec((None, tr, C), lambda j, i, p: (_chip_of_relation(j, p), i, 0))],
            out_specs=pl.BlockSpec((None, tr, C), lambda j, i, p: (j, i, 0))),
        out_shape=jax.ShapeDtypeStruct((3, R, C), BF16),
        compiler_params=_params(("parallel", "parallel")),
    )(pos, grad, recv)


ADAM_LR = 0.001
ADAM_B1 = 0.9
ADAM_B2 = 0.999
ADAM_EPS = 1e-08
ADAM_WD = 0.01
ADAM_STEP = 10


def _adamw(w, g, m, v):
    m = ADAM_B1 * m + (1.0 - ADAM_B1) * g
    v = ADAM_B2 * v + (1.0 - ADAM_B2) * (g * g)
    m_hat = m / (1.0 - ADAM_B1 ** ADAM_STEP)
    v_hat = v / (1.0 - ADAM_B2 ** ADAM_STEP)
    delta = -ADAM_LR * (m_hat / (jnp.sqrt(v_hat) + ADAM_EPS) + ADAM_WD * w)
    return delta, m, v


def _adamw_big(grad, sib, recv, w, m, v, pos, layer, prev, name):
    L, R, C = w.shape
    tr = _rows_tile(R, C)
    has_prev = prev is not None

    def body(pos_ref, g_ref, s_ref, r0_ref, r1_ref, r2_ref, w_ref, m_ref, v_ref, *rest):
        del pos_ref
        go_ref, d_ref, mo_ref, vo_ref = rest[-4:]
        g = g_ref[...].astype(F32) + s_ref[...].astype(F32)
        g = ((g + r0_ref[...].astype(F32)) + r1_ref[...].astype(F32)) + r2_ref[...].astype(F32)
        delta, m2, v2 = _adamw(w_ref[...], g, m_ref[...], v_ref[...])
        go_ref[...] = g
        d_ref[...] = delta
        mo_ref[...] = m2
        vo_ref[...] = v2

    lay = pl.BlockSpec((None, tr, C), lambda i, p: (layer, i, 0))
    rspec = lambda j: pl.BlockSpec((None, tr, C), lambda i, p: (j, i, 0))
    mine = lambda p: 4 * p[0] + 2 * p[1] + p[2]
    if len(grad.shape) == 3:
        g_spec = pl.BlockSpec((None, tr, C), lambda i, p: (mine(p), i, 0))
    else:
        g_spec = pl.BlockSpec((tr, C), lambda i, p: (i, mine(p)))
    s_spec = pl.BlockSpec((None, tr, C), lambda i, p: (2 * p[0] + p[1], i, 0))
    in_specs = [g_spec, s_spec, rspec(0), rspec(1), rspec(2), lay, lay, lay]
    args = [pos, grad, sib, recv, recv, recv, w, m, v]
    aliases = {}
    if has_prev:
        in_specs += [_ANY] * 4
        args += list(prev)
        aliases = {9 + k: k for k in range(4)}
    return pl.pallas_call(
        body, name=name,
        grid_spec=pltpu.PrefetchScalarGridSpec(
            num_scalar_prefetch=1, grid=(R // tr,), in_specs=in_specs, out_specs=[lay] * 4),
        out_shape=[jax.ShapeDtypeStruct((L, R, C), F32)] * 4,
        input_output_aliases=aliases,
        compiler_params=_params(("parallel",)),
    )(*args)


def _sum_slots(gathered, name):
    _, R, C = gathered.shape
    tr = _rows_tile(R, C, 2 * MIB)

    def body(x_ref, o_ref):
        s = x_ref[0]
        for k in range(1, N_DEV):
            s = s + x_ref[k]
        o_ref[...] = s

    return pl.pallas_call(
        body, name=name, grid=(R // tr,),
        in_specs=[pl.BlockSpec((N_DEV, tr, C), lambda i: (0, i, 0))],
        out_specs=pl.BlockSpec((tr, C), lambda i: (i, 0)),
        out_shape=jax.ShapeDtypeStruct((R, C), F32),
        compiler_params=_params(("parallel",)),
    )(gathered)


def _adamw_small(w, g, m, v, name):
    R, C = w.shape
    tr = _rows_tile(R, C, 2 * MIB)

    def body(w_ref, g_ref, m_ref, v_ref, d_ref, mo_ref, vo_ref):
        d_ref[...], mo_ref[...], vo_ref[...] = _adamw(w_ref[...], g_ref[...], m_ref[...], v_ref[...])

    spec = pl.BlockSpec((tr, C), lambda i: (i, 0))
    return pl.pallas_call(
        body, name=name, grid=(R // tr,),
        in_specs=[spec] * 4, out_specs=[spec] * 3,
        out_shape=[jax.ShapeDtypeStruct((R, C), F32)] * 3,
        compiler_params=_params(("parallel",)),
    )(w, g, m, v)


def _pack(arrs):
    parts = []
    for a in arrs:
        flat = a.reshape(-1)
        pad = (-flat.shape[0]) % (8 * LANE)
        if pad:
            flat = jnp.pad(flat, (0, pad))
        parts.append(flat.reshape(-1, LANE))
    return jnp.concatenate(parts, axis=0)


def _unpack(packed, shapes):
    out, r = [], 0
    for s in shapes:
        n = 1
        for d in s:
            n *= d
        rows = -(-n // (8 * LANE)) * 8
        out.append(packed[r:r + rows].reshape(-1)[:n].reshape(s))
        r += rows
    return out


def kernel(x, norm1_g, w_in, w_pool, pool_scale, sgu_norm_g, sgu_w, sgu_b, conv_w, w_pool_out, w_sgu_out, w_conv_out, w_o, norm2_g, w_up, ffn_conv_w, w_down, final_g, loss_target, m_norm1_g, m_w_in, m_w_pool, m_pool_scale, m_sgu_norm_g, m_sgu_w, m_sgu_b, m_conv_w, m_w_pool_out, m_w_sgu_out, m_w_conv_out, m_w_o, m_norm2_g, m_w_up, m_ffn_conv_w, m_w_down, m_final_g, v_norm1_g, v_w_in, v_w_pool, v_pool_scale, v_sgu_norm_g, v_sgu_w, v_sgu_b, v_conv_w, v_w_pool_out, v_w_sgu_out, v_w_conv_out, v_w_o, v_norm2_g, v_w_up, v_ffn_conv_w, v_w_down, v_final_g):
    L = norm1_g.shape[0]
    x0 = x[0]
    tgt = loss_target[0]
    G, PG = w_pool.shape[1], w_pool.shape[3]
    F = w_down.shape[1] * N_DEV
    T, D = x0.shape
    tT = _pick(T, (1024, 512, 256, 128))
    tF = _pick(F, (512, 256))
    tF2 = _pick(2 * F, (1024, 512, 256))
    kF = _pick(F, (2816, 2048, 1024, 512))
    px, py, pc = _mesh_pos()
    me = 4 * px + 2 * py + pc
    pos = jnp.stack([px, py, pc]).astype(jnp.int32)

    slot = jnp.reshape(me, (1,)).astype(jnp.int32)
    ns_in = w_in.shape[2]
    nq = 3 if ns_in % (3 * 256) == 0 else 1
    chunk_cols = [(q * (ns_in // nq), ns_in // nq) for q in range(nq)]
    win0 = _cast_place(w_in, 0, slot, BF16, "place_w_in_0", by_cols=True)
    first, tok0 = _split_start([([win0],) + _copies_gather_chips(1, chunk_cols[0])], "ag_start_first")
    win0 = first[0][2][0]
    h1_first = _rms_fwd(x0, norm1_g[0][None], "rms1_fwd_0")
    stage1, group_sizes, started1 = {}, {0: 1}, {}

    def start_groups(l, extra, deps, name):
        grp = [[("w_in", w_in, BF16, True)],
               [("w_pool_out", w_pool_out, BF16, True), ("w_sgu_out", w_sgu_out, BF16, True),
                ("w_conv_out", w_conv_out, BF16, True), ("w_pool", w_pool.reshape(L, -1, PG), BF16, False),
                ("conv_w", conv_w, F32, True), ("w_o", w_o, BF16, False)],
               [("w_up", w_up, BF16, True), ("ffn_conv_w", ffn_conv_w, F32, True)],
               [("w_down", w_down, BF16, False)]]
        ks = []
        for gi, members in enumerate(grp):
            if l == 0 and gi == 0:
                continue
            k = 4 * l + gi
            bufs = [_cast_place(a, l, slot, dt, f"place_{n}_{l}", deps, bc) for n, a, dt, bc in members]
            stage1[k] = (bufs,) + _copies_gather_chips(len(members))
            group_sizes[k] = len(members)
            ks.append(k)
        st, _ = _split_start(extra + [stage1[k] for k in ks], name)
        for k, s in zip(ks, st[len(extra):]):
            started1[k] = s
        return st[:len(extra)]

    rest = start_groups(0, [([win0],) + _copies_gather_chips(1, cols) for cols in chunk_cols[1:]],
                        [tok0, h1_first], "ag_start_0")
    if nq > 1:
        win0 = rest[0][2][0]
    chunk_sems = [first[0][:2]] + [r[:2] for r in rest]
    started2 = {}

    def gather_forward(k, after):
        n = group_sizes[k]
        bufs = _split_wait(started1[k], stage1[k][1], after, f"ag_wait1_{k}")
        st, tok = _split_start([(bufs,) + _copies_gather_forward(n)], f"ag_fwd_{k}")
        started2[k] = st[0]
        return tok

    def gather_done(k, after):
        return _split_wait(started2.pop(k), _copies_gather_forward(group_sizes[k])[0], after, f"ag_wait2_{k}")

    W, saved = [], []
    xc = x0
    for l in range(L):
        w = dict(
            sguw=sgu_w[l].astype(BF16),
            bexp=jnp.broadcast_to(sgu_b[l].T[:, :, None],
                                  (sgu_b.shape[2], sgu_b.shape[1], sgu_norm_g.shape[1] // sgu_b.shape[2])),
            g1=norm1_g[l][None], g2=norm2_g[l][None], gs=sgu_norm_g[l][None], scale=pool_scale[l][None])
        ka, kb, kc, kd = 4 * l, 4 * l + 1, 4 * l + 2, 4 * l + 3
        h1 = h1_first if l == 0 else _rms_fwd(xc, w["g1"], f"rms1_fwd_{l}")
        if l == 0:
            P, after = None, h1
            for q, cols in enumerate(chunk_cols):
                (win0,) = _split_wait(chunk_sems[q] + ([win0],), _copies_gather_chips(1, cols)[0], after,
                                      f"ag_wait1_0_{q}")
                st, tok = _split_start([([win0],) + _copies_gather_forward(1, cols)], f"ag_fwd_0_{q}")
                (win0,) = _split_wait(st[0], _copies_gather_forward(1, cols)[0], after, f"ag_wait2_0_{q}")
                P = _mm(h1, win0, mode="nn", out_dtype=BF16, name=f"proj_in_0_{q}", tm=tT, tn=cols[1], tk=D,
                        cols=(q, nq), prev=P, deps=[tok])
                after = P
            w["win"] = win0
        else:
            (w["win"],) = gather_done(ka, h1)
            tok = gather_forward(kb, h1)
            P = _mm(h1, w["win"], mode="nn", out_dtype=BF16, name=f"proj_in_{l}", tm=tT, tn=1024, tk=D, deps=[tok])
        if l == 0:
            gather_forward(kb, P)
        w["wa"], w["wb"], w["wc"], wp, w["cw"], wo = gather_done(kb, P)
        w.update(wpool=wp.reshape(N_DEV, G, -1, PG).transpose(1, 0, 2, 3).reshape(G, PG, PG),
                 wo=wo.reshape(-1, wo.shape[2]))
        ya = _pool_fwd(P, w["wpool"], w["scale"], f"pool_fwd_{l}")
        yb = _sgu_fwd(P, w["sguw"], w["bexp"], w["gs"], f"sgu_fwd_{l}")
        yc = _conv_fwd(P, w["cw"], f"conv_fwd_{l}")
        oa, ob, oc, M = _merge_fwd(P, (ya, yb, yc), (w["wa"], w["wb"], w["wc"]), f"merge_fwd_{l}")
        tok = gather_forward(kc, M)
        if l == 0:
            for l2 in range(1, L):
                start_groups(l2, [], [tok], f"ag_start_{l2}")
        x1 = _mm(M, w["wo"], mode="nn", out_dtype=F32, name=f"proj_o_{l}", tm=tT, tn=1024, tk=D, res=xc, deps=[tok])
        h2 = _rms_fwd(x1, w["g2"], f"rms2_fwd_{l}")
        w["wup"], w["fw"] = gather_done(kc, h2)
        tok = gather_forward(kd, h2)
        U = _mm(h2, w["wup"], mode="nn", out_dtype=BF16, name=f"proj_up_{l}", tm=tT, tn=tF2, tk=D, deps=[tok])
        act = _glu_fwd(U, w["fw"], f"glu_fwd_{l}")
        (wdown,) = gather_done(kd, act)
        w["wdown"] = wdown.reshape(-1, wdown.shape[2])
        deps = [gather_forward(4 * (l + 1), act)] if l + 1 < L else []
        x2 = _mm(act, w["wdown"], mode="nn", out_dtype=F32, name=f"proj_down_{l}", tm=tT, tn=512, tk=F, res=x1,
                 deps=deps)
        if deps:
            tok = deps[0]
        W.append(w)
        saved.append(dict(x0=xc, h1=h1, P=P, ya=ya, yb=yb, yc=yc, oa=oa, ob=ob, oc=oc, M=M, x1=x1, h2=h2, U=U, act=act))
        xc = x2

    dx, dxb, d_final_g, loss_part = _loss_head(xc, tgt, final_g[None], "loss_head")
    loss = lax.psum(loss_part[0, 0], ("x", "y", "c"))

    wmv = dict(w_in=(w_in, m_w_in, v_w_in), w_up=(w_up, m_w_up, v_w_up), w_o=(w_o, m_w_o, v_w_o),
               w_down=(w_down, m_w_down, v_w_down), w_pool_out=(w_pool_out, m_w_pool_out, v_w_pool_out),
               w_sgu_out=(w_sgu_out, m_w_sgu_out, v_w_sgu_out), w_conv_out=(w_conv_out, m_w_conv_out, v_w_conv_out),
               w_pool=(w_pool, m_w_pool, v_w_pool))
    adam_out = {}

    def rs_begin(tag, layer, names, grads):
        n = len(grads)
        slice_shape = lambda g: g.shape[1:] if len(g.shape) == 3 else (g.shape[0], g.shape[1] // N_DEV)
        lands = [lax.empty((4,) + slice_shape(g), BF16) for g in grads]
        st, tok = _split_start([(list(grads) + lands,) + _copies_rs_sibling(n)], f"rs_sib_{tag}")
        return dict(tag=tag, layer=layer, names=names, n=n, st=st[0]), tok

    def rs_to_chips(state, after):
        n, tag = state["n"], state["tag"]
        bufs = _split_wait(state["st"], _copies_rs_sibling(n)[0], after, f"rs_sibw_{tag}")
        grads, sib = bufs[:n], bufs[n:]
        pairs = [_pair_sum(g, r, pos, f"pair_{tag}_{i}") for i, (g, r) in enumerate(zip(grads, sib))]
        lands = [lax.empty(p.shape, BF16) for p in pairs]
        st, tok = _split_start([(pairs + lands,) + _copies_rs_chips(n)], f"rs_chips_{tag}")
        state.update(st=st[0], grads=grads, sib=sib)
        return tok

    def rs_finish(state, after):
        n, tag, layer = state["n"], state["tag"], state["layer"]
        bufs = _split_wait(state["st"], _copies_rs_chips(n)[0], after, f"rs_chipsw_{tag}")
        for name, g, sb, recv in zip(state["names"], state["grads"], state["sib"], bufs[n:]):
            wk, mk, vk = (a.reshape(L, -1, a.shape[-1]) for a in wmv[name])
            adam_out[name] = _adamw_big(g, sb, recv, wk, mk, vk, pos, layer, adam_out.get(name),
                                        f"adamw_{name}_{layer}")

    small = [None] * L
    pending = []
    deps = []
    for l in reversed(range(L)):
        w, s = W[l], saved[l]
        dact = _mm(dxb, w["wdown"], mode="nt", out_dtype=BF16, name=f"d_act_{l}", tm=tT, tn=tF, tk=D, deps=deps)
        g_down = _mm(s["act"], dxb, mode="tn", out_dtype=BF16, name=f"g_down_{l}", tm=tF, tn=1024, tk=T)
        dug, duv, dfwg, dfwv = _glu_bwd(s["U"], dact, w["fw"], f"glu_bwd_{l}")
        dh2 = _mm(dug, w["wup"], mode="nt", out_dtype=F32, name=f"d_h2a_{l}", tm=tT, tn=1024, tk=kF)
        dh2 = _mm(duv, w["wup"], mode="nt", out_dtype=F32, name=f"d_h2b_{l}", tm=tT, tn=1024, tk=kF, b_koff=F // kF,
                  res=dh2)
        g_up = _mm(s["h2"], dug, mode="tn", out_dtype=BF16, name=f"g_upa_{l}", tm=1024, tn=tF, tk=T, out_n=2 * F)
        g_up = _mm(s["h2"], duv, mode="tn", out_dtype=BF16, name=f"g_upb_{l}", tm=1024, tn=tF, tk=T, out_n=2 * F,
                   out_off=F // tF, prev=g_up)
        ra, tok = rs_begin(f"a{l}", l, ["w_down", "w_up"], [g_down.reshape(N_DEV, -1, g_down.shape[1]), g_up])
        if pending:
            rs_finish(pending.pop(0), g_up)
        dx1, dx1b, dg2 = _rms_bwd(dh2, s["x1"], w["g2"], dx, f"rms2_bwd_{l}")
        dM = _mm(dx1b, w["wo"], mode="nt", out_dtype=BF16, name=f"d_m_{l}", tm=tT, tn=1024, tk=D, deps=[tok])
        g_o = _mm(s["M"], dx1b, mode="tn", out_dtype=BF16, name=f"g_o_{l}", tm=1024, tn=1024, tk=T)
        doa, dob, doc, dPg = _merge_bwd(s["P"], dM, (s["oa"], s["ob"], s["oc"]), f"merge_bwd_{l}")
        tok = rs_to_chips(ra, dPg)
        if pending:
            rs_finish(pending.pop(0), dPg)
        dya = _mm(doa, w["wa"], mode="nt", out_dtype=BF16, name=f"d_ya_{l}", tm=tT, tn=1024, tk=D, deps=[tok])
        dyb = _mm(dob, w["wb"], mode="nt", out_dtype=BF16, name=f"d_yb_{l}", tm=tT, tn=1024, tk=D)
        dyc = _mm(doc, w["wc"], mode="nt", out_dtype=BF16, name=f"d_yc_{l}", tm=tT, tn=1024, tk=D)
        g_a = _mm(s["ya"], doa, mode="tn", out_dtype=BF16, name=f"g_a_{l}", tm=1024, tn=1024, tk=T)
        g_b = _mm(s["yb"], dob, mode="tn", out_dtype=BF16, name=f"g_b_{l}", tm=1024, tn=1024, tk=T)
        g_c = _mm(s["yc"], doc, mode="tn", out_dtype=BF16, name=f"g_c_{l}", tm=1024, tn=1024, tk=T)
        da, g_pool, dscale = _pool_bwd(s["P"], dya, w["wpool"], w["scale"], f"pool_bwd_{l}")
        g_pool_s = g_pool.reshape(G, N_DEV, -1, PG).transpose(1, 0, 2, 3).reshape(N_DEV, -1, PG).astype(BF16)
        rb, tok = rs_begin(f"b{l}", l, ["w_o", "w_pool_out", "w_sgu_out", "w_conv_out", "w_pool"],
                           [g_o.reshape(N_DEV, -1, g_o.shape[1]), g_a, g_b, g_c, g_pool_s])
        if pending:
            rs_finish(pending.pop(0), da)
        dPl, g_sguw, db_exp, dgs = _sgu_bwd(s["P"], dyb, da, w["sguw"], w["bexp"], w["gs"], f"sgu_bwd_{l}")
        dPl, dcw = _conv_bwd(s["P"], dyc, w["cw"], dPl, f"conv_bwd_{l}")
        tok2 = rs_to_chips(rb, dPl)
        half = dPl.shape[1]
        g_in = _mm(s["h1"], dPl, mode="tn", out_dtype=BF16, name=f"g_ina_{l}", tm=1024, tn=1024, tk=T, out_n=2 * half,
                   deps=[tok, tok2])
        g_in = _mm(s["h1"], dPg, mode="tn", out_dtype=BF16, name=f"g_inb_{l}", tm=1024, tn=1024, tk=T, out_n=2 * half,
                   out_off=half // 1024, prev=g_in)
        rc, tok = rs_begin(f"c{l}", l, ["w_in"], [g_in])
        dh1 = _mm(dPl, w["win"], mode="nt", out_dtype=F32, name=f"d_h1a_{l}", tm=tT, tn=1024, tk=D, deps=[tok])
        tok = rs_to_chips(rc, dh1)
        dh1 = _mm(dPg, w["win"], mode="nt", out_dtype=F32, name=f"d_h1b_{l}", tm=tT, tn=1024, tk=D, b_koff=half // D,
                  res=dh1, deps=[tok])
        dx, dxb, dg1 = _rms_bwd(dh1, s["x0"], w["g1"], dx1, f"rms1_bwd_{l}")
        deps = []
        pending = [ra, rb, rc]
        small[l] = dict(norm1_g=dg1[0], pool_scale=dscale[0], sgu_norm_g=dgs[0], sgu_w=g_sguw,
                        sgu_b=db_exp[:, :, 0].T, conv_w=dcw[:3], norm2_g=dg2[0],
                        ffn_conv_w=jnp.concatenate([dfwg[:3], dfwv[:3]], axis=1))
    grad_x = dx[None]

    snames = ["norm1_g", "pool_scale", "sgu_norm_g", "sgu_w", "sgu_b", "conv_w", "norm2_g", "ffn_conv_w"]
    sparts = [jnp.stack([small[l][n] for l in range(L)]) for n in snames] + [d_final_g[0]]
    snames = snames + ["final_g"]
    packed = _pack(sparts)
    sbuf = _cast_place(packed[None], 0, slot, F32, "place_small")
    st, tok = _split_start([([sbuf],) + _copies_gather_chips(1)], "ag_small_start")
    rs_finish(pending[0], tok)
    rs_finish(pending[1], tok)
    bufs = _split_wait(st[0], _copies_gather_chips(1)[0], [o[0] for o in adam_out.values()], "ag_small_wait1")
    st, tok = _split_start([(bufs,) + _copies_gather_forward(1)], "ag_small_fwd")
    (gathered_small,) = _split_wait(st[0], _copies_gather_forward(1)[0], tok, "ag_small_wait2")
    total = _sum_slots(gathered_small, "sum_small")
    sgrads = dict(zip(snames, _unpack(total, [p.shape for p in sparts])))
    for n, width in (("conv_w", conv_w.shape[2]), ("ffn_conv_w", ffn_conv_w.shape[2])):
        sgrads[n] = lax.dynamic_slice_in_dim(sgrads[n], me * width, width, axis=2)
    sw = dict(norm1_g=(norm1_g, m_norm1_g, v_norm1_g), pool_scale=(pool_scale, m_pool_scale, v_pool_scale),
              sgu_norm_g=(sgu_norm_g, m_sgu_norm_g, v_sgu_norm_g), sgu_w=(sgu_w, m_sgu_w, v_sgu_w),
              sgu_b=(sgu_b, m_sgu_b, v_sgu_b), conv_w=(conv_w, m_conv_w, v_conv_w),
              norm2_g=(norm2_g, m_norm2_g, v_norm2_g), ffn_conv_w=(ffn_conv_w, m_ffn_conv_w, v_ffn_conv_w),
              final_g=(final_g, m_final_g, v_final_g))
    shapes = [sw[n][0].shape for n in snames]
    upd = _adamw_small(_pack([sw[n][0] for n in snames]), _pack([sgrads[n] for n in snames]),
                       _pack([sw[n][1] for n in snames]), _pack([sw[n][2] for n in snames]), "adamw_small")
    sdelta, sm, sv = (dict(zip(snames, _unpack(u, shapes))) for u in upd)
    res = {n: [sgrads[n], sdelta[n], sm[n], sv[n]] for n in snames}

    rs_finish(pending[2], [upd[0]] + [o[0] for o in adam_out.values()])
    for n, outs in adam_out.items():
        res[n] = [o.reshape(wmv[n][0].shape) for o in outs]

    order = ["norm1_g", "w_in", "w_pool", "pool_scale", "sgu_norm_g", "sgu_w", "sgu_b", "conv_w", "w_pool_out",
             "w_sgu_out", "w_conv_out", "w_o", "norm2_g", "w_up", "ffn_conv_w", "w_down", "final_g"]
    return (loss, grad_x) + tuple(res[n][k] for k in range(4) for n in order)
```

```python
import functools

import jax
import jax.numpy as jnp
from jax import lax
from jax.experimental import pallas as pl
from jax.experimental.pallas import tpu as pltpu

BF16 = jnp.bfloat16
F32 = jnp.float32
EPS = 1e-6
MIB = 1024 * 1024
V7X_VMEM_BYTES = 64 * MIB
VMEM_LIMIT = 56 * MIB
LANE = 128
N_DEV = 8


def _params(sem, **kw):
    return pltpu.CompilerParams(dimension_semantics=sem, vmem_limit_bytes=VMEM_LIMIT, **kw)


def _pick(n, cands):
    for c in cands:
        if n % c == 0:
            return c
    return n


_DIMS = {"nn": (((1,), (0,)), ((), ())), "nt": (((1,), (1,)), ((), ())), "tn": (((0,), (0,)), ((), ()))}


def _mm(a, b, *, mode, out_dtype, name, tm, tn, tk, res=None, prev=None, deps=(), cols=(0, 1), b_koff=0,
        out_off=0, out_n=None):
    if mode == "nn":
        M, Kc = a.shape
        nj = (b.shape[1] // tn - cols[0] + cols[1] - 1) // cols[1]
        out_w = b.shape[1]
    elif mode == "nt":
        M, Kc = a.shape
        nj = b.shape[0] // tn
        out_w = b.shape[0]
    else:
        Kc, M = a.shape
        nj = b.shape[1] // tn
        out_w = out_n or b.shape[1]
    nk = Kc // tk
    assert M % tm == 0 and Kc % tk == 0, (name, M, Kc, tm, tk)
    off, stride = cols
    if mode == "nn":
        a_spec = pl.BlockSpec((tm, tk), lambda i, j, k: (i, k))
        b_spec = pl.BlockSpec((tk, tn), lambda i, j, k: (k, off + j * stride))
        o_spec = pl.BlockSpec((tm, tn), lambda i, j, k: (i, off + j * stride))
    elif mode == "nt":
        a_spec = pl.BlockSpec((tm, tk), lambda i, j, k: (i, k))
        b_spec = pl.BlockSpec((tn, tk), lambda i, j, k: (j, b_koff + k))
        o_spec = pl.BlockSpec((tm, tn), lambda i, j, k: (i, j))
    else:
        a_spec = pl.BlockSpec((tk, tm), lambda i, j, k: (k, i))
        b_spec = pl.BlockSpec((tk, tn), lambda i, j, k: (k, j))
        o_spec = pl.BlockSpec((tm, tn), lambda i, j, k: (i, out_off + j))
    dims = _DIMS[mode]
    has_res = res is not None
    has_prev = prev is not None
    n_in = 2 + has_res + has_prev + len(deps)

    def body(*refs):
        a_ref, b_ref = refs[0], refs[1]
        o_ref = refs[n_in]

        def finish(acc):
            if has_res:
                acc = acc + refs[2][...]
            o_ref[...] = acc.astype(o_ref.dtype)

        if nk == 1:
            finish(lax.dot_general(a_ref[...], b_ref[...], dims, preferred_element_type=F32))
        else:
            acc_ref = refs[n_in + 1]
            k = pl.program_id(2)

            @pl.when(k == 0)
            def _():
                acc_ref[...] = jnp.zeros(acc_ref.shape, F32)

            acc_ref[...] += lax.dot_general(a_ref[...], b_ref[...], dims, preferred_element_type=F32)

            @pl.when(k == nk - 1)
            def _():
                finish(acc_ref[...])

    in_specs = [a_spec, b_spec]
    args = [a, b]
    if has_res:
        in_specs.append(pl.BlockSpec((tm, tn), lambda i, j, k: (i, j)))
        args.append(res)
    aliases = {}
    if has_prev:
        aliases = {len(args): 0}
        in_specs.append(pl.BlockSpec(memory_space=pl.ANY))
        args.append(prev)
    for d in deps:
        in_specs.append(pl.BlockSpec(memory_space=pl.ANY))
        args.append(d)
    return pl.pallas_call(
        body,
        name=name,
        grid=(M // tm, nj, nk),
        in_specs=in_specs,
        out_specs=o_spec,
        out_shape=jax.ShapeDtypeStruct((M, out_w), out_dtype),
        scratch_shapes=[pltpu.VMEM((tm, tn), F32)] if nk > 1 else [],
        input_output_aliases=aliases,
        compiler_params=_params(("parallel", "parallel", "arbitrary")),
    )(*args)


def _mm_res_rms(a, b, res, g, name, deps=()):
    M, K = a.shape
    N = b.shape[1]
    tm = _pick(M, (512, 256, 128))

    def body(a_ref, b_ref, res_ref, g_ref, *rest):
        x_ref, h_ref = rest[-2:]
        xv = res_ref[...] + jnp.dot(a_ref[...], b_ref[...], preferred_element_type=F32)
        x_ref[...] = xv
        r = lax.rsqrt(jnp.mean(xv * xv, axis=-1, keepdims=True) + EPS)
        h_ref[...] = (xv * r * g_ref[...]).astype(BF16)

    row = pl.BlockSpec((tm, N), lambda i: (i, 0))
    return pl.pallas_call(
        body, name=name, grid=(M // tm,),
        in_specs=[pl.BlockSpec((tm, K), lambda i: (i, 0)), pl.BlockSpec((K, N), lambda i: (0, 0)), row,
                  pl.BlockSpec((1, N), lambda i: (0, 0))] + [pl.BlockSpec(memory_space=pl.ANY)] * len(deps),
        out_specs=[row, row],
        out_shape=[jax.ShapeDtypeStruct((M, N), F32), jax.ShapeDtypeStruct((M, N), BF16)],
        compiler_params=_params(("parallel",)),
    )(a, b, res, g, *deps)


HALO = 16
CH = 256


def _fill_pad(pad_ref, chunk_fn, T):
    z = jnp.zeros((HALO, pad_ref.shape[1]), F32)
    pad_ref[pl.ds(0, HALO), :] = z
    pad_ref[pl.ds(HALO + T, HALO), :] = z

    def body(c, carry):
        r0 = pl.multiple_of(c * CH, CH)
        pad_ref[pl.ds(HALO + r0, CH), :] = chunk_fn(r0)
        return carry

    lax.fori_loop(0, T // CH, body, 0)


def _ext(pad_ref, r0):
    return pad_ref[pl.ds(r0, CH + 2 * HALO), :]


def _tap(ext, o):
    if o == 0:
        return ext[HALO:HALO + CH]
    return pltpu.roll(ext, (-o) % ext.shape[0], axis=0)[HALO:HALO + CH]


def _chunks(T, fn, init=0):
    def body(c, carry):
        return fn(pl.multiple_of(c * CH, CH), carry)

    return lax.fori_loop(0, T // CH, body, init)


def _conv3(ext, w):
    return _tap(ext, -1) * w[0:1] + _tap(ext, 0) * w[1:2] + _tap(ext, 1) * w[2:3]


def _conv3_t(ext, w):
    return _tap(ext, 1) * w[0:1] + _tap(ext, 0) * w[1:2] + _tap(ext, -1) * w[2:3]


def _conv3_dw(xext, d):
    return [jnp.sum(_tap(xext, k - 1) * d, axis=0, keepdims=True) for k in range(3)]


def _sigmoid(x):
    return 0.5 * (jnp.tanh(0.5 * x) + 1.0)


def _silu(x):
    h = 0.5 * x
    return h * (jnp.tanh(h) + 1.0)


_GELU_C = 0.7978845608028654


def _gelu(x):
    return 0.5 * x * (1.0 + jnp.tanh(_GELU_C * (x + 0.044715 * (x * x * x))))


def _gelu_grad(x):
    t = jnp.tanh(_GELU_C * (x + 0.044715 * (x * x * x)))
    return 0.5 * (1.0 + t) + 0.5 * x * (1.0 - t * t) * (_GELU_C * (1.0 + 3.0 * 0.044715 * (x * x)))


def _row_tile(T):
    return _pick(T, (256, 128))


def _rms_fwd(x, g, name):
    T, D = x.shape
    tm = _row_tile(T)

    def body(x_ref, g_ref, h_ref):
        xv = x_ref[...]
        r = lax.rsqrt(jnp.mean(xv * xv, axis=-1, keepdims=True) + EPS)
        h_ref[...] = (xv * r * g_ref[...]).astype(BF16)

    return pl.pallas_call(
        body, name=name, grid=(T // tm,),
        in_specs=[pl.BlockSpec((tm, D), lambda i: (i, 0)), pl.BlockSpec((1, D), lambda i: (0, 0))],
        out_specs=pl.BlockSpec((tm, D), lambda i: (i, 0)),
        out_shape=jax.ShapeDtypeStruct((T, D), BF16),
        compiler_params=_params(("parallel",)),
    )(x, g)


def _rms_bwd(dh, x, g, dres, name, deps=()):
    T, D = x.shape
    tm = _row_tile(T)

    def body(dh_ref, x_ref, g_ref, dres_ref, *rest):
        dx_ref, dxb_ref, dg_ref = rest[len(deps):]
        i = pl.program_id(0)
        xv = x_ref[...]
        r = lax.rsqrt(jnp.mean(xv * xv, axis=-1, keepdims=True) + EPS)
        n = xv * r
        dh_v = dh_ref[...].astype(F32)
        dn = dh_v * g_ref[...]
        dx = dres_ref[...] + r * (dn - n * jnp.mean(dn * n, axis=-1, keepdims=True))
        dx_ref[...] = dx
        dxb_ref[...] = dx.astype(BF16)
        dg = jnp.sum(dh_v * n, axis=0, keepdims=True)

        @pl.when(i == 0)
        def _():
            dg_ref[...] = dg

        @pl.when(i > 0)
        def _():
            dg_ref[...] += dg

    row = pl.BlockSpec((tm, D), lambda i: (i, 0))
    vec = pl.BlockSpec((1, D), lambda i: (0, 0))
    return pl.pallas_call(
        body, name=name, grid=(T // tm,),
        in_specs=[row, row, vec, row] + [pl.BlockSpec(memory_space=pl.ANY)] * len(deps),
        out_specs=[row, row, vec],
        out_shape=[jax.ShapeDtypeStruct((T, D), F32), jax.ShapeDtypeStruct((T, D), BF16),
                   jax.ShapeDtypeStruct((1, D), F32)],
        compiler_params=_params(("arbitrary",)),
    )(dh, x, g, dres, *deps)


def _loss_head(x, tgt, g, name):
    T, D = x.shape
    tm = _row_tile(T)

    def body(x_ref, t_ref, g_ref, dx_ref, dxb_ref, dg_ref, l_ref):
        i = pl.program_id(0)
        xv = x_ref[...]
        gv = g_ref[...]
        r = lax.rsqrt(jnp.mean(xv * xv, axis=-1, keepdims=True) + EPS)
        n = xv * r
        e = n * gv - t_ref[...]
        dy = e * (1.0 / D)
        dn = dy * gv
        dx = r * (dn - n * jnp.mean(dn * n, axis=-1, keepdims=True))
        dx_ref[...] = dx
        dxb_ref[...] = dx.astype(BF16)
        dg = jnp.sum(dy * n, axis=0, keepdims=True)
        per_tok = jnp.mean(e * e, axis=-1, keepdims=True)
        lv = jnp.broadcast_to(0.5 * jnp.sum(per_tok, axis=0, keepdims=True), (1, LANE))

        @pl.when(i == 0)
        def _():
            dg_ref[...] = dg
            l_ref[...] = lv

        @pl.when(i > 0)
        def _():
            dg_ref[...] += dg
            l_ref[...] += lv

    row = pl.BlockSpec((tm, D), lambda i: (i, 0))
    vec = pl.BlockSpec((1, D), lambda i: (0, 0))
    return pl.pallas_call(
        body, name=name, grid=(T // tm,),
        in_specs=[row, row, vec],
        out_specs=[row, row, vec, pl.BlockSpec((1, LANE), lambda i: (0, 0))],
        out_shape=[jax.ShapeDtypeStruct((T, D), F32), jax.ShapeDtypeStruct((T, D), BF16),
                   jax.ShapeDtypeStruct((1, D), F32), jax.ShapeDtypeStruct((1, LANE), F32)],
        compiler_params=_params(("arbitrary",)),
    )(x, tgt, g)


POOL_WINDOWS = (2, 4, 8, 16)


def _pool_offsets(w):
    return range(-(w // 2), w - w // 2)


def _pool_cnt(r0, w, T, shape):
    t = r0 + lax.broadcasted_iota(jnp.int32, shape, 0)
    lo = jnp.maximum(t - w // 2, 0)
    hi = jnp.minimum(t + (w - w // 2 - 1), T - 1)
    return (hi - lo + 1).astype(F32)


def _pooled(pad_ref, r0, w, T):
    ext = _ext(pad_ref, r0)
    s = None
    for o in _pool_offsets(w):
        tap = _tap(ext, o)
        s = tap if s is None else s + tap
    cur = ext[HALO:HALO + CH]
    return s / _pool_cnt(r0, w, T, cur.shape) - cur


def _pool_fwd(P, w_pool, scale, name):
    T = P.shape[0]
    G, PG, _ = w_pool.shape

    def body(a_ref, w_ref, s_ref, o_ref, pad_ref):
        g = pl.program_id(0)
        _fill_pad(pad_ref, lambda r0: a_ref[pl.ds(r0, CH), :].astype(F32), T)
        for gi, w in enumerate(POOL_WINDOWS):
            @pl.when(g == gi)
            def _(w=w):
                def chunk(r0, carry):
                    pooled = _pooled(pad_ref, r0, w, T).astype(BF16)
                    y = jnp.dot(pooled, w_ref[...], preferred_element_type=F32) * s_ref[...]
                    o_ref[pl.ds(r0, CH), :] = y.astype(BF16)
                    return carry

                _chunks(T, chunk)

    return pl.pallas_call(
        body, name=name, grid=(G,),
        in_specs=[pl.BlockSpec((T, PG), lambda g: (0, g)),
                  pl.BlockSpec((None, PG, PG), lambda g: (g, 0, 0)),
                  pl.BlockSpec((1, PG), lambda g: (0, g))],
        out_specs=pl.BlockSpec((T, PG), lambda g: (0, g)),
        out_shape=jax.ShapeDtypeStruct((T, G * PG), BF16),
        scratch_shapes=[pltpu.VMEM((T + 2 * HALO, PG), F32)],
        compiler_params=_params(("parallel",)),
    )(P, w_pool, scale)


def _pool_bwd(P, dy, w_pool, scale, name):
    T = P.shape[0]
    G, PG, _ = w_pool.shape

    def body(a_ref, dy_ref, w_ref, s_ref, da_ref, dw_ref, ds_ref, pad_ref, gpad_ref):
        g = pl.program_id(0)
        gpad_ref[pl.ds(0, HALO), :] = jnp.zeros((HALO, PG), F32)
        gpad_ref[pl.ds(HALO + T, HALO), :] = jnp.zeros((HALO, PG), F32)
        _fill_pad(pad_ref, lambda r0: a_ref[pl.ds(r0, CH), :].astype(F32), T)
        for gi, w in enumerate(POOL_WINDOWS):
            @pl.when(g == gi)
            def _(w=w):
                def chunk1(r0, carry):
                    dw, ds = carry
                    pooled = _pooled(pad_ref, r0, w, T).astype(BF16)
                    ypre = jnp.dot(pooled, w_ref[...], preferred_element_type=F32)
                    dyv = dy_ref[pl.ds(r0, CH), :].astype(F32)
                    ds = ds + jnp.sum(dyv * ypre, axis=0, keepdims=True)
                    dyp = (dyv * s_ref[...]).astype(BF16)
                    dw = dw + lax.dot_general(pooled, dyp, _DIMS["tn"], preferred_element_type=F32)
                    dpool = lax.dot_general(dyp, w_ref[...], _DIMS["nt"], preferred_element_type=F32)
                    gpad_ref[pl.ds(HALO + r0, CH), :] = dpool / _pool_cnt(r0, w, T, dpool.shape)
                    return dw, ds

                dw, ds = _chunks(T, chunk1, (jnp.zeros((PG, PG), F32), jnp.zeros((1, PG), F32)))
                dw_ref[...] = dw
                ds_ref[...] = ds

                def chunk2(r0, carry):
                    ext = _ext(gpad_ref, r0)
                    cur = ext[HALO:HALO + CH]
                    acc = None
                    for o in _pool_offsets(w):
                        tap = _tap(ext, -o)
                        acc = tap if acc is None else acc + tap
                    da_ref[pl.ds(r0, CH), :] = (acc - cur * _pool_cnt(r0, w, T, cur.shape)).astype(BF16)
                    return carry

                _chunks(T, chunk2)

    col = pl.BlockSpec((T, PG), lambda g: (0, g))
    return pl.pallas_call(
        body, name=name, grid=(G,),
        in_specs=[col, col, pl.BlockSpec((None, PG, PG), lambda g: (g, 0, 0)), pl.BlockSpec((1, PG), lambda g: (0, g))],
        out_specs=[col, pl.BlockSpec((None, PG, PG), lambda g: (g, 0, 0)), pl.BlockSpec((1, PG), lambda g: (0, g))],
        out_shape=[jax.ShapeDtypeStruct((T, G * PG), BF16), jax.ShapeDtypeStruct((G, PG, PG), F32),
                   jax.ShapeDtypeStruct((1, G * PG), F32)],
        scratch_shapes=[pltpu.VMEM((T + 2 * HALO, PG), F32), pltpu.VMEM((T + 2 * HALO, PG), F32)],
        compiler_params=_params(("parallel",)),
    )(P, dy, w_pool, scale)


SGU_CHUNK = 128


def _sgu_common(u_ref, v_ref, gs_ref):
    up = u_ref[...].astype(F32)
    vp = v_ref[...].astype(F32)
    gv = _gelu(vp)
    rv = lax.rsqrt(jnp.mean(gv * gv, axis=-1, keepdims=True) + EPS)
    nrm = gv * rv
    return up, vp, nrm, rv, (nrm * gs_ref[...]).astype(BF16)


def _sgu_fwd(P, sgu_w, b_exp, gs, name):
    T = P.shape[0]
    G, _, SG = b_exp.shape
    DP = G * SG
    tm = _row_tile(T)

    def body(u_ref, v_ref, w_ref, b_ref, gs_ref, o_ref):
        up, _, _, _, vn = _sgu_common(u_ref, v_ref, gs_ref)
        gu = _gelu(up)
        for n in range(tm // SGU_CHUNK):
            rows = slice(n * SGU_CHUNK, (n + 1) * SGU_CHUNK)
            for g in range(G):
                cols = slice(g * SG, (g + 1) * SG)
                z = jnp.dot(w_ref[g], vn[rows, cols], preferred_element_type=F32) + b_ref[g]
                o_ref[rows, cols] = (gu[rows, cols] * z).astype(BF16)

    return pl.pallas_call(
        body, name=name, grid=(T // tm,),
        in_specs=[pl.BlockSpec((tm, DP), lambda i: (i, 1)), pl.BlockSpec((tm, DP), lambda i: (i, 2)),
                  pl.BlockSpec(sgu_w.shape, lambda i: (0, 0, 0)), pl.BlockSpec(b_exp.shape, lambda i: (0, 0, 0)),
                  pl.BlockSpec((1, DP), lambda i: (0, 0))],
        out_specs=pl.BlockSpec((tm, DP), lambda i: (i, 0)),
        out_shape=jax.ShapeDtypeStruct((T, DP), BF16),
        compiler_params=_params(("parallel",)),
    )(P, P, sgu_w, b_exp, gs)


def _sgu_bwd(P, dy, da, sgu_w, b_exp, gs, name):
    T = P.shape[0]
    G, _, SG = b_exp.shape
    DP = G * SG
    tm = _row_tile(T)

    def body(u_ref, v_ref, dy_ref, da_ref, w_ref, b_ref, gs_ref, o_ref, dw_ref, db_ref, dgs_ref, dzs_ref):
        i = pl.program_id(0)
        up, vp, nrm, rv, vn = _sgu_common(u_ref, v_ref, gs_ref)
        gu = _gelu(up)
        dyv = dy_ref[...].astype(F32)
        o_ref[:, 0:DP] = da_ref[...]

        @pl.when(i == 0)
        def _():
            dw_ref[...] = jnp.zeros(dw_ref.shape, F32)
            dzs_ref[...] = jnp.zeros(dzs_ref.shape, F32)
            dgs_ref[...] = jnp.zeros(dgs_ref.shape, F32)

        dgs = jnp.zeros((1, DP), F32)
        for n in range(tm // SGU_CHUNK):
            rows = slice(n * SGU_CHUNK, (n + 1) * SGU_CHUNK)
            dvn_parts = []
            for g in range(G):
                cols = slice(g * SG, (g + 1) * SG)
                vng = vn[rows, cols]
                z = jnp.dot(w_ref[g], vng, preferred_element_type=F32) + b_ref[g]
                dyg = dyv[rows, cols]
                du = dyg * z
                o_ref[rows, DP + g * SG:DP + (g + 1) * SG] = (du * _gelu_grad(up[rows, cols])).astype(BF16)
                dz = dyg * gu[rows, cols]
                dzb = dz.astype(BF16)
                dzs_ref[g] += dz
                dw_ref[g] += lax.dot_general(dzb, vng, _DIMS["nt"], preferred_element_type=F32)
                dvn_parts.append(lax.dot_general(w_ref[g], dzb, _DIMS["tn"], preferred_element_type=F32))
            dvn = jnp.concatenate(dvn_parts, axis=1)
            nr = nrm[rows]
            dgs = dgs + jnp.sum(dvn * nr, axis=0, keepdims=True)
            dn = dvn * gs_ref[...]
            dgv = rv[rows] * (dn - nr * jnp.mean(dn * nr, axis=-1, keepdims=True))
            o_ref[rows, 2 * DP:3 * DP] = (dgv * _gelu_grad(vp[rows])).astype(BF16)
        dgs_ref[...] += dgs

        @pl.when(i == T // tm - 1)
        def _():
            for g in range(G):
                db_ref[g] = jnp.broadcast_to(jnp.sum(dzs_ref[g], axis=1, keepdims=True), (SGU_CHUNK, SG))

    full3 = lambda a: pl.BlockSpec(a.shape, lambda i: (0, 0, 0))
    return pl.pallas_call(
        body, name=name, grid=(T // tm,),
        in_specs=[pl.BlockSpec((tm, DP), lambda i: (i, 1)), pl.BlockSpec((tm, DP), lambda i: (i, 2)),
                  pl.BlockSpec((tm, DP), lambda i: (i, 0)), pl.BlockSpec((tm, DP), lambda i: (i, 0)),
                  full3(sgu_w), full3(b_exp), pl.BlockSpec((1, DP), lambda i: (0, 0))],
        out_specs=[pl.BlockSpec((tm, 3 * DP), lambda i: (i, 0)), full3(sgu_w), full3(b_exp),
                   pl.BlockSpec((1, DP), lambda i: (0, 0))],
        out_shape=[jax.ShapeDtypeStruct((T, 6 * DP), BF16), jax.ShapeDtypeStruct(sgu_w.shape, F32),
                   jax.ShapeDtypeStruct(b_exp.shape, F32), jax.ShapeDtypeStruct((1, DP), F32)],
        scratch_shapes=[pltpu.VMEM(b_exp.shape, F32)],
        compiler_params=_params(("arbitrary",)),
    )(P, P, dy, da, sgu_w, b_exp, gs)


def _conv_fwd(P, cw, name):
    T = P.shape[0]
    DP = cw.shape[1]
    tc = 256
    nb = DP // tc

    def body(xc_ref, bg_ref, cg_ref, w_ref, o_ref, pad_ref):
        _fill_pad(pad_ref, lambda r0: cg_ref[pl.ds(r0, CH), :].astype(F32) * xc_ref[pl.ds(r0, CH), :].astype(F32), T)
        w = w_ref[...]

        def chunk(r0, carry):
            cq = _conv3(_ext(pad_ref, r0), w)
            o_ref[pl.ds(r0, CH), :] = (bg_ref[pl.ds(r0, CH), :].astype(F32) * cq).astype(BF16)
            return carry

        _chunks(T, chunk)

    col = lambda off: pl.BlockSpec((T, tc), lambda j: (0, off * nb + j))
    return pl.pallas_call(
        body, name=name, grid=(nb,),
        in_specs=[col(3), col(4), col(5), pl.BlockSpec((3, tc), lambda j: (0, j))],
        out_specs=pl.BlockSpec((T, tc), lambda j: (0, j)),
        out_shape=jax.ShapeDtypeStruct((T, DP), BF16),
        scratch_shapes=[pltpu.VMEM((T + 2 * HALO, tc), F32)],
        compiler_params=_params(("parallel",)),
    )(P, P, P, cw)


def _conv_bwd(P, dy, cw, dPl, name):
    T = P.shape[0]
    DP = cw.shape[1]
    tc = 256
    nb = DP // tc

    def body(xc_ref, bg_ref, cg_ref, dy_ref, w_ref, prev_ref, o_ref, dw_ref, qpad_ref, dpad_ref):
        del prev_ref
        seg = pl.program_id(1)
        w = w_ref[...]
        rows = lambda ref, r0: ref[pl.ds(r0, CH), :].astype(F32)
        _fill_pad(qpad_ref, lambda r0: rows(cg_ref, r0) * rows(xc_ref, r0), T)
        _fill_pad(dpad_ref, lambda r0: rows(dy_ref, r0) * rows(bg_ref, r0), T)

        @pl.when(seg == 0)
        def _():
            def chunk(r0, carry):
                dq = _conv3_t(_ext(dpad_ref, r0), w)
                o_ref[pl.ds(r0, CH), :] = (dq * rows(cg_ref, r0)).astype(BF16)
                dcq = rows(dy_ref, r0) * rows(bg_ref, r0)
                return [a + b for a, b in zip(carry, _conv3_dw(_ext(qpad_ref, r0), dcq))]

            dws = _chunks(T, chunk, [jnp.zeros((1, tc), F32)] * 3)
            dw_ref[...] = jnp.concatenate(dws + [jnp.zeros((5, tc), F32)], axis=0)

        @pl.when(seg == 1)
        def _():
            def chunk(r0, carry):
                cq = _conv3(_ext(qpad_ref, r0), w)
                o_ref[pl.ds(r0, CH), :] = (rows(dy_ref, r0) * cq).astype(BF16)
                return carry

            _chunks(T, chunk)

        @pl.when(seg == 2)
        def _():
            def chunk(r0, carry):
                dq = _conv3_t(_ext(dpad_ref, r0), w)
                o_ref[pl.ds(r0, CH), :] = (dq * rows(xc_ref, r0)).astype(BF16)
                return carry

            _chunks(T, chunk)

    col = lambda off: pl.BlockSpec((T, tc), lambda j, s: (0, off * nb + j))
    return pl.pallas_call(
        body, name=name, grid=(nb, 3),
        in_specs=[col(3), col(4), col(5), pl.BlockSpec((T, tc), lambda j, s: (0, j)),
                  pl.BlockSpec((3, tc), lambda j, s: (0, j)), pl.BlockSpec(memory_space=pl.ANY)],
        out_specs=[pl.BlockSpec((T, tc), lambda j, s: (0, (3 + s) * nb + j)), pl.BlockSpec((8, tc), lambda j, s: (0, j))],
        out_shape=[jax.ShapeDtypeStruct(dPl.shape, BF16), jax.ShapeDtypeStruct((8, DP), F32)],
        scratch_shapes=[pltpu.VMEM((T + 2 * HALO, tc), F32), pltpu.VMEM((T + 2 * HALO, tc), F32)],
        input_output_aliases={5: 0},
        compiler_params=_params(("parallel", "arbitrary")),
    )(P, P, P, dy, cw, dPl)


def _merge_fwd(P, ys, ws, name):
    T = P.shape[0]
    DP, D = ws[0].shape
    tm = _pick(T, (512, 256, 128))
    tn = _pick(D, (1024, 512, 256))
    goff = (6 * DP) // tn

    def body(ya, yb, yc, wa, wb, wc, ga, gb, gc, oa_ref, ob_ref, oc_ref, m_ref):
        m = None
        for y, w, g, o_ref in ((ya, wa, ga, oa_ref), (yb, wb, gb, ob_ref), (yc, wc, gc, oc_ref)):
            o = jnp.dot(y[...], w[...], preferred_element_type=F32)
            o_ref[...] = o.astype(BF16)
            t = _sigmoid(g[...].astype(F32)) * o
            m = t if m is None else m + t
        m_ref[...] = m.astype(BF16)

    yspec = pl.BlockSpec((tm, DP), lambda i, j: (i, 0))
    wspec = pl.BlockSpec((DP, tn), lambda i, j: (0, j))
    gspec = lambda b: pl.BlockSpec((tm, tn), lambda i, j: (i, goff + b * (D // tn) + j))
    ospec = pl.BlockSpec((tm, tn), lambda i, j: (i, j))
    return pl.pallas_call(
        body, name=name, grid=(T // tm, D // tn),
        in_specs=[yspec] * 3 + [wspec] * 3 + [gspec(0), gspec(1), gspec(2)],
        out_specs=[ospec] * 4,
        out_shape=[jax.ShapeDtypeStruct((T, D), BF16)] * 4,
        compiler_params=_params(("parallel", "parallel")),
    )(*ys, *ws, P, P, P)


def _merge_bwd(P, dM, os_, name):
    T, D = dM.shape
    tm = _row_tile(T)

    def body(dm_ref, oa, ob, oc, g_ref, da_ref, db_ref, dc_ref, dg_ref):
        dm = dm_ref[...].astype(F32)
        for b, (o_ref, d_ref) in enumerate(((oa, da_ref), (ob, db_ref), (oc, dc_ref))):
            s = _sigmoid(g_ref[:, b * D:(b + 1) * D].astype(F32))
            d_ref[...] = (dm * s).astype(BF16)
            dg_ref[:, b * D:(b + 1) * D] = (dm * o_ref[...].astype(F32) * s * (1.0 - s)).astype(BF16)

    row = pl.BlockSpec((tm, D), lambda i: (i, 0))
    return pl.pallas_call(
        body, name=name, grid=(T // tm,),
        in_specs=[row] * 4 + [pl.BlockSpec((tm, 3 * D), lambda i: (i, 1))],
        out_specs=[row] * 3 + [pl.BlockSpec((tm, 3 * D), lambda i: (i, 0))],
        out_shape=[jax.ShapeDtypeStruct((T, D), BF16)] * 3 + [jax.ShapeDtypeStruct((T, 3 * D), BF16)],
        compiler_params=_params(("parallel",)),
    )(dM, *os_, P)


def _glu_fwd(U, fw, name):
    T, F2 = U.shape
    F = F2 // 2
    tc = 256
    nb = F // tc

    def body(ug_ref, uv_ref, wg_ref, wv_ref, o_ref, gpad_ref, vpad_ref):
        _fill_pad(gpad_ref, lambda r0: ug_ref[pl.ds(r0, CH), :].astype(F32), T)
        _fill_pad(vpad_ref, lambda r0: uv_ref[pl.ds(r0, CH), :].astype(F32), T)
        wg, wv = wg_ref[...], wv_ref[...]

        def chunk(r0, carry):
            gate = _conv3(_ext(gpad_ref, r0), wg)
            val = _conv3(_ext(vpad_ref, r0), wv)
            o_ref[pl.ds(r0, CH), :] = (_silu(gate) * val).astype(BF16)
            return carry

        _chunks(T, chunk)

    return pl.pallas_call(
        body, name=name, grid=(nb,),
        in_specs=[pl.BlockSpec((T, tc), lambda j: (0, j)), pl.BlockSpec((T, tc), lambda j: (0, nb + j)),
                  pl.BlockSpec((3, tc), lambda j: (0, j)), pl.BlockSpec((3, tc), lambda j: (0, nb + j))],
        out_specs=pl.BlockSpec((T, tc), lambda j: (0, j)),
        out_shape=jax.ShapeDtypeStruct((T, F), BF16),
        scratch_shapes=[pltpu.VMEM((T + 2 * HALO, tc), F32)] * 2,
        compiler_params=_params(("parallel",)),
    )(U, U, fw, fw)


def _glu_bwd(U, dact, fw, name):
    T, F2 = U.shape
    F = F2 // 2
    tc = 128
    nb = F // tc

    def body(ug_ref, uv_ref, da_ref, wg_ref, wv_ref, dug_ref, duv_ref, dwg_ref, dwv_ref, gpad, vpad, dgpad, dvpad):
        _fill_pad(gpad, lambda r0: ug_ref[pl.ds(r0, CH), :].astype(F32), T)
        _fill_pad(vpad, lambda r0: uv_ref[pl.ds(r0, CH), :].astype(F32), T)
        wg, wv = wg_ref[...], wv_ref[...]

        def chunk1(r0, carry):
            gate = _conv3(_ext(gpad, r0), wg)
            val = _conv3(_ext(vpad, r0), wv)
            s = _sigmoid(gate)
            da = da_ref[pl.ds(r0, CH), :].astype(F32)
            dgpad[pl.ds(HALO + r0, CH), :] = da * val * (s * (1.0 + gate * (1.0 - s)))
            dvpad[pl.ds(HALO + r0, CH), :] = da * (gate * s)
            return carry

        z = jnp.zeros((HALO, tc), F32)
        for p in (dgpad, dvpad):
            p[pl.ds(0, HALO), :] = z
            p[pl.ds(HALO + T, HALO), :] = z
        _chunks(T, chunk1)

        def chunk2(r0, carry):
            new = []
            for pad, dpad, wk, o_ref in ((gpad, dgpad, wg, dug_ref), (vpad, dvpad, wv, duv_ref)):
                dext = _ext(dpad, r0)
                taps = [_tap(dext, 1), _tap(dext, 0), _tap(dext, -1)]
                o_ref[pl.ds(r0, CH), :] = (taps[0] * wk[0:1] + taps[1] * wk[1:2] + taps[2] * wk[2:3]).astype(BF16)
                xs = pad[pl.ds(HALO + r0, CH), :]
                new += [jnp.sum(xs * t, axis=0, keepdims=True) for t in taps]
            return [a + b for a, b in zip(carry, new)]

        dws = _chunks(T, chunk2, [jnp.zeros((1, tc), F32)] * 6)
        dwg_ref[...] = jnp.concatenate(dws[:3] + [jnp.zeros((5, tc), F32)], axis=0)
        dwv_ref[...] = jnp.concatenate(dws[3:] + [jnp.zeros((5, tc), F32)], axis=0)

    lo = pl.BlockSpec((T, tc), lambda j: (0, j))
    hi = pl.BlockSpec((T, tc), lambda j: (0, nb + j))
    wlo = pl.BlockSpec((3, tc), lambda j: (0, j))
    whi = pl.BlockSpec((3, tc), lambda j: (0, nb + j))
    dwspec = pl.BlockSpec((8, tc), lambda j: (0, j))
    return pl.pallas_call(
        body, name=name, grid=(nb,),
        in_specs=[lo, hi, lo, wlo, whi],
        out_specs=[lo, lo, dwspec, dwspec],
        out_shape=[jax.ShapeDtypeStruct((T, F), BF16)] * 2 + [jax.ShapeDtypeStruct((8, F), F32)] * 2,
        scratch_shapes=[pltpu.VMEM((T + 2 * HALO, tc), F32)] * 4,
        compiler_params=_params(("parallel",)),
    )(U, U, dact, fw, fw)


_ANY = pl.BlockSpec(memory_space=pl.ANY)
_MESH = pl.DeviceIdType.MESH


def _mesh_pos():
    return lax.axis_index("x"), lax.axis_index("y"), lax.axis_index("c")


def _other_chips(x, y):
    return [(1 - x, y), (x, 1 - y), (1 - x, 1 - y)]


_HBM = pl.BlockSpec(memory_space=pltpu.HBM)
_SEM = pl.BlockSpec(memory_space=pltpu.SEMAPHORE)
_EFFECT = pltpu.SideEffectType.DATAFLOW_SIDE_EFFECTING


def _in_hbm(a):
    return pltpu.with_memory_space_constraint(a, pltpu.HBM)


def _split_start(groups, name, deps=()):
    flat, where = [], []
    for bufs, _, _ in groups:
        idx = []
        for b in bufs:
            hit = [i for i, f in enumerate(flat) if f is b]
            if not hit:
                flat.append(b)
            idx.append(hit[0] if hit else len(flat) - 1)
        where.append(idx)
    nb = len(flat)
    ng = len(groups)

    def body(*refs):
        ins = refs[:nb]
        sems = refs[nb + len(deps):nb + len(deps) + 2 * ng]
        token = refs[-1]
        for gi, (_, make_copies, _) in enumerate(groups):
            for cp in make_copies([ins[i] for i in where[gi]], sems[2 * gi], sems[2 * gi + 1]):
                cp.start()
        token[...] = jnp.zeros(token.shape, token.dtype)

    sem_shapes = []
    for _, _, n in groups:
        sem_shapes += [pltpu.SemaphoreType.DMA((n,)), pltpu.SemaphoreType.DMA((n,))]
    outs = pl.pallas_call(
        body, name=name,
        in_specs=[_HBM] * nb + [_ANY] * len(deps),
        out_specs=[_SEM] * (2 * ng) + [_HBM] * nb + [pl.BlockSpec(memory_space=pltpu.VMEM)],
        out_shape=sem_shapes + [pltpu.HBM(b.shape, b.dtype) for b in flat] + [jax.ShapeDtypeStruct((8, LANE), F32)],
        input_output_aliases={i: 2 * ng + i for i in range(nb)},
        compiler_params=pltpu.CompilerParams(has_side_effects=_EFFECT),
    )(*[_in_hbm(b) for b in flat], *deps)
    res = [(outs[2 * gi], outs[2 * gi + 1], [outs[2 * ng + i] for i in where[gi]]) for gi in range(ng)]
    return res, outs[-1]


def _split_wait(started, make_copies, after, name):
    send, recv, bufs = started
    nb = len(bufs)
    after = list(after) if isinstance(after, (list, tuple)) else [after]

    def body(*refs):
        for cp in make_copies(refs[:nb], refs[nb], refs[nb + 1]):
            cp.wait_send()
            cp.wait_recv()

    outs = pl.pallas_call(
        body, name=name,
        in_specs=[_HBM] * nb + [_SEM, _SEM] + [_ANY] * len(after),
        out_specs=[_HBM] * nb,
        out_shape=[pltpu.HBM(b.shape, b.dtype) for b in bufs],
        input_output_aliases={i: i for i in range(nb)},
        compiler_params=pltpu.CompilerParams(has_side_effects=_EFFECT),
    )(*bufs, send, recv, *after)
    return list(outs)


def _slot(buf, s, cols=None):
    if len(buf.shape) == 3:
        return buf.at[s] if cols is None else buf.at[s, :, pl.ds(cols[0], cols[1])]
    ns = buf.shape[1] // N_DEV
    first, width = (0, ns) if cols is None else cols
    return buf.at[:, pl.ds(pl.multiple_of(s * ns, LANE) + first, width)]


def _copies_gather_chips(nbuf, cols=None):
    def make(bufs, send, recv):
        x, y, c = _mesh_pos()
        targets = [(x, y, 1 - c)] + [(px, py, c) for px, py in _other_chips(x, y)]
        cps = []
        for a in range(nbuf):
            mine = _slot(bufs[a], 4 * x + 2 * y + c, cols)
            for k, to in enumerate(targets):
                cps.append(pltpu.make_async_remote_copy(src_ref=mine, dst_ref=mine, send_sem=send.at[4 * a + k],
                                                        recv_sem=recv.at[4 * a + k], device_id=to, device_id_type=_MESH))
        return cps

    return make, 4 * nbuf


def _copies_gather_forward(nbuf, cols=None):
    def make(bufs, send, recv):
        x, y, c = _mesh_pos()
        cps = []
        for a in range(nbuf):
            for j, (px, py) in enumerate(_other_chips(x, y)):
                slot = _slot(bufs[a], 4 * px + 2 * py + c, cols)
                cps.append(pltpu.make_async_remote_copy(src_ref=slot, dst_ref=slot, send_sem=send.at[3 * a + j],
                                                        recv_sem=recv.at[3 * a + j], device_id=(x, y, 1 - c),
                                                        device_id_type=_MESH))
        return cps

    return make, 3 * nbuf


def _copies_rs_sibling(n):
    def make(bufs, send, recv):
        x, y, c = _mesh_pos()
        cps = []
        for a in range(n):
            for q in range(4):
                cps.append(pltpu.make_async_remote_copy(
                    src_ref=_slot(bufs[a], 2 * q + (1 - c)), dst_ref=bufs[n + a].at[q], send_sem=send.at[4 * a + q],
                    recv_sem=recv.at[4 * a + q], device_id=(x, y, 1 - c), device_id_type=_MESH))
        return cps

    return make, 4 * n


def _copies_rs_chips(n):
    def make(bufs, send, recv):
        x, y, c = _mesh_pos()
        cps = []
        for a in range(n):
            for j, (px, py) in enumerate(_other_chips(x, y)):
                cps.append(pltpu.make_async_remote_copy(
                    src_ref=bufs[a].at[j], dst_ref=bufs[n + a].at[j], send_sem=send.at[3 * a + j],
                    recv_sem=recv.at[3 * a + j], device_id=(px, py, c), device_id_type=_MESH))
        return cps

    return make, 3 * n


def _cast_place(stacked, layer, slot, dtype, name, deps=(), by_cols=False):
    _, R, C = stacked.shape
    tr = _rows_tile(R, C)

    def body(slot_ref, x_ref, *rest):
        del slot_ref
        rest[-1][...] = x_ref[...].astype(dtype)

    if by_cols:
        out_spec = pl.BlockSpec((tr, C), lambda i, s: (i, s[0]))
        out_shape = jax.ShapeDtypeStruct((R, N_DEV * C), dtype)
    else:
        out_spec = pl.BlockSpec((None, tr, C), lambda i, s: (s[0], i, 0))
        out_shape = jax.ShapeDtypeStruct((N_DEV, R, C), dtype)
    return pl.pallas_call(
        body, name=name,
        grid_spec=pltpu.PrefetchScalarGridSpec(
            num_scalar_prefetch=1, grid=(R // tr,),
            in_specs=[pl.BlockSpec((None, tr, C), lambda i, s: (layer, i, 0))] + [_ANY] * len(deps),
            out_specs=out_spec),
        out_shape=out_shape,
        compiler_params=_params(("parallel",)),
    )(slot, stacked, *deps)


def _rows_tile(R, C, budget=2 * MIB):
    if R * C * 4 <= budget:
        return R
    for t in (2048, 1024, 704, 512, 352, 256, 128, 64, 32, 16, 8):
        if R % t == 0 and t * C * 4 <= budget:
            return t
    return R


def _chip_of_relation(j, pos):
    x, y = pos[0], pos[1]
    return 2 * jnp.where(j == 1, x, 1 - x) + jnp.where(j == 0, y, 1 - y)


def _pair_sum(grad, recv, pos, name):
    _, R, C = recv.shape
    tr = _rows_tile(R, C, 8 * MIB)

    def body(pos_ref, g_ref, r_ref, o_ref):
        del pos_ref
        o_ref[...] = (g_ref[...].astype(F32) + r_ref[...].astype(F32)).astype(BF16)

    if len(grad.shape) == 3:
        g_spec = pl.BlockSpec((None, tr, C), lambda j, i, p: (2 * _chip_of_relation(j, p) + p[2], i, 0))
    else:
        g_spec = pl.BlockSpec((tr, C), lambda j, i, p: (i, 2 * _chip_of_relation(j, p) + p[2]))
    return pl.pallas_call(
        body, name=name,
        grid_spec=pltpu.PrefetchScalarGridSpec(
            num_scalar_prefetch=1, grid=(3, R // tr),
            in_specs=[g_spec,
                      pl.BlockSpec((None, tr, C), lambda j, i, p: (_chip_of_relation(j, p), i, 0))],
            out_specs=pl.BlockSpec((None, tr, C), lambda j, i, p: (j, i, 0))),
        out_shape=jax.ShapeDtypeStruct((3, R, C), BF16),
        compiler_params=_params(("parallel", "parallel")),
    )(pos, grad, recv)


ADAM_LR = 0.001
ADAM_B1 = 0.9
ADAM_B2 = 0.999
ADAM_EPS = 1e-08
ADAM_WD = 0.01
ADAM_STEP = 10


def _adamw(w, g, m, v):
    m = ADAM_B1 * m + (1.0 - ADAM_B1) * g
    v = ADAM_B2 * v + (1.0 - ADAM_B2) * (g * g)
    m_hat = m / (1.0 - ADAM_B1 ** ADAM_STEP)
    v_hat = v / (1.0 - ADAM_B2 ** ADAM_STEP)
    delta = -ADAM_LR * (m_hat / (jnp.sqrt(v_hat) + ADAM_EPS) + ADAM_WD * w)
    return delta, m, v


def _adamw_big(grad, sib, recv, w, m, v, pos, layer, prev, name):
    L, R, C = w.shape
    tr = _rows_tile(R, C)
    has_prev = prev is not None

    def body(pos_ref, g_ref, s_ref, r0_ref, r1_ref, r2_ref, w_ref, m_ref, v_ref, *rest):
        del pos_ref
        go_ref, d_ref, mo_ref, vo_ref = rest[-4:]
        g = g_ref[...].astype(F32) + s_ref[...].astype(F32)
        g = ((g + r0_ref[...].astype(F32)) + r1_ref[...].astype(F32)) + r2_ref[...].astype(F32)
        delta, m2, v2 = _adamw(w_ref[...], g, m_ref[...], v_ref[...])
        go_ref[...] = g
        d_ref[...] = delta
        mo_ref[...] = m2
        vo_ref[...] = v2

    lay = pl.BlockSpec((None, tr, C), lambda i, p: (layer, i, 0))
    rspec = lambda j: pl.BlockSpec((None, tr, C), lambda i, p: (j, i, 0))
    mine = lambda p: 4 * p[0] + 2 * p[1] + p[2]
    if len(grad.shape) == 3:
        g_spec = pl.BlockSpec((None, tr, C), lambda i, p: (mine(p), i, 0))
    else:
        g_spec = pl.BlockSpec((tr, C), lambda i, p: (i, mine(p)))
    s_spec = pl.BlockSpec((None, tr, C), lambda i, p: (2 * p[0] + p[1], i, 0))
    in_specs = [g_spec, s_spec, rspec(0), rspec(1), rspec(2), lay, lay, lay]
    args = [pos, grad, sib, recv, recv, recv, w, m, v]
    aliases = {}
    if has_prev:
        in_specs += [_ANY] * 4
        args += list(prev)
        aliases = {9 + k: k for k in range(4)}
    return pl.pallas_call(
        body, name=name,
        grid_spec=pltpu.PrefetchScalarGridSpec(
            num_scalar_prefetch=1, grid=(R // tr,), in_specs=in_specs, out_specs=[lay] * 4),
        out_shape=[jax.ShapeDtypeStruct((L, R, C), F32)] * 4,
        input_output_aliases=aliases,
        compiler_params=_params(("parallel",)),
    )(*args)


def _sum_slots(gathered, name):
    _, R, C = gathered.shape
    tr = _rows_tile(R, C, 2 * MIB)

    def body(x_ref, o_ref):
        s = x_ref[0]
        for k in range(1, N_DEV):
            s = s + x_ref[k]
        o_ref[...] = s

    return pl.pallas_call(
        body, name=name, grid=(R // tr,),
        in_specs=[pl.BlockSpec((N_DEV, tr, C), lambda i: (0, i, 0))],
        out_specs=pl.BlockSpec((tr, C), lambda i: (i, 0)),
        out_shape=jax.ShapeDtypeStruct((R, C), F32),
        compiler_params=_params(("parallel",)),
    )(gathered)


def _adamw_small(w, g, m, v, name):
    R, C = w.shape
    tr = _rows_tile(R, C, 2 * MIB)

    def body(w_ref, g_ref, m_ref, v_ref, d_ref, mo_ref, vo_ref):
        d_ref[...], mo_ref[...], vo_ref[...] = _adamw(w_ref[...], g_ref[...], m_ref[...], v_ref[...])

    spec = pl.BlockSpec((tr, C), lambda i: (i, 0))
    return pl.pallas_call(
        body, name=name, grid=(R // tr,),
        in_specs=[spec] * 4, out_specs=[spec] * 3,
        out_shape=[jax.ShapeDtypeStruct((R, C), F32)] * 3,
        compiler_params=_params(("parallel",)),
    )(w, g, m, v)


def _pack(arrs):
    parts = []
    for a in arrs:
        flat = a.reshape(-1)
        pad = (-flat.shape[0]) % (8 * LANE)
        if pad:
            flat = jnp.pad(flat, (0, pad))
        parts.append(flat.reshape(-1, LANE))
    return jnp.concatenate(parts, axis=0)


def _unpack(packed, shapes):
    out, r = [], 0
    for s in shapes:
        n = 1
        for d in s:
            n *= d
        rows = -(-n // (8 * LANE)) * 8
        out.append(packed[r:r + rows].reshape(-1)[:n].reshape(s))
        r += rows
    return out


def kernel(x, norm1_g, w_in, w_pool, pool_scale, sgu_norm_g, sgu_w, sgu_b, conv_w, w_pool_out, w_sgu_out, w_conv_out, w_o, norm2_g, w_up, ffn_conv_w, w_down, final_g, loss_target, m_norm1_g, m_w_in, m_w_pool, m_pool_scale, m_sgu_norm_g, m_sgu_w, m_sgu_b, m_conv_w, m_w_pool_out, m_w_sgu_out, m_w_conv_out, m_w_o, m_norm2_g, m_w_up, m_ffn_conv_w, m_w_down, m_final_g, v_norm1_g, v_w_in, v_w_pool, v_pool_scale, v_sgu_norm_g, v_sgu_w, v_sgu_b, v_conv_w, v_w_pool_out, v_w_sgu_out, v_w_conv_out, v_w_o, v_norm2_g, v_w_up, v_ffn_conv_w, v_w_down, v_final_g):
    L = norm1_g.shape[0]
    x0 = x[0]
    tgt = loss_target[0]
    G, PG = w_pool.shape[1], w_pool.shape[3]
    F = w_down.shape[1] * N_DEV
    T, D = x0.shape
    tT = _pick(T, (1024, 512, 256, 128))
    tF = _pick(F, (512, 256))
    tF2 = _pick(2 * F, (1024, 512, 256))
    kF = _pick(F, (2816, 2048, 1024, 512))
    px, py, pc = _mesh_pos()
    me = 4 * px + 2 * py + pc
    pos = jnp.stack([px, py, pc]).astype(jnp.int32)

    slot = jnp.reshape(me, (1,)).astype(jnp.int32)
    ns_in = w_in.shape[2]
    nq = 3 if ns_in % (3 * 256) == 0 else 1
    chunk_cols = [(q * (ns_in // nq), ns_in // nq) for q in range(nq)]
    win0 = _cast_place(w_in, 0, slot, BF16, "place_w_in_0", by_cols=True)
    first, tok0 = _split_start([([win0],) + _copies_gather_chips(1, chunk_cols[0])], "ag_start_first")
    win0 = first[0][2][0]
    h1_first = _rms_fwd(x0, norm1_g[0][None], "rms1_fwd_0")
    stage1, group_sizes, started1 = {}, {0: 1}, {}

    def place_groups(l, deps):
        grp = [[("w_in", w_in, BF16, True)],
               [("w_pool_out", w_pool_out, BF16, True), ("w_sgu_out", w_sgu_out, BF16, True),
                ("w_conv_out", w_conv_out, BF16, True), ("w_pool", w_pool.reshape(L, -1, PG), BF16, False),
                ("conv_w", conv_w, F32, True), ("w_o", w_o, BF16, False)],
               [("w_up", w_up, BF16, True), ("ffn_conv_w", ffn_conv_w, F32, True)],
               [("w_down", w_down, BF16, False)]]
        ks = []
        for gi, members in enumerate(grp):
            if l == 0 and gi == 0:
                continue
            k = 4 * l + gi
            bufs = [_cast_place(a, l, slot, dt, f"place_{n}_{l}", deps, bc) for n, a, dt, bc in members]
            stage1[k] = (bufs,) + _copies_gather_chips(len(members))
            group_sizes[k] = len(members)
            ks.append(k)
        return ks

    def start_groups(ks, extra, name, deps=()):
        st, tok = _split_start(extra + [stage1[k] for k in ks], name, deps)
        for k, s in zip(ks, st[len(extra):]):
            started1[k] = s
        return st[:len(extra)], tok

    rest, tok_start0 = start_groups(place_groups(0, [tok0, h1_first]),
                                    [([win0],) + _copies_gather_chips(1, cols) for cols in chunk_cols[1:]], "ag_start_0")
    if nq > 1:
        win0 = rest[0][2][0]
    chunk_sems = [first[0][:2]] + [r[:2] for r in rest]
    later_groups = [place_groups(l, [tok_start0]) for l in range(1, L)]
    placed_later = [stage1[ks[-1]][0][-1] for ks in later_groups]
    started2 = {}

    def gather_forward(k, after):
        n = group_sizes[k]
        bufs = _split_wait(started1[k], stage1[k][1], after, f"ag_wait1_{k}")
        st, tok = _split_start([(bufs,) + _copies_gather_forward(n)], f"ag_fwd_{k}")
        started2[k] = st[0]
        return tok

    def gather_done(k, after):
        return _split_wait(started2.pop(k), _copies_gather_forward(group_sizes[k])[0], after, f"ag_wait2_{k}")

    W, saved = [], []
    xc = x0
    for l in range(L):
        w = dict(
            sguw=sgu_w[l].astype(BF16),
            bexp=jnp.broadcast_to(sgu_b[l].T[:, :, None],
                                  (sgu_b.shape[2], sgu_b.shape[1], sgu_norm_g.shape[1] // sgu_b.shape[2])),
            g1=norm1_g[l][None], g2=norm2_g[l][None], gs=sgu_norm_g[l][None], scale=pool_scale[l][None])
        ka, kb, kc, kd = 4 * l, 4 * l + 1, 4 * l + 2, 4 * l + 3
        h1 = h1_first if l == 0 else _rms_fwd(xc, w["g1"], f"rms1_fwd_{l}")
        if l == 0:
            P, after = None, [h1] + placed_later
            for q, cols in enumerate(chunk_cols):
                (win0,) = _split_wait(chunk_sems[q] + ([win0],), _copies_gather_chips(1, cols)[0], after,
                                      f"ag_wait1_0_{q}")
                st, tok = _split_start([([win0],) + _copies_gather_forward(1, cols)], f"ag_fwd_0_{q}")
                (win0,) = _split_wait(st[0], _copies_gather_forward(1, cols)[0], after, f"ag_wait2_0_{q}")
                P = _mm(h1, win0, mode="nn", out_dtype=BF16, name=f"proj_in_0_{q}", tm=tT, tn=cols[1], tk=D,
                        cols=(q, nq), prev=P, deps=[tok])
                after = P
            w["win"] = win0
        else:
            (w["win"],) = gather_done(ka, h1)
            tok = gather_forward(kb, h1)
            P = _mm(h1, w["win"], mode="nn", out_dtype=BF16, name=f"proj_in_{l}", tm=tT, tn=1024, tk=D, deps=[tok])
        if l == 0:
            gather_forward(kb, P)
        w["wa"], w["wb"], w["wc"], wp, w["cw"], wo = gather_done(kb, P)
        w.update(wpool=wp.reshape(N_DEV, G, -1, PG).transpose(1, 0, 2, 3).reshape(G, PG, PG),
                 wo=wo.reshape(-1, wo.shape[2]))
        ya = _pool_fwd(P, w["wpool"], w["scale"], f"pool_fwd_{l}")
        yb = _sgu_fwd(P, w["sguw"], w["bexp"], w["gs"], f"sgu_fwd_{l}")
        yc = _conv_fwd(P, w["cw"], f"conv_fwd_{l}")
        oa, ob, oc, M = _merge_fwd(P, (ya, yb, yc), (w["wa"], w["wb"], w["wc"]), f"merge_fwd_{l}")
        tok = gather_forward(kc, M)
        if l == 0:
            for l2 in range(1, L):
                start_groups(later_groups[l2 - 1], [], f"ag_start_{l2}", [tok])
        x1, h2 = _mm_res_rms(M, w["wo"], xc, w["g2"], f"proj_o_{l}", deps=[tok])
        w["wup"], w["fw"] = gather_done(kc, h2)
        tok = gather_forward(kd, h2)
        U = _mm(h2, w["wup"], mode="nn", out_dtype=BF16, name=f"proj_up_{l}", tm=tT, tn=tF2, tk=D, deps=[tok])
        act = _glu_fwd(U, w["fw"], f"glu_fwd_{l}")
        (wdown,) = gather_done(kd, act)
        w["wdown"] = wdown.reshape(-1, wdown.shape[2])
        deps = [gather_forward(4 * (l + 1), act)] if l + 1 < L else []
        x2 = _mm(act, w["wdown"], mode="nn", out_dtype=F32, name=f"proj_down_{l}", tm=tT, tn=512, tk=F, res=x1,
                 deps=deps)
        if deps:
            tok = deps[0]
        W.append(w)
        saved.append(dict(x0=xc, h1=h1, P=P, ya=ya, yb=yb, yc=yc, oa=oa, ob=ob, oc=oc, M=M, x1=x1, h2=h2, U=U, act=act))
        xc = x2

    dx, dxb, d_final_g, loss_part = _loss_head(xc, tgt, final_g[None], "loss_head")
    loss = lax.psum(loss_part[0, 0], ("x", "y", "c"))

    wmv = dict(w_in=(w_in, m_w_in, v_w_in), w_up=(w_up, m_w_up, v_w_up), w_o=(w_o, m_w_o, v_w_o),
               w_down=(w_down, m_w_down, v_w_down), w_pool_out=(w_pool_out, m_w_pool_out, v_w_pool_out),
               w_sgu_out=(w_sgu_out, m_w_sgu_out, v_w_sgu_out), w_conv_out=(w_conv_out, m_w_conv_out, v_w_conv_out),
               w_pool=(w_pool, m_w_pool, v_w_pool))
    adam_out = {}

    def rs_begin(tag, layer, names, grads):
        n = len(grads)
        slice_shape = lambda g: g.shape[1:] if len(g.shape) == 3 else (g.shape[0], g.shape[1] // N_DEV)
        lands = [lax.empty((4,) + slice_shape(g), BF16) for g in grads]
        st, tok = _split_start([(list(grads) + lands,) + _copies_rs_sibling(n)], f"rs_sib_{tag}")
        return dict(tag=tag, layer=layer, names=names, n=n, st=st[0]), tok

    def rs_to_chips(state, after):
        n, tag = state["n"], state["tag"]
        bufs = _split_wait(state["st"], _copies_rs_sibling(n)[0], after, f"rs_sibw_{tag}")
        grads, sib = bufs[:n], bufs[n:]
        pairs = [_pair_sum(g, r, pos, f"pair_{tag}_{i}") for i, (g, r) in enumerate(zip(grads, sib))]
        lands = [lax.empty(p.shape, BF16) for p in pairs]
        st, tok = _split_start([(pairs + lands,) + _copies_rs_chips(n)], f"rs_chips_{tag}")
        state.update(st=st[0], grads=grads, sib=sib)
        return tok

    def rs_finish(state, after):
        n, tag, layer = state["n"], state["tag"], state["layer"]
        bufs = _split_wait(state["st"], _copies_rs_chips(n)[0], after, f"rs_chipsw_{tag}")
        for name, g, sb, recv in zip(state["names"], state["grads"], state["sib"], bufs[n:]):
            wk, mk, vk = (a.reshape(L, -1, a.shape[-1]) for a in wmv[name])
            adam_out[name] = _adamw_big(g, sb, recv, wk, mk, vk, pos, layer, adam_out.get(name),
                                        f"adamw_{name}_{layer}")

    small = [None] * L
    pending = []
    deps = []
    for l in reversed(range(L)):
        w, s = W[l], saved[l]
        dact = _mm(dxb, w["wdown"], mode="nt", out_dtype=BF16, name=f"d_act_{l}", tm=tT, tn=tF, tk=D, deps=deps)
        g_down = _mm(s["act"], dxb, mode="tn", out_dtype=BF16, name=f"g_down_{l}", tm=tF, tn=1024, tk=T)
        dug, duv, dfwg, dfwv = _glu_bwd(s["U"], dact, w["fw"], f"glu_bwd_{l}")
        dh2 = _mm(dug, w["wup"], mode="nt", out_dtype=F32, name=f"d_h2a_{l}", tm=tT, tn=1024, tk=kF)
        dh2 = _mm(duv, w["wup"], mode="nt", out_dtype=F32, name=f"d_h2b_{l}", tm=tT, tn=1024, tk=kF, b_koff=F // kF,
                  res=dh2)
        g_up = _mm(s["h2"], dug, mode="tn", out_dtype=BF16, name=f"g_upa_{l}", tm=1024, tn=tF, tk=T, out_n=2 * F)
        g_up = _mm(s["h2"], duv, mode="tn", out_dtype=BF16, name=f"g_upb_{l}", tm=1024, tn=tF, tk=T, out_n=2 * F,
                   out_off=F // tF, prev=g_up)
        ra, tok = rs_begin(f"a{l}", l, ["w_down", "w_up"], [g_down.reshape(N_DEV, -1, g_down.shape[1]), g_up])
        if pending:
            rs_finish(pending.pop(0), g_up)
        dx1, dx1b, dg2 = _rms_bwd(dh2, s["x1"], w["g2"], dx, f"rms2_bwd_{l}")
        dM = _mm(dx1b, w["wo"], mode="nt", out_dtype=BF16, name=f"d_m_{l}", tm=tT, tn=1024, tk=D, deps=[tok])
        g_o = _mm(s["M"], dx1b, mode="tn", out_dtype=BF16, name=f"g_o_{l}", tm=1024, tn=1024, tk=T)
        doa, dob, doc, dPg = _merge_bwd(s["P"], dM, (s["oa"], s["ob"], s["oc"]), f"merge_bwd_{l}")
        tok = rs_to_chips(ra, dPg)
        if pending:
            rs_finish(pending.pop(0), dPg)
        dya = _mm(doa, w["wa"], mode="nt", out_dtype=BF16, name=f"d_ya_{l}", tm=tT, tn=1024, tk=D, deps=[tok])
        dyb = _mm(dob, w["wb"], mode="nt", out_dtype=BF16, name=f"d_yb_{l}", tm=tT, tn=1024, tk=D)
        dyc = _mm(doc, w["wc"], mode="nt", out_dtype=BF16, name=f"d_yc_{l}", tm=tT, tn=1024, tk=D)
        g_a = _mm(s["ya"], doa, mode="tn", out_dtype=BF16, name=f"g_a_{l}", tm=1024, tn=1024, tk=T)
        g_b = _mm(s["yb"], dob, mode="tn", out_dtype=BF16, name=f"g_b_{l}", tm=1024, tn=1024, tk=T)
        g_c = _mm(s["yc"], doc, mode="tn", out_dtype=BF16, name=f"g_c_{l}", tm=1024, tn=1024, tk=T)
        da, g_pool, dscale = _pool_bwd(s["P"], dya, w["wpool"], w["scale"], f"pool_bwd_{l}")
        g_pool_s = g_pool.reshape(G, N_DEV, -1, PG).transpose(1, 0, 2, 3).reshape(N_DEV, -1, PG).astype(BF16)
        rb, tok = rs_begin(f"b{l}", l, ["w_o", "w_pool_out", "w_sgu_out", "w_conv_out", "w_pool"],
                           [g_o.reshape(N_DEV, -1, g_o.shape[1]), g_a, g_b, g_c, g_pool_s])
        if pending:
            rs_finish(pending.pop(0), da)
        dPl, g_sguw, db_exp, dgs = _sgu_bwd(s["P"], dyb, da, w["sguw"], w["bexp"], w["gs"], f"sgu_bwd_{l}")
        dPl, dcw = _conv_bwd(s["P"], dyc, w["cw"], dPl, f"conv_bwd_{l}")
        tok2 = rs_to_chips(rb, dPl)
        half = dPl.shape[1]
        g_in = _mm(s["h1"], dPl, mode="tn", out_dtype=BF16, name=f"g_ina_{l}", tm=1024, tn=1024, tk=T, out_n=2 * half,
                   deps=[tok, tok2])
        g_in = _mm(s["h1"], dPg, mode="tn", out_dtype=BF16, name=f"g_inb_{l}", tm=1024, tn=1024, tk=T, out_n=2 * half,
                   out_off=half // 1024, prev=g_in)
        rc, tok = rs_begin(f"c{l}", l, ["w_in"], [g_in])
        dh1 = _mm(dPl, w["win"], mode="nt", out_dtype=F32, name=f"d_h1a_{l}", tm=tT, tn=1024, tk=D, deps=[tok])
        tok = rs_to_chips(rc, dh1)
        dh1 = _mm(dPg, w["win"], mode="nt", out_dtype=F32, name=f"d_h1b_{l}", tm=tT, tn=1024, tk=D, b_koff=half // D,
                  res=dh1, deps=[tok])
        dx, dxb, dg1 = _rms_bwd(dh1, s["x0"], w["g1"], dx1, f"rms1_bwd_{l}")
        deps = []
        pending = [ra, rb, rc]
        small[l] = dict(norm1_g=dg1[0], pool_scale=dscale[0], sgu_norm_g=dgs[0], sgu_w=g_sguw,
                        sgu_b=db_exp[:, :, 0].T, conv_w=dcw[:3], norm2_g=dg2[0],
                        ffn_conv_w=jnp.concatenate([dfwg[:3], dfwv[:3]], axis=1))
    grad_x = dx[None]

    snames = ["norm1_g", "pool_scale", "sgu_norm_g", "sgu_w", "sgu_b", "conv_w", "norm2_g", "ffn_conv_w"]
    sparts = [jnp.stack([small[l][n] for l in range(L)]) for n in snames] + [d_final_g[0]]
    snames = snames + ["final_g"]
    packed = _pack(sparts)
    sbuf = _cast_place(packed[None], 0, slot, F32, "place_small")
    st, tok = _split_start([([sbuf],) + _copies_gather_chips(1)], "ag_small_start")
    rs_finish(pending[0], tok)
    rs_finish(pending[1], tok)
    bufs = _split_wait(st[0], _copies_gather_chips(1)[0], [o[0] for o in adam_out.values()], "ag_small_wait1")
    st, tok = _split_start([(bufs,) + _copies_gather_forward(1)], "ag_small_fwd")
    (gathered_small,) = _split_wait(st[0], _copies_gather_forward(1)[0], tok, "ag_small_wait2")
    total = _sum_slots(gathered_small, "sum_small")
    sgrads = dict(zip(snames, _unpack(total, [p.shape for p in sparts])))
    for n, width in (("conv_w", conv_w.shape[2]), ("ffn_conv_w", ffn_conv_w.shape[2])):
        sgrads[n] = lax.dynamic_slice_in_dim(sgrads[n], me * width, width, axis=2)
    sw = dict(norm1_g=(norm1_g, m_norm1_g, v_norm1_g), pool_scale=(pool_scale, m_pool_scale, v_pool_scale),
              sgu_norm_g=(sgu_norm_g, m_sgu_norm_g, v_sgu_norm_g), sgu_w=(sgu_w, m_sgu_w, v_sgu_w),
              sgu_b=(sgu_b, m_sgu_b, v_sgu_b), conv_w=(conv_w, m_conv_w, v_conv_w),
              norm2_g=(norm2_g, m_norm2_g, v_norm2_g), ffn_conv_w=(ffn_conv_w, m_ffn_conv_w, v_ffn_conv_w),
              final_g=(final_g, m_final_g, v_final_g))
    shapes = [sw[n][0].shape for n in snames]
    upd = _adamw_small(_pack([sw[n][0] for n in snames]), _pack([sgrads[n] for n in snames]),
                       _pack([sw[n][1] for n in snames]), _pack([sw[n][2] for n in snames]), "adamw_small")
    sdelta, sm, sv = (dict(zip(snames, _unpack(u, shapes))) for u in upd)
    res = {n: [sgrads[n], sdelta[n], sm[n], sv[n]] for n in snames}

    rs_finish(pending[2], [upd[0]] + [o[0] for o in adam_out.values()])
    for n, outs in adam_out.items():
        res[n] = [o.reshape(wmv[n][0].shape) for o in outs]

    order = ["norm1_g", "w_in", "w_pool", "pool_scale", "sgu_norm_g", "sgu_w", "sgu_b", "conv_w", "w_pool_out",
             "w_sgu_out", "w_conv_out", "w_o", "norm2_g", "w_up", "ffn_conv_w", "w_down", "final_g"]
    return (loss, grad_x) + tuple(res[n][k] for k in range(4) for n in order)
```

```python
import functools

import jax
import jax.numpy as jnp
from jax import lax
from jax.experimental import pallas as pl
from jax.experimental.pallas import tpu as pltpu

BF16 = jnp.bfloat16
F32 = jnp.float32
EPS = 1e-6
MIB = 1024 * 1024
V7X_VMEM_BYTES = 64 * MIB
VMEM_LIMIT = 56 * MIB
LANE = 128
N_DEV = 8


def _params(sem, **kw):
    return pltpu.CompilerParams(dimension_semantics=sem, vmem_limit_bytes=VMEM_LIMIT, **kw)


def _pick(n, cands):
    for c in cands:
        if n % c == 0:
            return c
    return n


_DIMS = {"nn": (((1,), (0,)), ((), ())), "nt": (((1,), (1,)), ((), ())), "tn": (((0,), (0,)), ((), ()))}


def _mm(a, b, *, mode, out_dtype, name, tm, tn, tk, res=None, prev=None, deps=(), cols=(0, 1), b_koff=0,
        out_off=0, out_n=None, a2=None):
    if mode == "nn":
        M, Kc = a.shape
        nj = (b.shape[1] // tn - cols[0] + cols[1] - 1) // cols[1]
        out_w = b.shape[1]
    elif mode == "nt":
        M, Kc = a.shape
        nj = b.shape[0] // tn
        out_w = b.shape[0]
    else:
        Kc, M = a.shape
        nj = b.shape[1] // tn
        out_w = out_n or b.shape[1]
    nk1 = Kc // tk
    nk = nk1 + (a2.shape[1] // tk if a2 is not None else 0)
    assert M % tm == 0 and Kc % tk == 0 and (a2 is None or mode == "nt"), (name, M, Kc, tm, tk)
    off, stride = cols
    if mode == "nn":
        a_spec = pl.BlockSpec((tm, tk), lambda i, j, k: (i, k))
        b_spec = pl.BlockSpec((tk, tn), lambda i, j, k: (k, off + j * stride))
        o_spec = pl.BlockSpec((tm, tn), lambda i, j, k: (i, off + j * stride))
    elif mode == "nt":
        a_spec = pl.BlockSpec((tm, tk), lambda i, j, k: (i, jnp.minimum(k, nk1 - 1)))
        a2_spec = pl.BlockSpec((tm, tk), lambda i, j, k: (i, jnp.maximum(k - nk1, 0)))
        b_spec = pl.BlockSpec((tn, tk), lambda i, j, k: (j, b_koff + k))
        o_spec = pl.BlockSpec((tm, tn), lambda i, j, k: (i, j))
    else:
        a_spec = pl.BlockSpec((tk, tm), lambda i, j, k: (k, i))
        b_spec = pl.BlockSpec((tk, tn), lambda i, j, k: (k, j))
        o_spec = pl.BlockSpec((tm, tn), lambda i, j, k: (i, out_off + j))
    dims = _DIMS[mode]
    has_res = res is not None
    has_prev = prev is not None
    n_in = 2 + has_res + has_prev + len(deps) + (a2 is not None)

    def body(*refs):
        a_ref, b_ref = refs[0], refs[1]
        o_ref = refs[n_in]

        def finish(acc):
            if has_res:
                acc = acc + refs[2][...]
            o_ref[...] = acc.astype(o_ref.dtype)

        if nk == 1:
            finish(lax.dot_general(a_ref[...], b_ref[...], dims, preferred_element_type=F32))
        else:
            acc_ref = refs[n_in + 1]
            k = pl.program_id(2)

            @pl.when(k == 0)
            def _():
                acc_ref[...] = jnp.zeros(acc_ref.shape, F32)

            if a2 is None:
                acc_ref[...] += lax.dot_general(a_ref[...], b_ref[...], dims, preferred_element_type=F32)
            else:
                @pl.when(k < nk1)
                def _():
                    acc_ref[...] += lax.dot_general(a_ref[...], b_ref[...], dims, preferred_element_type=F32)

                @pl.when(k >= nk1)
                def _():
                    acc_ref[...] += lax.dot_general(refs[n_in - 1][...], b_ref[...], dims, preferred_element_type=F32)

            @pl.when(k == nk - 1)
            def _():
                finish(acc_ref[...])

    in_specs = [a_spec, b_spec]
    args = [a, b]
    if has_res:
        in_specs.append(pl.BlockSpec((tm, tn), lambda i, j, k: (i, j)))
        args.append(res)
    aliases = {}
    if has_prev:
        aliases = {len(args): 0}
        in_specs.append(pl.BlockSpec(memory_space=pl.ANY))
        args.append(prev)
    for d in deps:
        in_specs.append(pl.BlockSpec(memory_space=pl.ANY))
        args.append(d)
    if a2 is not None:
        in_specs.append(a2_spec)
        args.append(a2)
    return pl.pallas_call(
        body,
        name=name,
        grid=(M // tm, nj, nk),
        in_specs=in_specs,
        out_specs=o_spec,
        out_shape=jax.ShapeDtypeStruct((M, out_w), out_dtype),
        scratch_shapes=[pltpu.VMEM((tm, tn), F32)] if nk > 1 else [],
        input_output_aliases=aliases,
        compiler_params=_params(("parallel", "parallel", "arbitrary")),
    )(*args)


def _mm_res_rms(a, b, res, g, name, deps=()):
    M, K = a.shape
    N = b.shape[1]
    tm = _pick(M, (512, 256, 128))

    def body(a_ref, b_ref, res_ref, g_ref, *rest):
        x_ref, h_ref = rest[-2:]
        xv = res_ref[...] + jnp.dot(a_ref[...], b_ref[...], preferred_element_type=F32)
        x_ref[...] = xv
        r = lax.rsqrt(jnp.mean(xv * xv, axis=-1, keepdims=True) + EPS)
        h_ref[...] = (xv * r * g_ref[...]).astype(BF16)

    row = pl.BlockSpec((tm, N), lambda i: (i, 0))
    return pl.pallas_call(
        body, name=name, grid=(M // tm,),
        in_specs=[pl.BlockSpec((tm, K), lambda i: (i, 0)), pl.BlockSpec((K, N), lambda i: (0, 0)), row,
                  pl.BlockSpec((1, N), lambda i: (0, 0))] + [pl.BlockSpec(memory_space=pl.ANY)] * len(deps),
        out_specs=[row, row],
        out_shape=[jax.ShapeDtypeStruct((M, N), F32), jax.ShapeDtypeStruct((M, N), BF16)],
        compiler_params=_params(("parallel",)),
    )(a, b, res, g, *deps)


HALO = 16
CH = 256


def _fill_pad(pad_ref, chunk_fn, T):
    z = jnp.zeros((HALO, pad_ref.shape[1]), F32)
    pad_ref[pl.ds(0, HALO), :] = z
    pad_ref[pl.ds(HALO + T, HALO), :] = z

    def body(c, carry):
        r0 = pl.multiple_of(c * CH, CH)
        pad_ref[pl.ds(HALO + r0, CH), :] = chunk_fn(r0)
        return carry

    lax.fori_loop(0, T // CH, body, 0)


def _ext(pad_ref, r0):
    return pad_ref[pl.ds(r0, CH + 2 * HALO), :]


def _tap(ext, o):
    if o == 0:
        return ext[HALO:HALO + CH]
    return pltpu.roll(ext, (-o) % ext.shape[0], axis=0)[HALO:HALO + CH]


def _chunks(T, fn, init=0):
    def body(c, carry):
        return fn(pl.multiple_of(c * CH, CH), carry)

    return lax.fori_loop(0, T // CH, body, init)


def _conv3(ext, w):
    return _tap(ext, -1) * w[0:1] + _tap(ext, 0) * w[1:2] + _tap(ext, 1) * w[2:3]


def _conv3_t(ext, w):
    return _tap(ext, 1) * w[0:1] + _tap(ext, 0) * w[1:2] + _tap(ext, -1) * w[2:3]


def _conv3_dw(xext, d):
    return [jnp.sum(_tap(xext, k - 1) * d, axis=0, keepdims=True) for k in range(3)]


def _sigmoid(x):
    return 0.5 * (jnp.tanh(0.5 * x) + 1.0)


def _silu(x):
    h = 0.5 * x
    return h * (jnp.tanh(h) + 1.0)


_GELU_C = 0.7978845608028654


def _gelu(x):
    return 0.5 * x * (1.0 + jnp.tanh(_GELU_C * (x + 0.044715 * (x * x * x))))


def _gelu_grad(x):
    t = jnp.tanh(_GELU_C * (x + 0.044715 * (x * x * x)))
    return 0.5 * (1.0 + t) + 0.5 * x * (1.0 - t * t) * (_GELU_C * (1.0 + 3.0 * 0.044715 * (x * x)))


def _row_tile(T):
    return _pick(T, (256, 128))


def _rms_fwd(x, g, name):
    T, D = x.shape
    tm = _row_tile(T)

    def body(x_ref, g_ref, h_ref):
        xv = x_ref[...]
        r = lax.rsqrt(jnp.mean(xv * xv, axis=-1, keepdims=True) + EPS)
        h_ref[...] = (xv * r * g_ref[...]).astype(BF16)

    return pl.pallas_call(
        body, name=name, grid=(T // tm,),
        in_specs=[pl.BlockSpec((tm, D), lambda i: (i, 0)), pl.BlockSpec((1, D), lambda i: (0, 0))],
        out_specs=pl.BlockSpec((tm, D), lambda i: (i, 0)),
        out_shape=jax.ShapeDtypeStruct((T, D), BF16),
        compiler_params=_params(("parallel",)),
    )(x, g)


def _rms_bwd(dh, x, g, dres, name, deps=()):
    T, D = x.shape
    tm = _row_tile(T)

    def body(dh_ref, x_ref, g_ref, dres_ref, *rest):
        dx_ref, dxb_ref, dg_ref = rest[len(deps):]
        i = pl.program_id(0)
        xv = x_ref[...]
        r = lax.rsqrt(jnp.mean(xv * xv, axis=-1, keepdims=True) + EPS)
        n = xv * r
        dh_v = dh_ref[...].astype(F32)
        dn = dh_v * g_ref[...]
        dx = dres_ref[...] + r * (dn - n * jnp.mean(dn * n, axis=-1, keepdims=True))
        dx_ref[...] = dx
        dxb_ref[...] = dx.astype(BF16)
        dg = jnp.sum(dh_v * n, axis=0, keepdims=True)

        @pl.when(i == 0)
        def _():
            dg_ref[...] = dg

        @pl.when(i > 0)
        def _():
            dg_ref[...] += dg

    row = pl.BlockSpec((tm, D), lambda i: (i, 0))
    vec = pl.BlockSpec((1, D), lambda i: (0, 0))
    return pl.pallas_call(
        body, name=name, grid=(T // tm,),
        in_specs=[row, row, vec, row] + [pl.BlockSpec(memory_space=pl.ANY)] * len(deps),
        out_specs=[row, row, vec],
        out_shape=[jax.ShapeDtypeStruct((T, D), F32), jax.ShapeDtypeStruct((T, D), BF16),
                   jax.ShapeDtypeStruct((1, D), F32)],
        compiler_params=_params(("arbitrary",)),
    )(dh, x, g, dres, *deps)


def _loss_head(x, tgt, g, name):
    T, D = x.shape
    tm = _row_tile(T)

    def body(x_ref, t_ref, g_ref, dx_ref, dxb_ref, dg_ref, l_ref):
        i = pl.program_id(0)
        xv = x_ref[...]
        gv = g_ref[...]
        r = lax.rsqrt(jnp.mean(xv * xv, axis=-1, keepdims=True) + EPS)
        n = xv * r
        e = n * gv - t_ref[...]
        dy = e * (1.0 / D)
        dn = dy * gv
        dx = r * (dn - n * jnp.mean(dn * n, axis=-1, keepdims=True))
        dx_ref[...] = dx
        dxb_ref[...] = dx.astype(BF16)
        dg = jnp.sum(dy * n, axis=0, keepdims=True)
        per_tok = jnp.mean(e * e, axis=-1, keepdims=True)
        lv = jnp.broadcast_to(0.5 * jnp.sum(per_tok, axis=0, keepdims=True), (1, LANE))

        @pl.when(i == 0)
        def _():
            dg_ref[...] = dg
            l_ref[...] = lv

        @pl.when(i > 0)
        def _():
            dg_ref[...] += dg
            l_ref[...] += lv

    row = pl.BlockSpec((tm, D), lambda i: (i, 0))
    vec = pl.BlockSpec((1, D), lambda i: (0, 0))
    return pl.pallas_call(
        body, name=name, grid=(T // tm,),
        in_specs=[row, row, vec],
        out_specs=[row, row, vec, pl.BlockSpec((1, LANE), lambda i: (0, 0))],
        out_shape=[jax.ShapeDtypeStruct((T, D), F32), jax.ShapeDtypeStruct((T, D), BF16),
                   jax.ShapeDtypeStruct((1, D), F32), jax.ShapeDtypeStruct((1, LANE), F32)],
        compiler_params=_params(("arbitrary",)),
    )(x, tgt, g)


POOL_WINDOWS = (2, 4, 8, 16)


def _pool_offsets(w):
    return range(-(w // 2), w - w // 2)


def _pool_cnt(r0, w, T, shape):
    t = r0 + lax.broadcasted_iota(jnp.int32, shape, 0)
    lo = jnp.maximum(t - w // 2, 0)
    hi = jnp.minimum(t + (w - w // 2 - 1), T - 1)
    return (hi - lo + 1).astype(F32)


def _pooled(pad_ref, r0, w, T):
    ext = _ext(pad_ref, r0)
    s = None
    for o in _pool_offsets(w):
        tap = _tap(ext, o)
        s = tap if s is None else s + tap
    cur = ext[HALO:HALO + CH]
    return s / _pool_cnt(r0, w, T, cur.shape) - cur


def _pool_fwd(P, w_pool, scale, name):
    T = P.shape[0]
    G, PG, _ = w_pool.shape

    def body(a_ref, w_ref, s_ref, o_ref, pad_ref):
        g = pl.program_id(0)
        _fill_pad(pad_ref, lambda r0: a_ref[pl.ds(r0, CH), :].astype(F32), T)
        for gi, w in enumerate(POOL_WINDOWS):
            @pl.when(g == gi)
            def _(w=w):
                def chunk(r0, carry):
                    pooled = _pooled(pad_ref, r0, w, T).astype(BF16)
                    y = jnp.dot(pooled, w_ref[...], preferred_element_type=F32) * s_ref[...]
                    o_ref[pl.ds(r0, CH), :] = y.astype(BF16)
                    return carry

                _chunks(T, chunk)

    return pl.pallas_call(
        body, name=name, grid=(G,),
        in_specs=[pl.BlockSpec((T, PG), lambda g: (0, g)),
                  pl.BlockSpec((None, PG, PG), lambda g: (g, 0, 0)),
                  pl.BlockSpec((1, PG), lambda g: (0, g))],
        out_specs=pl.BlockSpec((T, PG), lambda g: (0, g)),
        out_shape=jax.ShapeDtypeStruct((T, G * PG), BF16),
        scratch_shapes=[pltpu.VMEM((T + 2 * HALO, PG), F32)],
        compiler_params=_params(("parallel",)),
    )(P, w_pool, scale)


def _pool_bwd(P, dy, w_pool, scale, name):
    T = P.shape[0]
    G, PG, _ = w_pool.shape

    def body(a_ref, dy_ref, w_ref, s_ref, da_ref, dw_ref, ds_ref, pad_ref, gpad_ref):
        g = pl.program_id(0)
        gpad_ref[pl.ds(0, HALO), :] = jnp.zeros((HALO, PG), F32)
        gpad_ref[pl.ds(HALO + T, HALO), :] = jnp.zeros((HALO, PG), F32)
        _fill_pad(pad_ref, lambda r0: a_ref[pl.ds(r0, CH), :].astype(F32), T)
        for gi, w in enumerate(POOL_WINDOWS):
            @pl.when(g == gi)
            def _(w=w):
                def chunk1(r0, carry):
                    dw, ds = carry
                    pooled = _pooled(pad_ref, r0, w, T).astype(BF16)
                    ypre = jnp.dot(pooled, w_ref[...], preferred_element_type=F32)
                    dyv = dy_ref[pl.ds(r0, CH), :].astype(F32)
                    ds = ds + jnp.sum(dyv * ypre, axis=0, keepdims=True)
                    dyp = (dyv * s_ref[...]).astype(BF16)
                    dw = dw + lax.dot_general(pooled, dyp, _DIMS["tn"], preferred_element_type=F32)
                    dpool = lax.dot_general(dyp, w_ref[...], _DIMS["nt"], preferred_element_type=F32)
                    gpad_ref[pl.ds(HALO + r0, CH), :] = dpool / _pool_cnt(r0, w, T, dpool.shape)
                    return dw, ds

                dw, ds = _chunks(T, chunk1, (jnp.zeros((PG, PG), F32), jnp.zeros((1, PG), F32)))
                dw_ref[...] = dw
                ds_ref[...] = ds

                def chunk2(r0, carry):
                    ext = _ext(gpad_ref, r0)
                    cur = ext[HALO:HALO + CH]
                    acc = None
                    for o in _pool_offsets(w):
                        tap = _tap(ext, -o)
                        acc = tap if acc is None else acc + tap
                    da_ref[pl.ds(r0, CH), :] = (acc - cur * _pool_cnt(r0, w, T, cur.shape)).astype(BF16)
                    return carry

                _chunks(T, chunk2)

    col = pl.BlockSpec((T, PG), lambda g: (0, g))
    return pl.pallas_call(
        body, name=name, grid=(G,),
        in_specs=[col, col, pl.BlockSpec((None, PG, PG), lambda g: (g, 0, 0)), pl.BlockSpec((1, PG), lambda g: (0, g))],
        out_specs=[col, pl.BlockSpec((None, PG, PG), lambda g: (g, 0, 0)), pl.BlockSpec((1, PG), lambda g: (0, g))],
        out_shape=[jax.ShapeDtypeStruct((T, G * PG), BF16), jax.ShapeDtypeStruct((G, PG, PG), F32),
                   jax.ShapeDtypeStruct((1, G * PG), F32)],
        scratch_shapes=[pltpu.VMEM((T + 2 * HALO, PG), F32), pltpu.VMEM((T + 2 * HALO, PG), F32)],
        compiler_params=_params(("parallel",)),
    )(P, dy, w_pool, scale)


SGU_CHUNK = 128


def _sgu_common(u_ref, v_ref, gs_ref):
    up = u_ref[...].astype(F32)
    vp = v_ref[...].astype(F32)
    gv = _gelu(vp)
    rv = lax.rsqrt(jnp.mean(gv * gv, axis=-1, keepdims=True) + EPS)
    nrm = gv * rv
    return up, vp, nrm, rv, (nrm * gs_ref[...]).astype(BF16)


def _sgu_fwd(P, sgu_w, b_exp, gs, name):
    T = P.shape[0]
    G, _, SG = b_exp.shape
    DP = G * SG
    tm = _row_tile(T)

    def body(u_ref, v_ref, w_ref, b_ref, gs_ref, o_ref):
        up, _, _, _, vn = _sgu_common(u_ref, v_ref, gs_ref)
        gu = _gelu(up)
        for n in range(tm // SGU_CHUNK):
            rows = slice(n * SGU_CHUNK, (n + 1) * SGU_CHUNK)
            for g in range(G):
                cols = slice(g * SG, (g + 1) * SG)
                z = jnp.dot(w_ref[g], vn[rows, cols], preferred_element_type=F32) + b_ref[g]
                o_ref[rows, cols] = (gu[rows, cols] * z).astype(BF16)

    return pl.pallas_call(
        body, name=name, grid=(T // tm,),
        in_specs=[pl.BlockSpec((tm, DP), lambda i: (i, 1)), pl.BlockSpec((tm, DP), lambda i: (i, 2)),
                  pl.BlockSpec(sgu_w.shape, lambda i: (0, 0, 0)), pl.BlockSpec(b_exp.shape, lambda i: (0, 0, 0)),
                  pl.BlockSpec((1, DP), lambda i: (0, 0))],
        out_specs=pl.BlockSpec((tm, DP), lambda i: (i, 0)),
        out_shape=jax.ShapeDtypeStruct((T, DP), BF16),
        compiler_params=_params(("parallel",)),
    )(P, P, sgu_w, b_exp, gs)


def _sgu_bwd(P, dy, da, sgu_w, b_exp, gs, name):
    T = P.shape[0]
    G, _, SG = b_exp.shape
    DP = G * SG
    tm = _row_tile(T)

    def body(u_ref, v_ref, dy_ref, da_ref, w_ref, b_ref, gs_ref, o_ref, dw_ref, db_ref, dgs_ref, dzs_ref):
        i = pl.program_id(0)
        up, vp, nrm, rv, vn = _sgu_common(u_ref, v_ref, gs_ref)
        gu = _gelu(up)
        dyv = dy_ref[...].astype(F32)
        o_ref[:, 0:DP] = da_ref[...]

        @pl.when(i == 0)
        def _():
            dw_ref[...] = jnp.zeros(dw_ref.shape, F32)
            dzs_ref[...] = jnp.zeros(dzs_ref.shape, F32)
            dgs_ref[...] = jnp.zeros(dgs_ref.shape, F32)

        dgs = jnp.zeros((1, DP), F32)
        for n in range(tm // SGU_CHUNK):
            rows = slice(n * SGU_CHUNK, (n + 1) * SGU_CHUNK)
            dvn_parts = []
            for g in range(G):
                cols = slice(g * SG, (g + 1) * SG)
                vng = vn[rows, cols]
                z = jnp.dot(w_ref[g], vng, preferred_element_type=F32) + b_ref[g]
                dyg = dyv[rows, cols]
                du = dyg * z
                o_ref[rows, DP + g * SG:DP + (g + 1) * SG] = (du * _gelu_grad(up[rows, cols])).astype(BF16)
                dz = dyg * gu[rows, cols]
                dzb = dz.astype(BF16)
                dzs_ref[g] += dz
                dw_ref[g] += lax.dot_general(dzb, vng, _DIMS["nt"], preferred_element_type=F32)
                dvn_parts.append(lax.dot_general(w_ref[g], dzb, _DIMS["tn"], preferred_element_type=F32))
            dvn = jnp.concatenate(dvn_parts, axis=1)
            nr = nrm[rows]
            dgs = dgs + jnp.sum(dvn * nr, axis=0, keepdims=True)
            dn = dvn * gs_ref[...]
            dgv = rv[rows] * (dn - nr * jnp.mean(dn * nr, axis=-1, keepdims=True))
            o_ref[rows, 2 * DP:3 * DP] = (dgv * _gelu_grad(vp[rows])).astype(BF16)
        dgs_ref[...] += dgs

        @pl.when(i == T // tm - 1)
        def _():
            for g in range(G):
                db_ref[g] = jnp.broadcast_to(jnp.sum(dzs_ref[g], axis=1, keepdims=True), (SGU_CHUNK, SG))

    full3 = lambda a: pl.BlockSpec(a.shape, lambda i: (0, 0, 0))
    return pl.pallas_call(
        body, name=name, grid=(T // tm,),
        in_specs=[pl.BlockSpec((tm, DP), lambda i: (i, 1)), pl.BlockSpec((tm, DP), lambda i: (i, 2)),
                  pl.BlockSpec((tm, DP), lambda i: (i, 0)), pl.BlockSpec((tm, DP), lambda i: (i, 0)),
                  full3(sgu_w), full3(b_exp), pl.BlockSpec((1, DP), lambda i: (0, 0))],
        out_specs=[pl.BlockSpec((tm, 3 * DP), lambda i: (i, 0)), full3(sgu_w), full3(b_exp),
                   pl.BlockSpec((1, DP), lambda i: (0, 0))],
        out_shape=[jax.ShapeDtypeStruct((T, 6 * DP), BF16), jax.ShapeDtypeStruct(sgu_w.shape, F32),
                   jax.ShapeDtypeStruct(b_exp.shape, F32), jax.ShapeDtypeStruct((1, DP), F32)],
        scratch_shapes=[pltpu.VMEM(b_exp.shape, F32)],
        compiler_params=_params(("arbitrary",)),
    )(P, P, dy, da, sgu_w, b_exp, gs)


def _conv_fwd(P, cw, name):
    T = P.shape[0]
    DP = cw.shape[1]
    tc = 256
    nb = DP // tc

    def body(xc_ref, bg_ref, cg_ref, w_ref, o_ref, pad_ref):
        _fill_pad(pad_ref, lambda r0: cg_ref[pl.ds(r0, CH), :].astype(F32) * xc_ref[pl.ds(r0, CH), :].astype(F32), T)
        w = w_ref[...]

        def chunk(r0, carry):
            cq = _conv3(_ext(pad_ref, r0), w)
            o_ref[pl.ds(r0, CH), :] = (bg_ref[pl.ds(r0, CH), :].astype(F32) * cq).astype(BF16)
            return carry

        _chunks(T, chunk)

    col = lambda off: pl.BlockSpec((T, tc), lambda j: (0, off * nb + j))
    return pl.pallas_call(
        body, name=name, grid=(nb,),
        in_specs=[col(3), col(4), col(5), pl.BlockSpec((3, tc), lambda j: (0, j))],
        out_specs=pl.BlockSpec((T, tc), lambda j: (0, j)),
        out_shape=jax.ShapeDtypeStruct((T, DP), BF16),
        scratch_shapes=[pltpu.VMEM((T + 2 * HALO, tc), F32)],
        compiler_params=_params(("parallel",)),
    )(P, P, P, cw)


def _conv_bwd(P, dy, cw, dPl, name):
    T = P.shape[0]
    DP = cw.shape[1]
    tc = 256
    nb = DP // tc

    def body(xc_ref, bg_ref, cg_ref, dy_ref, w_ref, prev_ref, o_ref, dw_ref, qpad_ref, dpad_ref):
        del prev_ref
        seg = pl.program_id(1)
        w = w_ref[...]
        rows = lambda ref, r0: ref[pl.ds(r0, CH), :].astype(F32)
        _fill_pad(qpad_ref, lambda r0: rows(cg_ref, r0) * rows(xc_ref, r0), T)
        _fill_pad(dpad_ref, lambda r0: rows(dy_ref, r0) * rows(bg_ref, r0), T)

        @pl.when(seg == 0)
        def _():
            def chunk(r0, carry):
                dq = _conv3_t(_ext(dpad_ref, r0), w)
                o_ref[pl.ds(r0, CH), :] = (dq * rows(cg_ref, r0)).astype(BF16)
                dcq = rows(dy_ref, r0) * rows(bg_ref, r0)
                return [a + b for a, b in zip(carry, _conv3_dw(_ext(qpad_ref, r0), dcq))]

            dws = _chunks(T, chunk, [jnp.zeros((1, tc), F32)] * 3)
            dw_ref[...] = jnp.concatenate(dws + [jnp.zeros((5, tc), F32)], axis=0)

        @pl.when(seg == 1)
        def _():
            def chunk(r0, carry):
                cq = _conv3(_ext(qpad_ref, r0), w)
                o_ref[pl.ds(r0, CH), :] = (rows(dy_ref, r0) * cq).astype(BF16)
                return carry

            _chunks(T, chunk)

        @pl.when(seg == 2)
        def _():
            def chunk(r0, carry):
                dq = _conv3_t(_ext(dpad_ref, r0), w)
                o_ref[pl.ds(r0, CH), :] = (dq * rows(xc_ref, r0)).astype(BF16)
                return carry

            _chunks(T, chunk)

    col = lambda off: pl.BlockSpec((T, tc), lambda j, s: (0, off * nb + j))
    return pl.pallas_call(
        body, name=name, grid=(nb, 3),
        in_specs=[col(3), col(4), col(5), pl.BlockSpec((T, tc), lambda j, s: (0, j)),
                  pl.BlockSpec((3, tc), lambda j, s: (0, j)), pl.BlockSpec(memory_space=pl.ANY)],
        out_specs=[pl.BlockSpec((T, tc), lambda j, s: (0, (3 + s) * nb + j)), pl.BlockSpec((8, tc), lambda j, s: (0, j))],
        out_shape=[jax.ShapeDtypeStruct(dPl.shape, BF16), jax.ShapeDtypeStruct((8, DP), F32)],
        scratch_shapes=[pltpu.VMEM((T + 2 * HALO, tc), F32), pltpu.VMEM((T + 2 * HALO, tc), F32)],
        input_output_aliases={5: 0},
        compiler_params=_params(("parallel", "arbitrary")),
    )(P, P, P, dy, cw, dPl)


def _merge_fwd(P, ys, ws, name):
    T = P.shape[0]
    DP, D = ws[0].shape
    tm = _pick(T, (512, 256, 128))
    tn = _pick(D, (1024, 512, 256))
    goff = (6 * DP) // tn

    def body(ya, yb, yc, wa, wb, wc, ga, gb, gc, oa_ref, ob_ref, oc_ref, m_ref):
        m = None
        for y, w, g, o_ref in ((ya, wa, ga, oa_ref), (yb, wb, gb, ob_ref), (yc, wc, gc, oc_ref)):
            o = jnp.dot(y[...], w[...], preferred_element_type=F32)
            o_ref[...] = o.astype(BF16)
            t = _sigmoid(g[...].astype(F32)) * o
            m = t if m is None else m + t
        m_ref[...] = m.astype(BF16)

    yspec = pl.BlockSpec((tm, DP), lambda i, j: (i, 0))
    wspec = pl.BlockSpec((DP, tn), lambda i, j: (0, j))
    gspec = lambda b: pl.BlockSpec((tm, tn), lambda i, j: (i, goff + b * (D // tn) + j))
    ospec = pl.BlockSpec((tm, tn), lambda i, j: (i, j))
    return pl.pallas_call(
        body, name=name, grid=(T // tm, D // tn),
        in_specs=[yspec] * 3 + [wspec] * 3 + [gspec(0), gspec(1), gspec(2)],
        out_specs=[ospec] * 4,
        out_shape=[jax.ShapeDtypeStruct((T, D), BF16)] * 4,
        compiler_params=_params(("parallel", "parallel")),
    )(*ys, *ws, P, P, P)


def _merge_bwd(P, dM, os_, name):
    T, D = dM.shape
    tm = _row_tile(T)

    def body(dm_ref, oa, ob, oc, g_ref, da_ref, db_ref, dc_ref, dg_ref):
        dm = dm_ref[...].astype(F32)
        for b, (o_ref, d_ref) in enumerate(((oa, da_ref), (ob, db_ref), (oc, dc_ref))):
            s = _sigmoid(g_ref[:, b * D:(b + 1) * D].astype(F32))
            d_ref[...] = (dm * s).astype(BF16)
            dg_ref[:, b * D:(b + 1) * D] = (dm * o_ref[...].astype(F32) * s * (1.0 - s)).astype(BF16)

    row = pl.BlockSpec((tm, D), lambda i: (i, 0))
    return pl.pallas_call(
        body, name=name, grid=(T // tm,),
        in_specs=[row] * 4 + [pl.BlockSpec((tm, 3 * D), lambda i: (i, 1))],
        out_specs=[row] * 3 + [pl.BlockSpec((tm, 3 * D), lambda i: (i, 0))],
        out_shape=[jax.ShapeDtypeStruct((T, D), BF16)] * 3 + [jax.ShapeDtypeStruct((T, 3 * D), BF16)],
        compiler_params=_params(("parallel",)),
    )(dM, *os_, P)


def _glu_fwd(U, fw, name):
    T, F2 = U.shape
    F = F2 // 2
    tc = 256
    nb = F // tc

    def body(ug_ref, uv_ref, wg_ref, wv_ref, o_ref, gpad_ref, vpad_ref):
        _fill_pad(gpad_ref, lambda r0: ug_ref[pl.ds(r0, CH), :].astype(F32), T)
        _fill_pad(vpad_ref, lambda r0: uv_ref[pl.ds(r0, CH), :].astype(F32), T)
        wg, wv = wg_ref[...], wv_ref[...]

        def chunk(r0, carry):
            gate = _conv3(_ext(gpad_ref, r0), wg)
            val = _conv3(_ext(vpad_ref, r0), wv)
            o_ref[pl.ds(r0, CH), :] = (_silu(gate) * val).astype(BF16)
            return carry

        _chunks(T, chunk)

    return pl.pallas_call(
        body, name=name, grid=(nb,),
        in_specs=[pl.BlockSpec((T, tc), lambda j: (0, j)), pl.BlockSpec((T, tc), lambda j: (0, nb + j)),
                  pl.BlockSpec((3, tc), lambda j: (0, j)), pl.BlockSpec((3, tc), lambda j: (0, nb + j))],
        out_specs=pl.BlockSpec((T, tc), lambda j: (0, j)),
        out_shape=jax.ShapeDtypeStruct((T, F), BF16),
        scratch_shapes=[pltpu.VMEM((T + 2 * HALO, tc), F32)] * 2,
        compiler_params=_params(("parallel",)),
    )(U, U, fw, fw)


def _glu_bwd(U, dact, fw, name):
    T, F2 = U.shape
    F = F2 // 2
    tc = 128
    nb = F // tc

    def body(ug_ref, uv_ref, da_ref, wg_ref, wv_ref, dug_ref, duv_ref, dwg_ref, dwv_ref, gpad, vpad, dgpad, dvpad):
        _fill_pad(gpad, lambda r0: ug_ref[pl.ds(r0, CH), :].astype(F32), T)
        _fill_pad(vpad, lambda r0: uv_ref[pl.ds(r0, CH), :].astype(F32), T)
        wg, wv = wg_ref[...], wv_ref[...]

        def chunk1(r0, carry):
            gate = _conv3(_ext(gpad, r0), wg)
            val = _conv3(_ext(vpad, r0), wv)
            s = _sigmoid(gate)
            da = da_ref[pl.ds(r0, CH), :].astype(F32)
            dgpad[pl.ds(HALO + r0, CH), :] = da * val * (s * (1.0 + gate * (1.0 - s)))
            dvpad[pl.ds(HALO + r0, CH), :] = da * (gate * s)
            return carry

        z = jnp.zeros((HALO, tc), F32)
        for p in (dgpad, dvpad):
            p[pl.ds(0, HALO), :] = z
            p[pl.ds(HALO + T, HALO), :] = z
        _chunks(T, chunk1)

        def chunk2(r0, carry):
            new = []
            for pad, dpad, wk, o_ref in ((gpad, dgpad, wg, dug_ref), (vpad, dvpad, wv, duv_ref)):
                dext = _ext(dpad, r0)
                taps = [_tap(dext, 1), _tap(dext, 0), _tap(dext, -1)]
                o_ref[pl.ds(r0, CH), :] = (taps[0] * wk[0:1] + taps[1] * wk[1:2] + taps[2] * wk[2:3]).astype(BF16)
                xs = pad[pl.ds(HALO + r0, CH), :]
                new += [jnp.sum(xs * t, axis=0, keepdims=True) for t in taps]
            return [a + b for a, b in zip(carry, new)]

        dws = _chunks(T, chunk2, [jnp.zeros((1, tc), F32)] * 6)
        dwg_ref[...] = jnp.concatenate(dws[:3] + [jnp.zeros((5, tc), F32)], axis=0)
        dwv_ref[...] = jnp.concatenate(dws[3:] + [jnp.zeros((5, tc), F32)], axis=0)

    lo = pl.BlockSpec((T, tc), lambda j: (0, j))
    hi = pl.BlockSpec((T, tc), lambda j: (0, nb + j))
    wlo = pl.BlockSpec((3, tc), lambda j: (0, j))
    whi = pl.BlockSpec((3, tc), lambda j: (0, nb + j))
    dwspec = pl.BlockSpec((8, tc), lambda j: (0, j))
    return pl.pallas_call(
        body, name=name, grid=(nb,),
        in_specs=[lo, hi, lo, wlo, whi],
        out_specs=[lo, lo, dwspec, dwspec],
        out_shape=[jax.ShapeDtypeStruct((T, F), BF16)] * 2 + [jax.ShapeDtypeStruct((8, F), F32)] * 2,
        scratch_shapes=[pltpu.VMEM((T + 2 * HALO, tc), F32)] * 4,
        compiler_params=_params(("parallel",)),
    )(U, U, dact, fw, fw)


_ANY = pl.BlockSpec(memory_space=pl.ANY)
_MESH = pl.DeviceIdType.MESH


def _mesh_pos():
    return lax.axis_index("x"), lax.axis_index("y"), lax.axis_index("c")


def _other_chips(x, y):
    return [(1 - x, y), (x, 1 - y), (1 - x, 1 - y)]


_HBM = pl.BlockSpec(memory_space=pltpu.HBM)
_SEM = pl.BlockSpec(memory_space=pltpu.SEMAPHORE)
_EFFECT = pltpu.SideEffectType.DATAFLOW_SIDE_EFFECTING


def _in_hbm(a):
    return pltpu.with_memory_space_constraint(a, pltpu.HBM)


def _split_start(groups, name, deps=()):
    flat, where = [], []
    for bufs, _, _ in groups:
        idx = []
        for b in bufs:
            hit = [i for i, f in enumerate(flat) if f is b]
            if not hit:
                flat.append(b)
            idx.append(hit[0] if hit else len(flat) - 1)
        where.append(idx)
    nb = len(flat)
    ng = len(groups)

    def body(*refs):
        ins = refs[:nb]
        sems = refs[nb + len(deps):nb + len(deps) + 2 * ng]
        token = refs[-1]
        for gi, (_, make_copies, _) in enumerate(groups):
            for cp in make_copies([ins[i] for i in where[gi]], sems[2 * gi], sems[2 * gi + 1]):
                cp.start()
        token[...] = jnp.zeros(token.shape, token.dtype)

    sem_shapes = []
    for _, _, n in groups:
        sem_shapes += [pltpu.SemaphoreType.DMA((n,)), pltpu.SemaphoreType.DMA((n,))]
    outs = pl.pallas_call(
        body, name=name,
        in_specs=[_HBM] * nb + [_ANY] * len(deps),
        out_specs=[_SEM] * (2 * ng) + [_HBM] * nb + [pl.BlockSpec(memory_space=pltpu.VMEM)],
        out_shape=sem_shapes + [pltpu.HBM(b.shape, b.dtype) for b in flat] + [jax.ShapeDtypeStruct((8, LANE), F32)],
        input_output_aliases={i: 2 * ng + i for i in range(nb)},
        compiler_params=pltpu.CompilerParams(has_side_effects=_EFFECT),
    )(*[_in_hbm(b) for b in flat], *deps)
    res = [(outs[2 * gi], outs[2 * gi + 1], [outs[2 * ng + i] for i in where[gi]]) for gi in range(ng)]
    return res, outs[-1]


def _split_wait(started, make_copies, after, name):
    send, recv, bufs = started
    nb = len(bufs)
    after = list(after) if isinstance(after, (list, tuple)) else [after]

    def body(*refs):
        for cp in make_copies(refs[:nb], refs[nb], refs[nb + 1]):
            cp.wait_send()
            cp.wait_recv()

    outs = pl.pallas_call(
        body, name=name,
        in_specs=[_HBM] * nb + [_SEM, _SEM] + [_ANY] * len(after),
        out_specs=[_HBM] * nb,
        out_shape=[pltpu.HBM(b.shape, b.dtype) for b in bufs],
        input_output_aliases={i: i for i in range(nb)},
        compiler_params=pltpu.CompilerParams(has_side_effects=_EFFECT),
    )(*bufs, send, recv, *after)
    return list(outs)


def _slot(buf, s, cols=None):
    if len(buf.shape) == 3:
        return buf.at[s] if cols is None else buf.at[s, :, pl.ds(cols[0], cols[1])]
    ns = buf.shape[1] // N_DEV
    first, width = (0, ns) if cols is None else cols
    return buf.at[:, pl.ds(pl.multiple_of(s * ns, LANE) + first, width)]


def _copies_gather_chips(nbuf, cols=None):
    def make(bufs, send, recv):
        x, y, c = _mesh_pos()
        targets = [(x, y, 1 - c)] + [(px, py, c) for px, py in _other_chips(x, y)]
        cps = []
        for a in range(nbuf):
            mine = _slot(bufs[a], 4 * x + 2 * y + c, cols)
            for k, to in enumerate(targets):
                cps.append(pltpu.make_async_remote_copy(src_ref=mine, dst_ref=mine, send_sem=send.at[4 * a + k],
                                                        recv_sem=recv.at[4 * a + k], device_id=to, device_id_type=_MESH))
        return cps

    return make, 4 * nbuf


def _copies_gather_forward(nbuf, cols=None):
    def make(bufs, send, recv):
        x, y, c = _mesh_pos()
        cps = []
        for a in range(nbuf):
            for j, (px, py) in enumerate(_other_chips(x, y)):
                slot = _slot(bufs[a], 4 * px + 2 * py + c, cols)
                cps.append(pltpu.make_async_remote_copy(src_ref=slot, dst_ref=slot, send_sem=send.at[3 * a + j],
                                                        recv_sem=recv.at[3 * a + j], device_id=(x, y, 1 - c),
                                                        device_id_type=_MESH))
        return cps

    return make, 3 * nbuf


def _copies_rs_sibling(n):
    def make(bufs, send, recv):
        x, y, c = _mesh_pos()
        cps = []
        for a in range(n):
            for q in range(4):
                cps.append(pltpu.make_async_remote_copy(
                    src_ref=_slot(bufs[a], 2 * q + (1 - c)), dst_ref=bufs[n + a].at[q], send_sem=send.at[4 * a + q],
                    recv_sem=recv.at[4 * a + q], device_id=(x, y, 1 - c), device_id_type=_MESH))
        return cps

    return make, 4 * n


def _copies_rs_chips(n):
    def make(bufs, send, recv):
        x, y, c = _mesh_pos()
        cps = []
        for a in range(n):
            for j, (px, py) in enumerate(_other_chips(x, y)):
                cps.append(pltpu.make_async_remote_copy(
                    src_ref=bufs[a].at[j], dst_ref=bufs[n + a].at[j], send_sem=send.at[3 * a + j],
                    recv_sem=recv.at[3 * a + j], device_id=(px, py, c), device_id_type=_MESH))
        return cps

    return make, 3 * n


def _cast_place(stacked, layer, slot, dtype, name, deps=(), by_cols=False):
    _, R, C = stacked.shape
    tr = _rows_tile(R, C)

    def body(slot_ref, x_ref, *rest):
        del slot_ref
        rest[-1][...] = x_ref[...].astype(dtype)

    if by_cols:
        out_spec = pl.BlockSpec((tr, C), lambda i, s: (i, s[0]))
        out_shape = jax.ShapeDtypeStruct((R, N_DEV * C), dtype)
    else:
        out_spec = pl.BlockSpec((None, tr, C), lambda i, s: (s[0], i, 0))
        out_shape = jax.ShapeDtypeStruct((N_DEV, R, C), dtype)
    return pl.pallas_call(
        body, name=name,
        grid_spec=pltpu.PrefetchScalarGridSpec(
            num_scalar_prefetch=1, grid=(R // tr,),
            in_specs=[pl.BlockSpec((None, tr, C), lambda i, s: (layer, i, 0))] + [_ANY] * len(deps),
            out_specs=out_spec),
        out_shape=out_shape,
        compiler_params=_params(("parallel",)),
    )(slot, stacked, *deps)


def _rows_tile(R, C, budget=2 * MIB):
    if R * C * 4 <= budget:
        return R
    for t in (2048, 1024, 704, 512, 352, 256, 128, 64, 32, 16, 8):
        if R % t == 0 and t * C * 4 <= budget:
            return t
    return R


def _chip_of_relation(j, pos):
    x, y = pos[0], pos[1]
    return 2 * jnp.where(j == 1, x, 1 - x) + jnp.where(j == 0, y, 1 - y)


def _pair_sum(grad, recv, pos, name):
    _, R, C = recv.shape
    tr = _rows_tile(R, C, 8 * MIB)

    def body(pos_ref, g_ref, r_ref, o_ref):
        del pos_ref
        o_ref[...] = (g_ref[...].astype(F32) + r_ref[...].astype(F32)).astype(BF16)

    if len(grad.shape) == 3:
        g_spec = pl.BlockSpec((None, tr, C), lambda j, i, p: (2 * _chip_of_relation(j, p) + p[2], i, 0))
    else:
        g_spec = pl.BlockSpec((tr, C), lambda j, i, p: (i, 2 * _chip_of_relation(j, p) + p[2]))
    return pl.pallas_call(
        body, name=name,
        grid_spec=pltpu.PrefetchScalarGridSpec(
            num_scalar_prefetch=1, grid=(3, R // tr),
            in_specs=[g_spec,
                      pl.BlockSpec((None, tr, C), lambda j, i, p: (_chip_of_relation(j, p), i, 0))],
            out_specs=pl.BlockSpec((None, tr, C), lambda j, i, p: (j, i, 0))),
        out_shape=jax.ShapeDtypeStruct((3, R, C), BF16),
        compiler_params=_params(("parallel", "parallel")),
    )(pos, grad, recv)


ADAM_LR = 0.001
ADAM_B1 = 0.9
ADAM_B2 = 0.999
ADAM_EPS = 1e-08
ADAM_WD = 0.01
ADAM_STEP = 10


def _adamw(w, g, m, v):
    m = ADAM_B1 * m + (1.0 - ADAM_B1) * g
    v = ADAM_B2 * v + (1.0 - ADAM_B2) * (g * g)
    m_hat = m / (1.0 - ADAM_B1 ** ADAM_STEP)
    v_hat = v / (1.0 - ADAM_B2 ** ADAM_STEP)
    delta = -ADAM_LR * (m_hat / (jnp.sqrt(v_hat) + ADAM_EPS) + ADAM_WD * w)
    return delta, m, v


def _adamw_big(grad, sib, recv, w, m, v, pos, layer, prev, name):
    L, R, C = w.shape
    tr = _rows_tile(R, C)
    has_prev = prev is not None

    def body(pos_ref, g_ref, s_ref, r0_ref, r1_ref, r2_ref, w_ref, m_ref, v_ref, *rest):
        del pos_ref
        go_ref, d_ref, mo_ref, vo_ref = rest[-4:]
        g = g_ref[...].astype(F32) + s_ref[...].astype(F32)
        g = ((g + r0_ref[...].astype(F32)) + r1_ref[...].astype(F32)) + r2_ref[...].astype(F32)
        delta, m2, v2 = _adamw(w_ref[...], g, m_ref[...], v_ref[...])
        go_ref[...] = g
        d_ref[...] = delta
        mo_ref[...] = m2
        vo_ref[...] = v2

    lay = pl.BlockSpec((None, tr, C), lambda i, p: (layer, i, 0))
    rspec = lambda j: pl.BlockSpec((None, tr, C), lambda i, p: (j, i, 0))
    mine = lambda p: 4 * p[0] + 2 * p[1] + p[2]
    if len(grad.shape) == 3:
        g_spec = pl.BlockSpec((None, tr, C), lambda i, p: (mine(p), i, 0))
    else:
        g_spec = pl.BlockSpec((tr, C), lambda i, p: (i, mine(p)))
    s_spec = pl.BlockSpec((None, tr, C), lambda i, p: (2 * p[0] + p[1], i, 0))
    in_specs = [g_spec, s_spec, rspec(0), rspec(1), rspec(2), lay, lay, lay]
    args = [pos, grad, sib, recv, recv, recv, w, m, v]
    aliases = {}
    if has_prev:
        in_specs += [_ANY] * 4
        args += list(prev)
        aliases = {9 + k: k for k in range(4)}
    return pl.pallas_call(
        body, name=name,
        grid_spec=pltpu.PrefetchScalarGridSpec(
            num_scalar_prefetch=1, grid=(R // tr,), in_specs=in_specs, out_specs=[lay] * 4),
        out_shape=[jax.ShapeDtypeStruct((L, R, C), F32)] * 4,
        input_output_aliases=aliases,
        compiler_params=_params(("parallel",)),
    )(*args)


def _sum_slots(gathered, name):
    _, R, C = gathered.shape
    tr = _rows_tile(R, C, 2 * MIB)

    def body(x_ref, o_ref):
        s = x_ref[0]
        for k in range(1, N_DEV):
            s = s + x_ref[k]
        o_ref[...] = s

    return pl.pallas_call(
        body, name=name, grid=(R // tr,),
        in_specs=[pl.BlockSpec((N_DEV, tr, C), lambda i: (0, i, 0))],
        out_specs=pl.BlockSpec((tr, C), lambda i: (i, 0)),
        out_shape=jax.ShapeDtypeStruct((R, C), F32),
        compiler_params=_params(("parallel",)),
    )(gathered)


def _adamw_small(w, g, m, v, name):
    R, C = w.shape
    tr = _rows_tile(R, C, 2 * MIB)

    def body(w_ref, g_ref, m_ref, v_ref, d_ref, mo_ref, vo_ref):
        d_ref[...], mo_ref[...], vo_ref[...] = _adamw(w_ref[...], g_ref[...], m_ref[...], v_ref[...])

    spec = pl.BlockSpec((tr, C), lambda i: (i, 0))
    return pl.pallas_call(
        body, name=name, grid=(R // tr,),
        in_specs=[spec] * 4, out_specs=[spec] * 3,
        out_shape=[jax.ShapeDtypeStruct((R, C), F32)] * 3,
        compiler_params=_params(("parallel",)),
    )(w, g, m, v)


def _pack(arrs):
    parts = []
    for a in arrs:
        flat = a.reshape(-1)
        pad = (-flat.shape[0]) % (8 * LANE)
        if pad:
            flat = jnp.pad(flat, (0, pad))
        parts.append(flat.reshape(-1, LANE))
    return jnp.concatenate(parts, axis=0)


def _unpack(packed, shapes):
    out, r = [], 0
    for s in shapes:
        n = 1
        for d in s:
            n *= d
        rows = -(-n // (8 * LANE)) * 8
        out.append(packed[r:r + rows].reshape(-1)[:n].reshape(s))
        r += rows
    return out


def kernel(x, norm1_g, w_in, w_pool, pool_scale, sgu_norm_g, sgu_w, sgu_b, conv_w, w_pool_out, w_sgu_out, w_conv_out, w_o, norm2_g, w_up, ffn_conv_w, w_down, final_g, loss_target, m_norm1_g, m_w_in, m_w_pool, m_pool_scale, m_sgu_norm_g, m_sgu_w, m_sgu_b, m_conv_w, m_w_pool_out, m_w_sgu_out, m_w_conv_out, m_w_o, m_norm2_g, m_w_up, m_ffn_conv_w, m_w_down, m_final_g, v_norm1_g, v_w_in, v_w_pool, v_pool_scale, v_sgu_norm_g, v_sgu_w, v_sgu_b, v_conv_w, v_w_pool_out, v_w_sgu_out, v_w_conv_out, v_w_o, v_norm2_g, v_w_up, v_ffn_conv_w, v_w_down, v_final_g):
    L = norm1_g.shape[0]
    x0 = x[0]
    tgt = loss_target[0]
    G, PG = w_pool.shape[1], w_pool.shape[3]
    F = w_down.shape[1] * N_DEV
    T, D = x0.shape
    tT = _pick(T, (1024, 512, 256, 128))
    tF = _pick(F, (512, 256))
    tF2 = _pick(2 * F, (1024, 512, 256))
    kF = _pick(F, (2816, 2048, 1024, 512))
    px, py, pc = _mesh_pos()
    me = 4 * px + 2 * py + pc
    pos = jnp.stack([px, py, pc]).astype(jnp.int32)

    slot = jnp.reshape(me, (1,)).astype(jnp.int32)
    ns_in = w_in.shape[2]
    nq = 3 if ns_in % (3 * 256) == 0 else 1
    chunk_cols = [(q * (ns_in // nq), ns_in // nq) for q in range(nq)]
    win0 = _cast_place(w_in, 0, slot, BF16, "place_w_in_0", by_cols=True)
    first, tok0 = _split_start([([win0],) + _copies_gather_chips(1, chunk_cols[0])], "ag_start_first")
    win0 = first[0][2][0]
    h1_first = _rms_fwd(x0, norm1_g[0][None], "rms1_fwd_0")
    stage1, group_sizes, started1 = {}, {0: 1}, {}

    def place_groups(l, deps):
        grp = [[("w_in", w_in, BF16, True)],
               [("w_pool_out", w_pool_out, BF16, True), ("w_sgu_out", w_sgu_out, BF16, True),
                ("w_conv_out", w_conv_out, BF16, True), ("w_pool", w_pool.reshape(L, -1, PG), BF16, False),
                ("conv_w", conv_w, F32, True), ("w_o", w_o, BF16, False)],
               [("w_up", w_up, BF16, True), ("ffn_conv_w", ffn_conv_w, F32, True)],
               [("w_down", w_down, BF16, False)]]
        ks = []
        for gi, members in enumerate(grp):
            if l == 0 and gi == 0:
                continue
            k = 4 * l + gi
            bufs = [_cast_place(a, l, slot, dt, f"place_{n}_{l}", deps, bc) for n, a, dt, bc in members]
            stage1[k] = (bufs,) + _copies_gather_chips(len(members))
            group_sizes[k] = len(members)
            ks.append(k)
        return ks

    def start_groups(ks, extra, name, deps=()):
        st, tok = _split_start(extra + [stage1[k] for k in ks], name, deps)
        for k, s in zip(ks, st[len(extra):]):
            started1[k] = s
        return st[:len(extra)], tok

    rest, tok_start0 = start_groups(place_groups(0, [tok0, h1_first]),
                                    [([win0],) + _copies_gather_chips(1, cols) for cols in chunk_cols[1:]], "ag_start_0")
    if nq > 1:
        win0 = rest[0][2][0]
    chunk_sems = [first[0][:2]] + [r[:2] for r in rest]
    later_groups = [place_groups(l, [tok_start0]) for l in range(1, L)]
    placed_later = [b for ks in later_groups for k in ks for b in stage1[k][0]]
    started2 = {}

    def gather_forward(k, after):
        n = group_sizes[k]
        bufs = _split_wait(started1[k], stage1[k][1], after, f"ag_wait1_{k}")
        st, tok = _split_start([(bufs,) + _copies_gather_forward(n)], f"ag_fwd_{k}")
        started2[k] = st[0]
        return tok

    def gather_done(k, after):
        return _split_wait(started2.pop(k), _copies_gather_forward(group_sizes[k])[0], after, f"ag_wait2_{k}")

    W, saved = [], []
    xc = x0
    for l in range(L):
        w = dict(
            sguw=sgu_w[l].astype(BF16),
            bexp=jnp.broadcast_to(sgu_b[l].T[:, :, None],
                                  (sgu_b.shape[2], sgu_b.shape[1], sgu_norm_g.shape[1] // sgu_b.shape[2])),
            g1=norm1_g[l][None], g2=norm2_g[l][None], gs=sgu_norm_g[l][None], scale=pool_scale[l][None])
        ka, kb, kc, kd = 4 * l, 4 * l + 1, 4 * l + 2, 4 * l + 3
        h1 = h1_first if l == 0 else _rms_fwd(xc, w["g1"], f"rms1_fwd_{l}")
        if l == 0:
            P, after = None, [h1] + placed_later
            for q, cols in enumerate(chunk_cols):
                (win0,) = _split_wait(chunk_sems[q] + ([win0],), _copies_gather_chips(1, cols)[0], after,
                                      f"ag_wait1_0_{q}")
                st, tok = _split_start([([win0],) + _copies_gather_forward(1, cols)], f"ag_fwd_0_{q}")
                (win0,) = _split_wait(st[0], _copies_gather_forward(1, cols)[0], after, f"ag_wait2_0_{q}")
                P = _mm(h1, win0, mode="nn", out_dtype=BF16, name=f"proj_in_0_{q}", tm=tT, tn=cols[1], tk=D,
                        cols=(q, nq), prev=P, deps=[tok])
                after = P
            w["win"] = win0
        else:
            (w["win"],) = gather_done(ka, h1)
            tok = gather_forward(kb, h1)
            P = _mm(h1, w["win"], mode="nn", out_dtype=BF16, name=f"proj_in_{l}", tm=tT, tn=1024, tk=D, deps=[tok])
        if l == 0:
            gather_forward(kb, P)
        w["wa"], w["wb"], w["wc"], wp, w["cw"], wo = gather_done(kb, P)
        w.update(wpool=wp.reshape(N_DEV, G, -1, PG).transpose(1, 0, 2, 3).reshape(G, PG, PG),
                 wo=wo.reshape(-1, wo.shape[2]))
        ya = _pool_fwd(P, w["wpool"], w["scale"], f"pool_fwd_{l}")
        yb = _sgu_fwd(P, w["sguw"], w["bexp"], w["gs"], f"sgu_fwd_{l}")
        yc = _conv_fwd(P, w["cw"], f"conv_fwd_{l}")
        oa, ob, oc, M = _merge_fwd(P, (ya, yb, yc), (w["wa"], w["wb"], w["wc"]), f"merge_fwd_{l}")
        tok = gather_forward(kc, M)
        if l == 0:
            for l2 in range(1, L):
                start_groups(later_groups[l2 - 1], [], f"ag_start_{l2}", [tok])
        x1, h2 = _mm_res_rms(M, w["wo"], xc, w["g2"], f"proj_o_{l}", deps=[tok])
        w["wup"], w["fw"] = gather_done(kc, h2)
        tok = gather_forward(kd, h2)
        U = _mm(h2, w["wup"], mode="nn", out_dtype=BF16, name=f"proj_up_{l}", tm=tT, tn=tF2, tk=D, deps=[tok])
        act = _glu_fwd(U, w["fw"], f"glu_fwd_{l}")
        (wdown,) = gather_done(kd, act)
        w["wdown"] = wdown.reshape(-1, wdown.shape[2])
        deps = [gather_forward(4 * (l + 1), act)] if l + 1 < L else []
        x2 = _mm(act, w["wdown"], mode="nn", out_dtype=F32, name=f"proj_down_{l}", tm=tT, tn=512, tk=F, res=x1,
                 deps=deps)
        if deps:
            tok = deps[0]
        W.append(w)
        saved.append(dict(x0=xc, h1=h1, P=P, ya=ya, yb=yb, yc=yc, oa=oa, ob=ob, oc=oc, M=M, x1=x1, h2=h2, U=U, act=act))
        xc = x2

    dx, dxb, d_final_g, loss_part = _loss_head(xc, tgt, final_g[None], "loss_head")
    loss = lax.psum(loss_part[0, 0], ("x", "y", "c"))

    wmv = dict(w_in=(w_in, m_w_in, v_w_in), w_up=(w_up, m_w_up, v_w_up), w_o=(w_o, m_w_o, v_w_o),
               w_down=(w_down, m_w_down, v_w_down), w_pool_out=(w_pool_out, m_w_pool_out, v_w_pool_out),
               w_sgu_out=(w_sgu_out, m_w_sgu_out, v_w_sgu_out), w_conv_out=(w_conv_out, m_w_conv_out, v_w_conv_out),
               w_pool=(w_pool, m_w_pool, v_w_pool))
    adam_out = {}

    def rs_begin(tag, layer, names, grads):
        n = len(grads)
        slice_shape = lambda g: g.shape[1:] if len(g.shape) == 3 else (g.shape[0], g.shape[1] // N_DEV)
        lands = [lax.empty((4,) + slice_shape(g), BF16) for g in grads]
        st, tok = _split_start([(list(grads) + lands,) + _copies_rs_sibling(n)], f"rs_sib_{tag}")
        return dict(tag=tag, layer=layer, names=names, n=n, st=st[0]), tok

    def rs_to_chips(state, after):
        n, tag = state["n"], state["tag"]
        bufs = _split_wait(state["st"], _copies_rs_sibling(n)[0], after, f"rs_sibw_{tag}")
        grads, sib = bufs[:n], bufs[n:]
        pairs = [_pair_sum(g, r, pos, f"pair_{tag}_{i}") for i, (g, r) in enumerate(zip(grads, sib))]
        lands = [lax.empty(p.shape, BF16) for p in pairs]
        st, tok = _split_start([(pairs + lands,) + _copies_rs_chips(n)], f"rs_chips_{tag}")
        state.update(st=st[0], grads=grads, sib=sib)
        return tok

    def rs_finish(state, after):
        n, tag, layer = state["n"], state["tag"], state["layer"]
        bufs = _split_wait(state["st"], _copies_rs_chips(n)[0], after, f"rs_chipsw_{tag}")
        for name, g, sb, recv in zip(state["names"], state["grads"], state["sib"], bufs[n:]):
            wk, mk, vk = (a.reshape(L, -1, a.shape[-1]) for a in wmv[name])
            adam_out[name] = _adamw_big(g, sb, recv, wk, mk, vk, pos, layer, adam_out.get(name),
                                        f"adamw_{name}_{layer}")

    small = [None] * L
    pending = []
    deps = []
    for l in reversed(range(L)):
        w, s = W[l], saved[l]
        dact = _mm(dxb, w["wdown"], mode="nt", out_dtype=BF16, name=f"d_act_{l}", tm=tT, tn=tF, tk=D, deps=deps)
        g_down = _mm(s["act"], dxb, mode="tn", out_dtype=BF16, name=f"g_down_{l}", tm=tF, tn=1024, tk=T)
        dug, duv, dfwg, dfwv = _glu_bwd(s["U"], dact, w["fw"], f"glu_bwd_{l}")
        dh2 = _mm(dug, w["wup"], mode="nt", out_dtype=F32, name=f"d_h2_{l}", tm=tT, tn=1024, tk=kF, a2=duv)
        g_up = _mm(s["h2"], dug, mode="tn", out_dtype=BF16, name=f"g_upa_{l}", tm=1024, tn=tF, tk=T, out_n=2 * F)
        g_up = _mm(s["h2"], duv, mode="tn", out_dtype=BF16, name=f"g_upb_{l}", tm=1024, tn=tF, tk=T, out_n=2 * F,
                   out_off=F // tF, prev=g_up)
        ra, tok = rs_begin(f"a{l}", l, ["w_down", "w_up"], [g_down.reshape(N_DEV, -1, g_down.shape[1]), g_up])
        if pending:
            rs_finish(pending.pop(0), g_up)
        dx1, dx1b, dg2 = _rms_bwd(dh2, s["x1"], w["g2"], dx, f"rms2_bwd_{l}")
        dM = _mm(dx1b, w["wo"], mode="nt", out_dtype=BF16, name=f"d_m_{l}", tm=tT, tn=1024, tk=D, deps=[tok])
        g_o = _mm(s["M"], dx1b, mode="tn", out_dtype=BF16, name=f"g_o_{l}", tm=1024, tn=1024, tk=T)
        doa, dob, doc, dPg = _merge_bwd(s["P"], dM, (s["oa"], s["ob"], s["oc"]), f"merge_bwd_{l}")
        tok = rs_to_chips(ra, dPg)
        if pending:
            rs_finish(pending.pop(0), dPg)
        dya = _mm(doa, w["wa"], mode="nt", out_dtype=BF16, name=f"d_ya_{l}", tm=tT, tn=1024, tk=D, deps=[tok])
        dyb = _mm(dob, w["wb"], mode="nt", out_dtype=BF16, name=f"d_yb_{l}", tm=tT, tn=1024, tk=D)
        dyc = _mm(doc, w["wc"], mode="nt", out_dtype=BF16, name=f"d_yc_{l}", tm=tT, tn=1024, tk=D)
        g_a = _mm(s["ya"], doa, mode="tn", out_dtype=BF16, name=f"g_a_{l}", tm=1024, tn=1024, tk=T)
        g_b = _mm(s["yb"], dob, mode="tn", out_dtype=BF16, name=f"g_b_{l}", tm=1024, tn=1024, tk=T)
        g_c = _mm(s["yc"], doc, mode="tn", out_dtype=BF16, name=f"g_c_{l}", tm=1024, tn=1024, tk=T)
        da, g_pool, dscale = _pool_bwd(s["P"], dya, w["wpool"], w["scale"], f"pool_bwd_{l}")
        g_pool_s = g_pool.reshape(G, N_DEV, -1, PG).transpose(1, 0, 2, 3).reshape(N_DEV, -1, PG).astype(BF16)
        rb, tok = rs_begin(f"b{l}", l, ["w_o", "w_pool_out", "w_sgu_out", "w_conv_out", "w_pool"],
                           [g_o.reshape(N_DEV, -1, g_o.shape[1]), g_a, g_b, g_c, g_pool_s])
        if pending:
            rs_finish(pending.pop(0), da)
        dPl, g_sguw, db_exp, dgs = _sgu_bwd(s["P"], dyb, da, w["sguw"], w["bexp"], w["gs"], f"sgu_bwd_{l}")
        dPl, dcw = _conv_bwd(s["P"], dyc, w["cw"], dPl, f"conv_bwd_{l}")
        tok2 = rs_to_chips(rb, dPl)
        half = dPl.shape[1]
        g_in = _mm(s["h1"], dPl, mode="tn", out_dtype=BF16, name=f"g_ina_{l}", tm=1024, tn=1024, tk=T, out_n=2 * half,
                   deps=[tok, tok2])
        g_in = _mm(s["h1"], dPg, mode="tn", out_dtype=BF16, name=f"g_inb_{l}", tm=1024, tn=1024, tk=T, out_n=2 * half,
                   out_off=half // 1024, prev=g_in)
        rc, tok = rs_begin(f"c{l}", l, ["w_in"], [g_in])
        if l > 0:
            dh1 = _mm(dPl, w["win"], mode="nt", out_dtype=F32, name=f"d_h1_{l}", tm=tT, tn=1024, tk=D, a2=dPg,
                      deps=[tok])
            tok = rs_to_chips(rc, dh1)
        else:
            dh1 = _mm(dPl, w["win"], mode="nt", out_dtype=F32, name=f"d_h1a_{l}", tm=tT, tn=1024, tk=D, deps=[tok])
            tok = rs_to_chips(rc, dh1)
            dh1 = _mm(dPg, w["win"], mode="nt", out_dtype=F32, name=f"d_h1b_{l}", tm=tT, tn=1024, tk=D,
                      b_koff=half // D, res=dh1, deps=[tok])
        dx, dxb, dg1 = _rms_bwd(dh1, s["x0"], w["g1"], dx1, f"rms1_bwd_{l}", deps=[tok])
        deps = []
        pending = [ra, rb, rc]
        small[l] = dict(norm1_g=dg1[0], pool_scale=dscale[0], sgu_norm_g=dgs[0], sgu_w=g_sguw,
                        sgu_b=db_exp[:, :, 0].T, conv_w=dcw[:3], norm2_g=dg2[0],
                        ffn_conv_w=jnp.concatenate([dfwg[:3], dfwv[:3]], axis=1))
    grad_x = dx[None]

    snames = ["norm1_g", "pool_scale", "sgu_norm_g", "sgu_w", "sgu_b", "conv_w", "norm2_g", "ffn_conv_w"]
    sparts = [jnp.stack([small[l][n] for l in range(L)]) for n in snames] + [d_final_g[0]]
    snames = snames + ["final_g"]
    packed = _pack(sparts)
    sbuf = _cast_place(packed[None], 0, slot, F32, "place_small")
    st, tok = _split_start([([sbuf],) + _copies_gather_chips(1)], "ag_small_start")
    rs_finish(pending[0], tok)
    rs_finish(pending[1], tok)
    bufs = _split_wait(st[0], _copies_gather_chips(1)[0], [o[0] for o in adam_out.values()], "ag_small_wait1")
    st, tok = _split_start([(bufs,) + _copies_gather_forward(1)], "ag_small_fwd")
    (gathered_small,) = _split_wait(st[0], _copies_gather_forward(1)[0], tok, "ag_small_wait2")
    total = _sum_slots(gathered_small, "sum_small")
    sgrads = dict(zip(snames, _unpack(total, [p.shape for p in sparts])))
    for n, width in (("conv_w", conv_w.shape[2]), ("ffn_conv_w", ffn_conv_w.shape[2])):
        sgrads[n] = lax.dynamic_slice_in_dim(sgrads[n], me * width, width, axis=2)
    sw = dict(norm1_g=(norm1_g, m_norm1_g, v_norm1_g), pool_scale=(pool_scale, m_pool_scale, v_pool_scale),
              sgu_norm_g=(sgu_norm_g, m_sgu_norm_g, v_sgu_norm_g), sgu_w=(sgu_w, m_sgu_w, v_sgu_w),
              sgu_b=(sgu_b, m_sgu_b, v_sgu_b), conv_w=(conv_w, m_conv_w, v_conv_w),
              norm2_g=(norm2_g, m_norm2_g, v_norm2_g), ffn_conv_w=(ffn_conv_w, m_ffn_conv_w, v_ffn_conv_w),
              final_g=(final_g, m_final_g, v_final_g))
    shapes = [sw[n][0].shape for n in snames]
    upd = _adamw_small(_pack([sw[n][0] for n in snames]), _pack([sgrads[n] for n in snames]),
                       _pack([sw[n][1] for n in snames]), _pack([sw[n][2] for n in snames]), "adamw_small")
    sdelta, sm, sv = (dict(zip(snames, _unpack(u, shapes))) for u in upd)
    res = {n: [sgrads[n], sdelta[n], sm[n], sv[n]] for n in snames}

    rs_finish(pending[2], [upd[0]] + [o[0] for o in adam_out.values()])
    for n, outs in adam_out.items():
        res[n] = [o.reshape(wmv[n][0].shape) for o in outs]

    order = ["norm1_g", "w_in", "w_pool", "pool_scale", "sgu_norm_g", "sgu_w", "sgu_b", "conv_w", "w_pool_out",
             "w_sgu_out", "w_conv_out", "w_o", "norm2_g", "w_up", "ffn_conv_w", "w_down", "final_g"]
    return (loss, grad_x) + tuple(res[n][k] for k in range(4) for n in order)
```

```python
import functools

import jax
import jax.numpy as jnp
from jax import lax
from jax.experimental import pallas as pl
from jax.experimental.pallas import tpu as pltpu

BF16 = jnp.bfloat16
F32 = jnp.float32
EPS = 1e-6
MIB = 1024 * 1024
V7X_VMEM_BYTES = 64 * MIB
VMEM_LIMIT = 56 * MIB
LANE = 128
N_DEV = 8


def _params(sem, **kw):
    return pltpu.CompilerParams(dimension_semantics=sem, vmem_limit_bytes=VMEM_LIMIT, **kw)


def _pick(n, cands):
    for c in cands:
        if n % c == 0:
            return c
    return n


_DIMS = {"nn": (((1,), (0,)), ((), ())), "nt": (((1,), (1,)), ((), ())), "tn": (((0,), (0,)), ((), ()))}


def _mm(a, b, *, mode, out_dtype, name, tm, tn, tk, res=None, prev=None, deps=(), cols=(0, 1), b_koff=0,
        out_off=0, out_n=None, a2=None):
    if mode == "nn":
        M, Kc = a.shape
        nj = (b.shape[1] // tn - cols[0] + cols[1] - 1) // cols[1]
        out_w = b.shape[1]
    elif mode == "nt":
        M, Kc = a.shape
        nj = b.shape[0] // tn
        out_w = b.shape[0]
    else:
        Kc, M = a.shape
        nj = b.shape[1] // tn
        out_w = out_n or b.shape[1]
    nk1 = Kc // tk
    nk = nk1 + (a2.shape[1] // tk if a2 is not None else 0)
    assert M % tm == 0 and Kc % tk == 0 and (a2 is None or mode == "nt"), (name, M, Kc, tm, tk)
    off, stride = cols
    if mode == "nn":
        a_spec = pl.BlockSpec((tm, tk), lambda i, j, k: (i, k))
        b_spec = pl.BlockSpec((tk, tn), lambda i, j, k: (k, off + j * stride))
        o_spec = pl.BlockSpec((tm, tn), lambda i, j, k: (i, off + j * stride))
    elif mode == "nt":
        a_spec = pl.BlockSpec((tm, tk), lambda i, j, k: (i, jnp.minimum(k, nk1 - 1)))
        a2_spec = pl.BlockSpec((tm, tk), lambda i, j, k: (i, jnp.maximum(k - nk1, 0)))
        b_spec = pl.BlockSpec((tn, tk), lambda i, j, k: (j, b_koff + k))
        o_spec = pl.BlockSpec((tm, tn), lambda i, j, k: (i, j))
    else:
        a_spec = pl.BlockSpec((tk, tm), lambda i, j, k: (k, i))
        b_spec = pl.BlockSpec((tk, tn), lambda i, j, k: (k, j))
        o_spec = pl.BlockSpec((tm, tn), lambda i, j, k: (i, out_off + j))
    dims = _DIMS[mode]
    has_res = res is not None
    has_prev = prev is not None
    n_in = 2 + has_res + has_prev + len(deps) + (a2 is not None)

    def body(*refs):
        a_ref, b_ref = refs[0], refs[1]
        o_ref = refs[n_in]

        def finish(acc):
            if has_res:
                acc = acc + refs[2][...]
            o_ref[...] = acc.astype(o_ref.dtype)

        if nk == 1:
            finish(lax.dot_general(a_ref[...], b_ref[...], dims, preferred_element_type=F32))
        else:
            acc_ref = refs[n_in + 1]
            k = pl.program_id(2)

            @pl.when(k == 0)
            def _():
                acc_ref[...] = jnp.zeros(acc_ref.shape, F32)

            if a2 is None:
                acc_ref[...] += lax.dot_general(a_ref[...], b_ref[...], dims, preferred_element_type=F32)
            else:
                @pl.when(k < nk1)
                def _():
                    acc_ref[...] += lax.dot_general(a_ref[...], b_ref[...], dims, preferred_element_type=F32)

                @pl.when(k >= nk1)
                def _():
                    acc_ref[...] += lax.dot_general(refs[n_in - 1][...], b_ref[...], dims, preferred_element_type=F32)

            @pl.when(k == nk - 1)
            def _():
                finish(acc_ref[...])

    in_specs = [a_spec, b_spec]
    args = [a, b]
    if has_res:
        in_specs.append(pl.BlockSpec((tm, tn), lambda i, j, k: (i, j)))
        args.append(res)
    aliases = {}
    if has_prev:
        aliases = {len(args): 0}
        in_specs.append(pl.BlockSpec(memory_space=pl.ANY))
        args.append(prev)
    for d in deps:
        in_specs.append(pl.BlockSpec(memory_space=pl.ANY))
        args.append(d)
    if a2 is not None:
        in_specs.append(a2_spec)
        args.append(a2)
    return pl.pallas_call(
        body,
        name=name,
        grid=(M // tm, nj, nk),
        in_specs=in_specs,
        out_specs=o_spec,
        out_shape=jax.ShapeDtypeStruct((M, out_w), out_dtype),
        scratch_shapes=[pltpu.VMEM((tm, tn), F32)] if nk > 1 else [],
        input_output_aliases=aliases,
        compiler_params=_params(("parallel", "parallel", "arbitrary")),
    )(*args)


def _mm_res_rms(a, b, res, g, name, deps=()):
    M, K = a.shape
    N = b.shape[1]
    tm = _pick(M, (512, 256, 128))

    def body(a_ref, b_ref, res_ref, g_ref, *rest):
        x_ref, h_ref = rest[-2:]
        xv = res_ref[...] + jnp.dot(a_ref[...], b_ref[...], preferred_element_type=F32)
        x_ref[...] = xv
        r = lax.rsqrt(jnp.mean(xv * xv, axis=-1, keepdims=True) + EPS)
        h_ref[...] = (xv * r * g_ref[...]).astype(BF16)

    row = pl.BlockSpec((tm, N), lambda i: (i, 0))
    return pl.pallas_call(
        body, name=name, grid=(M // tm,),
        in_specs=[pl.BlockSpec((tm, K), lambda i: (i, 0)), pl.BlockSpec((K, N), lambda i: (0, 0)), row,
                  pl.BlockSpec((1, N), lambda i: (0, 0))] + [pl.BlockSpec(memory_space=pl.ANY)] * len(deps),
        out_specs=[row, row],
        out_shape=[jax.ShapeDtypeStruct((M, N), F32), jax.ShapeDtypeStruct((M, N), BF16)],
        compiler_params=_params(("parallel",)),
    )(a, b, res, g, *deps)


HALO = 16
CH = 256


def _fill_pad(pad_ref, chunk_fn, T):
    z = jnp.zeros((HALO, pad_ref.shape[1]), F32)
    pad_ref[pl.ds(0, HALO), :] = z
    pad_ref[pl.ds(HALO + T, HALO), :] = z

    def body(c, carry):
        r0 = pl.multiple_of(c * CH, CH)
        pad_ref[pl.ds(HALO + r0, CH), :] = chunk_fn(r0)
        return carry

    lax.fori_loop(0, T // CH, body, 0)


def _ext(pad_ref, r0):
    return pad_ref[pl.ds(r0, CH + 2 * HALO), :]


def _tap(ext, o):
    if o == 0:
        return ext[HALO:HALO + CH]
    return pltpu.roll(ext, (-o) % ext.shape[0], axis=0)[HALO:HALO + CH]


def _chunks(T, fn, init=0):
    def body(c, carry):
        return fn(pl.multiple_of(c * CH, CH), carry)

    return lax.fori_loop(0, T // CH, body, init)


def _conv3(ext, w):
    return _tap(ext, -1) * w[0:1] + _tap(ext, 0) * w[1:2] + _tap(ext, 1) * w[2:3]


def _conv3_t(ext, w):
    return _tap(ext, 1) * w[0:1] + _tap(ext, 0) * w[1:2] + _tap(ext, -1) * w[2:3]


def _sigmoid(x):
    return 0.5 * (jnp.tanh(0.5 * x) + 1.0)


def _silu(x):
    h = 0.5 * x
    return h * (jnp.tanh(h) + 1.0)


_GELU_C = 0.7978845608028654


def _gelu(x):
    return 0.5 * x * (1.0 + jnp.tanh(_GELU_C * (x + 0.044715 * (x * x * x))))


def _gelu_grad(x):
    t = jnp.tanh(_GELU_C * (x + 0.044715 * (x * x * x)))
    return 0.5 * (1.0 + t) + 0.5 * x * (1.0 - t * t) * (_GELU_C * (1.0 + 3.0 * 0.044715 * (x * x)))


def _row_tile(T):
    return _pick(T, (256, 128))


def _rms_fwd(x, g, name):
    T, D = x.shape
    tm = _row_tile(T)

    def body(x_ref, g_ref, h_ref):
        xv = x_ref[...]
        r = lax.rsqrt(jnp.mean(xv * xv, axis=-1, keepdims=True) + EPS)
        h_ref[...] = (xv * r * g_ref[...]).astype(BF16)

    return pl.pallas_call(
        body, name=name, grid=(T // tm,),
        in_specs=[pl.BlockSpec((tm, D), lambda i: (i, 0)), pl.BlockSpec((1, D), lambda i: (0, 0))],
        out_specs=pl.BlockSpec((tm, D), lambda i: (i, 0)),
        out_shape=jax.ShapeDtypeStruct((T, D), BF16),
        compiler_params=_params(("parallel",)),
    )(x, g)


def _rms_bwd(dh, x, g, dres, name, deps=()):
    T, D = x.shape
    tm = _row_tile(T)

    def body(dh_ref, x_ref, g_ref, dres_ref, *rest):
        dx_ref, dxb_ref, dg_ref = rest[len(deps):]
        i = pl.program_id(0)
        xv = x_ref[...]
        r = lax.rsqrt(jnp.mean(xv * xv, axis=-1, keepdims=True) + EPS)
        n = xv * r
        dh_v = dh_ref[...].astype(F32)
        dn = dh_v * g_ref[...]
        dx = dres_ref[...] + r * (dn - n * jnp.mean(dn * n, axis=-1, keepdims=True))
        dx_ref[...] = dx
        dxb_ref[...] = dx.astype(BF16)
        dg = jnp.sum(dh_v * n, axis=0, keepdims=True)

        @pl.when(i == 0)
        def _():
            dg_ref[...] = dg

        @pl.when(i > 0)
        def _():
            dg_ref[...] += dg

    row = pl.BlockSpec((tm, D), lambda i: (i, 0))
    vec = pl.BlockSpec((1, D), lambda i: (0, 0))
    return pl.pallas_call(
        body, name=name, grid=(T // tm,),
        in_specs=[row, row, vec, row] + [pl.BlockSpec(memory_space=pl.ANY)] * len(deps),
        out_specs=[row, row, vec],
        out_shape=[jax.ShapeDtypeStruct((T, D), F32), jax.ShapeDtypeStruct((T, D), BF16),
                   jax.ShapeDtypeStruct((1, D), F32)],
        compiler_params=_params(("arbitrary",)),
    )(dh, x, g, dres, *deps)


def _loss_head(x, tgt, g, name):
    T, D = x.shape
    tm = _row_tile(T)

    def body(x_ref, t_ref, g_ref, dx_ref, dxb_ref, dg_ref, l_ref):
        i = pl.program_id(0)
        xv = x_ref[...]
        gv = g_ref[...]
        r = lax.rsqrt(jnp.mean(xv * xv, axis=-1, keepdims=True) + EPS)
        n = xv * r
        e = n * gv - t_ref[...]
        dy = e * (1.0 / D)
        dn = dy * gv
        dx = r * (dn - n * jnp.mean(dn * n, axis=-1, keepdims=True))
        dx_ref[...] = dx
        dxb_ref[...] = dx.astype(BF16)
        dg = jnp.sum(dy * n, axis=0, keepdims=True)
        per_tok = jnp.mean(e * e, axis=-1, keepdims=True)
        lv = jnp.broadcast_to(0.5 * jnp.sum(per_tok, axis=0, keepdims=True), (1, LANE))

        @pl.when(i == 0)
        def _():
            dg_ref[...] = dg
            l_ref[...] = lv

        @pl.when(i > 0)
        def _():
            dg_ref[...] += dg
            l_ref[...] += lv

    row = pl.BlockSpec((tm, D), lambda i: (i, 0))
    vec = pl.BlockSpec((1, D), lambda i: (0, 0))
    return pl.pallas_call(
        body, name=name, grid=(T // tm,),
        in_specs=[row, row, vec],
        out_specs=[row, row, vec, pl.BlockSpec((1, LANE), lambda i: (0, 0))],
        out_shape=[jax.ShapeDtypeStruct((T, D), F32), jax.ShapeDtypeStruct((T, D), BF16),
                   jax.ShapeDtypeStruct((1, D), F32), jax.ShapeDtypeStruct((1, LANE), F32)],
        compiler_params=_params(("arbitrary",)),
    )(x, tgt, g)


POOL_WINDOWS = (2, 4, 8, 16)


def _pool_offsets(w):
    return range(-(w // 2), w - w // 2)


def _pool_cnt(r0, w, T, shape):
    t = r0 + lax.broadcasted_iota(jnp.int32, shape, 0)
    lo = jnp.maximum(t - w // 2, 0)
    hi = jnp.minimum(t + (w - w // 2 - 1), T - 1)
    return (hi - lo + 1).astype(F32)


def _pooled(pad_ref, r0, w, T):
    ext = _ext(pad_ref, r0)
    s = None
    for o in _pool_offsets(w):
        tap = _tap(ext, o)
        s = tap if s is None else s + tap
    cur = ext[HALO:HALO + CH]
    return s / _pool_cnt(r0, w, T, cur.shape) - cur


def _pool_fwd(P, w_pool, scale, name):
    T = P.shape[0]
    G, PG, _ = w_pool.shape

    def body(a_ref, w_ref, s_ref, o_ref, pad_ref):
        g = pl.program_id(0)
        _fill_pad(pad_ref, lambda r0: a_ref[pl.ds(r0, CH), :].astype(F32), T)
        for gi, w in enumerate(POOL_WINDOWS):
            @pl.when(g == gi)
            def _(w=w):
                def chunk(r0, carry):
                    pooled = _pooled(pad_ref, r0, w, T).astype(BF16)
                    y = jnp.dot(pooled, w_ref[...], preferred_element_type=F32) * s_ref[...]
                    o_ref[pl.ds(r0, CH), :] = y.astype(BF16)
                    return carry

                _chunks(T, chunk)

    return pl.pallas_call(
        body, name=name, grid=(G,),
        in_specs=[pl.BlockSpec((T, PG), lambda g: (0, g)),
                  pl.BlockSpec((None, PG, PG), lambda g: (g, 0, 0)),
                  pl.BlockSpec((1, PG), lambda g: (0, g))],
        out_specs=pl.BlockSpec((T, PG), lambda g: (0, g)),
        out_shape=jax.ShapeDtypeStruct((T, G * PG), BF16),
        scratch_shapes=[pltpu.VMEM((T + 2 * HALO, PG), F32)],
        compiler_params=_params(("parallel",)),
    )(P, w_pool, scale)


def _pool_bwd(P, dy, w_pool, scale, name):
    T = P.shape[0]
    G, PG, _ = w_pool.shape

    def body(a_ref, dy_ref, w_ref, s_ref, da_ref, dw_ref, ds_ref, pad_ref, gpad_ref):
        g = pl.program_id(0)
        gpad_ref[pl.ds(0, HALO), :] = jnp.zeros((HALO, PG), F32)
        gpad_ref[pl.ds(HALO + T, HALO), :] = jnp.zeros((HALO, PG), F32)
        _fill_pad(pad_ref, lambda r0: a_ref[pl.ds(r0, CH), :].astype(F32), T)
        for gi, w in enumerate(POOL_WINDOWS):
            @pl.when(g == gi)
            def _(w=w):
                def chunk1(r0, carry):
                    dw, ds = carry
                    pooled = _pooled(pad_ref, r0, w, T).astype(BF16)
                    ypre = jnp.dot(pooled, w_ref[...], preferred_element_type=F32)
                    dyv = dy_ref[pl.ds(r0, CH), :].astype(F32)
                    ds = ds + jnp.sum(dyv * ypre, axis=0, keepdims=True)
                    dyp = (dyv * s_ref[...]).astype(BF16)
                    dw = dw + lax.dot_general(pooled, dyp, _DIMS["tn"], preferred_element_type=F32)
                    dpool = lax.dot_general(dyp, w_ref[...], _DIMS["nt"], preferred_element_type=F32)
                    gpad_ref[pl.ds(HALO + r0, CH), :] = dpool / _pool_cnt(r0, w, T, dpool.shape)
                    return dw, ds

                dw, ds = _chunks(T, chunk1, (jnp.zeros((PG, PG), F32), jnp.zeros((1, PG), F32)))
                dw_ref[...] = dw
                ds_ref[...] = ds

                def chunk2(r0, carry):
                    ext = _ext(gpad_ref, r0)
                    cur = ext[HALO:HALO + CH]
                    acc = None
                    for o in _pool_offsets(w):
                        tap = _tap(ext, -o)
                        acc = tap if acc is None else acc + tap
                    da_ref[pl.ds(r0, CH), :] = (acc - cur * _pool_cnt(r0, w, T, cur.shape)).astype(BF16)
                    return carry

                _chunks(T, chunk2)

    col = pl.BlockSpec((T, PG), lambda g: (0, g))
    return pl.pallas_call(
        body, name=name, grid=(G,),
        in_specs=[col, col, pl.BlockSpec((None, PG, PG), lambda g: (g, 0, 0)), pl.BlockSpec((1, PG), lambda g: (0, g))],
        out_specs=[col, pl.BlockSpec((None, PG, PG), lambda g: (g, 0, 0)), pl.BlockSpec((1, PG), lambda g: (0, g))],
        out_shape=[jax.ShapeDtypeStruct((T, G * PG), BF16), jax.ShapeDtypeStruct((G, PG, PG), F32),
                   jax.ShapeDtypeStruct((1, G * PG), F32)],
        scratch_shapes=[pltpu.VMEM((T + 2 * HALO, PG), F32), pltpu.VMEM((T + 2 * HALO, PG), F32)],
        compiler_params=_params(("parallel",)),
    )(P, dy, w_pool, scale)


SGU_CHUNK = 128


def _sgu_common(u_ref, v_ref, gs_ref):
    up = u_ref[...].astype(F32)
    vp = v_ref[...].astype(F32)
    gv = _gelu(vp)
    rv = lax.rsqrt(jnp.mean(gv * gv, axis=-1, keepdims=True) + EPS)
    nrm = gv * rv
    return up, vp, nrm, rv, (nrm * gs_ref[...]).astype(BF16)


def _sgu_fwd(P, sgu_w, b_exp, gs, name):
    T = P.shape[0]
    G, _, SG = b_exp.shape
    DP = G * SG
    tm = _row_tile(T)

    def body(u_ref, v_ref, w_ref, b_ref, gs_ref, o_ref):
        up, _, _, _, vn = _sgu_common(u_ref, v_ref, gs_ref)
        gu = _gelu(up)
        for n in range(tm // SGU_CHUNK):
            rows = slice(n * SGU_CHUNK, (n + 1) * SGU_CHUNK)
            for g in range(G):
                cols = slice(g * SG, (g + 1) * SG)
                z = jnp.dot(w_ref[g], vn[rows, cols], preferred_element_type=F32) + b_ref[g]
                o_ref[rows, cols] = (gu[rows, cols] * z).astype(BF16)

    return pl.pallas_call(
        body, name=name, grid=(T // tm,),
        in_specs=[pl.BlockSpec((tm, DP), lambda i: (i, 1)), pl.BlockSpec((tm, DP), lambda i: (i, 2)),
                  pl.BlockSpec(sgu_w.shape, lambda i: (0, 0, 0)), pl.BlockSpec(b_exp.shape, lambda i: (0, 0, 0)),
                  pl.BlockSpec((1, DP), lambda i: (0, 0))],
        out_specs=pl.BlockSpec((tm, DP), lambda i: (i, 0)),
        out_shape=jax.ShapeDtypeStruct((T, DP), BF16),
        compiler_params=_params(("parallel",)),
    )(P, P, sgu_w, b_exp, gs)


def _sgu_bwd(P, dy, da, sgu_w, b_exp, gs, name):
    T = P.shape[0]
    G, _, SG = b_exp.shape
    DP = G * SG
    tm = _row_tile(T)

    def body(u_ref, v_ref, dy_ref, da_ref, w_ref, b_ref, gs_ref, o_ref, dw_ref, db_ref, dgs_ref, dzs_ref):
        i = pl.program_id(0)
        up, vp, nrm, rv, vn = _sgu_common(u_ref, v_ref, gs_ref)
        gu = _gelu(up)
        dyv = dy_ref[...].astype(F32)
        o_ref[:, 0:DP] = da_ref[...]

        @pl.when(i == 0)
        def _():
            dw_ref[...] = jnp.zeros(dw_ref.shape, F32)
            dzs_ref[...] = jnp.zeros(dzs_ref.shape, F32)
            dgs_ref[...] = jnp.zeros(dgs_ref.shape, F32)

        dgs = jnp.zeros((1, DP), F32)
        for n in range(tm // SGU_CHUNK):
            rows = slice(n * SGU_CHUNK, (n + 1) * SGU_CHUNK)
            dvn_parts = []
            for g in range(G):
                cols = slice(g * SG, (g + 1) * SG)
                vng = vn[rows, cols]
                z = jnp.dot(w_ref[g], vng, preferred_element_type=F32) + b_ref[g]
                dyg = dyv[rows, cols]
                du = dyg * z
                o_ref[rows, DP + g * SG:DP + (g + 1) * SG] = (du * _gelu_grad(up[rows, cols])).astype(BF16)
                dz = dyg * gu[rows, cols]
                dzb = dz.astype(BF16)
                dzs_ref[g] += dz
                dw_ref[g] += lax.dot_general(dzb, vng, _DIMS["nt"], preferred_element_type=F32)
                dvn_parts.append(lax.dot_general(w_ref[g], dzb, _DIMS["tn"], preferred_element_type=F32))
            dvn = jnp.concatenate(dvn_parts, axis=1)
            nr = nrm[rows]
            dgs = dgs + jnp.sum(dvn * nr, axis=0, keepdims=True)
            dn = dvn * gs_ref[...]
            dgv = rv[rows] * (dn - nr * jnp.mean(dn * nr, axis=-1, keepdims=True))
            o_ref[rows, 2 * DP:3 * DP] = (dgv * _gelu_grad(vp[rows])).astype(BF16)
        dgs_ref[...] += dgs

        @pl.when(i == T // tm - 1)
        def _():
            for g in range(G):
                db_ref[g] = jnp.broadcast_to(jnp.sum(dzs_ref[g], axis=1, keepdims=True), (SGU_CHUNK, SG))

    full3 = lambda a: pl.BlockSpec(a.shape, lambda i: (0, 0, 0))
    return pl.pallas_call(
        body, name=name, grid=(T // tm,),
        in_specs=[pl.BlockSpec((tm, DP), lambda i: (i, 1)), pl.BlockSpec((tm, DP), lambda i: (i, 2)),
                  pl.BlockSpec((tm, DP), lambda i: (i, 0)), pl.BlockSpec((tm, DP), lambda i: (i, 0)),
                  full3(sgu_w), full3(b_exp), pl.BlockSpec((1, DP), lambda i: (0, 0))],
        out_specs=[pl.BlockSpec((tm, 3 * DP), lambda i: (i, 0)), full3(sgu_w), full3(b_exp),
                   pl.BlockSpec((1, DP), lambda i: (0, 0))],
        out_shape=[jax.ShapeDtypeStruct((T, 6 * DP), BF16), jax.ShapeDtypeStruct(sgu_w.shape, F32),
                   jax.ShapeDtypeStruct(b_exp.shape, F32), jax.ShapeDtypeStruct((1, DP), F32)],
        scratch_shapes=[pltpu.VMEM(b_exp.shape, F32)],
        compiler_params=_params(("arbitrary",)),
    )(P, P, dy, da, sgu_w, b_exp, gs)


def _conv_fwd(P, cw, name):
    T = P.shape[0]
    DP = cw.shape[1]
    tc = 256
    nb = DP // tc

    def body(xc_ref, bg_ref, cg_ref, w_ref, o_ref, pad_ref):
        _fill_pad(pad_ref, lambda r0: cg_ref[pl.ds(r0, CH), :].astype(F32) * xc_ref[pl.ds(r0, CH), :].astype(F32), T)
        w = w_ref[...]

        def chunk(r0, carry):
            cq = _conv3(_ext(pad_ref, r0), w)
            o_ref[pl.ds(r0, CH), :] = (bg_ref[pl.ds(r0, CH), :].astype(F32) * cq).astype(BF16)
            return carry

        _chunks(T, chunk)

    col = lambda off: pl.BlockSpec((T, tc), lambda j: (0, off * nb + j))
    return pl.pallas_call(
        body, name=name, grid=(nb,),
        in_specs=[col(3), col(4), col(5), pl.BlockSpec((3, tc), lambda j: (0, j))],
        out_specs=pl.BlockSpec((T, tc), lambda j: (0, j)),
        out_shape=jax.ShapeDtypeStruct((T, DP), BF16),
        scratch_shapes=[pltpu.VMEM((T + 2 * HALO, tc), F32)],
        compiler_params=_params(("parallel",)),
    )(P, P, P, cw)


def _conv_bwd(P, dy, cw, dPl, name):
    T = P.shape[0]
    DP = cw.shape[1]
    tc = 256
    nb = DP // tc

    def body(xc_ref, bg_ref, cg_ref, dy_ref, w_ref, prev_ref, o_ref, dw_ref, qpad_ref, dpad_ref):
        del prev_ref
        seg = pl.program_id(1)
        w = w_ref[...]
        rows = lambda ref, r0: ref[pl.ds(r0, CH), :].astype(F32)

        @pl.when(seg == 0)
        def _():
            _fill_pad(qpad_ref, lambda r0: rows(cg_ref, r0) * rows(xc_ref, r0), T)
            _fill_pad(dpad_ref, lambda r0: rows(dy_ref, r0) * rows(bg_ref, r0), T)

        @pl.when(seg == 0)
        def _():
            def chunk(r0, carry):
                dext = _ext(dpad_ref, r0)
                taps = [_tap(dext, 1), _tap(dext, 0), _tap(dext, -1)]
                dq = taps[0] * w[0:1] + taps[1] * w[1:2] + taps[2] * w[2:3]
                o_ref[pl.ds(r0, CH), :] = (dq * rows(cg_ref, r0)).astype(BF16)
                q = qpad_ref[pl.ds(HALO + r0, CH), :]
                return [a + jnp.sum(q * t, axis=0, keepdims=True) for a, t in zip(carry, taps)]

            dws = _chunks(T, chunk, [jnp.zeros((1, tc), F32)] * 3)
            dw_ref[...] = jnp.concatenate(dws + [jnp.zeros((5, tc), F32)], axis=0)

        @pl.when(seg == 1)
        def _():
            def chunk(r0, carry):
                cq = _conv3(_ext(qpad_ref, r0), w)
                o_ref[pl.ds(r0, CH), :] = (rows(dy_ref, r0) * cq).astype(BF16)
                return carry

            _chunks(T, chunk)

        @pl.when(seg == 2)
        def _():
            def chunk(r0, carry):
                dq = _conv3_t(_ext(dpad_ref, r0), w)
                o_ref[pl.ds(r0, CH), :] = (dq * rows(xc_ref, r0)).astype(BF16)
                return carry

            _chunks(T, chunk)

    col = lambda off: pl.BlockSpec((T, tc), lambda j, s: (0, off * nb + j))
    return pl.pallas_call(
        body, name=name, grid=(nb, 3),
        in_specs=[col(3), col(4), col(5), pl.BlockSpec((T, tc), lambda j, s: (0, j)),
                  pl.BlockSpec((3, tc), lambda j, s: (0, j)), pl.BlockSpec(memory_space=pl.ANY)],
        out_specs=[pl.BlockSpec((T, tc), lambda j, s: (0, (3 + s) * nb + j)), pl.BlockSpec((8, tc), lambda j, s: (0, j))],
        out_shape=[jax.ShapeDtypeStruct(dPl.shape, BF16), jax.ShapeDtypeStruct((8, DP), F32)],
        scratch_shapes=[pltpu.VMEM((T + 2 * HALO, tc), F32), pltpu.VMEM((T + 2 * HALO, tc), F32)],
        input_output_aliases={5: 0},
        compiler_params=_params(("parallel", "arbitrary")),
    )(P, P, P, dy, cw, dPl)


def _merge_fwd(P, ys, ws, name):
    T = P.shape[0]
    DP, D = ws[0].shape
    tm = _pick(T, (512, 256, 128))
    tn = _pick(D, (1024, 512, 256))
    goff = (6 * DP) // tn

    def body(ya, yb, yc, wa, wb, wc, ga, gb, gc, oa_ref, ob_ref, oc_ref, m_ref):
        m = None
        for y, w, g, o_ref in ((ya, wa, ga, oa_ref), (yb, wb, gb, ob_ref), (yc, wc, gc, oc_ref)):
            o = jnp.dot(y[...], w[...], preferred_element_type=F32)
            o_ref[...] = o.astype(BF16)
            t = _sigmoid(g[...].astype(F32)) * o
            m = t if m is None else m + t
        m_ref[...] = m.astype(BF16)

    yspec = pl.BlockSpec((tm, DP), lambda i, j: (i, 0))
    wspec = pl.BlockSpec((DP, tn), lambda i, j: (0, j))
    gspec = lambda b: pl.BlockSpec((tm, tn), lambda i, j: (i, goff + b * (D // tn) + j))
    ospec = pl.BlockSpec((tm, tn), lambda i, j: (i, j))
    return pl.pallas_call(
        body, name=name, grid=(T // tm, D // tn),
        in_specs=[yspec] * 3 + [wspec] * 3 + [gspec(0), gspec(1), gspec(2)],
        out_specs=[ospec] * 4,
        out_shape=[jax.ShapeDtypeStruct((T, D), BF16)] * 4,
        compiler_params=_params(("parallel", "parallel")),
    )(*ys, *ws, P, P, P)


def _merge_bwd(P, dM, os_, name):
    T, D = dM.shape
    tm = _row_tile(T)

    def body(dm_ref, oa, ob, oc, g_ref, da_ref, db_ref, dc_ref, dg_ref):
        dm = dm_ref[...].astype(F32)
        for b, (o_ref, d_ref) in enumerate(((oa, da_ref), (ob, db_ref), (oc, dc_ref))):
            s = _sigmoid(g_ref[:, b * D:(b + 1) * D].astype(F32))
            d_ref[...] = (dm * s).astype(BF16)
            dg_ref[:, b * D:(b + 1) * D] = (dm * o_ref[...].astype(F32) * s * (1.0 - s)).astype(BF16)

    row = pl.BlockSpec((tm, D), lambda i: (i, 0))
    return pl.pallas_call(
        body, name=name, grid=(T // tm,),
        in_specs=[row] * 4 + [pl.BlockSpec((tm, 3 * D), lambda i: (i, 1))],
        out_specs=[row] * 3 + [pl.BlockSpec((tm, 3 * D), lambda i: (i, 0))],
        out_shape=[jax.ShapeDtypeStruct((T, D), BF16)] * 3 + [jax.ShapeDtypeStruct((T, 3 * D), BF16)],
        compiler_params=_params(("parallel",)),
    )(dM, *os_, P)


def _glu_fwd(U, fw, name):
    T, F2 = U.shape
    F = F2 // 2
    tc = 256
    nb = F // tc

    def body(ug_ref, uv_ref, wg_ref, wv_ref, o_ref, gpad_ref, vpad_ref):
        _fill_pad(gpad_ref, lambda r0: ug_ref[pl.ds(r0, CH), :].astype(F32), T)
        _fill_pad(vpad_ref, lambda r0: uv_ref[pl.ds(r0, CH), :].astype(F32), T)
        wg, wv = wg_ref[...], wv_ref[...]

        def chunk(r0, carry):
            gate = _conv3(_ext(gpad_ref, r0), wg)
            val = _conv3(_ext(vpad_ref, r0), wv)
            o_ref[pl.ds(r0, CH), :] = (_silu(gate) * val).astype(BF16)
            return carry

        _chunks(T, chunk)

    return pl.pallas_call(
        body, name=name, grid=(nb,),
        in_specs=[pl.BlockSpec((T, tc), lambda j: (0, j)), pl.BlockSpec((T, tc), lambda j: (0, nb + j)),
                  pl.BlockSpec((3, tc), lambda j: (0, j)), pl.BlockSpec((3, tc), lambda j: (0, nb + j))],
        out_specs=pl.BlockSpec((T, tc), lambda j: (0, j)),
        out_shape=jax.ShapeDtypeStruct((T, F), BF16),
        scratch_shapes=[pltpu.VMEM((T + 2 * HALO, tc), F32)] * 2,
        compiler_params=_params(("parallel",)),
    )(U, U, fw, fw)


def _glu_bwd(U, dact, fw, name):
    T, F2 = U.shape
    F = F2 // 2
    tc = 128
    nb = F // tc

    def body(ug_ref, uv_ref, da_ref, wg_ref, wv_ref, dug_ref, duv_ref, dwg_ref, dwv_ref, gpad, vpad, dgpad, dvpad):
        _fill_pad(gpad, lambda r0: ug_ref[pl.ds(r0, CH), :].astype(F32), T)
        _fill_pad(vpad, lambda r0: uv_ref[pl.ds(r0, CH), :].astype(F32), T)
        wg, wv = wg_ref[...], wv_ref[...]

        def chunk1(r0, carry):
            gate = _conv3(_ext(gpad, r0), wg)
            val = _conv3(_ext(vpad, r0), wv)
            s = _sigmoid(gate)
            da = da_ref[pl.ds(r0, CH), :].astype(F32)
            dgpad[pl.ds(HALO + r0, CH), :] = da * val * (s * (1.0 + gate * (1.0 - s)))
            dvpad[pl.ds(HALO + r0, CH), :] = da * (gate * s)
            return carry

        z = jnp.zeros((HALO, tc), F32)
        for p in (dgpad, dvpad):
            p[pl.ds(0, HALO), :] = z
            p[pl.ds(HALO + T, HALO), :] = z
        _chunks(T, chunk1)

        def chunk2(r0, carry):
            new = []
            for pad, dpad, wk, o_ref in ((gpad, dgpad, wg, dug_ref), (vpad, dvpad, wv, duv_ref)):
                dext = _ext(dpad, r0)
                taps = [_tap(dext, 1), _tap(dext, 0), _tap(dext, -1)]
                o_ref[pl.ds(r0, CH), :] = (taps[0] * wk[0:1] + taps[1] * wk[1:2] + taps[2] * wk[2:3]).astype(BF16)
                xs = pad[pl.ds(HALO + r0, CH), :]
                new += [jnp.sum(xs * t, axis=0, keepdims=True) for t in taps]
            return [a + b for a, b in zip(carry, new)]

        dws = _chunks(T, chunk2, [jnp.zeros((1, tc), F32)] * 6)
        dwg_ref[...] = jnp.concatenate(dws[:3] + [jnp.zeros((5, tc), F32)], axis=0)
        dwv_ref[...] = jnp.concatenate(dws[3:] + [jnp.zeros((5, tc), F32)], axis=0)

    lo = pl.BlockSpec((T, tc), lambda j: (0, j))
    hi = pl.BlockSpec((T, tc), lambda j: (0, nb + j))
    wlo = pl.BlockSpec((3, tc), lambda j: (0, j))
    whi = pl.BlockSpec((3, tc), lambda j: (0, nb + j))
    dwspec = pl.BlockSpec((8, tc), lambda j: (0, j))
    return pl.pallas_call(
        body, name=name, grid=(nb,),
        in_specs=[lo, hi, lo, wlo, whi],
        out_specs=[lo, lo, dwspec, dwspec],
        out_shape=[jax.ShapeDtypeStruct((T, F), BF16)] * 2 + [jax.ShapeDtypeStruct((8, F), F32)] * 2,
        scratch_shapes=[pltpu.VMEM((T + 2 * HALO, tc), F32)] * 4,
        compiler_params=_params(("parallel",)),
    )(U, U, dact, fw, fw)


_ANY = pl.BlockSpec(memory_space=pl.ANY)
_MESH = pl.DeviceIdType.MESH


def _mesh_pos():
    return lax.axis_index("x"), lax.axis_index("y"), lax.axis_index("c")


def _other_chips(x, y):
    return [(1 - x, y), (x, 1 - y), (1 - x, 1 - y)]


_HBM = pl.BlockSpec(memory_space=pltpu.HBM)
_SEM = pl.BlockSpec(memory_space=pltpu.SEMAPHORE)
_EFFECT = pltpu.SideEffectType.DATAFLOW_SIDE_EFFECTING


def _in_hbm(a):
    return pltpu.with_memory_space_constraint(a, pltpu.HBM)


def _split_start(groups, name, deps=()):
    flat, where = [], []
    for bufs, _, _ in groups:
        idx = []
        for b in bufs:
            hit = [i for i, f in enumerate(flat) if f is b]
            if not hit:
                flat.append(b)
            idx.append(hit[0] if hit else len(flat) - 1)
        where.append(idx)
    nb = len(flat)
    ng = len(groups)

    def body(*refs):
        ins = refs[:nb]
        sems = refs[nb + len(deps):nb + len(deps) + 2 * ng]
        token = refs[-1]
        for gi, (_, make_copies, _) in enumerate(groups):
            for cp in make_copies([ins[i] for i in where[gi]], sems[2 * gi], sems[2 * gi + 1]):
                cp.start()
        token[...] = jnp.zeros(token.shape, token.dtype)

    sem_shapes = []
    for _, _, n in groups:
        sem_shapes += [pltpu.SemaphoreType.DMA((n,)), pltpu.SemaphoreType.DMA((n,))]
    outs = pl.pallas_call(
        body, name=name,
        in_specs=[_HBM] * nb + [_ANY] * len(deps),
        out_specs=[_SEM] * (2 * ng) + [_HBM] * nb + [pl.BlockSpec(memory_space=pltpu.VMEM)],
        out_shape=sem_shapes + [pltpu.HBM(b.shape, b.dtype) for b in flat] + [jax.ShapeDtypeStruct((8, LANE), F32)],
        input_output_aliases={i: 2 * ng + i for i in range(nb)},
        compiler_params=pltpu.CompilerParams(has_side_effects=_EFFECT),
    )(*[_in_hbm(b) for b in flat], *deps)
    res = [(outs[2 * gi], outs[2 * gi + 1], [outs[2 * ng + i] for i in where[gi]]) for gi in range(ng)]
    return res, outs[-1]


def _split_wait(started, make_copies, after, name):
    send, recv, bufs = started
    nb = len(bufs)
    after = list(after) if isinstance(after, (list, tuple)) else [after]

    def body(*refs):
        for cp in make_copies(refs[:nb], refs[nb], refs[nb + 1]):
            cp.wait_send()
            cp.wait_recv()

    outs = pl.pallas_call(
        body, name=name,
        in_specs=[_HBM] * nb + [_SEM, _SEM] + [_ANY] * len(after),
        out_specs=[_HBM] * nb,
        out_shape=[pltpu.HBM(b.shape, b.dtype) for b in bufs],
        input_output_aliases={i: i for i in range(nb)},
        compiler_params=pltpu.CompilerParams(has_side_effects=_EFFECT),
    )(*bufs, send, recv, *after)
    return list(outs)


def _slot(buf, s, cols=None):
    if len(buf.shape) == 3:
        return buf.at[s] if cols is None else buf.at[s, :, pl.ds(cols[0], cols[1])]
    ns = buf.shape[1] // N_DEV
    first, width = (0, ns) if cols is None else cols
    return buf.at[:, pl.ds(pl.multiple_of(s * ns, LANE) + first, width)]


def _copies_gather_chips(nbuf, cols=None):
    def make(bufs, send, recv):
        x, y, c = _mesh_pos()
        targets = [(x, y, 1 - c)] + [(px, py, c) for px, py in _other_chips(x, y)]
        cps = []
        for a in range(nbuf):
            mine = _slot(bufs[a], 4 * x + 2 * y + c, cols)
            for k, to in enumerate(targets):
                cps.append(pltpu.make_async_remote_copy(src_ref=mine, dst_ref=mine, send_sem=send.at[4 * a + k],
                                                        recv_sem=recv.at[4 * a + k], device_id=to, device_id_type=_MESH))
        return cps

    return make, 4 * nbuf


def _copies_gather_forward(nbuf, cols=None):
    def make(bufs, send, recv):
        x, y, c = _mesh_pos()
        cps = []
        for a in range(nbuf):
            for j, (px, py) in enumerate(_other_chips(x, y)):
                slot = _slot(bufs[a], 4 * px + 2 * py + c, cols)
                cps.append(pltpu.make_async_remote_copy(src_ref=slot, dst_ref=slot, send_sem=send.at[3 * a + j],
                                                        recv_sem=recv.at[3 * a + j], device_id=(x, y, 1 - c),
                                                        device_id_type=_MESH))
        return cps

    return make, 3 * nbuf


def _copies_rs_sibling(n):
    def make(bufs, send, recv):
        x, y, c = _mesh_pos()
        cps = []
        for a in range(n):
            for q in range(4):
                cps.append(pltpu.make_async_remote_copy(
                    src_ref=_slot(bufs[a], 2 * q + (1 - c)), dst_ref=bufs[n + a].at[q], send_sem=send.at[4 * a + q],
                    recv_sem=recv.at[4 * a + q], device_id=(x, y, 1 - c), device_id_type=_MESH))
        return cps

    return make, 4 * n


def _copies_rs_chips(n):
    def make(bufs, send, recv):
        x, y, c = _mesh_pos()
        cps = []
        for a in range(n):
            for j, (px, py) in enumerate(_other_chips(x, y)):
                cps.append(pltpu.make_async_remote_copy(
                    src_ref=bufs[a].at[j], dst_ref=bufs[n + a].at[j], send_sem=send.at[3 * a + j],
                    recv_sem=recv.at[3 * a + j], device_id=(px, py, c), device_id_type=_MESH))
        return cps

    return make, 3 * n


def _cast_place(stacked, layer, slot, dtype, name, deps=(), by_cols=False):
    _, R, C = stacked.shape
    tr = _rows_tile(R, C)

    def body(slot_ref, x_ref, *rest):
        del slot_ref
        rest[-1][...] = x_ref[...].astype(dtype)

    if by_cols:
        out_spec = pl.BlockSpec((tr, C), lambda i, s: (i, s[0]))
        out_shape = jax.ShapeDtypeStruct((R, N_DEV * C), dtype)
    else:
        out_spec = pl.BlockSpec((None, tr, C), lambda i, s: (s[0], i, 0))
        out_shape = jax.ShapeDtypeStruct((N_DEV, R, C), dtype)
    return pl.pallas_call(
        body, name=name,
        grid_spec=pltpu.PrefetchScalarGridSpec(
            num_scalar_prefetch=1, grid=(R // tr,),
            in_specs=[pl.BlockSpec((None, tr, C), lambda i, s: (layer, i, 0))] + [_ANY] * len(deps),
            out_specs=out_spec),
        out_shape=out_shape,
        compiler_params=_params(("parallel",)),
    )(slot, stacked, *deps)


def _rows_tile(R, C, budget=2 * MIB):
    if R * C * 4 <= budget:
        return R
    for t in (2048, 1024, 704, 512, 352, 256, 128, 64, 32, 16, 8):
        if R % t == 0 and t * C * 4 <= budget:
            return t
    return R


def _chip_of_relation(j, pos):
    x, y = pos[0], pos[1]
    return 2 * jnp.where(j == 1, x, 1 - x) + jnp.where(j == 0, y, 1 - y)


def _pair_sum(grad, recv, pos, name):
    _, R, C = recv.shape
    tr = _rows_tile(R, C, 8 * MIB)

    def body(pos_ref, g_ref, r_ref, o_ref):
        del pos_ref
        o_ref[...] = (g_ref[...].astype(F32) + r_ref[...].astype(F32)).astype(BF16)

    if len(grad.shape) == 3:
        g_spec = pl.BlockSpec((None, tr, C), lambda j, i, p: (2 * _chip_of_relation(j, p) + p[2], i, 0))
    else:
        g_spec = pl.BlockSpec((tr, C), lambda j, i, p: (i, 2 * _chip_of_relation(j, p) + p[2]))
    return pl.pallas_call(
        body, name=name,
        grid_spec=pltpu.PrefetchScalarGridSpec(
            num_scalar_prefetch=1, grid=(3, R // tr),
            in_specs=[g_spec,
                      pl.BlockSpec((None, tr, C), lambda j, i, p: (_chip_of_relation(j, p), i, 0))],
            out_specs=pl.BlockSpec((None, tr, C), lambda j, i, p: (j, i, 0))),
        out_shape=jax.ShapeDtypeStruct((3, R, C), BF16),
        compiler_params=_params(("parallel", "parallel")),
    )(pos, grad, recv)


ADAM_LR = 0.001
ADAM_B1 = 0.9
ADAM_B2 = 0.999
ADAM_EPS = 1e-08
ADAM_WD = 0.01
ADAM_STEP = 10


def _adamw(w, g, m, v):
    m = ADAM_B1 * m + (1.0 - ADAM_B1) * g
    v = ADAM_B2 * v + (1.0 - ADAM_B2) * (g * g)
    m_hat = m / (1.0 - ADAM_B1 ** ADAM_STEP)
    v_hat = v / (1.0 - ADAM_B2 ** ADAM_STEP)
    delta = -ADAM_LR * (m_hat / (jnp.sqrt(v_hat) + ADAM_EPS) + ADAM_WD * w)
    return delta, m, v


def _adamw_big(grad, sib, recv, w, m, v, pos, layer, prev, name):
    L, R, C = w.shape
    tr = _rows_tile(R, C)
    has_prev = prev is not None

    def body(pos_ref, g_ref, s_ref, r0_ref, r1_ref, r2_ref, w_ref, m_ref, v_ref, *rest):
        del pos_ref
        go_ref, d_ref, mo_ref, vo_ref = rest[-4:]
        g = g_ref[...].astype(F32) + s_ref[...].astype(F32)
        g = ((g + r0_ref[...].astype(F32)) + r1_ref[...].astype(F32)) + r2_ref[...].astype(F32)
        delta, m2, v2 = _adamw(w_ref[...], g, m_ref[...], v_ref[...])
        go_ref[...] = g
        d_ref[...] = delta
        mo_ref[...] = m2
        vo_ref[...] = v2

    lay = pl.BlockSpec((None, tr, C), lambda i, p: (layer, i, 0))
    rspec = lambda j: pl.BlockSpec((None, tr, C), lambda i, p: (j, i, 0))
    mine = lambda p: 4 * p[0] + 2 * p[1] + p[2]
    if len(grad.shape) == 3:
        g_spec = pl.BlockSpec((None, tr, C), lambda i, p: (mine(p), i, 0))
    else:
        g_spec = pl.BlockSpec((tr, C), lambda i, p: (i, mine(p)))
    s_spec = pl.BlockSpec((None, tr, C), lambda i, p: (2 * p[0] + p[1], i, 0))
    in_specs = [g_spec, s_spec, rspec(0), rspec(1), rspec(2), lay, lay, lay]
    args = [pos, grad, sib, recv, recv, recv, w, m, v]
    aliases = {}
    if has_prev:
        in_specs += [_ANY] * 4
        args += list(prev)
        aliases = {9 + k: k for k in range(4)}
    return pl.pallas_call(
        body, name=name,
        grid_spec=pltpu.PrefetchScalarGridSpec(
            num_scalar_prefetch=1, grid=(R // tr,), in_specs=in_specs, out_specs=[lay] * 4),
        out_shape=[jax.ShapeDtypeStruct((L, R, C), F32)] * 4,
        input_output_aliases=aliases,
        compiler_params=_params(("parallel",)),
    )(*args)


def _sum_slots(gathered, name):
    _, R, C = gathered.shape
    tr = _rows_tile(R, C, 2 * MIB)

    def body(x_ref, o_ref):
        s = x_ref[0]
        for k in range(1, N_DEV):
            s = s + x_ref[k]
        o_ref[...] = s

    return pl.pallas_call(
        body, name=name, grid=(R // tr,),
        in_specs=[pl.BlockSpec((N_DEV, tr, C), lambda i: (0, i, 0))],
        out_specs=pl.BlockSpec((tr, C), lambda i: (i, 0)),
        out_shape=jax.ShapeDtypeStruct((R, C), F32),
        compiler_params=_params(("parallel",)),
    )(gathered)


def _adamw_small(w, g, m, v, name):
    R, C = w.shape
    tr = _rows_tile(R, C, 2 * MIB)

    def body(w_ref, g_ref, m_ref, v_ref, d_ref, mo_ref, vo_ref):
        d_ref[...], mo_ref[...], vo_ref[...] = _adamw(w_ref[...], g_ref[...], m_ref[...], v_ref[...])

    spec = pl.BlockSpec((tr, C), lambda i: (i, 0))
    return pl.pallas_call(
        body, name=name, grid=(R // tr,),
        in_specs=[spec] * 4, out_specs=[spec] * 3,
        out_shape=[jax.ShapeDtypeStruct((R, C), F32)] * 3,
        compiler_params=_params(("parallel",)),
    )(w, g, m, v)


def _pack(arrs):
    parts = []
    for a in arrs:
        flat = a.reshape(-1)
        pad = (-flat.shape[0]) % (8 * LANE)
        if pad:
            flat = jnp.pad(flat, (0, pad))
        parts.append(flat.reshape(-1, LANE))
    return jnp.concatenate(parts, axis=0)


def _unpack(packed, shapes):
    out, r = [], 0
    for s in shapes:
        n = 1
        for d in s:
            n *= d
        rows = -(-n // (8 * LANE)) * 8
        out.append(packed[r:r + rows].reshape(-1)[:n].reshape(s))
        r += rows
    return out


def kernel(x, norm1_g, w_in, w_pool, pool_scale, sgu_norm_g, sgu_w, sgu_b, conv_w, w_pool_out, w_sgu_out, w_conv_out, w_o, norm2_g, w_up, ffn_conv_w, w_down, final_g, loss_target, m_norm1_g, m_w_in, m_w_pool, m_pool_scale, m_sgu_norm_g, m_sgu_w, m_sgu_b, m_conv_w, m_w_pool_out, m_w_sgu_out, m_w_conv_out, m_w_o, m_norm2_g, m_w_up, m_ffn_conv_w, m_w_down, m_final_g, v_norm1_g, v_w_in, v_w_pool, v_pool_scale, v_sgu_norm_g, v_sgu_w, v_sgu_b, v_conv_w, v_w_pool_out, v_w_sgu_out, v_w_conv_out, v_w_o, v_norm2_g, v_w_up, v_ffn_conv_w, v_w_down, v_final_g):
    L = norm1_g.shape[0]
    x0 = x[0]
    tgt = loss_target[0]
    G, PG = w_pool.shape[1], w_pool.shape[3]
    F = w_down.shape[1] * N_DEV
    T, D = x0.shape
    tT = _pick(T, (1024, 512, 256, 128))
    tF = _pick(F, (512, 256))
    tF2 = _pick(2 * F, (1024, 512, 256))
    kF = _pick(F, (2816, 2048, 1024, 512))
    px, py, pc = _mesh_pos()
    me = 4 * px + 2 * py + pc
    pos = jnp.stack([px, py, pc]).astype(jnp.int32)

    slot = jnp.reshape(me, (1,)).astype(jnp.int32)
    ns_in = w_in.shape[2]
    nq = 3 if ns_in % (3 * 256) == 0 else 1
    chunk_cols = [(q * (ns_in // nq), ns_in // nq) for q in range(nq)]
    win0 = _cast_place(w_in, 0, slot, BF16, "place_w_in_0", by_cols=True)
    first, tok0 = _split_start([([win0],) + _copies_gather_chips(1, chunk_cols[0])], "ag_start_first")
    win0 = first[0][2][0]
    h1_first = _rms_fwd(x0, norm1_g[0][None], "rms1_fwd_0")
    stage1, group_sizes, started1 = {}, {0: 1}, {}

    def place_groups(l, deps):
        grp = [[("w_in", w_in, BF16, True)],
               [("w_pool_out", w_pool_out, BF16, True), ("w_sgu_out", w_sgu_out, BF16, True),
                ("w_conv_out", w_conv_out, BF16, True), ("w_pool", w_pool.reshape(L, -1, PG), BF16, False),
                ("conv_w", conv_w, F32, True), ("w_o", w_o, BF16, False)],
               [("w_up", w_up, BF16, True), ("ffn_conv_w", ffn_conv_w, F32, True)],
               [("w_down", w_down, BF16, False)]]
        ks = []
        for gi, members in enumerate(grp):
            if l == 0 and gi == 0:
                continue
            k = 4 * l + gi
            bufs = [_cast_place(a, l, slot, dt, f"place_{n}_{l}", deps, bc) for n, a, dt, bc in members]
            stage1[k] = (bufs,) + _copies_gather_chips(len(members))
            group_sizes[k] = len(members)
            ks.append(k)
        return ks

    def start_groups(ks, extra, name, deps=()):
        st, tok = _split_start(extra + [stage1[k] for k in ks], name, deps)
        for k, s in zip(ks, st[len(extra):]):
            started1[k] = s
        return st[:len(extra)], tok

    rest, tok_start0 = start_groups(place_groups(0, [tok0, h1_first]),
                                    [([win0],) + _copies_gather_chips(1, cols) for cols in chunk_cols[1:]], "ag_start_0")
    if nq > 1:
        win0 = rest[0][2][0]
    chunk_sems = [first[0][:2]] + [r[:2] for r in rest]
    later_groups = [place_groups(l, [tok_start0]) for l in range(1, L)]
    placed_later = [b for ks in later_groups for k in ks for b in stage1[k][0]]
    started2 = {}

    def gather_forward(k, after):
        n = group_sizes[k]
        bufs = _split_wait(started1[k], stage1[k][1], after, f"ag_wait1_{k}")
        st, tok = _split_start([(bufs,) + _copies_gather_forward(n)], f"ag_fwd_{k}")
        started2[k] = st[0]
        return tok

    def gather_done(k, after):
        return _split_wait(started2.pop(k), _copies_gather_forward(group_sizes[k])[0], after, f"ag_wait2_{k}")

    W, saved = [], []
    xc = x0
    for l in range(L):
        w = dict(
            sguw=sgu_w[l].astype(BF16),
            bexp=jnp.broadcast_to(sgu_b[l].T[:, :, None],
                                  (sgu_b.shape[2], sgu_b.shape[1], sgu_norm_g.shape[1] // sgu_b.shape[2])),
            g1=norm1_g[l][None], g2=norm2_g[l][None], gs=sgu_norm_g[l][None], scale=pool_scale[l][None])
        ka, kb, kc, kd = 4 * l, 4 * l + 1, 4 * l + 2, 4 * l + 3
        h1 = h1_first if l == 0 else _rms_fwd(xc, w["g1"], f"rms1_fwd_{l}")
        if l == 0:
            P, after = None, [h1] + placed_later
            for q, cols in enumerate(chunk_cols):
                (win0,) = _split_wait(chunk_sems[q] + ([win0],), _copies_gather_chips(1, cols)[0], after,
                                      f"ag_wait1_0_{q}")
                st, tok = _split_start([([win0],) + _copies_gather_forward(1, cols)], f"ag_fwd_0_{q}")
                (win0,) = _split_wait(st[0], _copies_gather_forward(1, cols)[0], after, f"ag_wait2_0_{q}")
                P = _mm(h1, win0, mode="nn", out_dtype=BF16, name=f"proj_in_0_{q}", tm=_pick(T, (2048, tT)), tn=cols[1], tk=D,
                        cols=(q, nq), prev=P, deps=[tok])
                after = P
            w["win"] = win0
        else:
            (w["win"],) = gather_done(ka, h1)
            tok = gather_forward(kb, h1)
            P = _mm(h1, w["win"], mode="nn", out_dtype=BF16, name=f"proj_in_{l}", tm=tT, tn=1024, tk=D, deps=[tok])
        if l == 0:
            gather_forward(kb, P)
        w["wa"], w["wb"], w["wc"], wp, w["cw"], wo = gather_done(kb, P)
        w.update(wpool=wp.reshape(N_DEV, G, -1, PG).transpose(1, 0, 2, 3).reshape(G, PG, PG),
                 wo=wo.reshape(-1, wo.shape[2]))
        ya = _pool_fwd(P, w["wpool"], w["scale"], f"pool_fwd_{l}")
        yb = _sgu_fwd(P, w["sguw"], w["bexp"], w["gs"], f"sgu_fwd_{l}")
        yc = _conv_fwd(P, w["cw"], f"conv_fwd_{l}")
        oa, ob, oc, M = _merge_fwd(P, (ya, yb, yc), (w["wa"], w["wb"], w["wc"]), f"merge_fwd_{l}")
        tok = gather_forward(kc, M)
        if l == 0:
            for l2 in range(1, L):
                start_groups(later_groups[l2 - 1], [], f"ag_start_{l2}", [tok])
        x1, h2 = _mm_res_rms(M, w["wo"], xc, w["g2"], f"proj_o_{l}", deps=[tok])
        w["wup"], w["fw"] = gather_done(kc, h2)
        tok = gather_forward(kd, h2)
        U = _mm(h2, w["wup"], mode="nn", out_dtype=BF16, name=f"proj_up_{l}", tm=tT, tn=tF2, tk=D, deps=[tok])
        act = _glu_fwd(U, w["fw"], f"glu_fwd_{l}")
        (wdown,) = gather_done(kd, act)
        w["wdown"] = wdown.reshape(-1, wdown.shape[2])
        deps = [gather_forward(4 * (l + 1), act)] if l + 1 < L else []
        x2 = _mm(act, w["wdown"], mode="nn", out_dtype=F32, name=f"proj_down_{l}", tm=tT, tn=512, tk=F, res=x1,
                 deps=deps)
        if deps:
            tok = deps[0]
        W.append(w)
        saved.append(dict(x0=xc, h1=h1, P=P, ya=ya, yb=yb, yc=yc, oa=oa, ob=ob, oc=oc, M=M, x1=x1, h2=h2, U=U, act=act))
        xc = x2

    dx, dxb, d_final_g, loss_part = _loss_head(xc, tgt, final_g[None], "loss_head")
    loss = lax.psum(loss_part[0, 0], ("x", "y", "c"))

    wmv = dict(w_in=(w_in, m_w_in, v_w_in), w_up=(w_up, m_w_up, v_w_up), w_o=(w_o, m_w_o, v_w_o),
               w_down=(w_down, m_w_down, v_w_down), w_pool_out=(w_pool_out, m_w_pool_out, v_w_pool_out),
               w_sgu_out=(w_sgu_out, m_w_sgu_out, v_w_sgu_out), w_conv_out=(w_conv_out, m_w_conv_out, v_w_conv_out),
               w_pool=(w_pool, m_w_pool, v_w_pool))
    adam_out = {}

    def rs_begin(tag, layer, names, grads):
        n = len(grads)
        slice_shape = lambda g: g.shape[1:] if len(g.shape) == 3 else (g.shape[0], g.shape[1] // N_DEV)
        lands = [lax.empty((4,) + slice_shape(g), BF16) for g in grads]
        st, tok = _split_start([(list(grads) + lands,) + _copies_rs_sibling(n)], f"rs_sib_{tag}")
        return dict(tag=tag, layer=layer, names=names, n=n, st=st[0]), tok

    def rs_to_chips(state, after):
        n, tag = state["n"], state["tag"]
        bufs = _split_wait(state["st"], _copies_rs_sibling(n)[0], after, f"rs_sibw_{tag}")
        grads, sib = bufs[:n], bufs[n:]
        pairs = [_pair_sum(g, r, pos, f"pair_{tag}_{i}") for i, (g, r) in enumerate(zip(grads, sib))]
        lands = [lax.empty(p.shape, BF16) for p in pairs]
        st, tok = _split_start([(pairs + lands,) + _copies_rs_chips(n)], f"rs_chips_{tag}")
        state.update(st=st[0], grads=grads, sib=sib)
        return tok

    def rs_finish(state, after):
        n, tag, layer = state["n"], state["tag"], state["layer"]
        bufs = _split_wait(state["st"], _copies_rs_chips(n)[0], after, f"rs_chipsw_{tag}")
        for name, g, sb, recv in zip(state["names"], state["grads"], state["sib"], bufs[n:]):
            wk, mk, vk = (a.reshape(L, -1, a.shape[-1]) for a in wmv[name])
            adam_out[name] = _adamw_big(g, sb, recv, wk, mk, vk, pos, layer, adam_out.get(name),
                                        f"adamw_{name}_{layer}")

    small = [None] * L
    pending = []
    deps = []
    for l in reversed(range(L)):
        w, s = W[l], saved[l]
        dact = _mm(dxb, w["wdown"], mode="nt", out_dtype=BF16, name=f"d_act_{l}", tm=tT, tn=tF, tk=D, deps=deps)
        g_down = _mm(s["act"], dxb, mode="tn", out_dtype=BF16, name=f"g_down_{l}", tm=tF, tn=1024, tk=T)
        dug, duv, dfwg, dfwv = _glu_bwd(s["U"], dact, w["fw"], f"glu_bwd_{l}")
        dh2 = _mm(dug, w["wup"], mode="nt", out_dtype=F32, name=f"d_h2_{l}", tm=tT, tn=1024, tk=kF, a2=duv)
        g_up = _mm(s["h2"], dug, mode="tn", out_dtype=BF16, name=f"g_upa_{l}", tm=1024, tn=tF, tk=T, out_n=2 * F)
        g_up = _mm(s["h2"], duv, mode="tn", out_dtype=BF16, name=f"g_upb_{l}", tm=1024, tn=tF, tk=T, out_n=2 * F,
                   out_off=F // tF, prev=g_up)
        ra, tok = rs_begin(f"a{l}", l, ["w_down", "w_up"], [g_down.reshape(N_DEV, -1, g_down.shape[1]), g_up])
        if pending:
            rs_finish(pending.pop(0), g_up)
        dx1, dx1b, dg2 = _rms_bwd(dh2, s["x1"], w["g2"], dx, f"rms2_bwd_{l}")
        dM = _mm(dx1b, w["wo"], mode="nt", out_dtype=BF16, name=f"d_m_{l}", tm=tT, tn=1024, tk=D, deps=[tok])
        g_o = _mm(s["M"], dx1b, mode="tn", out_dtype=BF16, name=f"g_o_{l}", tm=1024, tn=1024, tk=T)
        doa, dob, doc, dPg = _merge_bwd(s["P"], dM, (s["oa"], s["ob"], s["oc"]), f"merge_bwd_{l}")
        tok = rs_to_chips(ra, dPg)
        if pending:
            rs_finish(pending.pop(0), dPg)
        dya = _mm(doa, w["wa"], mode="nt", out_dtype=BF16, name=f"d_ya_{l}", tm=tT, tn=1024, tk=D, deps=[tok])
        dyb = _mm(dob, w["wb"], mode="nt", out_dtype=BF16, name=f"d_yb_{l}", tm=tT, tn=1024, tk=D)
        dyc = _mm(doc, w["wc"], mode="nt", out_dtype=BF16, name=f"d_yc_{l}", tm=tT, tn=1024, tk=D)
        g_a = _mm(s["ya"], doa, mode="tn", out_dtype=BF16, name=f"g_a_{l}", tm=1024, tn=1024, tk=T)
        g_b = _mm(s["yb"], dob, mode="tn", out_dtype=BF16, name=f"g_b_{l}", tm=1024, tn=1024, tk=T)
        g_c = _mm(s["yc"], doc, mode="tn", out_dtype=BF16, name=f"g_c_{l}", tm=1024, tn=1024, tk=T)
        da, g_pool, dscale = _pool_bwd(s["P"], dya, w["wpool"], w["scale"], f"pool_bwd_{l}")
        g_pool_s = g_pool.reshape(G, N_DEV, -1, PG).transpose(1, 0, 2, 3).reshape(N_DEV, -1, PG).astype(BF16)
        rb, tok = rs_begin(f"b{l}", l, ["w_o", "w_pool_out", "w_sgu_out", "w_conv_out", "w_pool"],
                           [g_o.reshape(N_DEV, -1, g_o.shape[1]), g_a, g_b, g_c, g_pool_s])
        if pending:
            rs_finish(pending.pop(0), da)
        dPl, g_sguw, db_exp, dgs = _sgu_bwd(s["P"], dyb, da, w["sguw"], w["bexp"], w["gs"], f"sgu_bwd_{l}")
        dPl, dcw = _conv_bwd(s["P"], dyc, w["cw"], dPl, f"conv_bwd_{l}")
        tok2 = rs_to_chips(rb, dPl)
        half = dPl.shape[1]
        g_in = _mm(s["h1"], dPl, mode="tn", out_dtype=BF16, name=f"g_ina_{l}", tm=1024, tn=1024, tk=T, out_n=2 * half,
                   deps=[tok, tok2])
        g_in = _mm(s["h1"], dPg, mode="tn", out_dtype=BF16, name=f"g_inb_{l}", tm=1024, tn=1024, tk=T, out_n=2 * half,
                   out_off=half // 1024, prev=g_in)
        rc, tok = rs_begin(f"c{l}", l, ["w_in"], [g_in])
        if l > 0:
            dh1 = _mm(dPl, w["win"], mode="nt", out_dtype=F32, name=f"d_h1_{l}", tm=tT, tn=1024, tk=D, a2=dPg,
                      deps=[tok])
            tok = rs_to_chips(rc, dh1)
        else:
            dh1 = _mm(dPl, w["win"], mode="nt", out_dtype=F32, name=f"d_h1a_{l}", tm=tT, tn=1024, tk=D, deps=[tok])
            tok = rs_to_chips(rc, dh1)
            dh1 = _mm(dPg, w["win"], mode="nt", out_dtype=F32, name=f"d_h1b_{l}", tm=tT, tn=1024, tk=D,
                      b_koff=half // D, res=dh1, deps=[tok])
        dx, dxb, dg1 = _rms_bwd(dh1, s["x0"], w["g1"], dx1, f"rms1_bwd_{l}", deps=[tok])
        deps = []
        pending = [ra, rb, rc]
        small[l] = dict(norm1_g=dg1[0], pool_scale=dscale[0], sgu_norm_g=dgs[0], sgu_w=g_sguw,
                        sgu_b=db_exp[:, :, 0].T, conv_w=dcw[:3], norm2_g=dg2[0],
                        ffn_conv_w=jnp.concatenate([dfwg[:3], dfwv[:3]], axis=1))
    grad_x = dx[None]

    snames = ["norm1_g", "pool_scale", "sgu_norm_g", "sgu_w", "sgu_b", "conv_w", "norm2_g", "ffn_conv_w"]
    sparts = [jnp.stack([small[l][n] for l in range(L)]) for n in snames] + [d_final_g[0]]
    snames = snames + ["final_g"]
    packed = _pack(sparts)
    sbuf = _cast_place(packed[None], 0, slot, F32, "place_small")
    st, tok = _split_start([([sbuf],) + _copies_gather_chips(1)], "ag_small_start")
    rs_finish(pending[0], tok)
    rs_finish(pending[1], tok)
    bufs = _split_wait(st[0], _copies_gather_chips(1)[0], [o[0] for o in adam_out.values()], "ag_small_wait1")
    st, tok = _split_start([(bufs,) + _copies_gather_forward(1)], "ag_small_fwd")
    (gathered_small,) = _split_wait(st[0], _copies_gather_forward(1)[0], tok, "ag_small_wait2")
    total = _sum_slots(gathered_small, "sum_small")
    sgrads = dict(zip(snames, _unpack(total, [p.shape for p in sparts])))
    for n, width in (("conv_w", conv_w.shape[2]), ("ffn_conv_w", ffn_conv_w.shape[2])):
        sgrads[n] = lax.dynamic_slice_in_dim(sgrads[n], me * width, width, axis=2)
    sw = dict(norm1_g=(norm1_g, m_norm1_g, v_norm1_g), pool_scale=(pool_scale, m_pool_scale, v_pool_scale),
              sgu_norm_g=(sgu_norm_g, m_sgu_norm_g, v_sgu_norm_g), sgu_w=(sgu_w, m_sgu_w, v_sgu_w),
              sgu_b=(sgu_b, m_sgu_b, v_sgu_b), conv_w=(conv_w, m_conv_w, v_conv_w),
              norm2_g=(norm2_g, m_norm2_g, v_norm2_g), ffn_conv_w=(ffn_conv_w, m_ffn_conv_w, v_ffn_conv_w),
              final_g=(final_g, m_final_g, v_final_g))
    shapes = [sw[n][0].shape for n in snames]
    upd = _adamw_small(_pack([sw[n][0] for n in snames]), _pack([sgrads[n] for n in snames]),
                       _pack([sw[n][1] for n in snames]), _pack([sw[n][2] for n in snames]), "adamw_small")
    sdelta, sm, sv = (dict(zip(snames, _unpack(u, shapes))) for u in upd)
    res = {n: [sgrads[n], sdelta[n], sm[n], sv[n]] for n in snames}

    rs_finish(pending[2], [upd[0]] + [o[0] for o in adam_out.values()])
    for n, outs in adam_out.items():
        res[n] = [o.reshape(wmv[n][0].shape) for o in outs]

    order = ["norm1_g", "w_in", "w_pool", "pool_scale", "sgu_norm_g", "sgu_w", "sgu_b", "conv_w", "w_pool_out",
             "w_sgu_out", "w_conv_out", "w_o", "norm2_g", "w_up", "ffn_conv_w", "w_down", "final_g"]
    return (loss, grad_x) + tuple(res[n][k] for k in range(4) for n in order)
```

```python
import functools

import jax
import jax.numpy as jnp
from jax import lax
from jax.experimental import pallas as pl
from jax.experimental.pallas import tpu as pltpu

BF16 = jnp.bfloat16
F32 = jnp.float32
EPS = 1e-6
MIB = 1024 * 1024
V7X_VMEM_BYTES = 64 * MIB
VMEM_LIMIT = 56 * MIB
LANE = 128
N_DEV = 8


def _params(sem, **kw):
    return pltpu.CompilerParams(dimension_semantics=sem, vmem_limit_bytes=VMEM_LIMIT, **kw)


def _pick(n, cands):
    for c in cands:
        if n % c == 0:
            return c
    return n


_DIMS = {"nn": (((1,), (0,)), ((), ())), "nt": (((1,), (1,)), ((), ())), "tn": (((0,), (0,)), ((), ()))}


def _mm(a, b, *, mode, out_dtype, name, tm, tn, tk, res=None, prev=None, deps=(), cols=(0, 1), b_koff=0,
        out_off=0, out_n=None, a2=None):
    if mode == "nn":
        M, Kc = a.shape
        nj = (b.shape[1] // tn - cols[0] + cols[1] - 1) // cols[1]
        out_w = b.shape[1]
    elif mode == "nt":
        M, Kc = a.shape
        nj = b.shape[0] // tn
        out_w = b.shape[0]
    else:
        Kc, M = a.shape
        nj = b.shape[1] // tn
        out_w = out_n or b.shape[1]
    nk1 = Kc // tk
    nk = nk1 + (a2.shape[1] // tk if a2 is not None else 0)
    assert M % tm == 0 and Kc % tk == 0 and (a2 is None or mode == "nt"), (name, M, Kc, tm, tk)
    off, stride = cols
    if mode == "nn":
        a_spec = pl.BlockSpec((tm, tk), lambda i, j, k: (i, k))
        b_spec = pl.BlockSpec((tk, tn), lambda i, j, k: (k, off + j * stride))
        o_spec = pl.BlockSpec((tm, tn), lambda i, j, k: (i, off + j * stride))
    elif mode == "nt":
        a_spec = pl.BlockSpec((tm, tk), lambda i, j, k: (i, jnp.minimum(k, nk1 - 1)))
        a2_spec = pl.BlockSpec((tm, tk), lambda i, j, k: (i, jnp.maximum(k - nk1, 0)))
        b_spec = pl.BlockSpec((tn, tk), lambda i, j, k: (j, b_koff + k))
        o_spec = pl.BlockSpec((tm, tn), lambda i, j, k: (i, j))
    else:
        a_spec = pl.BlockSpec((tk, tm), lambda i, j, k: (k, i))
        b_spec = pl.BlockSpec((tk, tn), lambda i, j, k: (k, j))
        o_spec = pl.BlockSpec((tm, tn), lambda i, j, k: (i, out_off + j))
    dims = _DIMS[mode]
    has_res = res is not None
    has_prev = prev is not None
    n_in = 2 + has_res + has_prev + len(deps) + (a2 is not None)

    def body(*refs):
        a_ref, b_ref = refs[0], refs[1]
        o_ref = refs[n_in]

        def finish(acc):
            if has_res:
                acc = acc + refs[2][...]
            o_ref[...] = acc.astype(o_ref.dtype)

        if nk == 1:
            finish(lax.dot_general(a_ref[...], b_ref[...], dims, preferred_element_type=F32))
        else:
            acc_ref = refs[n_in + 1]
            k = pl.program_id(2)

            @pl.when(k == 0)
            def _():
                acc_ref[...] = jnp.zeros(acc_ref.shape, F32)

            if a2 is None:
                acc_ref[...] += lax.dot_general(a_ref[...], b_ref[...], dims, preferred_element_type=F32)
            else:
                @pl.when(k < nk1)
                def _():
                    acc_ref[...] += lax.dot_general(a_ref[...], b_ref[...], dims, preferred_element_type=F32)

                @pl.when(k >= nk1)
                def _():
                    acc_ref[...] += lax.dot_general(refs[n_in - 1][...], b_ref[...], dims, preferred_element_type=F32)

            @pl.when(k == nk - 1)
            def _():
                finish(acc_ref[...])

    in_specs = [a_spec, b_spec]
    args = [a, b]
    if has_res:
        in_specs.append(pl.BlockSpec((tm, tn), lambda i, j, k: (i, j)))
        args.append(res)
    aliases = {}
    if has_prev:
        aliases = {len(args): 0}
        in_specs.append(pl.BlockSpec(memory_space=pl.ANY))
        args.append(prev)
    for d in deps:
        in_specs.append(pl.BlockSpec(memory_space=pl.ANY))
        args.append(d)
    if a2 is not None:
        in_specs.append(a2_spec)
        args.append(a2)
    return pl.pallas_call(
        body,
        name=name,
        grid=(M // tm, nj, nk),
        in_specs=in_specs,
        out_specs=o_spec,
        out_shape=jax.ShapeDtypeStruct((M, out_w), out_dtype),
        scratch_shapes=[pltpu.VMEM((tm, tn), F32)] if nk > 1 else [],
        input_output_aliases=aliases,
        compiler_params=_params(("parallel", "parallel", "arbitrary")),
    )(*args)


def _mm_res_rms(a, b, res, g, name, deps=()):
    M, K = a.shape
    N = b.shape[1]
    tm = _pick(M, (512, 256, 128))

    def body(a_ref, b_ref, res_ref, g_ref, *rest):
        x_ref, h_ref = rest[-2:]
        xv = res_ref[...] + jnp.dot(a_ref[...], b_ref[...], preferred_element_type=F32)
        x_ref[...] = xv
        r = lax.rsqrt(jnp.mean(xv * xv, axis=-1, keepdims=True) + EPS)
        h_ref[...] = (xv * r * g_ref[...]).astype(BF16)

    row = pl.BlockSpec((tm, N), lambda i: (i, 0))
    return pl.pallas_call(
        body, name=name, grid=(M // tm,),
        in_specs=[pl.BlockSpec((tm, K), lambda i: (i, 0)), pl.BlockSpec((K, N), lambda i: (0, 0)), row,
                  pl.BlockSpec((1, N), lambda i: (0, 0))] + [pl.BlockSpec(memory_space=pl.ANY)] * len(deps),
        out_specs=[row, row],
        out_shape=[jax.ShapeDtypeStruct((M, N), F32), jax.ShapeDtypeStruct((M, N), BF16)],
        compiler_params=_params(("parallel",)),
    )(a, b, res, g, *deps)


HALO = 16
CH = 256


def _fill_pad(pad_ref, chunk_fn, T):
    z = jnp.zeros((HALO, pad_ref.shape[1]), F32)
    pad_ref[pl.ds(0, HALO), :] = z
    pad_ref[pl.ds(HALO + T, HALO), :] = z

    def body(c, carry):
        r0 = pl.multiple_of(c * CH, CH)
        pad_ref[pl.ds(HALO + r0, CH), :] = chunk_fn(r0)
        return carry

    lax.fori_loop(0, T // CH, body, 0)


def _ext(pad_ref, r0):
    return pad_ref[pl.ds(r0, CH + 2 * HALO), :]


def _tap(ext, o):
    if o == 0:
        return ext[HALO:HALO + CH]
    return pltpu.roll(ext, (-o) % ext.shape[0], axis=0)[HALO:HALO + CH]


def _chunks(T, fn, init=0):
    def body(c, carry):
        return fn(pl.multiple_of(c * CH, CH), carry)

    return lax.fori_loop(0, T // CH, body, init)


def _conv3(ext, w):
    return _tap(ext, -1) * w[0:1] + _tap(ext, 0) * w[1:2] + _tap(ext, 1) * w[2:3]


def _conv3_t(ext, w):
    return _tap(ext, 1) * w[0:1] + _tap(ext, 0) * w[1:2] + _tap(ext, -1) * w[2:3]


def _sigmoid(x):
    return 0.5 * (jnp.tanh(0.5 * x) + 1.0)


def _silu(x):
    h = 0.5 * x
    return h * (jnp.tanh(h) + 1.0)


_GELU_C = 0.7978845608028654


def _gelu(x):
    return 0.5 * x * (1.0 + jnp.tanh(_GELU_C * (x + 0.044715 * (x * x * x))))


def _gelu_grad(x):
    t = jnp.tanh(_GELU_C * (x + 0.044715 * (x * x * x)))
    return 0.5 * (1.0 + t) + 0.5 * x * (1.0 - t * t) * (_GELU_C * (1.0 + 3.0 * 0.044715 * (x * x)))


def _row_tile(T):
    return _pick(T, (256, 128))


def _rms_fwd(x, g, name):
    T, D = x.shape
    tm = _row_tile(T)

    def body(x_ref, g_ref, h_ref):
        xv = x_ref[...]
        r = lax.rsqrt(jnp.mean(xv * xv, axis=-1, keepdims=True) + EPS)
        h_ref[...] = (xv * r * g_ref[...]).astype(BF16)

    return pl.pallas_call(
        body, name=name, grid=(T // tm,),
        in_specs=[pl.BlockSpec((tm, D), lambda i: (i, 0)), pl.BlockSpec((1, D), lambda i: (0, 0))],
        out_specs=pl.BlockSpec((tm, D), lambda i: (i, 0)),
        out_shape=jax.ShapeDtypeStruct((T, D), BF16),
        compiler_params=_params(("parallel",)),
    )(x, g)


def _rms_bwd(dh, x, g, dres, name, deps=()):
    T, D = x.shape
    tm = _row_tile(T)

    def body(dh_ref, x_ref, g_ref, dres_ref, *rest):
        dx_ref, dxb_ref, dg_ref = rest[len(deps):]
        i = pl.program_id(0)
        xv = x_ref[...]
        r = lax.rsqrt(jnp.mean(xv * xv, axis=-1, keepdims=True) + EPS)
        n = xv * r
        dh_v = dh_ref[...].astype(F32)
        dn = dh_v * g_ref[...]
        dx = dres_ref[...] + r * (dn - n * jnp.mean(dn * n, axis=-1, keepdims=True))
        dx_ref[...] = dx
        dxb_ref[...] = dx.astype(BF16)
        dg = jnp.sum(dh_v * n, axis=0, keepdims=True)

        @pl.when(i == 0)
        def _():
            dg_ref[...] = dg

        @pl.when(i > 0)
        def _():
            dg_ref[...] += dg

    row = pl.BlockSpec((tm, D), lambda i: (i, 0))
    vec = pl.BlockSpec((1, D), lambda i: (0, 0))
    return pl.pallas_call(
        body, name=name, grid=(T // tm,),
        in_specs=[row, row, vec, row] + [pl.BlockSpec(memory_space=pl.ANY)] * len(deps),
        out_specs=[row, row, vec],
        out_shape=[jax.ShapeDtypeStruct((T, D), F32), jax.ShapeDtypeStruct((T, D), BF16),
                   jax.ShapeDtypeStruct((1, D), F32)],
        compiler_params=_params(("arbitrary",)),
    )(dh, x, g, dres, *deps)


def _loss_head(x, tgt, g, name):
    T, D = x.shape
    tm = _row_tile(T)

    def body(x_ref, t_ref, g_ref, dx_ref, dxb_ref, dg_ref, l_ref):
        i = pl.program_id(0)
        xv = x_ref[...]
        gv = g_ref[...]
        r = lax.rsqrt(jnp.mean(xv * xv, axis=-1, keepdims=True) + EPS)
        n = xv * r
        e = n * gv - t_ref[...]
        dy = e * (1.0 / D)
        dn = dy * gv
        dx = r * (dn - n * jnp.mean(dn * n, axis=-1, keepdims=True))
        dx_ref[...] = dx
        dxb_ref[...] = dx.astype(BF16)
        dg = jnp.sum(dy * n, axis=0, keepdims=True)
        per_tok = jnp.mean(e * e, axis=-1, keepdims=True)
        lv = jnp.broadcast_to(0.5 * jnp.sum(per_tok, axis=0, keepdims=True), (1, LANE))

        @pl.when(i == 0)
        def _():
            dg_ref[...] = dg
            l_ref[...] = lv

        @pl.when(i > 0)
        def _():
            dg_ref[...] += dg
            l_ref[...] += lv

    row = pl.BlockSpec((tm, D), lambda i: (i, 0))
    vec = pl.BlockSpec((1, D), lambda i: (0, 0))
    return pl.pallas_call(
        body, name=name, grid=(T // tm,),
        in_specs=[row, row, vec],
        out_specs=[row, row, vec, pl.BlockSpec((1, LANE), lambda i: (0, 0))],
        out_shape=[jax.ShapeDtypeStruct((T, D), F32), jax.ShapeDtypeStruct((T, D), BF16),
                   jax.ShapeDtypeStruct((1, D), F32), jax.ShapeDtypeStruct((1, LANE), F32)],
        compiler_params=_params(("arbitrary",)),
    )(x, tgt, g)


POOL_WINDOWS = (2, 4, 8, 16)


def _pool_cnt(r0, w, T, shape):
    t = r0 + lax.broadcasted_iota(jnp.int32, shape, 0)
    lo = jnp.maximum(t - w // 2, 0)
    hi = jnp.minimum(t + (w - w // 2 - 1), T - 1)
    return (hi - lo + 1).astype(F32)


def _window_sum(ext, w, mirrored=False):
    n = ext.shape[0]
    cur = ext + pltpu.roll(ext, n - 1 if mirrored else 1, axis=0)
    width, sh = 2, 1
    while width < w:
        cur = pltpu.roll(cur, sh, axis=0) + pltpu.roll(cur, n - sh, axis=0)
        width, sh = 2 * width, 2 * sh
    return cur[HALO:HALO + CH]


def _pooled(pad_ref, r0, w, T):
    ext = _ext(pad_ref, r0)
    cur = ext[HALO:HALO + CH]
    return _window_sum(ext, w) / _pool_cnt(r0, w, T, cur.shape) - cur


def _pool_fwd(P, w_pool, scale, name):
    T = P.shape[0]
    G, PG, _ = w_pool.shape

    def body(a_ref, w_ref, s_ref, o_ref, pad_ref):
        g = pl.program_id(0)
        _fill_pad(pad_ref, lambda r0: a_ref[pl.ds(r0, CH), :].astype(F32), T)
        for gi, w in enumerate(POOL_WINDOWS):
            @pl.when(g == gi)
            def _(w=w):
                def chunk(r0, carry):
                    pooled = _pooled(pad_ref, r0, w, T).astype(BF16)
                    y = jnp.dot(pooled, w_ref[...], preferred_element_type=F32) * s_ref[...]
                    o_ref[pl.ds(r0, CH), :] = y.astype(BF16)
                    return carry

                _chunks(T, chunk)

    return pl.pallas_call(
        body, name=name, grid=(G,),
        in_specs=[pl.BlockSpec((T, PG), lambda g: (0, g)),
                  pl.BlockSpec((None, PG, PG), lambda g: (g, 0, 0)),
                  pl.BlockSpec((1, PG), lambda g: (0, g))],
        out_specs=pl.BlockSpec((T, PG), lambda g: (0, g)),
        out_shape=jax.ShapeDtypeStruct((T, G * PG), BF16),
        scratch_shapes=[pltpu.VMEM((T + 2 * HALO, PG), F32)],
        compiler_params=_params(("parallel",)),
    )(P, w_pool, scale)


def _pool_bwd(P, dy, w_pool, scale, name):
    T = P.shape[0]
    G, PG, _ = w_pool.shape

    def body(a_ref, dy_ref, w_ref, s_ref, da_ref, dw_ref, ds_ref, pad_ref, gpad_ref):
        g = pl.program_id(0)
        gpad_ref[pl.ds(0, HALO), :] = jnp.zeros((HALO, PG), F32)
        gpad_ref[pl.ds(HALO + T, HALO), :] = jnp.zeros((HALO, PG), F32)
        _fill_pad(pad_ref, lambda r0: a_ref[pl.ds(r0, CH), :].astype(F32), T)
        for gi, w in enumerate(POOL_WINDOWS):
            @pl.when(g == gi)
            def _(w=w):
                def chunk1(r0, carry):
                    dw, ds = carry
                    pooled = _pooled(pad_ref, r0, w, T).astype(BF16)
                    ypre = jnp.dot(pooled, w_ref[...], preferred_element_type=F32)
                    dyv = dy_ref[pl.ds(r0, CH), :].astype(F32)
                    ds = ds + jnp.sum(dyv * ypre, axis=0, keepdims=True)
                    dyp = (dyv * s_ref[...]).astype(BF16)
                    dw = dw + lax.dot_general(pooled, dyp, _DIMS["tn"], preferred_element_type=F32)
                    dpool = lax.dot_general(dyp, w_ref[...], _DIMS["nt"], preferred_element_type=F32)
                    gpad_ref[pl.ds(HALO + r0, CH), :] = dpool / _pool_cnt(r0, w, T, dpool.shape)
                    return dw, ds

                dw, ds = _chunks(T, chunk1, (jnp.zeros((PG, PG), F32), jnp.zeros((1, PG), F32)))
                dw_ref[...] = dw
                ds_ref[...] = ds

                def chunk2(r0, carry):
                    ext = _ext(gpad_ref, r0)
                    cur = ext[HALO:HALO + CH]
                    acc = _window_sum(ext, w, mirrored=True)
                    da_ref[pl.ds(r0, CH), :] = (acc - cur * _pool_cnt(r0, w, T, cur.shape)).astype(BF16)
                    return carry

                _chunks(T, chunk2)

    col = pl.BlockSpec((T, PG), lambda g: (0, g))
    return pl.pallas_call(
        body, name=name, grid=(G,),
        in_specs=[col, col, pl.BlockSpec((None, PG, PG), lambda g: (g, 0, 0)), pl.BlockSpec((1, PG), lambda g: (0, g))],
        out_specs=[col, pl.BlockSpec((None, PG, PG), lambda g: (g, 0, 0)), pl.BlockSpec((1, PG), lambda g: (0, g))],
        out_shape=[jax.ShapeDtypeStruct((T, G * PG), BF16), jax.ShapeDtypeStruct((G, PG, PG), F32),
                   jax.ShapeDtypeStruct((1, G * PG), F32)],
        scratch_shapes=[pltpu.VMEM((T + 2 * HALO, PG), F32), pltpu.VMEM((T + 2 * HALO, PG), F32)],
        compiler_params=_params(("parallel",)),
    )(P, dy, w_pool, scale)


SGU_CHUNK = 128


def _sgu_common(u_ref, v_ref, gs_ref):
    up = u_ref[...].astype(F32)
    vp = v_ref[...].astype(F32)
    gv = _gelu(vp)
    rv = lax.rsqrt(jnp.mean(gv * gv, axis=-1, keepdims=True) + EPS)
    nrm = gv * rv
    return up, vp, nrm, rv, (nrm * gs_ref[...]).astype(BF16)


def _sgu_fwd(P, sgu_w, b_exp, gs, name):
    T = P.shape[0]
    G, _, SG = b_exp.shape
    DP = G * SG
    tm = _row_tile(T)

    def body(u_ref, v_ref, w_ref, b_ref, gs_ref, o_ref):
        up, _, _, _, vn = _sgu_common(u_ref, v_ref, gs_ref)
        gu = _gelu(up)
        for n in range(tm // SGU_CHUNK):
            rows = slice(n * SGU_CHUNK, (n + 1) * SGU_CHUNK)
            for g in range(G):
                cols = slice(g * SG, (g + 1) * SG)
                z = jnp.dot(w_ref[g], vn[rows, cols], preferred_element_type=F32) + b_ref[g]
                o_ref[rows, cols] = (gu[rows, cols] * z).astype(BF16)

    return pl.pallas_call(
        body, name=name, grid=(T // tm,),
        in_specs=[pl.BlockSpec((tm, DP), lambda i: (i, 1)), pl.BlockSpec((tm, DP), lambda i: (i, 2)),
                  pl.BlockSpec(sgu_w.shape, lambda i: (0, 0, 0)), pl.BlockSpec(b_exp.shape, lambda i: (0, 0, 0)),
                  pl.BlockSpec((1, DP), lambda i: (0, 0))],
        out_specs=pl.BlockSpec((tm, DP), lambda i: (i, 0)),
        out_shape=jax.ShapeDtypeStruct((T, DP), BF16),
        compiler_params=_params(("parallel",)),
    )(P, P, sgu_w, b_exp, gs)


def _sgu_bwd(P, dy, da, sgu_w, b_exp, gs, name):
    T = P.shape[0]
    G, _, SG = b_exp.shape
    DP = G * SG
    tm = _row_tile(T)

    def body(u_ref, v_ref, dy_ref, da_ref, w_ref, b_ref, gs_ref, o_ref, dw_ref, db_ref, dgs_ref, dzs_ref):
        i = pl.program_id(0)
        up, vp, nrm, rv, vn = _sgu_common(u_ref, v_ref, gs_ref)
        gu = _gelu(up)
        dyv = dy_ref[...].astype(F32)
        o_ref[:, 0:DP] = da_ref[...]

        @pl.when(i == 0)
        def _():
            dw_ref[...] = jnp.zeros(dw_ref.shape, F32)
            dzs_ref[...] = jnp.zeros(dzs_ref.shape, F32)
            dgs_ref[...] = jnp.zeros(dgs_ref.shape, F32)

        dgs = jnp.zeros((1, DP), F32)
        for n in range(tm // SGU_CHUNK):
            rows = slice(n * SGU_CHUNK, (n + 1) * SGU_CHUNK)
            dvn_parts = []
            for g in range(G):
                cols = slice(g * SG, (g + 1) * SG)
                vng = vn[rows, cols]
                z = jnp.dot(w_ref[g], vng, preferred_element_type=F32) + b_ref[g]
                dyg = dyv[rows, cols]
                du = dyg * z
                o_ref[rows, DP + g * SG:DP + (g + 1) * SG] = (du * _gelu_grad(up[rows, cols])).astype(BF16)
                dz = dyg * gu[rows, cols]
                dzb = dz.astype(BF16)
                dzs_ref[g] += dz
                dw_ref[g] += lax.dot_general(dzb, vng, _DIMS["nt"], preferred_element_type=F32)
                dvn_parts.append(lax.dot_general(w_ref[g], dzb, _DIMS["tn"], preferred_element_type=F32))
            dvn = jnp.concatenate(dvn_parts, axis=1)
            nr = nrm[rows]
            dgs = dgs + jnp.sum(dvn * nr, axis=0, keepdims=True)
            dn = dvn * gs_ref[...]
            dgv = rv[rows] * (dn - nr * jnp.mean(dn * nr, axis=-1, keepdims=True))
            o_ref[rows, 2 * DP:3 * DP] = (dgv * _gelu_grad(vp[rows])).astype(BF16)
        dgs_ref[...] += dgs

        @pl.when(i == T // tm - 1)
        def _():
            for g in range(G):
                db_ref[g] = jnp.broadcast_to(jnp.sum(dzs_ref[g], axis=1, keepdims=True), (SGU_CHUNK, SG))

    full3 = lambda a: pl.BlockSpec(a.shape, lambda i: (0, 0, 0))
    return pl.pallas_call(
        body, name=name, grid=(T // tm,),
        in_specs=[pl.BlockSpec((tm, DP), lambda i: (i, 1)), pl.BlockSpec((tm, DP), lambda i: (i, 2)),
                  pl.BlockSpec((tm, DP), lambda i: (i, 0)), pl.BlockSpec((tm, DP), lambda i: (i, 0)),
                  full3(sgu_w), full3(b_exp), pl.BlockSpec((1, DP), lambda i: (0, 0))],
        out_specs=[pl.BlockSpec((tm, 3 * DP), lambda i: (i, 0)), full3(sgu_w), full3(b_exp),
                   pl.BlockSpec((1, DP), lambda i: (0, 0))],
        out_shape=[jax.ShapeDtypeStruct((T, 6 * DP), BF16), jax.ShapeDtypeStruct(sgu_w.shape, F32),
                   jax.ShapeDtypeStruct(b_exp.shape, F32), jax.ShapeDtypeStruct((1, DP), F32)],
        scratch_shapes=[pltpu.VMEM(b_exp.shape, F32)],
        compiler_params=_params(("arbitrary",)),
    )(P, P, dy, da, sgu_w, b_exp, gs)


def _conv_fwd(P, cw, name):
    T = P.shape[0]
    DP = cw.shape[1]
    tc = 256
    nb = DP // tc

    def body(xc_ref, bg_ref, cg_ref, w_ref, o_ref, pad_ref):
        _fill_pad(pad_ref, lambda r0: cg_ref[pl.ds(r0, CH), :].astype(F32) * xc_ref[pl.ds(r0, CH), :].astype(F32), T)
        w = w_ref[...]

        def chunk(r0, carry):
            cq = _conv3(_ext(pad_ref, r0), w)
            o_ref[pl.ds(r0, CH), :] = (bg_ref[pl.ds(r0, CH), :].astype(F32) * cq).astype(BF16)
            return carry

        _chunks(T, chunk)

    col = lambda off: pl.BlockSpec((T, tc), lambda j: (0, off * nb + j))
    return pl.pallas_call(
        body, name=name, grid=(nb,),
        in_specs=[col(3), col(4), col(5), pl.BlockSpec((3, tc), lambda j: (0, j))],
        out_specs=pl.BlockSpec((T, tc), lambda j: (0, j)),
        out_shape=jax.ShapeDtypeStruct((T, DP), BF16),
        scratch_shapes=[pltpu.VMEM((T + 2 * HALO, tc), F32)],
        compiler_params=_params(("parallel",)),
    )(P, P, P, cw)


def _conv_bwd(P, dy, cw, dPl, name):
    T = P.shape[0]
    DP = cw.shape[1]
    tc = 256
    nb = DP // tc

    def body(xc_ref, bg_ref, cg_ref, dy_ref, w_ref, prev_ref, o_ref, dw_ref, qpad_ref, dpad_ref):
        del prev_ref
        seg = pl.program_id(1)
        w = w_ref[...]
        rows = lambda ref, r0: ref[pl.ds(r0, CH), :].astype(F32)

        @pl.when(seg == 0)
        def _():
            _fill_pad(qpad_ref, lambda r0: rows(cg_ref, r0) * rows(xc_ref, r0), T)
            _fill_pad(dpad_ref, lambda r0: rows(dy_ref, r0) * rows(bg_ref, r0), T)

        @pl.when(seg == 0)
        def _():
            def chunk(r0, carry):
                dext = _ext(dpad_ref, r0)
                taps = [_tap(dext, 1), _tap(dext, 0), _tap(dext, -1)]
                dq = taps[0] * w[0:1] + taps[1] * w[1:2] + taps[2] * w[2:3]
                o_ref[pl.ds(r0, CH), :] = (dq * rows(cg_ref, r0)).astype(BF16)
                q = qpad_ref[pl.ds(HALO + r0, CH), :]
                return [a + jnp.sum(q * t, axis=0, keepdims=True) for a, t in zip(carry, taps)]

            dws = _chunks(T, chunk, [jnp.zeros((1, tc), F32)] * 3)
            dw_ref[...] = jnp.concatenate(dws + [jnp.zeros((5, tc), F32)], axis=0)

        @pl.when(seg == 1)
        def _():
            def chunk(r0, carry):
                cq = _conv3(_ext(qpad_ref, r0), w)
                o_ref[pl.ds(r0, CH), :] = (rows(dy_ref, r0) * cq).astype(BF16)
                return carry

            _chunks(T, chunk)

        @pl.when(seg == 2)
        def _():
            def chunk(r0, carry):
                dq = _conv3_t(_ext(dpad_ref, r0), w)
                o_ref[pl.ds(r0, CH), :] = (dq * rows(xc_ref, r0)).astype(BF16)
                return carry

            _chunks(T, chunk)

    col = lambda off: pl.BlockSpec((T, tc), lambda j, s: (0, off * nb + j))
    return pl.pallas_call(
        body, name=name, grid=(nb, 3),
        in_specs=[col(3), col(4), col(5), pl.BlockSpec((T, tc), lambda j, s: (0, j)),
                  pl.BlockSpec((3, tc), lambda j, s: (0, j)), pl.BlockSpec(memory_space=pl.ANY)],
        out_specs=[pl.BlockSpec((T, tc), lambda j, s: (0, (3 + s) * nb + j)), pl.BlockSpec((8, tc), lambda j, s: (0, j))],
        out_shape=[jax.ShapeDtypeStruct(dPl.shape, BF16), jax.ShapeDtypeStruct((8, DP), F32)],
        scratch_shapes=[pltpu.VMEM((T + 2 * HALO, tc), F32), pltpu.VMEM((T + 2 * HALO, tc), F32)],
        input_output_aliases={5: 0},
        compiler_params=_params(("parallel", "arbitrary")),
    )(P, P, P, dy, cw, dPl)


def _merge_fwd(P, ys, ws, name):
    T = P.shape[0]
    DP, D = ws[0].shape
    tm = _pick(T, (512, 256, 128))
    tn = _pick(D, (1024, 512, 256))
    goff = (6 * DP) // tn

    def body(ya, yb, yc, wa, wb, wc, ga, gb, gc, oa_ref, ob_ref, oc_ref, m_ref):
        m = None
        for y, w, g, o_ref in ((ya, wa, ga, oa_ref), (yb, wb, gb, ob_ref), (yc, wc, gc, oc_ref)):
            o = jnp.dot(y[...], w[...], preferred_element_type=F32)
            o_ref[...] = o.astype(BF16)
            t = _sigmoid(g[...].astype(F32)) * o
            m = t if m is None else m + t
        m_ref[...] = m.astype(BF16)

    yspec = pl.BlockSpec((tm, DP), lambda i, j: (i, 0))
    wspec = pl.BlockSpec((DP, tn), lambda i, j: (0, j))
    gspec = lambda b: pl.BlockSpec((tm, tn), lambda i, j: (i, goff + b * (D // tn) + j))
    ospec = pl.BlockSpec((tm, tn), lambda i, j: (i, j))
    return pl.pallas_call(
        body, name=name, grid=(T // tm, D // tn),
        in_specs=[yspec] * 3 + [wspec] * 3 + [gspec(0), gspec(1), gspec(2)],
        out_specs=[ospec] * 4,
        out_shape=[jax.ShapeDtypeStruct((T, D), BF16)] * 4,
        compiler_params=_params(("parallel", "parallel")),
    )(*ys, *ws, P, P, P)


def _merge_bwd(P, dM, os_, name):
    T, D = dM.shape
    tm = _row_tile(T)

    def body(dm_ref, oa, ob, oc, g_ref, da_ref, db_ref, dc_ref, dg_ref):
        dm = dm_ref[...].astype(F32)
        for b, (o_ref, d_ref) in enumerate(((oa, da_ref), (ob, db_ref), (oc, dc_ref))):
            s = _sigmoid(g_ref[:, b * D:(b + 1) * D].astype(F32))
            d_ref[...] = (dm * s).astype(BF16)
            dg_ref[:, b * D:(b + 1) * D] = (dm * o_ref[...].astype(F32) * s * (1.0 - s)).astype(BF16)

    row = pl.BlockSpec((tm, D), lambda i: (i, 0))
    return pl.pallas_call(
        body, name=name, grid=(T // tm,),
        in_specs=[row] * 4 + [pl.BlockSpec((tm, 3 * D), lambda i: (i, 1))],
        out_specs=[row] * 3 + [pl.BlockSpec((tm, 3 * D), lambda i: (i, 0))],
        out_shape=[jax.ShapeDtypeStruct((T, D), BF16)] * 3 + [jax.ShapeDtypeStruct((T, 3 * D), BF16)],
        compiler_params=_params(("parallel",)),
    )(dM, *os_, P)


def _glu_fwd(U, fw, name):
    T, F2 = U.shape
    F = F2 // 2
    tc = 256
    nb = F // tc

    def body(ug_ref, uv_ref, wg_ref, wv_ref, o_ref, gpad_ref, vpad_ref):
        _fill_pad(gpad_ref, lambda r0: ug_ref[pl.ds(r0, CH), :].astype(F32), T)
        _fill_pad(vpad_ref, lambda r0: uv_ref[pl.ds(r0, CH), :].astype(F32), T)
        wg, wv = wg_ref[...], wv_ref[...]

        def chunk(r0, carry):
            gate = _conv3(_ext(gpad_ref, r0), wg)
            val = _conv3(_ext(vpad_ref, r0), wv)
            o_ref[pl.ds(r0, CH), :] = (_silu(gate) * val).astype(BF16)
            return carry

        _chunks(T, chunk)

    return pl.pallas_call(
        body, name=name, grid=(nb,),
        in_specs=[pl.BlockSpec((T, tc), lambda j: (0, j)), pl.BlockSpec((T, tc), lambda j: (0, nb + j)),
                  pl.BlockSpec((3, tc), lambda j: (0, j)), pl.BlockSpec((3, tc), lambda j: (0, nb + j))],
        out_specs=pl.BlockSpec((T, tc), lambda j: (0, j)),
        out_shape=jax.ShapeDtypeStruct((T, F), BF16),
        scratch_shapes=[pltpu.VMEM((T + 2 * HALO, tc), F32)] * 2,
        compiler_params=_params(("parallel",)),
    )(U, U, fw, fw)


def _glu_bwd(U, dact, fw, name):
    T, F2 = U.shape
    F = F2 // 2
    tc = 128
    nb = F // tc

    def body(ug_ref, uv_ref, da_ref, wg_ref, wv_ref, dug_ref, duv_ref, dwg_ref, dwv_ref, gpad, vpad, dgpad, dvpad):
        _fill_pad(gpad, lambda r0: ug_ref[pl.ds(r0, CH), :].astype(F32), T)
        _fill_pad(vpad, lambda r0: uv_ref[pl.ds(r0, CH), :].astype(F32), T)
        wg, wv = wg_ref[...], wv_ref[...]

        def chunk1(r0, carry):
            gate = _conv3(_ext(gpad, r0), wg)
            val = _conv3(_ext(vpad, r0), wv)
            s = _sigmoid(gate)
            da = da_ref[pl.ds(r0, CH), :].astype(F32)
            dgpad[pl.ds(HALO + r0, CH), :] = da * val * (s * (1.0 + gate * (1.0 - s)))
            dvpad[pl.ds(HALO + r0, CH), :] = da * (gate * s)
            return carry

        z = jnp.zeros((HALO, tc), F32)
        for p in (dgpad, dvpad):
            p[pl.ds(0, HALO), :] = z
            p[pl.ds(HALO + T, HALO), :] = z
        _chunks(T, chunk1)

        def chunk2(r0, carry):
            new = []
            for pad, dpad, wk, o_ref in ((gpad, dgpad, wg, dug_ref), (vpad, dvpad, wv, duv_ref)):
                dext = _ext(dpad, r0)
                taps = [_tap(dext, 1), _tap(dext, 0), _tap(dext, -1)]
                o_ref[pl.ds(r0, CH), :] = (taps[0] * wk[0:1] + taps[1] * wk[1:2] + taps[2] * wk[2:3]).astype(BF16)
                xs = pad[pl.ds(HALO + r0, CH), :]
                new += [jnp.sum(xs * t, axis=0, keepdims=True) for t in taps]
            return [a + b for a, b in zip(carry, new)]

        dws = _chunks(T, chunk2, [jnp.zeros((1, tc), F32)] * 6)
        dwg_ref[...] = jnp.concatenate(dws[:3] + [jnp.zeros((5, tc), F32)], axis=0)
        dwv_ref[...] = jnp.concatenate(dws[3:] + [jnp.zeros((5, tc), F32)], axis=0)

    lo = pl.BlockSpec((T, tc), lambda j: (0, j))
    hi = pl.BlockSpec((T, tc), lambda j: (0, nb + j))
    wlo = pl.BlockSpec((3, tc), lambda j: (0, j))
    whi = pl.BlockSpec((3, tc), lambda j: (0, nb + j))
    dwspec = pl.BlockSpec((8, tc), lambda j: (0, j))
    return pl.pallas_call(
        body, name=name, grid=(nb,),
        in_specs=[lo, hi, lo, wlo, whi],
        out_specs=[lo, lo, dwspec, dwspec],
        out_shape=[jax.ShapeDtypeStruct((T, F), BF16)] * 2 + [jax.ShapeDtypeStruct((8, F), F32)] * 2,
        scratch_shapes=[pltpu.VMEM((T + 2 * HALO, tc), F32)] * 4,
        compiler_params=_params(("parallel",)),
    )(U, U, dact, fw, fw)


_ANY = pl.BlockSpec(memory_space=pl.ANY)
_MESH = pl.DeviceIdType.MESH


def _mesh_pos():
    return lax.axis_index("x"), lax.axis_index("y"), lax.axis_index("c")


def _other_chips(x, y):
    return [(1 - x, y), (x, 1 - y), (1 - x, 1 - y)]


_HBM = pl.BlockSpec(memory_space=pltpu.HBM)
_SEM = pl.BlockSpec(memory_space=pltpu.SEMAPHORE)
_EFFECT = pltpu.SideEffectType.DATAFLOW_SIDE_EFFECTING


def _in_hbm(a):
    return pltpu.with_memory_space_constraint(a, pltpu.HBM)


def _split_start(groups, name, deps=()):
    flat, where = [], []
    for bufs, _, _ in groups:
        idx = []
        for b in bufs:
            hit = [i for i, f in enumerate(flat) if f is b]
            if not hit:
                flat.append(b)
            idx.append(hit[0] if hit else len(flat) - 1)
        where.append(idx)
    nb = len(flat)
    ng = len(groups)

    def body(*refs):
        ins = refs[:nb]
        sems = refs[nb + len(deps):nb + len(deps) + 2 * ng]
        token = refs[-1]
        for gi, (_, make_copies, _) in enumerate(groups):
            for cp in make_copies([ins[i] for i in where[gi]], sems[2 * gi], sems[2 * gi + 1]):
                cp.start()
        token[...] = jnp.zeros(token.shape, token.dtype)

    sem_shapes = []
    for _, _, n in groups:
        sem_shapes += [pltpu.SemaphoreType.DMA((n,)), pltpu.SemaphoreType.DMA((n,))]
    outs = pl.pallas_call(
        body, name=name,
        in_specs=[_HBM] * nb + [_ANY] * len(deps),
        out_specs=[_SEM] * (2 * ng) + [_HBM] * nb + [pl.BlockSpec(memory_space=pltpu.VMEM)],
        out_shape=sem_shapes + [pltpu.HBM(b.shape, b.dtype) for b in flat] + [jax.ShapeDtypeStruct((8, LANE), F32)],
        input_output_aliases={i: 2 * ng + i for i in range(nb)},
        compiler_params=pltpu.CompilerParams(has_side_effects=_EFFECT),
    )(*[_in_hbm(b) for b in flat], *deps)
    res = [(outs[2 * gi], outs[2 * gi + 1], [outs[2 * ng + i] for i in where[gi]]) for gi in range(ng)]
    return res, outs[-1]


def _split_wait(started, make_copies, after, name):
    send, recv, bufs = started
    nb = len(bufs)
    after = list(after) if isinstance(after, (list, tuple)) else [after]

    def body(*refs):
        for cp in make_copies(refs[:nb], refs[nb], refs[nb + 1]):
            cp.wait_send()
            cp.wait_recv()

    outs = pl.pallas_call(
        body, name=name,
        in_specs=[_HBM] * nb + [_SEM, _SEM] + [_ANY] * len(after),
        out_specs=[_HBM] * nb,
        out_shape=[pltpu.HBM(b.shape, b.dtype) for b in bufs],
        input_output_aliases={i: i for i in range(nb)},
        compiler_params=pltpu.CompilerParams(has_side_effects=_EFFECT),
    )(*bufs, send, recv, *after)
    return list(outs)


def _slot(buf, s, cols=None):
    if len(buf.shape) == 3:
        return buf.at[s] if cols is None else buf.at[s, :, pl.ds(cols[0], cols[1])]
    ns = buf.shape[1] // N_DEV
    first, width = (0, ns) if cols is None else cols
    return buf.at[:, pl.ds(pl.multiple_of(s * ns, LANE) + first, width)]


def _copies_gather_chips(nbuf, cols=None):
    def make(bufs, send, recv):
        x, y, c = _mesh_pos()
        targets = [(x, y, 1 - c)] + [(px, py, c) for px, py in _other_chips(x, y)]
        cps = []
        for a in range(nbuf):
            mine = _slot(bufs[a], 4 * x + 2 * y + c, cols)
            for k, to in enumerate(targets):
                cps.append(pltpu.make_async_remote_copy(src_ref=mine, dst_ref=mine, send_sem=send.at[4 * a + k],
                                                        recv_sem=recv.at[4 * a + k], device_id=to, device_id_type=_MESH))
        return cps

    return make, 4 * nbuf


def _copies_gather_forward(nbuf, cols=None):
    def make(bufs, send, recv):
        x, y, c = _mesh_pos()
        cps = []
        for a in range(nbuf):
            for j, (px, py) in enumerate(_other_chips(x, y)):
                slot = _slot(bufs[a], 4 * px + 2 * py + c, cols)
                cps.append(pltpu.make_async_remote_copy(src_ref=slot, dst_ref=slot, send_sem=send.at[3 * a + j],
                                                        recv_sem=recv.at[3 * a + j], device_id=(x, y, 1 - c),
                                                        device_id_type=_MESH))
        return cps

    return make, 3 * nbuf


def _copies_rs_sibling(n):
    def make(bufs, send, recv):
        x, y, c = _mesh_pos()
        cps = []
        for a in range(n):
            for q in range(4):
                cps.append(pltpu.make_async_remote_copy(
                    src_ref=_slot(bufs[a], 2 * q + (1 - c)), dst_ref=bufs[n + a].at[q], send_sem=send.at[4 * a + q],
                    recv_sem=recv.at[4 * a + q], device_id=(x, y, 1 - c), device_id_type=_MESH))
        return cps

    return make, 4 * n


def _copies_rs_chips(n):
    def make(bufs, send, recv):
        x, y, c = _mesh_pos()
        cps = []
        for a in range(n):
            for j, (px, py) in enumerate(_other_chips(x, y)):
                cps.append(pltpu.make_async_remote_copy(
                    src_ref=bufs[a].at[j], dst_ref=bufs[n + a].at[j], send_sem=send.at[3 * a + j],
                    recv_sem=recv.at[3 * a + j], device_id=(px, py, c), device_id_type=_MESH))
        return cps

    return make, 3 * n


def _cast_place(stacked, layer, slot, dtype, name, deps=(), by_cols=False):
    _, R, C = stacked.shape
    tr = _rows_tile(R, C)

    def body(slot_ref, x_ref, *rest):
        del slot_ref
        rest[-1][...] = x_ref[...].astype(dtype)

    if by_cols:
        out_spec = pl.BlockSpec((tr, C), lambda i, s: (i, s[0]))
        out_shape = jax.ShapeDtypeStruct((R, N_DEV * C), dtype)
    else:
        out_spec = pl.BlockSpec((None, tr, C), lambda i, s: (s[0], i, 0))
        out_shape = jax.ShapeDtypeStruct((N_DEV, R, C), dtype)
    return pl.pallas_call(
        body, name=name,
        grid_spec=pltpu.PrefetchScalarGridSpec(
            num_scalar_prefetch=1, grid=(R // tr,),
            in_specs=[pl.BlockSpec((None, tr, C), lambda i, s: (layer, i, 0))] + [_ANY] * len(deps),
            out_specs=out_spec),
        out_shape=out_shape,
        compiler_params=_params(("parallel",)),
    )(slot, stacked, *deps)


def _rows_tile(R, C, budget=2 * MIB):
    if R * C * 4 <= budget:
        return R
    for t in (2048, 1024, 704, 512, 352, 256, 128, 64, 32, 16, 8):
        if R % t == 0 and t * C * 4 <= budget:
            return t
    return R


def _chip_of_relation(j, pos):
    x, y = pos[0], pos[1]
    return 2 * jnp.where(j == 1, x, 1 - x) + jnp.where(j == 0, y, 1 - y)


def _pair_sum(grad, recv, pos, name):
    _, R, C = recv.shape
    tr = _rows_tile(R, C, 8 * MIB)

    def body(pos_ref, g_ref, r_ref, o_ref):
        del pos_ref
        o_ref[...] = (g_ref[...].astype(F32) + r_ref[...].astype(F32)).astype(BF16)

    if len(grad.shape) == 3:
        g_spec = pl.BlockSpec((None, tr, C), lambda j, i, p: (2 * _chip_of_relation(j, p) + p[2], i, 0))
    else:
        g_spec = pl.BlockSpec((tr, C), lambda j, i, p: (i, 2 * _chip_of_relation(j, p) + p[2]))
    return pl.pallas_call(
        body, name=name,
        grid_spec=pltpu.PrefetchScalarGridSpec(
            num_scalar_prefetch=1, grid=(3, R // tr),
            in_specs=[g_spec,
                      pl.BlockSpec((None, tr, C), lambda j, i, p: (_chip_of_relation(j, p), i, 0))],
            out_specs=pl.BlockSpec((None, tr, C), lambda j, i, p: (j, i, 0))),
        out_shape=jax.ShapeDtypeStruct((3, R, C), BF16),
        compiler_params=_params(("parallel", "parallel")),
    )(pos, grad, recv)


ADAM_LR = 0.001
ADAM_B1 = 0.9
ADAM_B2 = 0.999
ADAM_EPS = 1e-08
ADAM_WD = 0.01
ADAM_STEP = 10


def _adamw(w, g, m, v):
    m = ADAM_B1 * m + (1.0 - ADAM_B1) * g
    v = ADAM_B2 * v + (1.0 - ADAM_B2) * (g * g)
    m_hat = m / (1.0 - ADAM_B1 ** ADAM_STEP)
    v_hat = v / (1.0 - ADAM_B2 ** ADAM_STEP)
    delta = -ADAM_LR * (m_hat / (jnp.sqrt(v_hat) + ADAM_EPS) + ADAM_WD * w)
    return delta, m, v


def _adamw_big(grad, sib, recv, w, m, v, pos, layer, prev, name):
    L, R, C = w.shape
    tr = _rows_tile(R, C)
    has_prev = prev is not None

    def body(pos_ref, g_ref, s_ref, r0_ref, r1_ref, r2_ref, w_ref, m_ref, v_ref, *rest):
        del pos_ref
        go_ref, d_ref, mo_ref, vo_ref = rest[-4:]
        g = g_ref[...].astype(F32) + s_ref[...].astype(F32)
        g = ((g + r0_ref[...].astype(F32)) + r1_ref[...].astype(F32)) + r2_ref[...].astype(F32)
        delta, m2, v2 = _adamw(w_ref[...], g, m_ref[...], v_ref[...])
        go_ref[...] = g
        d_ref[...] = delta
        mo_ref[...] = m2
        vo_ref[...] = v2

    lay = pl.BlockSpec((None, tr, C), lambda i, p: (layer, i, 0))
    rspec = lambda j: pl.BlockSpec((None, tr, C), lambda i, p: (j, i, 0))
    mine = lambda p: 4 * p[0] + 2 * p[1] + p[2]
    if len(grad.shape) == 3:
        g_spec = pl.BlockSpec((None, tr, C), lambda i, p: (mine(p), i, 0))
    else:
        g_spec = pl.BlockSpec((tr, C), lambda i, p: (i, mine(p)))
    s_spec = pl.BlockSpec((None, tr, C), lambda i, p: (2 * p[0] + p[1], i, 0))
    in_specs = [g_spec, s_spec, rspec(0), rspec(1), rspec(2), lay, lay, lay]
    args = [pos, grad, sib, recv, recv, recv, w, m, v]
    aliases = {}
    if has_prev:
        in_specs += [_ANY] * 4
        args += list(prev)
        aliases = {9 + k: k for k in range(4)}
    return pl.pallas_call(
        body, name=name,
        grid_spec=pltpu.PrefetchScalarGridSpec(
            num_scalar_prefetch=1, grid=(R // tr,), in_specs=in_specs, out_specs=[lay] * 4),
        out_shape=[jax.ShapeDtypeStruct((L, R, C), F32)] * 4,
        input_output_aliases=aliases,
        compiler_params=_params(("parallel",)),
    )(*args)


def _sum_slots(gathered, name):
    _, R, C = gathered.shape
    tr = _rows_tile(R, C, 2 * MIB)

    def body(x_ref, o_ref):
        s = x_ref[0]
        for k in range(1, N_DEV):
            s = s + x_ref[k]
        o_ref[...] = s

    return pl.pallas_call(
        body, name=name, grid=(R // tr,),
        in_specs=[pl.BlockSpec((N_DEV, tr, C), lambda i: (0, i, 0))],
        out_specs=pl.BlockSpec((tr, C), lambda i: (i, 0)),
        out_shape=jax.ShapeDtypeStruct((R, C), F32),
        compiler_params=_params(("parallel",)),
    )(gathered)


def _adamw_small(w, g, m, v, name):
    R, C = w.shape
    tr = _rows_tile(R, C, 2 * MIB)

    def body(w_ref, g_ref, m_ref, v_ref, d_ref, mo_ref, vo_ref):
        d_ref[...], mo_ref[...], vo_ref[...] = _adamw(w_ref[...], g_ref[...], m_ref[...], v_ref[...])

    spec = pl.BlockSpec((tr, C), lambda i: (i, 0))
    return pl.pallas_call(
        body, name=name, grid=(R // tr,),
        in_specs=[spec] * 4, out_specs=[spec] * 3,
        out_shape=[jax.ShapeDtypeStruct((R, C), F32)] * 3,
        compiler_params=_params(("parallel",)),
    )(w, g, m, v)


def _pack(arrs):
    parts = []
    for a in arrs:
        flat = a.reshape(-1)
        pad = (-flat.shape[0]) % (8 * LANE)
        if pad:
            flat = jnp.pad(flat, (0, pad))
        parts.append(flat.reshape(-1, LANE))
    return jnp.concatenate(parts, axis=0)


def _unpack(packed, shapes):
    out, r = [], 0
    for s in shapes:
        n = 1
        for d in s:
            n *= d
        rows = -(-n // (8 * LANE)) * 8
        out.append(packed[r:r + rows].reshape(-1)[:n].reshape(s))
        r += rows
    return out


def kernel(x, norm1_g, w_in, w_pool, pool_scale, sgu_norm_g, sgu_w, sgu_b, conv_w, w_pool_out, w_sgu_out, w_conv_out, w_o, norm2_g, w_up, ffn_conv_w, w_down, final_g, loss_target, m_norm1_g, m_w_in, m_w_pool, m_pool_scale, m_sgu_norm_g, m_sgu_w, m_sgu_b, m_conv_w, m_w_pool_out, m_w_sgu_out, m_w_conv_out, m_w_o, m_norm2_g, m_w_up, m_ffn_conv_w, m_w_down, m_final_g, v_norm1_g, v_w_in, v_w_pool, v_pool_scale, v_sgu_norm_g, v_sgu_w, v_sgu_b, v_conv_w, v_w_pool_out, v_w_sgu_out, v_w_conv_out, v_w_o, v_norm2_g, v_w_up, v_ffn_conv_w, v_w_down, v_final_g):
    L = norm1_g.shape[0]
    x0 = x[0]
    tgt = loss_target[0]
    G, PG = w_pool.shape[1], w_pool.shape[3]
    F = w_down.shape[1] * N_DEV
    T, D = x0.shape
    tT = _pick(T, (1024, 512, 256, 128))
    tF = _pick(F, (512, 256))
    tF2 = _pick(2 * F, (1024, 512, 256))
    kF = _pick(F, (2816, 2048, 1024, 512))
    px, py, pc = _mesh_pos()
    me = 4 * px + 2 * py + pc
    pos = jnp.stack([px, py, pc]).astype(jnp.int32)

    slot = jnp.reshape(me, (1,)).astype(jnp.int32)
    ns_in = w_in.shape[2]
    nq = 3 if ns_in % (3 * 256) == 0 else 1
    chunk_cols = [(q * (ns_in // nq), ns_in // nq) for q in range(nq)]
    win0 = _cast_place(w_in, 0, slot, BF16, "place_w_in_0", by_cols=True)
    first, tok0 = _split_start([([win0],) + _copies_gather_chips(1, chunk_cols[0])], "ag_start_first")
    win0 = first[0][2][0]
    h1_first = _rms_fwd(x0, norm1_g[0][None], "rms1_fwd_0")
    stage1, group_sizes, started1 = {}, {0: 1}, {}

    def place_groups(l, deps):
        grp = [[("w_in", w_in, BF16, True)],
               [("w_pool_out", w_pool_out, BF16, True), ("w_sgu_out", w_sgu_out, BF16, True),
                ("w_conv_out", w_conv_out, BF16, True), ("w_pool", w_pool.reshape(L, -1, PG), BF16, False),
                ("conv_w", conv_w, F32, True), ("w_o", w_o, BF16, False)],
               [("w_up", w_up, BF16, True), ("ffn_conv_w", ffn_conv_w, F32, True)],
               [("w_down", w_down, BF16, False)]]
        ks = []
        for gi, members in enumerate(grp):
            if l == 0 and gi == 0:
                continue
            k = 4 * l + gi
            bufs = [_cast_place(a, l, slot, dt, f"place_{n}_{l}", deps, bc) for n, a, dt, bc in members]
            stage1[k] = (bufs,) + _copies_gather_chips(len(members))
            group_sizes[k] = len(members)
            ks.append(k)
        return ks

    def start_groups(ks, extra, name, deps=()):
        st, tok = _split_start(extra + [stage1[k] for k in ks], name, deps)
        for k, s in zip(ks, st[len(extra):]):
            started1[k] = s
        return st[:len(extra)], tok

    rest, tok_start0 = start_groups(place_groups(0, [tok0, h1_first]),
                                    [([win0],) + _copies_gather_chips(1, cols) for cols in chunk_cols[1:]], "ag_start_0")
    if nq > 1:
        win0 = rest[0][2][0]
    chunk_sems = [first[0][:2]] + [r[:2] for r in rest]
    later_groups = [place_groups(l, [tok_start0]) for l in range(1, L)]
    placed_later = [b for ks in later_groups for k in ks for b in stage1[k][0]]
    started2 = {}

    def gather_forward(k, after):
        n = group_sizes[k]
        bufs = _split_wait(started1[k], stage1[k][1], after, f"ag_wait1_{k}")
        st, tok = _split_start([(bufs,) + _copies_gather_forward(n)], f"ag_fwd_{k}")
        started2[k] = st[0]
        return tok

    def gather_done(k, after):
        return _split_wait(started2.pop(k), _copies_gather_forward(group_sizes[k])[0], after, f"ag_wait2_{k}")

    W, saved = [], []
    xc = x0
    for l in range(L):
        w = dict(
            sguw=sgu_w[l].astype(BF16),
            bexp=jnp.broadcast_to(sgu_b[l].T[:, :, None],
                                  (sgu_b.shape[2], sgu_b.shape[1], sgu_norm_g.shape[1] // sgu_b.shape[2])),
            g1=norm1_g[l][None], g2=norm2_g[l][None], gs=sgu_norm_g[l][None], scale=pool_scale[l][None])
        ka, kb, kc, kd = 4 * l, 4 * l + 1, 4 * l + 2, 4 * l + 3
        h1 = h1_first if l == 0 else _rms_fwd(xc, w["g1"], f"rms1_fwd_{l}")
        if l == 0:
            P, after = None, [h1] + placed_later
            for q, cols in enumerate(chunk_cols):
                (win0,) = _split_wait(chunk_sems[q] + ([win0],), _copies_gather_chips(1, cols)[0], after,
                                      f"ag_wait1_0_{q}")
                st, tok = _split_start([([win0],) + _copies_gather_forward(1, cols)], f"ag_fwd_0_{q}")
                (win0,) = _split_wait(st[0], _copies_gather_forward(1, cols)[0], after, f"ag_wait2_0_{q}")
                P = _mm(h1, win0, mode="nn", out_dtype=BF16, name=f"proj_in_0_{q}", tm=_pick(T, (2048, tT)), tn=cols[1], tk=D,
                        cols=(q, nq), prev=P, deps=[tok])
                after = P
            w["win"] = win0
        else:
            (w["win"],) = gather_done(ka, h1)
            tok = gather_forward(kb, h1)
            P = _mm(h1, w["win"], mode="nn", out_dtype=BF16, name=f"proj_in_{l}", tm=tT, tn=1024, tk=D, deps=[tok])
        if l == 0:
            gather_forward(kb, P)
        w["wa"], w["wb"], w["wc"], wp, w["cw"], wo = gather_done(kb, P)
        w.update(wpool=wp.reshape(N_DEV, G, -1, PG).transpose(1, 0, 2, 3).reshape(G, PG, PG),
                 wo=wo.reshape(-1, wo.shape[2]))
        ya = _pool_fwd(P, w["wpool"], w["scale"], f"pool_fwd_{l}")
        yb = _sgu_fwd(P, w["sguw"], w["bexp"], w["gs"], f"sgu_fwd_{l}")
        yc = _conv_fwd(P, w["cw"], f"conv_fwd_{l}")
        oa, ob, oc, M = _merge_fwd(P, (ya, yb, yc), (w["wa"], w["wb"], w["wc"]), f"merge_fwd_{l}")
        tok = gather_forward(kc, M)
        if l == 0:
            for l2 in range(1, L):
                start_groups(later_groups[l2 - 1], [], f"ag_start_{l2}", [tok])
        x1, h2 = _mm_res_rms(M, w["wo"], xc, w["g2"], f"proj_o_{l}", deps=[tok])
        w["wup"], w["fw"] = gather_done(kc, h2)
        tok = gather_forward(kd, h2)
        U = _mm(h2, w["wup"], mode="nn", out_dtype=BF16, name=f"proj_up_{l}", tm=tT, tn=tF2, tk=D, deps=[tok])
        act = _glu_fwd(U, w["fw"], f"glu_fwd_{l}")
        (wdown,) = gather_done(kd, act)
        w["wdown"] = wdown.reshape(-1, wdown.shape[2])
        deps = [gather_forward(4 * (l + 1), act)] if l + 1 < L else []
        x2 = _mm(act, w["wdown"], mode="nn", out_dtype=F32, name=f"proj_down_{l}", tm=tT, tn=512, tk=F, res=x1,
                 deps=deps)
        if deps:
            tok = deps[0]
        W.append(w)
        saved.append(dict(x0=xc, h1=h1, P=P, ya=ya, yb=yb, yc=yc, oa=oa, ob=ob, oc=oc, M=M, x1=x1, h2=h2, U=U, act=act))
        xc = x2

    dx, dxb, d_final_g, loss_part = _loss_head(xc, tgt, final_g[None], "loss_head")
    loss = lax.psum(loss_part[0, 0], ("x", "y", "c"))

    wmv = dict(w_in=(w_in, m_w_in, v_w_in), w_up=(w_up, m_w_up, v_w_up), w_o=(w_o, m_w_o, v_w_o),
               w_down=(w_down, m_w_down, v_w_down), w_pool_out=(w_pool_out, m_w_pool_out, v_w_pool_out),
               w_sgu_out=(w_sgu_out, m_w_sgu_out, v_w_sgu_out), w_conv_out=(w_conv_out, m_w_conv_out, v_w_conv_out),
               w_pool=(w_pool, m_w_pool, v_w_pool))
    adam_out = {}

    def rs_begin(tag, layer, names, grads):
        n = len(grads)
        slice_shape = lambda g: g.shape[1:] if len(g.shape) == 3 else (g.shape[0], g.shape[1] // N_DEV)
        lands = [lax.empty((4,) + slice_shape(g), BF16) for g in grads]
        st, tok = _split_start([(list(grads) + lands,) + _copies_rs_sibling(n)], f"rs_sib_{tag}")
        return dict(tag=tag, layer=layer, names=names, n=n, st=st[0]), tok

    def rs_to_chips(state, after):
        n, tag = state["n"], state["tag"]
        bufs = _split_wait(state["st"], _copies_rs_sibling(n)[0], after, f"rs_sibw_{tag}")
        grads, sib = bufs[:n], bufs[n:]
        pairs = [_pair_sum(g, r, pos, f"pair_{tag}_{i}") for i, (g, r) in enumerate(zip(grads, sib))]
        lands = [lax.empty(p.shape, BF16) for p in pairs]
        st, tok = _split_start([(pairs + lands,) + _copies_rs_chips(n)], f"rs_chips_{tag}")
        state.update(st=st[0], grads=grads, sib=sib)
        return tok

    def rs_finish(state, after):
        n, tag, layer = state["n"], state["tag"], state["layer"]
        bufs = _split_wait(state["st"], _copies_rs_chips(n)[0], after, f"rs_chipsw_{tag}")
        for name, g, sb, recv in zip(state["names"], state["grads"], state["sib"], bufs[n:]):
            wk, mk, vk = (a.reshape(L, -1, a.shape[-1]) for a in wmv[name])
            adam_out[name] = _adamw_big(g, sb, recv, wk, mk, vk, pos, layer, adam_out.get(name),
                                        f"adamw_{name}_{layer}")

    small = [None] * L
    pending = []
    deps = []
    for l in reversed(range(L)):
        w, s = W[l], saved[l]
        dact = _mm(dxb, w["wdown"], mode="nt", out_dtype=BF16, name=f"d_act_{l}", tm=tT, tn=tF, tk=D, deps=deps)
        g_down = _mm(s["act"], dxb, mode="tn", out_dtype=BF16, name=f"g_down_{l}", tm=tF, tn=1024, tk=T)
        dug, duv, dfwg, dfwv = _glu_bwd(s["U"], dact, w["fw"], f"glu_bwd_{l}")
        dh2 = _mm(dug, w["wup"], mode="nt", out_dtype=F32, name=f"d_h2_{l}", tm=tT, tn=1024, tk=kF, a2=duv)
        g_up = _mm(s["h2"], dug, mode="tn", out_dtype=BF16, name=f"g_upa_{l}", tm=1024, tn=tF, tk=T, out_n=2 * F)
        g_up = _mm(s["h2"], duv, mode="tn", out_dtype=BF16, name=f"g_upb_{l}", tm=1024, tn=tF, tk=T, out_n=2 * F,
                   out_off=F // tF, prev=g_up)
        ra, tok = rs_begin(f"a{l}", l, ["w_down", "w_up"], [g_down.reshape(N_DEV, -1, g_down.shape[1]), g_up])
        if pending:
            rs_finish(pending.pop(0), g_up)
        dx1, dx1b, dg2 = _rms_bwd(dh2, s["x1"], w["g2"], dx, f"rms2_bwd_{l}")
        dM = _mm(dx1b, w["wo"], mode="nt", out_dtype=BF16, name=f"d_m_{l}", tm=tT, tn=1024, tk=D, deps=[tok])
        g_o = _mm(s["M"], dx1b, mode="tn", out_dtype=BF16, name=f"g_o_{l}", tm=1024, tn=1024, tk=T)
        doa, dob, doc, dPg = _merge_bwd(s["P"], dM, (s["oa"], s["ob"], s["oc"]), f"merge_bwd_{l}")
        tok = rs_to_chips(ra, dPg)
        if pending:
            rs_finish(pending.pop(0), dPg)
        dya = _mm(doa, w["wa"], mode="nt", out_dtype=BF16, name=f"d_ya_{l}", tm=tT, tn=1024, tk=D, deps=[tok])
        dyb = _mm(dob, w["wb"], mode="nt", out_dtype=BF16, name=f"d_yb_{l}", tm=tT, tn=1024, tk=D)
        dyc = _mm(doc, w["wc"], mode="nt", out_dtype=BF16, name=f"d_yc_{l}", tm=tT, tn=1024, tk=D)
        g_a = _mm(s["ya"], doa, mode="tn", out_dtype=BF16, name=f"g_a_{l}", tm=1024, tn=1024, tk=T)
        g_b = _mm(s["yb"], dob, mode="tn", out_dtype=BF16, name=f"g_b_{l}", tm=1024, tn=1024, tk=T)
        g_c = _mm(s["yc"], doc, mode="tn", out_dtype=BF16, name=f"g_c_{l}", tm=1024, tn=1024, tk=T)
        da, g_pool, dscale = _pool_bwd(s["P"], dya, w["wpool"], w["scale"], f"pool_bwd_{l}")
        g_pool_s = g_pool.reshape(G, N_DEV, -1, PG).transpose(1, 0, 2, 3).reshape(N_DEV, -1, PG).astype(BF16)
        rb, tok = rs_begin(f"b{l}", l, ["w_o", "w_pool_out", "w_sgu_out", "w_conv_out", "w_pool"],
                           [g_o.reshape(N_DEV, -1, g_o.shape[1]), g_a, g_b, g_c, g_pool_s])
        if pending:
            rs_finish(pending.pop(0), da)
        dPl, g_sguw, db_exp, dgs = _sgu_bwd(s["P"], dyb, da, w["sguw"], w["bexp"], w["gs"], f"sgu_bwd_{l}")
        dPl, dcw = _conv_bwd(s["P"], dyc, w["cw"], dPl, f"conv_bwd_{l}")
        tok2 = rs_to_chips(rb, dPl)
        half = dPl.shape[1]
        g_in = _mm(s["h1"], dPl, mode="tn", out_dtype=BF16, name=f"g_ina_{l}", tm=1024, tn=1024, tk=T, out_n=2 * half,
                   deps=[tok, tok2])
        g_in = _mm(s["h1"], dPg, mode="tn", out_dtype=BF16, name=f"g_inb_{l}", tm=1024, tn=1024, tk=T, out_n=2 * half,
                   out_off=half // 1024, prev=g_in)
        rc, tok = rs_begin(f"c{l}", l, ["w_in"], [g_in])
        if l > 0:
            dh1 = _mm(dPl, w["win"], mode="nt", out_dtype=F32, name=f"d_h1_{l}", tm=tT, tn=1024, tk=D, a2=dPg,
                      deps=[tok])
            tok = rs_to_chips(rc, dh1)
        else:
            dh1 = _mm(dPl, w["win"], mode="nt", out_dtype=F32, name=f"d_h1a_{l}", tm=tT, tn=1024, tk=D, deps=[tok])
            tok = rs_to_chips(rc, dh1)
            dh1 = _mm(dPg, w["win"], mode="nt", out_dtype=F32, name=f"d_h1b_{l}", tm=tT, tn=1024, tk=D,
                      b_koff=half // D, res=dh1, deps=[tok])
        dx, dxb, dg1 = _rms_bwd(dh1, s["x0"], w["g1"], dx1, f"rms1_bwd_{l}", deps=[tok])
        deps = []
        pending = [ra, rb, rc]
        small[l] = dict(norm1_g=dg1[0], pool_scale=dscale[0], sgu_norm_g=dgs[0], sgu_w=g_sguw,
                        sgu_b=db_exp[:, :, 0].T, conv_w=dcw[:3], norm2_g=dg2[0],
                        ffn_conv_w=jnp.concatenate([dfwg[:3], dfwv[:3]], axis=1))
    grad_x = dx[None]

    snames = ["norm1_g", "pool_scale", "sgu_norm_g", "sgu_w", "sgu_b", "conv_w", "norm2_g", "ffn_conv_w"]
    sparts = [jnp.stack([small[l][n] for l in range(L)]) for n in snames] + [d_final_g[0]]
    snames = snames + ["final_g"]
    packed = _pack(sparts)
    sbuf = _cast_place(packed[None], 0, slot, F32, "place_small")
    st, tok = _split_start([([sbuf],) + _copies_gather_chips(1)], "ag_small_start")
    rs_finish(pending[0], tok)
    rs_finish(pending[1], tok)
    bufs = _split_wait(st[0], _copies_gather_chips(1)[0], [o[0] for o in adam_out.values()], "ag_small_wait1")
    st, tok = _split_start([(bufs,) + _copies_gather_forward(1)], "ag_small_fwd")
    (gathered_small,) = _split_wait(st[0], _copies_gather_forward(1)[0], tok, "ag_small_wait2")
    total = _sum_slots(gathered_small, "sum_small")
    sgrads = dict(zip(snames, _unpack(total, [p.shape for p in sparts])))
    for n, width in (("conv_w", conv_w.shape[2]), ("ffn_conv_w", ffn_conv_w.shape[2])):
        sgrads[n] = lax.dynamic_slice_in_dim(sgrads[n], me * width, width, axis=2)
    sw = dict(norm1_g=(norm1_g, m_norm1_g, v_norm1_g), pool_scale=(pool_scale, m_pool_scale, v_pool_scale),
              sgu_norm_g=(sgu_norm_g, m_sgu_norm_g, v_sgu_norm_g), sgu_w=(sgu_w, m_sgu_w, v_sgu_w),
              sgu_b=(sgu_b, m_sgu_b, v_sgu_b), conv_w=(conv_w, m_conv_w, v_conv_w),
              norm2_g=(norm2_g, m_norm2_g, v_norm2_g), ffn_conv_w=(ffn_conv_w, m_ffn_conv_w, v_ffn_conv_w),
              final_g=(final_g, m_final_g, v_final_g))
    shapes = [sw[n][0].shape for n in snames]
    upd = _adamw_small(_pack([sw[n][0] for n in snames]), _pack([sgrads[n] for n in snames]),
                       _pack([sw[n][1] for n in snames]), _pack([sw[n][2] for n in snames]), "adamw_small")
    sdelta, sm, sv = (dict(zip(snames, _unpack(u, shapes))) for u in upd)
    res = {n: [sgrads[n], sdelta[n], sm[n], sv[n]] for n in snames}

    rs_finish(pending[2], [upd[0]] + [o[0] for o in adam_out.values()])
    for n, outs in adam_out.items():
        res[n] = [o.reshape(wmv[n][0].shape) for o in outs]

    order = ["norm1_g", "w_in", "w_pool", "pool_scale", "sgu_norm_g", "sgu_w", "sgu_b", "conv_w", "w_pool_out",
             "w_sgu_out", "w_conv_out", "w_o", "norm2_g", "w_up", "ffn_conv_w", "w_down", "final_g"]
    return (loss, grad_x) + tuple(res[n][k] for k in range(4) for n in order)
```

```python
import functools

import jax
import jax.numpy as jnp
from jax import lax
from jax.experimental import pallas as pl
from jax.experimental.pallas import tpu as pltpu

BF16 = jnp.bfloat16
F32 = jnp.float32
EPS = 1e-6
MIB = 1024 * 1024
V7X_VMEM_BYTES = 64 * MIB
VMEM_LIMIT = 56 * MIB
LANE = 128
N_DEV = 8


def _params(sem, **kw):
    return pltpu.CompilerParams(dimension_semantics=sem, vmem_limit_bytes=VMEM_LIMIT, **kw)


def _pick(n, cands):
    for c in cands:
        if n % c == 0:
            return c
    return n


_DIMS = {"nn": (((1,), (0,)), ((), ())), "nt": (((1,), (1,)), ((), ())), "tn": (((0,), (0,)), ((), ()))}


def _mm(a, b, *, mode, out_dtype, name, tm, tn, tk, res=None, prev=None, deps=(), cols=(0, 1), b_koff=0,
        out_off=0, out_n=None, a2=None):
    if mode == "nn":
        M, Kc = a.shape
        nj = (b.shape[1] // tn - cols[0] + cols[1] - 1) // cols[1]
        out_w = b.shape[1]
    elif mode == "nt":
        M, Kc = a.shape
        nj = b.shape[0] // tn
        out_w = b.shape[0]
    else:
        Kc, M = a.shape
        nj = b.shape[1] // tn
        out_w = out_n or b.shape[1]
    nk1 = Kc // tk
    nk = nk1 + (a2.shape[1] // tk if a2 is not None else 0)
    assert M % tm == 0 and Kc % tk == 0 and (a2 is None or mode == "nt"), (name, M, Kc, tm, tk)
    off, stride = cols
    if mode == "nn":
        a_spec = pl.BlockSpec((tm, tk), lambda i, j, k: (i, k))
        b_spec = pl.BlockSpec((tk, tn), lambda i, j, k: (k, off + j * stride))
        o_spec = pl.BlockSpec((tm, tn), lambda i, j, k: (i, off + j * stride))
    elif mode == "nt":
        a_spec = pl.BlockSpec((tm, tk), lambda i, j, k: (i, jnp.minimum(k, nk1 - 1)))
        a2_spec = pl.BlockSpec((tm, tk), lambda i, j, k: (i, jnp.maximum(k - nk1, 0)))
        b_spec = pl.BlockSpec((tn, tk), lambda i, j, k: (j, b_koff + k))
        o_spec = pl.BlockSpec((tm, tn), lambda i, j, k: (i, j))
    else:
        a_spec = pl.BlockSpec((tk, tm), lambda i, j, k: (k, i))
        b_spec = pl.BlockSpec((tk, tn), lambda i, j, k: (k, j))
        o_spec = pl.BlockSpec((tm, tn), lambda i, j, k: (i, out_off + j))
    dims = _DIMS[mode]
    has_res = res is not None
    has_prev = prev is not None
    n_in = 2 + has_res + has_prev + len(deps) + (a2 is not None)

    def body(*refs):
        a_ref, b_ref = refs[0], refs[1]
        o_ref = refs[n_in]

        def finish(acc):
            if has_res:
                acc = acc + refs[2][...]
            o_ref[...] = acc.astype(o_ref.dtype)

        if nk == 1:
            finish(lax.dot_general(a_ref[...], b_ref[...], dims, preferred_element_type=F32))
        else:
            acc_ref = refs[n_in + 1]
            k = pl.program_id(2)

            @pl.when(k == 0)
            def _():
                acc_ref[...] = jnp.zeros(acc_ref.shape, F32)

            if a2 is None:
                acc_ref[...] += lax.dot_general(a_ref[...], b_ref[...], dims, preferred_element_type=F32)
            else:
                @pl.when(k < nk1)
                def _():
                    acc_ref[...] += lax.dot_general(a_ref[...], b_ref[...], dims, preferred_element_type=F32)

                @pl.when(k >= nk1)
                def _():
                    acc_ref[...] += lax.dot_general(refs[n_in - 1][...], b_ref[...], dims, preferred_element_type=F32)

            @pl.when(k == nk - 1)
            def _():
                finish(acc_ref[...])

    in_specs = [a_spec, b_spec]
    args = [a, b]
    if has_res:
        in_specs.append(pl.BlockSpec((tm, tn), lambda i, j, k: (i, j)))
        args.append(res)
    aliases = {}
    if has_prev:
        aliases = {len(args): 0}
        in_specs.append(pl.BlockSpec(memory_space=pl.ANY))
        args.append(prev)
    for d in deps:
        in_specs.append(pl.BlockSpec(memory_space=pl.ANY))
        args.append(d)
    if a2 is not None:
        in_specs.append(a2_spec)
        args.append(a2)
    return pl.pallas_call(
        body,
        name=name,
        grid=(M // tm, nj, nk),
        in_specs=in_specs,
        out_specs=o_spec,
        out_shape=jax.ShapeDtypeStruct((M, out_w), out_dtype),
        scratch_shapes=[pltpu.VMEM((tm, tn), F32)] if nk > 1 else [],
        input_output_aliases=aliases,
        compiler_params=_params(("parallel", "parallel", "arbitrary")),
    )(*args)


def _mm_res_rms(a, b, res, g, name, deps=()):
    M, K = a.shape
    N = b.shape[1]
    tm = _pick(M, (512, 256, 128))

    def body(a_ref, b_ref, res_ref, g_ref, *rest):
        x_ref, h_ref = rest[-2:]
        xv = res_ref[...] + jnp.dot(a_ref[...], b_ref[...], preferred_element_type=F32)
        x_ref[...] = xv
        r = lax.rsqrt(jnp.mean(xv * xv, axis=-1, keepdims=True) + EPS)
        h_ref[...] = (xv * r * g_ref[...]).astype(BF16)

    row = pl.BlockSpec((tm, N), lambda i: (i, 0))
    return pl.pallas_call(
        body, name=name, grid=(M // tm,),
        in_specs=[pl.BlockSpec((tm, K), lambda i: (i, 0)), pl.BlockSpec((K, N), lambda i: (0, 0)), row,
                  pl.BlockSpec((1, N), lambda i: (0, 0))] + [pl.BlockSpec(memory_space=pl.ANY)] * len(deps),
        out_specs=[row, row],
        out_shape=[jax.ShapeDtypeStruct((M, N), F32), jax.ShapeDtypeStruct((M, N), BF16)],
        compiler_params=_params(("parallel",)),
    )(a, b, res, g, *deps)


HALO = 16
CH = 256


def _fill_pad(pad_ref, chunk_fn, T):
    z = jnp.zeros((HALO, pad_ref.shape[1]), F32)
    pad_ref[pl.ds(0, HALO), :] = z
    pad_ref[pl.ds(HALO + T, HALO), :] = z

    def body(c, carry):
        r0 = pl.multiple_of(c * CH, CH)
        pad_ref[pl.ds(HALO + r0, CH), :] = chunk_fn(r0)
        return carry

    lax.fori_loop(0, T // CH, body, 0)


def _ext(pad_ref, r0):
    return pad_ref[pl.ds(r0, CH + 2 * HALO), :]


def _tap(ext, o):
    if o == 0:
        return ext[HALO:HALO + CH]
    return pltpu.roll(ext, (-o) % ext.shape[0], axis=0)[HALO:HALO + CH]


def _chunks(T, fn, init=0):
    def body(c, carry):
        return fn(pl.multiple_of(c * CH, CH), carry)

    return lax.fori_loop(0, T // CH, body, init)


def _conv3(ext, w):
    return _tap(ext, -1) * w[0:1] + _tap(ext, 0) * w[1:2] + _tap(ext, 1) * w[2:3]


def _conv3_t(ext, w):
    return _tap(ext, 1) * w[0:1] + _tap(ext, 0) * w[1:2] + _tap(ext, -1) * w[2:3]


def _sigmoid(x):
    return 0.5 * (jnp.tanh(0.5 * x) + 1.0)


def _silu(x):
    h = 0.5 * x
    return h * (jnp.tanh(h) + 1.0)


_GELU_C = 0.7978845608028654


def _gelu(x):
    return 0.5 * x * (1.0 + jnp.tanh(_GELU_C * (x + 0.044715 * (x * x * x))))


def _gelu_grad(x):
    t = jnp.tanh(_GELU_C * (x + 0.044715 * (x * x * x)))
    return 0.5 * (1.0 + t) + 0.5 * x * (1.0 - t * t) * (_GELU_C * (1.0 + 3.0 * 0.044715 * (x * x)))


def _row_tile(T):
    return _pick(T, (256, 128))


def _rms_fwd(x, g, name):
    T, D = x.shape
    tm = _row_tile(T)

    def body(x_ref, g_ref, h_ref):
        xv = x_ref[...]
        r = lax.rsqrt(jnp.mean(xv * xv, axis=-1, keepdims=True) + EPS)
        h_ref[...] = (xv * r * g_ref[...]).astype(BF16)

    return pl.pallas_call(
        body, name=name, grid=(T // tm,),
        in_specs=[pl.BlockSpec((tm, D), lambda i: (i, 0)), pl.BlockSpec((1, D), lambda i: (0, 0))],
        out_specs=pl.BlockSpec((tm, D), lambda i: (i, 0)),
        out_shape=jax.ShapeDtypeStruct((T, D), BF16),
        compiler_params=_params(("parallel",)),
    )(x, g)


def _rms_bwd(dh, x, g, dres, name, deps=()):
    T, D = x.shape
    tm = _row_tile(T)

    def body(dh_ref, x_ref, g_ref, dres_ref, *rest):
        dx_ref, dxb_ref, dg_ref = rest[len(deps):]
        i = pl.program_id(0)
        xv = x_ref[...]
        r = lax.rsqrt(jnp.mean(xv * xv, axis=-1, keepdims=True) + EPS)
        n = xv * r
        dh_v = dh_ref[...].astype(F32)
        dn = dh_v * g_ref[...]
        dx = dres_ref[...] + r * (dn - n * jnp.mean(dn * n, axis=-1, keepdims=True))
        dx_ref[...] = dx
        dxb_ref[...] = dx.astype(BF16)
        dg = jnp.sum(dh_v * n, axis=0, keepdims=True)

        @pl.when(i == 0)
        def _():
            dg_ref[...] = dg

        @pl.when(i > 0)
        def _():
            dg_ref[...] += dg

    row = pl.BlockSpec((tm, D), lambda i: (i, 0))
    vec = pl.BlockSpec((1, D), lambda i: (0, 0))
    return pl.pallas_call(
        body, name=name, grid=(T // tm,),
        in_specs=[row, row, vec, row] + [pl.BlockSpec(memory_space=pl.ANY)] * len(deps),
        out_specs=[row, row, vec],
        out_shape=[jax.ShapeDtypeStruct((T, D), F32), jax.ShapeDtypeStruct((T, D), BF16),
                   jax.ShapeDtypeStruct((1, D), F32)],
        compiler_params=_params(("arbitrary",)),
    )(dh, x, g, dres, *deps)


def _loss_head(x, tgt, g, name):
    T, D = x.shape
    tm = _row_tile(T)

    def body(x_ref, t_ref, g_ref, dx_ref, dxb_ref, dg_ref, l_ref):
        i = pl.program_id(0)
        xv = x_ref[...]
        gv = g_ref[...]
        r = lax.rsqrt(jnp.mean(xv * xv, axis=-1, keepdims=True) + EPS)
        n = xv * r
        e = n * gv - t_ref[...]
        dy = e * (1.0 / D)
        dn = dy * gv
        dx = r * (dn - n * jnp.mean(dn * n, axis=-1, keepdims=True))
        dx_ref[...] = dx
        dxb_ref[...] = dx.astype(BF16)
        dg = jnp.sum(dy * n, axis=0, keepdims=True)
        per_tok = jnp.mean(e * e, axis=-1, keepdims=True)
        lv = jnp.broadcast_to(0.5 * jnp.sum(per_tok, axis=0, keepdims=True), (1, LANE))

        @pl.when(i == 0)
        def _():
            dg_ref[...] = dg
            l_ref[...] = lv

        @pl.when(i > 0)
        def _():
            dg_ref[...] += dg
            l_ref[...] += lv

    row = pl.BlockSpec((tm, D), lambda i: (i, 0))
    vec = pl.BlockSpec((1, D), lambda i: (0, 0))
    return pl.pallas_call(
        body, name=name, grid=(T // tm,),
        in_specs=[row, row, vec],
        out_specs=[row, row, vec, pl.BlockSpec((1, LANE), lambda i: (0, 0))],
        out_shape=[jax.ShapeDtypeStruct((T, D), F32), jax.ShapeDtypeStruct((T, D), BF16),
                   jax.ShapeDtypeStruct((1, D), F32), jax.ShapeDtypeStruct((1, LANE), F32)],
        compiler_params=_params(("arbitrary",)),
    )(x, tgt, g)


POOL_WINDOWS = (2, 4, 8, 16)


def _pool_cnt(r0, w, T, shape):
    t = r0 + lax.broadcasted_iota(jnp.int32, shape, 0)
    lo = jnp.maximum(t - w // 2, 0)
    hi = jnp.minimum(t + (w - w // 2 - 1), T - 1)
    return (hi - lo + 1).astype(F32)


def _window_sum(ext, w, mirrored=False):
    n = ext.shape[0]
    cur = ext + pltpu.roll(ext, n - 1 if mirrored else 1, axis=0)
    width, sh = 2, 1
    while width < w:
        cur = pltpu.roll(cur, sh, axis=0) + pltpu.roll(cur, n - sh, axis=0)
        width, sh = 2 * width, 2 * sh
    return cur[HALO:HALO + CH]


def _pooled(pad_ref, r0, w, T):
    ext = _ext(pad_ref, r0)
    cur = ext[HALO:HALO + CH]
    return _window_sum(ext, w) / _pool_cnt(r0, w, T, cur.shape) - cur


def _pool_fwd(P, w_pool, scale, name):
    T = P.shape[0]
    G, PG, _ = w_pool.shape

    def body(a_ref, w_ref, s_ref, o_ref, pad_ref):
        g = pl.program_id(0)
        _fill_pad(pad_ref, lambda r0: a_ref[pl.ds(r0, CH), :].astype(F32), T)
        for gi, w in enumerate(POOL_WINDOWS):
            @pl.when(g == gi)
            def _(w=w):
                def chunk(r0, carry):
                    pooled = _pooled(pad_ref, r0, w, T).astype(BF16)
                    y = jnp.dot(pooled, w_ref[...], preferred_element_type=F32) * s_ref[...]
                    o_ref[pl.ds(r0, CH), :] = y.astype(BF16)
                    return carry

                _chunks(T, chunk)

    return pl.pallas_call(
        body, name=name, grid=(G,),
        in_specs=[pl.BlockSpec((T, PG), lambda g: (0, g)),
                  pl.BlockSpec((None, PG, PG), lambda g: (g, 0, 0)),
                  pl.BlockSpec((1, PG), lambda g: (0, g))],
        out_specs=pl.BlockSpec((T, PG), lambda g: (0, g)),
        out_shape=jax.ShapeDtypeStruct((T, G * PG), BF16),
        scratch_shapes=[pltpu.VMEM((T + 2 * HALO, PG), F32)],
        compiler_params=_params(("parallel",)),
    )(P, w_pool, scale)


def _pool_bwd(P, dy, w_pool, scale, name):
    T = P.shape[0]
    G, PG, _ = w_pool.shape

    def body(a_ref, dy_ref, w_ref, s_ref, da_ref, dw_ref, ds_ref, pad_ref, gpad_ref):
        g = pl.program_id(0)
        gpad_ref[pl.ds(0, HALO), :] = jnp.zeros((HALO, PG), F32)
        gpad_ref[pl.ds(HALO + T, HALO), :] = jnp.zeros((HALO, PG), F32)
        _fill_pad(pad_ref, lambda r0: a_ref[pl.ds(r0, CH), :].astype(F32), T)
        for gi, w in enumerate(POOL_WINDOWS):
            @pl.when(g == gi)
            def _(w=w):
                def chunk1(r0, carry):
                    dw, ds = carry
                    pooled = _pooled(pad_ref, r0, w, T).astype(BF16)
                    ypre = jnp.dot(pooled, w_ref[...], preferred_element_type=F32)
                    dyv = dy_ref[pl.ds(r0, CH), :].astype(F32)
                    ds = ds + jnp.sum(dyv * ypre, axis=0, keepdims=True)
                    dyp = (dyv * s_ref[...]).astype(BF16)
                    dw = dw + lax.dot_general(pooled, dyp, _DIMS["tn"], preferred_element_type=F32)
                    dpool = lax.dot_general(dyp, w_ref[...], _DIMS["nt"], preferred_element_type=F32)
                    gpad_ref[pl.ds(HALO + r0, CH), :] = dpool / _pool_cnt(r0, w, T, dpool.shape)
                    return dw, ds

                dw, ds = _chunks(T, chunk1, (jnp.zeros((PG, PG), F32), jnp.zeros((1, PG), F32)))
                dw_ref[...] = dw
                ds_ref[...] = ds

                def chunk2(r0, carry):
                    ext = _ext(gpad_ref, r0)
                    cur = ext[HALO:HALO + CH]
                    acc = _window_sum(ext, w, mirrored=True)
                    da_ref[pl.ds(r0, CH), :] = (acc - cur * _pool_cnt(r0, w, T, cur.shape)).astype(BF16)
                    return carry

                _chunks(T, chunk2)

    col = pl.BlockSpec((T, PG), lambda g: (0, g))
    return pl.pallas_call(
        body, name=name, grid=(G,),
        in_specs=[col, col, pl.BlockSpec((None, PG, PG), lambda g: (g, 0, 0)), pl.BlockSpec((1, PG), lambda g: (0, g))],
        out_specs=[col, pl.BlockSpec((None, PG, PG), lambda g: (g, 0, 0)), pl.BlockSpec((1, PG), lambda g: (0, g))],
        out_shape=[jax.ShapeDtypeStruct((T, G * PG), BF16), jax.ShapeDtypeStruct((G, PG, PG), F32),
                   jax.ShapeDtypeStruct((1, G * PG), F32)],
        scratch_shapes=[pltpu.VMEM((T + 2 * HALO, PG), F32), pltpu.VMEM((T + 2 * HALO, PG), F32)],
        compiler_params=_params(("parallel",)),
    )(P, dy, w_pool, scale)


SGU_CHUNK = 128


def _sgu_common(u_ref, v_ref, gs_ref):
    up = u_ref[...].astype(F32)
    vp = v_ref[...].astype(F32)
    gv = _gelu(vp)
    rv = lax.rsqrt(jnp.mean(gv * gv, axis=-1, keepdims=True) + EPS)
    nrm = gv * rv
    return up, vp, nrm, rv, (nrm * gs_ref[...]).astype(BF16)


def _sgu_fwd(P, sgu_w, b_exp, gs, name):
    T = P.shape[0]
    G, _, SG = b_exp.shape
    DP = G * SG
    tm = _row_tile(T)

    def body(u_ref, v_ref, w_ref, b_ref, gs_ref, o_ref):
        up, _, _, _, vn = _sgu_common(u_ref, v_ref, gs_ref)
        gu = _gelu(up)
        for n in range(tm // SGU_CHUNK):
            rows = slice(n * SGU_CHUNK, (n + 1) * SGU_CHUNK)
            for g in range(G):
                cols = slice(g * SG, (g + 1) * SG)
                z = jnp.dot(w_ref[g], vn[rows, cols], preferred_element_type=F32) + b_ref[g]
                o_ref[rows, cols] = (gu[rows, cols] * z).astype(BF16)

    return pl.pallas_call(
        body, name=name, grid=(T // tm,),
        in_specs=[pl.BlockSpec((tm, DP), lambda i: (i, 1)), pl.BlockSpec((tm, DP), lambda i: (i, 2)),
                  pl.BlockSpec(sgu_w.shape, lambda i: (0, 0, 0)), pl.BlockSpec(b_exp.shape, lambda i: (0, 0, 0)),
                  pl.BlockSpec((1, DP), lambda i: (0, 0))],
        out_specs=pl.BlockSpec((tm, DP), lambda i: (i, 0)),
        out_shape=jax.ShapeDtypeStruct((T, DP), BF16),
        compiler_params=_params(("parallel",)),
    )(P, P, sgu_w, b_exp, gs)


def _sgu_bwd(P, dy, da, sgu_w, b_exp, gs, name):
    T = P.shape[0]
    G, _, SG = b_exp.shape
    DP = G * SG
    tm = _row_tile(T)

    def body(u_ref, v_ref, dy_ref, da_ref, w_ref, b_ref, gs_ref, o_ref, dw_ref, db_ref, dgs_ref, dzs_ref):
        i = pl.program_id(0)
        up, vp, nrm, rv, vn = _sgu_common(u_ref, v_ref, gs_ref)
        gu = _gelu(up)
        dyv = dy_ref[...].astype(F32)
        o_ref[:, 0:DP] = da_ref[...]

        @pl.when(i == 0)
        def _():
            dw_ref[...] = jnp.zeros(dw_ref.shape, F32)
            dzs_ref[...] = jnp.zeros(dzs_ref.shape, F32)
            dgs_ref[...] = jnp.zeros(dgs_ref.shape, F32)

        dgs = jnp.zeros((1, DP), F32)
        for n in range(tm // SGU_CHUNK):
            rows = slice(n * SGU_CHUNK, (n + 1) * SGU_CHUNK)
            dvn_parts = []
            for g in range(G):
                cols = slice(g * SG, (g + 1) * SG)
                vng = vn[rows, cols]
                z = jnp.dot(w_ref[g], vng, preferred_element_type=F32) + b_ref[g]
                dyg = dyv[rows, cols]
                du = dyg * z
                o_ref[rows, DP + g * SG:DP + (g + 1) * SG] = (du * _gelu_grad(up[rows, cols])).astype(BF16)
                dz = dyg * gu[rows, cols]
                dzb = dz.astype(BF16)
                dzs_ref[g] += dz
                dw_ref[g] += lax.dot_general(dzb, vng, _DIMS["nt"], preferred_element_type=F32)
                dvn_parts.append(lax.dot_general(w_ref[g], dzb, _DIMS["tn"], preferred_element_type=F32))
            dvn = jnp.concatenate(dvn_parts, axis=1)
            nr = nrm[rows]
            dgs = dgs + jnp.sum(dvn * nr, axis=0, keepdims=True)
            dn = dvn * gs_ref[...]
            dgv = rv[rows] * (dn - nr * jnp.mean(dn * nr, axis=-1, keepdims=True))
            o_ref[rows, 2 * DP:3 * DP] = (dgv * _gelu_grad(vp[rows])).astype(BF16)
        dgs_ref[...] += dgs

        @pl.when(i == T // tm - 1)
        def _():
            for g in range(G):
                db_ref[g] = jnp.broadcast_to(jnp.sum(dzs_ref[g], axis=1, keepdims=True), (SGU_CHUNK, SG))

    full3 = lambda a: pl.BlockSpec(a.shape, lambda i: (0, 0, 0))
    return pl.pallas_call(
        body, name=name, grid=(T // tm,),
        in_specs=[pl.BlockSpec((tm, DP), lambda i: (i, 1)), pl.BlockSpec((tm, DP), lambda i: (i, 2)),
                  pl.BlockSpec((tm, DP), lambda i: (i, 0)), pl.BlockSpec((tm, DP), lambda i: (i, 0)),
                  full3(sgu_w), full3(b_exp), pl.BlockSpec((1, DP), lambda i: (0, 0))],
        out_specs=[pl.BlockSpec((tm, 3 * DP), lambda i: (i, 0)), full3(sgu_w), full3(b_exp),
                   pl.BlockSpec((1, DP), lambda i: (0, 0))],
        out_shape=[jax.ShapeDtypeStruct((T, 6 * DP), BF16), jax.ShapeDtypeStruct(sgu_w.shape, F32),
                   jax.ShapeDtypeStruct(b_exp.shape, F32), jax.ShapeDtypeStruct((1, DP), F32)],
        scratch_shapes=[pltpu.VMEM(b_exp.shape, F32)],
        compiler_params=_params(("arbitrary",)),
    )(P, P, dy, da, sgu_w, b_exp, gs)


def _conv_fwd(P, cw, name):
    T = P.shape[0]
    DP = cw.shape[1]
    tc = 256
    nb = DP // tc

    def body(xc_ref, bg_ref, cg_ref, w_ref, o_ref, pad_ref):
        _fill_pad(pad_ref, lambda r0: cg_ref[pl.ds(r0, CH), :].astype(F32) * xc_ref[pl.ds(r0, CH), :].astype(F32), T)
        w = w_ref[...]

        def chunk(r0, carry):
            cq = _conv3(_ext(pad_ref, r0), w)
            o_ref[pl.ds(r0, CH), :] = (bg_ref[pl.ds(r0, CH), :].astype(F32) * cq).astype(BF16)
            return carry

        _chunks(T, chunk)

    col = lambda off: pl.BlockSpec((T, tc), lambda j: (0, off * nb + j))
    return pl.pallas_call(
        body, name=name, grid=(nb,),
        in_specs=[col(3), col(4), col(5), pl.BlockSpec((3, tc), lambda j: (0, j))],
        out_specs=pl.BlockSpec((T, tc), lambda j: (0, j)),
        out_shape=jax.ShapeDtypeStruct((T, DP), BF16),
        scratch_shapes=[pltpu.VMEM((T + 2 * HALO, tc), F32)],
        compiler_params=_params(("parallel",)),
    )(P, P, P, cw)


def _conv_bwd(P, dy, cw, dPl, name):
    T = P.shape[0]
    DP = cw.shape[1]
    tc = 256
    nb = DP // tc

    def body(xc_ref, bg_ref, cg_ref, dy_ref, w_ref, prev_ref, o_ref, dw_ref, qpad_ref, dpad_ref):
        del prev_ref
        seg = pl.program_id(1)
        w = w_ref[...]
        rows = lambda ref, r0: ref[pl.ds(r0, CH), :].astype(F32)

        @pl.when(seg == 0)
        def _():
            _fill_pad(qpad_ref, lambda r0: rows(cg_ref, r0) * rows(xc_ref, r0), T)
            _fill_pad(dpad_ref, lambda r0: rows(dy_ref, r0) * rows(bg_ref, r0), T)

        @pl.when(seg == 0)
        def _():
            def chunk(r0, carry):
                dext = _ext(dpad_ref, r0)
                taps = [_tap(dext, 1), _tap(dext, 0), _tap(dext, -1)]
                dq = taps[0] * w[0:1] + taps[1] * w[1:2] + taps[2] * w[2:3]
                o_ref[pl.ds(r0, CH), :] = (dq * rows(cg_ref, r0)).astype(BF16)
                q = qpad_ref[pl.ds(HALO + r0, CH), :]
                return [a + jnp.sum(q * t, axis=0, keepdims=True) for a, t in zip(carry, taps)]

            dws = _chunks(T, chunk, [jnp.zeros((1, tc), F32)] * 3)
            dw_ref[...] = jnp.concatenate(dws + [jnp.zeros((5, tc), F32)], axis=0)

        @pl.when(seg == 1)
        def _():
            def chunk(r0, carry):
                cq = _conv3(_ext(qpad_ref, r0), w)
                o_ref[pl.ds(r0, CH), :] = (rows(dy_ref, r0) * cq).astype(BF16)
                return carry

            _chunks(T, chunk)

        @pl.when(seg == 2)
        def _():
            def chunk(r0, carry):
                dq = _conv3_t(_ext(dpad_ref, r0), w)
                o_ref[pl.ds(r0, CH), :] = (dq * rows(xc_ref, r0)).astype(BF16)
                return carry

            _chunks(T, chunk)

    col = lambda off: pl.BlockSpec((T, tc), lambda j, s: (0, off * nb + j))
    return pl.pallas_call(
        body, name=name, grid=(nb, 3),
        in_specs=[col(3), col(4), col(5), pl.BlockSpec((T, tc), lambda j, s: (0, j)),
                  pl.BlockSpec((3, tc), lambda j, s: (0, j)), pl.BlockSpec(memory_space=pl.ANY)],
        out_specs=[pl.BlockSpec((T, tc), lambda j, s: (0, (3 + s) * nb + j)), pl.BlockSpec((8, tc), lambda j, s: (0, j))],
        out_shape=[jax.ShapeDtypeStruct(dPl.shape, BF16), jax.ShapeDtypeStruct((8, DP), F32)],
        scratch_shapes=[pltpu.VMEM((T + 2 * HALO, tc), F32), pltpu.VMEM((T + 2 * HALO, tc), F32)],
        input_output_aliases={5: 0},
        compiler_params=_params(("parallel", "arbitrary")),
    )(P, P, P, dy, cw, dPl)


def _merge_fwd(P, ys, ws, name):
    T = P.shape[0]
    DP, D = ws[0].shape
    tm = _pick(T, (512, 256, 128))
    tn = _pick(D, (1024, 512, 256))
    goff = (6 * DP) // tn

    def body(ya, yb, yc, wa, wb, wc, ga, gb, gc, oa_ref, ob_ref, oc_ref, m_ref):
        m = None
        for y, w, g, o_ref in ((ya, wa, ga, oa_ref), (yb, wb, gb, ob_ref), (yc, wc, gc, oc_ref)):
            o = jnp.dot(y[...], w[...], preferred_element_type=F32)
            o_ref[...] = o.astype(BF16)
            t = _sigmoid(g[...].astype(F32)) * o
            m = t if m is None else m + t
        m_ref[...] = m.astype(BF16)

    yspec = pl.BlockSpec((tm, DP), lambda i, j: (i, 0))
    wspec = pl.BlockSpec((DP, tn), lambda i, j: (0, j))
    gspec = lambda b: pl.BlockSpec((tm, tn), lambda i, j: (i, goff + b * (D // tn) + j))
    ospec = pl.BlockSpec((tm, tn), lambda i, j: (i, j))
    return pl.pallas_call(
        body, name=name, grid=(T // tm, D // tn),
        in_specs=[yspec] * 3 + [wspec] * 3 + [gspec(0), gspec(1), gspec(2)],
        out_specs=[ospec] * 4,
        out_shape=[jax.ShapeDtypeStruct((T, D), BF16)] * 4,
        compiler_params=_params(("parallel", "parallel")),
    )(*ys, *ws, P, P, P)


def _merge_bwd(P, dM, os_, name):
    T, D = dM.shape
    tm = _row_tile(T)

    def body(dm_ref, oa, ob, oc, g_ref, da_ref, db_ref, dc_ref, dg_ref):
        dm = dm_ref[...].astype(F32)
        for b, (o_ref, d_ref) in enumerate(((oa, da_ref), (ob, db_ref), (oc, dc_ref))):
            s = _sigmoid(g_ref[:, b * D:(b + 1) * D].astype(F32))
            d_ref[...] = (dm * s).astype(BF16)
            dg_ref[:, b * D:(b + 1) * D] = (dm * o_ref[...].astype(F32) * s * (1.0 - s)).astype(BF16)

    row = pl.BlockSpec((tm, D), lambda i: (i, 0))
    return pl.pallas_call(
        body, name=name, grid=(T // tm,),
        in_specs=[row] * 4 + [pl.BlockSpec((tm, 3 * D), lambda i: (i, 1))],
        out_specs=[row] * 3 + [pl.BlockSpec((tm, 3 * D), lambda i: (i, 0))],
        out_shape=[jax.ShapeDtypeStruct((T, D), BF16)] * 3 + [jax.ShapeDtypeStruct((T, 3 * D), BF16)],
        compiler_params=_params(("parallel",)),
    )(dM, *os_, P)


def _glu_fwd(U, fw, name):
    T, F2 = U.shape
    F = F2 // 2
    tc = 256
    nb = F // tc

    def body(ug_ref, uv_ref, wg_ref, wv_ref, o_ref, gpad_ref, vpad_ref):
        _fill_pad(gpad_ref, lambda r0: ug_ref[pl.ds(r0, CH), :].astype(F32), T)
        _fill_pad(vpad_ref, lambda r0: uv_ref[pl.ds(r0, CH), :].astype(F32), T)
        wg, wv = wg_ref[...], wv_ref[...]

        def chunk(r0, carry):
            gate = _conv3(_ext(gpad_ref, r0), wg)
            val = _conv3(_ext(vpad_ref, r0), wv)
            o_ref[pl.ds(r0, CH), :] = (_silu(gate) * val).astype(BF16)
            return carry

        _chunks(T, chunk)

    return pl.pallas_call(
        body, name=name, grid=(nb,),
        in_specs=[pl.BlockSpec((T, tc), lambda j: (0, j)), pl.BlockSpec((T, tc), lambda j: (0, nb + j)),
                  pl.BlockSpec((3, tc), lambda j: (0, j)), pl.BlockSpec((3, tc), lambda j: (0, nb + j))],
        out_specs=pl.BlockSpec((T, tc), lambda j: (0, j)),
        out_shape=jax.ShapeDtypeStruct((T, F), BF16),
        scratch_shapes=[pltpu.VMEM((T + 2 * HALO, tc), F32)] * 2,
        compiler_params=_params(("parallel",)),
    )(U, U, fw, fw)


def _glu_bwd(U, dact, fw, name):
    T, F2 = U.shape
    F = F2 // 2
    tc = 128
    nb = F // tc

    def body(ug_ref, uv_ref, da_ref, wg_ref, wv_ref, dug_ref, duv_ref, dwg_ref, dwv_ref, gpad, vpad, dgpad, dvpad):
        _fill_pad(gpad, lambda r0: ug_ref[pl.ds(r0, CH), :].astype(F32), T)
        _fill_pad(vpad, lambda r0: uv_ref[pl.ds(r0, CH), :].astype(F32), T)
        wg, wv = wg_ref[...], wv_ref[...]

        def chunk1(r0, carry):
            gate = _conv3(_ext(gpad, r0), wg)
            val = _conv3(_ext(vpad, r0), wv)
            s = _sigmoid(gate)
            da = da_ref[pl.ds(r0, CH), :].astype(F32)
            dgpad[pl.ds(HALO + r0, CH), :] = da * val * (s * (1.0 + gate * (1.0 - s)))
            dvpad[pl.ds(HALO + r0, CH), :] = da * (gate * s)
            return carry

        z = jnp.zeros((HALO, tc), F32)
        for p in (dgpad, dvpad):
            p[pl.ds(0, HALO), :] = z
            p[pl.ds(HALO + T, HALO), :] = z
        _chunks(T, chunk1)

        def chunk2(r0, carry):
            new = []
            for pad, dpad, wk, o_ref in ((gpad, dgpad, wg, dug_ref), (vpad, dvpad, wv, duv_ref)):
                dext = _ext(dpad, r0)
                taps = [_tap(dext, 1), _tap(dext, 0), _tap(dext, -1)]
                o_ref[pl.ds(r0, CH), :] = (taps[0] * wk[0:1] + taps[1] * wk[1:2] + taps[2] * wk[2:3]).astype(BF16)
                xs = pad[pl.ds(HALO + r0, CH), :]
                new += [jnp.sum(xs * t, axis=0, keepdims=True) for t in taps]
            return [a + b for a, b in zip(carry, new)]

        dws = _chunks(T, chunk2, [jnp.zeros((1, tc), F32)] * 6)
        dwg_ref[...] = jnp.concatenate(dws[:3] + [jnp.zeros((5, tc), F32)], axis=0)
        dwv_ref[...] = jnp.concatenate(dws[3:] + [jnp.zeros((5, tc), F32)], axis=0)

    lo = pl.BlockSpec((T, tc), lambda j: (0, j))
    hi = pl.BlockSpec((T, tc), lambda j: (0, nb + j))
    wlo = pl.BlockSpec((3, tc), lambda j: (0, j))
    whi = pl.BlockSpec((3, tc), lambda j: (0, nb + j))
    dwspec = pl.BlockSpec((8, tc), lambda j: (0, j))
    return pl.pallas_call(
        body, name=name, grid=(nb,),
        in_specs=[lo, hi, lo, wlo, whi],
        out_specs=[lo, lo, dwspec, dwspec],
        out_shape=[jax.ShapeDtypeStruct((T, F), BF16)] * 2 + [jax.ShapeDtypeStruct((8, F), F32)] * 2,
        scratch_shapes=[pltpu.VMEM((T + 2 * HALO, tc), F32)] * 4,
        compiler_params=_params(("parallel",)),
    )(U, U, dact, fw, fw)


_ANY = pl.BlockSpec(memory_space=pl.ANY)
_MESH = pl.DeviceIdType.MESH


def _mesh_pos():
    return lax.axis_index("x"), lax.axis_index("y"), lax.axis_index("c")


def _other_chips(x, y):
    return [(1 - x, y), (x, 1 - y), (1 - x, 1 - y)]


_HBM = pl.BlockSpec(memory_space=pltpu.HBM)
_SEM = pl.BlockSpec(memory_space=pltpu.SEMAPHORE)
_EFFECT = pltpu.SideEffectType.DATAFLOW_SIDE_EFFECTING


def _in_hbm(a):
    return pltpu.with_memory_space_constraint(a, pltpu.HBM)


def _split_start(groups, name, deps=()):
    flat, where = [], []
    for bufs, _, _ in groups:
        idx = []
        for b in bufs:
            hit = [i for i, f in enumerate(flat) if f is b]
            if not hit:
                flat.append(b)
            idx.append(hit[0] if hit else len(flat) - 1)
        where.append(idx)
    nb = len(flat)
    ng = len(groups)

    def body(*refs):
        ins = refs[:nb]
        sems = refs[nb + len(deps):nb + len(deps) + 2 * ng]
        token = refs[-1]
        for gi, (_, make_copies, _) in enumerate(groups):
            for cp in make_copies([ins[i] for i in where[gi]], sems[2 * gi], sems[2 * gi + 1]):
                cp.start()
        token[...] = jnp.zeros(token.shape, token.dtype)

    sem_shapes = []
    for _, _, n in groups:
        sem_shapes += [pltpu.SemaphoreType.DMA((n,)), pltpu.SemaphoreType.DMA((n,))]
    outs = pl.pallas_call(
        body, name=name,
        in_specs=[_HBM] * nb + [_ANY] * len(deps),
        out_specs=[_SEM] * (2 * ng) + [_HBM] * nb + [pl.BlockSpec(memory_space=pltpu.VMEM)],
        out_shape=sem_shapes + [pltpu.HBM(b.shape, b.dtype) for b in flat] + [jax.ShapeDtypeStruct((8, LANE), F32)],
        input_output_aliases={i: 2 * ng + i for i in range(nb)},
        compiler_params=pltpu.CompilerParams(has_side_effects=_EFFECT),
    )(*[_in_hbm(b) for b in flat], *deps)
    res = [(outs[2 * gi], outs[2 * gi + 1], [outs[2 * ng + i] for i in where[gi]]) for gi in range(ng)]
    return res, outs[-1]


def _split_wait(started, make_copies, after, name):
    send, recv, bufs = started
    nb = len(bufs)
    after = list(after) if isinstance(after, (list, tuple)) else [after]

    def body(*refs):
        for cp in make_copies(refs[:nb], refs[nb], refs[nb + 1]):
            cp.wait_send()
            cp.wait_recv()

    outs = pl.pallas_call(
        body, name=name,
        in_specs=[_HBM] * nb + [_SEM, _SEM] + [_ANY] * len(after),
        out_specs=[_HBM] * nb,
        out_shape=[pltpu.HBM(b.shape, b.dtype) for b in bufs],
        input_output_aliases={i: i for i in range(nb)},
        compiler_params=pltpu.CompilerParams(has_side_effects=_EFFECT),
    )(*bufs, send, recv, *after)
    return list(outs)


def _slot(buf, s, cols=None):
    if len(buf.shape) == 3:
        return buf.at[s] if cols is None else buf.at[s, :, pl.ds(cols[0], cols[1])]
    ns = buf.shape[1] // N_DEV
    first, width = (0, ns) if cols is None else cols
    return buf.at[:, pl.ds(pl.multiple_of(s * ns, LANE) + first, width)]


def _copies_gather_chips(nbuf, cols=None):
    def make(bufs, send, recv):
        x, y, c = _mesh_pos()
        targets = [(x, y, 1 - c)] + [(px, py, c) for px, py in _other_chips(x, y)]
        cps = []
        for a in range(nbuf):
            mine = _slot(bufs[a], 4 * x + 2 * y + c, cols)
            for k, to in enumerate(targets):
                cps.append(pltpu.make_async_remote_copy(src_ref=mine, dst_ref=mine, send_sem=send.at[4 * a + k],
                                                        recv_sem=recv.at[4 * a + k], device_id=to, device_id_type=_MESH))
        return cps

    return make, 4 * nbuf


def _copies_gather_forward(nbuf, cols=None):
    def make(bufs, send, recv):
        x, y, c = _mesh_pos()
        cps = []
        for a in range(nbuf):
            for j, (px, py) in enumerate(_other_chips(x, y)):
                slot = _slot(bufs[a], 4 * px + 2 * py + c, cols)
                cps.append(pltpu.make_async_remote_copy(src_ref=slot, dst_ref=slot, send_sem=send.at[3 * a + j],
                                                        recv_sem=recv.at[3 * a + j], device_id=(x, y, 1 - c),
                                                        device_id_type=_MESH))
        return cps

    return make, 3 * nbuf


def _copies_rs_sibling(n):
    def make(bufs, send, recv):
        x, y, c = _mesh_pos()
        cps = []
        for a in range(n):
            for q in range(4):
                cps.append(pltpu.make_async_remote_copy(
                    src_ref=_slot(bufs[a], 2 * q + (1 - c)), dst_ref=bufs[n + a].at[q], send_sem=send.at[4 * a + q],
                    recv_sem=recv.at[4 * a + q], device_id=(x, y, 1 - c), device_id_type=_MESH))
        return cps

    return make, 4 * n


def _copies_rs_chips(n):
    def make(bufs, send, recv):
        x, y, c = _mesh_pos()
        cps = []
        for a in range(n):
            for j, (px, py) in enumerate(_other_chips(x, y)):
                cps.append(pltpu.make_async_remote_copy(
                    src_ref=bufs[a].at[j], dst_ref=bufs[n + a].at[j], send_sem=send.at[3 * a + j],
                    recv_sem=recv.at[3 * a + j], device_id=(px, py, c), device_id_type=_MESH))
        return cps

    return make, 3 * n


def _cast_place(stacked, layer, slot, dtype, name, deps=(), by_cols=False):
    _, R, C = stacked.shape
    tr = _rows_tile(R, C)

    def body(slot_ref, x_ref, *rest):
        del slot_ref
        rest[-1][...] = x_ref[...].astype(dtype)

    if by_cols:
        out_spec = pl.BlockSpec((tr, C), lambda i, s: (i, s[0]))
        out_shape = jax.ShapeDtypeStruct((R, N_DEV * C), dtype)
    else:
        out_spec = pl.BlockSpec((None, tr, C), lambda i, s: (s[0], i, 0))
        out_shape = jax.ShapeDtypeStruct((N_DEV, R, C), dtype)
    return pl.pallas_call(
        body, name=name,
        grid_spec=pltpu.PrefetchScalarGridSpec(
            num_scalar_prefetch=1, grid=(R // tr,),
            in_specs=[pl.BlockSpec((None, tr, C), lambda i, s: (layer, i, 0))] + [_ANY] * len(deps),
            out_specs=out_spec),
        out_shape=out_shape,
        compiler_params=_params(("parallel",)),
    )(slot, stacked, *deps)


def _rows_tile(R, C, budget=2 * MIB):
    if R * C * 4 <= budget:
        return R
    for t in (2048, 1024, 704, 512, 352, 256, 128, 64, 32, 16, 8):
        if R % t == 0 and t * C * 4 <= budget:
            return t
    return R


def _chip_of_relation(j, pos):
    x, y = pos[0], pos[1]
    return 2 * jnp.where(j == 1, x, 1 - x) + jnp.where(j == 0, y, 1 - y)


def _pair_sum(grad, recv, pos, name):
    _, R, C = recv.shape
    tr = _rows_tile(R, C, 8 * MIB)

    def body(pos_ref, g_ref, r_ref, o_ref):
        del pos_ref
        o_ref[...] = (g_ref[...].astype(F32) + r_ref[...].astype(F32)).astype(BF16)

    if len(grad.shape) == 3:
        g_spec = pl.BlockSpec((None, tr, C), lambda j, i, p: (2 * _chip_of_relation(j, p) + p[2], i, 0))
    else:
        g_spec = pl.BlockSpec((tr, C), lambda j, i, p: (i, 2 * _chip_of_relation(j, p) + p[2]))
    return pl.pallas_call(
        body, name=name,
        grid_spec=pltpu.PrefetchScalarGridSpec(
            num_scalar_prefetch=1, grid=(3, R // tr),
            in_specs=[g_spec,
                      pl.BlockSpec((None, tr, C), lambda j, i, p: (_chip_of_relation(j, p), i, 0))],
            out_specs=pl.BlockSpec((None, tr, C), lambda j, i, p: (j, i, 0))),
        out_shape=jax.ShapeDtypeStruct((3, R, C), BF16),
        compiler_params=_params(("parallel", "parallel")),
    )(pos, grad, recv)


ADAM_LR = 0.001
ADAM_B1 = 0.9
ADAM_B2 = 0.999
ADAM_EPS = 1e-08
ADAM_WD = 0.01
ADAM_STEP = 10


def _adamw(w, g, m, v):
    m = ADAM_B1 * m + (1.0 - ADAM_B1) * g
    v = ADAM_B2 * v + (1.0 - ADAM_B2) * (g * g)
    m_hat = m / (1.0 - ADAM_B1 ** ADAM_STEP)
    v_hat = v / (1.0 - ADAM_B2 ** ADAM_STEP)
    delta = -ADAM_LR * (m_hat / (jnp.sqrt(v_hat) + ADAM_EPS) + ADAM_WD * w)
    return delta, m, v


def _adamw_big(grad, sib, recv, w, m, v, pos, layer, prev, name):
    L, R, C = w.shape
    tr = _rows_tile(R, C)
    has_prev = prev is not None

    def body(pos_ref, g_ref, s_ref, r0_ref, r1_ref, r2_ref, w_ref, m_ref, v_ref, *rest):
        del pos_ref
        go_ref, d_ref, mo_ref, vo_ref = rest[-4:]
        g = g_ref[...].astype(F32) + s_ref[...].astype(F32)
        g = ((g + r0_ref[...].astype(F32)) + r1_ref[...].astype(F32)) + r2_ref[...].astype(F32)
        delta, m2, v2 = _adamw(w_ref[...], g, m_ref[...], v_ref[...])
        go_ref[...] = g
        d_ref[...] = delta
        mo_ref[...] = m2
        vo_ref[...] = v2

    lay = pl.BlockSpec((None, tr, C), lambda i, p: (layer, i, 0))
    rspec = lambda j: pl.BlockSpec((None, tr, C), lambda i, p: (j, i, 0))
    mine = lambda p: 4 * p[0] + 2 * p[1] + p[2]
    if len(grad.shape) == 3:
        g_spec = pl.BlockSpec((None, tr, C), lambda i, p: (mine(p), i, 0))
    else:
        g_spec = pl.BlockSpec((tr, C), lambda i, p: (i, mine(p)))
    s_spec = pl.BlockSpec((None, tr, C), lambda i, p: (2 * p[0] + p[1], i, 0))
    in_specs = [g_spec, s_spec, rspec(0), rspec(1), rspec(2), lay, lay, lay]
    args = [pos, grad, sib, recv, recv, recv, w, m, v]
    aliases = {}
    if has_prev:
        in_specs += [_ANY] * 4
        args += list(prev)
        aliases = {9 + k: k for k in range(4)}
    return pl.pallas_call(
        body, name=name,
        grid_spec=pltpu.PrefetchScalarGridSpec(
            num_scalar_prefetch=1, grid=(R // tr,), in_specs=in_specs, out_specs=[lay] * 4),
        out_shape=[jax.ShapeDtypeStruct((L, R, C), F32)] * 4,
        input_output_aliases=aliases,
        compiler_params=_params(("parallel",)),
    )(*args)


def _sum_slots(gathered, name):
    _, R, C = gathered.shape
    tr = _rows_tile(R, C, 2 * MIB)

    def body(x_ref, o_ref):
        s = x_ref[0]
        for k in range(1, N_DEV):
            s = s + x_ref[k]
        o_ref[...] = s

    return pl.pallas_call(
        body, name=name, grid=(R // tr,),
        in_specs=[pl.BlockSpec((N_DEV, tr, C), lambda i: (0, i, 0))],
        out_specs=pl.BlockSpec((tr, C), lambda i: (i, 0)),
        out_shape=jax.ShapeDtypeStruct((R, C), F32),
        compiler_params=_params(("parallel",)),
    )(gathered)


def _adamw_small(w, g, m, v, name):
    R, C = w.shape
    tr = _rows_tile(R, C, 2 * MIB)

    def body(w_ref, g_ref, m_ref, v_ref, d_ref, mo_ref, vo_ref):
        d_ref[...], mo_ref[...], vo_ref[...] = _adamw(w_ref[...], g_ref[...], m_ref[...], v_ref[...])

    spec = pl.BlockSpec((tr, C), lambda i: (i, 0))
    return pl.pallas_call(
        body, name=name, grid=(R // tr,),
        in_specs=[spec] * 4, out_specs=[spec] * 3,
        out_shape=[jax.ShapeDtypeStruct((R, C), F32)] * 3,
        compiler_params=_params(("parallel",)),
    )(w, g, m, v)


def _pack(arrs):
    parts = []
    for a in arrs:
        flat = a.reshape(-1)
        pad = (-flat.shape[0]) % (8 * LANE)
        if pad:
            flat = jnp.pad(flat, (0, pad))
        parts.append(flat.reshape(-1, LANE))
    return jnp.concatenate(parts, axis=0)


def _unpack(packed, shapes):
    out, r = [], 0
    for s in shapes:
        n = 1
        for d in s:
            n *= d
        rows = -(-n // (8 * LANE)) * 8
        out.append(packed[r:r + rows].reshape(-1)[:n].reshape(s))
        r += rows
    return out


def kernel(x, norm1_g, w_in, w_pool, pool_scale, sgu_norm_g, sgu_w, sgu_b, conv_w, w_pool_out, w_sgu_out, w_conv_out, w_o, norm2_g, w_up, ffn_conv_w, w_down, final_g, loss_target, m_norm1_g, m_w_in, m_w_pool, m_pool_scale, m_sgu_norm_g, m_sgu_w, m_sgu_b, m_conv_w, m_w_pool_out, m_w_sgu_out, m_w_conv_out, m_w_o, m_norm2_g, m_w_up, m_ffn_conv_w, m_w_down, m_final_g, v_norm1_g, v_w_in, v_w_pool, v_pool_scale, v_sgu_norm_g, v_sgu_w, v_sgu_b, v_conv_w, v_w_pool_out, v_w_sgu_out, v_w_conv_out, v_w_o, v_norm2_g, v_w_up, v_ffn_conv_w, v_w_down, v_final_g):
    L = norm1_g.shape[0]
    x0 = x[0]
    tgt = loss_target[0]
    G, PG = w_pool.shape[1], w_pool.shape[3]
    F = w_down.shape[1] * N_DEV
    T, D = x0.shape
    tT = _pick(T, (1024, 512, 256, 128))
    tT2 = _pick(T, (2048, tT))
    tF = _pick(F, (512, 256))
    tF2 = _pick(2 * F, (1024, 512, 256))
    kF = _pick(F, (2816, 2048, 1024, 512))
    px, py, pc = _mesh_pos()
    me = 4 * px + 2 * py + pc
    pos = jnp.stack([px, py, pc]).astype(jnp.int32)

    slot = jnp.reshape(me, (1,)).astype(jnp.int32)
    ns_in = w_in.shape[2]
    nq = 3 if ns_in % (3 * 256) == 0 else 1
    chunk_cols = [(q * (ns_in // nq), ns_in // nq) for q in range(nq)]
    win0 = _cast_place(w_in, 0, slot, BF16, "place_w_in_0", by_cols=True)
    first, tok0 = _split_start([([win0],) + _copies_gather_chips(1, chunk_cols[0])], "ag_start_first")
    win0 = first[0][2][0]
    h1_first = _rms_fwd(x0, norm1_g[0][None], "rms1_fwd_0")
    stage1, group_sizes, started1 = {}, {0: 1}, {}

    def place_groups(l, deps):
        grp = [[("w_in", w_in, BF16, True)],
               [("w_pool_out", w_pool_out, BF16, True), ("w_sgu_out", w_sgu_out, BF16, True),
                ("w_conv_out", w_conv_out, BF16, True), ("w_pool", w_pool.reshape(L, -1, PG), BF16, False),
                ("conv_w", conv_w, F32, True), ("w_o", w_o, BF16, False)],
               [("w_up", w_up, BF16, True), ("ffn_conv_w", ffn_conv_w, F32, True)],
               [("w_down", w_down, BF16, False)]]
        ks = []
        for gi, members in enumerate(grp):
            if l == 0 and gi == 0:
                continue
            k = 4 * l + gi
            bufs = [_cast_place(a, l, slot, dt, f"place_{n}_{l}", deps, bc) for n, a, dt, bc in members]
            stage1[k] = (bufs,) + _copies_gather_chips(len(members))
            group_sizes[k] = len(members)
            ks.append(k)
        return ks

    def start_groups(ks, extra, name, deps=()):
        st, tok = _split_start(extra + [stage1[k] for k in ks], name, deps)
        for k, s in zip(ks, st[len(extra):]):
            started1[k] = s
        return st[:len(extra)], tok

    rest, tok_start0 = start_groups(place_groups(0, [tok0, h1_first]),
                                    [([win0],) + _copies_gather_chips(1, cols) for cols in chunk_cols[1:]], "ag_start_0")
    if nq > 1:
        win0 = rest[0][2][0]
    chunk_sems = [first[0][:2]] + [r[:2] for r in rest]
    later_groups = [place_groups(l, [tok_start0]) for l in range(1, L)]
    placed_later = [b for ks in later_groups for k in ks for b in stage1[k][0]]
    started2 = {}

    def gather_forward(k, after):
        n = group_sizes[k]
        bufs = _split_wait(started1[k], stage1[k][1], after, f"ag_wait1_{k}")
        st, tok = _split_start([(bufs,) + _copies_gather_forward(n)], f"ag_fwd_{k}")
        started2[k] = st[0]
        return tok

    def gather_done(k, after):
        return _split_wait(started2.pop(k), _copies_gather_forward(group_sizes[k])[0], after, f"ag_wait2_{k}")

    W, saved = [], []
    xc = x0
    for l in range(L):
        w = dict(
            sguw=sgu_w[l].astype(BF16),
            bexp=jnp.broadcast_to(sgu_b[l].T[:, :, None],
                                  (sgu_b.shape[2], sgu_b.shape[1], sgu_norm_g.shape[1] // sgu_b.shape[2])),
            g1=norm1_g[l][None], g2=norm2_g[l][None], gs=sgu_norm_g[l][None], scale=pool_scale[l][None])
        ka, kb, kc, kd = 4 * l, 4 * l + 1, 4 * l + 2, 4 * l + 3
        h1 = h1_first if l == 0 else _rms_fwd(xc, w["g1"], f"rms1_fwd_{l}")
        if l == 0:
            P, after = None, [h1] + placed_later
            for q, cols in enumerate(chunk_cols):
                (win0,) = _split_wait(chunk_sems[q] + ([win0],), _copies_gather_chips(1, cols)[0], after,
                                      f"ag_wait1_0_{q}")
                st, tok = _split_start([([win0],) + _copies_gather_forward(1, cols)], f"ag_fwd_0_{q}")
                (win0,) = _split_wait(st[0], _copies_gather_forward(1, cols)[0], after, f"ag_wait2_0_{q}")
                P = _mm(h1, win0, mode="nn", out_dtype=BF16, name=f"proj_in_0_{q}", tm=tT2, tn=cols[1], tk=D,
                        cols=(q, nq), prev=P, deps=[tok])
                after = P
            w["win"] = win0
        else:
            (w["win"],) = gather_done(ka, h1)
            tok = gather_forward(kb, h1)
            P = _mm(h1, w["win"], mode="nn", out_dtype=BF16, name=f"proj_in_{l}", tm=tT2, tn=1024, tk=D, deps=[tok])
        if l == 0:
            gather_forward(kb, P)
        w["wa"], w["wb"], w["wc"], wp, w["cw"], wo = gather_done(kb, P)
        w.update(wpool=wp.reshape(N_DEV, G, -1, PG).transpose(1, 0, 2, 3).reshape(G, PG, PG),
                 wo=wo.reshape(-1, wo.shape[2]))
        ya = _pool_fwd(P, w["wpool"], w["scale"], f"pool_fwd_{l}")
        yb = _sgu_fwd(P, w["sguw"], w["bexp"], w["gs"], f"sgu_fwd_{l}")
        yc = _conv_fwd(P, w["cw"], f"conv_fwd_{l}")
        oa, ob, oc, M = _merge_fwd(P, (ya, yb, yc), (w["wa"], w["wb"], w["wc"]), f"merge_fwd_{l}")
        tok = gather_forward(kc, M)
        if l == 0:
            for l2 in range(1, L):
                start_groups(later_groups[l2 - 1], [], f"ag_start_{l2}", [tok])
        x1, h2 = _mm_res_rms(M, w["wo"], xc, w["g2"], f"proj_o_{l}", deps=[tok])
        w["wup"], w["fw"] = gather_done(kc, h2)
        tok = gather_forward(kd, h2)
        U = _mm(h2, w["wup"], mode="nn", out_dtype=BF16, name=f"proj_up_{l}", tm=tT2, tn=tF2, tk=D, deps=[tok])
        act = _glu_fwd(U, w["fw"], f"glu_fwd_{l}")
        (wdown,) = gather_done(kd, act)
        w["wdown"] = wdown.reshape(-1, wdown.shape[2])
        deps = [gather_forward(4 * (l + 1), act)] if l + 1 < L else []
        x2 = _mm(act, w["wdown"], mode="nn", out_dtype=F32, name=f"proj_down_{l}", tm=tT, tn=512, tk=F, res=x1,
                 deps=deps)
        if deps:
            tok = deps[0]
        W.append(w)
        saved.append(dict(x0=xc, h1=h1, P=P, ya=ya, yb=yb, yc=yc, oa=oa, ob=ob, oc=oc, M=M, x1=x1, h2=h2, U=U, act=act))
        xc = x2

    dx, dxb, d_final_g, loss_part = _loss_head(xc, tgt, final_g[None], "loss_head")
    loss = lax.psum(loss_part[0, 0], ("x", "y", "c"))

    wmv = dict(w_in=(w_in, m_w_in, v_w_in), w_up=(w_up, m_w_up, v_w_up), w_o=(w_o, m_w_o, v_w_o),
               w_down=(w_down, m_w_down, v_w_down), w_pool_out=(w_pool_out, m_w_pool_out, v_w_pool_out),
               w_sgu_out=(w_sgu_out, m_w_sgu_out, v_w_sgu_out), w_conv_out=(w_conv_out, m_w_conv_out, v_w_conv_out),
               w_pool=(w_pool, m_w_pool, v_w_pool))
    adam_out = {}

    def rs_begin(tag, layer, names, grads):
        n = len(grads)
        slice_shape = lambda g: g.shape[1:] if len(g.shape) == 3 else (g.shape[0], g.shape[1] // N_DEV)
        lands = [lax.empty((4,) + slice_shape(g), BF16) for g in grads]
        st, tok = _split_start([(list(grads) + lands,) + _copies_rs_sibling(n)], f"rs_sib_{tag}")
        return dict(tag=tag, layer=layer, names=names, n=n, st=st[0]), tok

    def rs_to_chips(state, after):
        n, tag = state["n"], state["tag"]
        bufs = _split_wait(state["st"], _copies_rs_sibling(n)[0], after, f"rs_sibw_{tag}")
        grads, sib = bufs[:n], bufs[n:]
        pairs = [_pair_sum(g, r, pos, f"pair_{tag}_{i}") for i, (g, r) in enumerate(zip(grads, sib))]
        lands = [lax.empty(p.shape, BF16) for p in pairs]
        st, tok = _split_start([(pairs + lands,) + _copies_rs_chips(n)], f"rs_chips_{tag}")
        state.update(st=st[0], grads=grads, sib=sib)
        return tok

    def rs_finish(state, after):
        n, tag, layer = state["n"], state["tag"], state["layer"]
        bufs = _split_wait(state["st"], _copies_rs_chips(n)[0], after, f"rs_chipsw_{tag}")
        for name, g, sb, recv in zip(state["names"], state["grads"], state["sib"], bufs[n:]):
            wk, mk, vk = (a.reshape(L, -1, a.shape[-1]) for a in wmv[name])
            adam_out[name] = _adamw_big(g, sb, recv, wk, mk, vk, pos, layer, adam_out.get(name),
                                        f"adamw_{name}_{layer}")

    small = [None] * L
    pending = []
    deps = []
    for l in reversed(range(L)):
        w, s = W[l], saved[l]
        dact = _mm(dxb, w["wdown"], mode="nt", out_dtype=BF16, name=f"d_act_{l}", tm=tT2, tn=tF, tk=D, deps=deps)
        g_down = _mm(s["act"], dxb, mode="tn", out_dtype=BF16, name=f"g_down_{l}", tm=tF, tn=1024, tk=T)
        dug, duv, dfwg, dfwv = _glu_bwd(s["U"], dact, w["fw"], f"glu_bwd_{l}")
        dh2 = _mm(dug, w["wup"], mode="nt", out_dtype=F32, name=f"d_h2_{l}", tm=tT, tn=1024, tk=kF, a2=duv)
        g_up = _mm(s["h2"], dug, mode="tn", out_dtype=BF16, name=f"g_upa_{l}", tm=1024, tn=tF, tk=T, out_n=2 * F)
        g_up = _mm(s["h2"], duv, mode="tn", out_dtype=BF16, name=f"g_upb_{l}", tm=1024, tn=tF, tk=T, out_n=2 * F,
                   out_off=F // tF, prev=g_up)
        ra, tok = rs_begin(f"a{l}", l, ["w_down", "w_up"], [g_down.reshape(N_DEV, -1, g_down.shape[1]), g_up])
        if pending:
            rs_finish(pending.pop(0), g_up)
        dx1, dx1b, dg2 = _rms_bwd(dh2, s["x1"], w["g2"], dx, f"rms2_bwd_{l}")
        dM = _mm(dx1b, w["wo"], mode="nt", out_dtype=BF16, name=f"d_m_{l}", tm=tT, tn=1024, tk=D, deps=[tok])
        g_o = _mm(s["M"], dx1b, mode="tn", out_dtype=BF16, name=f"g_o_{l}", tm=1024, tn=1024, tk=T)
        doa, dob, doc, dPg = _merge_bwd(s["P"], dM, (s["oa"], s["ob"], s["oc"]), f"merge_bwd_{l}")
        tok = rs_to_chips(ra, dPg)
        if pending:
            rs_finish(pending.pop(0), dPg)
        dya = _mm(doa, w["wa"], mode="nt", out_dtype=BF16, name=f"d_ya_{l}", tm=tT, tn=1024, tk=D, deps=[tok])
        dyb = _mm(dob, w["wb"], mode="nt", out_dtype=BF16, name=f"d_yb_{l}", tm=tT, tn=1024, tk=D)
        dyc = _mm(doc, w["wc"], mode="nt", out_dtype=BF16, name=f"d_yc_{l}", tm=tT, tn=1024, tk=D)
        g_a = _mm(s["ya"], doa, mode="tn", out_dtype=BF16, name=f"g_a_{l}", tm=1024, tn=1024, tk=T)
        g_b = _mm(s["yb"], dob, mode="tn", out_dtype=BF16, name=f"g_b_{l}", tm=1024, tn=1024, tk=T)
        g_c = _mm(s["yc"], doc, mode="tn", out_dtype=BF16, name=f"g_c_{l}", tm=1024, tn=1024, tk=T)
        da, g_pool, dscale = _pool_bwd(s["P"], dya, w["wpool"], w["scale"], f"pool_bwd_{l}")
        g_pool_s = g_pool.reshape(G, N_DEV, -1, PG).transpose(1, 0, 2, 3).reshape(N_DEV, -1, PG).astype(BF16)
        rb, tok = rs_begin(f"b{l}", l, ["w_o", "w_pool_out", "w_sgu_out", "w_conv_out", "w_pool"],
                           [g_o.reshape(N_DEV, -1, g_o.shape[1]), g_a, g_b, g_c, g_pool_s])
        if pending:
            rs_finish(pending.pop(0), da)
        dPl, g_sguw, db_exp, dgs = _sgu_bwd(s["P"], dyb, da, w["sguw"], w["bexp"], w["gs"], f"sgu_bwd_{l}")
        dPl, dcw = _conv_bwd(s["P"], dyc, w["cw"], dPl, f"conv_bwd_{l}")
        tok2 = rs_to_chips(rb, dPl)
        half = dPl.shape[1]
        g_in = _mm(s["h1"], dPl, mode="tn", out_dtype=BF16, name=f"g_ina_{l}", tm=1024, tn=1024, tk=T, out_n=2 * half,
                   deps=[tok, tok2])
        g_in = _mm(s["h1"], dPg, mode="tn", out_dtype=BF16, name=f"g_inb_{l}", tm=1024, tn=1024, tk=T, out_n=2 * half,
                   out_off=half // 1024, prev=g_in)
        rc, tok = rs_begin(f"c{l}", l, ["w_in"], [g_in])
        if l > 0:
            dh1 = _mm(dPl, w["win"], mode="nt", out_dtype=F32, name=f"d_h1_{l}", tm=tT, tn=1024, tk=D, a2=dPg,
                      deps=[tok])
            tok = rs_to_chips(rc, dh1)
        else:
            dh1 = _mm(dPl, w["win"], mode="nt", out_dtype=F32, name=f"d_h1a_{l}", tm=tT, tn=1024, tk=D, deps=[tok])
            tok = rs_to_chips(rc, dh1)
            dh1 = _mm(dPg, w["win"], mode="nt", out_dtype=F32, name=f"d_h1b_{l}", tm=tT, tn=1024, tk=D,
                      b_koff=half // D, res=dh1, deps=[tok])
        dx, dxb, dg1 = _rms_bwd(dh1, s["x0"], w["g1"], dx1, f"rms1_bwd_{l}", deps=[tok])
        deps = []
        pending = [ra, rb, rc]
        small[l] = dict(norm1_g=dg1[0], pool_scale=dscale[0], sgu_norm_g=dgs[0], sgu_w=g_sguw,
                        sgu_b=db_exp[:, :, 0].T, conv_w=dcw[:3], norm2_g=dg2[0],
                        ffn_conv_w=jnp.concatenate([dfwg[:3], dfwv[:3]], axis=1))
    grad_x = dx[None]

    snames = ["norm1_g", "pool_scale", "sgu_norm_g", "sgu_w", "sgu_b", "conv_w", "norm2_g", "ffn_conv_w"]
    sparts = [jnp.stack([small[l][n] for l in range(L)]) for n in snames] + [d_final_g[0]]
    snames = snames + ["final_g"]
    packed = _pack(sparts)
    sbuf = _cast_place(packed[None], 0, slot, F32, "place_small")
    st, tok = _split_start([([sbuf],) + _copies_gather_chips(1)], "ag_small_start")
    rs_finish(pending[0], tok)
    rs_finish(pending[1], tok)
    bufs = _split_wait(st[0], _copies_gather_chips(1)[0], [o[0] for o in adam_out.values()], "ag_small_wait1")
    st, tok = _split_start([(bufs,) + _copies_gather_forward(1)], "ag_small_fwd")
    (gathered_small,) = _split_wait(st[0], _copies_gather_forward(1)[0], tok, "ag_small_wait2")
    total = _sum_slots(gathered_small, "sum_small")
    sgrads = dict(zip(snames, _unpack(total, [p.shape for p in sparts])))
    for n, width in (("conv_w", conv_w.shape[2]), ("ffn_conv_w", ffn_conv_w.shape[2])):
        sgrads[n] = lax.dynamic_slice_in_dim(sgrads[n], me * width, width, axis=2)
    sw = dict(norm1_g=(norm1_g, m_norm1_g, v_norm1_g), pool_scale=(pool_scale, m_pool_scale, v_pool_scale),
              sgu_norm_g=(sgu_norm_g, m_sgu_norm_g, v_sgu_norm_g), sgu_w=(sgu_w, m_sgu_w, v_sgu_w),
              sgu_b=(sgu_b, m_sgu_b, v_sgu_b), conv_w=(conv_w, m_conv_w, v_conv_w),
              norm2_g=(norm2_g, m_norm2_g, v_norm2_g), ffn_conv_w=(ffn_conv_w, m_ffn_conv_w, v_ffn_conv_w),
              final_g=(final_g, m_final_g, v_final_g))
    shapes = [sw[n][0].shape for n in snames]
    upd = _adamw_small(_pack([sw[n][0] for n in snames]), _pack([sgrads[n] for n in snames]),
                       _pack([sw[n][1] for n in snames]), _pack([sw[n][2] for n in snames]), "adamw_small")
    sdelta, sm, sv = (dict(zip(snames, _unpack(u, shapes))) for u in upd)
    res = {n: [sgrads[n], sdelta[n], sm[n], sv[n]] for n in snames}

    rs_finish(pending[2], [upd[0]] + [o[0] for o in adam_out.values()])
    for n, outs in adam_out.items():
        res[n] = [o.reshape(wmv[n][0].shape) for o in outs]

    order = ["norm1_g", "w_in", "w_pool", "pool_scale", "sgu_norm_g", "sgu_w", "sgu_b", "conv_w", "w_pool_out",
             "w_sgu_out", "w_conv_out", "w_o", "norm2_g", "w_up", "ffn_conv_w", "w_down", "final_g"]
    return (loss, grad_x) + tuple(res[n][k] for k in range(4) for n in order)
```

```python
import functools

import jax
import jax.numpy as jnp
from jax import lax
from jax.experimental import pallas as pl
from jax.experimental.pallas import tpu as pltpu

BF16 = jnp.bfloat16
F32 = jnp.float32
EPS = 1e-6
MIB = 1024 * 1024
V7X_VMEM_BYTES = 64 * MIB
VMEM_LIMIT = 56 * MIB
LANE = 128
N_DEV = 8


def _params(sem, **kw):
    return pltpu.CompilerParams(dimension_semantics=sem, vmem_limit_bytes=VMEM_LIMIT, **kw)


def _pick(n, cands):
    for c in cands:
        if n % c == 0:
            return c
    return n


_DIMS = {"nn": (((1,), (0,)), ((), ())), "nt": (((1,), (1,)), ((), ())), "tn": (((0,), (0,)), ((), ()))}


def _mm(a, b, *, mode, out_dtype, name, tm, tn, tk, res=None, prev=None, deps=(), cols=(0, 1), b_koff=0,
        out_off=0, out_n=None, a2=None):
    if mode == "nn":
        M, Kc = a.shape
        nj = (b.shape[1] // tn - cols[0] + cols[1] - 1) // cols[1]
        out_w = b.shape[1]
    elif mode == "nt":
        M, Kc = a.shape
        nj = b.shape[0] // tn
        out_w = b.shape[0]
    else:
        Kc, M = a.shape
        nj = b.shape[1] // tn
        out_w = out_n or b.shape[1]
    nk1 = Kc // tk
    nk = nk1 + (a2.shape[1] // tk if a2 is not None else 0)
    assert M % tm == 0 and Kc % tk == 0 and (a2 is None or mode == "nt"), (name, M, Kc, tm, tk)
    off, stride = cols
    if mode == "nn":
        a_spec = pl.BlockSpec((tm, tk), lambda i, j, k: (i, k))
        b_spec = pl.BlockSpec((tk, tn), lambda i, j, k: (k, off + j * stride))
        o_spec = pl.BlockSpec((tm, tn), lambda i, j, k: (i, off + j * stride))
    elif mode == "nt":
        a_spec = pl.BlockSpec((tm, tk), lambda i, j, k: (i, jnp.minimum(k, nk1 - 1)))
        a2_spec = pl.BlockSpec((tm, tk), lambda i, j, k: (i, jnp.maximum(k - nk1, 0)))
        b_spec = pl.BlockSpec((tn, tk), lambda i, j, k: (j, b_koff + k))
        o_spec = pl.BlockSpec((tm, tn), lambda i, j, k: (i, j))
    else:
        a_spec = pl.BlockSpec((tk, tm), lambda i, j, k: (k, i))
        b_spec = pl.BlockSpec((tk, tn), lambda i, j, k: (k, j))
        o_spec = pl.BlockSpec((tm, tn), lambda i, j, k: (i, out_off + j))
    dims = _DIMS[mode]
    has_res = res is not None
    has_prev = prev is not None
    n_in = 2 + has_res + has_prev + len(deps) + (a2 is not None)

    def body(*refs):
        a_ref, b_ref = refs[0], refs[1]
        o_ref = refs[n_in]

        def finish(acc):
            if has_res:
                acc = acc + refs[2][...]
            o_ref[...] = acc.astype(o_ref.dtype)

        if nk == 1:
            finish(lax.dot_general(a_ref[...], b_ref[...], dims, preferred_element_type=F32))
        else:
            acc_ref = refs[n_in + 1]
            k = pl.program_id(2)

            @pl.when(k == 0)
            def _():
                acc_ref[...] = jnp.zeros(acc_ref.shape, F32)

            if a2 is None:
                acc_ref[...] += lax.dot_general(a_ref[...], b_ref[...], dims, preferred_element_type=F32)
            else:
                @pl.when(k < nk1)
                def _():
                    acc_ref[...] += lax.dot_general(a_ref[...], b_ref[...], dims, preferred_element_type=F32)

                @pl.when(k >= nk1)
                def _():
                    acc_ref[...] += lax.dot_general(refs[n_in - 1][...], b_ref[...], dims, preferred_element_type=F32)

            @pl.when(k == nk - 1)
            def _():
                finish(acc_ref[...])

    in_specs = [a_spec, b_spec]
    args = [a, b]
    if has_res:
        in_specs.append(pl.BlockSpec((tm, tn), lambda i, j, k: (i, j)))
        args.append(res)
    aliases = {}
    if has_prev:
        aliases = {len(args): 0}
        in_specs.append(pl.BlockSpec(memory_space=pl.ANY))
        args.append(prev)
    for d in deps:
        in_specs.append(pl.BlockSpec(memory_space=pl.ANY))
        args.append(d)
    if a2 is not None:
        in_specs.append(a2_spec)
        args.append(a2)
    return pl.pallas_call(
        body,
        name=name,
        grid=(M // tm, nj, nk),
        in_specs=in_specs,
        out_specs=o_spec,
        out_shape=jax.ShapeDtypeStruct((M, out_w), out_dtype),
        scratch_shapes=[pltpu.VMEM((tm, tn), F32)] if nk > 1 else [],
        input_output_aliases=aliases,
        compiler_params=_params(("parallel", "parallel", "arbitrary")),
    )(*args)


def _mm_res_rms(a, b, res, g, name, deps=()):
    M, K = a.shape
    N = b.shape[1]
    tm = _pick(M, (512, 256, 128))

    def body(a_ref, b_ref, res_ref, g_ref, *rest):
        x_ref, h_ref = rest[-2:]
        xv = res_ref[...] + jnp.dot(a_ref[...], b_ref[...], preferred_element_type=F32)
        x_ref[...] = xv
        r = lax.rsqrt(jnp.mean(xv * xv, axis=-1, keepdims=True) + EPS)
        h_ref[...] = (xv * r * g_ref[...]).astype(BF16)

    row = pl.BlockSpec((tm, N), lambda i: (i, 0))
    return pl.pallas_call(
        body, name=name, grid=(M // tm,),
        in_specs=[pl.BlockSpec((tm, K), lambda i: (i, 0)), pl.BlockSpec((K, N), lambda i: (0, 0)), row,
                  pl.BlockSpec((1, N), lambda i: (0, 0))] + [pl.BlockSpec(memory_space=pl.ANY)] * len(deps),
        out_specs=[row, row],
        out_shape=[jax.ShapeDtypeStruct((M, N), F32), jax.ShapeDtypeStruct((M, N), BF16)],
        compiler_params=_params(("parallel",)),
    )(a, b, res, g, *deps)


HALO = 16
CH = 256


def _fill_pad(pad_ref, chunk_fn, T):
    z = jnp.zeros((HALO, pad_ref.shape[1]), F32)
    pad_ref[pl.ds(0, HALO), :] = z
    pad_ref[pl.ds(HALO + T, HALO), :] = z

    def body(c, carry):
        r0 = pl.multiple_of(c * CH, CH)
        pad_ref[pl.ds(HALO + r0, CH), :] = chunk_fn(r0)
        return carry

    lax.fori_loop(0, T // CH, body, 0)


def _ext(pad_ref, r0):
    return pad_ref[pl.ds(r0, CH + 2 * HALO), :]


def _tap(ext, o):
    if o == 0:
        return ext[HALO:HALO + CH]
    return pltpu.roll(ext, (-o) % ext.shape[0], axis=0)[HALO:HALO + CH]


def _chunks(T, fn, init=0):
    def body(c, carry):
        return fn(pl.multiple_of(c * CH, CH), carry)

    return lax.fori_loop(0, T // CH, body, init)


def _conv3(ext, w):
    return _tap(ext, -1) * w[0:1] + _tap(ext, 0) * w[1:2] + _tap(ext, 1) * w[2:3]


def _conv3_t(ext, w):
    return _tap(ext, 1) * w[0:1] + _tap(ext, 0) * w[1:2] + _tap(ext, -1) * w[2:3]


def _sigmoid(x):
    return 0.5 * (jnp.tanh(0.5 * x) + 1.0)


def _silu(x):
    h = 0.5 * x
    return h * (jnp.tanh(h) + 1.0)


_GELU_C = 0.7978845608028654


def _gelu(x):
    return 0.5 * x * (1.0 + jnp.tanh(_GELU_C * (x + 0.044715 * (x * x * x))))


def _gelu_grad(x):
    t = jnp.tanh(_GELU_C * (x + 0.044715 * (x * x * x)))
    return 0.5 * (1.0 + t) + 0.5 * x * (1.0 - t * t) * (_GELU_C * (1.0 + 3.0 * 0.044715 * (x * x)))


def _row_tile(T):
    return _pick(T, (256, 128))


def _norm_tile(T):
    return _pick(T, (512, 256, 128))


def _rms_fwd(x, g, name):
    T, D = x.shape
    tm = _norm_tile(T)

    def body(x_ref, g_ref, h_ref):
        xv = x_ref[...]
        r = lax.rsqrt(jnp.mean(xv * xv, axis=-1, keepdims=True) + EPS)
        h_ref[...] = (xv * r * g_ref[...]).astype(BF16)

    return pl.pallas_call(
        body, name=name, grid=(T // tm,),
        in_specs=[pl.BlockSpec((tm, D), lambda i: (i, 0)), pl.BlockSpec((1, D), lambda i: (0, 0))],
        out_specs=pl.BlockSpec((tm, D), lambda i: (i, 0)),
        out_shape=jax.ShapeDtypeStruct((T, D), BF16),
        compiler_params=_params(("parallel",)),
    )(x, g)


def _rms_bwd(dh, x, g, dres, name, deps=()):
    T, D = x.shape
    tm = _norm_tile(T)

    def body(dh_ref, x_ref, g_ref, dres_ref, *rest):
        dx_ref, dxb_ref, dg_ref = rest[len(deps):]
        i = pl.program_id(0)
        xv = x_ref[...]
        r = lax.rsqrt(jnp.mean(xv * xv, axis=-1, keepdims=True) + EPS)
        n = xv * r
        dh_v = dh_ref[...].astype(F32)
        dn = dh_v * g_ref[...]
        dx = dres_ref[...] + r * (dn - n * jnp.mean(dn * n, axis=-1, keepdims=True))
        dx_ref[...] = dx
        dxb_ref[...] = dx.astype(BF16)
        dg = jnp.sum(dh_v * n, axis=0, keepdims=True)

        @pl.when(i == 0)
        def _():
            dg_ref[...] = dg

        @pl.when(i > 0)
        def _():
            dg_ref[...] += dg

    row = pl.BlockSpec((tm, D), lambda i: (i, 0))
    vec = pl.BlockSpec((1, D), lambda i: (0, 0))
    return pl.pallas_call(
        body, name=name, grid=(T // tm,),
        in_specs=[row, row, vec, row] + [pl.BlockSpec(memory_space=pl.ANY)] * len(deps),
        out_specs=[row, row, vec],
        out_shape=[jax.ShapeDtypeStruct((T, D), F32), jax.ShapeDtypeStruct((T, D), BF16),
                   jax.ShapeDtypeStruct((1, D), F32)],
        compiler_params=_params(("arbitrary",)),
    )(dh, x, g, dres, *deps)


def _loss_head(x, tgt, g, name):
    T, D = x.shape
    tm = _norm_tile(T)

    def body(x_ref, t_ref, g_ref, dx_ref, dxb_ref, dg_ref, l_ref):
        i = pl.program_id(0)
        xv = x_ref[...]
        gv = g_ref[...]
        r = lax.rsqrt(jnp.mean(xv * xv, axis=-1, keepdims=True) + EPS)
        n = xv * r
        e = n * gv - t_ref[...]
        dy = e * (1.0 / D)
        dn = dy * gv
        dx = r * (dn - n * jnp.mean(dn * n, axis=-1, keepdims=True))
        dx_ref[...] = dx
        dxb_ref[...] = dx.astype(BF16)
        dg = jnp.sum(dy * n, axis=0, keepdims=True)
        per_tok = jnp.mean(e * e, axis=-1, keepdims=True)
        lv = jnp.broadcast_to(0.5 * jnp.sum(per_tok, axis=0, keepdims=True), (1, LANE))

        @pl.when(i == 0)
        def _():
            dg_ref[...] = dg
            l_ref[...] = lv

        @pl.when(i > 0)
        def _():
            dg_ref[...] += dg
            l_ref[...] += lv

    row = pl.BlockSpec((tm, D), lambda i: (i, 0))
    vec = pl.BlockSpec((1, D), lambda i: (0, 0))
    return pl.pallas_call(
        body, name=name, grid=(T // tm,),
        in_specs=[row, row, vec],
        out_specs=[row, row, vec, pl.BlockSpec((1, LANE), lambda i: (0, 0))],
        out_shape=[jax.ShapeDtypeStruct((T, D), F32), jax.ShapeDtypeStruct((T, D), BF16),
                   jax.ShapeDtypeStruct((1, D), F32), jax.ShapeDtypeStruct((1, LANE), F32)],
        compiler_params=_params(("arbitrary",)),
    )(x, tgt, g)


POOL_WINDOWS = (2, 4, 8, 16)


def _pool_cnt(r0, w, T, shape):
    t = r0 + lax.broadcasted_iota(jnp.int32, shape, 0)
    lo = jnp.maximum(t - w // 2, 0)
    hi = jnp.minimum(t + (w - w // 2 - 1), T - 1)
    return (hi - lo + 1).astype(F32)


def _window_sum(ext, w, mirrored=False):
    n = ext.shape[0]
    cur = ext + pltpu.roll(ext, n - 1 if mirrored else 1, axis=0)
    width, sh = 2, 1
    while width < w:
        cur = pltpu.roll(cur, sh, axis=0) + pltpu.roll(cur, n - sh, axis=0)
        width, sh = 2 * width, 2 * sh
    return cur[HALO:HALO + CH]


def _pooled(pad_ref, r0, w, T):
    ext = _ext(pad_ref, r0)
    cur = ext[HALO:HALO + CH]
    return _window_sum(ext, w) / _pool_cnt(r0, w, T, cur.shape) - cur


def _pool_fwd(P, w_pool, scale, name):
    T = P.shape[0]
    G, PG, _ = w_pool.shape

    def body(a_ref, w_ref, s_ref, o_ref, pad_ref):
        g = pl.program_id(0)
        _fill_pad(pad_ref, lambda r0: a_ref[pl.ds(r0, CH), :].astype(F32), T)
        for gi, w in enumerate(POOL_WINDOWS):
            @pl.when(g == gi)
            def _(w=w):
                def chunk(r0, carry):
                    pooled = _pooled(pad_ref, r0, w, T).astype(BF16)
                    y = jnp.dot(pooled, w_ref[...], preferred_element_type=F32) * s_ref[...]
                    o_ref[pl.ds(r0, CH), :] = y.astype(BF16)
                    return carry

                _chunks(T, chunk)

    return pl.pallas_call(
        body, name=name, grid=(G,),
        in_specs=[pl.BlockSpec((T, PG), lambda g: (0, g)),
                  pl.BlockSpec((None, PG, PG), lambda g: (g, 0, 0)),
                  pl.BlockSpec((1, PG), lambda g: (0, g))],
        out_specs=pl.BlockSpec((T, PG), lambda g: (0, g)),
        out_shape=jax.ShapeDtypeStruct((T, G * PG), BF16),
        scratch_shapes=[pltpu.VMEM((T + 2 * HALO, PG), F32)],
        compiler_params=_params(("parallel",)),
    )(P, w_pool, scale)


def _pool_bwd(P, dy, w_pool, scale, name):
    T = P.shape[0]
    G, PG, _ = w_pool.shape

    def body(a_ref, dy_ref, w_ref, s_ref, da_ref, dw_ref, ds_ref, pad_ref, gpad_ref):
        g = pl.program_id(0)
        gpad_ref[pl.ds(0, HALO), :] = jnp.zeros((HALO, PG), F32)
        gpad_ref[pl.ds(HALO + T, HALO), :] = jnp.zeros((HALO, PG), F32)
        _fill_pad(pad_ref, lambda r0: a_ref[pl.ds(r0, CH), :].astype(F32), T)
        for gi, w in enumerate(POOL_WINDOWS):
            @pl.when(g == gi)
            def _(w=w):
                def chunk1(r0, carry):
                    dw, ds = carry
                    pooled = _pooled(pad_ref, r0, w, T).astype(BF16)
                    ypre = jnp.dot(pooled, w_ref[...], preferred_element_type=F32)
                    dyv = dy_ref[pl.ds(r0, CH), :].astype(F32)
                    ds = ds + jnp.sum(dyv * ypre, axis=0, keepdims=True)
                    dyp = (dyv * s_ref[...]).astype(BF16)
                    dw = dw + lax.dot_general(pooled, dyp, _DIMS["tn"], preferred_element_type=F32)
                    dpool = lax.dot_general(dyp, w_ref[...], _DIMS["nt"], preferred_element_type=F32)
                    gpad_ref[pl.ds(HALO + r0, CH), :] = dpool / _pool_cnt(r0, w, T, dpool.shape)
                    return dw, ds

                dw, ds = _chunks(T, chunk1, (jnp.zeros((PG, PG), F32), jnp.zeros((1, PG), F32)))
                dw_ref[...] = dw
                ds_ref[...] = ds

                def chunk2(r0, carry):
                    ext = _ext(gpad_ref, r0)
                    cur = ext[HALO:HALO + CH]
                    acc = _window_sum(ext, w, mirrored=True)
                    da_ref[pl.ds(r0, CH), :] = (acc - cur * _pool_cnt(r0, w, T, cur.shape)).astype(BF16)
                    return carry

                _chunks(T, chunk2)

    col = pl.BlockSpec((T, PG), lambda g: (0, g))
    return pl.pallas_call(
        body, name=name, grid=(G,),
        in_specs=[col, col, pl.BlockSpec((None, PG, PG), lambda g: (g, 0, 0)), pl.BlockSpec((1, PG), lambda g: (0, g))],
        out_specs=[col, pl.BlockSpec((None, PG, PG), lambda g: (g, 0, 0)), pl.BlockSpec((1, PG), lambda g: (0, g))],
        out_shape=[jax.ShapeDtypeStruct((T, G * PG), BF16), jax.ShapeDtypeStruct((G, PG, PG), F32),
                   jax.ShapeDtypeStruct((1, G * PG), F32)],
        scratch_shapes=[pltpu.VMEM((T + 2 * HALO, PG), F32), pltpu.VMEM((T + 2 * HALO, PG), F32)],
        compiler_params=_params(("parallel",)),
    )(P, dy, w_pool, scale)


SGU_CHUNK = 128


def _sgu_common(u_ref, v_ref, gs_ref):
    up = u_ref[...].astype(F32)
    vp = v_ref[...].astype(F32)
    gv = _gelu(vp)
    rv = lax.rsqrt(jnp.mean(gv * gv, axis=-1, keepdims=True) + EPS)
    nrm = gv * rv
    return up, vp, nrm, rv, (nrm * gs_ref[...]).astype(BF16)


def _sgu_fwd(P, sgu_w, b_exp, gs, name):
    T = P.shape[0]
    G, _, SG = b_exp.shape
    DP = G * SG
    tm = _row_tile(T)

    def body(u_ref, v_ref, w_ref, b_ref, gs_ref, o_ref):
        up, _, _, _, vn = _sgu_common(u_ref, v_ref, gs_ref)
        gu = _gelu(up)
        for n in range(tm // SGU_CHUNK):
            rows = slice(n * SGU_CHUNK, (n + 1) * SGU_CHUNK)
            for g in range(G):
                cols = slice(g * SG, (g + 1) * SG)
                z = jnp.dot(w_ref[g], vn[rows, cols], preferred_element_type=F32) + b_ref[g]
                o_ref[rows, cols] = (gu[rows, cols] * z).astype(BF16)

    return pl.pallas_call(
        body, name=name, grid=(T // tm,),
        in_specs=[pl.BlockSpec((tm, DP), lambda i: (i, 1)), pl.BlockSpec((tm, DP), lambda i: (i, 2)),
                  pl.BlockSpec(sgu_w.shape, lambda i: (0, 0, 0)), pl.BlockSpec(b_exp.shape, lambda i: (0, 0, 0)),
                  pl.BlockSpec((1, DP), lambda i: (0, 0))],
        out_specs=pl.BlockSpec((tm, DP), lambda i: (i, 0)),
        out_shape=jax.ShapeDtypeStruct((T, DP), BF16),
        compiler_params=_params(("parallel",)),
    )(P, P, sgu_w, b_exp, gs)


def _sgu_bwd(P, dy, da, sgu_w, b_exp, gs, name):
    T = P.shape[0]
    G, _, SG = b_exp.shape
    DP = G * SG
    tm = _row_tile(T)

    def body(u_ref, v_ref, dy_ref, da_ref, w_ref, b_ref, gs_ref, o_ref, dw_ref, db_ref, dgs_ref, dzs_ref):
        i = pl.program_id(0)
        up, vp, nrm, rv, vn = _sgu_common(u_ref, v_ref, gs_ref)
        gu = _gelu(up)
        dyv = dy_ref[...].astype(F32)
        o_ref[:, 0:DP] = da_ref[...]

        @pl.when(i == 0)
        def _():
            dw_ref[...] = jnp.zeros(dw_ref.shape, F32)
            dzs_ref[...] = jnp.zeros(dzs_ref.shape, F32)
            dgs_ref[...] = jnp.zeros(dgs_ref.shape, F32)

        dgs = jnp.zeros((1, DP), F32)
        for n in range(tm // SGU_CHUNK):
            rows = slice(n * SGU_CHUNK, (n + 1) * SGU_CHUNK)
            dvn_parts = []
            for g in range(G):
                cols = slice(g * SG, (g + 1) * SG)
                vng = vn[rows, cols]
                z = jnp.dot(w_ref[g], vng, preferred_element_type=F32) + b_ref[g]
                dyg = dyv[rows, cols]
                du = dyg * z
                o_ref[rows, DP + g * SG:DP + (g + 1) * SG] = (du * _gelu_grad(up[rows, cols])).astype(BF16)
                dz = dyg * gu[rows, cols]
                dzb = dz.astype(BF16)
                dzs_ref[g] += dz
                dw_ref[g] += lax.dot_general(dzb, vng, _DIMS["nt"], preferred_element_type=F32)
                dvn_parts.append(lax.dot_general(w_ref[g], dzb, _DIMS["tn"], preferred_element_type=F32))
            dvn = jnp.concatenate(dvn_parts, axis=1)
            nr = nrm[rows]
            dgs = dgs + jnp.sum(dvn * nr, axis=0, keepdims=True)
            dn = dvn * gs_ref[...]
            dgv = rv[rows] * (dn - nr * jnp.mean(dn * nr, axis=-1, keepdims=True))
            o_ref[rows, 2 * DP:3 * DP] = (dgv * _gelu_grad(vp[rows])).astype(BF16)
        dgs_ref[...] += dgs

        @pl.when(i == T // tm - 1)
        def _():
            for g in range(G):
                db_ref[g] = jnp.broadcast_to(jnp.sum(dzs_ref[g], axis=1, keepdims=True), (SGU_CHUNK, SG))

    full3 = lambda a: pl.BlockSpec(a.shape, lambda i: (0, 0, 0))
    return pl.pallas_call(
        body, name=name, grid=(T // tm,),
        in_specs=[pl.BlockSpec((tm, DP), lambda i: (i, 1)), pl.BlockSpec((tm, DP), lambda i: (i, 2)),
                  pl.BlockSpec((tm, DP), lambda i: (i, 0)), pl.BlockSpec((tm, DP), lambda i: (i, 0)),
                  full3(sgu_w), full3(b_exp), pl.BlockSpec((1, DP), lambda i: (0, 0))],
        out_specs=[pl.BlockSpec((tm, 3 * DP), lambda i: (i, 0)), full3(sgu_w), full3(b_exp),
                   pl.BlockSpec((1, DP), lambda i: (0, 0))],
        out_shape=[jax.ShapeDtypeStruct((T, 6 * DP), BF16), jax.ShapeDtypeStruct(sgu_w.shape, F32),
                   jax.ShapeDtypeStruct(b_exp.shape, F32), jax.ShapeDtypeStruct((1, DP), F32)],
        scratch_shapes=[pltpu.VMEM(b_exp.shape, F32)],
        compiler_params=_params(("arbitrary",)),
    )(P, P, dy, da, sgu_w, b_exp, gs)


def _conv_fwd(P, cw, name):
    T = P.shape[0]
    DP = cw.shape[1]
    tc = 256
    nb = DP // tc

    def body(xc_ref, bg_ref, cg_ref, w_ref, o_ref, pad_ref):
        _fill_pad(pad_ref, lambda r0: cg_ref[pl.ds(r0, CH), :].astype(F32) * xc_ref[pl.ds(r0, CH), :].astype(F32), T)
        w = w_ref[...]

        def chunk(r0, carry):
            cq = _conv3(_ext(pad_ref, r0), w)
            o_ref[pl.ds(r0, CH), :] = (bg_ref[pl.ds(r0, CH), :].astype(F32) * cq).astype(BF16)
            return carry

        _chunks(T, chunk)

    col = lambda off: pl.BlockSpec((T, tc), lambda j: (0, off * nb + j))
    return pl.pallas_call(
        body, name=name, grid=(nb,),
        in_specs=[col(3), col(4), col(5), pl.BlockSpec((3, tc), lambda j: (0, j))],
        out_specs=pl.BlockSpec((T, tc), lambda j: (0, j)),
        out_shape=jax.ShapeDtypeStruct((T, DP), BF16),
        scratch_shapes=[pltpu.VMEM((T + 2 * HALO, tc), F32)],
        compiler_params=_params(("parallel",)),
    )(P, P, P, cw)


def _conv_bwd(P, dy, cw, dPl, name):
    T = P.shape[0]
    DP = cw.shape[1]
    tc = 256
    nb = DP // tc

    def body(xc_ref, bg_ref, cg_ref, dy_ref, w_ref, prev_ref, o_ref, dw_ref, qpad_ref, dpad_ref):
        del prev_ref
        seg = pl.program_id(1)
        w = w_ref[...]
        rows = lambda ref, r0: ref[pl.ds(r0, CH), :].astype(F32)

        @pl.when(seg == 0)
        def _():
            _fill_pad(qpad_ref, lambda r0: rows(cg_ref, r0) * rows(xc_ref, r0), T)
            _fill_pad(dpad_ref, lambda r0: rows(dy_ref, r0) * rows(bg_ref, r0), T)

        @pl.when(seg == 0)
        def _():
            def chunk(r0, carry):
                dext = _ext(dpad_ref, r0)
                taps = [_tap(dext, 1), _tap(dext, 0), _tap(dext, -1)]
                dq = taps[0] * w[0:1] + taps[1] * w[1:2] + taps[2] * w[2:3]
                o_ref[pl.ds(r0, CH), :] = (dq * rows(cg_ref, r0)).astype(BF16)
                q = qpad_ref[pl.ds(HALO + r0, CH), :]
                return [a + jnp.sum(q * t, axis=0, keepdims=True) for a, t in zip(carry, taps)]

            dws = _chunks(T, chunk, [jnp.zeros((1, tc), F32)] * 3)
            dw_ref[...] = jnp.concatenate(dws + [jnp.zeros((5, tc), F32)], axis=0)

        @pl.when(seg == 1)
        def _():
            def chunk(r0, carry):
                cq = _conv3(_ext(qpad_ref, r0), w)
                o_ref[pl.ds(r0, CH), :] = (rows(dy_ref, r0) * cq).astype(BF16)
                return carry

            _chunks(T, chunk)

        @pl.when(seg == 2)
        def _():
            def chunk(r0, carry):
                dq = _conv3_t(_ext(dpad_ref, r0), w)
                o_ref[pl.ds(r0, CH), :] = (dq * rows(xc_ref, r0)).astype(BF16)
                return carry

            _chunks(T, chunk)

    col = lambda off: pl.BlockSpec((T, tc), lambda j, s: (0, off * nb + j))
    return pl.pallas_call(
        body, name=name, grid=(nb, 3),
        in_specs=[col(3), col(4), col(5), pl.BlockSpec((T, tc), lambda j, s: (0, j)),
                  pl.BlockSpec((3, tc), lambda j, s: (0, j)), pl.BlockSpec(memory_space=pl.ANY)],
        out_specs=[pl.BlockSpec((T, tc), lambda j, s: (0, (3 + s) * nb + j)), pl.BlockSpec((8, tc), lambda j, s: (0, j))],
        out_shape=[jax.ShapeDtypeStruct(dPl.shape, BF16), jax.ShapeDtypeStruct((8, DP), F32)],
        scratch_shapes=[pltpu.VMEM((T + 2 * HALO, tc), F32), pltpu.VMEM((T + 2 * HALO, tc), F32)],
        input_output_aliases={5: 0},
        compiler_params=_params(("parallel", "arbitrary")),
    )(P, P, P, dy, cw, dPl)


def _merge_fwd(P, ys, ws, name):
    T = P.shape[0]
    DP, D = ws[0].shape
    tm = _pick(T, (512, 256, 128))
    tn = _pick(D, (1024, 512, 256))
    goff = (6 * DP) // tn

    def body(ya, yb, yc, wa, wb, wc, ga, gb, gc, oa_ref, ob_ref, oc_ref, m_ref):
        m = None
        for y, w, g, o_ref in ((ya, wa, ga, oa_ref), (yb, wb, gb, ob_ref), (yc, wc, gc, oc_ref)):
            o = jnp.dot(y[...], w[...], preferred_element_type=F32)
            o_ref[...] = o.astype(BF16)
            t = _sigmoid(g[...].astype(F32)) * o
            m = t if m is None else m + t
        m_ref[...] = m.astype(BF16)

    yspec = pl.BlockSpec((tm, DP), lambda i, j: (i, 0))
    wspec = pl.BlockSpec((DP, tn), lambda i, j: (0, j))
    gspec = lambda b: pl.BlockSpec((tm, tn), lambda i, j: (i, goff + b * (D // tn) + j))
    ospec = pl.BlockSpec((tm, tn), lambda i, j: (i, j))
    return pl.pallas_call(
        body, name=name, grid=(T // tm, D // tn),
        in_specs=[yspec] * 3 + [wspec] * 3 + [gspec(0), gspec(1), gspec(2)],
        out_specs=[ospec] * 4,
        out_shape=[jax.ShapeDtypeStruct((T, D), BF16)] * 4,
        compiler_params=_params(("parallel", "parallel")),
    )(*ys, *ws, P, P, P)


def _merge_bwd(P, dM, os_, name):
    T, D = dM.shape
    tm = _row_tile(T)

    def body(dm_ref, oa, ob, oc, g_ref, da_ref, db_ref, dc_ref, dg_ref):
        dm = dm_ref[...].astype(F32)
        for b, (o_ref, d_ref) in enumerate(((oa, da_ref), (ob, db_ref), (oc, dc_ref))):
            s = _sigmoid(g_ref[:, b * D:(b + 1) * D].astype(F32))
            d_ref[...] = (dm * s).astype(BF16)
            dg_ref[:, b * D:(b + 1) * D] = (dm * o_ref[...].astype(F32) * s * (1.0 - s)).astype(BF16)

    row = pl.BlockSpec((tm, D), lambda i: (i, 0))
    return pl.pallas_call(
        body, name=name, grid=(T // tm,),
        in_specs=[row] * 4 + [pl.BlockSpec((tm, 3 * D), lambda i: (i, 1))],
        out_specs=[row] * 3 + [pl.BlockSpec((tm, 3 * D), lambda i: (i, 0))],
        out_shape=[jax.ShapeDtypeStruct((T, D), BF16)] * 3 + [jax.ShapeDtypeStruct((T, 3 * D), BF16)],
        compiler_params=_params(("parallel",)),
    )(dM, *os_, P)


def _glu_fwd(U, fw, name):
    T, F2 = U.shape
    F = F2 // 2
    tc = 256
    nb = F // tc

    def body(ug_ref, uv_ref, wg_ref, wv_ref, o_ref, gpad_ref, vpad_ref):
        _fill_pad(gpad_ref, lambda r0: ug_ref[pl.ds(r0, CH), :].astype(F32), T)
        _fill_pad(vpad_ref, lambda r0: uv_ref[pl.ds(r0, CH), :].astype(F32), T)
        wg, wv = wg_ref[...], wv_ref[...]

        def chunk(r0, carry):
            gate = _conv3(_ext(gpad_ref, r0), wg)
            val = _conv3(_ext(vpad_ref, r0), wv)
            o_ref[pl.ds(r0, CH), :] = (_silu(gate) * val).astype(BF16)
            return carry

        _chunks(T, chunk)

    return pl.pallas_call(
        body, name=name, grid=(nb,),
        in_specs=[pl.BlockSpec((T, tc), lambda j: (0, j)), pl.BlockSpec((T, tc), lambda j: (0, nb + j)),
                  pl.BlockSpec((3, tc), lambda j: (0, j)), pl.BlockSpec((3, tc), lambda j: (0, nb + j))],
        out_specs=pl.BlockSpec((T, tc), lambda j: (0, j)),
        out_shape=jax.ShapeDtypeStruct((T, F), BF16),
        scratch_shapes=[pltpu.VMEM((T + 2 * HALO, tc), F32)] * 2,
        compiler_params=_params(("parallel",)),
    )(U, U, fw, fw)


def _glu_bwd(U, dact, fw, name):
    T, F2 = U.shape
    F = F2 // 2
    tc = 128
    nb = F // tc

    def body(ug_ref, uv_ref, da_ref, wg_ref, wv_ref, dug_ref, duv_ref, dwg_ref, dwv_ref, gpad, vpad, dgpad, dvpad):
        _fill_pad(gpad, lambda r0: ug_ref[pl.ds(r0, CH), :].astype(F32), T)
        _fill_pad(vpad, lambda r0: uv_ref[pl.ds(r0, CH), :].astype(F32), T)
        wg, wv = wg_ref[...], wv_ref[...]

        def chunk1(r0, carry):
            gate = _conv3(_ext(gpad, r0), wg)
            val = _conv3(_ext(vpad, r0), wv)
            s = _sigmoid(gate)
            da = da_ref[pl.ds(r0, CH), :].astype(F32)
            dgpad[pl.ds(HALO + r0, CH), :] = da * val * (s * (1.0 + gate * (1.0 - s)))
            dvpad[pl.ds(HALO + r0, CH), :] = da * (gate * s)
            return carry

        z = jnp.zeros((HALO, tc), F32)
        for p in (dgpad, dvpad):
            p[pl.ds(0, HALO), :] = z
            p[pl.ds(HALO + T, HALO), :] = z
        _chunks(T, chunk1)

        def chunk2(r0, carry):
            new = []
            for pad, dpad, wk, o_ref in ((gpad, dgpad, wg, dug_ref), (vpad, dvpad, wv, duv_ref)):
                dext = _ext(dpad, r0)
                taps = [_tap(dext, 1), _tap(dext, 0), _tap(dext, -1)]
                o_ref[pl.ds(r0, CH), :] = (taps[0] * wk[0:1] + taps[1] * wk[1:2] + taps[2] * wk[2:3]).astype(BF16)
                xs = pad[pl.ds(HALO + r0, CH), :]
                new += [jnp.sum(xs * t, axis=0, keepdims=True) for t in taps]
            return [a + b for a, b in zip(carry, new)]

        dws = _chunks(T, chunk2, [jnp.zeros((1, tc), F32)] * 6)
        dwg_ref[...] = jnp.concatenate(dws[:3] + [jnp.zeros((5, tc), F32)], axis=0)
        dwv_ref[...] = jnp.concatenate(dws[3:] + [jnp.zeros((5, tc), F32)], axis=0)

    lo = pl.BlockSpec((T, tc), lambda j: (0, j))
    hi = pl.BlockSpec((T, tc), lambda j: (0, nb + j))
    wlo = pl.BlockSpec((3, tc), lambda j: (0, j))
    whi = pl.BlockSpec((3, tc), lambda j: (0, nb + j))
    dwspec = pl.BlockSpec((8, tc), lambda j: (0, j))
    return pl.pallas_call(
        body, name=name, grid=(nb,),
        in_specs=[lo, hi, lo, wlo, whi],
        out_specs=[lo, lo, dwspec, dwspec],
        out_shape=[jax.ShapeDtypeStruct((T, F), BF16)] * 2 + [jax.ShapeDtypeStruct((8, F), F32)] * 2,
        scratch_shapes=[pltpu.VMEM((T + 2 * HALO, tc), F32)] * 4,
        compiler_params=_params(("parallel",)),
    )(U, U, dact, fw, fw)


_ANY = pl.BlockSpec(memory_space=pl.ANY)
_MESH = pl.DeviceIdType.MESH


def _mesh_pos():
    return lax.axis_index("x"), lax.axis_index("y"), lax.axis_index("c")


def _other_chips(x, y):
    return [(1 - x, y), (x, 1 - y), (1 - x, 1 - y)]


_HBM = pl.BlockSpec(memory_space=pltpu.HBM)
_SEM = pl.BlockSpec(memory_space=pltpu.SEMAPHORE)
_EFFECT = pltpu.SideEffectType.DATAFLOW_SIDE_EFFECTING


def _in_hbm(a):
    return pltpu.with_memory_space_constraint(a, pltpu.HBM)


def _split_start(groups, name, deps=()):
    flat, where = [], []
    for bufs, _, _ in groups:
        idx = []
        for b in bufs:
            hit = [i for i, f in enumerate(flat) if f is b]
            if not hit:
                flat.append(b)
            idx.append(hit[0] if hit else len(flat) - 1)
        where.append(idx)
    nb = len(flat)
    ng = len(groups)

    def body(*refs):
        ins = refs[:nb]
        sems = refs[nb + len(deps):nb + len(deps) + 2 * ng]
        token = refs[-1]
        for gi, (_, make_copies, _) in enumerate(groups):
            for cp in make_copies([ins[i] for i in where[gi]], sems[2 * gi], sems[2 * gi + 1]):
                cp.start()
        token[...] = jnp.zeros(token.shape, token.dtype)

    sem_shapes = []
    for _, _, n in groups:
        sem_shapes += [pltpu.SemaphoreType.DMA((n,)), pltpu.SemaphoreType.DMA((n,))]
    outs = pl.pallas_call(
        body, name=name,
        in_specs=[_HBM] * nb + [_ANY] * len(deps),
        out_specs=[_SEM] * (2 * ng) + [_HBM] * nb + [pl.BlockSpec(memory_space=pltpu.VMEM)],
        out_shape=sem_shapes + [pltpu.HBM(b.shape, b.dtype) for b in flat] + [jax.ShapeDtypeStruct((8, LANE), F32)],
        input_output_aliases={i: 2 * ng + i for i in range(nb)},
        compiler_params=pltpu.CompilerParams(has_side_effects=_EFFECT),
    )(*[_in_hbm(b) for b in flat], *deps)
    res = [(outs[2 * gi], outs[2 * gi + 1], [outs[2 * ng + i] for i in where[gi]]) for gi in range(ng)]
    return res, outs[-1]


def _split_wait(started, make_copies, after, name):
    send, recv, bufs = started
    nb = len(bufs)
    after = list(after) if isinstance(after, (list, tuple)) else [after]

    def body(*refs):
        for cp in make_copies(refs[:nb], refs[nb], refs[nb + 1]):
            cp.wait_send()
            cp.wait_recv()

    outs = pl.pallas_call(
        body, name=name,
        in_specs=[_HBM] * nb + [_SEM, _SEM] + [_ANY] * len(after),
        out_specs=[_HBM] * nb,
        out_shape=[pltpu.HBM(b.shape, b.dtype) for b in bufs],
        input_output_aliases={i: i for i in range(nb)},
        compiler_params=pltpu.CompilerParams(has_side_effects=_EFFECT),
    )(*bufs, send, recv, *after)
    return list(outs)


def _slot(buf, s, cols=None):
    if len(buf.shape) == 3:
        return buf.at[s] if cols is None else buf.at[s, :, pl.ds(cols[0], cols[1])]
    ns = buf.shape[1] // N_DEV
    first, width = (0, ns) if cols is None else cols
    return buf.at[:, pl.ds(pl.multiple_of(s * ns, LANE) + first, width)]


def _copies_gather_chips(nbuf, cols=None):
    def make(bufs, send, recv):
        x, y, c = _mesh_pos()
        targets = [(x, y, 1 - c)] + [(px, py, c) for px, py in _other_chips(x, y)]
        cps = []
        for a in range(nbuf):
            mine = _slot(bufs[a], 4 * x + 2 * y + c, cols)
            for k, to in enumerate(targets):
                cps.append(pltpu.make_async_remote_copy(src_ref=mine, dst_ref=mine, send_sem=send.at[4 * a + k],
                                                        recv_sem=recv.at[4 * a + k], device_id=to, device_id_type=_MESH))
        return cps

    return make, 4 * nbuf


def _copies_gather_forward(nbuf, cols=None):
    def make(bufs, send, recv):
        x, y, c = _mesh_pos()
        cps = []
        for a in range(nbuf):
            for j, (px, py) in enumerate(_other_chips(x, y)):
                slot = _slot(bufs[a], 4 * px + 2 * py + c, cols)
                cps.append(pltpu.make_async_remote_copy(src_ref=slot, dst_ref=slot, send_sem=send.at[3 * a + j],
                                                        recv_sem=recv.at[3 * a + j], device_id=(x, y, 1 - c),
                                                        device_id_type=_MESH))
        return cps

    return make, 3 * nbuf


def _copies_rs_sibling(n):
    def make(bufs, send, recv):
        x, y, c = _mesh_pos()
        cps = []
        for a in range(n):
            for q in range(4):
                cps.append(pltpu.make_async_remote_copy(
                    src_ref=_slot(bufs[a], 2 * q + (1 - c)), dst_ref=bufs[n + a].at[q], send_sem=send.at[4 * a + q],
                    recv_sem=recv.at[4 * a + q], device_id=(x, y, 1 - c), device_id_type=_MESH))
        return cps

    return make, 4 * n


def _copies_rs_chips(n):
    def make(bufs, send, recv):
        x, y, c = _mesh_pos()
        cps = []
        for a in range(n):
            for j, (px, py) in enumerate(_other_chips(x, y)):
                cps.append(pltpu.make_async_remote_copy(
                    src_ref=bufs[a].at[j], dst_ref=bufs[n + a].at[j], send_sem=send.at[3 * a + j],
                    recv_sem=recv.at[3 * a + j], device_id=(px, py, c), device_id_type=_MESH))
        return cps

    return make, 3 * n


def _cast_place(stacked, layer, slot, dtype, name, deps=(), by_cols=False):
    _, R, C = stacked.shape
    tr = _rows_tile(R, C)

    def body(slot_ref, x_ref, *rest):
        del slot_ref
        rest[-1][...] = x_ref[...].astype(dtype)

    if by_cols:
        out_spec = pl.BlockSpec((tr, C), lambda i, s: (i, s[0]))
        out_shape = jax.ShapeDtypeStruct((R, N_DEV * C), dtype)
    else:
        out_spec = pl.BlockSpec((None, tr, C), lambda i, s: (s[0], i, 0))
        out_shape = jax.ShapeDtypeStruct((N_DEV, R, C), dtype)
    return pl.pallas_call(
        body, name=name,
        grid_spec=pltpu.PrefetchScalarGridSpec(
            num_scalar_prefetch=1, grid=(R // tr,),
            in_specs=[pl.BlockSpec((None, tr, C), lambda i, s: (layer, i, 0))] + [_ANY] * len(deps),
            out_specs=out_spec),
        out_shape=out_shape,
        compiler_params=_params(("parallel",)),
    )(slot, stacked, *deps)


def _rows_tile(R, C, budget=2 * MIB):
    if R * C * 4 <= budget:
        return R
    for t in (2048, 1024, 704, 512, 352, 256, 128, 64, 32, 16, 8):
        if R % t == 0 and t * C * 4 <= budget:
            return t
    return R


def _chip_of_relation(j, pos):
    x, y = pos[0], pos[1]
    return 2 * jnp.where(j == 1, x, 1 - x) + jnp.where(j == 0, y, 1 - y)


def _pair_sum(grad, recv, pos, name):
    _, R, C = recv.shape
    tr = _rows_tile(R, C, 8 * MIB)

    def body(pos_ref, g_ref, r_ref, o_ref):
        del pos_ref
        o_ref[...] = (g_ref[...].astype(F32) + r_ref[...].astype(F32)).astype(BF16)

    if len(grad.shape) == 3:
        g_spec = pl.BlockSpec((None, tr, C), lambda j, i, p: (2 * _chip_of_relation(j, p) + p[2], i, 0))
    else:
        g_spec = pl.BlockSpec((tr, C), lambda j, i, p: (i, 2 * _chip_of_relation(j, p) + p[2]))
    return pl.pallas_call(
        body, name=name,
        grid_spec=pltpu.PrefetchScalarGridSpec(
            num_scalar_prefetch=1, grid=(3, R // tr),
            in_specs=[g_spec,
                      pl.BlockSpec((None, tr, C), lambda j, i, p: (_chip_of_relation(j, p), i, 0))],
            out_specs=pl.BlockSpec((None, tr, C), lambda j, i, p: (j, i, 0))),
        out_shape=jax.ShapeDtypeStruct((3, R, C), BF16),
        compiler_params=_params(("parallel", "parallel")),
    )(pos, grad, recv)


ADAM_LR = 0.001
ADAM_B1 = 0.9
ADAM_B2 = 0.999
ADAM_EPS = 1e-08
ADAM_WD = 0.01
ADAM_STEP = 10


def _adamw(w, g, m, v):
    m = ADAM_B1 * m + (1.0 - ADAM_B1) * g
    v = ADAM_B2 * v + (1.0 - ADAM_B2) * (g * g)
    m_hat = m / (1.0 - ADAM_B1 ** ADAM_STEP)
    v_hat = v / (1.0 - ADAM_B2 ** ADAM_STEP)
    delta = -ADAM_LR * (m_hat / (jnp.sqrt(v_hat) + ADAM_EPS) + ADAM_WD * w)
    return delta, m, v


def _adamw_big(grad, sib, recv, w, m, v, pos, layer, prev, name):
    L, R, C = w.shape
    tr = _rows_tile(R, C)
    has_prev = prev is not None

    def body(pos_ref, g_ref, s_ref, r0_ref, r1_ref, r2_ref, w_ref, m_ref, v_ref, *rest):
        del pos_ref
        go_ref, d_ref, mo_ref, vo_ref = rest[-4:]
        g = g_ref[...].astype(F32) + s_ref[...].astype(F32)
        g = ((g + r0_ref[...].astype(F32)) + r1_ref[...].astype(F32)) + r2_ref[...].astype(F32)
        delta, m2, v2 = _adamw(w_ref[...], g, m_ref[...], v_ref[...])
        go_ref[...] = g
        d_ref[...] = delta
        mo_ref[...] = m2
        vo_ref[...] = v2

    lay = pl.BlockSpec((None, tr, C), lambda i, p: (layer, i, 0))
    rspec = lambda j: pl.BlockSpec((None, tr, C), lambda i, p: (j, i, 0))
    mine = lambda p: 4 * p[0] + 2 * p[1] + p[2]
    if len(grad.shape) == 3:
        g_spec = pl.BlockSpec((None, tr, C), lambda i, p: (mine(p), i, 0))
    else:
        g_spec = pl.BlockSpec((tr, C), lambda i, p: (i, mine(p)))
    s_spec = pl.BlockSpec((None, tr, C), lambda i, p: (2 * p[0] + p[1], i, 0))
    in_specs = [g_spec, s_spec, rspec(0), rspec(1), rspec(2), lay, lay, lay]
    args = [pos, grad, sib, recv, recv, recv, w, m, v]
    aliases = {}
    if has_prev:
        in_specs += [_ANY] * 4
        args += list(prev)
        aliases = {9 + k: k for k in range(4)}
    return pl.pallas_call(
        body, name=name,
        grid_spec=pltpu.PrefetchScalarGridSpec(
            num_scalar_prefetch=1, grid=(R // tr,), in_specs=in_specs, out_specs=[lay] * 4),
        out_shape=[jax.ShapeDtypeStruct((L, R, C), F32)] * 4,
        input_output_aliases=aliases,
        compiler_params=_params(("parallel",)),
    )(*args)


def _sum_slots(gathered, name):
    _, R, C = gathered.shape
    tr = _rows_tile(R, C, 2 * MIB)

    def body(x_ref, o_ref):
        s = x_ref[0]
        for k in range(1, N_DEV):
            s = s + x_ref[k]
        o_ref[...] = s

    return pl.pallas_call(
        body, name=name, grid=(R // tr,),
        in_specs=[pl.BlockSpec((N_DEV, tr, C), lambda i: (0, i, 0))],
        out_specs=pl.BlockSpec((tr, C), lambda i: (i, 0)),
        out_shape=jax.ShapeDtypeStruct((R, C), F32),
        compiler_params=_params(("parallel",)),
    )(gathered)


def _adamw_small(w, g, m, v, name):
    R, C = w.shape
    tr = _rows_tile(R, C, 2 * MIB)

    def body(w_ref, g_ref, m_ref, v_ref, d_ref, mo_ref, vo_ref):
        d_ref[...], mo_ref[...], vo_ref[...] = _adamw(w_ref[...], g_ref[...], m_ref[...], v_ref[...])

    spec = pl.BlockSpec((tr, C), lambda i: (i, 0))
    return pl.pallas_call(
        body, name=name, grid=(R // tr,),
        in_specs=[spec] * 4, out_specs=[spec] * 3,
        out_shape=[jax.ShapeDtypeStruct((R, C), F32)] * 3,
        compiler_params=_params(("parallel",)),
    )(w, g, m, v)


def _pack(arrs):
    parts = []
    for a in arrs:
        flat = a.reshape(-1)
        pad = (-flat.shape[0]) % (8 * LANE)
        if pad:
            flat = jnp.pad(flat, (0, pad))
        parts.append(flat.reshape(-1, LANE))
    return jnp.concatenate(parts, axis=0)


def _unpack(packed, shapes):
    out, r = [], 0
    for s in shapes:
        n = 1
        for d in s:
            n *= d
        rows = -(-n // (8 * LANE)) * 8
        out.append(packed[r:r + rows].reshape(-1)[:n].reshape(s))
        r += rows
    return out


def kernel(x, norm1_g, w_in, w_pool, pool_scale, sgu_norm_g, sgu_w, sgu_b, conv_w, w_pool_out, w_sgu_out, w_conv_out, w_o, norm2_g, w_up, ffn_conv_w, w_down, final_g, loss_target, m_norm1_g, m_w_in, m_w_pool, m_pool_scale, m_sgu_norm_g, m_sgu_w, m_sgu_b, m_conv_w, m_w_pool_out, m_w_sgu_out, m_w_conv_out, m_w_o, m_norm2_g, m_w_up, m_ffn_conv_w, m_w_down, m_final_g, v_norm1_g, v_w_in, v_w_pool, v_pool_scale, v_sgu_norm_g, v_sgu_w, v_sgu_b, v_conv_w, v_w_pool_out, v_w_sgu_out, v_w_conv_out, v_w_o, v_norm2_g, v_w_up, v_ffn_conv_w, v_w_down, v_final_g):
    L = norm1_g.shape[0]
    x0 = x[0]
    tgt = loss_target[0]
    G, PG = w_pool.shape[1], w_pool.shape[3]
    F = w_down.shape[1] * N_DEV
    T, D = x0.shape
    tT = _pick(T, (1024, 512, 256, 128))
    tT2 = _pick(T, (2048, tT))
    tF = _pick(F, (512, 256))
    tF2 = _pick(2 * F, (1024, 512, 256))
    kF = _pick(F, (2816, 2048, 1024, 512))
    px, py, pc = _mesh_pos()
    me = 4 * px + 2 * py + pc
    pos = jnp.stack([px, py, pc]).astype(jnp.int32)

    slot = jnp.reshape(me, (1,)).astype(jnp.int32)
    ns_in = w_in.shape[2]
    nq = 3 if ns_in % (3 * 256) == 0 else 1
    chunk_cols = [(q * (ns_in // nq), ns_in // nq) for q in range(nq)]
    win0 = _cast_place(w_in, 0, slot, BF16, "place_w_in_0", by_cols=True)
    first, tok0 = _split_start([([win0],) + _copies_gather_chips(1, chunk_cols[0])], "ag_start_first")
    win0 = first[0][2][0]
    h1_first = _rms_fwd(x0, norm1_g[0][None], "rms1_fwd_0")
    stage1, group_sizes, started1 = {}, {0: 1}, {}

    def place_groups(l, deps):
        grp = [[("w_in", w_in, BF16, True)],
               [("w_pool_out", w_pool_out, BF16, True), ("w_sgu_out", w_sgu_out, BF16, True),
                ("w_conv_out", w_conv_out, BF16, True), ("w_pool", w_pool.reshape(L, -1, PG), BF16, False),
                ("conv_w", conv_w, F32, True), ("w_o", w_o, BF16, False)],
               [("w_up", w_up, BF16, True), ("ffn_conv_w", ffn_conv_w, F32, True)],
               [("w_down", w_down, BF16, False)]]
        ks = []
        for gi, members in enumerate(grp):
            if l == 0 and gi == 0:
                continue
            k = 4 * l + gi
            bufs = [_cast_place(a, l, slot, dt, f"place_{n}_{l}", deps, bc) for n, a, dt, bc in members]
            stage1[k] = (bufs,) + _copies_gather_chips(len(members))
            group_sizes[k] = len(members)
            ks.append(k)
        return ks

    def start_groups(ks, extra, name, deps=()):
        st, tok = _split_start(extra + [stage1[k] for k in ks], name, deps)
        for k, s in zip(ks, st[len(extra):]):
            started1[k] = s
        return st[:len(extra)], tok

    rest, tok_start0 = start_groups(place_groups(0, [tok0, h1_first]),
                                    [([win0],) + _copies_gather_chips(1, cols) for cols in chunk_cols[1:]], "ag_start_0")
    if nq > 1:
        win0 = rest[0][2][0]
    chunk_sems = [first[0][:2]] + [r[:2] for r in rest]
    later_groups = [place_groups(l, [tok_start0]) for l in range(1, L)]
    placed_later = [b for ks in later_groups for k in ks for b in stage1[k][0]]
    started2 = {}

    def gather_forward(k, after):
        n = group_sizes[k]
        bufs = _split_wait(started1[k], stage1[k][1], after, f"ag_wait1_{k}")
        st, tok = _split_start([(bufs,) + _copies_gather_forward(n)], f"ag_fwd_{k}")
        started2[k] = st[0]
        return tok

    def gather_done(k, after):
        return _split_wait(started2.pop(k), _copies_gather_forward(group_sizes[k])[0], after, f"ag_wait2_{k}")

    W, saved = [], []
    xc = x0
    for l in range(L):
        w = dict(
            sguw=sgu_w[l].astype(BF16),
            bexp=jnp.broadcast_to(sgu_b[l].T[:, :, None],
                                  (sgu_b.shape[2], sgu_b.shape[1], sgu_norm_g.shape[1] // sgu_b.shape[2])),
            g1=norm1_g[l][None], g2=norm2_g[l][None], gs=sgu_norm_g[l][None], scale=pool_scale[l][None])
        ka, kb, kc, kd = 4 * l, 4 * l + 1, 4 * l + 2, 4 * l + 3
        h1 = h1_first if l == 0 else _rms_fwd(xc, w["g1"], f"rms1_fwd_{l}")
        if l == 0:
            P, after = None, [h1] + placed_later
            for q, cols in enumerate(chunk_cols):
                (win0,) = _split_wait(chunk_sems[q] + ([win0],), _copies_gather_chips(1, cols)[0], after,
                                      f"ag_wait1_0_{q}")
                st, tok = _split_start([([win0],) + _copies_gather_forward(1, cols)], f"ag_fwd_0_{q}")
                (win0,) = _split_wait(st[0], _copies_gather_forward(1, cols)[0], after, f"ag_wait2_0_{q}")
                P = _mm(h1, win0, mode="nn", out_dtype=BF16, name=f"proj_in_0_{q}", tm=tT2, tn=cols[1], tk=D,
                        cols=(q, nq), prev=P, deps=[tok])
                after = P
            w["win"] = win0
        else:
            (w["win"],) = gather_done(ka, h1)
            tok = gather_forward(kb, h1)
            P = _mm(h1, w["win"], mode="nn", out_dtype=BF16, name=f"proj_in_{l}", tm=tT2, tn=1024, tk=D, deps=[tok])
        if l == 0:
            gather_forward(kb, P)
        w["wa"], w["wb"], w["wc"], wp, w["cw"], wo = gather_done(kb, P)
        w.update(wpool=wp.reshape(N_DEV, G, -1, PG).transpose(1, 0, 2, 3).reshape(G, PG, PG),
                 wo=wo.reshape(-1, wo.shape[2]))
        ya = _pool_fwd(P, w["wpool"], w["scale"], f"pool_fwd_{l}")
        yb = _sgu_fwd(P, w["sguw"], w["bexp"], w["gs"], f"sgu_fwd_{l}")
        yc = _conv_fwd(P, w["cw"], f"conv_fwd_{l}")
        oa, ob, oc, M = _merge_fwd(P, (ya, yb, yc), (w["wa"], w["wb"], w["wc"]), f"merge_fwd_{l}")
        tok = gather_forward(kc, M)
        if l == 0:
            for l2 in range(1, L):
                start_groups(later_groups[l2 - 1], [], f"ag_start_{l2}", [tok])
        x1, h2 = _mm_res_rms(M, w["wo"], xc, w["g2"], f"proj_o_{l}", deps=[tok])
        w["wup"], w["fw"] = gather_done(kc, h2)
        tok = gather_forward(kd, h2)
        U = _mm(h2, w["wup"], mode="nn", out_dtype=BF16, name=f"proj_up_{l}", tm=tT2, tn=tF2, tk=D, deps=[tok])
        act = _glu_fwd(U, w["fw"], f"glu_fwd_{l}")
        (wdown,) = gather_done(kd, act)
        w["wdown"] = wdown.reshape(-1, wdown.shape[2])
        deps = [gather_forward(4 * (l + 1), act)] if l + 1 < L else []
        x2 = _mm(act, w["wdown"], mode="nn", out_dtype=F32, name=f"proj_down_{l}", tm=tT, tn=512, tk=F, res=x1,
                 deps=deps)
        if deps:
            tok = deps[0]
        W.append(w)
        saved.append(dict(x0=xc, h1=h1, P=P, ya=ya, yb=yb, yc=yc, oa=oa, ob=ob, oc=oc, M=M, x1=x1, h2=h2, U=U, act=act))
        xc = x2

    dx, dxb, d_final_g, loss_part = _loss_head(xc, tgt, final_g[None], "loss_head")
    loss = lax.psum(loss_part[0, 0], ("x", "y", "c"))

    wmv = dict(w_in=(w_in, m_w_in, v_w_in), w_up=(w_up, m_w_up, v_w_up), w_o=(w_o, m_w_o, v_w_o),
               w_down=(w_down, m_w_down, v_w_down), w_pool_out=(w_pool_out, m_w_pool_out, v_w_pool_out),
               w_sgu_out=(w_sgu_out, m_w_sgu_out, v_w_sgu_out), w_conv_out=(w_conv_out, m_w_conv_out, v_w_conv_out),
               w_pool=(w_pool, m_w_pool, v_w_pool))
    adam_out = {}

    def rs_begin(tag, layer, names, grads):
        n = len(grads)
        slice_shape = lambda g: g.shape[1:] if len(g.shape) == 3 else (g.shape[0], g.shape[1] // N_DEV)
        lands = [lax.empty((4,) + slice_shape(g), BF16) for g in grads]
        st, tok = _split_start([(list(grads) + lands,) + _copies_rs_sibling(n)], f"rs_sib_{tag}")
        return dict(tag=tag, layer=layer, names=names, n=n, st=st[0]), tok

    def rs_to_chips(state, after):
        n, tag = state["n"], state["tag"]
        bufs = _split_wait(state["st"], _copies_rs_sibling(n)[0], after, f"rs_sibw_{tag}")
        grads, sib = bufs[:n], bufs[n:]
        pairs = [_pair_sum(g, r, pos, f"pair_{tag}_{i}") for i, (g, r) in enumerate(zip(grads, sib))]
        lands = [lax.empty(p.shape, BF16) for p in pairs]
        st, tok = _split_start([(pairs + lands,) + _copies_rs_chips(n)], f"rs_chips_{tag}")
        state.update(st=st[0], grads=grads, sib=sib)
        return tok

    def rs_finish(state, after):
        n, tag, layer = state["n"], state["tag"], state["layer"]
        bufs = _split_wait(state["st"], _copies_rs_chips(n)[0], after, f"rs_chipsw_{tag}")
        for name, g, sb, recv in zip(state["names"], state["grads"], state["sib"], bufs[n:]):
            wk, mk, vk = (a.reshape(L, -1, a.shape[-1]) for a in wmv[name])
            adam_out[name] = _adamw_big(g, sb, recv, wk, mk, vk, pos, layer, adam_out.get(name),
                                        f"adamw_{name}_{layer}")

    small = [None] * L
    pending = []
    deps = []
    for l in reversed(range(L)):
        w, s = W[l], saved[l]
        dact = _mm(dxb, w["wdown"], mode="nt", out_dtype=BF16, name=f"d_act_{l}", tm=tT2, tn=tF, tk=D, deps=deps)
        g_down = _mm(s["act"], dxb, mode="tn", out_dtype=BF16, name=f"g_down_{l}", tm=tF, tn=1024, tk=T)
        dug, duv, dfwg, dfwv = _glu_bwd(s["U"], dact, w["fw"], f"glu_bwd_{l}")
        dh2 = _mm(dug, w["wup"], mode="nt", out_dtype=F32, name=f"d_h2_{l}", tm=tT, tn=1024, tk=kF, a2=duv)
        g_up = _mm(s["h2"], dug, mode="tn", out_dtype=BF16, name=f"g_upa_{l}", tm=1024, tn=tF, tk=T, out_n=2 * F)
        g_up = _mm(s["h2"], duv, mode="tn", out_dtype=BF16, name=f"g_upb_{l}", tm=1024, tn=tF, tk=T, out_n=2 * F,
                   out_off=F // tF, prev=g_up)
        ra, tok = rs_begin(f"a{l}", l, ["w_down", "w_up"], [g_down.reshape(N_DEV, -1, g_down.shape[1]), g_up])
        if pending:
            rs_finish(pending.pop(0), g_up)
        dx1, dx1b, dg2 = _rms_bwd(dh2, s["x1"], w["g2"], dx, f"rms2_bwd_{l}")
        dM = _mm(dx1b, w["wo"], mode="nt", out_dtype=BF16, name=f"d_m_{l}", tm=tT, tn=1024, tk=D, deps=[tok])
        g_o = _mm(s["M"], dx1b, mode="tn", out_dtype=BF16, name=f"g_o_{l}", tm=1024, tn=1024, tk=T)
        doa, dob, doc, dPg = _merge_bwd(s["P"], dM, (s["oa"], s["ob"], s["oc"]), f"merge_bwd_{l}")
        tok = rs_to_chips(ra, dPg)
        if pending:
            rs_finish(pending.pop(0), dPg)
        dya = _mm(doa, w["wa"], mode="nt", out_dtype=BF16, name=f"d_ya_{l}", tm=tT, tn=1024, tk=D, deps=[tok])
        dyb = _mm(dob, w["wb"], mode="nt", out_dtype=BF16, name=f"d_yb_{l}", tm=tT, tn=1024, tk=D)
        dyc = _mm(doc, w["wc"], mode="nt", out_dtype=BF16, name=f"d_yc_{l}", tm=tT, tn=1024, tk=D)
        g_a = _mm(s["ya"], doa, mode="tn", out_dtype=BF16, name=f"g_a_{l}", tm=1024, tn=1024, tk=T)
        g_b = _mm(s["yb"], dob, mode="tn", out_dtype=BF16, name=f"g_b_{l}", tm=1024, tn=1024, tk=T)
        g_c = _mm(s["yc"], doc, mode="tn", out_dtype=BF16, name=f"g_c_{l}", tm=1024, tn=1024, tk=T)
        da, g_pool, dscale = _pool_bwd(s["P"], dya, w["wpool"], w["scale"], f"pool_bwd_{l}")
        g_pool_s = g_pool.reshape(G, N_DEV, -1, PG).transpose(1, 0, 2, 3).reshape(N_DEV, -1, PG).astype(BF16)
        rb, tok = rs_begin(f"b{l}", l, ["w_o", "w_pool_out", "w_sgu_out", "w_conv_out", "w_pool"],
                           [g_o.reshape(N_DEV, -1, g_o.shape[1]), g_a, g_b, g_c, g_pool_s])
        if pending:
            rs_finish(pending.pop(0), da)
        dPl, g_sguw, db_exp, dgs = _sgu_bwd(s["P"], dyb, da, w["sguw"], w["bexp"], w["gs"], f"sgu_bwd_{l}")
        dPl, dcw = _conv_bwd(s["P"], dyc, w["cw"], dPl, f"conv_bwd_{l}")
        tok2 = rs_to_chips(rb, dPl)
        half = dPl.shape[1]
        g_in = _mm(s["h1"], dPl, mode="tn", out_dtype=BF16, name=f"g_ina_{l}", tm=1024, tn=1024, tk=T, out_n=2 * half,
                   deps=[tok, tok2])
        g_in = _mm(s["h1"], dPg, mode="tn", out_dtype=BF16, name=f"g_inb_{l}", tm=1024, tn=1024, tk=T, out_n=2 * half,
                   out_off=half // 1024, prev=g_in)
        rc, tok = rs_begin(f"c{l}", l, ["w_in"], [g_in])
        if l > 0:
            dh1 = _mm(dPl, w["win"], mode="nt", out_dtype=F32, name=f"d_h1_{l}", tm=tT, tn=1024, tk=D, a2=dPg,
                      deps=[tok])
            tok = rs_to_chips(rc, dh1)
        else:
            dh1 = _mm(dPl, w["win"], mode="nt", out_dtype=F32, name=f"d_h1a_{l}", tm=tT, tn=1024, tk=D, deps=[tok])
            tok = rs_to_chips(rc, dh1)
            dh1 = _mm(dPg, w["win"], mode="nt", out_dtype=F32, name=f"d_h1b_{l}", tm=tT, tn=1024, tk=D,
                      b_koff=half // D, res=dh1, deps=[tok])
        dx, dxb, dg1 = _rms_bwd(dh1, s["x0"], w["g1"], dx1, f"rms1_bwd_{l}", deps=[tok])
        deps = []
        pending = [ra, rb, rc]
        small[l] = dict(norm1_g=dg1[0], pool_scale=dscale[0], sgu_norm_g=dgs[0], sgu_w=g_sguw,
                        sgu_b=db_exp[:, :, 0].T, conv_w=dcw[:3], norm2_g=dg2[0],
                        ffn_conv_w=jnp.concatenate([dfwg[:3], dfwv[:3]], axis=1))
    grad_x = dx[None]

    snames = ["norm1_g", "pool_scale", "sgu_norm_g", "sgu_w", "sgu_b", "conv_w", "norm2_g", "ffn_conv_w"]
    sparts = [jnp.stack([small[l][n] for l in range(L)]) for n in snames] + [d_final_g[0]]
    snames = snames + ["final_g"]
    packed = _pack(sparts)
    sbuf = _cast_place(packed[None], 0, slot, F32, "place_small")
    st, tok = _split_start([([sbuf],) + _copies_gather_chips(1)], "ag_small_start")
    rs_finish(pending[0], tok)
    rs_finish(pending[1], tok)
    bufs = _split_wait(st[0], _copies_gather_chips(1)[0], [o[0] for o in adam_out.values()], "ag_small_wait1")
    st, tok = _split_start([(bufs,) + _copies_gather_forward(1)], "ag_small_fwd")
    (gathered_small,) = _split_wait(st[0], _copies_gather_forward(1)[0], tok, "ag_small_wait2")
    total = _sum_slots(gathered_small, "sum_small")
    sgrads = dict(zip(snames, _unpack(total, [p.shape for p in sparts])))
    for n, width in (("conv_w", conv_w.shape[2]), ("ffn_conv_w", ffn_conv_w.shape[2])):
        sgrads[n] = lax.dynamic_slice_in_dim(sgrads[n], me * width, width, axis=2)
    sw = dict(norm1_g=(norm1_g, m_norm1_g, v_norm1_g), pool_scale=(pool_scale, m_pool_scale, v_pool_scale),
              sgu_norm_g=(sgu_norm_g, m_sgu_norm_g, v_sgu_norm_g), sgu_w=(sgu_w, m_sgu_w, v_sgu_w),
              sgu_b=(sgu_b, m_sgu_b, v_sgu_b), conv_w=(conv_w, m_conv_w, v_conv_w),
              norm2_g=(norm2_g, m_norm2_g, v_norm2_g), ffn_conv_w=(ffn_conv_w, m_ffn_conv_w, v_ffn_conv_w),
              final_g=(final_g, m_final_g, v_final_g))
    shapes = [sw[n][0].shape for n in snames]
    upd = _adamw_small(_pack([sw[n][0] for n in snames]), _pack([sgrads[n] for n in snames]),
                       _pack([sw[n][1] for n in snames]), _pack([sw[n][2] for n in snames]), "adamw_small")
    sdelta, sm, sv = (dict(zip(snames, _unpack(u, shapes))) for u in upd)
    res = {n: [sgrads[n], sdelta[n], sm[n], sv[n]] for n in snames}

    rs_finish(pending[2], [upd[0]] + [o[0] for o in adam_out.values()])
    for n, outs in adam_out.items():
        res[n] = [o.reshape(wmv[n][0].shape) for o in outs]

    order = ["norm1_g", "w_in", "w_pool", "pool_scale", "sgu_norm_g", "sgu_w", "sgu_b", "conv_w", "w_pool_out",
             "w_sgu_out", "w_conv_out", "w_o", "norm2_g", "w_up", "ffn_conv_w", "w_down", "final_g"]
    return (loss, grad_x) + tuple(res[n][k] for k in range(4) for n in order)
```

```python
import functools

import jax
import jax.numpy as jnp
from jax import lax
from jax.experimental import pallas as pl
from jax.experimental.pallas import tpu as pltpu

BF16 = jnp.bfloat16
F32 = jnp.float32
EPS = 1e-6
MIB = 1024 * 1024
V7X_VMEM_BYTES = 64 * MIB
VMEM_LIMIT = 56 * MIB
LANE = 128
N_DEV = 8


def _params(sem, **kw):
    return pltpu.CompilerParams(dimension_semantics=sem, vmem_limit_bytes=VMEM_LIMIT, **kw)


def _pick(n, cands):
    for c in cands:
        if n % c == 0:
            return c
    return n


_DIMS = {"nn": (((1,), (0,)), ((), ())), "nt": (((1,), (1,)), ((), ())), "tn": (((0,), (0,)), ((), ()))}


def _mm(a, b, *, mode, out_dtype, name, tm, tn, tk, res=None, prev=None, deps=(), cols=(0, 1), b_koff=0,
        out_off=0, out_n=None, a2=None):
    if mode == "nn":
        M, Kc = a.shape
        nj = (b.shape[1] // tn - cols[0] + cols[1] - 1) // cols[1]
        out_w = b.shape[1]
    elif mode == "nt":
        M, Kc = a.shape
        nj = b.shape[0] // tn
        out_w = b.shape[0]
    else:
        Kc, M = a.shape
        nj = b.shape[1] // tn
        out_w = out_n or b.shape[1]
    nk1 = Kc // tk
    nk = nk1 + (a2.shape[1] // tk if a2 is not None else 0)
    assert M % tm == 0 and Kc % tk == 0 and (a2 is None or mode == "nt"), (name, M, Kc, tm, tk)
    off, stride = cols
    if mode == "nn":
        a_spec = pl.BlockSpec((tm, tk), lambda i, j, k: (i, k))
        b_spec = pl.BlockSpec((tk, tn), lambda i, j, k: (k, off + j * stride))
        o_spec = pl.BlockSpec((tm, tn), lambda i, j, k: (i, off + j * stride))
    elif mode == "nt":
        a_spec = pl.BlockSpec((tm, tk), lambda i, j, k: (i, jnp.minimum(k, nk1 - 1)))
        a2_spec = pl.BlockSpec((tm, tk), lambda i, j, k: (i, jnp.maximum(k - nk1, 0)))
        b_spec = pl.BlockSpec((tn, tk), lambda i, j, k: (j, b_koff + k))
        o_spec = pl.BlockSpec((tm, tn), lambda i, j, k: (i, j))
    else:
        a_spec = pl.BlockSpec((tk, tm), lambda i, j, k: (k, i))
        b_spec = pl.BlockSpec((tk, tn), lambda i, j, k: (k, j))
        o_spec = pl.BlockSpec((tm, tn), lambda i, j, k: (i, out_off + j))
    dims = _DIMS[mode]
    has_res = res is not None
    has_prev = prev is not None
    n_in = 2 + has_res + has_prev + len(deps) + (a2 is not None)

    def body(*refs):
        a_ref, b_ref = refs[0], refs[1]
        o_ref = refs[n_in]

        def finish(acc):
            if has_res:
                acc = acc + refs[2][...]
            o_ref[...] = acc.astype(o_ref.dtype)

        if nk == 1:
            finish(lax.dot_general(a_ref[...], b_ref[...], dims, preferred_element_type=F32))
        else:
            acc_ref = refs[n_in + 1]
            k = pl.program_id(2)

            @pl.when(k == 0)
            def _():
                acc_ref[...] = jnp.zeros(acc_ref.shape, F32)

            if a2 is None:
                acc_ref[...] += lax.dot_general(a_ref[...], b_ref[...], dims, preferred_element_type=F32)
            else:
                @pl.when(k < nk1)
                def _():
                    acc_ref[...] += lax.dot_general(a_ref[...], b_ref[...], dims, preferred_element_type=F32)

                @pl.when(k >= nk1)
                def _():
                    acc_ref[...] += lax.dot_general(refs[n_in - 1][...], b_ref[...], dims, preferred_element_type=F32)

            @pl.when(k == nk - 1)
            def _():
                finish(acc_ref[...])

    in_specs = [a_spec, b_spec]
    args = [a, b]
    if has_res:
        in_specs.append(pl.BlockSpec((tm, tn), lambda i, j, k: (i, j)))
        args.append(res)
    aliases = {}
    if has_prev:
        aliases = {len(args): 0}
        in_specs.append(pl.BlockSpec(memory_space=pl.ANY))
        args.append(prev)
    for d in deps:
        in_specs.append(pl.BlockSpec(memory_space=pl.ANY))
        args.append(d)
    if a2 is not None:
        in_specs.append(a2_spec)
        args.append(a2)
    return pl.pallas_call(
        body,
        name=name,
        grid=(M // tm, nj, nk),
        in_specs=in_specs,
        out_specs=o_spec,
        out_shape=jax.ShapeDtypeStruct((M, out_w), out_dtype),
        scratch_shapes=[pltpu.VMEM((tm, tn), F32)] if nk > 1 else [],
        input_output_aliases=aliases,
        compiler_params=_params(("parallel", "parallel", "arbitrary")),
    )(*args)


def _mm_res_rms(a, b, res, g, name, deps=()):
    M, K = a.shape
    N = b.shape[1]
    tm = _pick(M, (512, 256, 128))

    def body(a_ref, b_ref, res_ref, g_ref, *rest):
        x_ref, h_ref = rest[-2:]
        xv = res_ref[...] + jnp.dot(a_ref[...], b_ref[...], preferred_element_type=F32)
        x_ref[...] = xv
        r = lax.rsqrt(jnp.mean(xv * xv, axis=-1, keepdims=True) + EPS)
        h_ref[...] = (xv * r * g_ref[...]).astype(BF16)

    row = pl.BlockSpec((tm, N), lambda i: (i, 0))
    return pl.pallas_call(
        body, name=name, grid=(M // tm,),
        in_specs=[pl.BlockSpec((tm, K), lambda i: (i, 0)), pl.BlockSpec((K, N), lambda i: (0, 0)), row,
                  pl.BlockSpec((1, N), lambda i: (0, 0))] + [pl.BlockSpec(memory_space=pl.ANY)] * len(deps),
        out_specs=[row, row],
        out_shape=[jax.ShapeDtypeStruct((M, N), F32), jax.ShapeDtypeStruct((M, N), BF16)],
        compiler_params=_params(("parallel",)),
    )(a, b, res, g, *deps)


HALO = 16
CH = 256


def _fill_pad(pad_ref, chunk_fn, T):
    z = jnp.zeros((HALO, pad_ref.shape[1]), F32)
    pad_ref[pl.ds(0, HALO), :] = z
    pad_ref[pl.ds(HALO + T, HALO), :] = z

    def body(c, carry):
        r0 = pl.multiple_of(c * CH, CH)
        pad_ref[pl.ds(HALO + r0, CH), :] = chunk_fn(r0)
        return carry

    lax.fori_loop(0, T // CH, body, 0)


def _ext(pad_ref, r0):
    return pad_ref[pl.ds(r0, CH + 2 * HALO), :]


def _tap(ext, o):
    if o == 0:
        return ext[HALO:HALO + CH]
    return pltpu.roll(ext, (-o) % ext.shape[0], axis=0)[HALO:HALO + CH]


def _chunks(T, fn, init=0):
    def body(c, carry):
        return fn(pl.multiple_of(c * CH, CH), carry)

    return lax.fori_loop(0, T // CH, body, init)


def _conv3(ext, w):
    return _tap(ext, -1) * w[0:1] + _tap(ext, 0) * w[1:2] + _tap(ext, 1) * w[2:3]


def _conv3_t(ext, w):
    return _tap(ext, 1) * w[0:1] + _tap(ext, 0) * w[1:2] + _tap(ext, -1) * w[2:3]


def _sigmoid(x):
    return 0.5 * (jnp.tanh(0.5 * x) + 1.0)


def _silu(x):
    h = 0.5 * x
    return h * (jnp.tanh(h) + 1.0)


_GELU_C = 0.7978845608028654


def _gelu(x):
    return 0.5 * x * (1.0 + jnp.tanh(_GELU_C * (x + 0.044715 * (x * x * x))))


def _gelu_and_grad(x):
    x2 = x * x
    t = jnp.tanh(_GELU_C * (x + 0.044715 * (x2 * x)))
    return 0.5 * x * (1.0 + t), 0.5 * (1.0 + t) + 0.5 * x * (1.0 - t * t) * (_GELU_C * (1.0 + 3.0 * 0.044715 * x2))


def _row_tile(T):
    return _pick(T, (256, 128))


def _norm_tile(T):
    return _pick(T, (512, 256, 128))


def _rms_fwd(x, g, name):
    T, D = x.shape
    tm = _norm_tile(T)

    def body(x_ref, g_ref, h_ref):
        xv = x_ref[...]
        r = lax.rsqrt(jnp.mean(xv * xv, axis=-1, keepdims=True) + EPS)
        h_ref[...] = (xv * r * g_ref[...]).astype(BF16)

    return pl.pallas_call(
        body, name=name, grid=(T // tm,),
        in_specs=[pl.BlockSpec((tm, D), lambda i: (i, 0)), pl.BlockSpec((1, D), lambda i: (0, 0))],
        out_specs=pl.BlockSpec((tm, D), lambda i: (i, 0)),
        out_shape=jax.ShapeDtypeStruct((T, D), BF16),
        compiler_params=_params(("parallel",)),
    )(x, g)


def _rms_bwd(dh, x, g, dres, name, deps=()):
    T, D = x.shape
    tm = _norm_tile(T)

    def body(dh_ref, x_ref, g_ref, dres_ref, *rest):
        dx_ref, dxb_ref, dg_ref = rest[len(deps):]
        i = pl.program_id(0)
        xv = x_ref[...]
        r = lax.rsqrt(jnp.mean(xv * xv, axis=-1, keepdims=True) + EPS)
        n = xv * r
        dh_v = dh_ref[...].astype(F32)
        dn = dh_v * g_ref[...]
        dx = dres_ref[...] + r * (dn - n * jnp.mean(dn * n, axis=-1, keepdims=True))
        dx_ref[...] = dx
        dxb_ref[...] = dx.astype(BF16)
        dg = jnp.sum(dh_v * n, axis=0, keepdims=True)

        @pl.when(i == 0)
        def _():
            dg_ref[...] = dg

        @pl.when(i > 0)
        def _():
            dg_ref[...] += dg

    row = pl.BlockSpec((tm, D), lambda i: (i, 0))
    vec = pl.BlockSpec((1, D), lambda i: (0, 0))
    return pl.pallas_call(
        body, name=name, grid=(T // tm,),
        in_specs=[row, row, vec, row] + [pl.BlockSpec(memory_space=pl.ANY)] * len(deps),
        out_specs=[row, row, vec],
        out_shape=[jax.ShapeDtypeStruct((T, D), F32), jax.ShapeDtypeStruct((T, D), BF16),
                   jax.ShapeDtypeStruct((1, D), F32)],
        compiler_params=_params(("arbitrary",)),
    )(dh, x, g, dres, *deps)


def _loss_head(x, tgt, g, name):
    T, D = x.shape
    tm = _norm_tile(T)

    def body(x_ref, t_ref, g_ref, dx_ref, dxb_ref, dg_ref, l_ref):
        i = pl.program_id(0)
        xv = x_ref[...]
        gv = g_ref[...]
        r = lax.rsqrt(jnp.mean(xv * xv, axis=-1, keepdims=True) + EPS)
        n = xv * r
        e = n * gv - t_ref[...]
        dy = e * (1.0 / D)
        dn = dy * gv
        dx = r * (dn - n * jnp.mean(dn * n, axis=-1, keepdims=True))
        dx_ref[...] = dx
        dxb_ref[...] = dx.astype(BF16)
        dg = jnp.sum(dy * n, axis=0, keepdims=True)
        per_tok = jnp.mean(e * e, axis=-1, keepdims=True)
        lv = jnp.broadcast_to(0.5 * jnp.sum(per_tok, axis=0, keepdims=True), (1, LANE))

        @pl.when(i == 0)
        def _():
            dg_ref[...] = dg
            l_ref[...] = lv

        @pl.when(i > 0)
        def _():
            dg_ref[...] += dg
            l_ref[...] += lv

    row = pl.BlockSpec((tm, D), lambda i: (i, 0))
    vec = pl.BlockSpec((1, D), lambda i: (0, 0))
    return pl.pallas_call(
        body, name=name, grid=(T // tm,),
        in_specs=[row, row, vec],
        out_specs=[row, row, vec, pl.BlockSpec((1, LANE), lambda i: (0, 0))],
        out_shape=[jax.ShapeDtypeStruct((T, D), F32), jax.ShapeDtypeStruct((T, D), BF16),
                   jax.ShapeDtypeStruct((1, D), F32), jax.ShapeDtypeStruct((1, LANE), F32)],
        compiler_params=_params(("arbitrary",)),
    )(x, tgt, g)


POOL_WINDOWS = (2, 4, 8, 16)


def _pool_cnt(r0, w, T, shape):
    t = r0 + lax.broadcasted_iota(jnp.int32, shape, 0)
    lo = jnp.maximum(t - w // 2, 0)
    hi = jnp.minimum(t + (w - w // 2 - 1), T - 1)
    return (hi - lo + 1).astype(F32)


def _window_sum(ext, w, mirrored=False):
    n = ext.shape[0]
    cur = ext + pltpu.roll(ext, n - 1 if mirrored else 1, axis=0)
    width, sh = 2, 1
    while width < w:
        cur = pltpu.roll(cur, sh, axis=0) + pltpu.roll(cur, n - sh, axis=0)
        width, sh = 2 * width, 2 * sh
    return cur[HALO:HALO + CH]


def _pooled(pad_ref, r0, w, T):
    ext = _ext(pad_ref, r0)
    cur = ext[HALO:HALO + CH]
    return _window_sum(ext, w) / _pool_cnt(r0, w, T, cur.shape) - cur


def _pool_fwd(P, w_pool, scale, name):
    T = P.shape[0]
    G, PG, _ = w_pool.shape

    def body(a_ref, w_ref, s_ref, o_ref, pad_ref):
        g = pl.program_id(0)
        _fill_pad(pad_ref, lambda r0: a_ref[pl.ds(r0, CH), :].astype(F32), T)
        for gi, w in enumerate(POOL_WINDOWS):
            @pl.when(g == gi)
            def _(w=w):
                def chunk(r0, carry):
                    pooled = _pooled(pad_ref, r0, w, T).astype(BF16)
                    y = jnp.dot(pooled, w_ref[...], preferred_element_type=F32) * s_ref[...]
                    o_ref[pl.ds(r0, CH), :] = y.astype(BF16)
                    return carry

                _chunks(T, chunk)

    return pl.pallas_call(
        body, name=name, grid=(G,),
        in_specs=[pl.BlockSpec((T, PG), lambda g: (0, g)),
                  pl.BlockSpec((None, PG, PG), lambda g: (g, 0, 0)),
                  pl.BlockSpec((1, PG), lambda g: (0, g))],
        out_specs=pl.BlockSpec((T, PG), lambda g: (0, g)),
        out_shape=jax.ShapeDtypeStruct((T, G * PG), BF16),
        scratch_shapes=[pltpu.VMEM((T + 2 * HALO, PG), F32)],
        compiler_params=_params(("parallel",)),
    )(P, w_pool, scale)


def _pool_bwd(P, dy, w_pool, scale, name):
    T = P.shape[0]
    G, PG, _ = w_pool.shape

    def body(a_ref, dy_ref, w_ref, s_ref, da_ref, dw_ref, ds_ref, pad_ref, gpad_ref):
        g = pl.program_id(0)
        gpad_ref[pl.ds(0, HALO), :] = jnp.zeros((HALO, PG), F32)
        gpad_ref[pl.ds(HALO + T, HALO), :] = jnp.zeros((HALO, PG), F32)
        _fill_pad(pad_ref, lambda r0: a_ref[pl.ds(r0, CH), :].astype(F32), T)
        for gi, w in enumerate(POOL_WINDOWS):
            @pl.when(g == gi)
            def _(w=w):
                def chunk1(r0, carry):
                    dw, ds = carry
                    pooled = _pooled(pad_ref, r0, w, T).astype(BF16)
                    ypre = jnp.dot(pooled, w_ref[...], preferred_element_type=F32)
                    dyv = dy_ref[pl.ds(r0, CH), :].astype(F32)
                    ds = ds + jnp.sum(dyv * ypre, axis=0, keepdims=True)
                    dyp = (dyv * s_ref[...]).astype(BF16)
                    dw = dw + lax.dot_general(pooled, dyp, _DIMS["tn"], preferred_element_type=F32)
                    dpool = lax.dot_general(dyp, w_ref[...], _DIMS["nt"], preferred_element_type=F32)
                    gpad_ref[pl.ds(HALO + r0, CH), :] = dpool / _pool_cnt(r0, w, T, dpool.shape)
                    return dw, ds

                dw, ds = _chunks(T, chunk1, (jnp.zeros((PG, PG), F32), jnp.zeros((1, PG), F32)))
                dw_ref[...] = dw
                ds_ref[...] = ds

                def chunk2(r0, carry):
                    ext = _ext(gpad_ref, r0)
                    cur = ext[HALO:HALO + CH]
                    acc = _window_sum(ext, w, mirrored=True)
                    da_ref[pl.ds(r0, CH), :] = (acc - cur * _pool_cnt(r0, w, T, cur.shape)).astype(BF16)
                    return carry

                _chunks(T, chunk2)

    col = pl.BlockSpec((T, PG), lambda g: (0, g))
    return pl.pallas_call(
        body, name=name, grid=(G,),
        in_specs=[col, col, pl.BlockSpec((None, PG, PG), lambda g: (g, 0, 0)), pl.BlockSpec((1, PG), lambda g: (0, g))],
        out_specs=[col, pl.BlockSpec((None, PG, PG), lambda g: (g, 0, 0)), pl.BlockSpec((1, PG), lambda g: (0, g))],
        out_shape=[jax.ShapeDtypeStruct((T, G * PG), BF16), jax.ShapeDtypeStruct((G, PG, PG), F32),
                   jax.ShapeDtypeStruct((1, G * PG), F32)],
        scratch_shapes=[pltpu.VMEM((T + 2 * HALO, PG), F32), pltpu.VMEM((T + 2 * HALO, PG), F32)],
        compiler_params=_params(("parallel",)),
    )(P, dy, w_pool, scale)


SGU_CHUNK = 128


def _sgu_common(u_ref, v_ref, gs_ref):
    up = u_ref[...].astype(F32)
    vp = v_ref[...].astype(F32)
    gv = _gelu(vp)
    rv = lax.rsqrt(jnp.mean(gv * gv, axis=-1, keepdims=True) + EPS)
    nrm = gv * rv
    return up, vp, nrm, rv, (nrm * gs_ref[...]).astype(BF16)


def _sgu_fwd(P, sgu_w, b_exp, gs, name):
    T = P.shape[0]
    G, _, SG = b_exp.shape
    DP = G * SG
    tm = _row_tile(T)

    def body(u_ref, v_ref, w_ref, b_ref, gs_ref, o_ref):
        up, _, _, _, vn = _sgu_common(u_ref, v_ref, gs_ref)
        gu = _gelu(up)
        for n in range(tm // SGU_CHUNK):
            rows = slice(n * SGU_CHUNK, (n + 1) * SGU_CHUNK)
            for g in range(G):
                cols = slice(g * SG, (g + 1) * SG)
                z = jnp.dot(w_ref[g], vn[rows, cols], preferred_element_type=F32) + b_ref[g]
                o_ref[rows, cols] = (gu[rows, cols] * z).astype(BF16)

    return pl.pallas_call(
        body, name=name, grid=(T // tm,),
        in_specs=[pl.BlockSpec((tm, DP), lambda i: (i, 1)), pl.BlockSpec((tm, DP), lambda i: (i, 2)),
                  pl.BlockSpec(sgu_w.shape, lambda i: (0, 0, 0)), pl.BlockSpec(b_exp.shape, lambda i: (0, 0, 0)),
                  pl.BlockSpec((1, DP), lambda i: (0, 0))],
        out_specs=pl.BlockSpec((tm, DP), lambda i: (i, 0)),
        out_shape=jax.ShapeDtypeStruct((T, DP), BF16),
        compiler_params=_params(("parallel",)),
    )(P, P, sgu_w, b_exp, gs)


def _sgu_bwd(P, dy, da, sgu_w, b_exp, gs, name):
    T = P.shape[0]
    G, _, SG = b_exp.shape
    DP = G * SG
    tm = _row_tile(T)

    def body(u_ref, v_ref, dy_ref, da_ref, w_ref, b_ref, gs_ref, o_ref, dw_ref, db_ref, dgs_ref, dzs_ref,
             gv_ref, gg_ref, dvn_ref):
        i = pl.program_id(0)
        o_ref[:, 0:DP] = da_ref[...]

        @pl.when(i == 0)
        def _():
            dw_ref[...] = jnp.zeros(dw_ref.shape, F32)
            dzs_ref[...] = jnp.zeros(dzs_ref.shape, F32)
            dgs_ref[...] = jnp.zeros(dgs_ref.shape, F32)

        blocks = [slice(g * SG, (g + 1) * SG) for g in range(G)]
        for n in range(tm // SGU_CHUNK):
            rows = slice(n * SGU_CHUNK, (n + 1) * SGU_CHUNK)
            ss = jnp.zeros((SGU_CHUNK, 1), F32)
            for cols in blocks:
                gv, gg = _gelu_and_grad(v_ref[rows, cols].astype(F32))
                gv_ref[rows, cols] = gv
                gg_ref[rows, cols] = gg
                ss = ss + jnp.sum(gv * gv, axis=1, keepdims=True)
            rv = lax.rsqrt(ss * (1.0 / DP) + EPS)
            acc = jnp.zeros((SGU_CHUNK, 1), F32)
            for g, cols in enumerate(blocks):
                gsg = gs_ref[:, cols]
                nr = gv_ref[rows, cols] * rv
                vng = (nr * gsg).astype(BF16)
                gu, gug = _gelu_and_grad(u_ref[rows, cols].astype(F32))
                z = jnp.dot(w_ref[g], vng, preferred_element_type=F32) + b_ref[g]
                dyg = dy_ref[rows, cols].astype(F32)
                o_ref[rows, DP + g * SG:DP + (g + 1) * SG] = (dyg * z * gug).astype(BF16)
                dz = dyg * gu
                dzb = dz.astype(BF16)
                dzs_ref[g] += dz
                dw_ref[g] += lax.dot_general(dzb, vng, _DIMS["nt"], preferred_element_type=F32)
                dvn = lax.dot_general(w_ref[g], dzb, _DIMS["tn"], preferred_element_type=F32)
                dvn_ref[rows, cols] = dvn
                dgs_ref[:, cols] += jnp.sum(dvn * nr, axis=0, keepdims=True)
                acc = acc + jnp.sum(dvn * gsg * nr, axis=1, keepdims=True)
            mean = acc * (1.0 / DP)
            for g, cols in enumerate(blocks):
                nr = gv_ref[rows, cols] * rv
                dn = dvn_ref[rows, cols] * gs_ref[:, cols]
                dgv = rv * (dn - nr * mean)
                o_ref[rows, 2 * DP + g * SG:2 * DP + (g + 1) * SG] = (dgv * gg_ref[rows, cols]).astype(BF16)

        @pl.when(i == T // tm - 1)
        def _():
            for g in range(G):
                db_ref[g] = jnp.broadcast_to(jnp.sum(dzs_ref[g], axis=1, keepdims=True), (SGU_CHUNK, SG))

    full3 = lambda a: pl.BlockSpec(a.shape, lambda i: (0, 0, 0))
    return pl.pallas_call(
        body, name=name, grid=(T // tm,),
        in_specs=[pl.BlockSpec((tm, DP), lambda i: (i, 1)), pl.BlockSpec((tm, DP), lambda i: (i, 2)),
                  pl.BlockSpec((tm, DP), lambda i: (i, 0)), pl.BlockSpec((tm, DP), lambda i: (i, 0)),
                  full3(sgu_w), full3(b_exp), pl.BlockSpec((1, DP), lambda i: (0, 0))],
        out_specs=[pl.BlockSpec((tm, 3 * DP), lambda i: (i, 0)), full3(sgu_w), full3(b_exp),
                   pl.BlockSpec((1, DP), lambda i: (0, 0))],
        out_shape=[jax.ShapeDtypeStruct((T, 6 * DP), BF16), jax.ShapeDtypeStruct(sgu_w.shape, F32),
                   jax.ShapeDtypeStruct(b_exp.shape, F32), jax.ShapeDtypeStruct((1, DP), F32)],
        scratch_shapes=[pltpu.VMEM(b_exp.shape, F32)] + [pltpu.VMEM((tm, DP), F32)] * 3,
        compiler_params=_params(("arbitrary",)),
    )(P, P, dy, da, sgu_w, b_exp, gs)


def _conv_fwd(P, cw, name):
    T = P.shape[0]
    DP = cw.shape[1]
    tc = 256
    nb = DP // tc

    def body(xc_ref, bg_ref, cg_ref, w_ref, o_ref, pad_ref):
        _fill_pad(pad_ref, lambda r0: cg_ref[pl.ds(r0, CH), :].astype(F32) * xc_ref[pl.ds(r0, CH), :].astype(F32), T)
        w = w_ref[...]

        def chunk(r0, carry):
            cq = _conv3(_ext(pad_ref, r0), w)
            o_ref[pl.ds(r0, CH), :] = (bg_ref[pl.ds(r0, CH), :].astype(F32) * cq).astype(BF16)
            return carry

        _chunks(T, chunk)

    col = lambda off: pl.BlockSpec((T, tc), lambda j: (0, off * nb + j))
    return pl.pallas_call(
        body, name=name, grid=(nb,),
        in_specs=[col(3), col(4), col(5), pl.BlockSpec((3, tc), lambda j: (0, j))],
        out_specs=pl.BlockSpec((T, tc), lambda j: (0, j)),
        out_shape=jax.ShapeDtypeStruct((T, DP), BF16),
        scratch_shapes=[pltpu.VMEM((T + 2 * HALO, tc), F32)],
        compiler_params=_params(("parallel",)),
    )(P, P, P, cw)


def _conv_bwd(P, dy, cw, dPl, name):
    T = P.shape[0]
    DP = cw.shape[1]
    tc = 256
    nb = DP // tc

    def body(xc_ref, bg_ref, cg_ref, dy_ref, w_ref, prev_ref, o_ref, dw_ref, qpad_ref, dpad_ref):
        del prev_ref
        seg = pl.program_id(1)
        w = w_ref[...]
        rows = lambda ref, r0: ref[pl.ds(r0, CH), :].astype(F32)

        @pl.when(seg == 0)
        def _():
            _fill_pad(qpad_ref, lambda r0: rows(cg_ref, r0) * rows(xc_ref, r0), T)
            _fill_pad(dpad_ref, lambda r0: rows(dy_ref, r0) * rows(bg_ref, r0), T)

        @pl.when(seg == 0)
        def _():
            def chunk(r0, carry):
                dext = _ext(dpad_ref, r0)
                taps = [_tap(dext, 1), _tap(dext, 0), _tap(dext, -1)]
                dq = taps[0] * w[0:1] + taps[1] * w[1:2] + taps[2] * w[2:3]
                o_ref[pl.ds(r0, CH), :] = (dq * rows(cg_ref, r0)).astype(BF16)
                q = qpad_ref[pl.ds(HALO + r0, CH), :]
                return [a + jnp.sum(q * t, axis=0, keepdims=True) for a, t in zip(carry, taps)]

            dws = _chunks(T, chunk, [jnp.zeros((1, tc), F32)] * 3)
            dw_ref[...] = jnp.concatenate(dws + [jnp.zeros((5, tc), F32)], axis=0)

        @pl.when(seg == 1)
        def _():
            def chunk(r0, carry):
                cq = _conv3(_ext(qpad_ref, r0), w)
                o_ref[pl.ds(r0, CH), :] = (rows(dy_ref, r0) * cq).astype(BF16)
                return carry

            _chunks(T, chunk)

        @pl.when(seg == 2)
        def _():
            def chunk(r0, carry):
                dq = _conv3_t(_ext(dpad_ref, r0), w)
                o_ref[pl.ds(r0, CH), :] = (dq * rows(xc_ref, r0)).astype(BF16)
                return carry

            _chunks(T, chunk)

    col = lambda off: pl.BlockSpec((T, tc), lambda j, s: (0, off * nb + j))
    return pl.pallas_call(
        body, name=name, grid=(nb, 3),
        in_specs=[col(3), col(4), col(5), pl.BlockSpec((T, tc), lambda j, s: (0, j)),
                  pl.BlockSpec((3, tc), lambda j, s: (0, j)), pl.BlockSpec(memory_space=pl.ANY)],
        out_specs=[pl.BlockSpec((T, tc), lambda j, s: (0, (3 + s) * nb + j)), pl.BlockSpec((8, tc), lambda j, s: (0, j))],
        out_shape=[jax.ShapeDtypeStruct(dPl.shape, BF16), jax.ShapeDtypeStruct((8, DP), F32)],
        scratch_shapes=[pltpu.VMEM((T + 2 * HALO, tc), F32), pltpu.VMEM((T + 2 * HALO, tc), F32)],
        input_output_aliases={5: 0},
        compiler_params=_params(("parallel", "arbitrary")),
    )(P, P, P, dy, cw, dPl)


def _merge_fwd(P, ys, ws, name):
    T = P.shape[0]
    DP, D = ws[0].shape
    tm = _pick(T, (512, 256, 128))
    tn = _pick(D, (1024, 512, 256))
    goff = (6 * DP) // tn

    def body(ya, yb, yc, wa, wb, wc, ga, gb, gc, oa_ref, ob_ref, oc_ref, m_ref):
        m = None
        for y, w, g, o_ref in ((ya, wa, ga, oa_ref), (yb, wb, gb, ob_ref), (yc, wc, gc, oc_ref)):
            o = jnp.dot(y[...], w[...], preferred_element_type=F32)
            o_ref[...] = o.astype(BF16)
            t = _sigmoid(g[...].astype(F32)) * o
            m = t if m is None else m + t
        m_ref[...] = m.astype(BF16)

    yspec = pl.BlockSpec((tm, DP), lambda i, j: (i, 0))
    wspec = pl.BlockSpec((DP, tn), lambda i, j: (0, j))
    gspec = lambda b: pl.BlockSpec((tm, tn), lambda i, j: (i, goff + b * (D // tn) + j))
    ospec = pl.BlockSpec((tm, tn), lambda i, j: (i, j))
    return pl.pallas_call(
        body, name=name, grid=(T // tm, D // tn),
        in_specs=[yspec] * 3 + [wspec] * 3 + [gspec(0), gspec(1), gspec(2)],
        out_specs=[ospec] * 4,
        out_shape=[jax.ShapeDtypeStruct((T, D), BF16)] * 4,
        compiler_params=_params(("parallel", "parallel")),
    )(*ys, *ws, P, P, P)


def _merge_bwd(P, dM, os_, name):
    T, D = dM.shape
    tm = _row_tile(T)

    def body(dm_ref, oa, ob, oc, g_ref, da_ref, db_ref, dc_ref, dg_ref):
        dm = dm_ref[...].astype(F32)
        for b, (o_ref, d_ref) in enumerate(((oa, da_ref), (ob, db_ref), (oc, dc_ref))):
            s = _sigmoid(g_ref[:, b * D:(b + 1) * D].astype(F32))
            d_ref[...] = (dm * s).astype(BF16)
            dg_ref[:, b * D:(b + 1) * D] = (dm * o_ref[...].astype(F32) * s * (1.0 - s)).astype(BF16)

    row = pl.BlockSpec((tm, D), lambda i: (i, 0))
    return pl.pallas_call(
        body, name=name, grid=(T // tm,),
        in_specs=[row] * 4 + [pl.BlockSpec((tm, 3 * D), lambda i: (i, 1))],
        out_specs=[row] * 3 + [pl.BlockSpec((tm, 3 * D), lambda i: (i, 0))],
        out_shape=[jax.ShapeDtypeStruct((T, D), BF16)] * 3 + [jax.ShapeDtypeStruct((T, 3 * D), BF16)],
        compiler_params=_params(("parallel",)),
    )(dM, *os_, P)


def _glu_fwd(U, fw, name):
    T, F2 = U.shape
    F = F2 // 2
    tc = 256
    nb = F // tc

    def body(ug_ref, uv_ref, wg_ref, wv_ref, o_ref, gpad_ref, vpad_ref):
        _fill_pad(gpad_ref, lambda r0: ug_ref[pl.ds(r0, CH), :].astype(F32), T)
        _fill_pad(vpad_ref, lambda r0: uv_ref[pl.ds(r0, CH), :].astype(F32), T)
        wg, wv = wg_ref[...], wv_ref[...]

        def chunk(r0, carry):
            gate = _conv3(_ext(gpad_ref, r0), wg)
            val = _conv3(_ext(vpad_ref, r0), wv)
            o_ref[pl.ds(r0, CH), :] = (_silu(gate) * val).astype(BF16)
            return carry

        _chunks(T, chunk)

    return pl.pallas_call(
        body, name=name, grid=(nb,),
        in_specs=[pl.BlockSpec((T, tc), lambda j: (0, j)), pl.BlockSpec((T, tc), lambda j: (0, nb + j)),
                  pl.BlockSpec((3, tc), lambda j: (0, j)), pl.BlockSpec((3, tc), lambda j: (0, nb + j))],
        out_specs=pl.BlockSpec((T, tc), lambda j: (0, j)),
        out_shape=jax.ShapeDtypeStruct((T, F), BF16),
        scratch_shapes=[pltpu.VMEM((T + 2 * HALO, tc), F32)] * 2,
        compiler_params=_params(("parallel",)),
    )(U, U, fw, fw)


def _glu_bwd(U, dact, fw, name):
    T, F2 = U.shape
    F = F2 // 2
    tc = 128
    nb = F // tc

    def body(ug_ref, uv_ref, da_ref, wg_ref, wv_ref, dug_ref, duv_ref, dwg_ref, dwv_ref, gpad, vpad, dgpad, dvpad):
        _fill_pad(gpad, lambda r0: ug_ref[pl.ds(r0, CH), :].astype(F32), T)
        _fill_pad(vpad, lambda r0: uv_ref[pl.ds(r0, CH), :].astype(F32), T)
        wg, wv = wg_ref[...], wv_ref[...]

        def chunk1(r0, carry):
            gate = _conv3(_ext(gpad, r0), wg)
            val = _conv3(_ext(vpad, r0), wv)
            s = _sigmoid(gate)
            da = da_ref[pl.ds(r0, CH), :].astype(F32)
            dgpad[pl.ds(HALO + r0, CH), :] = da * val * (s * (1.0 + gate * (1.0 - s)))
            dvpad[pl.ds(HALO + r0, CH), :] = da * (gate * s)
            return carry

        z = jnp.zeros((HALO, tc), F32)
        for p in (dgpad, dvpad):
            p[pl.ds(0, HALO), :] = z
            p[pl.ds(HALO + T, HALO), :] = z
        _chunks(T, chunk1)

        def chunk2(r0, carry):
            new = []
            for pad, dpad, wk, o_ref in ((gpad, dgpad, wg, dug_ref), (vpad, dvpad, wv, duv_ref)):
                dext = _ext(dpad, r0)
                taps = [_tap(dext, 1), _tap(dext, 0), _tap(dext, -1)]
                o_ref[pl.ds(r0, CH), :] = (taps[0] * wk[0:1] + taps[1] * wk[1:2] + taps[2] * wk[2:3]).astype(BF16)
                xs = pad[pl.ds(HALO + r0, CH), :]
                new += [jnp.sum(xs * t, axis=0, keepdims=True) for t in taps]
            return [a + b for a, b in zip(carry, new)]

        dws = _chunks(T, chunk2, [jnp.zeros((1, tc), F32)] * 6)
        dwg_ref[...] = jnp.concatenate(dws[:3] + [jnp.zeros((5, tc), F32)], axis=0)
        dwv_ref[...] = jnp.concatenate(dws[3:] + [jnp.zeros((5, tc), F32)], axis=0)

    lo = pl.BlockSpec((T, tc), lambda j: (0, j))
    hi = pl.BlockSpec((T, tc), lambda j: (0, nb + j))
    wlo = pl.BlockSpec((3, tc), lambda j: (0, j))
    whi = pl.BlockSpec((3, tc), lambda j: (0, nb + j))
    dwspec = pl.BlockSpec((8, tc), lambda j: (0, j))
    return pl.pallas_call(
        body, name=name, grid=(nb,),
        in_specs=[lo, hi, lo, wlo, whi],
        out_specs=[lo, lo, dwspec, dwspec],
        out_shape=[jax.ShapeDtypeStruct((T, F), BF16)] * 2 + [jax.ShapeDtypeStruct((8, F), F32)] * 2,
        scratch_shapes=[pltpu.VMEM((T + 2 * HALO, tc), F32)] * 4,
        compiler_params=_params(("parallel",)),
    )(U, U, dact, fw, fw)


_ANY = pl.BlockSpec(memory_space=pl.ANY)
_MESH = pl.DeviceIdType.MESH


def _mesh_pos():
    return lax.axis_index("x"), lax.axis_index("y"), lax.axis_index("c")


def _other_chips(x, y):
    return [(1 - x, y), (x, 1 - y), (1 - x, 1 - y)]


_HBM = pl.BlockSpec(memory_space=pltpu.HBM)
_SEM = pl.BlockSpec(memory_space=pltpu.SEMAPHORE)
_EFFECT = pltpu.SideEffectType.DATAFLOW_SIDE_EFFECTING


def _in_hbm(a):
    return pltpu.with_memory_space_constraint(a, pltpu.HBM)


def _split_start(groups, name, deps=()):
    flat, where = [], []
    for bufs, _, _ in groups:
        idx = []
        for b in bufs:
            hit = [i for i, f in enumerate(flat) if f is b]
            if not hit:
                flat.append(b)
            idx.append(hit[0] if hit else len(flat) - 1)
        where.append(idx)
    nb = len(flat)
    ng = len(groups)

    def body(*refs):
        ins = refs[:nb]
        sems = refs[nb + len(deps):nb + len(deps) + 2 * ng]
        token = refs[-1]
        for gi, (_, make_copies, _) in enumerate(groups):
            for cp in make_copies([ins[i] for i in where[gi]], sems[2 * gi], sems[2 * gi + 1]):
                cp.start()
        token[...] = jnp.zeros(token.shape, token.dtype)

    sem_shapes = []
    for _, _, n in groups:
        sem_shapes += [pltpu.SemaphoreType.DMA((n,)), pltpu.SemaphoreType.DMA((n,))]
    outs = pl.pallas_call(
        body, name=name,
        in_specs=[_HBM] * nb + [_ANY] * len(deps),
        out_specs=[_SEM] * (2 * ng) + [_HBM] * nb + [pl.BlockSpec(memory_space=pltpu.VMEM)],
        out_shape=sem_shapes + [pltpu.HBM(b.shape, b.dtype) for b in flat] + [jax.ShapeDtypeStruct((8, LANE), F32)],
        input_output_aliases={i: 2 * ng + i for i in range(nb)},
        compiler_params=pltpu.CompilerParams(has_side_effects=_EFFECT),
    )(*[_in_hbm(b) for b in flat], *deps)
    res = [(outs[2 * gi], outs[2 * gi + 1], [outs[2 * ng + i] for i in where[gi]]) for gi in range(ng)]
    return res, outs[-1]


def _split_wait(started, make_copies, after, name):
    send, recv, bufs = started
    nb = len(bufs)
    after = list(after) if isinstance(after, (list, tuple)) else [after]

    def body(*refs):
        for cp in make_copies(refs[:nb], refs[nb], refs[nb + 1]):
            cp.wait_send()
            cp.wait_recv()

    outs = pl.pallas_call(
        body, name=name,
        in_specs=[_HBM] * nb + [_SEM, _SEM] + [_ANY] * len(after),
        out_specs=[_HBM] * nb,
        out_shape=[pltpu.HBM(b.shape, b.dtype) for b in bufs],
        input_output_aliases={i: i for i in range(nb)},
        compiler_params=pltpu.CompilerParams(has_side_effects=_EFFECT),
    )(*bufs, send, recv, *after)
    return list(outs)


def _slot(buf, s, cols=None):
    if len(buf.shape) == 3:
        return buf.at[s] if cols is None else buf.at[s, :, pl.ds(cols[0], cols[1])]
    ns = buf.shape[1] // N_DEV
    first, width = (0, ns) if cols is None else cols
    return buf.at[:, pl.ds(pl.multiple_of(s * ns, LANE) + first, width)]


def _copies_gather_chips(nbuf, cols=None):
    def make(bufs, send, recv):
        x, y, c = _mesh_pos()
        targets = [(x, y, 1 - c)] + [(px, py, c) for px, py in _other_chips(x, y)]
        cps = []
        for a in range(nbuf):
            mine = _slot(bufs[a], 4 * x + 2 * y + c, cols)
            for k, to in enumerate(targets):
                cps.append(pltpu.make_async_remote_copy(src_ref=mine, dst_ref=mine, send_sem=send.at[4 * a + k],
                                                        recv_sem=recv.at[4 * a + k], device_id=to, device_id_type=_MESH))
        return cps

    return make, 4 * nbuf


def _copies_gather_forward(nbuf, cols=None):
    def make(bufs, send, recv):
        x, y, c = _mesh_pos()
        cps = []
        for a in range(nbuf):
            for j, (px, py) in enumerate(_other_chips(x, y)):
                slot = _slot(bufs[a], 4 * px + 2 * py + c, cols)
                cps.append(pltpu.make_async_remote_copy(src_ref=slot, dst_ref=slot, send_sem=send.at[3 * a + j],
                                                        recv_sem=recv.at[3 * a + j], device_id=(x, y, 1 - c),
                                                        device_id_type=_MESH))
        return cps

    return make, 3 * nbuf


def _copies_rs_sibling(n):
    def make(bufs, send, recv):
        x, y, c = _mesh_pos()
        cps = []
        for a in range(n):
            for q in range(4):
                cps.append(pltpu.make_async_remote_copy(
                    src_ref=_slot(bufs[a], 2 * q + (1 - c)), dst_ref=bufs[n + a].at[q], send_sem=send.at[4 * a + q],
                    recv_sem=recv.at[4 * a + q], device_id=(x, y, 1 - c), device_id_type=_MESH))
        return cps

    return make, 4 * n


def _copies_rs_chips(n):
    def make(bufs, send, recv):
        x, y, c = _mesh_pos()
        cps = []
        for a in range(n):
            for j, (px, py) in enumerate(_other_chips(x, y)):
                cps.append(pltpu.make_async_remote_copy(
                    src_ref=bufs[a].at[j], dst_ref=bufs[n + a].at[j], send_sem=send.at[3 * a + j],
                    recv_sem=recv.at[3 * a + j], device_id=(px, py, c), device_id_type=_MESH))
        return cps

    return make, 3 * n


def _cast_place(stacked, layer, slot, dtype, name, deps=(), by_cols=False):
    _, R, C = stacked.shape
    tr = _rows_tile(R, C)

    def body(slot_ref, x_ref, *rest):
        del slot_ref
        rest[-1][...] = x_ref[...].astype(dtype)

    if by_cols:
        out_spec = pl.BlockSpec((tr, C), lambda i, s: (i, s[0]))
        out_shape = jax.ShapeDtypeStruct((R, N_DEV * C), dtype)
    else:
        out_spec = pl.BlockSpec((None, tr, C), lambda i, s: (s[0], i, 0))
        out_shape = jax.ShapeDtypeStruct((N_DEV, R, C), dtype)
    return pl.pallas_call(
        body, name=name,
        grid_spec=pltpu.PrefetchScalarGridSpec(
            num_scalar_prefetch=1, grid=(R // tr,),
            in_specs=[pl.BlockSpec((None, tr, C), lambda i, s: (layer, i, 0))] + [_ANY] * len(deps),
            out_specs=out_spec),
        out_shape=out_shape,
        compiler_params=_params(("parallel",)),
    )(slot, stacked, *deps)


def _rows_tile(R, C, budget=2 * MIB):
    if R * C * 4 <= budget:
        return R
    for t in (2048, 1024, 704, 512, 352, 256, 128, 64, 32, 16, 8):
        if R % t == 0 and t * C * 4 <= budget:
            return t
    return R


def _chip_of_relation(j, pos):
    x, y = pos[0], pos[1]
    return 2 * jnp.where(j == 1, x, 1 - x) + jnp.where(j == 0, y, 1 - y)


def _pair_sum(grad, recv, pos, name):
    _, R, C = recv.shape
    tr = _rows_tile(R, C, 8 * MIB)

    def body(pos_ref, g_ref, r_ref, o_ref):
        del pos_ref
        o_ref[...] = (g_ref[...].astype(F32) + r_ref[...].astype(F32)).astype(BF16)

    if len(grad.shape) == 3:
        g_spec = pl.BlockSpec((None, tr, C), lambda j, i, p: (2 * _chip_of_relation(j, p) + p[2], i, 0))
    else:
        g_spec = pl.BlockSpec((tr, C), lambda j, i, p: (i, 2 * _chip_of_relation(j, p) + p[2]))
    return pl.pallas_call(
        body, name=name,
        grid_spec=pltpu.PrefetchScalarGridSpec(
            num_scalar_prefetch=1, grid=(3, R // tr),
            in_specs=[g_spec,
                      pl.BlockSpec((None, tr, C), lambda j, i, p: (_chip_of_relation(j, p), i, 0))],
            out_specs=pl.BlockSpec((None, tr, C), lambda j, i, p: (j, i, 0))),
        out_shape=jax.ShapeDtypeStruct((3, R, C), BF16),
        compiler_params=_params(("parallel", "parallel")),
    )(pos, grad, recv)


ADAM_LR = 0.001
ADAM_B1 = 0.9
ADAM_B2 = 0.999
ADAM_EPS = 1e-08
ADAM_WD = 0.01
ADAM_STEP = 10


def _adamw(w, g, m, v):
    m = ADAM_B1 * m + (1.0 - ADAM_B1) * g
    v = ADAM_B2 * v + (1.0 - ADAM_B2) * (g * g)
    m_hat = m / (1.0 - ADAM_B1 ** ADAM_STEP)
    v_hat = v / (1.0 - ADAM_B2 ** ADAM_STEP)
    delta = -ADAM_LR * (m_hat / (jnp.sqrt(v_hat) + ADAM_EPS) + ADAM_WD * w)
    return delta, m, v


def _adamw_big(grad, sib, recv, w, m, v, pos, layer, prev, name):
    L, R, C = w.shape
    tr = _rows_tile(R, C)
    has_prev = prev is not None

    def body(pos_ref, g_ref, s_ref, r0_ref, r1_ref, r2_ref, w_ref, m_ref, v_ref, *rest):
        del pos_ref
        go_ref, d_ref, mo_ref, vo_ref = rest[-4:]
        g = g_ref[...].astype(F32) + s_ref[...].astype(F32)
        g = ((g + r0_ref[...].astype(F32)) + r1_ref[...].astype(F32)) + r2_ref[...].astype(F32)
        delta, m2, v2 = _adamw(w_ref[...], g, m_ref[...], v_ref[...])
        go_ref[...] = g
        d_ref[...] = delta
        mo_ref[...] = m2
        vo_ref[...] = v2

    lay = pl.BlockSpec((None, tr, C), lambda i, p: (layer, i, 0))
    rspec = lambda j: pl.BlockSpec((None, tr, C), lambda i, p: (j, i, 0))
    mine = lambda p: 4 * p[0] + 2 * p[1] + p[2]
    if len(grad.shape) == 3:
        g_spec = pl.BlockSpec((None, tr, C), lambda i, p: (mine(p), i, 0))
    else:
        g_spec = pl.BlockSpec((tr, C), lambda i, p: (i, mine(p)))
    s_spec = pl.BlockSpec((None, tr, C), lambda i, p: (2 * p[0] + p[1], i, 0))
    in_specs = [g_spec, s_spec, rspec(0), rspec(1), rspec(2), lay, lay, lay]
    args = [pos, grad, sib, recv, recv, recv, w, m, v]
    aliases = {}
    if has_prev:
        in_specs += [_ANY] * 4
        args += list(prev)
        aliases = {9 + k: k for k in range(4)}
    return pl.pallas_call(
        body, name=name,
        grid_spec=pltpu.PrefetchScalarGridSpec(
            num_scalar_prefetch=1, grid=(R // tr,), in_specs=in_specs, out_specs=[lay] * 4),
        out_shape=[jax.ShapeDtypeStruct((L, R, C), F32)] * 4,
        input_output_aliases=aliases,
        compiler_params=_params(("parallel",)),
    )(*args)


def _sum_slots(gathered, name):
    _, R, C = gathered.shape
    tr = _rows_tile(R, C, 2 * MIB)

    def body(x_ref, o_ref):
        s = x_ref[0]
        for k in range(1, N_DEV):
            s = s + x_ref[k]
        o_ref[...] = s

    return pl.pallas_call(
        body, name=name, grid=(R // tr,),
        in_specs=[pl.BlockSpec((N_DEV, tr, C), lambda i: (0, i, 0))],
        out_specs=pl.BlockSpec((tr, C), lambda i: (i, 0)),
        out_shape=jax.ShapeDtypeStruct((R, C), F32),
        compiler_params=_params(("parallel",)),
    )(gathered)


def _adamw_small(w, g, m, v, name):
    R, C = w.shape
    tr = _rows_tile(R, C, 2 * MIB)

    def body(w_ref, g_ref, m_ref, v_ref, d_ref, mo_ref, vo_ref):
        d_ref[...], mo_ref[...], vo_ref[...] = _adamw(w_ref[...], g_ref[...], m_ref[...], v_ref[...])

    spec = pl.BlockSpec((tr, C), lambda i: (i, 0))
    return pl.pallas_call(
        body, name=name, grid=(R // tr,),
        in_specs=[spec] * 4, out_specs=[spec] * 3,
        out_shape=[jax.ShapeDtypeStruct((R, C), F32)] * 3,
        compiler_params=_params(("parallel",)),
    )(w, g, m, v)


def _pack(arrs):
    parts = []
    for a in arrs:
        flat = a.reshape(-1)
        pad = (-flat.shape[0]) % (8 * LANE)
        if pad:
            flat = jnp.pad(flat, (0, pad))
        parts.append(flat.reshape(-1, LANE))
    return jnp.concatenate(parts, axis=0)


def _unpack(packed, shapes):
    out, r = [], 0
    for s in shapes:
        n = 1
        for d in s:
            n *= d
        rows = -(-n // (8 * LANE)) * 8
        out.append(packed[r:r + rows].reshape(-1)[:n].reshape(s))
        r += rows
    return out


def kernel(x, norm1_g, w_in, w_pool, pool_scale, sgu_norm_g, sgu_w, sgu_b, conv_w, w_pool_out, w_sgu_out, w_conv_out, w_o, norm2_g, w_up, ffn_conv_w, w_down, final_g, loss_target, m_norm1_g, m_w_in, m_w_pool, m_pool_scale, m_sgu_norm_g, m_sgu_w, m_sgu_b, m_conv_w, m_w_pool_out, m_w_sgu_out, m_w_conv_out, m_w_o, m_norm2_g, m_w_up, m_ffn_conv_w, m_w_down, m_final_g, v_norm1_g, v_w_in, v_w_pool, v_pool_scale, v_sgu_norm_g, v_sgu_w, v_sgu_b, v_conv_w, v_w_pool_out, v_w_sgu_out, v_w_conv_out, v_w_o, v_norm2_g, v_w_up, v_ffn_conv_w, v_w_down, v_final_g):
    L = norm1_g.shape[0]
    x0 = x[0]
    tgt = loss_target[0]
    G, PG = w_pool.shape[1], w_pool.shape[3]
    F = w_down.shape[1] * N_DEV
    T, D = x0.shape
    tT = _pick(T, (1024, 512, 256, 128))
    tT2 = _pick(T, (2048, tT))
    tF = _pick(F, (512, 256))
    tF2 = _pick(2 * F, (1024, 512, 256))
    kF = _pick(F, (2816, 2048, 1024, 512))
    px, py, pc = _mesh_pos()
    me = 4 * px + 2 * py + pc
    pos = jnp.stack([px, py, pc]).astype(jnp.int32)

    slot = jnp.reshape(me, (1,)).astype(jnp.int32)
    ns_in = w_in.shape[2]
    nq = 3 if ns_in % (3 * 256) == 0 else 1
    chunk_cols = [(q * (ns_in // nq), ns_in // nq) for q in range(nq)]
    win0 = _cast_place(w_in, 0, slot, BF16, "place_w_in_0", by_cols=True)
    first, tok0 = _split_start([([win0],) + _copies_gather_chips(1, chunk_cols[0])], "ag_start_first")
    win0 = first[0][2][0]
    h1_first = _rms_fwd(x0, norm1_g[0][None], "rms1_fwd_0")
    stage1, group_sizes, started1 = {}, {0: 1}, {}

    def place_groups(l, deps):
        grp = [[("w_in", w_in, BF16, True)],
               [("w_pool_out", w_pool_out, BF16, True), ("w_sgu_out", w_sgu_out, BF16, True),
                ("w_conv_out", w_conv_out, BF16, True), ("w_pool", w_pool.reshape(L, -1, PG), BF16, False),
                ("conv_w", conv_w, F32, True), ("w_o", w_o, BF16, False)],
               [("w_up", w_up, BF16, True), ("ffn_conv_w", ffn_conv_w, F32, True)],
               [("w_down", w_down, BF16, False)]]
        ks = []
        for gi, members in enumerate(grp):
            if l == 0 and gi == 0:
                continue
            k = 4 * l + gi
            bufs = [_cast_place(a, l, slot, dt, f"place_{n}_{l}", deps, bc) for n, a, dt, bc in members]
            stage1[k] = (bufs,) + _copies_gather_chips(len(members))
            group_sizes[k] = len(members)
            ks.append(k)
        return ks

    def start_groups(ks, extra, name, deps=()):
        st, tok = _split_start(extra + [stage1[k] for k in ks], name, deps)
        for k, s in zip(ks, st[len(extra):]):
            started1[k] = s
        return st[:len(extra)], tok

    rest, tok_start0 = start_groups(place_groups(0, [tok0, h1_first]),
                                    [([win0],) + _copies_gather_chips(1, cols) for cols in chunk_cols[1:]], "ag_start_0")
    if nq > 1:
        win0 = rest[0][2][0]
    chunk_sems = [first[0][:2]] + [r[:2] for r in rest]
    later_groups = [place_groups(l, [tok_start0]) for l in range(1, L)]
    placed_later = [b for ks in later_groups for k in ks for b in stage1[k][0]]
    started2 = {}

    def gather_forward(k, after):
        n = group_sizes[k]
        bufs = _split_wait(started1[k], stage1[k][1], after, f"ag_wait1_{k}")
        st, tok = _split_start([(bufs,) + _copies_gather_forward(n)], f"ag_fwd_{k}")
        started2[k] = st[0]
        return tok

    def gather_done(k, after):
        return _split_wait(started2.pop(k), _copies_gather_forward(group_sizes[k])[0], after, f"ag_wait2_{k}")

    W, saved = [], []
    xc = x0
    for l in range(L):
        w = dict(
            sguw=sgu_w[l].astype(BF16),
            bexp=jnp.broadcast_to(sgu_b[l].T[:, :, None],
                                  (sgu_b.shape[2], sgu_b.shape[1], sgu_norm_g.shape[1] // sgu_b.shape[2])),
            g1=norm1_g[l][None], g2=norm2_g[l][None], gs=sgu_norm_g[l][None], scale=pool_scale[l][None])
        ka, kb, kc, kd = 4 * l, 4 * l + 1, 4 * l + 2, 4 * l + 3
        h1 = h1_first if l == 0 else _rms_fwd(xc, w["g1"], f"rms1_fwd_{l}")
        if l == 0:
            P, after = None, [h1] + placed_later
            for q, cols in enumerate(chunk_cols):
                (win0,) = _split_wait(chunk_sems[q] + ([win0],), _copies_gather_chips(1, cols)[0], after,
                                      f"ag_wait1_0_{q}")
                st, tok = _split_start([([win0],) + _copies_gather_forward(1, cols)], f"ag_fwd_0_{q}")
                (win0,) = _split_wait(st[0], _copies_gather_forward(1, cols)[0], after, f"ag_wait2_0_{q}")
                P = _mm(h1, win0, mode="nn", out_dtype=BF16, name=f"proj_in_0_{q}", tm=tT2, tn=cols[1], tk=D,
                        cols=(q, nq), prev=P, deps=[tok])
                after = P
            w["win"] = win0
        else:
            (w["win"],) = gather_done(ka, h1)
            tok = gather_forward(kb, h1)
            P = _mm(h1, w["win"], mode="nn", out_dtype=BF16, name=f"proj_in_{l}", tm=tT2, tn=1024, tk=D, deps=[tok])
        if l == 0:
            gather_forward(kb, P)
        w["wa"], w["wb"], w["wc"], wp, w["cw"], wo = gather_done(kb, P)
        w.update(wpool=wp.reshape(N_DEV, G, -1, PG).transpose(1, 0, 2, 3).reshape(G, PG, PG),
                 wo=wo.reshape(-1, wo.shape[2]))
        ya = _pool_fwd(P, w["wpool"], w["scale"], f"pool_fwd_{l}")
        yb = _sgu_fwd(P, w["sguw"], w["bexp"], w["gs"], f"sgu_fwd_{l}")
        yc = _conv_fwd(P, w["cw"], f"conv_fwd_{l}")
        oa, ob, oc, M = _merge_fwd(P, (ya, yb, yc), (w["wa"], w["wb"], w["wc"]), f"merge_fwd_{l}")
        tok = gather_forward(kc, M)
        if l == 0:
            for l2 in range(1, L):
                start_groups(later_groups[l2 - 1], [], f"ag_start_{l2}", [tok])
        x1, h2 = _mm_res_rms(M, w["wo"], xc, w["g2"], f"proj_o_{l}", deps=[tok])
        w["wup"], w["fw"] = gather_done(kc, h2)
        tok = gather_forward(kd, h2)
        U = _mm(h2, w["wup"], mode="nn", out_dtype=BF16, name=f"proj_up_{l}", tm=tT2, tn=tF2, tk=D, deps=[tok])
        act = _glu_fwd(U, w["fw"], f"glu_fwd_{l}")
        (wdown,) = gather_done(kd, act)
        w["wdown"] = wdown.reshape(-1, wdown.shape[2])
        deps = [gather_forward(4 * (l + 1), act)] if l + 1 < L else []
        x2 = _mm(act, w["wdown"], mode="nn", out_dtype=F32, name=f"proj_down_{l}", tm=tT, tn=512, tk=F, res=x1,
                 deps=deps)
        if deps:
            tok = deps[0]
        W.append(w)
        saved.append(dict(x0=xc, h1=h1, P=P, ya=ya, yb=yb, yc=yc, oa=oa, ob=ob, oc=oc, M=M, x1=x1, h2=h2, U=U, act=act))
        xc = x2

    dx, dxb, d_final_g, loss_part = _loss_head(xc, tgt, final_g[None], "loss_head")
    loss = lax.psum(loss_part[0, 0], ("x", "y", "c"))

    wmv = dict(w_in=(w_in, m_w_in, v_w_in), w_up=(w_up, m_w_up, v_w_up), w_o=(w_o, m_w_o, v_w_o),
               w_down=(w_down, m_w_down, v_w_down), w_pool_out=(w_pool_out, m_w_pool_out, v_w_pool_out),
               w_sgu_out=(w_sgu_out, m_w_sgu_out, v_w_sgu_out), w_conv_out=(w_conv_out, m_w_conv_out, v_w_conv_out),
               w_pool=(w_pool, m_w_pool, v_w_pool))
    adam_out = {}

    def rs_begin(tag, layer, names, grads):
        n = len(grads)
        slice_shape = lambda g: g.shape[1:] if len(g.shape) == 3 else (g.shape[0], g.shape[1] // N_DEV)
        lands = [lax.empty((4,) + slice_shape(g), BF16) for g in grads]
        st, tok = _split_start([(list(grads) + lands,) + _copies_rs_sibling(n)], f"rs_sib_{tag}")
        return dict(tag=tag, layer=layer, names=names, n=n, st=st[0]), tok

    def rs_to_chips(state, after):
        n, tag = state["n"], state["tag"]
        bufs = _split_wait(state["st"], _copies_rs_sibling(n)[0], after, f"rs_sibw_{tag}")
        grads, sib = bufs[:n], bufs[n:]
        pairs = [_pair_sum(g, r, pos, f"pair_{tag}_{i}") for i, (g, r) in enumerate(zip(grads, sib))]
        lands = [lax.empty(p.shape, BF16) for p in pairs]
        st, tok = _split_start([(pairs + lands,) + _copies_rs_chips(n)], f"rs_chips_{tag}")
        state.update(st=st[0], grads=grads, sib=sib)
        return tok

    def rs_finish(state, after):
        n, tag, layer = state["n"], state["tag"], state["layer"]
        bufs = _split_wait(state["st"], _copies_rs_chips(n)[0], after, f"rs_chipsw_{tag}")
        for name, g, sb, recv in zip(state["names"], state["grads"], state["sib"], bufs[n:]):
            wk, mk, vk = (a.reshape(L, -1, a.shape[-1]) for a in wmv[name])
            adam_out[name] = _adamw_big(g, sb, recv, wk, mk, vk, pos, layer, adam_out.get(name),
                                        f"adamw_{name}_{layer}")

    small = [None] * L
    pending = []
    deps = []
    for l in reversed(range(L)):
        w, s = W[l], saved[l]
        dact = _mm(dxb, w["wdown"], mode="nt", out_dtype=BF16, name=f"d_act_{l}", tm=tT2, tn=tF, tk=D, deps=deps)
        g_down = _mm(s["act"], dxb, mode="tn", out_dtype=BF16, name=f"g_down_{l}", tm=tF, tn=1024, tk=T)
        dug, duv, dfwg, dfwv = _glu_bwd(s["U"], dact, w["fw"], f"glu_bwd_{l}")
        dh2 = _mm(dug, w["wup"], mode="nt", out_dtype=F32, name=f"d_h2_{l}", tm=tT, tn=1024, tk=kF, a2=duv)
        g_up = _mm(s["h2"], dug, mode="tn", out_dtype=BF16, name=f"g_upa_{l}", tm=1024, tn=tF, tk=T, out_n=2 * F)
        g_up = _mm(s["h2"], duv, mode="tn", out_dtype=BF16, name=f"g_upb_{l}", tm=1024, tn=tF, tk=T, out_n=2 * F,
                   out_off=F // tF, prev=g_up)
        ra, tok = rs_begin(f"a{l}", l, ["w_down", "w_up"], [g_down.reshape(N_DEV, -1, g_down.shape[1]), g_up])
        if pending:
            rs_finish(pending.pop(0), g_up)
        dx1, dx1b, dg2 = _rms_bwd(dh2, s["x1"], w["g2"], dx, f"rms2_bwd_{l}")
        dM = _mm(dx1b, w["wo"], mode="nt", out_dtype=BF16, name=f"d_m_{l}", tm=tT, tn=1024, tk=D, deps=[tok])
        g_o = _mm(s["M"], dx1b, mode="tn", out_dtype=BF16, name=f"g_o_{l}", tm=1024, tn=1024, tk=T)
        doa, dob, doc, dPg = _merge_bwd(s["P"], dM, (s["oa"], s["ob"], s["oc"]), f"merge_bwd_{l}")
        tok = rs_to_chips(ra, dPg)
        if pending:
            rs_finish(pending.pop(0), dPg)
        dya = _mm(doa, w["wa"], mode="nt", out_dtype=BF16, name=f"d_ya_{l}", tm=tT, tn=1024, tk=D, deps=[tok])
        dyb = _mm(dob, w["wb"], mode="nt", out_dtype=BF16, name=f"d_yb_{l}", tm=tT, tn=1024, tk=D)
        dyc = _mm(doc, w["wc"], mode="nt", out_dtype=BF16, name=f"d_yc_{l}", tm=tT, tn=1024, tk=D)
        g_a = _mm(s["ya"], doa, mode="tn", out_dtype=BF16, name=f"g_a_{l}", tm=1024, tn=1024, tk=T)
        g_b = _mm(s["yb"], dob, mode="tn", out_dtype=BF16, name=f"g_b_{l}", tm=1024, tn=1024, tk=T)
        g_c = _mm(s["yc"], doc, mode="tn", out_dtype=BF16, name=f"g_c_{l}", tm=1024, tn=1024, tk=T)
        da, g_pool, dscale = _pool_bwd(s["P"], dya, w["wpool"], w["scale"], f"pool_bwd_{l}")
        g_pool_s = g_pool.reshape(G, N_DEV, -1, PG).transpose(1, 0, 2, 3).reshape(N_DEV, -1, PG).astype(BF16)
        rb, tok = rs_begin(f"b{l}", l, ["w_o", "w_pool_out", "w_sgu_out", "w_conv_out", "w_pool"],
                           [g_o.reshape(N_DEV, -1, g_o.shape[1]), g_a, g_b, g_c, g_pool_s])
        if pending:
            rs_finish(pending.pop(0), da)
        dPl, g_sguw, db_exp, dgs = _sgu_bwd(s["P"], dyb, da, w["sguw"], w["bexp"], w["gs"], f"sgu_bwd_{l}")
        dPl, dcw = _conv_bwd(s["P"], dyc, w["cw"], dPl, f"conv_bwd_{l}")
        tok2 = rs_to_chips(rb, dPl)
        half = dPl.shape[1]
        g_in = _mm(s["h1"], dPl, mode="tn", out_dtype=BF16, name=f"g_ina_{l}", tm=1024, tn=1024, tk=T, out_n=2 * half,
                   deps=[tok, tok2])
        g_in = _mm(s["h1"], dPg, mode="tn", out_dtype=BF16, name=f"g_inb_{l}", tm=1024, tn=1024, tk=T, out_n=2 * half,
                   out_off=half // 1024, prev=g_in)
        rc, tok = rs_begin(f"c{l}", l, ["w_in"], [g_in])
        if l > 0:
            dh1 = _mm(dPl, w["win"], mode="nt", out_dtype=F32, name=f"d_h1_{l}", tm=tT, tn=1024, tk=D, a2=dPg,
                      deps=[tok])
            tok = rs_to_chips(rc, dh1)
        else:
            dh1 = _mm(dPl, w["win"], mode="nt", out_dtype=F32, name=f"d_h1a_{l}", tm=tT, tn=1024, tk=D, deps=[tok])
            tok = rs_to_chips(rc, dh1)
            dh1 = _mm(dPg, w["win"], mode="nt", out_dtype=F32, name=f"d_h1b_{l}", tm=tT, tn=1024, tk=D,
                      b_koff=half // D, res=dh1, deps=[tok])
        dx, dxb, dg1 = _rms_bwd(dh1, s["x0"], w["g1"], dx1, f"rms1_bwd_{l}", deps=[tok])
        deps = []
        pending = [ra, rb, rc]
        small[l] = dict(norm1_g=dg1[0], pool_scale=dscale[0], sgu_norm_g=dgs[0], sgu_w=g_sguw,
                        sgu_b=db_exp[:, :, 0].T, conv_w=dcw[:3], norm2_g=dg2[0],
                        ffn_conv_w=jnp.concatenate([dfwg[:3], dfwv[:3]], axis=1))
    grad_x = dx[None]

    snames = ["norm1_g", "pool_scale", "sgu_norm_g", "sgu_w", "sgu_b", "conv_w", "norm2_g", "ffn_conv_w"]
    sparts = [jnp.stack([small[l][n] for l in range(L)]) for n in snames] + [d_final_g[0]]
    snames = snames + ["final_g"]
    packed = _pack(sparts)
    sbuf = _cast_place(packed[None], 0, slot, F32, "place_small")
    st, tok = _split_start([([sbuf],) + _copies_gather_chips(1)], "ag_small_start")
    rs_finish(pending[0], tok)
    rs_finish(pending[1], tok)
    bufs = _split_wait(st[0], _copies_gather_chips(1)[0], [o[0] for o in adam_out.values()], "ag_small_wait1")
    st, tok = _split_start([(bufs,) + _copies_gather_forward(1)], "ag_small_fwd")
    (gathered_small,) = _split_wait(st[0], _copies_gather_forward(1)[0], tok, "ag_small_wait2")
    total = _sum_slots(gathered_small, "sum_small")
    sgrads = dict(zip(snames, _unpack(total, [p.shape for p in sparts])))
    for n, width in (("conv_w", conv_w.shape[2]), ("ffn_conv_w", ffn_conv_w.shape[2])):
        sgrads[n] = lax.dynamic_slice_in_dim(sgrads[n], me * width, width, axis=2)
    sw = dict(norm1_g=(norm1_g, m_norm1_g, v_norm1_g), pool_scale=(pool_scale, m_pool_scale, v_pool_scale),
              sgu_norm_g=(sgu_norm_g, m_sgu_norm_g, v_sgu_norm_g), sgu_w=(sgu_w, m_sgu_w, v_sgu_w),
              sgu_b=(sgu_b, m_sgu_b, v_sgu_b), conv_w=(conv_w, m_conv_w, v_conv_w),
              norm2_g=(norm2_g, m_norm2_g, v_norm2_g), ffn_conv_w=(ffn_conv_w, m_ffn_conv_w, v_ffn_conv_w),
              final_g=(final_g, m_final_g, v_final_g))
    shapes = [sw[n][0].shape for n in snames]
    upd = _adamw_small(_pack([sw[n][0] for n in snames]), _pack([sgrads[n] for n in snames]),
                       _pack([sw[n][1] for n in snames]), _pack([sw[n][2] for n in snames]), "adamw_small")
    sdelta, sm, sv = (dict(zip(snames, _unpack(u, shapes))) for u in upd)
    res = {n: [sgrads[n], sdelta[n], sm[n], sv[n]] for n in snames}

    rs_finish(pending[2], [upd[0]] + [o[0] for o in adam_out.values()])
    for n, outs in adam_out.items():
        res[n] = [o.reshape(wmv[n][0].shape) for o in outs]

    order = ["norm1_g", "w_in", "w_pool", "pool_scale", "sgu_norm_g", "sgu_w", "sgu_b", "conv_w", "w_pool_out",
             "w_sgu_out", "w_conv_out", "w_o", "norm2_g", "w_up", "ffn_conv_w", "w_down", "final_g"]
    return (loss, grad_x) + tuple(res[n][k] for k in range(4) for n in order)
```

```python
import functools

import jax
import jax.numpy as jnp
from jax import lax
from jax.experimental import pallas as pl
from jax.experimental.pallas import tpu as pltpu

BF16 = jnp.bfloat16
F32 = jnp.float32
EPS = 1e-6
MIB = 1024 * 1024
V7X_VMEM_BYTES = 64 * MIB
VMEM_LIMIT = 56 * MIB
LANE = 128
N_DEV = 8


def _params(sem, **kw):
    return pltpu.CompilerParams(dimension_semantics=sem, vmem_limit_bytes=VMEM_LIMIT, **kw)


def _pick(n, cands):
    for c in cands:
        if n % c == 0:
            return c
    return n


_DIMS = {"nn": (((1,), (0,)), ((), ())), "nt": (((1,), (1,)), ((), ())), "tn": (((0,), (0,)), ((), ()))}


def _mm(a, b, *, mode, out_dtype, name, tm, tn, tk, res=None, prev=None, deps=(), cols=(0, 1), b_koff=0,
        out_off=0, out_n=None, a2=None):
    if mode == "nn":
        M, Kc = a.shape
        nj = (b.shape[1] // tn - cols[0] + cols[1] - 1) // cols[1]
        out_w = b.shape[1]
    elif mode == "nt":
        M, Kc = a.shape
        nj = b.shape[0] // tn
        out_w = b.shape[0]
    else:
        Kc, M = a.shape
        nj = b.shape[1] // tn
        out_w = out_n or b.shape[1]
    nk1 = Kc // tk
    nk = nk1 + (a2.shape[1] // tk if a2 is not None else 0)
    assert M % tm == 0 and Kc % tk == 0 and (a2 is None or mode == "nt"), (name, M, Kc, tm, tk)
    off, stride = cols
    if mode == "nn":
        a_spec = pl.BlockSpec((tm, tk), lambda i, j, k: (i, k))
        b_spec = pl.BlockSpec((tk, tn), lambda i, j, k: (k, off + j * stride))
        o_spec = pl.BlockSpec((tm, tn), lambda i, j, k: (i, off + j * stride))
    elif mode == "nt":
        a_spec = pl.BlockSpec((tm, tk), lambda i, j, k: (i, jnp.minimum(k, nk1 - 1)))
        a2_spec = pl.BlockSpec((tm, tk), lambda i, j, k: (i, jnp.maximum(k - nk1, 0)))
        b_spec = pl.BlockSpec((tn, tk), lambda i, j, k: (j, b_koff + k))
        o_spec = pl.BlockSpec((tm, tn), lambda i, j, k: (i, j))
    else:
        a_spec = pl.BlockSpec((tk, tm), lambda i, j, k: (k, i))
        b_spec = pl.BlockSpec((tk, tn), lambda i, j, k: (k, j))
        o_spec = pl.BlockSpec((tm, tn), lambda i, j, k: (i, out_off + j))
    dims = _DIMS[mode]
    has_res = res is not None
    has_prev = prev is not None
    n_in = 2 + has_res + has_prev + len(deps) + (a2 is not None)

    def body(*refs):
        a_ref, b_ref = refs[0], refs[1]
        o_ref = refs[n_in]

        def finish(acc):
            if has_res:
                acc = acc + refs[2][...]
            o_ref[...] = acc.astype(o_ref.dtype)

        if nk == 1:
            finish(lax.dot_general(a_ref[...], b_ref[...], dims, preferred_element_type=F32))
        else:
            acc_ref = refs[n_in + 1]
            k = pl.program_id(2)

            @pl.when(k == 0)
            def _():
                acc_ref[...] = jnp.zeros(acc_ref.shape, F32)

            if a2 is None:
                acc_ref[...] += lax.dot_general(a_ref[...], b_ref[...], dims, preferred_element_type=F32)
            else:
                @pl.when(k < nk1)
                def _():
                    acc_ref[...] += lax.dot_general(a_ref[...], b_ref[...], dims, preferred_element_type=F32)

                @pl.when(k >= nk1)
                def _():
                    acc_ref[...] += lax.dot_general(refs[n_in - 1][...], b_ref[...], dims, preferred_element_type=F32)

            @pl.when(k == nk - 1)
            def _():
                finish(acc_ref[...])

    in_specs = [a_spec, b_spec]
    args = [a, b]
    if has_res:
        in_specs.append(pl.BlockSpec((tm, tn), lambda i, j, k: (i, j)))
        args.append(res)
    aliases = {}
    if has_prev:
        aliases = {len(args): 0}
        in_specs.append(pl.BlockSpec(memory_space=pl.ANY))
        args.append(prev)
    for d in deps:
        in_specs.append(pl.BlockSpec(memory_space=pl.ANY))
        args.append(d)
    if a2 is not None:
        in_specs.append(a2_spec)
        args.append(a2)
    return pl.pallas_call(
        body,
        name=name,
        grid=(M // tm, nj, nk),
        in_specs=in_specs,
        out_specs=o_spec,
        out_shape=jax.ShapeDtypeStruct((M, out_w), out_dtype),
        scratch_shapes=[pltpu.VMEM((tm, tn), F32)] if nk > 1 else [],
        input_output_aliases=aliases,
        compiler_params=_params(("parallel", "parallel", "arbitrary")),
    )(*args)


def _mm_res_rms(a, b, res, g, name, deps=()):
    M, K = a.shape
    N = b.shape[1]
    tm = _pick(M, (512, 256, 128))

    def body(a_ref, b_ref, res_ref, g_ref, *rest):
        x_ref, h_ref = rest[-2:]
        xv = res_ref[...] + jnp.dot(a_ref[...], b_ref[...], preferred_element_type=F32)
        x_ref[...] = xv
        r = lax.rsqrt(jnp.mean(xv * xv, axis=-1, keepdims=True) + EPS)
        h_ref[...] = (xv * r * g_ref[...]).astype(BF16)

    row = pl.BlockSpec((tm, N), lambda i: (i, 0))
    return pl.pallas_call(
        body, name=name, grid=(M // tm,),
        in_specs=[pl.BlockSpec((tm, K), lambda i: (i, 0)), pl.BlockSpec((K, N), lambda i: (0, 0)), row,
                  pl.BlockSpec((1, N), lambda i: (0, 0))] + [pl.BlockSpec(memory_space=pl.ANY)] * len(deps),
        out_specs=[row, row],
        out_shape=[jax.ShapeDtypeStruct((M, N), F32), jax.ShapeDtypeStruct((M, N), BF16)],
        compiler_params=_params(("parallel",)),
    )(a, b, res, g, *deps)


HALO = 16
CH = 256


def _fill_pad(pad_ref, chunk_fn, T):
    z = jnp.zeros((HALO, pad_ref.shape[1]), F32)
    pad_ref[pl.ds(0, HALO), :] = z
    pad_ref[pl.ds(HALO + T, HALO), :] = z

    def body(c, carry):
        r0 = pl.multiple_of(c * CH, CH)
        pad_ref[pl.ds(HALO + r0, CH), :] = chunk_fn(r0)
        return carry

    lax.fori_loop(0, T // CH, body, 0)


def _ext(pad_ref, r0):
    return pad_ref[pl.ds(r0, CH + 2 * HALO), :]


def _tap(ext, o):
    if o == 0:
        return ext[HALO:HALO + CH]
    return pltpu.roll(ext, (-o) % ext.shape[0], axis=0)[HALO:HALO + CH]


def _chunks(T, fn, init=0):
    def body(c, carry):
        return fn(pl.multiple_of(c * CH, CH), carry)

    return lax.fori_loop(0, T // CH, body, init)


def _conv3(ext, w):
    return _tap(ext, -1) * w[0:1] + _tap(ext, 0) * w[1:2] + _tap(ext, 1) * w[2:3]


def _conv3_t(ext, w):
    return _tap(ext, 1) * w[0:1] + _tap(ext, 0) * w[1:2] + _tap(ext, -1) * w[2:3]


def _sigmoid(x):
    return 0.5 * (jnp.tanh(0.5 * x) + 1.0)


def _silu(x):
    h = 0.5 * x
    return h * (jnp.tanh(h) + 1.0)


_GELU_C = 0.7978845608028654


def _gelu(x):
    return 0.5 * x * (1.0 + jnp.tanh(_GELU_C * (x + 0.044715 * (x * x * x))))


def _gelu_and_grad(x):
    x2 = x * x
    t = jnp.tanh(_GELU_C * (x + 0.044715 * (x2 * x)))
    return 0.5 * x * (1.0 + t), 0.5 * (1.0 + t) + 0.5 * x * (1.0 - t * t) * (_GELU_C * (1.0 + 3.0 * 0.044715 * x2))


def _row_tile(T):
    return _pick(T, (256, 128))


def _norm_tile(T):
    return _pick(T, (512, 256, 128))


def _rms_fwd(x, g, name):
    T, D = x.shape
    tm = _norm_tile(T)

    def body(x_ref, g_ref, h_ref):
        xv = x_ref[...]
        r = lax.rsqrt(jnp.mean(xv * xv, axis=-1, keepdims=True) + EPS)
        h_ref[...] = (xv * r * g_ref[...]).astype(BF16)

    return pl.pallas_call(
        body, name=name, grid=(T // tm,),
        in_specs=[pl.BlockSpec((tm, D), lambda i: (i, 0)), pl.BlockSpec((1, D), lambda i: (0, 0))],
        out_specs=pl.BlockSpec((tm, D), lambda i: (i, 0)),
        out_shape=jax.ShapeDtypeStruct((T, D), BF16),
        compiler_params=_params(("parallel",)),
    )(x, g)


def _rms_bwd(dh, x, g, dres, name, deps=()):
    T, D = x.shape
    tm = _norm_tile(T)

    def body(dh_ref, x_ref, g_ref, dres_ref, *rest):
        dx_ref, dxb_ref, dg_ref = rest[len(deps):]
        i = pl.program_id(0)
        xv = x_ref[...]
        r = lax.rsqrt(jnp.mean(xv * xv, axis=-1, keepdims=True) + EPS)
        n = xv * r
        dh_v = dh_ref[...].astype(F32)
        dn = dh_v * g_ref[...]
        dx = dres_ref[...] + r * (dn - n * jnp.mean(dn * n, axis=-1, keepdims=True))
        dx_ref[...] = dx
        dxb_ref[...] = dx.astype(BF16)
        dg = jnp.sum(dh_v * n, axis=0, keepdims=True)

        @pl.when(i == 0)
        def _():
            dg_ref[...] = dg

        @pl.when(i > 0)
        def _():
            dg_ref[...] += dg

    row = pl.BlockSpec((tm, D), lambda i: (i, 0))
    vec = pl.BlockSpec((1, D), lambda i: (0, 0))
    return pl.pallas_call(
        body, name=name, grid=(T // tm,),
        in_specs=[row, row, vec, row] + [pl.BlockSpec(memory_space=pl.ANY)] * len(deps),
        out_specs=[row, row, vec],
        out_shape=[jax.ShapeDtypeStruct((T, D), F32), jax.ShapeDtypeStruct((T, D), BF16),
                   jax.ShapeDtypeStruct((1, D), F32)],
        compiler_params=_params(("arbitrary",)),
    )(dh, x, g, dres, *deps)


def _loss_head(x, tgt, g, name):
    T, D = x.shape
    tm = _norm_tile(T)

    def body(x_ref, t_ref, g_ref, dx_ref, dxb_ref, dg_ref, l_ref):
        i = pl.program_id(0)
        xv = x_ref[...]
        gv = g_ref[...]
        r = lax.rsqrt(jnp.mean(xv * xv, axis=-1, keepdims=True) + EPS)
        n = xv * r
        e = n * gv - t_ref[...]
        dy = e * (1.0 / D)
        dn = dy * gv
        dx = r * (dn - n * jnp.mean(dn * n, axis=-1, keepdims=True))
        dx_ref[...] = dx
        dxb_ref[...] = dx.astype(BF16)
        dg = jnp.sum(dy * n, axis=0, keepdims=True)
        per_tok = jnp.mean(e * e, axis=-1, keepdims=True)
        lv = jnp.broadcast_to(0.5 * jnp.sum(per_tok, axis=0, keepdims=True), (1, LANE))

        @pl.when(i == 0)
        def _():
            dg_ref[...] = dg
            l_ref[...] = lv

        @pl.when(i > 0)
        def _():
            dg_ref[...] += dg
            l_ref[...] += lv

    row = pl.BlockSpec((tm, D), lambda i: (i, 0))
    vec = pl.BlockSpec((1, D), lambda i: (0, 0))
    return pl.pallas_call(
        body, name=name, grid=(T // tm,),
        in_specs=[row, row, vec],
        out_specs=[row, row, vec, pl.BlockSpec((1, LANE), lambda i: (0, 0))],
        out_shape=[jax.ShapeDtypeStruct((T, D), F32), jax.ShapeDtypeStruct((T, D), BF16),
                   jax.ShapeDtypeStruct((1, D), F32), jax.ShapeDtypeStruct((1, LANE), F32)],
        compiler_params=_params(("arbitrary",)),
    )(x, tgt, g)


POOL_WINDOWS = (2, 4, 8, 16)


def _pool_cnt(r0, w, T, shape):
    t = r0 + lax.broadcasted_iota(jnp.int32, shape, 0)
    lo = jnp.maximum(t - w // 2, 0)
    hi = jnp.minimum(t + (w - w // 2 - 1), T - 1)
    return (hi - lo + 1).astype(F32)


def _window_sum(ext, w, mirrored=False):
    n = ext.shape[0]
    cur = ext + pltpu.roll(ext, n - 1 if mirrored else 1, axis=0)
    width, sh = 2, 1
    while width < w:
        cur = pltpu.roll(cur, sh, axis=0) + pltpu.roll(cur, n - sh, axis=0)
        width, sh = 2 * width, 2 * sh
    return cur[HALO:HALO + CH]


def _pooled(pad_ref, r0, w, T):
    ext = _ext(pad_ref, r0)
    cur = ext[HALO:HALO + CH]
    return _window_sum(ext, w) / _pool_cnt(r0, w, T, cur.shape) - cur


def _pool_fwd(P, w_pool, scale, name):
    T = P.shape[0]
    G, PG, _ = w_pool.shape

    def body(a_ref, w_ref, s_ref, o_ref, pad_ref):
        g = pl.program_id(0)
        _fill_pad(pad_ref, lambda r0: a_ref[pl.ds(r0, CH), :].astype(F32), T)
        for gi, w in enumerate(POOL_WINDOWS):
            @pl.when(g == gi)
            def _(w=w):
                def chunk(r0, carry):
                    pooled = _pooled(pad_ref, r0, w, T).astype(BF16)
                    y = jnp.dot(pooled, w_ref[...], preferred_element_type=F32) * s_ref[...]
                    o_ref[pl.ds(r0, CH), :] = y.astype(BF16)
                    return carry

                _chunks(T, chunk)

    return pl.pallas_call(
        body, name=name, grid=(G,),
        in_specs=[pl.BlockSpec((T, PG), lambda g: (0, g)),
                  pl.BlockSpec((None, PG, PG), lambda g: (g, 0, 0)),
                  pl.BlockSpec((1, PG), lambda g: (0, g))],
        out_specs=pl.BlockSpec((T, PG), lambda g: (0, g)),
        out_shape=jax.ShapeDtypeStruct((T, G * PG), BF16),
        scratch_shapes=[pltpu.VMEM((T + 2 * HALO, PG), F32)],
        compiler_params=_params(("parallel",)),
    )(P, w_pool, scale)


def _pool_bwd(P, dy, w_pool, scale, name):
    T = P.shape[0]
    G, PG, _ = w_pool.shape

    def body(a_ref, dy_ref, w_ref, s_ref, da_ref, dw_ref, ds_ref, pad_ref, gpad_ref):
        g = pl.program_id(0)
        gpad_ref[pl.ds(0, HALO), :] = jnp.zeros((HALO, PG), F32)
        gpad_ref[pl.ds(HALO + T, HALO), :] = jnp.zeros((HALO, PG), F32)
        _fill_pad(pad_ref, lambda r0: a_ref[pl.ds(r0, CH), :].astype(F32), T)
        for gi, w in enumerate(POOL_WINDOWS):
            @pl.when(g == gi)
            def _(w=w):
                def chunk1(r0, carry):
                    dw, ds = carry
                    pooled = _pooled(pad_ref, r0, w, T).astype(BF16)
                    ypre = jnp.dot(pooled, w_ref[...], preferred_element_type=F32)
                    dyv = dy_ref[pl.ds(r0, CH), :].astype(F32)
                    ds = ds + jnp.sum(dyv * ypre, axis=0, keepdims=True)
                    dyp = (dyv * s_ref[...]).astype(BF16)
                    dw = dw + lax.dot_general(pooled, dyp, _DIMS["tn"], preferred_element_type=F32)
                    dpool = lax.dot_general(dyp, w_ref[...], _DIMS["nt"], preferred_element_type=F32)
                    gpad_ref[pl.ds(HALO + r0, CH), :] = dpool / _pool_cnt(r0, w, T, dpool.shape)
                    return dw, ds

                dw, ds = _chunks(T, chunk1, (jnp.zeros((PG, PG), F32), jnp.zeros((1, PG), F32)))
                dw_ref[...] = dw
                ds_ref[...] = ds

                def chunk2(r0, carry):
                    ext = _ext(gpad_ref, r0)
                    cur = ext[HALO:HALO + CH]
                    acc = _window_sum(ext, w, mirrored=True)
                    da_ref[pl.ds(r0, CH), :] = (acc - cur * _pool_cnt(r0, w, T, cur.shape)).astype(BF16)
                    return carry

                _chunks(T, chunk2)

    col = pl.BlockSpec((T, PG), lambda g: (0, g))
    return pl.pallas_call(
        body, name=name, grid=(G,),
        in_specs=[col, col, pl.BlockSpec((None, PG, PG), lambda g: (g, 0, 0)), pl.BlockSpec((1, PG), lambda g: (0, g))],
        out_specs=[col, pl.BlockSpec((None, PG, PG), lambda g: (g, 0, 0)), pl.BlockSpec((1, PG), lambda g: (0, g))],
        out_shape=[jax.ShapeDtypeStruct((T, G * PG), BF16), jax.ShapeDtypeStruct((G, PG, PG), F32),
                   jax.ShapeDtypeStruct((1, G * PG), F32)],
        scratch_shapes=[pltpu.VMEM((T + 2 * HALO, PG), F32), pltpu.VMEM((T + 2 * HALO, PG), F32)],
        compiler_params=_params(("parallel",)),
    )(P, dy, w_pool, scale)


SGU_CHUNK = 128


def _sgu_common(u_ref, v_ref, gs_ref):
    up = u_ref[...].astype(F32)
    vp = v_ref[...].astype(F32)
    gv = _gelu(vp)
    rv = lax.rsqrt(jnp.mean(gv * gv, axis=-1, keepdims=True) + EPS)
    nrm = gv * rv
    return up, vp, nrm, rv, (nrm * gs_ref[...]).astype(BF16)


def _sgu_fwd(P, sgu_w, b_exp, gs, name):
    T = P.shape[0]
    G, _, SG = b_exp.shape
    DP = G * SG
    tm = _row_tile(T)

    def body(u_ref, v_ref, w_ref, b_ref, gs_ref, o_ref):
        up, _, _, _, vn = _sgu_common(u_ref, v_ref, gs_ref)
        gu = _gelu(up)
        for n in range(tm // SGU_CHUNK):
            rows = slice(n * SGU_CHUNK, (n + 1) * SGU_CHUNK)
            for g in range(G):
                cols = slice(g * SG, (g + 1) * SG)
                z = jnp.dot(w_ref[g], vn[rows, cols], preferred_element_type=F32) + b_ref[g]
                o_ref[rows, cols] = (gu[rows, cols] * z).astype(BF16)

    return pl.pallas_call(
        body, name=name, grid=(T // tm,),
        in_specs=[pl.BlockSpec((tm, DP), lambda i: (i, 1)), pl.BlockSpec((tm, DP), lambda i: (i, 2)),
                  pl.BlockSpec(sgu_w.shape, lambda i: (0, 0, 0)), pl.BlockSpec(b_exp.shape, lambda i: (0, 0, 0)),
                  pl.BlockSpec((1, DP), lambda i: (0, 0))],
        out_specs=pl.BlockSpec((tm, DP), lambda i: (i, 0)),
        out_shape=jax.ShapeDtypeStruct((T, DP), BF16),
        compiler_params=_params(("parallel",)),
    )(P, P, sgu_w, b_exp, gs)


def _sgu_bwd(P, dy, da, sgu_w, b_exp, gs, name):
    T = P.shape[0]
    G, _, SG = b_exp.shape
    DP = G * SG
    tm = _row_tile(T)

    def body(u_ref, v_ref, dy_ref, da_ref, w_ref, b_ref, gs_ref, o_ref, dw_ref, db_ref, dgs_ref, dzs_ref,
             gv_ref, gg_ref, dvn_ref):
        i = pl.program_id(0)
        o_ref[:, 0:DP] = da_ref[...]

        @pl.when(i == 0)
        def _():
            dw_ref[...] = jnp.zeros(dw_ref.shape, F32)
            dzs_ref[...] = jnp.zeros(dzs_ref.shape, F32)
            dgs_ref[...] = jnp.zeros(dgs_ref.shape, F32)

        blocks = [slice(g * SG, (g + 1) * SG) for g in range(G)]
        for n in range(tm // SGU_CHUNK):
            rows = slice(n * SGU_CHUNK, (n + 1) * SGU_CHUNK)
            ss = jnp.zeros((SGU_CHUNK, 1), F32)
            for cols in blocks:
                gv, gg = _gelu_and_grad(v_ref[rows, cols].astype(F32))
                gv_ref[rows, cols] = gv
                gg_ref[rows, cols] = gg
                ss = ss + jnp.sum(gv * gv, axis=1, keepdims=True)
            rv = lax.rsqrt(ss * (1.0 / DP) + EPS)
            acc = jnp.zeros((SGU_CHUNK, 1), F32)
            for g, cols in enumerate(blocks):
                gsg = gs_ref[:, cols]
                nr = gv_ref[rows, cols] * rv
                vng = (nr * gsg).astype(BF16)
                gu, gug = _gelu_and_grad(u_ref[rows, cols].astype(F32))
                z = jnp.dot(w_ref[g], vng, preferred_element_type=F32) + b_ref[g]
                dyg = dy_ref[rows, cols].astype(F32)
                o_ref[rows, DP + g * SG:DP + (g + 1) * SG] = (dyg * z * gug).astype(BF16)
                dz = dyg * gu
                dzb = dz.astype(BF16)
                dzs_ref[g] += dz
                dw_ref[g] += lax.dot_general(dzb, vng, _DIMS["nt"], preferred_element_type=F32)
                dvn = lax.dot_general(w_ref[g], dzb, _DIMS["tn"], preferred_element_type=F32)
                dvn_ref[rows, cols] = dvn
                dgs_ref[:, cols] += jnp.sum(dvn * nr, axis=0, keepdims=True)
                acc = acc + jnp.sum(dvn * gsg * nr, axis=1, keepdims=True)
            mean = acc * (1.0 / DP)
            for g, cols in enumerate(blocks):
                nr = gv_ref[rows, cols] * rv
                dn = dvn_ref[rows, cols] * gs_ref[:, cols]
                dgv = rv * (dn - nr * mean)
                o_ref[rows, 2 * DP + g * SG:2 * DP + (g + 1) * SG] = (dgv * gg_ref[rows, cols]).astype(BF16)

        @pl.when(i == T // tm - 1)
        def _():
            for g in range(G):
                db_ref[g] = jnp.broadcast_to(jnp.sum(dzs_ref[g], axis=1, keepdims=True), (SGU_CHUNK, SG))

    full3 = lambda a: pl.BlockSpec(a.shape, lambda i: (0, 0, 0))
    return pl.pallas_call(
        body, name=name, grid=(T // tm,),
        in_specs=[pl.BlockSpec((tm, DP), lambda i: (i, 1)), pl.BlockSpec((tm, DP), lambda i: (i, 2)),
                  pl.BlockSpec((tm, DP), lambda i: (i, 0)), pl.BlockSpec((tm, DP), lambda i: (i, 0)),
                  full3(sgu_w), full3(b_exp), pl.BlockSpec((1, DP), lambda i: (0, 0))],
        out_specs=[pl.BlockSpec((tm, 3 * DP), lambda i: (i, 0)), full3(sgu_w), full3(b_exp),
                   pl.BlockSpec((1, DP), lambda i: (0, 0))],
        out_shape=[jax.ShapeDtypeStruct((T, 6 * DP), BF16), jax.ShapeDtypeStruct(sgu_w.shape, F32),
                   jax.ShapeDtypeStruct(b_exp.shape, F32), jax.ShapeDtypeStruct((1, DP), F32)],
        scratch_shapes=[pltpu.VMEM(b_exp.shape, F32)] + [pltpu.VMEM((tm, DP), F32)] * 3,
        compiler_params=_params(("arbitrary",)),
    )(P, P, dy, da, sgu_w, b_exp, gs)


def _conv_fwd(P, cw, name):
    T = P.shape[0]
    DP = cw.shape[1]
    tc = 256
    nb = DP // tc

    def body(xc_ref, bg_ref, cg_ref, w_ref, o_ref, pad_ref):
        _fill_pad(pad_ref, lambda r0: cg_ref[pl.ds(r0, CH), :].astype(F32) * xc_ref[pl.ds(r0, CH), :].astype(F32), T)
        w = w_ref[...]

        def chunk(r0, carry):
            cq = _conv3(_ext(pad_ref, r0), w)
            o_ref[pl.ds(r0, CH), :] = (bg_ref[pl.ds(r0, CH), :].astype(F32) * cq).astype(BF16)
            return carry

        _chunks(T, chunk)

    col = lambda off: pl.BlockSpec((T, tc), lambda j: (0, off * nb + j))
    return pl.pallas_call(
        body, name=name, grid=(nb,),
        in_specs=[col(3), col(4), col(5), pl.BlockSpec((3, tc), lambda j: (0, j))],
        out_specs=pl.BlockSpec((T, tc), lambda j: (0, j)),
        out_shape=jax.ShapeDtypeStruct((T, DP), BF16),
        scratch_shapes=[pltpu.VMEM((T + 2 * HALO, tc), F32)],
        compiler_params=_params(("parallel",)),
    )(P, P, P, cw)


def _conv_bwd(P, dy, cw, dPl, name):
    T = P.shape[0]
    DP = cw.shape[1]
    tc = 256
    nb = DP // tc

    def body(xc_ref, bg_ref, cg_ref, dy_ref, w_ref, prev_ref, o_ref, dw_ref, qpad_ref, dpad_ref):
        del prev_ref
        seg = pl.program_id(1)
        w = w_ref[...]
        rows = lambda ref, r0: ref[pl.ds(r0, CH), :].astype(F32)

        @pl.when(seg == 0)
        def _():
            _fill_pad(qpad_ref, lambda r0: rows(cg_ref, r0) * rows(xc_ref, r0), T)
            _fill_pad(dpad_ref, lambda r0: rows(dy_ref, r0) * rows(bg_ref, r0), T)

        @pl.when(seg == 0)
        def _():
            def chunk(r0, carry):
                dext = _ext(dpad_ref, r0)
                taps = [_tap(dext, 1), _tap(dext, 0), _tap(dext, -1)]
                dq = taps[0] * w[0:1] + taps[1] * w[1:2] + taps[2] * w[2:3]
                o_ref[pl.ds(r0, CH), :] = (dq * rows(cg_ref, r0)).astype(BF16)
                q = qpad_ref[pl.ds(HALO + r0, CH), :]
                return [a + jnp.sum(q * t, axis=0, keepdims=True) for a, t in zip(carry, taps)]

            dws = _chunks(T, chunk, [jnp.zeros((1, tc), F32)] * 3)
            dw_ref[...] = jnp.concatenate(dws + [jnp.zeros((5, tc), F32)], axis=0)

        @pl.when(seg == 1)
        def _():
            def chunk(r0, carry):
                cq = _conv3(_ext(qpad_ref, r0), w)
                o_ref[pl.ds(r0, CH), :] = (rows(dy_ref, r0) * cq).astype(BF16)
                return carry

            _chunks(T, chunk)

        @pl.when(seg == 2)
        def _():
            def chunk(r0, carry):
                dq = _conv3_t(_ext(dpad_ref, r0), w)
                o_ref[pl.ds(r0, CH), :] = (dq * rows(xc_ref, r0)).astype(BF16)
                return carry

            _chunks(T, chunk)

    col = lambda off: pl.BlockSpec((T, tc), lambda j, s: (0, off * nb + j))
    return pl.pallas_call(
        body, name=name, grid=(nb, 3),
        in_specs=[col(3), col(4), col(5), pl.BlockSpec((T, tc), lambda j, s: (0, j)),
                  pl.BlockSpec((3, tc), lambda j, s: (0, j)), pl.BlockSpec(memory_space=pl.ANY)],
        out_specs=[pl.BlockSpec((T, tc), lambda j, s: (0, (3 + s) * nb + j)), pl.BlockSpec((8, tc), lambda j, s: (0, j))],
        out_shape=[jax.ShapeDtypeStruct(dPl.shape, BF16), jax.ShapeDtypeStruct((8, DP), F32)],
        scratch_shapes=[pltpu.VMEM((T + 2 * HALO, tc), F32), pltpu.VMEM((T + 2 * HALO, tc), F32)],
        input_output_aliases={5: 0},
        compiler_params=_params(("parallel", "arbitrary")),
    )(P, P, P, dy, cw, dPl)


def _merge_fwd(P, ys, ws, name):
    T = P.shape[0]
    DP, D = ws[0].shape
    tm = _pick(T, (512, 256, 128))
    tn = _pick(D, (1024, 512, 256))
    goff = (6 * DP) // tn

    def body(ya, yb, yc, wa, wb, wc, ga, gb, gc, oa_ref, ob_ref, oc_ref, m_ref):
        m = None
        for y, w, g, o_ref in ((ya, wa, ga, oa_ref), (yb, wb, gb, ob_ref), (yc, wc, gc, oc_ref)):
            o = jnp.dot(y[...], w[...], preferred_element_type=F32)
            o_ref[...] = o.astype(BF16)
            t = _sigmoid(g[...].astype(F32)) * o
            m = t if m is None else m + t
        m_ref[...] = m.astype(BF16)

    yspec = pl.BlockSpec((tm, DP), lambda i, j: (i, 0))
    wspec = pl.BlockSpec((DP, tn), lambda i, j: (0, j))
    gspec = lambda b: pl.BlockSpec((tm, tn), lambda i, j: (i, goff + b * (D // tn) + j))
    ospec = pl.BlockSpec((tm, tn), lambda i, j: (i, j))
    return pl.pallas_call(
        body, name=name, grid=(T // tm, D // tn),
        in_specs=[yspec] * 3 + [wspec] * 3 + [gspec(0), gspec(1), gspec(2)],
        out_specs=[ospec] * 4,
        out_shape=[jax.ShapeDtypeStruct((T, D), BF16)] * 4,
        compiler_params=_params(("parallel", "parallel")),
    )(*ys, *ws, P, P, P)


def _merge_bwd(P, dM, os_, name):
    T, D = dM.shape
    tm = _row_tile(T)

    def body(dm_ref, oa, ob, oc, g_ref, da_ref, db_ref, dc_ref, dg_ref):
        dm = dm_ref[...].astype(F32)
        for b, (o_ref, d_ref) in enumerate(((oa, da_ref), (ob, db_ref), (oc, dc_ref))):
            s = _sigmoid(g_ref[:, b * D:(b + 1) * D].astype(F32))
            d_ref[...] = (dm * s).astype(BF16)
            dg_ref[:, b * D:(b + 1) * D] = (dm * o_ref[...].astype(F32) * s * (1.0 - s)).astype(BF16)

    row = pl.BlockSpec((tm, D), lambda i: (i, 0))
    return pl.pallas_call(
        body, name=name, grid=(T // tm,),
        in_specs=[row] * 4 + [pl.BlockSpec((tm, 3 * D), lambda i: (i, 1))],
        out_specs=[row] * 3 + [pl.BlockSpec((tm, 3 * D), lambda i: (i, 0))],
        out_shape=[jax.ShapeDtypeStruct((T, D), BF16)] * 3 + [jax.ShapeDtypeStruct((T, 3 * D), BF16)],
        compiler_params=_params(("parallel",)),
    )(dM, *os_, P)


def _glu_fwd(U, fw, name):
    T, F2 = U.shape
    F = F2 // 2
    tc = 256
    nb = F // tc

    def body(ug_ref, uv_ref, wg_ref, wv_ref, o_ref, gpad_ref, vpad_ref):
        _fill_pad(gpad_ref, lambda r0: ug_ref[pl.ds(r0, CH), :].astype(F32), T)
        _fill_pad(vpad_ref, lambda r0: uv_ref[pl.ds(r0, CH), :].astype(F32), T)
        wg, wv = wg_ref[...], wv_ref[...]

        def chunk(r0, carry):
            gate = _conv3(_ext(gpad_ref, r0), wg)
            val = _conv3(_ext(vpad_ref, r0), wv)
            o_ref[pl.ds(r0, CH), :] = (_silu(gate) * val).astype(BF16)
            return carry

        _chunks(T, chunk)

    return pl.pallas_call(
        body, name=name, grid=(nb,),
        in_specs=[pl.BlockSpec((T, tc), lambda j: (0, j)), pl.BlockSpec((T, tc), lambda j: (0, nb + j)),
                  pl.BlockSpec((3, tc), lambda j: (0, j)), pl.BlockSpec((3, tc), lambda j: (0, nb + j))],
        out_specs=pl.BlockSpec((T, tc), lambda j: (0, j)),
        out_shape=jax.ShapeDtypeStruct((T, F), BF16),
        scratch_shapes=[pltpu.VMEM((T + 2 * HALO, tc), F32)] * 2,
        compiler_params=_params(("parallel",)),
    )(U, U, fw, fw)


def _glu_bwd(U, dact, fw, name):
    T, F2 = U.shape
    F = F2 // 2
    tc = 128
    nb = F // tc

    def body(ug_ref, uv_ref, da_ref, wg_ref, wv_ref, dug_ref, duv_ref, dwg_ref, dwv_ref, gpad, vpad, dgpad, dvpad):
        _fill_pad(gpad, lambda r0: ug_ref[pl.ds(r0, CH), :].astype(F32), T)
        _fill_pad(vpad, lambda r0: uv_ref[pl.ds(r0, CH), :].astype(F32), T)
        wg, wv = wg_ref[...], wv_ref[...]

        def chunk1(r0, carry):
            gate = _conv3(_ext(gpad, r0), wg)
            val = _conv3(_ext(vpad, r0), wv)
            s = _sigmoid(gate)
            da = da_ref[pl.ds(r0, CH), :].astype(F32)
            dgpad[pl.ds(HALO + r0, CH), :] = da * val * (s * (1.0 + gate * (1.0 - s)))
            dvpad[pl.ds(HALO + r0, CH), :] = da * (gate * s)
            return carry

        z = jnp.zeros((HALO, tc), F32)
        for p in (dgpad, dvpad):
            p[pl.ds(0, HALO), :] = z
            p[pl.ds(HALO + T, HALO), :] = z
        _chunks(T, chunk1)

        def chunk2(r0, carry):
            new = []
            for pad, dpad, wk, o_ref in ((gpad, dgpad, wg, dug_ref), (vpad, dvpad, wv, duv_ref)):
                dext = _ext(dpad, r0)
                taps = [_tap(dext, 1), _tap(dext, 0), _tap(dext, -1)]
                o_ref[pl.ds(r0, CH), :] = (taps[0] * wk[0:1] + taps[1] * wk[1:2] + taps[2] * wk[2:3]).astype(BF16)
                xs = pad[pl.ds(HALO + r0, CH), :]
                new += [jnp.sum(xs * t, axis=0, keepdims=True) for t in taps]
            return [a + b for a, b in zip(carry, new)]

        dws = _chunks(T, chunk2, [jnp.zeros((1, tc), F32)] * 6)
        dwg_ref[...] = jnp.concatenate(dws[:3] + [jnp.zeros((5, tc), F32)], axis=0)
        dwv_ref[...] = jnp.concatenate(dws[3:] + [jnp.zeros((5, tc), F32)], axis=0)

    lo = pl.BlockSpec((T, tc), lambda j: (0, j))
    hi = pl.BlockSpec((T, tc), lambda j: (0, nb + j))
    wlo = pl.BlockSpec((3, tc), lambda j: (0, j))
    whi = pl.BlockSpec((3, tc), lambda j: (0, nb + j))
    dwspec = pl.BlockSpec((8, tc), lambda j: (0, j))
    return pl.pallas_call(
        body, name=name, grid=(nb,),
        in_specs=[lo, hi, lo, wlo, whi],
        out_specs=[lo, lo, dwspec, dwspec],
        out_shape=[jax.ShapeDtypeStruct((T, F), BF16)] * 2 + [jax.ShapeDtypeStruct((8, F), F32)] * 2,
        scratch_shapes=[pltpu.VMEM((T + 2 * HALO, tc), F32)] * 4,
        compiler_params=_params(("parallel",)),
    )(U, U, dact, fw, fw)


_ANY = pl.BlockSpec(memory_space=pl.ANY)
_MESH = pl.DeviceIdType.MESH


def _mesh_pos():
    return lax.axis_index("x"), lax.axis_index("y"), lax.axis_index("c")


def _other_chips(x, y):
    return [(1 - x, y), (x, 1 - y), (1 - x, 1 - y)]


_HBM = pl.BlockSpec(memory_space=pltpu.HBM)
_SEM = pl.BlockSpec(memory_space=pltpu.SEMAPHORE)
_EFFECT = pltpu.SideEffectType.DATAFLOW_SIDE_EFFECTING


def _in_hbm(a):
    return pltpu.with_memory_space_constraint(a, pltpu.HBM)


def _split_start(groups, name, deps=()):
    flat, where = [], []
    for bufs, _, _ in groups:
        idx = []
        for b in bufs:
            hit = [i for i, f in enumerate(flat) if f is b]
            if not hit:
                flat.append(b)
            idx.append(hit[0] if hit else len(flat) - 1)
        where.append(idx)
    nb = len(flat)
    ng = len(groups)

    def body(*refs):
        ins = refs[:nb]
        sems = refs[nb + len(deps):nb + len(deps) + 2 * ng]
        token = refs[-1]
        for gi, (_, make_copies, _) in enumerate(groups):
            for cp in make_copies([ins[i] for i in where[gi]], sems[2 * gi], sems[2 * gi + 1]):
                cp.start()
        token[...] = jnp.zeros(token.shape, token.dtype)

    sem_shapes = []
    for _, _, n in groups:
        sem_shapes += [pltpu.SemaphoreType.DMA((n,)), pltpu.SemaphoreType.DMA((n,))]
    outs = pl.pallas_call(
        body, name=name,
        in_specs=[_HBM] * nb + [_ANY] * len(deps),
        out_specs=[_SEM] * (2 * ng) + [_HBM] * nb + [pl.BlockSpec(memory_space=pltpu.VMEM)],
        out_shape=sem_shapes + [pltpu.HBM(b.shape, b.dtype) for b in flat] + [jax.ShapeDtypeStruct((8, LANE), F32)],
        input_output_aliases={i: 2 * ng + i for i in range(nb)},
        compiler_params=pltpu.CompilerParams(has_side_effects=_EFFECT),
    )(*[_in_hbm(b) for b in flat], *deps)
    res = [(outs[2 * gi], outs[2 * gi + 1], [outs[2 * ng + i] for i in where[gi]]) for gi in range(ng)]
    return res, outs[-1]


def _split_wait(started, make_copies, after, name):
    send, recv, bufs = started
    nb = len(bufs)
    after = list(after) if isinstance(after, (list, tuple)) else [after]

    def body(*refs):
        for cp in make_copies(refs[:nb], refs[nb], refs[nb + 1]):
            cp.wait_send()
            cp.wait_recv()

    outs = pl.pallas_call(
        body, name=name,
        in_specs=[_HBM] * nb + [_SEM, _SEM] + [_ANY] * len(after),
        out_specs=[_HBM] * nb,
        out_shape=[pltpu.HBM(b.shape, b.dtype) for b in bufs],
        input_output_aliases={i: i for i in range(nb)},
        compiler_params=pltpu.CompilerParams(has_side_effects=_EFFECT),
    )(*bufs, send, recv, *after)
    return list(outs)


def _slot(buf, s, cols=None):
    if len(buf.shape) == 3:
        return buf.at[s] if cols is None else buf.at[s, :, pl.ds(cols[0], cols[1])]
    ns = buf.shape[1] // N_DEV
    first, width = (0, ns) if cols is None else cols
    return buf.at[:, pl.ds(pl.multiple_of(s * ns, LANE) + first, width)]


def _copies_gather_chips(nbuf, cols=None):
    def make(bufs, send, recv):
        x, y, c = _mesh_pos()
        targets = [(x, y, 1 - c)] + [(px, py, c) for px, py in _other_chips(x, y)]
        cps = []
        for a in range(nbuf):
            mine = _slot(bufs[a], 4 * x + 2 * y + c, cols)
            for k, to in enumerate(targets):
                cps.append(pltpu.make_async_remote_copy(src_ref=mine, dst_ref=mine, send_sem=send.at[4 * a + k],
                                                        recv_sem=recv.at[4 * a + k], device_id=to, device_id_type=_MESH))
        return cps

    return make, 4 * nbuf


def _copies_gather_forward(nbuf, cols=None):
    def make(bufs, send, recv):
        x, y, c = _mesh_pos()
        cps = []
        for a in range(nbuf):
            for j, (px, py) in enumerate(_other_chips(x, y)):
                slot = _slot(bufs[a], 4 * px + 2 * py + c, cols)
                cps.append(pltpu.make_async_remote_copy(src_ref=slot, dst_ref=slot, send_sem=send.at[3 * a + j],
                                                        recv_sem=recv.at[3 * a + j], device_id=(x, y, 1 - c),
                                                        device_id_type=_MESH))
        return cps

    return make, 3 * nbuf


def _copies_rs_sibling(n):
    def make(bufs, send, recv):
        x, y, c = _mesh_pos()
        cps = []
        for a in range(n):
            for q in range(4):
                cps.append(pltpu.make_async_remote_copy(
                    src_ref=_slot(bufs[a], 2 * q + (1 - c)), dst_ref=bufs[n + a].at[q], send_sem=send.at[4 * a + q],
                    recv_sem=recv.at[4 * a + q], device_id=(x, y, 1 - c), device_id_type=_MESH))
        return cps

    return make, 4 * n


def _copies_rs_chips(n):
    def make(bufs, send, recv):
        x, y, c = _mesh_pos()
        cps = []
        for a in range(n):
            for j, (px, py) in enumerate(_other_chips(x, y)):
                cps.append(pltpu.make_async_remote_copy(
                    src_ref=bufs[a].at[j], dst_ref=bufs[n + a].at[j], send_sem=send.at[3 * a + j],
                    recv_sem=recv.at[3 * a + j], device_id=(px, py, c), device_id_type=_MESH))
        return cps

    return make, 3 * n


def _cast_place(stacked, layer, slot, dtype, name, deps=(), by_cols=False):
    _, R, C = stacked.shape
    tr = _rows_tile(R, C)

    def body(slot_ref, x_ref, *rest):
        del slot_ref
        rest[-1][...] = x_ref[...].astype(dtype)

    if by_cols:
        out_spec = pl.BlockSpec((tr, C), lambda i, s: (i, s[0]))
        out_shape = jax.ShapeDtypeStruct((R, N_DEV * C), dtype)
    else:
        out_spec = pl.BlockSpec((None, tr, C), lambda i, s: (s[0], i, 0))
        out_shape = jax.ShapeDtypeStruct((N_DEV, R, C), dtype)
    return pl.pallas_call(
        body, name=name,
        grid_spec=pltpu.PrefetchScalarGridSpec(
            num_scalar_prefetch=1, grid=(R // tr,),
            in_specs=[pl.BlockSpec((None, tr, C), lambda i, s: (layer, i, 0))] + [_ANY] * len(deps),
            out_specs=out_spec),
        out_shape=out_shape,
        compiler_params=_params(("parallel",)),
    )(slot, stacked, *deps)


def _rows_tile(R, C, budget=2 * MIB):
    if R * C * 4 <= budget:
        return R
    for t in (2048, 1024, 704, 512, 352, 256, 128, 64, 32, 16, 8):
        if R % t == 0 and t * C * 4 <= budget:
            return t
    return R


def _chip_of_relation(j, pos):
    x, y = pos[0], pos[1]
    return 2 * jnp.where(j == 1, x, 1 - x) + jnp.where(j == 0, y, 1 - y)


def _pair_sum(grad, recv, pos, name):
    _, R, C = recv.shape
    tr = _rows_tile(R, C, 8 * MIB)

    def body(pos_ref, g_ref, r_ref, o_ref):
        del pos_ref
        o_ref[...] = (g_ref[...].astype(F32) + r_ref[...].astype(F32)).astype(BF16)

    if len(grad.shape) == 3:
        g_spec = pl.BlockSpec((None, tr, C), lambda j, i, p: (2 * _chip_of_relation(j, p) + p[2], i, 0))
    else:
        g_spec = pl.BlockSpec((tr, C), lambda j, i, p: (i, 2 * _chip_of_relation(j, p) + p[2]))
    return pl.pallas_call(
        body, name=name,
        grid_spec=pltpu.PrefetchScalarGridSpec(
            num_scalar_prefetch=1, grid=(3, R // tr),
            in_specs=[g_spec,
                      pl.BlockSpec((None, tr, C), lambda j, i, p: (_chip_of_relation(j, p), i, 0))],
            out_specs=pl.BlockSpec((None, tr, C), lambda j, i, p: (j, i, 0))),
        out_shape=jax.ShapeDtypeStruct((3, R, C), BF16),
        compiler_params=_params(("parallel", "parallel")),
    )(pos, grad, recv)


ADAM_LR = 0.001
ADAM_B1 = 0.9
ADAM_B2 = 0.999
ADAM_EPS = 1e-08
ADAM_WD = 0.01
ADAM_STEP = 10


def _adamw(w, g, m, v):
    m = ADAM_B1 * m + (1.0 - ADAM_B1) * g
    v = ADAM_B2 * v + (1.0 - ADAM_B2) * (g * g)
    m_hat = m / (1.0 - ADAM_B1 ** ADAM_STEP)
    v_hat = v / (1.0 - ADAM_B2 ** ADAM_STEP)
    delta = -ADAM_LR * (m_hat / (jnp.sqrt(v_hat) + ADAM_EPS) + ADAM_WD * w)
    return delta, m, v


def _adamw_big(grad, sib, recv, w, m, v, pos, layer, prev, name):
    L, R, C = w.shape
    tr = _rows_tile(R, C)
    has_prev = prev is not None

    def body(pos_ref, g_ref, s_ref, r0_ref, r1_ref, r2_ref, w_ref, m_ref, v_ref, *rest):
        del pos_ref
        go_ref, d_ref, mo_ref, vo_ref = rest[-4:]
        g = g_ref[...].astype(F32) + s_ref[...].astype(F32)
        g = ((g + r0_ref[...].astype(F32)) + r1_ref[...].astype(F32)) + r2_ref[...].astype(F32)
        delta, m2, v2 = _adamw(w_ref[...], g, m_ref[...], v_ref[...])
        go_ref[...] = g
        d_ref[...] = delta
        mo_ref[...] = m2
        vo_ref[...] = v2

    lay = pl.BlockSpec((None, tr, C), lambda i, p: (layer, i, 0))
    rspec = lambda j: pl.BlockSpec((None, tr, C), lambda i, p: (j, i, 0))
    mine = lambda p: 4 * p[0] + 2 * p[1] + p[2]
    if len(grad.shape) == 3:
        g_spec = pl.BlockSpec((None, tr, C), lambda i, p: (mine(p), i, 0))
    else:
        g_spec = pl.BlockSpec((tr, C), lambda i, p: (i, mine(p)))
    s_spec = pl.BlockSpec((None, tr, C), lambda i, p: (2 * p[0] + p[1], i, 0))
    in_specs = [g_spec, s_spec, rspec(0), rspec(1), rspec(2), lay, lay, lay]
    args = [pos, grad, sib, recv, recv, recv, w, m, v]
    aliases = {}
    if has_prev:
        in_specs += [_ANY] * 4
        args += list(prev)
        aliases = {9 + k: k for k in range(4)}
    return pl.pallas_call(
        body, name=name,
        grid_spec=pltpu.PrefetchScalarGridSpec(
            num_scalar_prefetch=1, grid=(R // tr,), in_specs=in_specs, out_specs=[lay] * 4),
        out_shape=[jax.ShapeDtypeStruct((L, R, C), F32)] * 4,
        input_output_aliases=aliases,
        compiler_params=_params(("parallel",)),
    )(*args)


def _sum_slots(gathered, name):
    _, R, C = gathered.shape
    tr = _rows_tile(R, C, 2 * MIB)

    def body(x_ref, o_ref):
        s = x_ref[0]
        for k in range(1, N_DEV):
            s = s + x_ref[k]
        o_ref[...] = s

    return pl.pallas_call(
        body, name=name, grid=(R // tr,),
        in_specs=[pl.BlockSpec((N_DEV, tr, C), lambda i: (0, i, 0))],
        out_specs=pl.BlockSpec((tr, C), lambda i: (i, 0)),
        out_shape=jax.ShapeDtypeStruct((R, C), F32),
        compiler_params=_params(("parallel",)),
    )(gathered)


def _adamw_small(w, g, m, v, name):
    R, C = w.shape
    tr = _rows_tile(R, C, 2 * MIB)

    def body(w_ref, g_ref, m_ref, v_ref, d_ref, mo_ref, vo_ref):
        d_ref[...], mo_ref[...], vo_ref[...] = _adamw(w_ref[...], g_ref[...], m_ref[...], v_ref[...])

    spec = pl.BlockSpec((tr, C), lambda i: (i, 0))
    return pl.pallas_call(
        body, name=name, grid=(R // tr,),
        in_specs=[spec] * 4, out_specs=[spec] * 3,
        out_shape=[jax.ShapeDtypeStruct((R, C), F32)] * 3,
        compiler_params=_params(("parallel",)),
    )(w, g, m, v)


def _pack(arrs):
    parts = []
    for a in arrs:
        flat = a.reshape(-1)
        pad = (-flat.shape[0]) % (8 * LANE)
        if pad:
            flat = jnp.pad(flat, (0, pad))
        parts.append(flat.reshape(-1, LANE))
    return jnp.concatenate(parts, axis=0)


def _unpack(packed, shapes):
    out, r = [], 0
    for s in shapes:
        n = 1
        for d in s:
            n *= d
        rows = -(-n // (8 * LANE)) * 8
        out.append(packed[r:r + rows].reshape(-1)[:n].reshape(s))
        r += rows
    return out


def kernel(x, norm1_g, w_in, w_pool, pool_scale, sgu_norm_g, sgu_w, sgu_b, conv_w, w_pool_out, w_sgu_out, w_conv_out, w_o, norm2_g, w_up, ffn_conv_w, w_down, final_g, loss_target, m_norm1_g, m_w_in, m_w_pool, m_pool_scale, m_sgu_norm_g, m_sgu_w, m_sgu_b, m_conv_w, m_w_pool_out, m_w_sgu_out, m_w_conv_out, m_w_o, m_norm2_g, m_w_up, m_ffn_conv_w, m_w_down, m_final_g, v_norm1_g, v_w_in, v_w_pool, v_pool_scale, v_sgu_norm_g, v_sgu_w, v_sgu_b, v_conv_w, v_w_pool_out, v_w_sgu_out, v_w_conv_out, v_w_o, v_norm2_g, v_w_up, v_ffn_conv_w, v_w_down, v_final_g):
    L = norm1_g.shape[0]
    x0 = x[0]
    tgt = loss_target[0]
    G, PG = w_pool.shape[1], w_pool.shape[3]
    F = w_down.shape[1] * N_DEV
    T, D = x0.shape
    tT = _pick(T, (1024, 512, 256, 128))
    tT2 = _pick(T, (2048, tT))
    tF = _pick(F, (512, 256))
    tF2 = _pick(2 * F, (1024, 512, 256))
    kF = _pick(F, (2816, 2048, 1024, 512))
    px, py, pc = _mesh_pos()
    me = 4 * px + 2 * py + pc
    pos = jnp.stack([px, py, pc]).astype(jnp.int32)

    slot = jnp.reshape(me, (1,)).astype(jnp.int32)
    ns_in = w_in.shape[2]
    nq = 3 if ns_in % (3 * 256) == 0 else 1
    chunk_cols = [(q * (ns_in // nq), ns_in // nq) for q in range(nq)]
    win0 = _cast_place(w_in, 0, slot, BF16, "place_w_in_0", by_cols=True)
    first, tok0 = _split_start([([win0],) + _copies_gather_chips(1, chunk_cols[0])], "ag_start_first")
    win0 = first[0][2][0]
    h1_first = _rms_fwd(x0, norm1_g[0][None], "rms1_fwd_0")
    stage1, group_sizes, started1 = {}, {0: 1}, {}

    def place_groups(l, deps):
        grp = [[("w_in", w_in, BF16, True)],
               [("w_pool_out", w_pool_out, BF16, True), ("w_sgu_out", w_sgu_out, BF16, True),
                ("w_conv_out", w_conv_out, BF16, True), ("w_pool", w_pool.reshape(L, -1, PG), BF16, False),
                ("conv_w", conv_w, F32, True), ("w_o", w_o, BF16, False)],
               [("w_up", w_up, BF16, True), ("ffn_conv_w", ffn_conv_w, F32, True)],
               [("w_down", w_down, BF16, False)]]
        ks = []
        for gi, members in enumerate(grp):
            if l == 0 and gi == 0:
                continue
            k = 4 * l + gi
            bufs = [_cast_place(a, l, slot, dt, f"place_{n}_{l}", deps, bc) for n, a, dt, bc in members]
            stage1[k] = (bufs,) + _copies_gather_chips(len(members))
            group_sizes[k] = len(members)
            ks.append(k)
        return ks

    def start_groups(ks, extra, name, deps=()):
        st, tok = _split_start(extra + [stage1[k] for k in ks], name, deps)
        for k, s in zip(ks, st[len(extra):]):
            started1[k] = s
        return st[:len(extra)], tok

    rest, tok_start0 = start_groups(place_groups(0, [tok0, h1_first]),
                                    [([win0],) + _copies_gather_chips(1, cols) for cols in chunk_cols[1:]], "ag_start_0")
    if nq > 1:
        win0 = rest[0][2][0]
    chunk_sems = [first[0][:2]] + [r[:2] for r in rest]
    later_groups = [place_groups(l, [tok_start0]) for l in range(1, L)]
    placed_later = [b for ks in later_groups for k in ks for b in stage1[k][0]]
    started2 = {}

    def gather_forward(k, after):
        n = group_sizes[k]
        bufs = _split_wait(started1[k], stage1[k][1], after, f"ag_wait1_{k}")
        st, tok = _split_start([(bufs,) + _copies_gather_forward(n)], f"ag_fwd_{k}")
        started2[k] = st[0]
        return tok

    def gather_done(k, after):
        return _split_wait(started2.pop(k), _copies_gather_forward(group_sizes[k])[0], after, f"ag_wait2_{k}")

    W, saved = [], []
    xc = x0
    for l in range(L):
        w = dict(
            sguw=sgu_w[l].astype(BF16),
            bexp=jnp.broadcast_to(sgu_b[l].T[:, :, None],
                                  (sgu_b.shape[2], sgu_b.shape[1], sgu_norm_g.shape[1] // sgu_b.shape[2])),
            g1=norm1_g[l][None], g2=norm2_g[l][None], gs=sgu_norm_g[l][None], scale=pool_scale[l][None])
        ka, kb, kc, kd = 4 * l, 4 * l + 1, 4 * l + 2, 4 * l + 3
        h1 = h1_first if l == 0 else _rms_fwd(xc, w["g1"], f"rms1_fwd_{l}")
        if l == 0:
            P, after = None, [h1] + placed_later
            for q, cols in enumerate(chunk_cols):
                (win0,) = _split_wait(chunk_sems[q] + ([win0],), _copies_gather_chips(1, cols)[0], after,
                                      f"ag_wait1_0_{q}")
                st, tok = _split_start([([win0],) + _copies_gather_forward(1, cols)], f"ag_fwd_0_{q}")
                (win0,) = _split_wait(st[0], _copies_gather_forward(1, cols)[0], after, f"ag_wait2_0_{q}")
                P = _mm(h1, win0, mode="nn", out_dtype=BF16, name=f"proj_in_0_{q}", tm=tT2, tn=cols[1], tk=D,
                        cols=(q, nq), prev=P, deps=[tok])
                after = P
            w["win"] = win0
        else:
            (w["win"],) = gather_done(ka, h1)
            tok = gather_forward(kb, h1)
            P = _mm(h1, w["win"], mode="nn", out_dtype=BF16, name=f"proj_in_{l}", tm=tT2, tn=1024, tk=D, deps=[tok])
        if l == 0:
            gather_forward(kb, P)
        w["wa"], w["wb"], w["wc"], wp, w["cw"], wo = gather_done(kb, P)
        w.update(wpool=wp.reshape(N_DEV, G, -1, PG).transpose(1, 0, 2, 3).reshape(G, PG, PG),
                 wo=wo.reshape(-1, wo.shape[2]))
        ya = _pool_fwd(P, w["wpool"], w["scale"], f"pool_fwd_{l}")
        yb = _sgu_fwd(P, w["sguw"], w["bexp"], w["gs"], f"sgu_fwd_{l}")
        yc = _conv_fwd(P, w["cw"], f"conv_fwd_{l}")
        oa, ob, oc, M = _merge_fwd(P, (ya, yb, yc), (w["wa"], w["wb"], w["wc"]), f"merge_fwd_{l}")
        tok = gather_forward(kc, M)
        if l == 0:
            for l2 in range(1, L):
                start_groups(later_groups[l2 - 1], [], f"ag_start_{l2}", [tok])
        x1, h2 = _mm_res_rms(M, w["wo"], xc, w["g2"], f"proj_o_{l}", deps=[tok])
        w["wup"], w["fw"] = gather_done(kc, h2)
        tok = gather_forward(kd, h2)
        U = _mm(h2, w["wup"], mode="nn", out_dtype=BF16, name=f"proj_up_{l}", tm=tT2, tn=tF2, tk=D, deps=[tok])
        act = _glu_fwd(U, w["fw"], f"glu_fwd_{l}")
        (wdown,) = gather_done(kd, act)
        w["wdown"] = wdown.reshape(-1, wdown.shape[2])
        deps = [gather_forward(4 * (l + 1), act)] if l + 1 < L else []
        x2 = _mm(act, w["wdown"], mode="nn", out_dtype=F32, name=f"proj_down_{l}", tm=tT, tn=512, tk=F, res=x1,
                 deps=deps)
        if deps:
            tok = deps[0]
        W.append(w)
        saved.append(dict(x0=xc, h1=h1, P=P, ya=ya, yb=yb, yc=yc, oa=oa, ob=ob, oc=oc, M=M, x1=x1, h2=h2, U=U, act=act))
        xc = x2

    dx, dxb, d_final_g, loss_part = _loss_head(xc, tgt, final_g[None], "loss_head")
    loss = lax.psum(loss_part[0, 0], ("x", "y", "c"))

    wmv = dict(w_in=(w_in, m_w_in, v_w_in), w_up=(w_up, m_w_up, v_w_up), w_o=(w_o, m_w_o, v_w_o),
               w_down=(w_down, m_w_down, v_w_down), w_pool_out=(w_pool_out, m_w_pool_out, v_w_pool_out),
               w_sgu_out=(w_sgu_out, m_w_sgu_out, v_w_sgu_out), w_conv_out=(w_conv_out, m_w_conv_out, v_w_conv_out),
               w_pool=(w_pool, m_w_pool, v_w_pool))
    adam_out = {}

    def rs_begin(tag, layer, names, grads):
        n = len(grads)
        slice_shape = lambda g: g.shape[1:] if len(g.shape) == 3 else (g.shape[0], g.shape[1] // N_DEV)
        lands = [lax.empty((4,) + slice_shape(g), BF16) for g in grads]
        st, tok = _split_start([(list(grads) + lands,) + _copies_rs_sibling(n)], f"rs_sib_{tag}")
        return dict(tag=tag, layer=layer, names=names, n=n, st=st[0]), tok

    def rs_to_chips(state, after):
        n, tag = state["n"], state["tag"]
        bufs = _split_wait(state["st"], _copies_rs_sibling(n)[0], after, f"rs_sibw_{tag}")
        grads, sib = bufs[:n], bufs[n:]
        pairs = [_pair_sum(g, r, pos, f"pair_{tag}_{i}") for i, (g, r) in enumerate(zip(grads, sib))]
        lands = [lax.empty(p.shape, BF16) for p in pairs]
        st, tok = _split_start([(pairs + lands,) + _copies_rs_chips(n)], f"rs_chips_{tag}")
        state.update(st=st[0], grads=grads, sib=sib)
        return tok

    def rs_finish(state, after):
        n, tag, layer = state["n"], state["tag"], state["layer"]
        bufs = _split_wait(state["st"], _copies_rs_chips(n)[0], after, f"rs_chipsw_{tag}")
        for name, g, sb, recv in zip(state["names"], state["grads"], state["sib"], bufs[n:]):
            wk, mk, vk = (a.reshape(L, -1, a.shape[-1]) for a in wmv[name])
            adam_out[name] = _adamw_big(g, sb, recv, wk, mk, vk, pos, layer, adam_out.get(name),
                                        f"adamw_{name}_{layer}")

    small = [None] * L
    pending = []
    deps = []
    for l in reversed(range(L)):
        w, s = W[l], saved[l]
        dact = _mm(dxb, w["wdown"], mode="nt", out_dtype=BF16, name=f"d_act_{l}", tm=tT2, tn=tF, tk=D, deps=deps)
        g_down = _mm(s["act"], dxb, mode="tn", out_dtype=BF16, name=f"g_down_{l}", tm=tF, tn=1024, tk=T)
        dug, duv, dfwg, dfwv = _glu_bwd(s["U"], dact, w["fw"], f"glu_bwd_{l}")
        dh2 = _mm(dug, w["wup"], mode="nt", out_dtype=BF16, name=f"d_h2_{l}", tm=tT, tn=1024, tk=kF, a2=duv)
        g_up = _mm(s["h2"], dug, mode="tn", out_dtype=BF16, name=f"g_upa_{l}", tm=1024, tn=tF, tk=T, out_n=2 * F)
        g_up = _mm(s["h2"], duv, mode="tn", out_dtype=BF16, name=f"g_upb_{l}", tm=1024, tn=tF, tk=T, out_n=2 * F,
                   out_off=F // tF, prev=g_up)
        ra, tok = rs_begin(f"a{l}", l, ["w_down", "w_up"], [g_down.reshape(N_DEV, -1, g_down.shape[1]), g_up])
        if pending:
            rs_finish(pending.pop(0), g_up)
        dx1, dx1b, dg2 = _rms_bwd(dh2, s["x1"], w["g2"], dx, f"rms2_bwd_{l}")
        dM = _mm(dx1b, w["wo"], mode="nt", out_dtype=BF16, name=f"d_m_{l}", tm=tT, tn=1024, tk=D, deps=[tok])
        g_o = _mm(s["M"], dx1b, mode="tn", out_dtype=BF16, name=f"g_o_{l}", tm=1024, tn=1024, tk=T)
        doa, dob, doc, dPg = _merge_bwd(s["P"], dM, (s["oa"], s["ob"], s["oc"]), f"merge_bwd_{l}")
        tok = rs_to_chips(ra, dPg)
        if pending:
            rs_finish(pending.pop(0), dPg)
        dya = _mm(doa, w["wa"], mode="nt", out_dtype=BF16, name=f"d_ya_{l}", tm=tT, tn=1024, tk=D, deps=[tok])
        dyb = _mm(dob, w["wb"], mode="nt", out_dtype=BF16, name=f"d_yb_{l}", tm=tT, tn=1024, tk=D)
        dyc = _mm(doc, w["wc"], mode="nt", out_dtype=BF16, name=f"d_yc_{l}", tm=tT, tn=1024, tk=D)
        g_a = _mm(s["ya"], doa, mode="tn", out_dtype=BF16, name=f"g_a_{l}", tm=1024, tn=1024, tk=T)
        g_b = _mm(s["yb"], dob, mode="tn", out_dtype=BF16, name=f"g_b_{l}", tm=1024, tn=1024, tk=T)
        g_c = _mm(s["yc"], doc, mode="tn", out_dtype=BF16, name=f"g_c_{l}", tm=1024, tn=1024, tk=T)
        da, g_pool, dscale = _pool_bwd(s["P"], dya, w["wpool"], w["scale"], f"pool_bwd_{l}")
        g_pool_s = g_pool.reshape(G, N_DEV, -1, PG).transpose(1, 0, 2, 3).reshape(N_DEV, -1, PG).astype(BF16)
        rb, tok = rs_begin(f"b{l}", l, ["w_o", "w_pool_out", "w_sgu_out", "w_conv_out", "w_pool"],
                           [g_o.reshape(N_DEV, -1, g_o.shape[1]), g_a, g_b, g_c, g_pool_s])
        if pending:
            rs_finish(pending.pop(0), da)
        dPl, g_sguw, db_exp, dgs = _sgu_bwd(s["P"], dyb, da, w["sguw"], w["bexp"], w["gs"], f"sgu_bwd_{l}")
        dPl, dcw = _conv_bwd(s["P"], dyc, w["cw"], dPl, f"conv_bwd_{l}")
        tok2 = rs_to_chips(rb, dPl)
        half = dPl.shape[1]
        g_in = _mm(s["h1"], dPl, mode="tn", out_dtype=BF16, name=f"g_ina_{l}", tm=1024, tn=1024, tk=T, out_n=2 * half,
                   deps=[tok, tok2])
        g_in = _mm(s["h1"], dPg, mode="tn", out_dtype=BF16, name=f"g_inb_{l}", tm=1024, tn=1024, tk=T, out_n=2 * half,
                   out_off=half // 1024, prev=g_in)
        rc, tok = rs_begin(f"c{l}", l, ["w_in"], [g_in])
        if l > 0:
            dh1 = _mm(dPl, w["win"], mode="nt", out_dtype=BF16, name=f"d_h1_{l}", tm=tT, tn=1024, tk=D, a2=dPg,
                      deps=[tok])
            tok = rs_to_chips(rc, dh1)
        else:
            dh1 = _mm(dPl, w["win"], mode="nt", out_dtype=F32, name=f"d_h1a_{l}", tm=tT, tn=1024, tk=D, deps=[tok])
            tok = rs_to_chips(rc, dh1)
            dh1 = _mm(dPg, w["win"], mode="nt", out_dtype=BF16, name=f"d_h1b_{l}", tm=tT, tn=1024, tk=D,
                      b_koff=half // D, res=dh1, deps=[tok])
        dx, dxb, dg1 = _rms_bwd(dh1, s["x0"], w["g1"], dx1, f"rms1_bwd_{l}", deps=[tok])
        deps = []
        pending = [ra, rb, rc]
        small[l] = dict(norm1_g=dg1[0], pool_scale=dscale[0], sgu_norm_g=dgs[0], sgu_w=g_sguw,
                        sgu_b=db_exp[:, :, 0].T, conv_w=dcw[:3], norm2_g=dg2[0],
                        ffn_conv_w=jnp.concatenate([dfwg[:3], dfwv[:3]], axis=1))
    grad_x = dx[None]

    snames = ["norm1_g", "pool_scale", "sgu_norm_g", "sgu_w", "sgu_b", "conv_w", "norm2_g", "ffn_conv_w"]
    sparts = [jnp.stack([small[l][n] for l in range(L)]) for n in snames] + [d_final_g[0]]
    snames = snames + ["final_g"]
    packed = _pack(sparts)
    sbuf = _cast_place(packed[None], 0, slot, F32, "place_small")
    st, tok = _split_start([([sbuf],) + _copies_gather_chips(1)], "ag_small_start")
    rs_finish(pending[0], tok)
    rs_finish(pending[1], tok)
    bufs = _split_wait(st[0], _copies_gather_chips(1)[0], [o[0] for o in adam_out.values()], "ag_small_wait1")
    st, tok = _split_start([(bufs,) + _copies_gather_forward(1)], "ag_small_fwd")
    (gathered_small,) = _split_wait(st[0], _copies_gather_forward(1)[0], tok, "ag_small_wait2")
    total = _sum_slots(gathered_small, "sum_small")
    sgrads = dict(zip(snames, _unpack(total, [p.shape for p in sparts])))
    for n, width in (("conv_w", conv_w.shape[2]), ("ffn_conv_w", ffn_conv_w.shape[2])):
        sgrads[n] = lax.dynamic_slice_in_dim(sgrads[n], me * width, width, axis=2)
    sw = dict(norm1_g=(norm1_g, m_norm1_g, v_norm1_g), pool_scale=(pool_scale, m_pool_scale, v_pool_scale),
              sgu_norm_g=(sgu_norm_g, m_sgu_norm_g, v_sgu_norm_g), sgu_w=(sgu_w, m_sgu_w, v_sgu_w),
              sgu_b=(sgu_b, m_sgu_b, v_sgu_b), conv_w=(conv_w, m_conv_w, v_conv_w),
              norm2_g=(norm2_g, m_norm2_g, v_norm2_g), ffn_conv_w=(ffn_conv_w, m_ffn_conv_w, v_ffn_conv_w),
              final_g=(final_g, m_final_g, v_final_g))
    shapes = [sw[n][0].shape for n in snames]
    upd = _adamw_small(_pack([sw[n][0] for n in snames]), _pack([sgrads[n] for n in snames]),
                       _pack([sw[n][1] for n in snames]), _pack([sw[n][2] for n in snames]), "adamw_small")
    sdelta, sm, sv = (dict(zip(snames, _unpack(u, shapes))) for u in upd)
    res = {n: [sgrads[n], sdelta[n], sm[n], sv[n]] for n in snames}

    rs_finish(pending[2], [upd[0]] + [o[0] for o in adam_out.values()])
    for n, outs in adam_out.items():
        res[n] = [o.reshape(wmv[n][0].shape) for o in outs]

    order = ["norm1_g", "w_in", "w_pool", "pool_scale", "sgu_norm_g", "sgu_w", "sgu_b", "conv_w", "w_pool_out",
             "w_sgu_out", "w_conv_out", "w_o", "norm2_g", "w_up", "ffn_conv_w", "w_down", "final_g"]
    return (loss, grad_x) + tuple(res[n][k] for k in range(4) for n in order)
```
